```python
import math
import jax, jax.numpy as jnp
from jax import lax
import numpy as np

D_MODEL = 1024
BATCH = 8
SEQ = 4096
DEPTH = 2

HEAD_DIM = 64
A_HEADS = D_MODEL // (2 * HEAD_DIM)
DILATED_BRANCHES = ((128, 1), (512, 4), (2048, 16))
DIFF_HALF = 64
DIFF_VDIM = 2 * DIFF_HALF
B_HEADS = D_MODEL // (2 * DIFF_VDIM)
C_HEADS = D_MODEL // HEAD_DIM
Q_BLOCK = 128
N_BUCKETS = 32
MAX_DISTANCE = 2048
BIAS_HEADS = A_HEADS + B_HEADS
PEER_HEADS = 8
N_KEYS = 128
N_EXPERTS = N_KEYS * N_KEYS
PEER_TOPK = 16
PEER_KEY_HALF = 128
PEER_CHUNK = 128
RMS_EPS = 1e-6
NEG_INF = -1e30
FORGET_BIAS_INIT = 3.0
A_WIDTH = A_HEADS * HEAD_DIM
B_QK_WIDTH = B_HEADS * 2 * DIFF_HALF
B_V_WIDTH = B_HEADS * DIFF_VDIM
EVEN_IN = 3 * A_WIDTH + 2 * B_QK_WIDTH + B_V_WIDTH
EVEN_MIX = A_WIDTH + B_V_WIDTH
C_WIDTH = C_HEADS * HEAD_DIM
ODD_IN = 3 * C_WIDTH + C_HEADS

kernel_name = 'hybrid_dilated_diff_fox_peer'


def rms_norm(x, g):
    x32 = x.astype(jnp.float32)
    y = x32 * lax.rsqrt(jnp.mean(x32 * x32, axis=-1, keepdims=True) + RMS_EPS)
    return (y * g.astype(jnp.float32)).astype(x.dtype)


def t5_bucket(dist):
    max_exact = N_BUCKETS // 2
    d = jnp.maximum(dist, 1).astype(jnp.float32)
    large = max_exact + (jnp.log(d / max_exact) / math.log(MAX_DISTANCE / max_exact)
                         * (N_BUCKETS - max_exact)).astype(jnp.int32)
    large = jnp.minimum(large, N_BUCKETS - 1)
    return jnp.where(dist < max_exact, dist, large)


def split_heads(t, n_heads):
    b, s, _ = t.shape
    return t.reshape(b, s, n_heads, -1).transpose(0, 2, 1, 3)


def sweep_query_blocks(fn, q):
    b, h, s, dq = q.shape
    nb = s // Q_BLOCK
    qb = q.reshape(b, h, nb, Q_BLOCK, dq).transpose(2, 0, 1, 3, 4)
    out = lax.map(lambda a: fn(a[0], a[1]), (jnp.arange(nb, dtype=jnp.int32), qb))
    out = out.transpose(1, 0, 3, 2, 4)
    return out.reshape(b, s, h * out.shape[-1])


def dilated_attention(q, k, v, bias_table):
    scale = HEAD_DIM ** -0.5
    branch_offs = [d * jnp.arange(w // d + 1, dtype=jnp.int32) for (w, d) in DILATED_BRANCHES]
    branch_bias = [bias_table[t5_bucket(o)].T for o in branch_offs]

    def block(bi, qblk):
        pos = bi * Q_BLOCK + jnp.arange(Q_BLOCK, dtype=jnp.int32)
        outs, lses = [], []
        for offs, bb in zip(branch_offs, branch_bias):
            idx = pos[:, None] - offs[None, :]
            valid = idx >= 0
            idx = jnp.maximum(idx, 0)
            kg = k[:, :, idx]
            vg = v[:, :, idx]
            logits = jnp.einsum('bhqd,bhqnd->bhqn', qblk, kg).astype(jnp.float32) * scale
            logits = jnp.where(valid, logits + bb[None, :, None, :].astype(jnp.float32), NEG_INF)
            lse = jax.nn.logsumexp(logits, axis=-1, keepdims=True)
            p = jnp.exp(logits - lse)
            outs.append(jnp.einsum('bhqn,bhqnd->bhqd', p.astype(v.dtype), vg).astype(jnp.float32))
            lses.append(lse)
        wts = jax.nn.softmax(jnp.stack(lses), axis=0)
        return jnp.sum(wts * jnp.stack(outs), axis=0).astype(v.dtype)

    return sweep_query_blocks(block, q)


def diff_attention(q, k, v, bias_table, lam, lam_init, ln_g):
    scale = DIFF_HALF ** -0.5
    s = q.shape[2]
    kpos = jnp.arange(s, dtype=jnp.int32)
    k1, k2 = k[..., :DIFF_HALF], k[..., DIFF_HALF:]

    def block(bi, qblk):
        qpos = bi * Q_BLOCK + jnp.arange(Q_BLOCK, dtype=jnp.int32)
        dist = qpos[:, None] - kpos[None, :]
        causal = dist >= 0
        bias = bias_table[t5_bucket(jnp.maximum(dist, 0))].transpose(2, 0, 1).astype(jnp.float32)

        def attn_map(qh, kh):
            l = jnp.einsum('bhqd,bhkd->bhqk', qh, kh).astype(jnp.float32) * scale + bias[None]
            return jax.nn.softmax(jnp.where(causal, l, NEG_INF), axis=-1)

        a = attn_map(qblk[..., :DIFF_HALF], k1) - lam * attn_map(qblk[..., DIFF_HALF:], k2)
        o = jnp.einsum('bhqk,bhkd->bhqd', a.astype(v.dtype), v)
        return (rms_norm(o, ln_g) * (1.0 - lam_init)).astype(v.dtype)

    return sweep_query_blocks(block, q)


def forgetting_attention(q, k, v, fg_logit):
    scale = HEAD_DIM ** -0.5
    s = q.shape[2]
    kpos = jnp.arange(s, dtype=jnp.int32)
    c = jnp.cumsum(jax.nn.log_sigmoid(fg_logit.astype(jnp.float32)), axis=-1)

    def block(bi, qblk):
        qpos = bi * Q_BLOCK + jnp.arange(Q_BLOCK, dtype=jnp.int32)
        causal = kpos[None, :] <= qpos[:, None]
        cq = lax.dynamic_slice_in_dim(c, bi * Q_BLOCK, Q_BLOCK, axis=2)
        l = jnp.einsum('bhqd,bhkd->bhqk', qblk, k).astype(jnp.float32) * scale
        l = l + cq[..., :, None] - c[..., None, :]
        p = jax.nn.softmax(jnp.where(causal, l, NEG_INF), axis=-1)
        return jnp.einsum('bhqk,bhkd->bhqd', p.astype(v.dtype), v)

    return sweep_query_blocks(block, q)


def even_mixer(h, w_in, w_out, rel_bias, lam_params, ln_g, layer):
    p = h @ w_in
    qa = split_heads(p[..., :A_WIDTH], A_HEADS)
    ka = split_heads(p[..., A_WIDTH:2 * A_WIDTH], A_HEADS)
    va = split_heads(p[..., 2 * A_WIDTH:3 * A_WIDTH], A_HEADS)
    off = 3 * A_WIDTH
    qb = split_heads(p[..., off:off + B_QK_WIDTH], B_HEADS)
    off = off + B_QK_WIDTH
    kb = split_heads(p[..., off:off + B_QK_WIDTH], B_HEADS)
    off = off + B_QK_WIDTH
    vb = split_heads(p[..., off:], B_HEADS)
    lam_init = 0.8 - 0.6 * math.exp(-0.3 * layer)
    lp = lam_params.astype(jnp.float32)
    lam = jnp.exp(jnp.sum(lp[0] * lp[1])) - jnp.exp(jnp.sum(lp[2] * lp[3])) + lam_init
    oa = dilated_attention(qa, ka, va, rel_bias[:, :A_HEADS])
    ob = diff_attention(qb, kb, vb, rel_bias[:, A_HEADS:], lam, lam_init, ln_g)
    return jnp.concatenate([oa, ob], axis=-1) @ w_out


def odd_mixer(h, w_in, b_f, w_out):
    p = h @ w_in
    q = split_heads(p[..., :C_WIDTH], C_HEADS)
    k = split_heads(p[..., C_WIDTH:2 * C_WIDTH], C_HEADS)
    v = split_heads(p[..., 2 * C_WIDTH:3 * C_WIDTH], C_HEADS)
    fg = (p[..., 3 * C_WIDTH:] + b_f).transpose(0, 2, 1)
    return forgetting_attention(q, k, v, fg) @ w_out


def peer_ffn(h, wq, subkeys, u, v):
    b, s, d = h.shape
    t = b * s
    ht = h.reshape(t, d)
    q = (ht @ wq).reshape(t, PEER_HEADS, 2, PEER_KEY_HALF)
    sc = jnp.einsum('thcd,hcnd->thcn', q, subkeys).astype(jnp.float32)
    top_s, top_i = lax.top_k(sc, PEER_TOPK)
    cand_s = (top_s[:, :, 0, :, None] + top_s[:, :, 1, None, :]).reshape(t, PEER_HEADS, -1)
    cand_i = (top_i[:, :, 0, :, None] * N_KEYS + top_i[:, :, 1, None, :]).reshape(t, PEER_HEADS, -1)
    best_s, best_j = lax.top_k(cand_s, PEER_TOPK)
    idx = jnp.take_along_axis(cand_i, best_j, axis=-1)
    gate = jax.nn.softmax(best_s, axis=-1)
    nc = t // PEER_CHUNK
    xs = (ht.reshape(nc, PEER_CHUNK, d),
          idx.reshape(nc, PEER_CHUNK, PEER_HEADS * PEER_TOPK),
          gate.reshape(nc, PEER_CHUNK, PEER_HEADS * PEER_TOPK).astype(h.dtype))

    def chunk(a):
        xc, ic, gc = a
        act = jax.nn.gelu(jnp.einsum('cd,ced->ce', xc, u[ic]), approximate=False)
        return jnp.einsum('ce,ced->cd', gc * act, v[ic])

    return lax.map(chunk, xs).reshape(b, s, d)


def setup_inputs(seed: int = 0) -> dict:
    key = jax.random.key(seed)
    ks = jax.random.split(key, 16)
    n_even = (DEPTH + 1) // 2
    n_odd = DEPTH // 2
    nrm = jax.random.normal
    f32 = jnp.float32
    return {
        'x': nrm(ks[0], (BATCH, SEQ, D_MODEL), f32),
        'norm_mix_g': 1.0 + 0.02 * nrm(ks[1], (DEPTH, D_MODEL), f32),
        'norm_ffn_g': 1.0 + 0.02 * nrm(ks[2], (DEPTH, D_MODEL), f32),
        'final_norm_g': 1.0 + 0.02 * nrm(ks[3], (D_MODEL,), f32),
        'rel_bias': 0.5 * nrm(ks[4], (N_BUCKETS, BIAS_HEADS), f32),
        'even_w_in': nrm(ks[5], (n_even, D_MODEL, EVEN_IN), f32) * D_MODEL ** -0.5,
        'even_w_out': nrm(ks[6], (n_even, EVEN_MIX, D_MODEL), f32) * EVEN_MIX ** -0.5,
        'diff_lambda': 0.1 * nrm(ks[7], (n_even, 4, DIFF_HALF), f32),
        'diff_ln_g': 1.0 + 0.02 * nrm(ks[8], (n_even, DIFF_VDIM), f32),
        'odd_w_in': nrm(ks[9], (n_odd, D_MODEL, ODD_IN), f32) * D_MODEL ** -0.5,
        'odd_b_f': FORGET_BIAS_INIT + 0.5 * nrm(ks[10], (n_odd, C_HEADS), f32),
        'odd_w_out': nrm(ks[11], (n_odd, C_WIDTH, D_MODEL), f32) * C_WIDTH ** -0.5,
        'peer_wq': nrm(ks[12], (DEPTH, D_MODEL, PEER_HEADS * 2 * PEER_KEY_HALF), f32) * D_MODEL ** -0.5,
        'peer_subkeys': nrm(ks[13], (DEPTH, PEER_HEADS, 2, N_KEYS, PEER_KEY_HALF), f32) * PEER_KEY_HALF ** -0.5,
        'peer_u': nrm(ks[14], (DEPTH, N_EXPERTS, D_MODEL), f32) * D_MODEL ** -0.5,
        'peer_v': nrm(ks[15], (DEPTH, N_EXPERTS, D_MODEL), f32) * PEER_HEADS ** -0.5,
    }


def reference(x, norm_mix_g, norm_ffn_g, final_norm_g, rel_bias, even_w_in, even_w_out,
              diff_lambda, diff_ln_g, odd_w_in, odd_b_f, odd_w_out, peer_wq, peer_subkeys,
              peer_u, peer_v):
    for layer in range(DEPTH):
        h = rms_norm(x, norm_mix_g[layer])
        i = layer // 2
        if layer % 2 == 0:
            x = x + even_mixer(h, even_w_in[i], even_w_out[i], rel_bias,
                               diff_lambda[i], diff_ln_g[i], layer)
        else:
            x = x + odd_mixer(h, odd_w_in[i], odd_b_f[i], odd_w_out[i])
        h = rms_norm(x, norm_ffn_g[layer])
        x = x + peer_ffn(h, peer_wq[layer], peer_subkeys[layer], peer_u[layer], peer_v[layer])
    return rms_norm(x, final_norm_g)
```

```python
import functools
import math

import numpy as np
import jax
import jax.numpy as jnp
from jax import lax
from jax.experimental import pallas as pl
from jax.experimental.pallas import tpu as pltpu
from jax.experimental.pallas import tpu_sc as plsc

F32 = jnp.float32
BF16 = jnp.bfloat16
I32 = jnp.int32

D_MODEL = 1024
HEAD_DIM = 64
A_HEADS = 8
DILATED_BRANCHES = ((128, 1), (512, 4), (2048, 16))
DIFF_HALF = 64
DIFF_VDIM = 128
B_HEADS = 4
C_HEADS = 16
N_BUCKETS = 32
MAX_DISTANCE = 2048
PEER_HEADS = 8
N_KEYS = 128
PEER_TOPK = 16
PEER_KEY_HALF = 128
RMS_EPS = 1e-6
NEG_INF = -1e30
A_WIDTH = A_HEADS * HEAD_DIM
B_QK_WIDTH = B_HEADS * 2 * DIFF_HALF
B_V_WIDTH = B_HEADS * DIFF_VDIM
C_WIDTH = C_HEADS * HEAD_DIM
QK_SCALE = 0.125
LANES = 128
DIL_BLOCK = 128
ATT_BLOCK = 256

SC_CORES = 2
SC_SUBCORES = 16
SC_LANES = 16
SC_WORKERS = SC_CORES * SC_SUBCORES


def _t5_bucket_table(n):
    max_exact = N_BUCKETS // 2
    d = np.arange(n)
    df = np.maximum(d, 1).astype(np.float32)
    large = max_exact + (
        np.log(df / np.float32(max_exact)) / np.float32(math.log(MAX_DISTANCE / max_exact))
        * np.float32(N_BUCKETS - max_exact)).astype(np.int32)
    large = np.minimum(large, N_BUCKETS - 1)
    return np.where(d < max_exact, d, large).astype(np.int32)


def _norm_matmul_body(*refs, has_res, want_h):
    it = iter(refs)
    x_ref = next(it)
    r_ref = next(it) if has_res else None
    g_ref = next(it)
    w_ref = next(it)
    o_ref = next(it)
    xs_ref = next(it) if has_res else None
    hout_ref = next(it) if want_h else None
    h_scr = next(it)

    @pl.when(pl.program_id(1) == 0)
    def _():
        x = x_ref[...]
        if has_res:
            x = x + r_ref[...]
            xs_ref[...] = x
        ms = jnp.mean(x * x, axis=-1, keepdims=True)
        h = x * lax.rsqrt(ms + RMS_EPS) * g_ref[...]
        if want_h:
            hout_ref[...] = h
        h_scr[...] = h.astype(BF16)

    o_ref[...] = jnp.dot(h_scr[...], w_ref[...],
                         preferred_element_type=F32).astype(o_ref.dtype)


def norm_matmul(x, g, w, *, res=None, want_h=False, out_dtype=BF16, tm=512, tn=512):
    m, d = x.shape
    n = w.shape[1]
    tn = min(tn, n)
    row = pl.BlockSpec((tm, d), lambda i, j: (i, 0))
    in_specs = [row] + ([row] if res is not None else []) + [
        pl.BlockSpec((1, d), lambda i, j: (0, 0)),
        pl.BlockSpec((d, tn), lambda i, j: (0, j))]
    out_specs = [pl.BlockSpec((tm, tn), lambda i, j: (i, j))]
    out_shape = [jax.ShapeDtypeStruct((m, n), out_dtype)]
    if res is not None:
        out_specs.append(row)
        out_shape.append(jax.ShapeDtypeStruct((m, d), F32))
    if want_h:
        out_specs.append(row)
        out_shape.append(jax.ShapeDtypeStruct((m, d), F32))
    args = [x] + ([res] if res is not None else []) + [g.reshape(1, d), w]
    return pl.pallas_call(
        functools.partial(_norm_matmul_body, has_res=res is not None, want_h=want_h),
        grid=(m // tm, n // tn),
        in_specs=in_specs,
        out_specs=out_specs,
        out_shape=out_shape,
        scratch_shapes=[pltpu.VMEM((tm, d), BF16)],
        compiler_params=pltpu.CompilerParams(
            dimension_semantics=("parallel", "arbitrary")),
        name="norm_matmul",
    )(*args)


def _dilated_body(q_ref, kp_ref, kc_ref, vp_ref, vc_ref, b_ref, o_ref, lse_ref):
    i = pl.program_id(3)
    q = q_ref[...] * QK_SCALE
    k = jnp.concatenate([kp_ref[...], kc_ref[...]], axis=0)
    v = jnp.concatenate([vp_ref[...], vc_ref[...]], axis=0)
    col = lax.broadcasted_iota(I32, (DIL_BLOCK, 2 * DIL_BLOCK), 1)
    has_prev = jnp.logical_or(col >= DIL_BLOCK, i > 0)
    outs, lses = [], []
    for hh in range(2):
        sl = slice(hh * HEAD_DIM, (hh + 1) * HEAD_DIM)
        s = lax.dot_general(q[:, sl], k[:, sl], (((1,), (1,)), ((), ())),
                            preferred_element_type=F32)
        s = jnp.where(has_prev, s + b_ref[hh], NEG_INF)
        m = jnp.max(s, axis=-1, keepdims=True)
        p = jnp.exp(s - m)
        l = jnp.sum(p, axis=-1, keepdims=True)
        o = jnp.dot(p.astype(BF16), v[:, sl], preferred_element_type=F32) / l
        outs.append(o)
        lses.append(jnp.broadcast_to(m + jnp.log(l), (DIL_BLOCK, HEAD_DIM)))
    o_ref[...] = jnp.concatenate(outs, axis=1)
    lse_ref[...] = jnp.concatenate(lses, axis=1)


def dilated_branch(p, bias_tile, dil, batch, seq):
    n_cols = p.shape[1]
    cb = n_cols // LANES
    rows = seq // dil
    nblk = rows // DIL_BLOCK
    pv = p.reshape(batch, rows, dil * n_cols)
    kq, kk, kv = 0, A_WIDTH // LANES, 2 * A_WIDTH // LANES
    blk = (None, DIL_BLOCK, LANES)

    def spec(col0, prev):
        if prev:
            return pl.BlockSpec(blk, lambda b, h, r, i: (b, jnp.maximum(i - 1, 0), r * cb + col0 + h))
        return pl.BlockSpec(blk, lambda b, h, r, i: (b, i, r * cb + col0 + h))

    ocb = A_WIDTH // LANES
    ospec = pl.BlockSpec(blk, lambda b, h, r, i: (b, i, r * ocb + h))
    oshape = jax.ShapeDtypeStruct((batch, rows, dil * A_WIDTH), F32)
    o, lse = pl.pallas_call(
        _dilated_body,
        grid=(batch, A_HEADS // 2, dil, nblk),
        in_specs=[spec(kq, False), spec(kk, True), spec(kk, False),
                  spec(kv, True), spec(kv, False),
                  pl.BlockSpec((2, DIL_BLOCK, 2 * DIL_BLOCK), lambda b, h, r, i: (h, 0, 0))],
        out_specs=[ospec, ospec],
        out_shape=[oshape, oshape],
        compiler_params=pltpu.CompilerParams(
            dimension_semantics=("parallel", "parallel", "parallel", "arbitrary")),
        name=f"dilated_d{dil}",
    )(pv, pv, pv, pv, pv, bias_tile)
    return o.reshape(batch * seq, A_WIDTH), lse.reshape(batch * seq, A_WIDTH)


def dilated_bias_tile(rel_bias_a, window, dil):
    n = window // dil
    assert n == DIL_BLOCK
    bucket = _t5_bucket_table(window + 1)
    row = np.arange(DIL_BLOCK)[:, None]
    c = np.arange(2 * DIL_BLOCK)[None, :]
    j = DIL_BLOCK + row - c
    valid = (j >= 0) & (j <= n)
    bk = bucket[np.clip(j, 0, n) * dil]
    tile = rel_bias_a.T[:, bk]
    return jnp.where(jnp.asarray(valid)[None], tile, NEG_INF).astype(F32)


def _diff_body(q_ref, k_ref, v_ref, b_ref, lam_ref, g_ref, o_ref,
               m_scr, l_scr, acc_scr, *, n_tiles, lam_init):
    t = ATT_BLOCK
    qi = pl.program_id(2)
    q = q_ref[...] * QK_SCALE
    for a in range(2):
        m_scr[a] = jnp.full((t, 1), NEG_INF, F32)
        l_scr[a] = jnp.zeros((t, 1), F32)
        acc_scr[a] = jnp.zeros((t, DIFF_VDIM), F32)

    def step(j, masked):
        off = pl.multiple_of(j * t, t)
        ks = k_ref[pl.ds(off, t), :]
        vs = v_ref[pl.ds(off, t), :]
        bias = b_ref[jnp.minimum(qi - j, n_tiles - 1)]
        if masked:
            row = lax.broadcasted_iota(I32, (t, t), 0)
            col = lax.broadcasted_iota(I32, (t, t), 1)
            causal = row >= col
        for a in range(2):
            sl = slice(a * DIFF_HALF, (a + 1) * DIFF_HALF)
            s = lax.dot_general(q[:, sl], ks[:, sl], (((1,), (1,)), ((), ())),
                                preferred_element_type=F32) + bias
            if masked:
                s = jnp.where(causal, s, NEG_INF)
            m_prev = m_scr[a]
            m_new = jnp.maximum(m_prev, jnp.max(s, axis=-1, keepdims=True))
            alpha = jnp.exp(m_prev - m_new)
            p = jnp.exp(s - m_new)
            l_scr[a] = alpha * l_scr[a] + jnp.sum(p, axis=-1, keepdims=True)
            acc_scr[a] = alpha * acc_scr[a] + jnp.dot(
                p.astype(BF16), vs, preferred_element_type=F32)
            m_scr[a] = m_new

    def loop_body(j, carry):
        step(j, False)
        return carry

    lax.fori_loop(0, qi, loop_body, 0)
    step(qi, True)

    lp = lam_ref[...]
    lam = (jnp.exp(jnp.sum(lp[0:1] * lp[1:2])) - jnp.exp(jnp.sum(lp[2:3] * lp[3:4]))
           + lam_init)
    o = acc_scr[0] / l_scr[0] - lam * (acc_scr[1] / l_scr[1])
    ms = jnp.mean(o * o, axis=-1, keepdims=True)
    y = o * lax.rsqrt(ms + RMS_EPS) * g_ref[...]
    o_ref[...] = (y * (1.0 - lam_init)).astype(o_ref.dtype)


def diff_attention(p, bias_tiles, lam_params, ln_g, lam_init, batch, seq):
    t = ATT_BLOCK
    n_tiles = bias_tiles.shape[1]
    cq = 3 * A_WIDTH // LANES
    ck = cq + B_QK_WIDTH // LANES
    cv = ck + B_QK_WIDTH // LANES
    return pl.pallas_call(
        functools.partial(_diff_body, n_tiles=n_tiles, lam_init=lam_init),
        grid=(batch, B_HEADS, seq // t),
        in_specs=[
            pl.BlockSpec((None, t, LANES), lambda b, h, i: (b, i, cq + h)),
            pl.BlockSpec((None, seq, LANES), lambda b, h, i: (b, 0, ck + h)),
            pl.BlockSpec((None, seq, LANES), lambda b, h, i: (b, 0, cv + h)),
            pl.BlockSpec((None, n_tiles, t, t), lambda b, h, i: (h, 0, 0, 0)),
            pl.BlockSpec((4, DIFF_HALF), lambda b, h, i: (0, 0)),
            pl.BlockSpec((1, DIFF_VDIM), lambda b, h, i: (0, 0)),
        ],
        out_specs=pl.BlockSpec((None, t, LANES), lambda b, h, i: (b, i, h)),
        out_shape=jax.ShapeDtypeStruct((batch, seq, B_V_WIDTH), BF16),
        scratch_shapes=[pltpu.VMEM((2, t, 1), F32), pltpu.VMEM((2, t, 1), F32),
                        pltpu.VMEM((2, t, DIFF_VDIM), F32)],
        compiler_params=pltpu.CompilerParams(
            dimension_semantics=("parallel", "parallel", "arbitrary")),
        name="diff_attention",
    )(p, p, p, bias_tiles, lam_params, ln_g.reshape(1, DIFF_VDIM))


def diff_bias_tiles(rel_bias_b, seq):
    t = ATT_BLOCK
    bucket = _t5_bucket_table(seq + t)
    sat = bucket[-1]
    d_sat = int(np.max(np.nonzero(bucket != sat)[0])) + 1
    n_full = (d_sat + t - 1 + t - 1) // t
    row = np.arange(t)[:, None]
    col = np.arange(t)[None, :]
    dist = np.clip(np.arange(n_full + 1)[:, None, None] * t + row - col, 0, None)
    assert dist[n_full].min() >= d_sat
    bk = bucket[np.minimum(dist, seq + t - 1)]
    return rel_bias_b.T[:, bk].astype(F32)


def _logsig_cumsum_body(f_ref, b_ref, c_ref, carry_scr):
    t = f_ref.shape[0]

    @pl.when(pl.program_id(1) == 0)
    def _():
        carry_scr[...] = jnp.zeros_like(carry_scr)

    x = f_ref[...] + b_ref[...]
    ls = jnp.minimum(x, 0.0) - jnp.log1p(jnp.exp(-jnp.abs(x)))
    row = lax.broadcasted_iota(I32, (t, t), 0)
    col = lax.broadcasted_iota(I32, (t, t), 1)
    tri = (row >= col).astype(F32)
    c = jnp.dot(tri, ls, preferred_element_type=F32,
                precision=lax.Precision.HIGHEST) + carry_scr[...]
    c_ref[...] = c
    carry_scr[...] = c[t - 1:t, :]


def logsig_cumsum(fg, b_f, batch, seq, t=512):
    return pl.pallas_call(
        _logsig_cumsum_body,
        grid=(batch, seq // t),
        in_specs=[pl.BlockSpec((None, t, LANES), lambda b, i: (b, i, 0)),
                  pl.BlockSpec((1, LANES), lambda b, i: (0, 0))],
        out_specs=pl.BlockSpec((None, t, LANES), lambda b, i: (b, i, 0)),
        out_shape=jax.ShapeDtypeStruct((batch, seq, LANES), F32),
        scratch_shapes=[pltpu.VMEM((1, LANES), F32)],
        compiler_params=pltpu.CompilerParams(
            dimension_semantics=("parallel", "arbitrary")),
        name="logsig_cumsum",
    )(fg, b_f)


def _fox_body(q_ref, k_ref, v_ref, cq_ref, ck_ref, o_ref, m_scr, l_scr, acc_scr):
    t = ATT_BLOCK
    qi = pl.program_id(2)
    q = q_ref[...] * QK_SCALE
    for a in range(2):
        m_scr[a] = jnp.full((t, 1), NEG_INF, F32)
        l_scr[a] = jnp.zeros((t, 1), F32)
        acc_scr[a] = jnp.zeros((t, HEAD_DIM), F32)

    def step(j, masked):
        off = pl.multiple_of(j * t, t)
        ks = k_ref[pl.ds(off, t), :]
        vs = v_ref[pl.ds(off, t), :]
        if masked:
            row = lax.broadcasted_iota(I32, (t, t), 0)
            col = lax.broadcasted_iota(I32, (t, t), 1)
            causal = row >= col
        for a in range(2):
            sl = slice(a * HEAD_DIM, (a + 1) * HEAD_DIM)
            s = lax.dot_general(q[:, sl], ks[:, sl], (((1,), (1,)), ((), ())),
                                preferred_element_type=F32)
            s = s + (cq_ref[a] - ck_ref[a, :, pl.ds(off, t)])
            if masked:
                s = jnp.where(causal, s, NEG_INF)
            m_prev = m_scr[a]
            m_new = jnp.maximum(m_prev, jnp.max(s, axis=-1, keepdims=True))
            alpha = jnp.exp(m_prev - m_new)
            p = jnp.exp(s - m_new)
            l_scr[a] = alpha * l_scr[a] + jnp.sum(p, axis=-1, keepdims=True)
            acc_scr[a] = alpha * acc_scr[a] + jnp.dot(
                p.astype(BF16), vs[:, sl], preferred_element_type=F32)
            m_scr[a] = m_new

    def loop_body(j, carry):
        step(j, False)
        return carry

    lax.fori_loop(0, qi, loop_body, 0)
    step(qi, True)
    o_ref[...] = jnp.concatenate(
        [acc_scr[0] / l_scr[0], acc_scr[1] / l_scr[1]], axis=1).astype(o_ref.dtype)


def fox_attention(p, cq, ck, batch, seq):
    t = ATT_BLOCK
    nk = C_WIDTH // LANES
    return pl.pallas_call(
        _fox_body,
        grid=(batch, C_HEADS // 2, seq // t),
        in_specs=[
            pl.BlockSpec((None, t, LANES), lambda b, h, i: (b, i, h)),
            pl.BlockSpec((None, seq, LANES), lambda b, h, i: (b, 0, nk + h)),
            pl.BlockSpec((None, seq, LANES), lambda b, h, i: (b, 0, 2 * nk + h)),
            pl.BlockSpec((None, 2, t, 1), lambda b, h, i: (b, h, i, 0)),
            pl.BlockSpec((None, 2, 1, seq), lambda b, h, i: (b, h, 0, 0)),
        ],
        out_specs=pl.BlockSpec((None, t, LANES), lambda b, h, i: (b, i, h)),
        out_shape=jax.ShapeDtypeStruct((batch, seq, C_WIDTH), BF16),
        scratch_shapes=[pltpu.VMEM((2, t, 1), F32), pltpu.VMEM((2, t, 1), F32),
                        pltpu.VMEM((2, t, HEAD_DIM), F32)],
        compiler_params=pltpu.CompilerParams(
            dimension_semantics=("parallel", "parallel", "arbitrary")),
        name="fox_attention",
    )(p, p, p, cq, ck)


def _even_out_body(o1, l1, o2, l2, o3, l3, ob_ref, w_ref, x_ref, out_ref, a_scr):
    @pl.when(pl.program_id(1) == 0)
    def _():
        a1, a2, a3 = l1[...], l2[...], l3[...]
        mx = jnp.maximum(jnp.maximum(a1, a2), a3)
        e1, e2, e3 = jnp.exp(a1 - mx), jnp.exp(a2 - mx), jnp.exp(a3 - mx)
        oa = (e1 * o1[...] + e2 * o2[...] + e3 * o3[...]) / (e1 + e2 + e3)
        a_scr[:, :A_WIDTH] = oa.astype(BF16)
        a_scr[:, A_WIDTH:] = ob_ref[...]

    out_ref[...] = x_ref[...] + jnp.dot(a_scr[...], w_ref[...], preferred_element_type=F32)


def even_out_proj(branches, ob, w, x, tm=512, tn=512):
    m, d = x.shape
    half = pl.BlockSpec((tm, A_WIDTH), lambda i, j: (i, 0))
    flat = [a for pair in branches for a in pair]
    return pl.pallas_call(
        _even_out_body,
        grid=(m // tm, d // tn),
        in_specs=[half] * 6 + [
            pl.BlockSpec((tm, B_V_WIDTH), lambda i, j: (i, 0)),
            pl.BlockSpec((A_WIDTH + B_V_WIDTH, tn), lambda i, j: (0, j)),
            pl.BlockSpec((tm, tn), lambda i, j: (i, j))],
        out_specs=pl.BlockSpec((tm, tn), lambda i, j: (i, j)),
        out_shape=jax.ShapeDtypeStruct((m, d), F32),
        scratch_shapes=[pltpu.VMEM((tm, A_WIDTH + B_V_WIDTH), BF16)],
        compiler_params=pltpu.CompilerParams(
            dimension_semantics=("parallel", "arbitrary")),
        name="even_out_proj",
    )(*flat, ob, w, x)


def _matmul_res_body(a_ref, w_ref, x_ref, o_ref):
    o_ref[...] = x_ref[...] + jnp.dot(a_ref[...], w_ref[...], preferred_element_type=F32)


def matmul_residual(a, w, x, tm=512, tn=512):
    m, k = a.shape
    n = w.shape[1]
    return pl.pallas_call(
        _matmul_res_body,
        grid=(m // tm, n // tn),
        in_specs=[pl.BlockSpec((tm, k), lambda i, j: (i, 0)),
                  pl.BlockSpec((k, tn), lambda i, j: (0, j)),
                  pl.BlockSpec((tm, tn), lambda i, j: (i, j))],
        out_specs=pl.BlockSpec((tm, tn), lambda i, j: (i, j)),
        out_shape=jax.ShapeDtypeStruct((m, n), F32),
        compiler_params=pltpu.CompilerParams(
            dimension_semantics=("parallel", "parallel")),
        name="matmul_residual",
    )(a, w, x)


def _add_norm_body(x_ref, r_ref, g_ref, o_ref):
    x = x_ref[...] + r_ref[...]
    ms = jnp.mean(x * x, axis=-1, keepdims=True)
    o_ref[...] = x * lax.rsqrt(ms + RMS_EPS) * g_ref[...]


def add_norm(x, r, g, tm=512):
    m, d = x.shape
    row = pl.BlockSpec((tm, d), lambda i: (i, 0))
    return pl.pallas_call(
        _add_norm_body,
        grid=(m // tm,),
        in_specs=[row, row, pl.BlockSpec((1, d), lambda i: (0, 0))],
        out_specs=row,
        out_shape=jax.ShapeDtypeStruct((m, d), F32),
        compiler_params=pltpu.CompilerParams(dimension_semantics=("parallel",)),
        name="add_norm",
    )(x, r, g.reshape(1, d))


PEER_CAND_ROWS = PEER_TOPK + 8 * (PEER_TOPK - 1)


def _peer_topk_body(q_ref, sk_ref, idx_ref, gate_ref, ts_scr, ti_scr, bs_scr, be_scr):
    tm = q_ref.shape[0]
    neg_inf = jnp.float32(-jnp.inf)
    key_id = lax.broadcasted_iota(I32, (N_KEYS, tm), 0)

    def pair_body(pr, carry):
        off = pl.multiple_of(pr * PEER_KEY_HALF, PEER_KEY_HALF)
        sc = lax.dot_general(sk_ref[pr], q_ref[:, pl.ds(off, PEER_KEY_HALF)],
                             (((1,), (1,)), ((), ())), preferred_element_type=F32)

        def k_body(k, vals):
            m = jnp.max(vals, axis=0, keepdims=True)
            sel = jnp.min(jnp.where(vals == m, key_id, N_KEYS), axis=0, keepdims=True)
            ts_scr[pr, pl.ds(k, 1), :] = m
            ti_scr[pr, pl.ds(k, 1), :] = sel
            return jnp.where(key_id == sel, neg_inf, vals)

        lax.fori_loop(0, PEER_TOPK, k_body, sc)
        return carry

    lax.fori_loop(0, 2 * PEER_HEADS, pair_body, 0)

    r = lax.broadcasted_iota(I32, (PEER_CAND_ROWS, 1), 0)
    cand_id = jnp.where(r < PEER_TOPK, r,
                        (1 + (r - PEER_TOPK) // 8) * PEER_TOPK + (r - PEER_TOPK) % 8)

    def head_body(h, carry):
        s1, s2 = ts_scr[2 * h], ts_scr[2 * h + 1]
        i1, i2 = ti_scr[2 * h] * N_KEYS, ti_scr[2 * h + 1]
        vals = jnp.concatenate(
            [s1[0:1] + s2] + [s1[a:a + 1] + s2[0:8] for a in range(1, PEER_TOPK)], axis=0)
        eidx = jnp.concatenate(
            [i1[0:1] + i2] + [i1[a:a + 1] + i2[0:8] for a in range(1, PEER_TOPK)], axis=0)

        def k_body(k, vals):
            m = jnp.max(vals, axis=0, keepdims=True)
            sel = jnp.min(jnp.where(vals == m, cand_id, PEER_TOPK * PEER_TOPK),
                          axis=0, keepdims=True)
            hit = cand_id == sel
            bs_scr[pl.ds(k, 1), :] = m
            be_scr[pl.ds(k, 1), :] = jnp.sum(jnp.where(hit, eidx, 0), axis=0, keepdims=True)
            return jnp.where(hit, neg_inf, vals)

        lax.fori_loop(0, PEER_TOPK, k_body, vals)
        bs = bs_scr[...]
        e = jnp.exp(bs - jnp.max(bs, axis=0, keepdims=True))
        row0 = pl.multiple_of(h * PEER_TOPK, PEER_TOPK)
        gate_ref[pl.ds(row0, PEER_TOPK), :] = e / jnp.sum(e, axis=0, keepdims=True)
        idx_ref[pl.ds(row0, PEER_TOPK), :] = be_scr[...]
        return carry

    lax.fori_loop(0, PEER_HEADS, head_body, 0)


def peer_topk(q, subkeys, tm=256):
    m = q.shape[0]
    n_sel = PEER_HEADS * PEER_TOPK
    out_spec = pl.BlockSpec((n_sel, tm), lambda i: (0, i))
    return pl.pallas_call(
        _peer_topk_body,
        grid=(m // tm,),
        in_specs=[pl.BlockSpec((tm, q.shape[1]), lambda i: (i, 0)),
                  pl.BlockSpec(subkeys.shape, lambda i: (0, 0, 0))],
        out_specs=[out_spec, out_spec],
        out_shape=[jax.ShapeDtypeStruct((n_sel, m), I32),
                   jax.ShapeDtypeStruct((n_sel, m), F32)],
        scratch_shapes=[pltpu.VMEM((2 * PEER_HEADS, PEER_TOPK, tm), F32),
                        pltpu.VMEM((2 * PEER_HEADS, PEER_TOPK, tm), I32),
                        pltpu.VMEM((PEER_TOPK, tm), F32),
                        pltpu.VMEM((PEER_TOPK, tm), I32)],
        compiler_params=pltpu.CompilerParams(dimension_semantics=("parallel",)),
        name="peer_topk",
    )(q, subkeys)


def _gelu_gate_body(a_ref, g_ref, o_ref):
    a = a_ref[...]
    o_ref[...] = g_ref[...] * (0.5 * a * (1.0 + lax.erf(a * (2.0 ** -0.5))))


def gelu_gate(act, gate, tm=2048):
    m, n = act.shape
    spec = pl.BlockSpec((tm, n), lambda i: (i, 0))
    return pl.pallas_call(
        _gelu_gate_body,
        grid=(m // tm,),
        in_specs=[spec, spec],
        out_specs=spec,
        out_shape=jax.ShapeDtypeStruct((m, n), F32),
        compiler_params=pltpu.CompilerParams(dimension_semantics=("parallel",)),
        name="gelu_gate",
    )(act, gate)


SC_TOK_CHUNK = 8


def _sc_worker_id():
    return lax.axis_index("s") * SC_CORES + lax.axis_index("c")


def peer_expert_dots(h, idx, u):
    m, d = h.shape
    n_sel = idx.shape[1]
    per_w = m // SC_WORKERS
    n_chunks = per_w // SC_TOK_CHUNK
    n_vec = d // SC_LANES
    groups = n_sel // SC_LANES
    mesh = plsc.VectorSubcoreMesh(core_axis_name="c", subcore_axis_name="s")

    @functools.partial(
        pl.kernel, mesh=mesh,
        out_type=jax.ShapeDtypeStruct((m, n_sel), F32),
        scratch_types=[
            pltpu.VMEM((SC_TOK_CHUNK, n_sel), I32),
            pltpu.VMEM((SC_TOK_CHUNK, d), F32),
            pltpu.VMEM((SC_TOK_CHUNK, n_sel), F32),
            pltpu.VMEM((2, SC_LANES, d), F32),
            pltpu.VMEM((SC_LANES * SC_LANES,), F32),
            pltpu.SemaphoreType.DMA((2,)),
        ],
        compiler_params=pltpu.CompilerParams(needs_layout_passes=False),
        name="peer_expert_dots",
    )
    def k(h_hbm, idx_hbm, u_hbm, act_hbm, idx_v, h_v, act_v, rows_v, part_v, sems):
        base = _sc_worker_id() * per_w
        lane = lax.broadcasted_iota(I32, (SC_LANES,), 0)

        def gather(tt, g, slot):
            ids = idx_v[tt, pl.ds(g * SC_LANES, SC_LANES)]
            return pltpu.make_async_copy(u_hbm.at[ids], rows_v.at[slot], sems.at[slot])

        def compute(tt, g, slot):
            def j_body(j, accs):
                off = pl.multiple_of(j * SC_LANES, SC_LANES)
                xv = h_v[tt, pl.ds(off, SC_LANES)]
                return tuple(accs[e] + rows_v[slot, e, pl.ds(off, SC_LANES)] * xv
                             for e in range(SC_LANES))

            accs = lax.fori_loop(
                0, n_vec, j_body,
                tuple(jnp.zeros((SC_LANES,), F32) for _ in range(SC_LANES)))
            for e in range(SC_LANES):
                part_v[pl.ds(e * SC_LANES, SC_LANES)] = accs[e]
            tot = jnp.zeros((SC_LANES,), F32)
            for l in range(SC_LANES):
                tot = tot + plsc.load_gather(part_v, [lane * SC_LANES + l])
            act_v[tt, pl.ds(g * SC_LANES, SC_LANES)] = tot

        def chunk_body(c, carry):
            t0 = base + c * SC_TOK_CHUNK
            pltpu.sync_copy(idx_hbm.at[pl.ds(t0, SC_TOK_CHUNK)], idx_v)
            pltpu.sync_copy(h_hbm.at[pl.ds(t0, SC_TOK_CHUNK)], h_v)
            n_items = SC_TOK_CHUNK * groups
            gather(0, 0, 0).start()

            def item_body(it, carry2):
                for par in range(2):
                    item = it * 2 + par
                    tt, g = item // groups, item % groups
                    nxt = item + 1

                    @pl.when(nxt < n_items)
                    def _():
                        gather(nxt // groups, nxt % groups, 1 - par).start()

                    gather(tt, g, par).wait()
                    compute(tt, g, par)
                return carry2

            lax.fori_loop(0, n_items // 2, item_body, 0)
            pltpu.sync_copy(act_v, act_hbm.at[pl.ds(t0, SC_TOK_CHUNK)])
            return carry

        lax.fori_loop(0, n_chunks, chunk_body, 0)

    return k(h, idx, u)


def peer_expert_combine(w, idx, v):
    m, n_sel = w.shape
    d = v.shape[1]
    per_w = m // SC_WORKERS
    n_chunks = per_w // SC_TOK_CHUNK
    n_vec = d // SC_LANES
    groups = n_sel // SC_LANES
    mesh = plsc.VectorSubcoreMesh(core_axis_name="c", subcore_axis_name="s")

    @functools.partial(
        pl.kernel, mesh=mesh,
        out_type=jax.ShapeDtypeStruct((m, d), F32),
        scratch_types=[
            pltpu.VMEM((SC_TOK_CHUNK, n_sel), I32),
            pltpu.VMEM((SC_TOK_CHUNK, n_sel), F32),
            pltpu.VMEM((SC_TOK_CHUNK, d), F32),
            pltpu.VMEM((2, SC_LANES, d), F32),
            pltpu.SemaphoreType.DMA((2,)),
        ],
        compiler_params=pltpu.CompilerParams(needs_layout_passes=False),
        name="peer_expert_combine",
    )
    def k(w_hbm, idx_hbm, v_hbm, out_hbm, idx_v, w_v, out_v, rows_v, sems):
        base = _sc_worker_id() * per_w

        def gather(tt, g, slot):
            ids = idx_v[tt, pl.ds(g * SC_LANES, SC_LANES)]
            return pltpu.make_async_copy(v_hbm.at[ids], rows_v.at[slot], sems.at[slot])

        def compute(tt, g, slot):
            splat = [plsc.load_gather(
                w_v, [jnp.full((SC_LANES,), tt, I32),
                      jnp.full((SC_LANES,), g * SC_LANES + e, I32)])
                for e in range(SC_LANES)]

            def j_body(j, carry):
                off = pl.multiple_of(j * SC_LANES, SC_LANES)
                acc = out_v[tt, pl.ds(off, SC_LANES)]
                for e in range(SC_LANES):
                    acc = acc + splat[e] * rows_v[slot, e, pl.ds(off, SC_LANES)]
                out_v[tt, pl.ds(off, SC_LANES)] = acc
                return carry

            lax.fori_loop(0, n_vec, j_body, 0)

        def chunk_body(c, carry):
            t0 = base + c * SC_TOK_CHUNK
            pltpu.sync_copy(idx_hbm.at[pl.ds(t0, SC_TOK_CHUNK)], idx_v)
            pltpu.sync_copy(w_hbm.at[pl.ds(t0, SC_TOK_CHUNK)], w_v)
            n_items = SC_TOK_CHUNK * groups

            def zero_body(z, carry2):
                tt, j = z // n_vec, z % n_vec
                out_v[tt, pl.ds(pl.multiple_of(j * SC_LANES, SC_LANES), SC_LANES)] = (
                    jnp.zeros((SC_LANES,), F32))
                return carry2

            lax.fori_loop(0, SC_TOK_CHUNK * n_vec, zero_body, 0)
            gather(0, 0, 0).start()

            def item_body(it, carry2):
                for par in range(2):
                    item = it * 2 + par
                    tt, g = item // groups, item % groups
                    nxt = item + 1

                    @pl.when(nxt < n_items)
                    def _():
                        gather(nxt // groups, nxt % groups, 1 - par).start()

                    gather(tt, g, par).wait()
                    compute(tt, g, par)
                return carry2

            lax.fori_loop(0, n_items // 2, item_body, 0)
            pltpu.sync_copy(out_v, out_hbm.at[pl.ds(t0, SC_TOK_CHUNK)])
            return carry

        lax.fori_loop(0, n_chunks, chunk_body, 0)

    return k(w, idx, v)


def peer_ffn(x, res, g, wq, subkeys, u, v):
    outs = norm_matmul(x, g, wq.astype(BF16), res=res, want_h=True)
    if res is None:
        q, h = outs
        xs = x
    else:
        q, xs, h = outs
    sk = subkeys.reshape(2 * PEER_HEADS, N_KEYS, PEER_KEY_HALF).astype(BF16)
    idx_t, gate_t = peer_topk(q, sk)
    idx = idx_t.T
    act = peer_expert_dots(h, idx, u)
    w = gelu_gate(act, gate_t.T)
    return xs, peer_expert_combine(w, idx, v)


def kernel(x, norm_mix_g, norm_ffn_g, final_norm_g, rel_bias, even_w_in, even_w_out,
           diff_lambda, diff_ln_g, odd_w_in, odd_b_f, odd_w_out, peer_wq, peer_subkeys,
           peer_u, peer_v):
    batch, seq, d = x.shape
    m = batch * seq
    xf = x.reshape(m, d)

    (p,) = norm_matmul(xf, norm_mix_g[0], even_w_in[0].astype(BF16))
    branches = []
    for window, dil in DILATED_BRANCHES:
        tile = dilated_bias_tile(rel_bias[:, :A_HEADS], window, dil)
        branches.append(dilated_branch(p, tile, dil, batch, seq))
    lam_init = 0.8 - 0.6 * math.exp(-0.3 * 0)
    ob = diff_attention(p.reshape(batch, seq, -1), diff_bias_tiles(rel_bias[:, A_HEADS:], seq),
                        diff_lambda[0], diff_ln_g[0], lam_init, batch, seq)
    xf = even_out_proj(branches, ob.reshape(m, B_V_WIDTH), even_w_out[0].astype(BF16), xf)
    xf, peer0 = peer_ffn(xf, None, norm_ffn_g[0], peer_wq[0], peer_subkeys[0],
                         peer_u[0], peer_v[0])

    w_in = odd_w_in[0]
    p, xf, h = norm_matmul(xf, norm_mix_g[1], w_in[:, :3 * C_WIDTH].astype(BF16),
                           res=peer0, want_h=True)
    w_gate = jnp.pad(w_in[:, 3 * C_WIDTH:], ((0, 0), (0, LANES - C_HEADS)))
    fg = gate_matmul(h, w_gate)
    b_f = jnp.pad(odd_b_f[0], (0, LANES - C_HEADS)).reshape(1, LANES)
    c = logsig_cumsum(fg.reshape(batch, seq, LANES), b_f, batch, seq)
    ct = c[:, :, :C_HEADS].transpose(0, 2, 1)
    o = fox_attention(p.reshape(batch, seq, -1), ct[:, :, :, None], ct[:, :, None, :],
                      batch, seq)
    xf = matmul_residual(o.reshape(m, C_WIDTH), odd_w_out[0].astype(BF16), xf)
    xf, peer1 = peer_ffn(xf, None, norm_ffn_g[1], peer_wq[1], peer_subkeys[1],
                         peer_u[1], peer_v[1])

    return add_norm(xf, peer1, final_norm_g).reshape(batch, seq, d)


def _gate_matmul_body(h_ref, w_ref, o_ref):
    o_ref[...] = jnp.dot(h_ref[...], w_ref[...], preferred_element_type=F32,
                         precision=lax.Precision.HIGHEST)


def gate_matmul(h, w, tm=512):
    m, d = h.shape
    n = w.shape[1]
    return pl.pallas_call(
        _gate_matmul_body,
        grid=(m // tm,),
        in_specs=[pl.BlockSpec((tm, d), lambda i: (i, 0)),
                  pl.BlockSpec((d, n), lambda i: (0, 0))],
        out_specs=pl.BlockSpec((tm, n), lambda i: (i, 0)),
        out_shape=jax.ShapeDtypeStruct((m, n), F32),
        compiler_params=pltpu.CompilerParams(dimension_semantics=("parallel",)),
        name="gate_matmul",
    )(h, w)
```

```python
import functools
import math

import numpy as np
import jax
import jax.numpy as jnp
from jax import lax
from jax.experimental import pallas as pl
from jax.experimental.pallas import tpu as pltpu
from jax.experimental.pallas import tpu_sc as plsc

F32 = jnp.float32
BF16 = jnp.bfloat16
I32 = jnp.int32

D_MODEL = 1024
HEAD_DIM = 64
A_HEADS = 8
DILATED_BRANCHES = ((128, 1), (512, 4), (2048, 16))
DIFF_HALF = 64
DIFF_VDIM = 128
B_HEADS = 4
C_HEADS = 16
N_BUCKETS = 32
MAX_DISTANCE = 2048
PEER_HEADS = 8
N_KEYS = 128
PEER_TOPK = 16
PEER_KEY_HALF = 128
RMS_EPS = 1e-6
NEG_INF = -1e30
A_WIDTH = A_HEADS * HEAD_DIM
B_QK_WIDTH = B_HEADS * 2 * DIFF_HALF
B_V_WIDTH = B_HEADS * DIFF_VDIM
C_WIDTH = C_HEADS * HEAD_DIM
QK_SCALE = 0.125
LANES = 128
DIL_BLOCK = 128
ATT_BLOCK = 256

SC_CORES = 2
SC_SUBCORES = 16
SC_LANES = 16
SC_WORKERS = SC_CORES * SC_SUBCORES


def _t5_bucket_table(n):
    max_exact = N_BUCKETS // 2
    d = np.arange(n)
    df = np.maximum(d, 1).astype(np.float32)
    large = max_exact + (
        np.log(df / np.float32(max_exact)) / np.float32(math.log(MAX_DISTANCE / max_exact))
        * np.float32(N_BUCKETS - max_exact)).astype(np.int32)
    large = np.minimum(large, N_BUCKETS - 1)
    return np.where(d < max_exact, d, large).astype(np.int32)


def _norm_matmul_body(*refs, has_res, want_h):
    it = iter(refs)
    x_ref = next(it)
    r_ref = next(it) if has_res else None
    g_ref = next(it)
    w_ref = next(it)
    o_ref = next(it)
    xs_ref = next(it) if has_res else None
    hout_ref = next(it) if want_h else None
    h_scr = next(it)

    @pl.when(pl.program_id(1) == 0)
    def _():
        x = x_ref[...]
        if has_res:
            x = x + r_ref[...]
            xs_ref[...] = x
        ms = jnp.mean(x * x, axis=-1, keepdims=True)
        h = x * lax.rsqrt(ms + RMS_EPS) * g_ref[...]
        if want_h:
            hout_ref[...] = h
        h_scr[...] = h.astype(BF16)

    o_ref[...] = jnp.dot(h_scr[...], w_ref[...],
                         preferred_element_type=F32).astype(o_ref.dtype)


def norm_matmul(x, g, w, *, res=None, want_h=False, out_dtype=BF16, tm=512, tn=512):
    m, d = x.shape
    n = w.shape[1]
    tn = min(tn, n)
    row = pl.BlockSpec((tm, d), lambda i, j: (i, 0))
    in_specs = [row] + ([row] if res is not None else []) + [
        pl.BlockSpec((1, d), lambda i, j: (0, 0)),
        pl.BlockSpec((d, tn), lambda i, j: (0, j))]
    out_specs = [pl.BlockSpec((tm, tn), lambda i, j: (i, j))]
    out_shape = [jax.ShapeDtypeStruct((m, n), out_dtype)]
    if res is not None:
        out_specs.append(row)
        out_shape.append(jax.ShapeDtypeStruct((m, d), F32))
    if want_h:
        out_specs.append(row)
        out_shape.append(jax.ShapeDtypeStruct((m, d), F32))
    args = [x] + ([res] if res is not None else []) + [g.reshape(1, d), w]
    return pl.pallas_call(
        functools.partial(_norm_matmul_body, has_res=res is not None, want_h=want_h),
        grid=(m // tm, n // tn),
        in_specs=in_specs,
        out_specs=out_specs,
        out_shape=out_shape,
        scratch_shapes=[pltpu.VMEM((tm, d), BF16)],
        compiler_params=pltpu.CompilerParams(
            dimension_semantics=("parallel", "arbitrary")),
        name="norm_matmul",
    )(*args)


def _dilated_body(q_ref, kp_ref, kc_ref, vp_ref, vc_ref, b_ref, o_ref, lse_ref):
    i = pl.program_id(3)
    q = q_ref[...] * QK_SCALE
    k = jnp.concatenate([kp_ref[...], kc_ref[...]], axis=0)
    v = jnp.concatenate([vp_ref[...], vc_ref[...]], axis=0)
    col = lax.broadcasted_iota(I32, (DIL_BLOCK, 2 * DIL_BLOCK), 1)
    has_prev = jnp.logical_or(col >= DIL_BLOCK, i > 0)
    outs, lses = [], []
    for hh in range(2):
        sl = slice(hh * HEAD_DIM, (hh + 1) * HEAD_DIM)
        s = lax.dot_general(q[:, sl], k[:, sl], (((1,), (1,)), ((), ())),
                            preferred_element_type=F32)
        s = jnp.where(has_prev, s + b_ref[hh], NEG_INF)
        m = jnp.max(s, axis=-1, keepdims=True)
        p = jnp.exp(s - m)
        l = jnp.sum(p, axis=-1, keepdims=True)
        o = jnp.dot(p.astype(BF16), v[:, sl], preferred_element_type=F32) / l
        outs.append(o)
        lses.append(jnp.broadcast_to(m + jnp.log(l), (DIL_BLOCK, HEAD_DIM)))
    o_ref[...] = jnp.concatenate(outs, axis=1)
    lse_ref[...] = jnp.concatenate(lses, axis=1)


def dilated_branch(p, bias_tile, dil, batch, seq):
    n_cols = p.shape[1]
    cb = n_cols // LANES
    rows = seq // dil
    nblk = rows // DIL_BLOCK
    pv = p.reshape(batch, rows, dil * n_cols)
    kq, kk, kv = 0, A_WIDTH // LANES, 2 * A_WIDTH // LANES
    blk = (None, DIL_BLOCK, LANES)

    def spec(col0, prev):
        if prev:
            return pl.BlockSpec(blk, lambda b, h, r, i: (b, jnp.maximum(i - 1, 0), r * cb + col0 + h))
        return pl.BlockSpec(blk, lambda b, h, r, i: (b, i, r * cb + col0 + h))

    ocb = A_WIDTH // LANES
    ospec = pl.BlockSpec(blk, lambda b, h, r, i: (b, i, r * ocb + h))
    oshape = jax.ShapeDtypeStruct((batch, rows, dil * A_WIDTH), F32)
    o, lse = pl.pallas_call(
        _dilated_body,
        grid=(batch, A_HEADS // 2, dil, nblk),
        in_specs=[spec(kq, False), spec(kk, True), spec(kk, False),
                  spec(kv, True), spec(kv, False),
                  pl.BlockSpec((2, DIL_BLOCK, 2 * DIL_BLOCK), lambda b, h, r, i: (h, 0, 0))],
        out_specs=[ospec, ospec],
        out_shape=[oshape, oshape],
        compiler_params=pltpu.CompilerParams(
            dimension_semantics=("parallel", "parallel", "parallel", "arbitrary")),
        name=f"dilated_d{dil}",
    )(pv, pv, pv, pv, pv, bias_tile)
    return o.reshape(batch * seq, A_WIDTH), lse.reshape(batch * seq, A_WIDTH)


def dilated_bias_tile(rel_bias_a, window, dil):
    n = window // dil
    assert n == DIL_BLOCK
    bucket = _t5_bucket_table(window + 1)
    row = np.arange(DIL_BLOCK)[:, None]
    c = np.arange(2 * DIL_BLOCK)[None, :]
    j = DIL_BLOCK + row - c
    valid = (j >= 0) & (j <= n)
    bk = bucket[np.clip(j, 0, n) * dil]
    tile = rel_bias_a.T[:, bk]
    return jnp.where(jnp.asarray(valid)[None], tile, NEG_INF).astype(F32)


def _diff_body(q_ref, k_ref, v_ref, b_ref, lam_ref, g_ref, o_ref,
               m_scr, l_scr, acc_scr, *, n_tiles, lam_init):
    t = ATT_BLOCK
    qi = pl.program_id(2)
    q = q_ref[...] * QK_SCALE
    for a in range(2):
        m_scr[a] = jnp.full((t, 1), NEG_INF, F32)
        l_scr[a] = jnp.zeros((t, 1), F32)
        acc_scr[a] = jnp.zeros((t, DIFF_VDIM), F32)

    def step(j, masked):
        off = pl.multiple_of(j * t, t)
        ks = k_ref[pl.ds(off, t), :]
        vs = v_ref[pl.ds(off, t), :]
        bias = b_ref[jnp.minimum(qi - j, n_tiles - 1)]
        if masked:
            row = lax.broadcasted_iota(I32, (t, t), 0)
            col = lax.broadcasted_iota(I32, (t, t), 1)
            causal = row >= col
        for a in range(2):
            sl = slice(a * DIFF_HALF, (a + 1) * DIFF_HALF)
            s = lax.dot_general(q[:, sl], ks[:, sl], (((1,), (1,)), ((), ())),
                                preferred_element_type=F32) + bias
            if masked:
                s = jnp.where(causal, s, NEG_INF)
            m_prev = m_scr[a]
            m_new = jnp.maximum(m_prev, jnp.max(s, axis=-1, keepdims=True))
            alpha = jnp.exp(m_prev - m_new)
            p = jnp.exp(s - m_new)
            l_scr[a] = alpha * l_scr[a] + jnp.sum(p, axis=-1, keepdims=True)
            acc_scr[a] = alpha * acc_scr[a] + jnp.dot(
                p.astype(BF16), vs, preferred_element_type=F32)
            m_scr[a] = m_new

    def loop_body(j, carry):
        step(j, False)
        return carry

    lax.fori_loop(0, qi, loop_body, 0)
    step(qi, True)

    lp = lam_ref[...]
    lam = (jnp.exp(jnp.sum(lp[0:1] * lp[1:2])) - jnp.exp(jnp.sum(lp[2:3] * lp[3:4]))
           + lam_init)
    o = acc_scr[0] / l_scr[0] - lam * (acc_scr[1] / l_scr[1])
    ms = jnp.mean(o * o, axis=-1, keepdims=True)
    y = o * lax.rsqrt(ms + RMS_EPS) * g_ref[...]
    o_ref[...] = (y * (1.0 - lam_init)).astype(o_ref.dtype)


def diff_attention(p, bias_tiles, lam_params, ln_g, lam_init, batch, seq):
    t = ATT_BLOCK
    n_tiles = bias_tiles.shape[1]
    cq = 3 * A_WIDTH // LANES
    ck = cq + B_QK_WIDTH // LANES
    cv = ck + B_QK_WIDTH // LANES
    return pl.pallas_call(
        functools.partial(_diff_body, n_tiles=n_tiles, lam_init=lam_init),
        grid=(batch, B_HEADS, seq // t),
        in_specs=[
            pl.BlockSpec((None, t, LANES), lambda b, h, i: (b, i, cq + h)),
            pl.BlockSpec((None, seq, LANES), lambda b, h, i: (b, 0, ck + h)),
            pl.BlockSpec((None, seq, LANES), lambda b, h, i: (b, 0, cv + h)),
            pl.BlockSpec((None, n_tiles, t, t), lambda b, h, i: (h, 0, 0, 0)),
            pl.BlockSpec((4, DIFF_HALF), lambda b, h, i: (0, 0)),
            pl.BlockSpec((1, DIFF_VDIM), lambda b, h, i: (0, 0)),
        ],
        out_specs=pl.BlockSpec((None, t, LANES), lambda b, h, i: (b, i, h)),
        out_shape=jax.ShapeDtypeStruct((batch, seq, B_V_WIDTH), BF16),
        scratch_shapes=[pltpu.VMEM((2, t, 1), F32), pltpu.VMEM((2, t, 1), F32),
                        pltpu.VMEM((2, t, DIFF_VDIM), F32)],
        compiler_params=pltpu.CompilerParams(
            dimension_semantics=("parallel", "parallel", "arbitrary")),
        name="diff_attention",
    )(p, p, p, bias_tiles, lam_params, ln_g.reshape(1, DIFF_VDIM))


def diff_bias_tiles(rel_bias_b, seq):
    t = ATT_BLOCK
    bucket = _t5_bucket_table(seq + t)
    sat = bucket[-1]
    d_sat = int(np.max(np.nonzero(bucket != sat)[0])) + 1
    n_full = (d_sat + t - 1 + t - 1) // t
    n_tiles = n_full + 1
    assert n_full * t - (t - 1) >= d_sat
    span = n_tiles * t + t - 1
    dist = np.clip(span - 1 - np.arange(span) - (t - 1), 0, None)
    rev = rel_bias_b.T[:, bucket[dist]].astype(F32)
    start = span - t - (np.arange(n_tiles)[:, None] * t + np.arange(t)[None, :])
    rows = jax.vmap(lambda c: lax.dynamic_slice_in_dim(rev, c, t, axis=1))(
        jnp.asarray(start.reshape(-1), I32))
    return rows.reshape(n_tiles, t, B_HEADS, t).transpose(2, 0, 1, 3)


def _logsig_cumsum_body(f_ref, b_ref, c_ref, carry_scr):
    t = f_ref.shape[0]

    @pl.when(pl.program_id(1) == 0)
    def _():
        carry_scr[...] = jnp.zeros_like(carry_scr)

    x = f_ref[...] + b_ref[...]
    ls = jnp.minimum(x, 0.0) - jnp.log1p(jnp.exp(-jnp.abs(x)))
    row = lax.broadcasted_iota(I32, (t, t), 0)
    col = lax.broadcasted_iota(I32, (t, t), 1)
    tri = (row >= col).astype(F32)
    c = jnp.dot(tri, ls, preferred_element_type=F32,
                precision=lax.Precision.HIGHEST) + carry_scr[...]
    c_ref[...] = c
    carry_scr[...] = c[t - 1:t, :]


def logsig_cumsum(fg, b_f, batch, seq, t=512):
    return pl.pallas_call(
        _logsig_cumsum_body,
        grid=(batch, seq // t),
        in_specs=[pl.BlockSpec((None, t, LANES), lambda b, i: (b, i, 0)),
                  pl.BlockSpec((1, LANES), lambda b, i: (0, 0))],
        out_specs=pl.BlockSpec((None, t, LANES), lambda b, i: (b, i, 0)),
        out_shape=jax.ShapeDtypeStruct((batch, seq, LANES), F32),
        scratch_shapes=[pltpu.VMEM((1, LANES), F32)],
        compiler_params=pltpu.CompilerParams(
            dimension_semantics=("parallel", "arbitrary")),
        name="logsig_cumsum",
    )(fg, b_f)


def _fox_body(q_ref, k_ref, v_ref, cq_ref, ck_ref, o_ref, m_scr, l_scr, acc_scr):
    t = ATT_BLOCK
    qi = pl.program_id(2)
    q = q_ref[...] * QK_SCALE
    for a in range(2):
        m_scr[a] = jnp.full((t, 1), NEG_INF, F32)
        l_scr[a] = jnp.zeros((t, 1), F32)
        acc_scr[a] = jnp.zeros((t, HEAD_DIM), F32)

    def step(j, masked):
        off = pl.multiple_of(j * t, t)
        ks = k_ref[pl.ds(off, t), :]
        vs = v_ref[pl.ds(off, t), :]
        if masked:
            row = lax.broadcasted_iota(I32, (t, t), 0)
            col = lax.broadcasted_iota(I32, (t, t), 1)
            causal = row >= col
        for a in range(2):
            sl = slice(a * HEAD_DIM, (a + 1) * HEAD_DIM)
            s = lax.dot_general(q[:, sl], ks[:, sl], (((1,), (1,)), ((), ())),
                                preferred_element_type=F32)
            s = s + (cq_ref[a] - ck_ref[a, :, pl.ds(off, t)])
            if masked:
                s = jnp.where(causal, s, NEG_INF)
            m_prev = m_scr[a]
            m_new = jnp.maximum(m_prev, jnp.max(s, axis=-1, keepdims=True))
            alpha = jnp.exp(m_prev - m_new)
            p = jnp.exp(s - m_new)
            l_scr[a] = alpha * l_scr[a] + jnp.sum(p, axis=-1, keepdims=True)
            acc_scr[a] = alpha * acc_scr[a] + jnp.dot(
                p.astype(BF16), vs[:, sl], preferred_element_type=F32)
            m_scr[a] = m_new

    def loop_body(j, carry):
        step(j, False)
        return carry

    lax.fori_loop(0, qi, loop_body, 0)
    step(qi, True)
    o_ref[...] = jnp.concatenate(
        [acc_scr[0] / l_scr[0], acc_scr[1] / l_scr[1]], axis=1).astype(o_ref.dtype)


def fox_attention(p, cq, ck, batch, seq):
    t = ATT_BLOCK
    nk = C_WIDTH // LANES
    return pl.pallas_call(
        _fox_body,
        grid=(batch, C_HEADS // 2, seq // t),
        in_specs=[
            pl.BlockSpec((None, t, LANES), lambda b, h, i: (b, i, h)),
            pl.BlockSpec((None, seq, LANES), lambda b, h, i: (b, 0, nk + h)),
            pl.BlockSpec((None, seq, LANES), lambda b, h, i: (b, 0, 2 * nk + h)),
            pl.BlockSpec((None, 2, t, 1), lambda b, h, i: (b, h, i, 0)),
            pl.BlockSpec((None, 2, 1, seq), lambda b, h, i: (b, h, 0, 0)),
        ],
        out_specs=pl.BlockSpec((None, t, LANES), lambda b, h, i: (b, i, h)),
        out_shape=jax.ShapeDtypeStruct((batch, seq, C_WIDTH), BF16),
        scratch_shapes=[pltpu.VMEM((2, t, 1), F32), pltpu.VMEM((2, t, 1), F32),
                        pltpu.VMEM((2, t, HEAD_DIM), F32)],
        compiler_params=pltpu.CompilerParams(
            dimension_semantics=("parallel", "parallel", "arbitrary")),
        name="fox_attention",
    )(p, p, p, cq, ck)


def _even_out_body(o1, l1, o2, l2, o3, l3, ob_ref, w_ref, x_ref, out_ref, a_scr):
    @pl.when(pl.program_id(1) == 0)
    def _():
        a1, a2, a3 = l1[...], l2[...], l3[...]
        mx = jnp.maximum(jnp.maximum(a1, a2), a3)
        e1, e2, e3 = jnp.exp(a1 - mx), jnp.exp(a2 - mx), jnp.exp(a3 - mx)
        oa = (e1 * o1[...] + e2 * o2[...] + e3 * o3[...]) / (e1 + e2 + e3)
        a_scr[:, :A_WIDTH] = oa.astype(BF16)
        a_scr[:, A_WIDTH:] = ob_ref[...]

    out_ref[...] = x_ref[...] + jnp.dot(a_scr[...], w_ref[...], preferred_element_type=F32)


def even_out_proj(branches, ob, w, x, tm=512, tn=512):
    m, d = x.shape
    half = pl.BlockSpec((tm, A_WIDTH), lambda i, j: (i, 0))
    flat = [a for pair in branches for a in pair]
    return pl.pallas_call(
        _even_out_body,
        grid=(m // tm, d // tn),
        in_specs=[half] * 6 + [
            pl.BlockSpec((tm, B_V_WIDTH), lambda i, j: (i, 0)),
            pl.BlockSpec((A_WIDTH + B_V_WIDTH, tn), lambda i, j: (0, j)),
            pl.BlockSpec((tm, tn), lambda i, j: (i, j))],
        out_specs=pl.BlockSpec((tm, tn), lambda i, j: (i, j)),
        out_shape=jax.ShapeDtypeStruct((m, d), F32),
        scratch_shapes=[pltpu.VMEM((tm, A_WIDTH + B_V_WIDTH), BF16)],
        compiler_params=pltpu.CompilerParams(
            dimension_semantics=("parallel", "arbitrary")),
        name="even_out_proj",
    )(*flat, ob, w, x)


def _matmul_res_body(a_ref, w_ref, x_ref, o_ref):
    o_ref[...] = x_ref[...] + jnp.dot(a_ref[...], w_ref[...], preferred_element_type=F32)


def matmul_residual(a, w, x, tm=512, tn=512):
    m, k = a.shape
    n = w.shape[1]
    return pl.pallas_call(
        _matmul_res_body,
        grid=(m // tm, n // tn),
        in_specs=[pl.BlockSpec((tm, k), lambda i, j: (i, 0)),
                  pl.BlockSpec((k, tn), lambda i, j: (0, j)),
                  pl.BlockSpec((tm, tn), lambda i, j: (i, j))],
        out_specs=pl.BlockSpec((tm, tn), lambda i, j: (i, j)),
        out_shape=jax.ShapeDtypeStruct((m, n), F32),
        compiler_params=pltpu.CompilerParams(
            dimension_semantics=("parallel", "parallel")),
        name="matmul_residual",
    )(a, w, x)


def _add_norm_body(x_ref, r_ref, g_ref, o_ref):
    x = x_ref[...] + r_ref[...]
    ms = jnp.mean(x * x, axis=-1, keepdims=True)
    o_ref[...] = x * lax.rsqrt(ms + RMS_EPS) * g_ref[...]


def add_norm(x, r, g, tm=512):
    m, d = x.shape
    row = pl.BlockSpec((tm, d), lambda i: (i, 0))
    return pl.pallas_call(
        _add_norm_body,
        grid=(m // tm,),
        in_specs=[row, row, pl.BlockSpec((1, d), lambda i: (0, 0))],
        out_specs=row,
        out_shape=jax.ShapeDtypeStruct((m, d), F32),
        compiler_params=pltpu.CompilerParams(dimension_semantics=("parallel",)),
        name="add_norm",
    )(x, r, g.reshape(1, d))


PEER_CAND_ROWS = PEER_TOPK + 8 * (PEER_TOPK - 1)


def _peer_topk_body(q_ref, sk_ref, idx_ref, gate_ref, ts_scr, ti_scr, bs_scr, be_scr,
                    gt_scr, it_scr):
    tm = q_ref.shape[0]
    neg_inf = jnp.float32(-jnp.inf)
    key_id = lax.broadcasted_iota(I32, (N_KEYS, tm), 0)

    def pair_body(pr, carry):
        off = pl.multiple_of(pr * PEER_KEY_HALF, PEER_KEY_HALF)
        sc = lax.dot_general(sk_ref[pr], q_ref[:, pl.ds(off, PEER_KEY_HALF)],
                             (((1,), (1,)), ((), ())), preferred_element_type=F32)

        def k_body(k, vals):
            m = jnp.max(vals, axis=0, keepdims=True)
            sel = jnp.min(jnp.where(vals == m, key_id, N_KEYS), axis=0, keepdims=True)
            ts_scr[pr, pl.ds(k, 1), :] = m
            ti_scr[pr, pl.ds(k, 1), :] = sel
            return jnp.where(key_id == sel, neg_inf, vals)

        lax.fori_loop(0, PEER_TOPK, k_body, sc)
        return carry

    lax.fori_loop(0, 2 * PEER_HEADS, pair_body, 0)

    r = lax.broadcasted_iota(I32, (PEER_CAND_ROWS, 1), 0)
    cand_id = jnp.where(r < PEER_TOPK, r,
                        (1 + (r - PEER_TOPK) // 8) * PEER_TOPK + (r - PEER_TOPK) % 8)

    def head_body(h, carry):
        s1, s2 = ts_scr[2 * h], ts_scr[2 * h + 1]
        i1, i2 = ti_scr[2 * h] * N_KEYS, ti_scr[2 * h + 1]
        vals = jnp.concatenate(
            [s1[0:1] + s2] + [s1[a:a + 1] + s2[0:8] for a in range(1, PEER_TOPK)], axis=0)
        eidx = jnp.concatenate(
            [i1[0:1] + i2] + [i1[a:a + 1] + i2[0:8] for a in range(1, PEER_TOPK)], axis=0)

        def k_body(k, vals):
            m = jnp.max(vals, axis=0, keepdims=True)
            sel = jnp.min(jnp.where(vals == m, cand_id, PEER_TOPK * PEER_TOPK),
                          axis=0, keepdims=True)
            hit = cand_id == sel
            bs_scr[pl.ds(k, 1), :] = m
            be_scr[pl.ds(k, 1), :] = jnp.sum(jnp.where(hit, eidx, 0), axis=0, keepdims=True)
            return jnp.where(hit, neg_inf, vals)

        lax.fori_loop(0, PEER_TOPK, k_body, vals)
        bs = bs_scr[...]
        e = jnp.exp(bs - jnp.max(bs, axis=0, keepdims=True))
        row0 = pl.multiple_of(h * PEER_TOPK, PEER_TOPK)
        gt_scr[pl.ds(row0, PEER_TOPK), :] = e / jnp.sum(e, axis=0, keepdims=True)
        it_scr[pl.ds(row0, PEER_TOPK), :] = be_scr[...]
        return carry

    lax.fori_loop(0, PEER_HEADS, head_body, 0)
    gate_ref[...] = gt_scr[...].T
    idx_ref[...] = it_scr[...].T


def peer_topk(q, subkeys, tm=256):
    m = q.shape[0]
    n_sel = PEER_HEADS * PEER_TOPK
    out_spec = pl.BlockSpec((tm, n_sel), lambda i: (i, 0))
    return pl.pallas_call(
        _peer_topk_body,
        grid=(m // tm,),
        in_specs=[pl.BlockSpec((tm, q.shape[1]), lambda i: (i, 0)),
                  pl.BlockSpec(subkeys.shape, lambda i: (0, 0, 0))],
        out_specs=[out_spec, out_spec],
        out_shape=[jax.ShapeDtypeStruct((m, n_sel), I32),
                   jax.ShapeDtypeStruct((m, n_sel), F32)],
        scratch_shapes=[pltpu.VMEM((2 * PEER_HEADS, PEER_TOPK, tm), F32),
                        pltpu.VMEM((2 * PEER_HEADS, PEER_TOPK, tm), I32),
                        pltpu.VMEM((PEER_TOPK, tm), F32),
                        pltpu.VMEM((PEER_TOPK, tm), I32),
                        pltpu.VMEM((n_sel, tm), F32),
                        pltpu.VMEM((n_sel, tm), I32)],
        compiler_params=pltpu.CompilerParams(dimension_semantics=("parallel",)),
        name="peer_topk",
    )(q, subkeys)


def _gelu_gate_body(a_ref, g_ref, o_ref):
    a = a_ref[...]
    o_ref[...] = g_ref[...] * (0.5 * a * (1.0 + lax.erf(a * (2.0 ** -0.5))))


def gelu_gate(act, gate, tm=2048):
    m, n = act.shape
    spec = pl.BlockSpec((tm, n), lambda i: (i, 0))
    return pl.pallas_call(
        _gelu_gate_body,
        grid=(m // tm,),
        in_specs=[spec, spec],
        out_specs=spec,
        out_shape=jax.ShapeDtypeStruct((m, n), F32),
        compiler_params=pltpu.CompilerParams(dimension_semantics=("parallel",)),
        name="gelu_gate",
    )(act, gate)


SC_TOK_CHUNK = 32
SC_RING = 8
SC_PAIR = 2 * SC_LANES
SC_BF16_GROUP = 4
SC_FMT = plsc.PackFormat.INTERLEAVED


def _sc_worker_id():
    return lax.axis_index("s") * SC_CORES + lax.axis_index("c")


def pack_bf16_pairs(t):
    lead, d = t.shape[:-1], t.shape[-1]
    tb = t.astype(BF16).reshape(lead + (d // SC_PAIR, 2, SC_LANES))
    tb = jnp.swapaxes(tb, -1, -2)
    return lax.bitcast_convert_type(tb, I32).reshape(lead + (d // 2,))


def _sc_row_pipeline(idx_v, table_hbm, rows_v, sems, n_items, groups, compute):
    def gather(item):
        tt, g = item // groups, item % groups
        ids = idx_v[tt, pl.ds(g * SC_LANES, SC_LANES)]
        slot = item % SC_RING
        return pltpu.make_async_copy(table_hbm.at[ids], rows_v.at[slot], sems.at[slot])

    for s in range(SC_RING - 1):
        gather(s).start()

    def item_body(item, carry):
        nxt = item + SC_RING - 1

        @pl.when(nxt < n_items)
        def _():
            gather(nxt).start()

        gather(item).wait()
        compute(item // groups, item % groups, item % SC_RING)
        return carry

    lax.fori_loop(0, n_items, item_body, 0)


def peer_expert_dots(hp, idx, up):
    m, dw = hp.shape
    n_sel = idx.shape[1]
    per_w = m // SC_WORKERS
    n_chunks = per_w // SC_TOK_CHUNK
    groups = n_sel // SC_LANES
    step = SC_BF16_GROUP * SC_LANES
    mesh = plsc.VectorSubcoreMesh(core_axis_name="c", subcore_axis_name="s")

    @functools.partial(
        pl.kernel, mesh=mesh,
        out_type=jax.ShapeDtypeStruct((m, n_sel), F32),
        scratch_types=[
            pltpu.VMEM((SC_TOK_CHUNK, n_sel), I32),
            pltpu.VMEM((SC_TOK_CHUNK, dw), I32),
            pltpu.VMEM((SC_TOK_CHUNK, n_sel), F32),
            pltpu.VMEM((SC_RING, SC_LANES, dw), I32),
            pltpu.VMEM((SC_LANES * SC_LANES,), F32),
            pltpu.SemaphoreType.DMA((SC_RING,)),
        ],
        compiler_params=pltpu.CompilerParams(needs_layout_passes=False),
        name="peer_expert_dots",
    )
    def k(h_hbm, idx_hbm, u_hbm, act_hbm, idx_v, h_v, act_v, rows_v, part_v, sems):
        base = _sc_worker_id() * per_w
        lane = lax.broadcasted_iota(I32, (SC_LANES,), 0)

        def compute(tt, g, slot):
            def grp_body(q, accs):
                off = pl.multiple_of(q * step, step)
                xs = [plsc.bitcast(h_v[tt, pl.ds(off + c * SC_LANES, SC_LANES)], BF16)
                      for c in range(SC_BF16_GROUP)]
                new = []
                for e in range(SC_LANES):
                    s = None
                    for c in range(SC_BF16_GROUP):
                        p = plsc.bitcast(
                            rows_v[slot, e, pl.ds(off + c * SC_LANES, SC_LANES)], BF16) * xs[c]
                        s = p if s is None else s + p
                    lo, hi = plsc.unpack(s, format=SC_FMT)
                    new.append(accs[e] + (lo + hi))
                return tuple(new)

            accs = lax.fori_loop(
                0, dw // step, grp_body,
                tuple(jnp.zeros((SC_LANES,), F32) for _ in range(SC_LANES)))
            for e in range(SC_LANES):
                part_v[pl.ds(e * SC_LANES, SC_LANES)] = accs[e]
            tot = jnp.zeros((SC_LANES,), F32)
            for l in range(SC_LANES):
                tot = tot + plsc.load_gather(part_v, [lane * SC_LANES + l])
            act_v[tt, pl.ds(g * SC_LANES, SC_LANES)] = tot

        def chunk_body(c, carry):
            t0 = base + c * SC_TOK_CHUNK
            pltpu.sync_copy(idx_hbm.at[pl.ds(t0, SC_TOK_CHUNK)], idx_v)
            pltpu.sync_copy(h_hbm.at[pl.ds(t0, SC_TOK_CHUNK)], h_v)
            _sc_row_pipeline(idx_v, u_hbm, rows_v, sems, SC_TOK_CHUNK * groups, groups, compute)
            pltpu.sync_copy(act_v, act_hbm.at[pl.ds(t0, SC_TOK_CHUNK)])
            return carry

        lax.fori_loop(0, n_chunks, chunk_body, 0)

    return k(hp, idx, up)


def peer_expert_combine(w, idx, vp):
    m, n_sel = w.shape
    dw = vp.shape[1]
    d = 2 * dw
    per_w = m // SC_WORKERS
    n_chunks = per_w // SC_TOK_CHUNK
    n_vec = d // SC_LANES
    groups = n_sel // SC_LANES
    mesh = plsc.VectorSubcoreMesh(core_axis_name="c", subcore_axis_name="s")

    @functools.partial(
        pl.kernel, mesh=mesh,
        out_type=jax.ShapeDtypeStruct((m, d), F32),
        scratch_types=[
            pltpu.VMEM((SC_TOK_CHUNK, n_sel), I32),
            pltpu.VMEM((SC_TOK_CHUNK, n_sel), F32),
            pltpu.VMEM((SC_TOK_CHUNK, d), F32),
            pltpu.VMEM((SC_RING, SC_LANES, dw), I32),
            pltpu.SemaphoreType.DMA((SC_RING,)),
        ],
        compiler_params=pltpu.CompilerParams(needs_layout_passes=False),
        name="peer_expert_combine",
    )
    def k(w_hbm, idx_hbm, v_hbm, out_hbm, idx_v, w_v, out_v, rows_v, sems):
        base = _sc_worker_id() * per_w

        def compute(tt, g, slot):
            splat = []
            for e in range(SC_LANES):
                s = plsc.load_gather(w_v, [jnp.full((SC_LANES,), tt, I32),
                                           jnp.full((SC_LANES,), g * SC_LANES + e, I32)])
                splat.append(plsc.pack(s, s, format=SC_FMT))

            @plsc.parallel_loop(0, dw // SC_LANES)
            def _(j):
                off = pl.multiple_of(j * SC_LANES, SC_LANES)
                o0 = pl.multiple_of(j * SC_PAIR, SC_PAIR)
                acc_lo = out_v[tt, pl.ds(o0, SC_LANES)]
                acc_hi = out_v[tt, pl.ds(o0 + SC_LANES, SC_LANES)]
                for e0 in range(0, SC_LANES, SC_BF16_GROUP):
                    s = None
                    for e in range(e0, e0 + SC_BF16_GROUP):
                        p = plsc.bitcast(rows_v[slot, e, pl.ds(off, SC_LANES)], BF16) * splat[e]
                        s = p if s is None else s + p
                    lo, hi = plsc.unpack(s, format=SC_FMT)
                    acc_lo = acc_lo + lo
                    acc_hi = acc_hi + hi
                out_v[tt, pl.ds(o0, SC_LANES)] = acc_lo
                out_v[tt, pl.ds(o0 + SC_LANES, SC_LANES)] = acc_hi

        def chunk_body(c, carry):
            t0 = base + c * SC_TOK_CHUNK
            pltpu.sync_copy(idx_hbm.at[pl.ds(t0, SC_TOK_CHUNK)], idx_v)
            pltpu.sync_copy(w_hbm.at[pl.ds(t0, SC_TOK_CHUNK)], w_v)

            def zero_body(z, carry2):
                tt, j = z // n_vec, z % n_vec
                out_v[tt, pl.ds(pl.multiple_of(j * SC_LANES, SC_LANES), SC_LANES)] = (
                    jnp.zeros((SC_LANES,), F32))
                return carry2

            lax.fori_loop(0, SC_TOK_CHUNK * n_vec, zero_body, 0)
            _sc_row_pipeline(idx_v, v_hbm, rows_v, sems, SC_TOK_CHUNK * groups, groups, compute)
            pltpu.sync_copy(out_v, out_hbm.at[pl.ds(t0, SC_TOK_CHUNK)])
            return carry

        lax.fori_loop(0, n_chunks, chunk_body, 0)

    return k(w, idx, vp)


def peer_ffn(x, res, g, wq, subkeys, u, v):
    outs = norm_matmul(x, g, wq.astype(BF16), res=res, want_h=True)
    if res is None:
        q, h = outs
        xs = x
    else:
        q, xs, h = outs
    sk = subkeys.reshape(2 * PEER_HEADS, N_KEYS, PEER_KEY_HALF).astype(BF16)
    idx, gate = peer_topk(q, sk)
    act = peer_expert_dots(pack_bf16_pairs(h), idx, pack_bf16_pairs(u))
    w = gelu_gate(act, gate)
    return xs, peer_expert_combine(w, idx, pack_bf16_pairs(v))


def kernel(x, norm_mix_g, norm_ffn_g, final_norm_g, rel_bias, even_w_in, even_w_out,
           diff_lambda, diff_ln_g, odd_w_in, odd_b_f, odd_w_out, peer_wq, peer_subkeys,
           peer_u, peer_v):
    batch, seq, d = x.shape
    m = batch * seq
    xf = x.reshape(m, d)

    (p,) = norm_matmul(xf, norm_mix_g[0], even_w_in[0].astype(BF16))
    branches = []
    for window, dil in DILATED_BRANCHES:
        tile = dilated_bias_tile(rel_bias[:, :A_HEADS], window, dil)
        branches.append(dilated_branch(p, tile, dil, batch, seq))
    lam_init = 0.8 - 0.6 * math.exp(-0.3 * 0)
    ob = diff_attention(p.reshape(batch, seq, -1), diff_bias_tiles(rel_bias[:, A_HEADS:], seq),
                        diff_lambda[0], diff_ln_g[0], lam_init, batch, seq)
    xf = even_out_proj(branches, ob.reshape(m, B_V_WIDTH), even_w_out[0].astype(BF16), xf)
    xf, peer0 = peer_ffn(xf, None, norm_ffn_g[0], peer_wq[0], peer_subkeys[0],
                         peer_u[0], peer_v[0])

    w_in = odd_w_in[0]
    p, xf, h = norm_matmul(xf, norm_mix_g[1], w_in[:, :3 * C_WIDTH].astype(BF16),
                           res=peer0, want_h=True)
    w_gate = jnp.pad(w_in[:, 3 * C_WIDTH:], ((0, 0), (0, LANES - C_HEADS)))
    fg = gate_matmul(h, w_gate)
    b_f = jnp.pad(odd_b_f[0], (0, LANES - C_HEADS)).reshape(1, LANES)
    c = logsig_cumsum(fg.reshape(batch, seq, LANES), b_f, batch, seq)
    ct = c[:, :, :C_HEADS].transpose(0, 2, 1)
    o = fox_attention(p.reshape(batch, seq, -1), ct[:, :, :, None], ct[:, :, None, :],
                      batch, seq)
    xf = matmul_residual(o.reshape(m, C_WIDTH), odd_w_out[0].astype(BF16), xf)
    xf, peer1 = peer_ffn(xf, None, norm_ffn_g[1], peer_wq[1], peer_subkeys[1],
                         peer_u[1], peer_v[1])

    return add_norm(xf, peer1, final_norm_g).reshape(batch, seq, d)


def _gate_matmul_body(h_ref, w_ref, o_ref):
    o_ref[...] = jnp.dot(h_ref[...], w_ref[...], preferred_element_type=F32,
                         precision=lax.Precision.HIGHEST)


def gate_matmul(h, w, tm=512):
    m, d = h.shape
    n = w.shape[1]
    return pl.pallas_call(
        _gate_matmul_body,
        grid=(m // tm,),
        in_specs=[pl.BlockSpec((tm, d), lambda i: (i, 0)),
                  pl.BlockSpec((d, n), lambda i: (0, 0))],
        out_specs=pl.BlockSpec((tm, n), lambda i: (i, 0)),
        out_shape=jax.ShapeDtypeStruct((m, n), F32),
        compiler_params=pltpu.CompilerParams(dimension_semantics=("parallel",)),
        name="gate_matmul",
    )(h, w)
```

```python
import functools
import math

import numpy as np
import jax
import jax.numpy as jnp
from jax import lax
from jax.experimental import pallas as pl
from jax.experimental.pallas import tpu as pltpu
from jax.experimental.pallas import tpu_sc as plsc

F32 = jnp.float32
BF16 = jnp.bfloat16
I32 = jnp.int32

D_MODEL = 1024
HEAD_DIM = 64
A_HEADS = 8
DILATED_BRANCHES = ((128, 1), (512, 4), (2048, 16))
DIFF_HALF = 64
DIFF_VDIM = 128
B_HEADS = 4
C_HEADS = 16
N_BUCKETS = 32
MAX_DISTANCE = 2048
PEER_HEADS = 8
N_KEYS = 128
PEER_TOPK = 16
PEER_KEY_HALF = 128
RMS_EPS = 1e-6
NEG_INF = -1e30
A_WIDTH = A_HEADS * HEAD_DIM
B_QK_WIDTH = B_HEADS * 2 * DIFF_HALF
B_V_WIDTH = B_HEADS * DIFF_VDIM
C_WIDTH = C_HEADS * HEAD_DIM
QK_SCALE = 0.125
LANES = 128
DIL_BLOCK = 128
ATT_BLOCK = 512

SC_CORES = 2
SC_SUBCORES = 16
SC_LANES = 16
SC_WORKERS = SC_CORES * SC_SUBCORES


def _t5_bucket_table(n):
    max_exact = N_BUCKETS // 2
    d = np.arange(n)
    df = np.maximum(d, 1).astype(np.float32)
    large = max_exact + (
        np.log(df / np.float32(max_exact)) / np.float32(math.log(MAX_DISTANCE / max_exact))
        * np.float32(N_BUCKETS - max_exact)).astype(np.int32)
    large = np.minimum(large, N_BUCKETS - 1)
    return np.where(d < max_exact, d, large).astype(np.int32)


def _norm_matmul_body(*refs, has_res, want_h):
    it = iter(refs)
    x_ref = next(it)
    r_ref = next(it) if has_res else None
    g_ref = next(it)
    w_ref = next(it)
    o_ref = next(it)
    xs_ref = next(it) if has_res else None
    hout_ref = next(it) if want_h else None
    h_scr = next(it)

    @pl.when(pl.program_id(1) == 0)
    def _():
        x = x_ref[...]
        if has_res:
            x = x + r_ref[...]
            xs_ref[...] = x
        ms = jnp.mean(x * x, axis=-1, keepdims=True)
        h = x * lax.rsqrt(ms + RMS_EPS) * g_ref[...]
        if want_h:
            hout_ref[...] = h
        h_scr[...] = h.astype(BF16)

    o_ref[...] = jnp.dot(h_scr[...], w_ref[...],
                         preferred_element_type=F32).astype(o_ref.dtype)


def norm_matmul(x, g, w, *, res=None, want_h=False, out_dtype=BF16, tm=512, tn=512):
    m, d = x.shape
    n = w.shape[1]
    tn = min(tn, n)
    row = pl.BlockSpec((tm, d), lambda i, j: (i, 0))
    in_specs = [row] + ([row] if res is not None else []) + [
        pl.BlockSpec((1, d), lambda i, j: (0, 0)),
        pl.BlockSpec((d, tn), lambda i, j: (0, j))]
    out_specs = [pl.BlockSpec((tm, tn), lambda i, j: (i, j))]
    out_shape = [jax.ShapeDtypeStruct((m, n), out_dtype)]
    if res is not None:
        out_specs.append(row)
        out_shape.append(jax.ShapeDtypeStruct((m, d), F32))
    if want_h:
        out_specs.append(row)
        out_shape.append(jax.ShapeDtypeStruct((m, d), F32))
    args = [x] + ([res] if res is not None else []) + [g.reshape(1, d), w]
    return pl.pallas_call(
        functools.partial(_norm_matmul_body, has_res=res is not None, want_h=want_h),
        grid=(m // tm, n // tn),
        in_specs=in_specs,
        out_specs=out_specs,
        out_shape=out_shape,
        scratch_shapes=[pltpu.VMEM((tm, d), BF16)],
        compiler_params=pltpu.CompilerParams(
            dimension_semantics=("parallel", "arbitrary")),
        name="norm_matmul",
    )(*args)


def _dilated_body(q_ref, kp_ref, kc_ref, vp_ref, vc_ref, b_ref, o_ref, lse_ref):
    i = pl.program_id(3)
    q = q_ref[...] * QK_SCALE
    k = jnp.concatenate([kp_ref[...], kc_ref[...]], axis=0)
    v = jnp.concatenate([vp_ref[...], vc_ref[...]], axis=0)
    col = lax.broadcasted_iota(I32, (DIL_BLOCK, 2 * DIL_BLOCK), 1)
    has_prev = jnp.logical_or(col >= DIL_BLOCK, i > 0)
    outs, lses = [], []
    for hh in range(2):
        sl = slice(hh * HEAD_DIM, (hh + 1) * HEAD_DIM)
        s = lax.dot_general(q[:, sl], k[:, sl], (((1,), (1,)), ((), ())),
                            preferred_element_type=F32)
        s = jnp.where(has_prev, s + b_ref[hh], NEG_INF)
        m = jnp.max(s, axis=-1, keepdims=True)
        p = jnp.exp(s - m)
        l = jnp.sum(p, axis=-1, keepdims=True)
        o = jnp.dot(p.astype(BF16), v[:, sl], preferred_element_type=F32) / l
        outs.append(o)
        lses.append(jnp.broadcast_to(m + jnp.log(l), (DIL_BLOCK, HEAD_DIM)))
    o_ref[...] = jnp.concatenate(outs, axis=1)
    lse_ref[...] = jnp.concatenate(lses, axis=1)


def dilated_branch(p, bias_tile, dil, batch, seq):
    n_cols = p.shape[1]
    cb = n_cols // LANES
    rows = seq // dil
    nblk = rows // DIL_BLOCK
    pv = p.reshape(batch, rows, dil * n_cols)
    kq, kk, kv = 0, A_WIDTH // LANES, 2 * A_WIDTH // LANES
    blk = (None, DIL_BLOCK, LANES)

    def spec(col0, prev):
        if prev:
            return pl.BlockSpec(blk, lambda b, h, r, i: (b, jnp.maximum(i - 1, 0), r * cb + col0 + h))
        return pl.BlockSpec(blk, lambda b, h, r, i: (b, i, r * cb + col0 + h))

    ocb = A_WIDTH // LANES
    ospec = pl.BlockSpec(blk, lambda b, h, r, i: (b, i, r * ocb + h))
    oshape = jax.ShapeDtypeStruct((batch, rows, dil * A_WIDTH), F32)
    o, lse = pl.pallas_call(
        _dilated_body,
        grid=(batch, A_HEADS // 2, dil, nblk),
        in_specs=[spec(kq, False), spec(kk, True), spec(kk, False),
                  spec(kv, True), spec(kv, False),
                  pl.BlockSpec((2, DIL_BLOCK, 2 * DIL_BLOCK), lambda b, h, r, i: (h, 0, 0))],
        out_specs=[ospec, ospec],
        out_shape=[oshape, oshape],
        compiler_params=pltpu.CompilerParams(
            dimension_semantics=("parallel", "parallel", "parallel", "arbitrary")),
        name=f"dilated_d{dil}",
    )(pv, pv, pv, pv, pv, bias_tile)
    return o.reshape(batch * seq, A_WIDTH), lse.reshape(batch * seq, A_WIDTH)


def dilated_bias_tile(rel_bias_a, window, dil):
    n = window // dil
    assert n == DIL_BLOCK
    bucket = _t5_bucket_table(window + 1)
    row = np.arange(DIL_BLOCK)[:, None]
    c = np.arange(2 * DIL_BLOCK)[None, :]
    j = DIL_BLOCK + row - c
    valid = (j >= 0) & (j <= n)
    bk = bucket[np.clip(j, 0, n) * dil]
    tile = rel_bias_a.T[:, bk]
    return jnp.where(jnp.asarray(valid)[None], tile, NEG_INF).astype(F32)


def _diff_body(q_ref, k_ref, v_ref, b_ref, lam_ref, g_ref, o_ref,
               m_scr, l_scr, acc_scr, *, n_tiles, lam_init):
    t = ATT_BLOCK
    qi = pl.program_id(2)
    q = q_ref[...] * QK_SCALE
    lane = lax.broadcasted_iota(I32, (1, LANES), 1)
    qa = [jnp.where((lane < DIFF_HALF) == (a == 0), q, jnp.zeros_like(q)) for a in range(2)]
    ones = jnp.ones((t, LANES), BF16)
    for a in range(2):
        m_scr[a] = jnp.full((t, LANES), NEG_INF, F32)
        l_scr[a] = jnp.zeros((t, LANES), F32)
        acc_scr[a] = jnp.zeros((t, DIFF_VDIM), F32)

    def step(j, masked):
        off = pl.multiple_of(j * t, t)
        ks = k_ref[pl.ds(off, t), :]
        vs = v_ref[pl.ds(off, t), :]
        bias = b_ref[jnp.minimum(qi - j, n_tiles - 1)]
        if masked:
            row = lax.broadcasted_iota(I32, (t, t), 0)
            col = lax.broadcasted_iota(I32, (t, t), 1)
            causal = row >= col
        for a in range(2):
            s = lax.dot_general(qa[a], ks, (((1,), (1,)), ((), ())),
                                preferred_element_type=F32) + bias
            if masked:
                s = jnp.where(causal, s, NEG_INF)
            m_prev = m_scr[a]
            m_new = jnp.maximum(m_prev, jnp.max(s, axis=-1, keepdims=True))
            alpha = jnp.exp(m_prev - m_new)
            pb = jnp.exp(s - jnp.concatenate([m_new] * (t // LANES), axis=1)).astype(BF16)
            l_scr[a] = alpha * l_scr[a] + jnp.dot(pb, ones, preferred_element_type=F32)
            acc_scr[a] = alpha * acc_scr[a] + jnp.dot(pb, vs, preferred_element_type=F32)
            m_scr[a] = m_new

    def loop_body(j, carry):
        step(j, False)
        return carry

    lax.fori_loop(0, qi, loop_body, 0)
    step(qi, True)

    lp = lam_ref[...]
    lam = (jnp.exp(jnp.sum(lp[0:1] * lp[1:2])) - jnp.exp(jnp.sum(lp[2:3] * lp[3:4]))
           + lam_init)
    o = acc_scr[0] / l_scr[0] - lam * (acc_scr[1] / l_scr[1])
    ms = jnp.mean(o * o, axis=-1, keepdims=True)
    y = o * lax.rsqrt(ms + RMS_EPS) * g_ref[...]
    o_ref[...] = (y * (1.0 - lam_init)).astype(o_ref.dtype)


def diff_attention(p, bias_tiles, lam_params, ln_g, lam_init, batch, seq):
    t = ATT_BLOCK
    n_tiles = bias_tiles.shape[1]
    cq = 3 * A_WIDTH // LANES
    ck = cq + B_QK_WIDTH // LANES
    cv = ck + B_QK_WIDTH // LANES
    return pl.pallas_call(
        functools.partial(_diff_body, n_tiles=n_tiles, lam_init=lam_init),
        grid=(batch, B_HEADS, seq // t),
        in_specs=[
            pl.BlockSpec((None, t, LANES), lambda b, h, i: (b, i, cq + h)),
            pl.BlockSpec((None, seq, LANES), lambda b, h, i: (b, 0, ck + h)),
            pl.BlockSpec((None, seq, LANES), lambda b, h, i: (b, 0, cv + h)),
            pl.BlockSpec((None, n_tiles, t, t), lambda b, h, i: (h, 0, 0, 0)),
            pl.BlockSpec((4, DIFF_HALF), lambda b, h, i: (0, 0)),
            pl.BlockSpec((1, DIFF_VDIM), lambda b, h, i: (0, 0)),
        ],
        out_specs=pl.BlockSpec((None, t, LANES), lambda b, h, i: (b, i, h)),
        out_shape=jax.ShapeDtypeStruct((batch, seq, B_V_WIDTH), BF16),
        scratch_shapes=[pltpu.VMEM((2, t, LANES), F32), pltpu.VMEM((2, t, LANES), F32),
                        pltpu.VMEM((2, t, DIFF_VDIM), F32)],
        compiler_params=pltpu.CompilerParams(
            dimension_semantics=("parallel", "parallel", "arbitrary")),
        name="diff_attention",
    )(p, p, p, bias_tiles, lam_params, ln_g.reshape(1, DIFF_VDIM))


def diff_bias_tiles(rel_bias_b, seq):
    t = ATT_BLOCK
    bucket = _t5_bucket_table(max(seq, 2 * MAX_DISTANCE) + 2 * t)
    sat = bucket[-1]
    d_sat = int(np.max(np.nonzero(bucket != sat)[0])) + 1
    n_full = (d_sat + t - 1 + t - 1) // t
    n_tiles = n_full + 1
    assert n_full * t - (t - 1) >= d_sat
    n = np.arange(2 * t)[None, :]
    base = np.arange(n_tiles)[:, None] * t
    dist = np.clip(np.where(n < t, base - n, base + 2 * t - n), 0, None)
    w = rel_bias_b.T[:, bucket[dist]].astype(F32)
    rep = jnp.broadcast_to(w[:, :, None, :], (B_HEADS, n_tiles, t, 2 * t))
    flat = rep.reshape(B_HEADS, n_tiles, 2 * t * t)[:, :, :t * (2 * t - 1)]
    return flat.reshape(B_HEADS, n_tiles, t, 2 * t - 1)[:, :, :, :t]


def _logsig_cumsum_body(f_ref, b_ref, c_ref, carry_scr):
    t = f_ref.shape[0]

    @pl.when(pl.program_id(1) == 0)
    def _():
        carry_scr[...] = jnp.zeros_like(carry_scr)

    x = f_ref[...] + b_ref[...]
    ls = jnp.minimum(x, 0.0) - jnp.log1p(jnp.exp(-jnp.abs(x)))
    row = lax.broadcasted_iota(I32, (t, t), 0)
    col = lax.broadcasted_iota(I32, (t, t), 1)
    tri = (row >= col).astype(F32)
    c = jnp.dot(tri, ls, preferred_element_type=F32,
                precision=lax.Precision.HIGHEST) + carry_scr[...]
    c_ref[...] = c
    carry_scr[...] = c[t - 1:t, :]


def logsig_cumsum(fg, b_f, batch, seq, t=512):
    return pl.pallas_call(
        _logsig_cumsum_body,
        grid=(batch, seq // t),
        in_specs=[pl.BlockSpec((None, t, LANES), lambda b, i: (b, i, 0)),
                  pl.BlockSpec((1, LANES), lambda b, i: (0, 0))],
        out_specs=pl.BlockSpec((None, t, LANES), lambda b, i: (b, i, 0)),
        out_shape=jax.ShapeDtypeStruct((batch, seq, LANES), F32),
        scratch_shapes=[pltpu.VMEM((1, LANES), F32)],
        compiler_params=pltpu.CompilerParams(
            dimension_semantics=("parallel", "arbitrary")),
        name="logsig_cumsum",
    )(fg, b_f)


def _fox_body(q_ref, k_ref, v_ref, cq_ref, ck_ref, o_ref, m_scr, acc_scr):
    t = ATT_BLOCK
    qi = pl.program_id(2)
    q = q_ref[...] * QK_SCALE
    lane = lax.broadcasted_iota(I32, (1, LANES), 1)
    own = [(lane < HEAD_DIM) == (a == 0) for a in range(2)]
    qa = [jnp.where(own[a], q, jnp.zeros_like(q)) for a in range(2)]
    cqb = [jnp.broadcast_to(cq_ref[a], (t, LANES)) for a in range(2)]
    for a in range(2):
        m_scr[a] = jnp.full((t, LANES), NEG_INF, F32)
        acc_scr[a] = jnp.zeros((t, LANES), F32)

    def step(j, masked):
        off = pl.multiple_of(j * t, t)
        ks = k_ref[pl.ds(off, t), :]
        vs = v_ref[pl.ds(off, t), :]
        if masked:
            row = lax.broadcasted_iota(I32, (t, t), 0)
            col = lax.broadcasted_iota(I32, (t, t), 1)
            causal = row >= col
        for a in range(2):
            s = lax.dot_general(qa[a], ks, (((1,), (1,)), ((), ())),
                                preferred_element_type=F32)
            s = s - ck_ref[a, :, pl.ds(off, t)]
            if masked:
                s = jnp.where(causal, s, NEG_INF)
            m_prev = m_scr[a]
            m_new = jnp.maximum(m_prev, jnp.max(s, axis=-1, keepdims=True) + cqb[a])
            alpha = jnp.exp(m_prev - m_new)
            shift = m_new - cqb[a]
            pb = jnp.exp(s - jnp.concatenate([shift] * (t // LANES), axis=1)).astype(BF16)
            v_aug = jnp.where(own[a], vs, jnp.ones_like(vs))
            acc_scr[a] = alpha * acc_scr[a] + jnp.dot(pb, v_aug, preferred_element_type=F32)
            m_scr[a] = m_new

    def loop_body(j, carry):
        step(j, False)
        return carry

    lax.fori_loop(0, qi, loop_body, 0)
    step(qi, True)
    r = [acc_scr[a] / pltpu.roll(acc_scr[a], HEAD_DIM, 1) for a in range(2)]
    o_ref[...] = jnp.where(own[0], r[0], r[1]).astype(o_ref.dtype)


def fox_attention(p, cq, ck, batch, seq):
    t = ATT_BLOCK
    nk = C_WIDTH // LANES
    return pl.pallas_call(
        _fox_body,
        grid=(batch, C_HEADS // 2, seq // t),
        in_specs=[
            pl.BlockSpec((None, t, LANES), lambda b, h, i: (b, i, h)),
            pl.BlockSpec((None, seq, LANES), lambda b, h, i: (b, 0, nk + h)),
            pl.BlockSpec((None, seq, LANES), lambda b, h, i: (b, 0, 2 * nk + h)),
            pl.BlockSpec((None, 2, t, 1), lambda b, h, i: (b, h, i, 0)),
            pl.BlockSpec((None, 2, 1, seq), lambda b, h, i: (b, h, 0, 0)),
        ],
        out_specs=pl.BlockSpec((None, t, LANES), lambda b, h, i: (b, i, h)),
        out_shape=jax.ShapeDtypeStruct((batch, seq, C_WIDTH), BF16),
        scratch_shapes=[pltpu.VMEM((2, t, LANES), F32), pltpu.VMEM((2, t, LANES), F32)],
        compiler_params=pltpu.CompilerParams(
            dimension_semantics=("parallel", "parallel", "arbitrary")),
        name="fox_attention",
    )(p, p, p, cq, ck)


def _even_out_body(o1, l1, o2, l2, o3, l3, ob_ref, w_ref, x_ref, out_ref, a_scr):
    @pl.when(pl.program_id(1) == 0)
    def _():
        a1, a2, a3 = l1[...], l2[...], l3[...]
        mx = jnp.maximum(jnp.maximum(a1, a2), a3)
        e1, e2, e3 = jnp.exp(a1 - mx), jnp.exp(a2 - mx), jnp.exp(a3 - mx)
        oa = (e1 * o1[...] + e2 * o2[...] + e3 * o3[...]) / (e1 + e2 + e3)
        a_scr[:, :A_WIDTH] = oa.astype(BF16)
        a_scr[:, A_WIDTH:] = ob_ref[...]

    out_ref[...] = x_ref[...] + jnp.dot(a_scr[...], w_ref[...], preferred_element_type=F32)


def even_out_proj(branches, ob, w, x, tm=512, tn=512):
    m, d = x.shape
    half = pl.BlockSpec((tm, A_WIDTH), lambda i, j: (i, 0))
    flat = [a for pair in branches for a in pair]
    return pl.pallas_call(
        _even_out_body,
        grid=(m // tm, d // tn),
        in_specs=[half] * 6 + [
            pl.BlockSpec((tm, B_V_WIDTH), lambda i, j: (i, 0)),
            pl.BlockSpec((A_WIDTH + B_V_WIDTH, tn), lambda i, j: (0, j)),
            pl.BlockSpec((tm, tn), lambda i, j: (i, j))],
        out_specs=pl.BlockSpec((tm, tn), lambda i, j: (i, j)),
        out_shape=jax.ShapeDtypeStruct((m, d), F32),
        scratch_shapes=[pltpu.VMEM((tm, A_WIDTH + B_V_WIDTH), BF16)],
        compiler_params=pltpu.CompilerParams(
            dimension_semantics=("parallel", "arbitrary")),
        name="even_out_proj",
    )(*flat, ob, w, x)


def _matmul_res_body(a_ref, w_ref, x_ref, o_ref):
    o_ref[...] = x_ref[...] + jnp.dot(a_ref[...], w_ref[...], preferred_element_type=F32)


def matmul_residual(a, w, x, tm=512, tn=512):
    m, k = a.shape
    n = w.shape[1]
    return pl.pallas_call(
        _matmul_res_body,
        grid=(m // tm, n // tn),
        in_specs=[pl.BlockSpec((tm, k), lambda i, j: (i, 0)),
                  pl.BlockSpec((k, tn), lambda i, j: (0, j)),
                  pl.BlockSpec((tm, tn), lambda i, j: (i, j))],
        out_specs=pl.BlockSpec((tm, tn), lambda i, j: (i, j)),
        out_shape=jax.ShapeDtypeStruct((m, n), F32),
        compiler_params=pltpu.CompilerParams(
            dimension_semantics=("parallel", "parallel")),
        name="matmul_residual",
    )(a, w, x)


def _add_norm_body(x_ref, r_ref, g_ref, o_ref):
    x = x_ref[...] + r_ref[...]
    ms = jnp.mean(x * x, axis=-1, keepdims=True)
    o_ref[...] = x * lax.rsqrt(ms + RMS_EPS) * g_ref[...]


def add_norm(x, r, g, tm=512):
    m, d = x.shape
    row = pl.BlockSpec((tm, d), lambda i: (i, 0))
    return pl.pallas_call(
        _add_norm_body,
        grid=(m // tm,),
        in_specs=[row, row, pl.BlockSpec((1, d), lambda i: (0, 0))],
        out_specs=row,
        out_shape=jax.ShapeDtypeStruct((m, d), F32),
        compiler_params=pltpu.CompilerParams(dimension_semantics=("parallel",)),
        name="add_norm",
    )(x, r, g.reshape(1, d))


PEER_CAND_ROWS = PEER_TOPK + 8 * (PEER_TOPK - 1)


def _peer_topk_body(q_ref, sk_ref, idx_ref, gate_ref, ts_scr, ti_scr, bs_scr, be_scr,
                    gt_scr, it_scr):
    tm = q_ref.shape[0]
    neg_inf = jnp.float32(-jnp.inf)
    key_id = lax.broadcasted_iota(I32, (N_KEYS, tm), 0)

    def pair_body(pr, carry):
        off = pl.multiple_of(pr * PEER_KEY_HALF, PEER_KEY_HALF)
        sc = lax.dot_general(sk_ref[pr], q_ref[:, pl.ds(off, PEER_KEY_HALF)],
                             (((1,), (1,)), ((), ())), preferred_element_type=F32)

        def k_body(k, vals):
            m = jnp.max(vals, axis=0, keepdims=True)
            sel = jnp.min(jnp.where(vals == m, key_id, N_KEYS), axis=0, keepdims=True)
            ts_scr[pr, pl.ds(k, 1), :] = m
            ti_scr[pr, pl.ds(k, 1), :] = sel
            return jnp.where(key_id == sel, neg_inf, vals)

        lax.fori_loop(0, PEER_TOPK, k_body, sc)
        return carry

    lax.fori_loop(0, 2 * PEER_HEADS, pair_body, 0)

    r = lax.broadcasted_iota(I32, (PEER_CAND_ROWS, 1), 0)
    cand_id = jnp.where(r < PEER_TOPK, r,
                        (1 + (r - PEER_TOPK) // 8) * PEER_TOPK + (r - PEER_TOPK) % 8)

    def head_body(h, carry):
        s1, s2 = ts_scr[2 * h], ts_scr[2 * h + 1]
        i1, i2 = ti_scr[2 * h] * N_KEYS, ti_scr[2 * h + 1]
        vals = jnp.concatenate(
            [s1[0:1] + s2] + [s1[a:a + 1] + s2[0:8] for a in range(1, PEER_TOPK)], axis=0)
        eidx = jnp.concatenate(
            [i1[0:1] + i2] + [i1[a:a + 1] + i2[0:8] for a in range(1, PEER_TOPK)], axis=0)

        def k_body(k, vals):
            m = jnp.max(vals, axis=0, keepdims=True)
            sel = jnp.min(jnp.where(vals == m, cand_id, PEER_TOPK * PEER_TOPK),
                          axis=0, keepdims=True)
            hit = cand_id == sel
            bs_scr[pl.ds(k, 1), :] = m
            be_scr[pl.ds(k, 1), :] = jnp.sum(jnp.where(hit, eidx, 0), axis=0, keepdims=True)
            return jnp.where(hit, neg_inf, vals)

        lax.fori_loop(0, PEER_TOPK, k_body, vals)
        bs = bs_scr[...]
        e = jnp.exp(bs - jnp.max(bs, axis=0, keepdims=True))
        row0 = pl.multiple_of(h * PEER_TOPK, PEER_TOPK)
        gt_scr[pl.ds(row0, PEER_TOPK), :] = e / jnp.sum(e, axis=0, keepdims=True)
        it_scr[pl.ds(row0, PEER_TOPK), :] = be_scr[...]
        return carry

    lax.fori_loop(0, PEER_HEADS, head_body, 0)
    gate_ref[...] = gt_scr[...].T
    idx_ref[...] = it_scr[...].T


def peer_topk(q, subkeys, tm=256):
    m = q.shape[0]
    n_sel = PEER_HEADS * PEER_TOPK
    out_spec = pl.BlockSpec((tm, n_sel), lambda i: (i, 0))
    return pl.pallas_call(
        _peer_topk_body,
        grid=(m // tm,),
        in_specs=[pl.BlockSpec((tm, q.shape[1]), lambda i: (i, 0)),
                  pl.BlockSpec(subkeys.shape, lambda i: (0, 0, 0))],
        out_specs=[out_spec, out_spec],
        out_shape=[jax.ShapeDtypeStruct((m, n_sel), I32),
                   jax.ShapeDtypeStruct((m, n_sel), F32)],
        scratch_shapes=[pltpu.VMEM((2 * PEER_HEADS, PEER_TOPK, tm), F32),
                        pltpu.VMEM((2 * PEER_HEADS, PEER_TOPK, tm), I32),
                        pltpu.VMEM((PEER_TOPK, tm), F32),
                        pltpu.VMEM((PEER_TOPK, tm), I32),
                        pltpu.VMEM((n_sel, tm), F32),
                        pltpu.VMEM((n_sel, tm), I32)],
        compiler_params=pltpu.CompilerParams(dimension_semantics=("parallel",)),
        name="peer_topk",
    )(q, subkeys)


def _gelu_gate_body(a_ref, g_ref, o_ref):
    a = a_ref[...]
    o_ref[...] = g_ref[...] * (0.5 * a * (1.0 + lax.erf(a * (2.0 ** -0.5))))


def gelu_gate(act, gate, tm=2048):
    m, n = act.shape
    spec = pl.BlockSpec((tm, n), lambda i: (i, 0))
    return pl.pallas_call(
        _gelu_gate_body,
        grid=(m // tm,),
        in_specs=[spec, spec],
        out_specs=spec,
        out_shape=jax.ShapeDtypeStruct((m, n), F32),
        compiler_params=pltpu.CompilerParams(dimension_semantics=("parallel",)),
        name="gelu_gate",
    )(act, gate)


SC_TOK_CHUNK = 32
SC_RING = 8
SC_PAIR = 2 * SC_LANES
SC_BF16_GROUP = 4
SC_FMT = plsc.PackFormat.INTERLEAVED


def _sc_worker_id():
    return lax.axis_index("s") * SC_CORES + lax.axis_index("c")


def pack_bf16_pairs(t):
    lead, d = t.shape[:-1], t.shape[-1]
    tb = t.astype(BF16).reshape(lead + (d // SC_PAIR, 2, SC_LANES))
    tb = jnp.swapaxes(tb, -1, -2)
    return lax.bitcast_convert_type(tb, I32).reshape(lead + (d // 2,))


def _sc_row_pipeline(idx_v, table_hbm, rows_v, sems, n_items, groups, compute):
    def gather(item):
        tt, g = item // groups, item % groups
        ids = idx_v[tt, pl.ds(g * SC_LANES, SC_LANES)]
        slot = item % SC_RING
        return pltpu.make_async_copy(table_hbm.at[ids], rows_v.at[slot], sems.at[slot])

    for s in range(SC_RING - 1):
        gather(s).start()

    def item_body(item, carry):
        nxt = item + SC_RING - 1

        @pl.when(nxt < n_items)
        def _():
            gather(nxt).start()

        gather(item).wait()
        compute(item // groups, item % groups, item % SC_RING)
        return carry

    lax.fori_loop(0, n_items, item_body, 0)


def peer_expert_dots(hp, idx, up):
    m, dw = hp.shape
    n_sel = idx.shape[1]
    per_w = m // SC_WORKERS
    n_chunks = per_w // SC_TOK_CHUNK
    groups = n_sel // SC_LANES
    step = SC_BF16_GROUP * SC_LANES
    mesh = plsc.VectorSubcoreMesh(core_axis_name="c", subcore_axis_name="s")

    @functools.partial(
        pl.kernel, mesh=mesh,
        out_type=jax.ShapeDtypeStruct((m, n_sel), F32),
        scratch_types=[
            pltpu.VMEM((SC_TOK_CHUNK, n_sel), I32),
            pltpu.VMEM((SC_TOK_CHUNK, dw), I32),
            pltpu.VMEM((SC_TOK_CHUNK, n_sel), F32),
            pltpu.VMEM((SC_RING, SC_LANES, dw), I32),
            pltpu.VMEM((SC_LANES * SC_LANES,), F32),
            pltpu.SemaphoreType.DMA((SC_RING,)),
        ],
        compiler_params=pltpu.CompilerParams(needs_layout_passes=False),
        name="peer_expert_dots",
    )
    def k(h_hbm, idx_hbm, u_hbm, act_hbm, idx_v, h_v, act_v, rows_v, part_v, sems):
        base = _sc_worker_id() * per_w
        lane = lax.broadcasted_iota(I32, (SC_LANES,), 0)

        def compute(tt, g, slot):
            def grp_body(q, accs):
                off = pl.multiple_of(q * step, step)
                xs = [plsc.bitcast(h_v[tt, pl.ds(off + c * SC_LANES, SC_LANES)], BF16)
                      for c in range(SC_BF16_GROUP)]
                new = []
                for e in range(SC_LANES):
                    s = None
                    for c in range(SC_BF16_GROUP):
                        p = plsc.bitcast(
                            rows_v[slot, e, pl.ds(off + c * SC_LANES, SC_LANES)], BF16) * xs[c]
                        s = p if s is None else s + p
                    lo, hi = plsc.unpack(s, format=SC_FMT)
                    new.append(accs[e] + (lo + hi))
                return tuple(new)

            accs = lax.fori_loop(
                0, dw // step, grp_body,
                tuple(jnp.zeros((SC_LANES,), F32) for _ in range(SC_LANES)))
            for e in range(SC_LANES):
                part_v[pl.ds(e * SC_LANES, SC_LANES)] = accs[e]
            tot = jnp.zeros((SC_LANES,), F32)
            for l in range(SC_LANES):
                tot = tot + plsc.load_gather(part_v, [lane * SC_LANES + l])
            act_v[tt, pl.ds(g * SC_LANES, SC_LANES)] = tot

        def chunk_body(c, carry):
            t0 = base + c * SC_TOK_CHUNK
            pltpu.sync_copy(idx_hbm.at[pl.ds(t0, SC_TOK_CHUNK)], idx_v)
            pltpu.sync_copy(h_hbm.at[pl.ds(t0, SC_TOK_CHUNK)], h_v)
            _sc_row_pipeline(idx_v, u_hbm, rows_v, sems, SC_TOK_CHUNK * groups, groups, compute)
            pltpu.sync_copy(act_v, act_hbm.at[pl.ds(t0, SC_TOK_CHUNK)])
            return carry

        lax.fori_loop(0, n_chunks, chunk_body, 0)

    return k(hp, idx, up)


def peer_expert_combine(w, idx, vp):
    m, n_sel = w.shape
    dw = vp.shape[1]
    d = 2 * dw
    per_w = m // SC_WORKERS
    n_chunks = per_w // SC_TOK_CHUNK
    n_vec = d // SC_LANES
    groups = n_sel // SC_LANES
    mesh = plsc.VectorSubcoreMesh(core_axis_name="c", subcore_axis_name="s")

    @functools.partial(
        pl.kernel, mesh=mesh,
        out_type=jax.ShapeDtypeStruct((m, d), F32),
        scratch_types=[
            pltpu.VMEM((SC_TOK_CHUNK, n_sel), I32),
            pltpu.VMEM((SC_TOK_CHUNK, n_sel), F32),
            pltpu.VMEM((SC_TOK_CHUNK, d), F32),
            pltpu.VMEM((SC_RING, SC_LANES, dw), I32),
            pltpu.SemaphoreType.DMA((SC_RING,)),
        ],
        compiler_params=pltpu.CompilerParams(needs_layout_passes=False),
        name="peer_expert_combine",
    )
    def k(w_hbm, idx_hbm, v_hbm, out_hbm, idx_v, w_v, out_v, rows_v, sems):
        base = _sc_worker_id() * per_w

        def compute(tt, g, slot):
            splat = []
            for e in range(SC_LANES):
                s = plsc.load_gather(w_v, [jnp.full((SC_LANES,), tt, I32),
                                           jnp.full((SC_LANES,), g * SC_LANES + e, I32)])
                splat.append(plsc.pack(s, s, format=SC_FMT))

            @plsc.parallel_loop(0, dw // SC_LANES)
            def _(j):
                off = pl.multiple_of(j * SC_LANES, SC_LANES)
                o0 = pl.multiple_of(j * SC_PAIR, SC_PAIR)
                acc_lo = out_v[tt, pl.ds(o0, SC_LANES)]
                acc_hi = out_v[tt, pl.ds(o0 + SC_LANES, SC_LANES)]
                for e0 in range(0, SC_LANES, SC_BF16_GROUP):
                    s = None
                    for e in range(e0, e0 + SC_BF16_GROUP):
                        p = plsc.bitcast(rows_v[slot, e, pl.ds(off, SC_LANES)], BF16) * splat[e]
                        s = p if s is None else s + p
                    lo, hi = plsc.unpack(s, format=SC_FMT)
                    acc_lo = acc_lo + lo
                    acc_hi = acc_hi + hi
                out_v[tt, pl.ds(o0, SC_LANES)] = acc_lo
                out_v[tt, pl.ds(o0 + SC_LANES, SC_LANES)] = acc_hi

        def chunk_body(c, carry):
            t0 = base + c * SC_TOK_CHUNK
            pltpu.sync_copy(idx_hbm.at[pl.ds(t0, SC_TOK_CHUNK)], idx_v)
            pltpu.sync_copy(w_hbm.at[pl.ds(t0, SC_TOK_CHUNK)], w_v)

            def zero_body(z, carry2):
                tt, j = z // n_vec, z % n_vec
                out_v[tt, pl.ds(pl.multiple_of(j * SC_LANES, SC_LANES), SC_LANES)] = (
                    jnp.zeros((SC_LANES,), F32))
                return carry2

            lax.fori_loop(0, SC_TOK_CHUNK * n_vec, zero_body, 0)
            _sc_row_pipeline(idx_v, v_hbm, rows_v, sems, SC_TOK_CHUNK * groups, groups, compute)
            pltpu.sync_copy(out_v, out_hbm.at[pl.ds(t0, SC_TOK_CHUNK)])
            return carry

        lax.fori_loop(0, n_chunks, chunk_body, 0)

    return k(w, idx, vp)


def peer_ffn(x, res, g, wq, subkeys, u, v):
    outs = norm_matmul(x, g, wq.astype(BF16), res=res, want_h=True)
    if res is None:
        q, h = outs
        xs = x
    else:
        q, xs, h = outs
    sk = subkeys.reshape(2 * PEER_HEADS, N_KEYS, PEER_KEY_HALF).astype(BF16)
    idx, gate = peer_topk(q, sk)
    act = peer_expert_dots(pack_bf16_pairs(h), idx, pack_bf16_pairs(u))
    w = gelu_gate(act, gate)
    return xs, peer_expert_combine(w, idx, pack_bf16_pairs(v))


def kernel(x, norm_mix_g, norm_ffn_g, final_norm_g, rel_bias, even_w_in, even_w_out,
           diff_lambda, diff_ln_g, odd_w_in, odd_b_f, odd_w_out, peer_wq, peer_subkeys,
           peer_u, peer_v):
    batch, seq, d = x.shape
    m = batch * seq
    xf = x.reshape(m, d)

    (p,) = norm_matmul(xf, norm_mix_g[0], even_w_in[0].astype(BF16))
    branches = []
    for window, dil in DILATED_BRANCHES:
        tile = dilated_bias_tile(rel_bias[:, :A_HEADS], window, dil)
        branches.append(dilated_branch(p, tile, dil, batch, seq))
    lam_init = 0.8 - 0.6 * math.exp(-0.3 * 0)
    ob = diff_attention(p.reshape(batch, seq, -1), diff_bias_tiles(rel_bias[:, A_HEADS:], seq),
                        diff_lambda[0], diff_ln_g[0], lam_init, batch, seq)
    xf = even_out_proj(branches, ob.reshape(m, B_V_WIDTH), even_w_out[0].astype(BF16), xf)
    xf, peer0 = peer_ffn(xf, None, norm_ffn_g[0], peer_wq[0], peer_subkeys[0],
                         peer_u[0], peer_v[0])

    w_in = odd_w_in[0]
    p, xf, h = norm_matmul(xf, norm_mix_g[1], w_in[:, :3 * C_WIDTH].astype(BF16),
                           res=peer0, want_h=True)
    w_gate = jnp.pad(w_in[:, 3 * C_WIDTH:], ((0, 0), (0, LANES - C_HEADS)))
    fg = gate_matmul(h, w_gate)
    b_f = jnp.pad(odd_b_f[0], (0, LANES - C_HEADS)).reshape(1, LANES)
    c = logsig_cumsum(fg.reshape(batch, seq, LANES), b_f, batch, seq)
    ct = c[:, :, :C_HEADS].transpose(0, 2, 1)
    o = fox_attention(p.reshape(batch, seq, -1), ct[:, :, :, None], ct[:, :, None, :],
                      batch, seq)
    xf = matmul_residual(o.reshape(m, C_WIDTH), odd_w_out[0].astype(BF16), xf)
    xf, peer1 = peer_ffn(xf, None, norm_ffn_g[1], peer_wq[1], peer_subkeys[1],
                         peer_u[1], peer_v[1])

    return add_norm(xf, peer1, final_norm_g).reshape(batch, seq, d)


def _gate_matmul_body(h_ref, w_ref, o_ref):
    o_ref[...] = jnp.dot(h_ref[...], w_ref[...], preferred_element_type=F32,
                         precision=lax.Precision.HIGHEST)


def gate_matmul(h, w, tm=512):
    m, d = h.shape
    n = w.shape[1]
    return pl.pallas_call(
        _gate_matmul_body,
        grid=(m // tm,),
        in_specs=[pl.BlockSpec((tm, d), lambda i: (i, 0)),
                  pl.BlockSpec((d, n), lambda i: (0, 0))],
        out_specs=pl.BlockSpec((tm, n), lambda i: (i, 0)),
        out_shape=jax.ShapeDtypeStruct((m, n), F32),
        compiler_params=pltpu.CompilerParams(dimension_semantics=("parallel",)),
        name="gate_matmul",
    )(h, w)
```

```python
import functools
import math

import numpy as np
import jax
import jax.numpy as jnp
from jax import lax
from jax.experimental import pallas as pl
from jax.experimental.pallas import tpu as pltpu
from jax.experimental.pallas import tpu_sc as plsc

F32 = jnp.float32
BF16 = jnp.bfloat16
I32 = jnp.int32

D_MODEL = 1024
HEAD_DIM = 64
A_HEADS = 8
DILATED_BRANCHES = ((128, 1), (512, 4), (2048, 16))
DIFF_HALF = 64
DIFF_VDIM = 128
B_HEADS = 4
C_HEADS = 16
N_BUCKETS = 32
MAX_DISTANCE = 2048
PEER_HEADS = 8
N_KEYS = 128
PEER_TOPK = 16
PEER_KEY_HALF = 128
RMS_EPS = 1e-6
NEG_INF = -1e30
A_WIDTH = A_HEADS * HEAD_DIM
B_QK_WIDTH = B_HEADS * 2 * DIFF_HALF
B_V_WIDTH = B_HEADS * DIFF_VDIM
C_WIDTH = C_HEADS * HEAD_DIM
QK_SCALE = 0.125
LANES = 128
DIL_BLOCK = 128
ATT_BLOCK = 512
BATCH_GROUPS = 2

SC_CORES = 2
SC_SUBCORES = 16
SC_LANES = 16
SC_WORKERS = SC_CORES * SC_SUBCORES


def _t5_bucket_table(n):
    max_exact = N_BUCKETS // 2
    d = np.arange(n)
    df = np.maximum(d, 1).astype(np.float32)
    large = max_exact + (
        np.log(df / np.float32(max_exact)) / np.float32(math.log(MAX_DISTANCE / max_exact))
        * np.float32(N_BUCKETS - max_exact)).astype(np.int32)
    large = np.minimum(large, N_BUCKETS - 1)
    return np.where(d < max_exact, d, large).astype(np.int32)


def _norm_matmul_body(*refs, has_res, want_h):
    it = iter(refs)
    x_ref = next(it)
    r_ref = next(it) if has_res else None
    g_ref = next(it)
    w_ref = next(it)
    o_ref = next(it)
    xs_ref = next(it) if has_res else None
    hout_ref = next(it) if want_h else None
    h_scr = next(it)

    @pl.when(pl.program_id(1) == 0)
    def _():
        x = x_ref[...]
        if has_res:
            x = x + r_ref[...]
            xs_ref[...] = x
        ms = jnp.mean(x * x, axis=-1, keepdims=True)
        h = x * lax.rsqrt(ms + RMS_EPS) * g_ref[...]
        if want_h == "f32":
            hout_ref[...] = h
        elif want_h == "packed":
            hout_ref[...] = pack_bf16_pairs(h)
        h_scr[...] = h.astype(BF16)

    o_ref[...] = jnp.dot(h_scr[...], w_ref[...],
                         preferred_element_type=F32).astype(o_ref.dtype)


def norm_matmul(x, g, w, *, res=None, want_h=None, out_dtype=BF16, tm=512, tn=512):
    m, d = x.shape
    n = w.shape[1]
    tn = min(tn, n)
    row = pl.BlockSpec((tm, d), lambda i, j: (i, 0))
    in_specs = [row] + ([row] if res is not None else []) + [
        pl.BlockSpec((1, d), lambda i, j: (0, 0)),
        pl.BlockSpec((d, tn), lambda i, j: (0, j))]
    out_specs = [pl.BlockSpec((tm, tn), lambda i, j: (i, j))]
    out_shape = [jax.ShapeDtypeStruct((m, n), out_dtype)]
    if res is not None:
        out_specs.append(row)
        out_shape.append(jax.ShapeDtypeStruct((m, d), F32))
    if want_h == "f32":
        out_specs.append(row)
        out_shape.append(jax.ShapeDtypeStruct((m, d), F32))
    elif want_h == "packed":
        out_specs.append(pl.BlockSpec((tm, d // 2), lambda i, j: (i, 0)))
        out_shape.append(jax.ShapeDtypeStruct((m, d // 2), I32))
    args =[x] + ([res] if res is not None else []) + [g.reshape(1, d), w]
    return pl.pallas_call(
        functools.partial(_norm_matmul_body, has_res=res is not None, want_h=want_h),
        grid=(m // tm, n // tn),
        in_specs=in_specs,
        out_specs=out_specs,
        out_shape=out_shape,
        scratch_shapes=[pltpu.VMEM((tm, d), BF16)],
        compiler_params=pltpu.CompilerParams(
            dimension_semantics=("parallel", "arbitrary")),
        name="norm_matmul",
    )(*args)


def _dilated_body(q_ref, kp_ref, kc_ref, vp_ref, vc_ref, b_ref, o_ref, lse_ref):
    i = pl.program_id(3)
    q = q_ref[...] * QK_SCALE
    k = jnp.concatenate([kp_ref[...], kc_ref[...]], axis=0)
    v = jnp.concatenate([vp_ref[...], vc_ref[...]], axis=0)
    col = lax.broadcasted_iota(I32, (DIL_BLOCK, 2 * DIL_BLOCK), 1)
    has_prev = jnp.logical_or(col >= DIL_BLOCK, i > 0)
    outs, lses = [], []
    for hh in range(2):
        sl = slice(hh * HEAD_DIM, (hh + 1) * HEAD_DIM)
        s = lax.dot_general(q[:, sl], k[:, sl], (((1,), (1,)), ((), ())),
                            preferred_element_type=F32)
        s = jnp.where(has_prev, s + b_ref[hh], NEG_INF)
        m = jnp.max(s, axis=-1, keepdims=True)
        p = jnp.exp(s - m)
        l = jnp.sum(p, axis=-1, keepdims=True)
        o = jnp.dot(p.astype(BF16), v[:, sl], preferred_element_type=F32) / l
        outs.append(o)
        lses.append(jnp.broadcast_to(m + jnp.log(l), (DIL_BLOCK, HEAD_DIM)))
    o_ref[...] = jnp.concatenate(outs, axis=1)
    lse_ref[...] = jnp.concatenate(lses, axis=1)


def dilated_branch(p, bias_tile, dil, batch, seq):
    n_cols = p.shape[1]
    cb = n_cols // LANES
    rows = seq // dil
    nblk = rows // DIL_BLOCK
    pv = p.reshape(batch, rows, dil * n_cols)
    kq, kk, kv = 0, A_WIDTH // LANES, 2 * A_WIDTH // LANES
    blk = (None, DIL_BLOCK, LANES)

    def spec(col0, prev):
        if prev:
            return pl.BlockSpec(blk, lambda b, h, r, i: (b, jnp.maximum(i - 1, 0), r * cb + col0 + h))
        return pl.BlockSpec(blk, lambda b, h, r, i: (b, i, r * cb + col0 + h))

    ocb = A_WIDTH // LANES
    ospec = pl.BlockSpec(blk, lambda b, h, r, i: (b, i, r * ocb + h))
    oshape = jax.ShapeDtypeStruct((batch, rows, dil * A_WIDTH), F32)
    o, lse = pl.pallas_call(
        _dilated_body,
        grid=(batch, A_HEADS // 2, dil, nblk),
        in_specs=[spec(kq, False), spec(kk, True), spec(kk, False),
                  spec(kv, True), spec(kv, False),
                  pl.BlockSpec((2, DIL_BLOCK, 2 * DIL_BLOCK), lambda b, h, r, i: (h, 0, 0))],
        out_specs=[ospec, ospec],
        out_shape=[oshape, oshape],
        compiler_params=pltpu.CompilerParams(
            dimension_semantics=("parallel", "parallel", "parallel", "arbitrary")),
        name=f"dilated_d{dil}",
    )(pv, pv, pv, pv, pv, bias_tile)
    return o.reshape(batch * seq, A_WIDTH), lse.reshape(batch * seq, A_WIDTH)


def dilated_bias_tile(rel_bias_a, window, dil):
    n = window // dil
    assert n == DIL_BLOCK
    bucket = _t5_bucket_table(window + 1)
    row = np.arange(DIL_BLOCK)[:, None]
    c = np.arange(2 * DIL_BLOCK)[None, :]
    j = DIL_BLOCK + row - c
    valid = (j >= 0) & (j <= n)
    bk = bucket[np.clip(j, 0, n) * dil]
    tile = rel_bias_a.T[:, bk]
    return jnp.where(jnp.asarray(valid)[None], tile, NEG_INF).astype(F32)


def _diff_body(q_ref, k_ref, v_ref, b_ref, lam_ref, g_ref, o_ref,
               m_scr, l_scr, acc_scr, *, n_tiles, lam_init):
    t = ATT_BLOCK
    qi = pl.program_id(2)
    q = q_ref[...] * QK_SCALE
    lane = lax.broadcasted_iota(I32, (1, LANES), 1)
    qa = [jnp.where((lane < DIFF_HALF) == (a == 0), q, jnp.zeros_like(q)) for a in range(2)]
    ones = jnp.ones((t, LANES), BF16)
    for a in range(2):
        m_scr[a] = jnp.full((t, LANES), NEG_INF, F32)
        l_scr[a] = jnp.zeros((t, LANES), F32)
        acc_scr[a] = jnp.zeros((t, DIFF_VDIM), F32)

    def step(j, masked):
        off = pl.multiple_of(j * t, t)
        ks = k_ref[pl.ds(off, t), :]
        vs = v_ref[pl.ds(off, t), :]
        bias = b_ref[jnp.minimum(qi - j, n_tiles - 1)]
        if masked:
            row = lax.broadcasted_iota(I32, (t, t), 0)
            col = lax.broadcasted_iota(I32, (t, t), 1)
            causal = row >= col
        for a in range(2):
            s = lax.dot_general(qa[a], ks, (((1,), (1,)), ((), ())),
                                preferred_element_type=F32) + bias
            if masked:
                s = jnp.where(causal, s, NEG_INF)
            m_prev = m_scr[a]
            m_new = jnp.maximum(m_prev, jnp.max(s, axis=-1, keepdims=True))
            alpha = jnp.exp(m_prev - m_new)
            pb = jnp.exp(s - jnp.concatenate([m_new] * (t // LANES), axis=1)).astype(BF16)
            l_scr[a] = alpha * l_scr[a] + jnp.dot(pb, ones, preferred_element_type=F32)
            acc_scr[a] = alpha * acc_scr[a] + jnp.dot(pb, vs, preferred_element_type=F32)
            m_scr[a] = m_new

    def loop_body(j, carry):
        step(j, False)
        return carry

    lax.fori_loop(0, qi, loop_body, 0)
    step(qi, True)

    lp = lam_ref[...]
    lam = (jnp.exp(jnp.sum(lp[0:1] * lp[1:2])) - jnp.exp(jnp.sum(lp[2:3] * lp[3:4]))
           + lam_init)
    o = acc_scr[0] / l_scr[0] - lam * (acc_scr[1] / l_scr[1])
    ms = jnp.mean(o * o, axis=-1, keepdims=True)
    y = o * lax.rsqrt(ms + RMS_EPS) * g_ref[...]
    o_ref[...] = (y * (1.0 - lam_init)).astype(o_ref.dtype)


def diff_attention(p, bias_tiles, lam_params, ln_g, lam_init, batch, seq):
    t = ATT_BLOCK
    n_tiles = bias_tiles.shape[1]
    cq = 3 * A_WIDTH // LANES
    ck = cq + B_QK_WIDTH // LANES
    cv = ck + B_QK_WIDTH // LANES
    return pl.pallas_call(
        functools.partial(_diff_body, n_tiles=n_tiles, lam_init=lam_init),
        grid=(batch, B_HEADS, seq // t),
        in_specs=[
            pl.BlockSpec((None, t, LANES), lambda b, h, i: (b, i, cq + h)),
            pl.BlockSpec((None, seq, LANES), lambda b, h, i: (b, 0, ck + h)),
            pl.BlockSpec((None, seq, LANES), lambda b, h, i: (b, 0, cv + h)),
            pl.BlockSpec((None, n_tiles, t, t), lambda b, h, i: (h, 0, 0, 0)),
            pl.BlockSpec((4, DIFF_HALF), lambda b, h, i: (0, 0)),
            pl.BlockSpec((1, DIFF_VDIM), lambda b, h, i: (0, 0)),
        ],
        out_specs=pl.BlockSpec((None, t, LANES), lambda b, h, i: (b, i, h)),
        out_shape=jax.ShapeDtypeStruct((batch, seq, B_V_WIDTH), BF16),
        scratch_shapes=[pltpu.VMEM((2, t, LANES), F32), pltpu.VMEM((2, t, LANES), F32),
                        pltpu.VMEM((2, t, DIFF_VDIM), F32)],
        compiler_params=pltpu.CompilerParams(
            dimension_semantics=("parallel", "parallel", "arbitrary")),
        name="diff_attention",
    )(p, p, p, bias_tiles, lam_params, ln_g.reshape(1, DIFF_VDIM))


def diff_bias_tiles(rel_bias_b, seq):
    t = ATT_BLOCK
    bucket = _t5_bucket_table(max(seq, 2 * MAX_DISTANCE) + 2 * t)
    sat = bucket[-1]
    d_sat = int(np.max(np.nonzero(bucket != sat)[0])) + 1
    n_full = (d_sat + t - 1 + t - 1) // t
    n_tiles = n_full + 1
    assert n_full * t - (t - 1) >= d_sat
    n = np.arange(2 * t)[None, :]
    base = np.arange(n_tiles)[:, None] * t
    dist = np.clip(np.where(n < t, base - n, base + 2 * t - n), 0, None)
    w = rel_bias_b.T[:, bucket[dist]].astype(F32)
    rep = jnp.broadcast_to(w[:, :, None, :], (B_HEADS, n_tiles, t, 2 * t))
    flat = rep.reshape(B_HEADS, n_tiles, 2 * t * t)[:, :, :t * (2 * t - 1)]
    return flat.reshape(B_HEADS, n_tiles, t, 2 * t - 1)[:, :, :, :t]


def _logsig_cumsum_body(f_ref, b_ref, c_ref, carry_scr):
    t = f_ref.shape[0]

    @pl.when(pl.program_id(1) == 0)
    def _():
        carry_scr[...] = jnp.zeros_like(carry_scr)

    x = f_ref[...] + b_ref[...]
    ls = jnp.minimum(x, 0.0) - jnp.log1p(jnp.exp(-jnp.abs(x)))
    row = lax.broadcasted_iota(I32, (t, t), 0)
    col = lax.broadcasted_iota(I32, (t, t), 1)
    tri = (row >= col).astype(F32)
    c = jnp.dot(tri, ls, preferred_element_type=F32,
                precision=lax.Precision.HIGHEST) + carry_scr[...]
    c_ref[...] = c
    carry_scr[...] = c[t - 1:t, :]


def logsig_cumsum(fg, b_f, batch, seq, t=512):
    return pl.pallas_call(
        _logsig_cumsum_body,
        grid=(batch, seq // t),
        in_specs=[pl.BlockSpec((None, t, LANES), lambda b, i: (b, i, 0)),
                  pl.BlockSpec((1, LANES), lambda b, i: (0, 0))],
        out_specs=pl.BlockSpec((None, t, LANES), lambda b, i: (b, i, 0)),
        out_shape=jax.ShapeDtypeStruct((batch, seq, LANES), F32),
        scratch_shapes=[pltpu.VMEM((1, LANES), F32)],
        compiler_params=pltpu.CompilerParams(
            dimension_semantics=("parallel", "arbitrary")),
        name="logsig_cumsum",
    )(fg, b_f)


def _fox_body(q_ref, k_ref, v_ref, cq_ref, ck_ref, o_ref, m_scr, acc_scr):
    t = ATT_BLOCK
    qi = pl.program_id(2)
    q = q_ref[...] * QK_SCALE
    lane = lax.broadcasted_iota(I32, (1, LANES), 1)
    own = [(lane < HEAD_DIM) == (a == 0) for a in range(2)]
    qa = [jnp.where(own[a], q, jnp.zeros_like(q)) for a in range(2)]
    cqb = [jnp.broadcast_to(cq_ref[a], (t, LANES)) for a in range(2)]
    for a in range(2):
        m_scr[a] = jnp.full((t, LANES), NEG_INF, F32)
        acc_scr[a] = jnp.zeros((t, LANES), F32)

    def step(j, masked):
        off = pl.multiple_of(j * t, t)
        ks = k_ref[pl.ds(off, t), :]
        vs = v_ref[pl.ds(off, t), :]
        if masked:
            row = lax.broadcasted_iota(I32, (t, t), 0)
            col = lax.broadcasted_iota(I32, (t, t), 1)
            causal = row >= col
        for a in range(2):
            s = lax.dot_general(qa[a], ks, (((1,), (1,)), ((), ())),
                                preferred_element_type=F32)
            s = s - ck_ref[a, :, pl.ds(off, t)]
            if masked:
                s = jnp.where(causal, s, NEG_INF)
            m_prev = m_scr[a]
            m_new = jnp.maximum(m_prev, jnp.max(s, axis=-1, keepdims=True) + cqb[a])
            alpha = jnp.exp(m_prev - m_new)
            shift = m_new - cqb[a]
            pb = jnp.exp(s - jnp.concatenate([shift] * (t // LANES), axis=1)).astype(BF16)
            v_aug = jnp.where(own[a], vs, jnp.ones_like(vs))
            acc_scr[a] = alpha * acc_scr[a] + jnp.dot(pb, v_aug, preferred_element_type=F32)
            m_scr[a] = m_new

    def loop_body(j, carry):
        step(j, False)
        return carry

    lax.fori_loop(0, qi, loop_body, 0)
    step(qi, True)
    r = [acc_scr[a] / pltpu.roll(acc_scr[a], HEAD_DIM, 1) for a in range(2)]
    o_ref[...] = jnp.where(own[0], r[0], r[1]).astype(o_ref.dtype)


def fox_attention(p, cq, ck, batch, seq):
    t = ATT_BLOCK
    nk = C_WIDTH // LANES
    return pl.pallas_call(
        _fox_body,
        grid=(batch, C_HEADS // 2, seq // t),
        in_specs=[
            pl.BlockSpec((None, t, LANES), lambda b, h, i: (b, i, h)),
            pl.BlockSpec((None, seq, LANES), lambda b, h, i: (b, 0, nk + h)),
            pl.BlockSpec((None, seq, LANES), lambda b, h, i: (b, 0, 2 * nk + h)),
            pl.BlockSpec((None, 2, t, 1), lambda b, h, i: (b, h, i, 0)),
            pl.BlockSpec((None, 2, 1, seq), lambda b, h, i: (b, h, 0, 0)),
        ],
        out_specs=pl.BlockSpec((None, t, LANES), lambda b, h, i: (b, i, h)),
        out_shape=jax.ShapeDtypeStruct((batch, seq, C_WIDTH), BF16),
        scratch_shapes=[pltpu.VMEM((2, t, LANES), F32), pltpu.VMEM((2, t, LANES), F32)],
        compiler_params=pltpu.CompilerParams(
            dimension_semantics=("parallel", "parallel", "arbitrary")),
        name="fox_attention",
    )(p, p, p, cq, ck)


def _even_out_body(o1, l1, o2, l2, o3, l3, ob_ref, w_ref, x_ref, out_ref, a_scr):
    @pl.when(pl.program_id(1) == 0)
    def _():
        a1, a2, a3 = l1[...], l2[...], l3[...]
        mx = jnp.maximum(jnp.maximum(a1, a2), a3)
        e1, e2, e3 = jnp.exp(a1 - mx), jnp.exp(a2 - mx), jnp.exp(a3 - mx)
        oa = (e1 * o1[...] + e2 * o2[...] + e3 * o3[...]) / (e1 + e2 + e3)
        a_scr[:, :A_WIDTH] = oa.astype(BF16)
        a_scr[:, A_WIDTH:] = ob_ref[...]

    out_ref[...] = x_ref[...] + jnp.dot(a_scr[...], w_ref[...], preferred_element_type=F32)


def even_out_proj(branches, ob, w, x, tm=512, tn=512):
    m, d = x.shape
    half = pl.BlockSpec((tm, A_WIDTH), lambda i, j: (i, 0))
    flat = [a for pair in branches for a in pair]
    return pl.pallas_call(
        _even_out_body,
        grid=(m // tm, d // tn),
        in_specs=[half] * 6 + [
            pl.BlockSpec((tm, B_V_WIDTH), lambda i, j: (i, 0)),
            pl.BlockSpec((A_WIDTH + B_V_WIDTH, tn), lambda i, j: (0, j)),
            pl.BlockSpec((tm, tn), lambda i, j: (i, j))],
        out_specs=pl.BlockSpec((tm, tn), lambda i, j: (i, j)),
        out_shape=jax.ShapeDtypeStruct((m, d), F32),
        scratch_shapes=[pltpu.VMEM((tm, A_WIDTH + B_V_WIDTH), BF16)],
        compiler_params=pltpu.CompilerParams(
            dimension_semantics=("parallel", "arbitrary")),
        name="even_out_proj",
    )(*flat, ob, w, x)


def _matmul_res_body(a_ref, w_ref, x_ref, o_ref):
    o_ref[...] = x_ref[...] + jnp.dot(a_ref[...], w_ref[...], preferred_element_type=F32)


def matmul_residual(a, w, x, tm=512, tn=512):
    m, k = a.shape
    n = w.shape[1]
    return pl.pallas_call(
        _matmul_res_body,
        grid=(m // tm, n // tn),
        in_specs=[pl.BlockSpec((tm, k), lambda i, j: (i, 0)),
                  pl.BlockSpec((k, tn), lambda i, j: (0, j)),
                  pl.BlockSpec((tm, tn), lambda i, j: (i, j))],
        out_specs=pl.BlockSpec((tm, tn), lambda i, j: (i, j)),
        out_shape=jax.ShapeDtypeStruct((m, n), F32),
        compiler_params=pltpu.CompilerParams(
            dimension_semantics=("parallel", "parallel")),
        name="matmul_residual",
    )(a, w, x)


def _add_norm_body(x_ref, r_ref, g_ref, o_ref):
    x = x_ref[...] + r_ref[...]
    ms = jnp.mean(x * x, axis=-1, keepdims=True)
    o_ref[...] = x * lax.rsqrt(ms + RMS_EPS) * g_ref[...]


def add_norm(x, r, g, tm=512):
    m, d = x.shape
    row = pl.BlockSpec((tm, d), lambda i: (i, 0))
    return pl.pallas_call(
        _add_norm_body,
        grid=(m // tm,),
        in_specs=[row, row, pl.BlockSpec((1, d), lambda i: (0, 0))],
        out_specs=row,
        out_shape=jax.ShapeDtypeStruct((m, d), F32),
        compiler_params=pltpu.CompilerParams(dimension_semantics=("parallel",)),
        name="add_norm",
    )(x, r, g.reshape(1, d))


PEER_CAND_ROWS = PEER_TOPK + 8 * (PEER_TOPK - 1)


def _peer_topk_body(q_ref, sk_ref, idx_ref, gate_ref, ts_scr, ti_scr, bs_scr, be_scr,
                    gt_scr, it_scr):
    tm = q_ref.shape[0]
    neg_inf = jnp.float32(-jnp.inf)
    key_id = lax.broadcasted_iota(I32, (N_KEYS, tm), 0)

    def pair_body(pr, carry):
        off = pl.multiple_of(pr * PEER_KEY_HALF, PEER_KEY_HALF)
        sc = lax.dot_general(sk_ref[pr], q_ref[:, pl.ds(off, PEER_KEY_HALF)],
                             (((1,), (1,)), ((), ())), preferred_element_type=F32)

        def k_body(k, vals):
            m = jnp.max(vals, axis=0, keepdims=True)
            sel = jnp.min(jnp.where(vals == m, key_id, N_KEYS), axis=0, keepdims=True)
            ts_scr[pr, pl.ds(k, 1), :] = m
            ti_scr[pr, pl.ds(k, 1), :] = sel
            return jnp.where(key_id == sel, neg_inf, vals)

        lax.fori_loop(0, PEER_TOPK, k_body, sc)
        return carry

    lax.fori_loop(0, 2 * PEER_HEADS, pair_body, 0)

    r = lax.broadcasted_iota(I32, (PEER_CAND_ROWS, 1), 0)
    cand_id = jnp.where(r < PEER_TOPK, r,
                        (1 + (r - PEER_TOPK) // 8) * PEER_TOPK + (r - PEER_TOPK) % 8)

    def head_body(h, carry):
        s1, s2 = ts_scr[2 * h], ts_scr[2 * h + 1]
        i1, i2 = ti_scr[2 * h] * N_KEYS, ti_scr[2 * h + 1]
        vals = jnp.concatenate(
            [s1[0:1] + s2] + [s1[a:a + 1] + s2[0:8] for a in range(1, PEER_TOPK)], axis=0)
        eidx = jnp.concatenate(
            [i1[0:1] + i2] + [i1[a:a + 1] + i2[0:8] for a in range(1, PEER_TOPK)], axis=0)

        def k_body(k, vals):
            m = jnp.max(vals, axis=0, keepdims=True)
            sel = jnp.min(jnp.where(vals == m, cand_id, PEER_TOPK * PEER_TOPK),
                          axis=0, keepdims=True)
            hit = cand_id == sel
            bs_scr[pl.ds(k, 1), :] = m
            be_scr[pl.ds(k, 1), :] = jnp.sum(jnp.where(hit, eidx, 0), axis=0, keepdims=True)
            return jnp.where(hit, neg_inf, vals)

        lax.fori_loop(0, PEER_TOPK, k_body, vals)
        bs = bs_scr[...]
        e = jnp.exp(bs - jnp.max(bs, axis=0, keepdims=True))
        row0 = pl.multiple_of(h * PEER_TOPK, PEER_TOPK)
        gt_scr[pl.ds(row0, PEER_TOPK), :] = e / jnp.sum(e, axis=0, keepdims=True)
        it_scr[pl.ds(row0, PEER_TOPK), :] = be_scr[...]
        return carry

    lax.fori_loop(0, PEER_HEADS, head_body, 0)
    gate_ref[...] = gt_scr[...].T
    idx_ref[...] = it_scr[...].T


def peer_topk(q, subkeys, tm=256):
    m = q.shape[0]
    n_sel = PEER_HEADS * PEER_TOPK
    out_spec = pl.BlockSpec((tm, n_sel), lambda i: (i, 0))
    return pl.pallas_call(
        _peer_topk_body,
        grid=(m // tm,),
        in_specs=[pl.BlockSpec((tm, q.shape[1]), lambda i: (i, 0)),
                  pl.BlockSpec(subkeys.shape, lambda i: (0, 0, 0))],
        out_specs=[out_spec, out_spec],
        out_shape=[jax.ShapeDtypeStruct((m, n_sel), I32),
                   jax.ShapeDtypeStruct((m, n_sel), F32)],
        scratch_shapes=[pltpu.VMEM((2 * PEER_HEADS, PEER_TOPK, tm), F32),
                        pltpu.VMEM((2 * PEER_HEADS, PEER_TOPK, tm), I32),
                        pltpu.VMEM((PEER_TOPK, tm), F32),
                        pltpu.VMEM((PEER_TOPK, tm), I32),
                        pltpu.VMEM((n_sel, tm), F32),
                        pltpu.VMEM((n_sel, tm), I32)],
        compiler_params=pltpu.CompilerParams(dimension_semantics=("parallel",)),
        name="peer_topk",
    )(q, subkeys)


def _gelu_gate_body(a_ref, g_ref, o_ref):
    a = a_ref[...]
    o_ref[...] = g_ref[...] * (0.5 * a * (1.0 + lax.erf(a * (2.0 ** -0.5))))


def gelu_gate(act, gate, tm=2048):
    m, n = act.shape
    spec = pl.BlockSpec((tm, n), lambda i: (i, 0))
    return pl.pallas_call(
        _gelu_gate_body,
        grid=(m // tm,),
        in_specs=[spec, spec],
        out_specs=spec,
        out_shape=jax.ShapeDtypeStruct((m, n), F32),
        compiler_params=pltpu.CompilerParams(dimension_semantics=("parallel",)),
        name="gelu_gate",
    )(act, gate)


SC_TOK_CHUNK = 32
SC_RING = 8
SC_BF16_GROUP = 4
SC_FMT = plsc.PackFormat.INTERLEAVED


def _sc_worker_id():
    return lax.axis_index("s") * SC_CORES + lax.axis_index("c")


def pack_bf16_pairs(t):
    half = t.shape[-1] // 2
    bits = lax.bitcast_convert_type(t.astype(BF16).astype(F32), I32)
    return (bits[..., half:] & jnp.int32(-65536)) | lax.shift_right_logical(
        bits[..., :half], jnp.int32(16))


def _sc_row_pipeline(idx_v, table_hbm, rows_v, sems, n_items, groups, compute):
    def gather(item):
        tt, g = item // groups, item % groups
        ids = idx_v[tt, pl.ds(g * SC_LANES, SC_LANES)]
        slot = item % SC_RING
        return pltpu.make_async_copy(table_hbm.at[ids], rows_v.at[slot], sems.at[slot])

    for s in range(SC_RING - 1):
        gather(s).start()

    def item_body(item, carry):
        nxt = item + SC_RING - 1

        @pl.when(nxt < n_items)
        def _():
            gather(nxt).start()

        gather(item).wait()
        compute(item // groups, item % groups, item % SC_RING)
        return carry

    lax.fori_loop(0, n_items, item_body, 0)


def peer_expert_dots(hp, idx, up):
    m, dw = hp.shape
    n_sel = idx.shape[1]
    per_w = m // SC_WORKERS
    n_chunks = per_w // SC_TOK_CHUNK
    groups = n_sel // SC_LANES
    step = SC_BF16_GROUP * SC_LANES
    mesh = plsc.VectorSubcoreMesh(core_axis_name="c", subcore_axis_name="s")

    @functools.partial(
        pl.kernel, mesh=mesh,
        out_type=jax.ShapeDtypeStruct((m, n_sel), F32),
        scratch_types=[
            pltpu.VMEM((SC_TOK_CHUNK, n_sel), I32),
            pltpu.VMEM((SC_TOK_CHUNK, dw), I32),
            pltpu.VMEM((SC_TOK_CHUNK, n_sel), F32),
            pltpu.VMEM((SC_RING, SC_LANES, dw), I32),
            pltpu.VMEM((SC_LANES * SC_LANES,), F32),
            pltpu.SemaphoreType.DMA((SC_RING,)),
        ],
        compiler_params=pltpu.CompilerParams(needs_layout_passes=False),
        name="peer_expert_dots",
    )
    def k(h_hbm, idx_hbm, u_hbm, act_hbm, idx_v, h_v, act_v, rows_v, part_v, sems):
        base = _sc_worker_id() * per_w
        lane = lax.broadcasted_iota(I32, (SC_LANES,), 0)

        def compute(tt, g, slot):
            def grp_body(q, accs):
                off = pl.multiple_of(q * step, step)
                xs = [plsc.bitcast(h_v[tt, pl.ds(off + c * SC_LANES, SC_LANES)], BF16)
                      for c in range(SC_BF16_GROUP)]
                new = []
                for e in range(SC_LANES):
                    s = None
                    for c in range(SC_BF16_GROUP):
                        p = plsc.bitcast(
                            rows_v[slot, e, pl.ds(off + c * SC_LANES, SC_LANES)], BF16) * xs[c]
                        s = p if s is None else s + p
                    lo, hi = plsc.unpack(s, format=SC_FMT)
                    new.append(accs[e] + (lo + hi))
                return tuple(new)

            accs = lax.fori_loop(
                0, dw // step, grp_body,
                tuple(jnp.zeros((SC_LANES,), F32) for _ in range(SC_LANES)))
            for e in range(SC_LANES):
                part_v[pl.ds(e * SC_LANES, SC_LANES)] = accs[e]
            tot = jnp.zeros((SC_LANES,), F32)
            for l in range(SC_LANES):
                tot = tot + plsc.load_gather(part_v, [lane * SC_LANES + l])
            act_v[tt, pl.ds(g * SC_LANES, SC_LANES)] = tot

        def chunk_body(c, carry):
            t0 = base + c * SC_TOK_CHUNK
            pltpu.sync_copy(idx_hbm.at[pl.ds(t0, SC_TOK_CHUNK)], idx_v)
            pltpu.sync_copy(h_hbm.at[pl.ds(t0, SC_TOK_CHUNK)], h_v)
            _sc_row_pipeline(idx_v, u_hbm, rows_v, sems, SC_TOK_CHUNK * groups, groups, compute)
            pltpu.sync_copy(act_v, act_hbm.at[pl.ds(t0, SC_TOK_CHUNK)])
            return carry

        lax.fori_loop(0, n_chunks, chunk_body, 0)

    return k(hp, idx, up)


def peer_expert_combine(w, idx, vp):
    m, n_sel = w.shape
    dw = vp.shape[1]
    d = 2 * dw
    per_w = m // SC_WORKERS
    n_chunks = per_w // SC_TOK_CHUNK
    n_vec = d // SC_LANES
    groups = n_sel // SC_LANES
    mesh = plsc.VectorSubcoreMesh(core_axis_name="c", subcore_axis_name="s")

    @functools.partial(
        pl.kernel, mesh=mesh,
        out_type=jax.ShapeDtypeStruct((m, d), F32),
        scratch_types=[
            pltpu.VMEM((SC_TOK_CHUNK, n_sel), I32),
            pltpu.VMEM((SC_TOK_CHUNK, n_sel), F32),
            pltpu.VMEM((SC_TOK_CHUNK, d), F32),
            pltpu.VMEM((SC_RING, SC_LANES, dw), I32),
            pltpu.SemaphoreType.DMA((SC_RING,)),
        ],
        compiler_params=pltpu.CompilerParams(needs_layout_passes=False),
        name="peer_expert_combine",
    )
    def k(w_hbm, idx_hbm, v_hbm, out_hbm, idx_v, w_v, out_v, rows_v, sems):
        base = _sc_worker_id() * per_w

        def compute(tt, g, slot):
            splat = []
            for e in range(SC_LANES):
                s = plsc.load_gather(w_v, [jnp.full((SC_LANES,), tt, I32),
                                           jnp.full((SC_LANES,), g * SC_LANES + e, I32)])
                splat.append(plsc.pack(s, s, format=SC_FMT))

            @plsc.parallel_loop(0, dw // SC_LANES)
            def _(j):
                off = pl.multiple_of(j * SC_LANES, SC_LANES)
                acc_lo = out_v[tt, pl.ds(off, SC_LANES)]
                acc_hi = out_v[tt, pl.ds(dw + off, SC_LANES)]
                for e0 in range(0, SC_LANES, SC_BF16_GROUP):
                    s = None
                    for e in range(e0, e0 + SC_BF16_GROUP):
                        p = plsc.bitcast(rows_v[slot, e, pl.ds(off, SC_LANES)], BF16) * splat[e]
                        s = p if s is None else s + p
                    lo, hi = plsc.unpack(s, format=SC_FMT)
                    acc_lo = acc_lo + lo
                    acc_hi = acc_hi + hi
                out_v[tt, pl.ds(off, SC_LANES)] = acc_lo
                out_v[tt, pl.ds(dw + off, SC_LANES)] = acc_hi

        def chunk_body(c, carry):
            t0 = base + c * SC_TOK_CHUNK
            pltpu.sync_copy(idx_hbm.at[pl.ds(t0, SC_TOK_CHUNK)], idx_v)
            pltpu.sync_copy(w_hbm.at[pl.ds(t0, SC_TOK_CHUNK)], w_v)

            def zero_body(z, carry2):
                tt, j = z // n_vec, z % n_vec
                out_v[tt, pl.ds(pl.multiple_of(j * SC_LANES, SC_LANES), SC_LANES)] = (
                    jnp.zeros((SC_LANES,), F32))
                return carry2

            lax.fori_loop(0, SC_TOK_CHUNK * n_vec, zero_body, 0)
            _sc_row_pipeline(idx_v, v_hbm, rows_v, sems, SC_TOK_CHUNK * groups, groups, compute)
            pltpu.sync_copy(out_v, out_hbm.at[pl.ds(t0, SC_TOK_CHUNK)])
            return carry

        lax.fori_loop(0, n_chunks, chunk_body, 0)

    return k(w, idx, vp)


def peer_ffn(x, res, g, *, wq, sk, up, vp):
    outs = norm_matmul(x, g, wq, res=res, want_h="packed")
    if res is None:
        q, hp = outs
        xs = x
    else:
        q, xs, hp = outs
    idx, gate = peer_topk(q, sk)
    act = peer_expert_dots(hp, idx, up)
    w = gelu_gate(act, gate)
    return xs, peer_expert_combine(w, idx, vp)


def kernel(x, norm_mix_g, norm_ffn_g, final_norm_g, rel_bias, even_w_in, even_w_out,
           diff_lambda, diff_ln_g, odd_w_in, odd_b_f, odd_w_out, peer_wq, peer_subkeys,
           peer_u, peer_v):
    batch, seq, d = x.shape

    dil_tiles = [dilated_bias_tile(rel_bias[:, :A_HEADS], w, dl) for w, dl in DILATED_BRANCHES]
    diff_tiles = diff_bias_tiles(rel_bias[:, A_HEADS:], seq)
    lam_init = 0.8 - 0.6 * math.exp(-0.3 * 0)
    even_in, even_out = even_w_in[0].astype(BF16), even_w_out[0].astype(BF16)
    w_in = odd_w_in[0]
    odd_in, odd_out = w_in[:, :3 * C_WIDTH].astype(BF16), odd_w_out[0].astype(BF16)
    w_gate = jnp.pad(w_in[:, 3 * C_WIDTH:], ((0, 0), (0, LANES - C_HEADS)))
    b_f = jnp.pad(odd_b_f[0], (0, LANES - C_HEADS)).reshape(1, LANES)
    peer = [dict(wq=peer_wq[l].astype(BF16),
                 sk=peer_subkeys[l].reshape(2 * PEER_HEADS, N_KEYS, PEER_KEY_HALF).astype(BF16),
                 up=pack_bf16_pairs(peer_u[l]), vp=pack_bf16_pairs(peer_v[l]))
            for l in range(2)]

    def trunk(xg):
        bg = xg.shape[0]
        m = bg * seq
        xf = xg.reshape(m, d)

        (p,) = norm_matmul(xf, norm_mix_g[0], even_in)
        branches = [dilated_branch(p, tile, dl, bg, seq)
                    for tile, (_, dl) in zip(dil_tiles, DILATED_BRANCHES)]
        ob = diff_attention(p.reshape(bg, seq, -1), diff_tiles, diff_lambda[0], diff_ln_g[0],
                            lam_init, bg, seq)
        xf = even_out_proj(branches, ob.reshape(m, B_V_WIDTH), even_out, xf)
        xf, peer0 = peer_ffn(xf, None, norm_ffn_g[0], **peer[0])

        p, xf, h = norm_matmul(xf, norm_mix_g[1], odd_in, res=peer0, want_h="f32")
        fg = gate_matmul(h, w_gate)
        c = logsig_cumsum(fg.reshape(bg, seq, LANES), b_f, bg, seq)
        ct = c[:, :, :C_HEADS].transpose(0, 2, 1)
        o = fox_attention(p.reshape(bg, seq, -1), ct[:, :, :, None], ct[:, :, None, :], bg, seq)
        xf = matmul_residual(o.reshape(m, C_WIDTH), odd_out, xf)
        xf, peer1 = peer_ffn(xf, None, norm_ffn_g[1], **peer[1])
        return add_norm(xf, peer1, final_norm_g).reshape(bg, seq, d)

    per = batch // BATCH_GROUPS
    return jnp.concatenate([trunk(x[g * per:(g + 1) * per]) for g in range(BATCH_GROUPS)], axis=0)


def _gate_matmul_body(h_ref, w_ref, o_ref):
    o_ref[...] = jnp.dot(h_ref[...], w_ref[...], preferred_element_type=F32,
                         precision=lax.Precision.HIGHEST)


def gate_matmul(h, w, tm=512):
    m, d = h.shape
    n = w.shape[1]
    return pl.pallas_call(
        _gate_matmul_body,
        grid=(m // tm,),
        in_specs=[pl.BlockSpec((tm, d), lambda i: (i, 0)),
                  pl.BlockSpec((d, n), lambda i: (0, 0))],
        out_specs=pl.BlockSpec((tm, n), lambda i: (i, 0)),
        out_shape=jax.ShapeDtypeStruct((m, n), F32),
        compiler_params=pltpu.CompilerParams(dimension_semantics=("parallel",)),
        name="gate_matmul",
    )(h, w)
```

```python
import functools
import math

import numpy as np
import jax
import jax.numpy as jnp
from jax import lax
from jax.experimental import pallas as pl
from jax.experimental.pallas import tpu as pltpu
from jax.experimental.pallas import tpu_sc as plsc

F32 = jnp.float32
BF16 = jnp.bfloat16
I32 = jnp.int32

D_MODEL = 1024
HEAD_DIM = 64
A_HEADS = 8
DILATED_BRANCHES = ((128, 1), (512, 4), (2048, 16))
DIFF_HALF = 64
DIFF_VDIM = 128
B_HEADS = 4
C_HEADS = 16
N_BUCKETS = 32
MAX_DISTANCE = 2048
PEER_HEADS = 8
N_KEYS = 128
PEER_TOPK = 16
PEER_KEY_HALF = 128
RMS_EPS = 1e-6
NEG_INF = -1e30
A_WIDTH = A_HEADS * HEAD_DIM
B_QK_WIDTH = B_HEADS * 2 * DIFF_HALF
B_V_WIDTH = B_HEADS * DIFF_VDIM
C_WIDTH = C_HEADS * HEAD_DIM
QK_SCALE = 0.125
LANES = 128
DIL_BLOCK = 128
ATT_BLOCK = 512
BATCH_GROUPS = 4

SC_CORES = 2
SC_SUBCORES = 16
SC_LANES = 16
SC_WORKERS = SC_CORES * SC_SUBCORES


def _t5_bucket_table(n):
    max_exact = N_BUCKETS // 2
    d = np.arange(n)
    df = np.maximum(d, 1).astype(np.float32)
    large = max_exact + (
        np.log(df / np.float32(max_exact)) / np.float32(math.log(MAX_DISTANCE / max_exact))
        * np.float32(N_BUCKETS - max_exact)).astype(np.int32)
    large = np.minimum(large, N_BUCKETS - 1)
    return np.where(d < max_exact, d, large).astype(np.int32)


def _norm_matmul_body(*refs, has_res, want_h):
    it = iter(refs)
    x_ref = next(it)
    r_ref = next(it) if has_res else None
    g_ref = next(it)
    w_ref = next(it)
    o_ref = next(it)
    xs_ref = next(it) if has_res else None
    hout_ref = next(it) if want_h else None
    h_scr = next(it)

    @pl.when(pl.program_id(1) == 0)
    def _():
        x = x_ref[...]
        if has_res:
            x = x + r_ref[...]
            xs_ref[...] = x
        ms = jnp.mean(x * x, axis=-1, keepdims=True)
        h = x * lax.rsqrt(ms + RMS_EPS) * g_ref[...]
        if want_h == "f32":
            hout_ref[...] = h
        elif want_h == "packed":
            hout_ref[...] = pack_bf16_pairs(h)
        h_scr[...] = h.astype(BF16)

    o_ref[...] = jnp.dot(h_scr[...], w_ref[...],
                         preferred_element_type=F32).astype(o_ref.dtype)


def norm_matmul(x, g, w, *, res=None, want_h=None, out_dtype=BF16, tm=512, tn=512):
    m, d = x.shape
    n = w.shape[1]
    tn = min(tn, n)
    row = pl.BlockSpec((tm, d), lambda i, j: (i, 0))
    in_specs = [row] + ([row] if res is not None else []) + [
        pl.BlockSpec((1, d), lambda i, j: (0, 0)),
        pl.BlockSpec((d, tn), lambda i, j: (0, j))]
    out_specs = [pl.BlockSpec((tm, tn), lambda i, j: (i, j))]
    out_shape = [jax.ShapeDtypeStruct((m, n), out_dtype)]
    if res is not None:
        out_specs.append(row)
        out_shape.append(jax.ShapeDtypeStruct((m, d), F32))
    if want_h == "f32":
        out_specs.append(row)
        out_shape.append(jax.ShapeDtypeStruct((m, d), F32))
    elif want_h == "packed":
        out_specs.append(pl.BlockSpec((tm, d // 2), lambda i, j: (i, 0)))
        out_shape.append(jax.ShapeDtypeStruct((m, d // 2), I32))
    args =[x] + ([res] if res is not None else []) + [g.reshape(1, d), w]
    return pl.pallas_call(
        functools.partial(_norm_matmul_body, has_res=res is not None, want_h=want_h),
        grid=(m // tm, n // tn),
        in_specs=in_specs,
        out_specs=out_specs,
        out_shape=out_shape,
        scratch_shapes=[pltpu.VMEM((tm, d), BF16)],
        compiler_params=pltpu.CompilerParams(
            dimension_semantics=("parallel", "arbitrary")),
        name="norm_matmul",
    )(*args)


def _dilated_body(q_ref, kp_ref, kc_ref, vp_ref, vc_ref, b_ref, o_ref, lse_ref):
    i = pl.program_id(3)
    q = q_ref[...] * QK_SCALE
    k = jnp.concatenate([kp_ref[...], kc_ref[...]], axis=0)
    v = jnp.concatenate([vp_ref[...], vc_ref[...]], axis=0)
    col = lax.broadcasted_iota(I32, (DIL_BLOCK, 2 * DIL_BLOCK), 1)
    has_prev = jnp.logical_or(col >= DIL_BLOCK, i > 0)
    outs, lses = [], []
    for hh in range(2):
        sl = slice(hh * HEAD_DIM, (hh + 1) * HEAD_DIM)
        s = lax.dot_general(q[:, sl], k[:, sl], (((1,), (1,)), ((), ())),
                            preferred_element_type=F32)
        s = jnp.where(has_prev, s + b_ref[hh], NEG_INF)
        m = jnp.max(s, axis=-1, keepdims=True)
        p = jnp.exp(s - m)
        l = jnp.sum(p, axis=-1, keepdims=True)
        o = jnp.dot(p.astype(BF16), v[:, sl], preferred_element_type=F32) / l
        outs.append(o)
        lses.append(jnp.broadcast_to(m + jnp.log(l), (DIL_BLOCK, HEAD_DIM)))
    o_ref[...] = jnp.concatenate(outs, axis=1)
    lse_ref[...] = jnp.concatenate(lses, axis=1)


def dilated_branch(p, bias_tile, dil, batch, seq):
    n_cols = p.shape[1]
    cb = n_cols // LANES
    rows = seq // dil
    nblk = rows // DIL_BLOCK
    pv = p.reshape(batch, rows, dil * n_cols)
    kq, kk, kv = 0, A_WIDTH // LANES, 2 * A_WIDTH // LANES
    blk = (None, DIL_BLOCK, LANES)

    def spec(col0, prev):
        if prev:
            return pl.BlockSpec(blk, lambda b, h, r, i: (b, jnp.maximum(i - 1, 0), r * cb + col0 + h))
        return pl.BlockSpec(blk, lambda b, h, r, i: (b, i, r * cb + col0 + h))

    ocb = A_WIDTH // LANES
    ospec = pl.BlockSpec(blk, lambda b, h, r, i: (b, i, r * ocb + h))
    oshape = jax.ShapeDtypeStruct((batch, rows, dil * A_WIDTH), F32)
    o, lse = pl.pallas_call(
        _dilated_body,
        grid=(batch, A_HEADS // 2, dil, nblk),
        in_specs=[spec(kq, False), spec(kk, True), spec(kk, False),
                  spec(kv, True), spec(kv, False),
                  pl.BlockSpec((2, DIL_BLOCK, 2 * DIL_BLOCK), lambda b, h, r, i: (h, 0, 0))],
        out_specs=[ospec, ospec],
        out_shape=[oshape, oshape],
        compiler_params=pltpu.CompilerParams(
            dimension_semantics=("parallel", "parallel", "parallel", "arbitrary")),
        name=f"dilated_d{dil}",
    )(pv, pv, pv, pv, pv, bias_tile)
    return o.reshape(batch * seq, A_WIDTH), lse.reshape(batch * seq, A_WIDTH)


def dilated_bias_tile(rel_bias_a, window, dil):
    n = window // dil
    assert n == DIL_BLOCK
    bucket = _t5_bucket_table(window + 1)
    row = np.arange(DIL_BLOCK)[:, None]
    c = np.arange(2 * DIL_BLOCK)[None, :]
    j = DIL_BLOCK + row - c
    valid = (j >= 0) & (j <= n)
    bk = bucket[np.clip(j, 0, n) * dil]
    tile = rel_bias_a.T[:, bk]
    return jnp.where(jnp.asarray(valid)[None], tile, NEG_INF).astype(F32)


def _diff_body(q_ref, k_ref, v_ref, b_ref, lam_ref, g_ref, o_ref,
               m_scr, l_scr, acc_scr, *, n_tiles, lam_init):
    t = ATT_BLOCK
    qi = pl.program_id(2)
    q = q_ref[...] * QK_SCALE
    lane = lax.broadcasted_iota(I32, (1, LANES), 1)
    qa = [jnp.where((lane < DIFF_HALF) == (a == 0), q, jnp.zeros_like(q)) for a in range(2)]
    ones = jnp.ones((t, LANES), BF16)
    for a in range(2):
        m_scr[a] = jnp.full((t, LANES), NEG_INF, F32)
        l_scr[a] = jnp.zeros((t, LANES), F32)
        acc_scr[a] = jnp.zeros((t, DIFF_VDIM), F32)

    def step(j, masked):
        off = pl.multiple_of(j * t, t)
        ks = k_ref[pl.ds(off, t), :]
        vs = v_ref[pl.ds(off, t), :]
        bias = b_ref[jnp.minimum(qi - j, n_tiles - 1)]
        if masked:
            row = lax.broadcasted_iota(I32, (t, t), 0)
            col = lax.broadcasted_iota(I32, (t, t), 1)
            causal = row >= col
        for a in range(2):
            s = lax.dot_general(qa[a], ks, (((1,), (1,)), ((), ())),
                                preferred_element_type=F32) + bias
            if masked:
                s = jnp.where(causal, s, NEG_INF)
            m_prev = m_scr[a]
            m_new = jnp.maximum(m_prev, jnp.max(s, axis=-1, keepdims=True))
            alpha = jnp.exp(m_prev - m_new)
            pb = jnp.exp(s - jnp.concatenate([m_new] * (t // LANES), axis=1)).astype(BF16)
            l_scr[a] = alpha * l_scr[a] + jnp.dot(pb, ones, preferred_element_type=F32)
            acc_scr[a] = alpha * acc_scr[a] + jnp.dot(pb, vs, preferred_element_type=F32)
            m_scr[a] = m_new

    def loop_body(j, carry):
        step(j, False)
        return carry

    lax.fori_loop(0, qi, loop_body, 0)
    step(qi, True)

    lp = lam_ref[...]
    lam = (jnp.exp(jnp.sum(lp[0:1] * lp[1:2])) - jnp.exp(jnp.sum(lp[2:3] * lp[3:4]))
           + lam_init)
    o = acc_scr[0] / l_scr[0] - lam * (acc_scr[1] / l_scr[1])
    ms = jnp.mean(o * o, axis=-1, keepdims=True)
    y = o * lax.rsqrt(ms + RMS_EPS) * g_ref[...]
    o_ref[...] = (y * (1.0 - lam_init)).astype(o_ref.dtype)


def diff_attention(p, bias_tiles, lam_params, ln_g, lam_init, batch, seq):
    t = ATT_BLOCK
    n_tiles = bias_tiles.shape[1]
    cq = 3 * A_WIDTH // LANES
    ck = cq + B_QK_WIDTH // LANES
    cv = ck + B_QK_WIDTH // LANES
    return pl.pallas_call(
        functools.partial(_diff_body, n_tiles=n_tiles, lam_init=lam_init),
        grid=(batch, B_HEADS, seq // t),
        in_specs=[
            pl.BlockSpec((None, t, LANES), lambda b, h, i: (b, i, cq + h)),
            pl.BlockSpec((None, seq, LANES), lambda b, h, i: (b, 0, ck + h)),
            pl.BlockSpec((None, seq, LANES), lambda b, h, i: (b, 0, cv + h)),
            pl.BlockSpec((None, n_tiles, t, t), lambda b, h, i: (h, 0, 0, 0)),
            pl.BlockSpec((4, DIFF_HALF), lambda b, h, i: (0, 0)),
            pl.BlockSpec((1, DIFF_VDIM), lambda b, h, i: (0, 0)),
        ],
        out_specs=pl.BlockSpec((None, t, LANES), lambda b, h, i: (b, i, h)),
        out_shape=jax.ShapeDtypeStruct((batch, seq, B_V_WIDTH), BF16),
        scratch_shapes=[pltpu.VMEM((2, t, LANES), F32), pltpu.VMEM((2, t, LANES), F32),
                        pltpu.VMEM((2, t, DIFF_VDIM), F32)],
        compiler_params=pltpu.CompilerParams(
            dimension_semantics=("parallel", "parallel", "arbitrary")),
        name="diff_attention",
    )(p, p, p, bias_tiles, lam_params, ln_g.reshape(1, DIFF_VDIM))


def diff_bias_tiles(rel_bias_b, seq):
    t = ATT_BLOCK
    bucket = _t5_bucket_table(max(seq, 2 * MAX_DISTANCE) + 2 * t)
    sat = bucket[-1]
    d_sat = int(np.max(np.nonzero(bucket != sat)[0])) + 1
    n_full = (d_sat + t - 1 + t - 1) // t
    n_tiles = n_full + 1
    assert n_full * t - (t - 1) >= d_sat
    n = np.arange(2 * t)[None, :]
    base = np.arange(n_tiles)[:, None] * t
    dist = np.clip(np.where(n < t, base - n, base + 2 * t - n), 0, None)
    w = rel_bias_b.T[:, bucket[dist]].astype(F32)
    rep = jnp.broadcast_to(w[:, :, None, :], (B_HEADS, n_tiles, t, 2 * t))
    flat = rep.reshape(B_HEADS, n_tiles, 2 * t * t)[:, :, :t * (2 * t - 1)]
    return flat.reshape(B_HEADS, n_tiles, t, 2 * t - 1)[:, :, :, :t]


def _logsig_cumsum_body(f_ref, b_ref, c_ref, carry_scr):
    t = f_ref.shape[0]

    @pl.when(pl.program_id(1) == 0)
    def _():
        carry_scr[...] = jnp.zeros_like(carry_scr)

    x = f_ref[...] + b_ref[...]
    ls = jnp.minimum(x, 0.0) - jnp.log1p(jnp.exp(-jnp.abs(x)))
    row = lax.broadcasted_iota(I32, (t, t), 0)
    col = lax.broadcasted_iota(I32, (t, t), 1)
    tri = (row >= col).astype(F32)
    c = jnp.dot(tri, ls, preferred_element_type=F32,
                precision=lax.Precision.HIGHEST) + carry_scr[...]
    c_ref[...] = c
    carry_scr[...] = c[t - 1:t, :]


def logsig_cumsum(fg, b_f, batch, seq, t=512):
    return pl.pallas_call(
        _logsig_cumsum_body,
        grid=(batch, seq // t),
        in_specs=[pl.BlockSpec((None, t, LANES), lambda b, i: (b, i, 0)),
                  pl.BlockSpec((1, LANES), lambda b, i: (0, 0))],
        out_specs=pl.BlockSpec((None, t, LANES), lambda b, i: (b, i, 0)),
        out_shape=jax.ShapeDtypeStruct((batch, seq, LANES), F32),
        scratch_shapes=[pltpu.VMEM((1, LANES), F32)],
        compiler_params=pltpu.CompilerParams(
            dimension_semantics=("parallel", "arbitrary")),
        name="logsig_cumsum",
    )(fg, b_f)


def _fox_body(q_ref, k_ref, v_ref, cq_ref, ck_ref, o_ref, m_scr, acc_scr):
    t = ATT_BLOCK
    qi = pl.program_id(2)
    q = q_ref[...] * QK_SCALE
    lane = lax.broadcasted_iota(I32, (1, LANES), 1)
    own = [(lane < HEAD_DIM) == (a == 0) for a in range(2)]
    qa = [jnp.where(own[a], q, jnp.zeros_like(q)) for a in range(2)]
    cqb = [jnp.broadcast_to(cq_ref[a], (t, LANES)) for a in range(2)]
    for a in range(2):
        m_scr[a] = jnp.full((t, LANES), NEG_INF, F32)
        acc_scr[a] = jnp.zeros((t, LANES), F32)

    def step(j, masked):
        off = pl.multiple_of(j * t, t)
        ks = k_ref[pl.ds(off, t), :]
        vs = v_ref[pl.ds(off, t), :]
        if masked:
            row = lax.broadcasted_iota(I32, (t, t), 0)
            col = lax.broadcasted_iota(I32, (t, t), 1)
            causal = row >= col
        for a in range(2):
            s = lax.dot_general(qa[a], ks, (((1,), (1,)), ((), ())),
                                preferred_element_type=F32)
            s = s - ck_ref[a, :, pl.ds(off, t)]
            if masked:
                s = jnp.where(causal, s, NEG_INF)
            m_prev = m_scr[a]
            m_new = jnp.maximum(m_prev, jnp.max(s, axis=-1, keepdims=True) + cqb[a])
            alpha = jnp.exp(m_prev - m_new)
            shift = m_new - cqb[a]
            pb = jnp.exp(s - jnp.concatenate([shift] * (t // LANES), axis=1)).astype(BF16)
            v_aug = jnp.where(own[a], vs, jnp.ones_like(vs))
            acc_scr[a] = alpha * acc_scr[a] + jnp.dot(pb, v_aug, preferred_element_type=F32)
            m_scr[a] = m_new

    def loop_body(j, carry):
        step(j, False)
        return carry

    lax.fori_loop(0, qi, loop_body, 0)
    step(qi, True)
    r = [acc_scr[a] / pltpu.roll(acc_scr[a], HEAD_DIM, 1) for a in range(2)]
    o_ref[...] = jnp.where(own[0], r[0], r[1]).astype(o_ref.dtype)


def fox_attention(p, cq, ck, batch, seq):
    t = ATT_BLOCK
    nk = C_WIDTH // LANES
    return pl.pallas_call(
        _fox_body,
        grid=(batch, C_HEADS // 2, seq // t),
        in_specs=[
            pl.BlockSpec((None, t, LANES), lambda b, h, i: (b, i, h)),
            pl.BlockSpec((None, seq, LANES), lambda b, h, i: (b, 0, nk + h)),
            pl.BlockSpec((None, seq, LANES), lambda b, h, i: (b, 0, 2 * nk + h)),
            pl.BlockSpec((None, 2, t, 1), lambda b, h, i: (b, h, i, 0)),
            pl.BlockSpec((None, 2, 1, seq), lambda b, h, i: (b, h, 0, 0)),
        ],
        out_specs=pl.BlockSpec((None, t, LANES), lambda b, h, i: (b, i, h)),
        out_shape=jax.ShapeDtypeStruct((batch, seq, C_WIDTH), BF16),
        scratch_shapes=[pltpu.VMEM((2, t, LANES), F32), pltpu.VMEM((2, t, LANES), F32)],
        compiler_params=pltpu.CompilerParams(
            dimension_semantics=("parallel", "parallel", "arbitrary")),
        name="fox_attention",
    )(p, p, p, cq, ck)


def _even_out_body(o1, l1, o2, l2, o3, l3, ob_ref, w_ref, x_ref, out_ref, a_scr):
    @pl.when(pl.program_id(1) == 0)
    def _():
        a1, a2, a3 = l1[...], l2[...], l3[...]
        mx = jnp.maximum(jnp.maximum(a1, a2), a3)
        e1, e2, e3 = jnp.exp(a1 - mx), jnp.exp(a2 - mx), jnp.exp(a3 - mx)
        oa = (e1 * o1[...] + e2 * o2[...] + e3 * o3[...]) / (e1 + e2 + e3)
        a_scr[:, :A_WIDTH] = oa.astype(BF16)
        a_scr[:, A_WIDTH:] = ob_ref[...]

    out_ref[...] = x_ref[...] + jnp.dot(a_scr[...], w_ref[...], preferred_element_type=F32)


def even_out_proj(branches, ob, w, x, tm=512, tn=512):
    m, d = x.shape
    half = pl.BlockSpec((tm, A_WIDTH), lambda i, j: (i, 0))
    flat = [a for pair in branches for a in pair]
    return pl.pallas_call(
        _even_out_body,
        grid=(m // tm, d // tn),
        in_specs=[half] * 6 + [
            pl.BlockSpec((tm, B_V_WIDTH), lambda i, j: (i, 0)),
            pl.BlockSpec((A_WIDTH + B_V_WIDTH, tn), lambda i, j: (0, j)),
            pl.BlockSpec((tm, tn), lambda i, j: (i, j))],
        out_specs=pl.BlockSpec((tm, tn), lambda i, j: (i, j)),
        out_shape=jax.ShapeDtypeStruct((m, d), F32),
        scratch_shapes=[pltpu.VMEM((tm, A_WIDTH + B_V_WIDTH), BF16)],
        compiler_params=pltpu.CompilerParams(
            dimension_semantics=("parallel", "arbitrary")),
        name="even_out_proj",
    )(*flat, ob, w, x)


def _matmul_res_body(a_ref, w_ref, x_ref, o_ref):
    o_ref[...] = x_ref[...] + jnp.dot(a_ref[...], w_ref[...], preferred_element_type=F32)


def matmul_residual(a, w, x, tm=512, tn=512):
    m, k = a.shape
    n = w.shape[1]
    return pl.pallas_call(
        _matmul_res_body,
        grid=(m // tm, n // tn),
        in_specs=[pl.BlockSpec((tm, k), lambda i, j: (i, 0)),
                  pl.BlockSpec((k, tn), lambda i, j: (0, j)),
                  pl.BlockSpec((tm, tn), lambda i, j: (i, j))],
        out_specs=pl.BlockSpec((tm, tn), lambda i, j: (i, j)),
        out_shape=jax.ShapeDtypeStruct((m, n), F32),
        compiler_params=pltpu.CompilerParams(
            dimension_semantics=("parallel", "parallel")),
        name="matmul_residual",
    )(a, w, x)


def _add_norm_body(x_ref, r_ref, g_ref, o_ref):
    x = x_ref[...] + r_ref[...]
    ms = jnp.mean(x * x, axis=-1, keepdims=True)
    o_ref[...] = x * lax.rsqrt(ms + RMS_EPS) * g_ref[...]


def add_norm(x, r, g, tm=512):
    m, d = x.shape
    row = pl.BlockSpec((tm, d), lambda i: (i, 0))
    return pl.pallas_call(
        _add_norm_body,
        grid=(m // tm,),
        in_specs=[row, row, pl.BlockSpec((1, d), lambda i: (0, 0))],
        out_specs=row,
        out_shape=jax.ShapeDtypeStruct((m, d), F32),
        compiler_params=pltpu.CompilerParams(dimension_semantics=("parallel",)),
        name="add_norm",
    )(x, r, g.reshape(1, d))


PEER_CAND_ROWS = PEER_TOPK + 8 * (PEER_TOPK - 1)


def _peer_topk_body(q_ref, sk_ref, idx_ref, gate_ref, ts_scr, ti_scr, bs_scr, be_scr,
                    gt_scr, it_scr):
    tm = q_ref.shape[0]
    neg_inf = jnp.float32(-jnp.inf)
    key_id = lax.broadcasted_iota(I32, (N_KEYS, tm), 0)

    def pair_body(pr, carry):
        off = pl.multiple_of(pr * PEER_KEY_HALF, PEER_KEY_HALF)
        sc = lax.dot_general(sk_ref[pr], q_ref[:, pl.ds(off, PEER_KEY_HALF)],
                             (((1,), (1,)), ((), ())), preferred_element_type=F32)

        def k_body(k, vals):
            m = jnp.max(vals, axis=0, keepdims=True)
            sel = jnp.min(jnp.where(vals == m, key_id, N_KEYS), axis=0, keepdims=True)
            ts_scr[pr, pl.ds(k, 1), :] = m
            ti_scr[pr, pl.ds(k, 1), :] = sel
            return jnp.where(key_id == sel, neg_inf, vals)

        lax.fori_loop(0, PEER_TOPK, k_body, sc)
        return carry

    lax.fori_loop(0, 2 * PEER_HEADS, pair_body, 0)

    r = lax.broadcasted_iota(I32, (PEER_CAND_ROWS, 1), 0)
    cand_id = jnp.where(r < PEER_TOPK, r,
                        (1 + (r - PEER_TOPK) // 8) * PEER_TOPK + (r - PEER_TOPK) % 8)

    def head_body(h, carry):
        s1, s2 = ts_scr[2 * h], ts_scr[2 * h + 1]
        i1, i2 = ti_scr[2 * h] * N_KEYS, ti_scr[2 * h + 1]
        vals = jnp.concatenate(
            [s1[0:1] + s2] + [s1[a:a + 1] + s2[0:8] for a in range(1, PEER_TOPK)], axis=0)
        eidx = jnp.concatenate(
            [i1[0:1] + i2] + [i1[a:a + 1] + i2[0:8] for a in range(1, PEER_TOPK)], axis=0)

        def k_body(k, vals):
            m = jnp.max(vals, axis=0, keepdims=True)
            sel = jnp.min(jnp.where(vals == m, cand_id, PEER_TOPK * PEER_TOPK),
                          axis=0, keepdims=True)
            hit = cand_id == sel
            bs_scr[pl.ds(k, 1), :] = m
            be_scr[pl.ds(k, 1), :] = jnp.sum(jnp.where(hit, eidx, 0), axis=0, keepdims=True)
            return jnp.where(hit, neg_inf, vals)

        lax.fori_loop(0, PEER_TOPK, k_body, vals)
        bs = bs_scr[...]
        e = jnp.exp(bs - jnp.max(bs, axis=0, keepdims=True))
        row0 = pl.multiple_of(h * PEER_TOPK, PEER_TOPK)
        gt_scr[pl.ds(row0, PEER_TOPK), :] = e / jnp.sum(e, axis=0, keepdims=True)
        it_scr[pl.ds(row0, PEER_TOPK), :] = be_scr[...]
        return carry

    lax.fori_loop(0, PEER_HEADS, head_body, 0)
    gate_ref[...] = gt_scr[...].T
    idx_ref[...] = it_scr[...].T


def peer_topk(q, subkeys, tm=256):
    m = q.shape[0]
    n_sel = PEER_HEADS * PEER_TOPK
    out_spec = pl.BlockSpec((tm, n_sel), lambda i: (i, 0))
    return pl.pallas_call(
        _peer_topk_body,
        grid=(m // tm,),
        in_specs=[pl.BlockSpec((tm, q.shape[1]), lambda i: (i, 0)),
                  pl.BlockSpec(subkeys.shape, lambda i: (0, 0, 0))],
        out_specs=[out_spec, out_spec],
        out_shape=[jax.ShapeDtypeStruct((m, n_sel), I32),
                   jax.ShapeDtypeStruct((m, n_sel), F32)],
        scratch_shapes=[pltpu.VMEM((2 * PEER_HEADS, PEER_TOPK, tm), F32),
                        pltpu.VMEM((2 * PEER_HEADS, PEER_TOPK, tm), I32),
                        pltpu.VMEM((PEER_TOPK, tm), F32),
                        pltpu.VMEM((PEER_TOPK, tm), I32),
                        pltpu.VMEM((n_sel, tm), F32),
                        pltpu.VMEM((n_sel, tm), I32)],
        compiler_params=pltpu.CompilerParams(dimension_semantics=("parallel",)),
        name="peer_topk",
    )(q, subkeys)


def _gelu_gate_body(a_ref, g_ref, o_ref):
    a = a_ref[...]
    o_ref[...] = g_ref[...] * (0.5 * a * (1.0 + lax.erf(a * (2.0 ** -0.5))))


def gelu_gate(act, gate, tm=2048):
    m, n = act.shape
    spec = pl.BlockSpec((tm, n), lambda i: (i, 0))
    return pl.pallas_call(
        _gelu_gate_body,
        grid=(m // tm,),
        in_specs=[spec, spec],
        out_specs=spec,
        out_shape=jax.ShapeDtypeStruct((m, n), F32),
        compiler_params=pltpu.CompilerParams(dimension_semantics=("parallel",)),
        name="gelu_gate",
    )(act, gate)


SC_TOK_CHUNK = 32
SC_RING = 8
SC_BF16_GROUP = 4
SC_FMT = plsc.PackFormat.INTERLEAVED


def _sc_worker_id():
    return lax.axis_index("s") * SC_CORES + lax.axis_index("c")


def pack_bf16_pairs(t):
    half = t.shape[-1] // 2
    bits = lax.bitcast_convert_type(t.astype(BF16).astype(F32), I32)
    return (bits[..., half:] & jnp.int32(-65536)) | lax.shift_right_logical(
        bits[..., :half], jnp.int32(16))


def _sc_row_pipeline(idx_v, table_hbm, rows_v, sems, n_items, groups, compute):
    def gather(item):
        tt, g = item // groups, item % groups
        ids = idx_v[tt, pl.ds(g * SC_LANES, SC_LANES)]
        slot = item % SC_RING
        return pltpu.make_async_copy(table_hbm.at[ids], rows_v.at[slot], sems.at[slot])

    for s in range(SC_RING - 1):
        gather(s).start()

    def item_body(item, carry):
        nxt = item + SC_RING - 1

        @pl.when(nxt < n_items)
        def _():
            gather(nxt).start()

        gather(item).wait()
        compute(item // groups, item % groups, item % SC_RING)
        return carry

    lax.fori_loop(0, n_items, item_body, 0)


def peer_expert_dots(hp, idx, up):
    m, dw = hp.shape
    n_sel = idx.shape[1]
    per_w = m // SC_WORKERS
    n_chunks = per_w // SC_TOK_CHUNK
    groups = n_sel // SC_LANES
    step = SC_BF16_GROUP * SC_LANES
    mesh = plsc.VectorSubcoreMesh(core_axis_name="c", subcore_axis_name="s")

    @functools.partial(
        pl.kernel, mesh=mesh,
        out_type=jax.ShapeDtypeStruct((m, n_sel), F32),
        scratch_types=[
            pltpu.VMEM((SC_TOK_CHUNK, n_sel), I32),
            pltpu.VMEM((SC_TOK_CHUNK, dw), I32),
            pltpu.VMEM((SC_TOK_CHUNK, n_sel), F32),
            pltpu.VMEM((SC_RING, SC_LANES, dw), I32),
            pltpu.VMEM((SC_LANES * SC_LANES,), F32),
            pltpu.SemaphoreType.DMA((SC_RING,)),
        ],
        compiler_params=pltpu.CompilerParams(needs_layout_passes=False),
        name="peer_expert_dots",
    )
    def k(h_hbm, idx_hbm, u_hbm, act_hbm, idx_v, h_v, act_v, rows_v, part_v, sems):
        base = _sc_worker_id() * per_w
        lane = lax.broadcasted_iota(I32, (SC_LANES,), 0)

        def compute(tt, g, slot):
            def grp_body(q, accs):
                off = pl.multiple_of(q * step, step)
                xs = [plsc.bitcast(h_v[tt, pl.ds(off + c * SC_LANES, SC_LANES)], BF16)
                      for c in range(SC_BF16_GROUP)]
                new = []
                for e in range(SC_LANES):
                    s = None
                    for c in range(SC_BF16_GROUP):
                        p = plsc.bitcast(
                            rows_v[slot, e, pl.ds(off + c * SC_LANES, SC_LANES)], BF16) * xs[c]
                        s = p if s is None else s + p
                    lo, hi = plsc.unpack(s, format=SC_FMT)
                    new.append(accs[e] + (lo + hi))
                return tuple(new)

            accs = lax.fori_loop(
                0, dw // step, grp_body,
                tuple(jnp.zeros((SC_LANES,), F32) for _ in range(SC_LANES)))
            for e in range(SC_LANES):
                part_v[pl.ds(e * SC_LANES, SC_LANES)] = accs[e]
            tot = jnp.zeros((SC_LANES,), F32)
            for l in range(SC_LANES):
                tot = tot + plsc.load_gather(part_v, [lane * SC_LANES + l])
            act_v[tt, pl.ds(g * SC_LANES, SC_LANES)] = tot

        def chunk_body(c, carry):
            t0 = base + c * SC_TOK_CHUNK
            pltpu.sync_copy(idx_hbm.at[pl.ds(t0, SC_TOK_CHUNK)], idx_v)
            pltpu.sync_copy(h_hbm.at[pl.ds(t0, SC_TOK_CHUNK)], h_v)
            _sc_row_pipeline(idx_v, u_hbm, rows_v, sems, SC_TOK_CHUNK * groups, groups, compute)
            pltpu.sync_copy(act_v, act_hbm.at[pl.ds(t0, SC_TOK_CHUNK)])
            return carry

        lax.fori_loop(0, n_chunks, chunk_body, 0)

    return k(hp, idx, up)


def peer_expert_combine(w, idx, vp):
    m, n_sel = w.shape
    dw = vp.shape[1]
    d = 2 * dw
    per_w = m // SC_WORKERS
    n_chunks = per_w // SC_TOK_CHUNK
    n_vec = d // SC_LANES
    groups = n_sel // SC_LANES
    mesh = plsc.VectorSubcoreMesh(core_axis_name="c", subcore_axis_name="s")

    @functools.partial(
        pl.kernel, mesh=mesh,
        out_type=jax.ShapeDtypeStruct((m, d), F32),
        scratch_types=[
            pltpu.VMEM((SC_TOK_CHUNK, n_sel), I32),
            pltpu.VMEM((SC_TOK_CHUNK, n_sel), F32),
            pltpu.VMEM((SC_TOK_CHUNK, d), F32),
            pltpu.VMEM((SC_RING, SC_LANES, dw), I32),
            pltpu.SemaphoreType.DMA((SC_RING,)),
        ],
        compiler_params=pltpu.CompilerParams(needs_layout_passes=False),
        name="peer_expert_combine",
    )
    def k(w_hbm, idx_hbm, v_hbm, out_hbm, idx_v, w_v, out_v, rows_v, sems):
        base = _sc_worker_id() * per_w

        def compute(tt, g, slot):
            splat = []
            for e in range(SC_LANES):
                s = plsc.load_gather(w_v, [jnp.full((SC_LANES,), tt, I32),
                                           jnp.full((SC_LANES,), g * SC_LANES + e, I32)])
                splat.append(plsc.pack(s, s, format=SC_FMT))

            @plsc.parallel_loop(0, dw // SC_LANES)
            def _(j):
                off = pl.multiple_of(j * SC_LANES, SC_LANES)
                acc_lo = out_v[tt, pl.ds(off, SC_LANES)]
                acc_hi = out_v[tt, pl.ds(dw + off, SC_LANES)]
                for e0 in range(0, SC_LANES, SC_BF16_GROUP):
                    s = None
                    for e in range(e0, e0 + SC_BF16_GROUP):
                        p = plsc.bitcast(rows_v[slot, e, pl.ds(off, SC_LANES)], BF16) * splat[e]
                        s = p if s is None else s + p
                    lo, hi = plsc.unpack(s, format=SC_FMT)
                    acc_lo = acc_lo + lo
                    acc_hi = acc_hi + hi
                out_v[tt, pl.ds(off, SC_LANES)] = acc_lo
                out_v[tt, pl.ds(dw + off, SC_LANES)] = acc_hi

        def chunk_body(c, carry):
            t0 = base + c * SC_TOK_CHUNK
            pltpu.sync_copy(idx_hbm.at[pl.ds(t0, SC_TOK_CHUNK)], idx_v)
            pltpu.sync_copy(w_hbm.at[pl.ds(t0, SC_TOK_CHUNK)], w_v)

            def zero_body(z, carry2):
                tt, j = z // n_vec, z % n_vec
                out_v[tt, pl.ds(pl.multiple_of(j * SC_LANES, SC_LANES), SC_LANES)] = (
                    jnp.zeros((SC_LANES,), F32))
                return carry2

            lax.fori_loop(0, SC_TOK_CHUNK * n_vec, zero_body, 0)
            _sc_row_pipeline(idx_v, v_hbm, rows_v, sems, SC_TOK_CHUNK * groups, groups, compute)
            pltpu.sync_copy(out_v, out_hbm.at[pl.ds(t0, SC_TOK_CHUNK)])
            return carry

        lax.fori_loop(0, n_chunks, chunk_body, 0)

    return k(w, idx, vp)


def kernel(x, norm_mix_g, norm_ffn_g, final_norm_g, rel_bias, even_w_in, even_w_out,
           diff_lambda, diff_ln_g, odd_w_in, odd_b_f, odd_w_out, peer_wq, peer_subkeys,
           peer_u, peer_v):
    batch, seq, d = x.shape

    dil_tiles = [dilated_bias_tile(rel_bias[:, :A_HEADS], w, dl) for w, dl in DILATED_BRANCHES]
    diff_tiles = diff_bias_tiles(rel_bias[:, A_HEADS:], seq)
    lam_init = 0.8 - 0.6 * math.exp(-0.3 * 0)
    even_in, even_out = even_w_in[0].astype(BF16), even_w_out[0].astype(BF16)
    w_in = odd_w_in[0]
    odd_in, odd_out = w_in[:, :3 * C_WIDTH].astype(BF16), odd_w_out[0].astype(BF16)
    w_gate = jnp.pad(w_in[:, 3 * C_WIDTH:], ((0, 0), (0, LANES - C_HEADS)))
    b_f = jnp.pad(odd_b_f[0], (0, LANES - C_HEADS)).reshape(1, LANES)
    peer = [dict(wq=peer_wq[l].astype(BF16),
                 sk=peer_subkeys[l].reshape(2 * PEER_HEADS, N_KEYS, PEER_KEY_HALF).astype(BF16),
                 up=pack_bf16_pairs(peer_u[l]), vp=pack_bf16_pairs(peer_v[l]))
            for l in range(2)]

    bg = batch // BATCH_GROUPS
    m = bg * seq
    groups = [dict(x=x[g * bg:(g + 1) * bg].reshape(m, d)) for g in range(BATCH_GROUPS)]

    def peer_select(st, layer):
        q, st["hp"] = norm_matmul(st["x"], norm_ffn_g[layer], peer[layer]["wq"], want_h="packed")
        st["idx"], st["gate"] = peer_topk(q, peer[layer]["sk"])
        st["act"] = peer_expert_dots(st["hp"], st["idx"], peer[layer]["up"])

    def peer_combine(st, layer):
        w = gelu_gate(st["act"], st["gate"])
        st["peer"] = peer_expert_combine(w, st["idx"], peer[layer]["vp"])

    def even_layer(st):
        (p,) = norm_matmul(st["x"], norm_mix_g[0], even_in)
        branches = [dilated_branch(p, tile, dl, bg, seq)
                    for tile, (_, dl) in zip(dil_tiles, DILATED_BRANCHES)]
        ob = diff_attention(p.reshape(bg, seq, -1), diff_tiles, diff_lambda[0], diff_ln_g[0],
                            lam_init, bg, seq)
        st["x"] = even_out_proj(branches, ob.reshape(m, B_V_WIDTH), even_out, st["x"])
        peer_select(st, 0)

    def odd_layer(st):
        p, st["x"], h = norm_matmul(st["x"], norm_mix_g[1], odd_in, res=st["peer"], want_h="f32")
        fg = gate_matmul(h, w_gate)
        c = logsig_cumsum(fg.reshape(bg, seq, LANES), b_f, bg, seq)
        ct = c[:, :, :C_HEADS].transpose(0, 2, 1)
        o = fox_attention(p.reshape(bg, seq, -1), ct[:, :, :, None], ct[:, :, None, :], bg, seq)
        st["x"] = matmul_residual(o.reshape(m, C_WIDTH), odd_out, st["x"])
        peer_select(st, 1)

    for st in groups:
        even_layer(st)
    for st in groups:
        peer_combine(st, 0)
    for st in groups:
        odd_layer(st)
    for st in groups:
        peer_combine(st, 1)
    outs = [add_norm(st["x"], st["peer"], final_norm_g).reshape(bg, seq, d) for st in groups]
    return jnp.concatenate(outs, axis=0)


def _gate_matmul_body(h_ref, w_ref, o_ref):
    o_ref[...] = jnp.dot(h_ref[...], w_ref[...], preferred_element_type=F32,
                         precision=lax.Precision.HIGHEST)


def gate_matmul(h, w, tm=512):
    m, d = h.shape
    n = w.shape[1]
    return pl.pallas_call(
        _gate_matmul_body,
        grid=(m // tm,),
        in_specs=[pl.BlockSpec((tm, d), lambda i: (i, 0)),
                  pl.BlockSpec((d, n), lambda i: (0, 0))],
        out_specs=pl.BlockSpec((tm, n), lambda i: (i, 0)),
        out_shape=jax.ShapeDtypeStruct((m, n), F32),
        compiler_params=pltpu.CompilerParams(dimension_semantics=("parallel",)),
        name="gate_matmul",
    )(h, w)
```

```python
import functools
import math

import numpy as np
import jax
import jax.numpy as jnp
from jax import lax
from jax.experimental import pallas as pl
from jax.experimental.pallas import tpu as pltpu
from jax.experimental.pallas import tpu_sc as plsc

F32 = jnp.float32
BF16 = jnp.bfloat16
I32 = jnp.int32

D_MODEL = 1024
HEAD_DIM = 64
A_HEADS = 8
DILATED_BRANCHES = ((128, 1), (512, 4), (2048, 16))
DIFF_HALF = 64
DIFF_VDIM = 128
B_HEADS = 4
C_HEADS = 16
N_BUCKETS = 32
MAX_DISTANCE = 2048
PEER_HEADS = 8
N_KEYS = 128
PEER_TOPK = 16
PEER_KEY_HALF = 128
RMS_EPS = 1e-6
NEG_INF = -1e30
A_WIDTH = A_HEADS * HEAD_DIM
B_QK_WIDTH = B_HEADS * 2 * DIFF_HALF
B_V_WIDTH = B_HEADS * DIFF_VDIM
C_WIDTH = C_HEADS * HEAD_DIM
QK_SCALE = 0.125
LANES = 128
DIL_BLOCK = 128
ATT_BLOCK = 512
BATCH_GROUPS = 4

SC_CORES = 2
SC_SUBCORES = 16
SC_LANES = 16
SC_WORKERS = SC_CORES * SC_SUBCORES


def _t5_bucket_table(n):
    max_exact = N_BUCKETS // 2
    d = np.arange(n)
    df = np.maximum(d, 1).astype(np.float32)
    large = max_exact + (
        np.log(df / np.float32(max_exact)) / np.float32(math.log(MAX_DISTANCE / max_exact))
        * np.float32(N_BUCKETS - max_exact)).astype(np.int32)
    large = np.minimum(large, N_BUCKETS - 1)
    return np.where(d < max_exact, d, large).astype(np.int32)


def _norm_matmul_body(*refs, has_res, want_h):
    it = iter(refs)
    x_ref = next(it)
    r_ref = next(it) if has_res else None
    g_ref = next(it)
    w_ref = next(it)
    o_ref = next(it)
    xs_ref = next(it) if has_res else None
    hout_ref = next(it) if want_h else None
    h_scr = next(it)

    @pl.when(pl.program_id(1) == 0)
    def _():
        x = x_ref[...]
        if has_res:
            x = x + r_ref[...]
            xs_ref[...] = x
        ms = jnp.mean(x * x, axis=-1, keepdims=True)
        h = x * lax.rsqrt(ms + RMS_EPS) * g_ref[...]
        if want_h == "f32":
            hout_ref[...] = h
        elif want_h == "packed":
            hout_ref[...] = pack_bf16_pairs(h)
        h_scr[...] = h.astype(BF16)

    o_ref[...] = jnp.dot(h_scr[...], w_ref[...],
                         preferred_element_type=F32).astype(o_ref.dtype)


def norm_matmul(x, g, w, *, res=None, want_h=None, out_dtype=BF16, tm=512, tn=512):
    m, d = x.shape
    n = w.shape[1]
    tn = min(tn, n)
    row = pl.BlockSpec((tm, d), lambda i, j: (i, 0))
    in_specs = [row] + ([row] if res is not None else []) + [
        pl.BlockSpec((1, d), lambda i, j: (0, 0)),
        pl.BlockSpec((d, tn), lambda i, j: (0, j))]
    out_specs = [pl.BlockSpec((tm, tn), lambda i, j: (i, j))]
    out_shape = [jax.ShapeDtypeStruct((m, n), out_dtype)]
    if res is not None:
        out_specs.append(row)
        out_shape.append(jax.ShapeDtypeStruct((m, d), F32))
    if want_h == "f32":
        out_specs.append(row)
        out_shape.append(jax.ShapeDtypeStruct((m, d), F32))
    elif want_h == "packed":
        out_specs.append(pl.BlockSpec((tm, d // 2), lambda i, j: (i, 0)))
        out_shape.append(jax.ShapeDtypeStruct((m, d // 2), I32))
    args =[x] + ([res] if res is not None else []) + [g.reshape(1, d), w]
    return pl.pallas_call(
        functools.partial(_norm_matmul_body, has_res=res is not None, want_h=want_h),
        grid=(m // tm, n // tn),
        in_specs=in_specs,
        out_specs=out_specs,
        out_shape=out_shape,
        scratch_shapes=[pltpu.VMEM((tm, d), BF16)],
        compiler_params=pltpu.CompilerParams(
            dimension_semantics=("parallel", "arbitrary")),
        name="norm_matmul",
    )(*args)


def _dilated_body(q_ref, kp_ref, kc_ref, vp_ref, vc_ref, b_ref, o_ref, lse_ref):
    i = pl.program_id(3)
    q = q_ref[...] * QK_SCALE
    k = jnp.concatenate([kp_ref[...], kc_ref[...]], axis=0)
    v = jnp.concatenate([vp_ref[...], vc_ref[...]], axis=0)
    col = lax.broadcasted_iota(I32, (DIL_BLOCK, 2 * DIL_BLOCK), 1)
    has_prev = jnp.logical_or(col >= DIL_BLOCK, i > 0)
    outs, lses = [], []
    for hh in range(2):
        sl = slice(hh * HEAD_DIM, (hh + 1) * HEAD_DIM)
        s = lax.dot_general(q[:, sl], k[:, sl], (((1,), (1,)), ((), ())),
                            preferred_element_type=F32)
        s = jnp.where(has_prev, s + b_ref[hh], NEG_INF)
        m = jnp.max(s, axis=-1, keepdims=True)
        p = jnp.exp(s - m)
        l = jnp.sum(p, axis=-1, keepdims=True)
        o = jnp.dot(p.astype(BF16), v[:, sl], preferred_element_type=F32) / l
        outs.append(o)
        lses.append(jnp.broadcast_to(m + jnp.log(l), (DIL_BLOCK, HEAD_DIM)))
    o_ref[...] = jnp.concatenate(outs, axis=1)
    lse_ref[...] = jnp.concatenate(lses, axis=1)


def dilated_branch(p, bias_tile, dil, batch, seq):
    n_cols = p.shape[1]
    cb = n_cols // LANES
    rows = seq // dil
    nblk = rows // DIL_BLOCK
    pv = p.reshape(batch, rows, dil * n_cols)
    kq, kk, kv = 0, A_WIDTH // LANES, 2 * A_WIDTH // LANES
    blk = (None, DIL_BLOCK, LANES)

    def spec(col0, prev):
        if prev:
            return pl.BlockSpec(blk, lambda b, h, r, i: (b, jnp.maximum(i - 1, 0), r * cb + col0 + h))
        return pl.BlockSpec(blk, lambda b, h, r, i: (b, i, r * cb + col0 + h))

    ocb = A_WIDTH // LANES
    ospec = pl.BlockSpec(blk, lambda b, h, r, i: (b, i, r * ocb + h))
    oshape = jax.ShapeDtypeStruct((batch, rows, dil * A_WIDTH), F32)
    o, lse = pl.pallas_call(
        _dilated_body,
        grid=(batch, A_HEADS // 2, dil, nblk),
        in_specs=[spec(kq, False), spec(kk, True), spec(kk, False),
                  spec(kv, True), spec(kv, False),
                  pl.BlockSpec((2, DIL_BLOCK, 2 * DIL_BLOCK), lambda b, h, r, i: (h, 0, 0))],
        out_specs=[ospec, ospec],
        out_shape=[oshape, oshape],
        compiler_params=pltpu.CompilerParams(
            dimension_semantics=("parallel", "parallel", "parallel", "arbitrary")),
        name=f"dilated_d{dil}",
    )(pv, pv, pv, pv, pv, bias_tile)
    return o.reshape(batch * seq, A_WIDTH), lse.reshape(batch * seq, A_WIDTH)


def dilated_bias_tile(rel_bias_a, window, dil):
    n = window // dil
    assert n == DIL_BLOCK
    bucket = _t5_bucket_table(window + 1)
    row = np.arange(DIL_BLOCK)[:, None]
    c = np.arange(2 * DIL_BLOCK)[None, :]
    j = DIL_BLOCK + row - c
    valid = (j >= 0) & (j <= n)
    bk = bucket[np.clip(j, 0, n) * dil]
    tile = rel_bias_a.T[:, bk]
    return jnp.where(jnp.asarray(valid)[None], tile, NEG_INF).astype(F32)


def _diff_body(q_ref, k_ref, v_ref, b_ref, lam_ref, g_ref, o_ref,
               m_scr, l_scr, acc_scr, *, n_tiles, lam_init):
    t = ATT_BLOCK
    qi = pl.program_id(2)
    q = q_ref[...] * QK_SCALE
    lane = lax.broadcasted_iota(I32, (1, LANES), 1)
    qa = [jnp.where((lane < DIFF_HALF) == (a == 0), q, jnp.zeros_like(q)) for a in range(2)]
    ones = jnp.ones((t, LANES), BF16)
    for a in range(2):
        m_scr[a] = jnp.full((t, LANES), NEG_INF, F32)
        l_scr[a] = jnp.zeros((t, LANES), F32)
        acc_scr[a] = jnp.zeros((t, DIFF_VDIM), F32)

    def step(j, masked):
        off = pl.multiple_of(j * t, t)
        ks = k_ref[pl.ds(off, t), :]
        vs = v_ref[pl.ds(off, t), :]
        bias = b_ref[jnp.minimum(qi - j, n_tiles - 1)]
        if masked:
            row = lax.broadcasted_iota(I32, (t, t), 0)
            col = lax.broadcasted_iota(I32, (t, t), 1)
            causal = row >= col
        for a in range(2):
            s = lax.dot_general(qa[a], ks, (((1,), (1,)), ((), ())),
                                preferred_element_type=F32) + bias
            if masked:
                s = jnp.where(causal, s, NEG_INF)
            m_prev = m_scr[a]
            m_new = jnp.maximum(m_prev, jnp.max(s, axis=-1, keepdims=True))
            alpha = jnp.exp(m_prev - m_new)
            pb = jnp.exp(s - jnp.concatenate([m_new] * (t // LANES), axis=1)).astype(BF16)
            l_scr[a] = alpha * l_scr[a] + jnp.dot(pb, ones, preferred_element_type=F32)
            acc_scr[a] = alpha * acc_scr[a] + jnp.dot(pb, vs, preferred_element_type=F32)
            m_scr[a] = m_new

    def loop_body(j, carry):
        step(j, False)
        return carry

    lax.fori_loop(0, qi, loop_body, 0)
    step(qi, True)

    lp = lam_ref[...]
    lam = (jnp.exp(jnp.sum(lp[0:1] * lp[1:2])) - jnp.exp(jnp.sum(lp[2:3] * lp[3:4]))
           + lam_init)
    o = acc_scr[0] / l_scr[0] - lam * (acc_scr[1] / l_scr[1])
    ms = jnp.mean(o * o, axis=-1, keepdims=True)
    y = o * lax.rsqrt(ms + RMS_EPS) * g_ref[...]
    o_ref[...] = (y * (1.0 - lam_init)).astype(o_ref.dtype)


def diff_attention(p, bias_tiles, lam_params, ln_g, lam_init, batch, seq):
    t = ATT_BLOCK
    n_tiles = bias_tiles.shape[1]
    cq = 3 * A_WIDTH // LANES
    ck = cq + B_QK_WIDTH // LANES
    cv = ck + B_QK_WIDTH // LANES
    return pl.pallas_call(
        functools.partial(_diff_body, n_tiles=n_tiles, lam_init=lam_init),
        grid=(batch, B_HEADS, seq // t),
        in_specs=[
            pl.BlockSpec((None, t, LANES), lambda b, h, i: (b, i, cq + h)),
            pl.BlockSpec((None, seq, LANES), lambda b, h, i: (b, 0, ck + h)),
            pl.BlockSpec((None, seq, LANES), lambda b, h, i: (b, 0, cv + h)),
            pl.BlockSpec((None, n_tiles, t, t), lambda b, h, i: (h, 0, 0, 0)),
            pl.BlockSpec((4, DIFF_HALF), lambda b, h, i: (0, 0)),
            pl.BlockSpec((1, DIFF_VDIM), lambda b, h, i: (0, 0)),
        ],
        out_specs=pl.BlockSpec((None, t, LANES), lambda b, h, i: (b, i, h)),
        out_shape=jax.ShapeDtypeStruct((batch, seq, B_V_WIDTH), BF16),
        scratch_shapes=[pltpu.VMEM((2, t, LANES), F32), pltpu.VMEM((2, t, LANES), F32),
                        pltpu.VMEM((2, t, DIFF_VDIM), F32)],
        compiler_params=pltpu.CompilerParams(
            dimension_semantics=("parallel", "parallel", "arbitrary")),
        name="diff_attention",
    )(p, p, p, bias_tiles, lam_params, ln_g.reshape(1, DIFF_VDIM))


def diff_bias_tiles(rel_bias_b, seq):
    t = ATT_BLOCK
    bucket = _t5_bucket_table(max(seq, 2 * MAX_DISTANCE) + 2 * t)
    sat = bucket[-1]
    d_sat = int(np.max(np.nonzero(bucket != sat)[0])) + 1
    n_full = (d_sat + t - 1 + t - 1) // t
    n_tiles = n_full + 1
    assert n_full * t - (t - 1) >= d_sat
    n = np.arange(2 * t)[None, :]
    base = np.arange(n_tiles)[:, None] * t
    dist = np.clip(np.where(n < t, base - n, base + 2 * t - n), 0, None)
    w = rel_bias_b.T[:, bucket[dist]].astype(F32)
    rep = jnp.broadcast_to(w[:, :, None, :], (B_HEADS, n_tiles, t, 2 * t))
    flat = rep.reshape(B_HEADS, n_tiles, 2 * t * t)[:, :, :t * (2 * t - 1)]
    return flat.reshape(B_HEADS, n_tiles, t, 2 * t - 1)[:, :, :, :t]


def _logsig_cumsum_body(f_ref, b_ref, c_ref, carry_scr):
    t = f_ref.shape[0]

    @pl.when(pl.program_id(1) == 0)
    def _():
        carry_scr[...] = jnp.zeros_like(carry_scr)

    x = f_ref[...] + b_ref[...]
    ls = jnp.minimum(x, 0.0) - jnp.log1p(jnp.exp(-jnp.abs(x)))
    row = lax.broadcasted_iota(I32, (t, t), 0)
    col = lax.broadcasted_iota(I32, (t, t), 1)
    tri = (row >= col).astype(F32)
    c = jnp.dot(tri, ls, preferred_element_type=F32,
                precision=lax.Precision.HIGHEST) + carry_scr[...]
    c_ref[...] = c
    carry_scr[...] = c[t - 1:t, :]


def logsig_cumsum(fg, b_f, batch, seq, t=512):
    return pl.pallas_call(
        _logsig_cumsum_body,
        grid=(batch, seq // t),
        in_specs=[pl.BlockSpec((None, t, LANES), lambda b, i: (b, i, 0)),
                  pl.BlockSpec((1, LANES), lambda b, i: (0, 0))],
        out_specs=pl.BlockSpec((None, t, LANES), lambda b, i: (b, i, 0)),
        out_shape=jax.ShapeDtypeStruct((batch, seq, LANES), F32),
        scratch_shapes=[pltpu.VMEM((1, LANES), F32)],
        compiler_params=pltpu.CompilerParams(
            dimension_semantics=("parallel", "arbitrary")),
        name="logsig_cumsum",
    )(fg, b_f)


def _fox_body(q_ref, k_ref, v_ref, cq_ref, ck_ref, o_ref, m_scr, acc_scr):
    t = ATT_BLOCK
    qi = pl.program_id(2)
    q = q_ref[...] * QK_SCALE
    lane = lax.broadcasted_iota(I32, (1, LANES), 1)
    own = [(lane < HEAD_DIM) == (a == 0) for a in range(2)]
    qa = [jnp.where(own[a], q, jnp.zeros_like(q)) for a in range(2)]
    cqb = [jnp.broadcast_to(cq_ref[a], (t, LANES)) for a in range(2)]
    for a in range(2):
        m_scr[a] = jnp.full((t, LANES), NEG_INF, F32)
        acc_scr[a] = jnp.zeros((t, LANES), F32)

    def step(j, masked):
        off = pl.multiple_of(j * t, t)
        ks = k_ref[pl.ds(off, t), :]
        vs = v_ref[pl.ds(off, t), :]
        if masked:
            row = lax.broadcasted_iota(I32, (t, t), 0)
            col = lax.broadcasted_iota(I32, (t, t), 1)
            causal = row >= col
        for a in range(2):
            s = lax.dot_general(qa[a], ks, (((1,), (1,)), ((), ())),
                                preferred_element_type=F32)
            s = s - ck_ref[a, :, pl.ds(off, t)]
            if masked:
                s = jnp.where(causal, s, NEG_INF)
            m_prev = m_scr[a]
            m_new = jnp.maximum(m_prev, jnp.max(s, axis=-1, keepdims=True) + cqb[a])
            alpha = jnp.exp(m_prev - m_new)
            shift = m_new - cqb[a]
            pb = jnp.exp(s - jnp.concatenate([shift] * (t // LANES), axis=1)).astype(BF16)
            v_aug = jnp.where(own[a], vs, jnp.ones_like(vs))
            acc_scr[a] = alpha * acc_scr[a] + jnp.dot(pb, v_aug, preferred_element_type=F32)
            m_scr[a] = m_new

    def loop_body(j, carry):
        step(j, False)
        return carry

    lax.fori_loop(0, qi, loop_body, 0)
    step(qi, True)
    r = [acc_scr[a] / pltpu.roll(acc_scr[a], HEAD_DIM, 1) for a in range(2)]
    o_ref[...] = jnp.where(own[0], r[0], r[1]).astype(o_ref.dtype)


def fox_attention(p, cq, ck, batch, seq):
    t = ATT_BLOCK
    nk = C_WIDTH // LANES
    return pl.pallas_call(
        _fox_body,
        grid=(batch, C_HEADS // 2, seq // t),
        in_specs=[
            pl.BlockSpec((None, t, LANES), lambda b, h, i: (b, i, h)),
            pl.BlockSpec((None, seq, LANES), lambda b, h, i: (b, 0, nk + h)),
            pl.BlockSpec((None, seq, LANES), lambda b, h, i: (b, 0, 2 * nk + h)),
            pl.BlockSpec((None, 2, t, 1), lambda b, h, i: (b, h, i, 0)),
            pl.BlockSpec((None, 2, 1, seq), lambda b, h, i: (b, h, 0, 0)),
        ],
        out_specs=pl.BlockSpec((None, t, LANES), lambda b, h, i: (b, i, h)),
        out_shape=jax.ShapeDtypeStruct((batch, seq, C_WIDTH), BF16),
        scratch_shapes=[pltpu.VMEM((2, t, LANES), F32), pltpu.VMEM((2, t, LANES), F32)],
        compiler_params=pltpu.CompilerParams(
            dimension_semantics=("parallel", "parallel", "arbitrary")),
        name="fox_attention",
    )(p, p, p, cq, ck)


def _even_out_body(o1, l1, o2, l2, o3, l3, ob_ref, w_ref, x_ref, out_ref, a_scr):
    @pl.when(pl.program_id(1) == 0)
    def _():
        a1, a2, a3 = l1[...], l2[...], l3[...]
        mx = jnp.maximum(jnp.maximum(a1, a2), a3)
        e1, e2, e3 = jnp.exp(a1 - mx), jnp.exp(a2 - mx), jnp.exp(a3 - mx)
        oa = (e1 * o1[...] + e2 * o2[...] + e3 * o3[...]) / (e1 + e2 + e3)
        a_scr[:, :A_WIDTH] = oa.astype(BF16)
        a_scr[:, A_WIDTH:] = ob_ref[...]

    out_ref[...] = x_ref[...] + jnp.dot(a_scr[...], w_ref[...], preferred_element_type=F32)


def even_out_proj(branches, ob, w, x, tm=512, tn=512):
    m, d = x.shape
    half = pl.BlockSpec((tm, A_WIDTH), lambda i, j: (i, 0))
    flat = [a for pair in branches for a in pair]
    return pl.pallas_call(
        _even_out_body,
        grid=(m // tm, d // tn),
        in_specs=[half] * 6 + [
            pl.BlockSpec((tm, B_V_WIDTH), lambda i, j: (i, 0)),
            pl.BlockSpec((A_WIDTH + B_V_WIDTH, tn), lambda i, j: (0, j)),
            pl.BlockSpec((tm, tn), lambda i, j: (i, j))],
        out_specs=pl.BlockSpec((tm, tn), lambda i, j: (i, j)),
        out_shape=jax.ShapeDtypeStruct((m, d), F32),
        scratch_shapes=[pltpu.VMEM((tm, A_WIDTH + B_V_WIDTH), BF16)],
        compiler_params=pltpu.CompilerParams(
            dimension_semantics=("parallel", "arbitrary")),
        name="even_out_proj",
    )(*flat, ob, w, x)


def _matmul_res_body(a_ref, w_ref, x_ref, o_ref):
    o_ref[...] = x_ref[...] + jnp.dot(a_ref[...], w_ref[...], preferred_element_type=F32)


def matmul_residual(a, w, x, tm=512, tn=512):
    m, k = a.shape
    n = w.shape[1]
    return pl.pallas_call(
        _matmul_res_body,
        grid=(m // tm, n // tn),
        in_specs=[pl.BlockSpec((tm, k), lambda i, j: (i, 0)),
                  pl.BlockSpec((k, tn), lambda i, j: (0, j)),
                  pl.BlockSpec((tm, tn), lambda i, j: (i, j))],
        out_specs=pl.BlockSpec((tm, tn), lambda i, j: (i, j)),
        out_shape=jax.ShapeDtypeStruct((m, n), F32),
        compiler_params=pltpu.CompilerParams(
            dimension_semantics=("parallel", "parallel")),
        name="matmul_residual",
    )(a, w, x)


def _add_norm_body(x_ref, r_ref, g_ref, o_ref):
    x = x_ref[...] + r_ref[...]
    ms = jnp.mean(x * x, axis=-1, keepdims=True)
    o_ref[...] = x * lax.rsqrt(ms + RMS_EPS) * g_ref[...]


def add_norm(x, r, g, tm=512):
    m, d = x.shape
    row = pl.BlockSpec((tm, d), lambda i: (i, 0))
    return pl.pallas_call(
        _add_norm_body,
        grid=(m // tm,),
        in_specs=[row, row, pl.BlockSpec((1, d), lambda i: (0, 0))],
        out_specs=row,
        out_shape=jax.ShapeDtypeStruct((m, d), F32),
        compiler_params=pltpu.CompilerParams(dimension_semantics=("parallel",)),
        name="add_norm",
    )(x, r, g.reshape(1, d))


PEER_CAND_ROWS = PEER_TOPK + 8 * (PEER_TOPK - 1)


def _peer_topk_body(q_ref, sk_ref, idx_ref, gate_ref, ts_scr, ti_scr, bs_scr, be_scr,
                    gt_scr, it_scr):
    tm = q_ref.shape[0]
    neg_inf = jnp.float32(-jnp.inf)
    key_id = lax.broadcasted_iota(I32, (N_KEYS, tm), 0)

    def pair_body(pr, carry):
        off = pl.multiple_of(pr * PEER_KEY_HALF, PEER_KEY_HALF)
        sc = lax.dot_general(sk_ref[pr], q_ref[:, pl.ds(off, PEER_KEY_HALF)],
                             (((1,), (1,)), ((), ())), preferred_element_type=F32)

        def k_body(k, vals):
            m = jnp.max(vals, axis=0, keepdims=True)
            sel = jnp.min(jnp.where(vals == m, key_id, N_KEYS), axis=0, keepdims=True)
            ts_scr[pr, pl.ds(k, 1), :] = m
            ti_scr[pr, pl.ds(k, 1), :] = sel
            return jnp.where(key_id == sel, neg_inf, vals)

        lax.fori_loop(0, PEER_TOPK, k_body, sc)
        return carry

    lax.fori_loop(0, 2 * PEER_HEADS, pair_body, 0)

    r = lax.broadcasted_iota(I32, (PEER_CAND_ROWS, 1), 0)
    cand_id = jnp.where(r < PEER_TOPK, r,
                        (1 + (r - PEER_TOPK) // 8) * PEER_TOPK + (r - PEER_TOPK) % 8)

    def head_body(h, carry):
        s1, s2 = ts_scr[2 * h], ts_scr[2 * h + 1]
        i1, i2 = ti_scr[2 * h] * N_KEYS, ti_scr[2 * h + 1]
        vals = jnp.concatenate(
            [s1[0:1] + s2] + [s1[a:a + 1] + s2[0:8] for a in range(1, PEER_TOPK)], axis=0)
        eidx = jnp.concatenate(
            [i1[0:1] + i2] + [i1[a:a + 1] + i2[0:8] for a in range(1, PEER_TOPK)], axis=0)

        def k_body(k, vals):
            m = jnp.max(vals, axis=0, keepdims=True)
            sel = jnp.min(jnp.where(vals == m, cand_id, PEER_TOPK * PEER_TOPK),
                          axis=0, keepdims=True)
            hit = cand_id == sel
            bs_scr[pl.ds(k, 1), :] = m
            be_scr[pl.ds(k, 1), :] = jnp.sum(jnp.where(hit, eidx, 0), axis=0, keepdims=True)
            return jnp.where(hit, neg_inf, vals)

        lax.fori_loop(0, PEER_TOPK, k_body, vals)
        bs = bs_scr[...]
        e = jnp.exp(bs - jnp.max(bs, axis=0, keepdims=True))
        row0 = pl.multiple_of(h * PEER_TOPK, PEER_TOPK)
        gt_scr[pl.ds(row0, PEER_TOPK), :] = e / jnp.sum(e, axis=0, keepdims=True)
        it_scr[pl.ds(row0, PEER_TOPK), :] = be_scr[...]
        return carry

    lax.fori_loop(0, PEER_HEADS, head_body, 0)
    gate_ref[...] = gt_scr[...].T
    idx_ref[...] = it_scr[...].T


def peer_topk(q, subkeys, tm=256):
    m = q.shape[0]
    n_sel = PEER_HEADS * PEER_TOPK
    out_spec = pl.BlockSpec((tm, n_sel), lambda i: (i, 0))
    return pl.pallas_call(
        _peer_topk_body,
        grid=(m // tm,),
        in_specs=[pl.BlockSpec((tm, q.shape[1]), lambda i: (i, 0)),
                  pl.BlockSpec(subkeys.shape, lambda i: (0, 0, 0))],
        out_specs=[out_spec, out_spec],
        out_shape=[jax.ShapeDtypeStruct((m, n_sel), I32),
                   jax.ShapeDtypeStruct((m, n_sel), F32)],
        scratch_shapes=[pltpu.VMEM((2 * PEER_HEADS, PEER_TOPK, tm), F32),
                        pltpu.VMEM((2 * PEER_HEADS, PEER_TOPK, tm), I32),
                        pltpu.VMEM((PEER_TOPK, tm), F32),
                        pltpu.VMEM((PEER_TOPK, tm), I32),
                        pltpu.VMEM((n_sel, tm), F32),
                        pltpu.VMEM((n_sel, tm), I32)],
        compiler_params=pltpu.CompilerParams(dimension_semantics=("parallel",)),
        name="peer_topk",
    )(q, subkeys)


def _gelu_gate_body(a_ref, g_ref, o_ref):
    a = a_ref[...]
    o_ref[...] = g_ref[...] * (0.5 * a * (1.0 + lax.erf(a * (2.0 ** -0.5))))


def gelu_gate(act, gate, tm=2048):
    m, n = act.shape
    spec = pl.BlockSpec((tm, n), lambda i: (i, 0))
    return pl.pallas_call(
        _gelu_gate_body,
        grid=(m // tm,),
        in_specs=[spec, spec],
        out_specs=spec,
        out_shape=jax.ShapeDtypeStruct((m, n), F32),
        compiler_params=pltpu.CompilerParams(dimension_semantics=("parallel",)),
        name="gelu_gate",
    )(act, gate)


SC_TOK_CHUNK = 32
SC_RING = 8
SC_BF16_GROUP = 4
SC_FMT = plsc.PackFormat.INTERLEAVED


def _sc_worker_id():
    return lax.axis_index("s") * SC_CORES + lax.axis_index("c")


def pack_bf16_pairs(t):
    half = t.shape[-1] // 2
    bits = lax.bitcast_convert_type(t.astype(BF16).astype(F32), I32)
    return (bits[..., half:] & jnp.int32(-65536)) | lax.shift_right_logical(
        bits[..., :half], jnp.int32(16))


def _sc_row_pipeline(idx_v, table_hbm, rows_v, sems, n_items, groups, compute):
    def gather(item):
        tt, g = item // groups, item % groups
        ids = idx_v[tt, pl.ds(g * SC_LANES, SC_LANES)]
        slot = item % SC_RING
        return pltpu.make_async_copy(table_hbm.at[ids], rows_v.at[slot], sems.at[slot])

    for s in range(SC_RING - 1):
        gather(s).start()

    def item_body(item, carry):
        nxt = item + SC_RING - 1

        @pl.when(nxt < n_items)
        def _():
            gather(nxt).start()

        gather(item).wait()
        compute(item // groups, item % groups, item % SC_RING)
        return carry

    lax.fori_loop(0, n_items, item_body, 0)


def peer_expert_dots(hp, idx, up):
    m, dw = hp.shape
    n_sel = idx.shape[1]
    per_w = m // SC_WORKERS
    n_chunks = per_w // SC_TOK_CHUNK
    groups = n_sel // SC_LANES
    step = SC_BF16_GROUP * SC_LANES
    mesh = plsc.VectorSubcoreMesh(core_axis_name="c", subcore_axis_name="s")

    @functools.partial(
        pl.kernel, mesh=mesh,
        out_type=jax.ShapeDtypeStruct((m, n_sel), F32),
        scratch_types=[
            pltpu.VMEM((SC_TOK_CHUNK, n_sel), I32),
            pltpu.VMEM((SC_TOK_CHUNK, dw), I32),
            pltpu.VMEM((SC_TOK_CHUNK, n_sel), F32),
            pltpu.VMEM((SC_RING, SC_LANES, dw), I32),
            pltpu.VMEM((SC_LANES * SC_LANES,), F32),
            pltpu.SemaphoreType.DMA((SC_RING,)),
        ],
        compiler_params=pltpu.CompilerParams(needs_layout_passes=False),
        name="peer_expert_dots",
    )
    def k(h_hbm, idx_hbm, u_hbm, act_hbm, idx_v, h_v, act_v, rows_v, part_v, sems):
        base = _sc_worker_id() * per_w
        lane = lax.broadcasted_iota(I32, (SC_LANES,), 0)

        def compute(tt, g, slot):
            def grp_body(q, accs):
                off = pl.multiple_of(q * step, step)
                xs = [plsc.bitcast(h_v[tt, pl.ds(off + c * SC_LANES, SC_LANES)], BF16)
                      for c in range(SC_BF16_GROUP)]
                new = []
                for e in range(SC_LANES):
                    s = None
                    for c in range(SC_BF16_GROUP):
                        p = plsc.bitcast(
                            rows_v[slot, e, pl.ds(off + c * SC_LANES, SC_LANES)], BF16) * xs[c]
                        s = p if s is None else s + p
                    lo, hi = plsc.unpack(s, format=SC_FMT)
                    new.append(accs[e] + (lo + hi))
                return tuple(new)

            accs = lax.fori_loop(
                0, dw // step, grp_body,
                tuple(jnp.zeros((SC_LANES,), F32) for _ in range(SC_LANES)))
            for e in range(SC_LANES):
                part_v[pl.ds(e * SC_LANES, SC_LANES)] = accs[e]
            tot = jnp.zeros((SC_LANES,), F32)
            for l in range(SC_LANES):
                tot = tot + plsc.load_gather(part_v, [lane * SC_LANES + l])
            act_v[tt, pl.ds(g * SC_LANES, SC_LANES)] = tot

        def chunk_body(c, carry):
            t0 = base + c * SC_TOK_CHUNK
            pltpu.sync_copy(idx_hbm.at[pl.ds(t0, SC_TOK_CHUNK)], idx_v)
            pltpu.sync_copy(h_hbm.at[pl.ds(t0, SC_TOK_CHUNK)], h_v)
            _sc_row_pipeline(idx_v, u_hbm, rows_v, sems, SC_TOK_CHUNK * groups, groups, compute)
            pltpu.sync_copy(act_v, act_hbm.at[pl.ds(t0, SC_TOK_CHUNK)])
            return carry

        lax.fori_loop(0, n_chunks, chunk_body, 0)

    return k(hp, idx, up)


def peer_expert_combine(w, idx, vp):
    m, n_sel = w.shape
    dw = vp.shape[1]
    d = 2 * dw
    per_w = m // SC_WORKERS
    n_chunks = per_w // SC_TOK_CHUNK
    n_vec = d // SC_LANES
    groups = n_sel // SC_LANES
    mesh = plsc.VectorSubcoreMesh(core_axis_name="c", subcore_axis_name="s")

    @functools.partial(
        pl.kernel, mesh=mesh,
        out_type=jax.ShapeDtypeStruct((m, d), F32),
        scratch_types=[
            pltpu.VMEM((SC_TOK_CHUNK, n_sel), I32),
            pltpu.VMEM((SC_TOK_CHUNK, n_sel), F32),
            pltpu.VMEM((SC_TOK_CHUNK, d), F32),
            pltpu.VMEM((SC_RING, SC_LANES, dw), I32),
            pltpu.SemaphoreType.DMA((SC_RING,)),
        ],
        compiler_params=pltpu.CompilerParams(needs_layout_passes=False),
        name="peer_expert_combine",
    )
    def k(w_hbm, idx_hbm, v_hbm, out_hbm, idx_v, w_v, out_v, rows_v, sems):
        base = _sc_worker_id() * per_w

        def compute(tt, g, slot):
            splat = []
            for e in range(SC_LANES):
                s = plsc.load_gather(w_v, [jnp.full((SC_LANES,), tt, I32),
                                           jnp.full((SC_LANES,), g * SC_LANES + e, I32)])
                splat.append(plsc.pack(s, s, format=SC_FMT))

            @plsc.parallel_loop(0, dw // SC_LANES)
            def _(j):
                off = pl.multiple_of(j * SC_LANES, SC_LANES)
                acc_lo = out_v[tt, pl.ds(off, SC_LANES)]
                acc_hi = out_v[tt, pl.ds(dw + off, SC_LANES)]
                for e0 in range(0, SC_LANES, SC_BF16_GROUP):
                    s = None
                    for e in range(e0, e0 + SC_BF16_GROUP):
                        p = plsc.bitcast(rows_v[slot, e, pl.ds(off, SC_LANES)], BF16) * splat[e]
                        s = p if s is None else s + p
                    lo, hi = plsc.unpack(s, format=SC_FMT)
                    acc_lo = acc_lo + lo
                    acc_hi = acc_hi + hi
                out_v[tt, pl.ds(off, SC_LANES)] = acc_lo
                out_v[tt, pl.ds(dw + off, SC_LANES)] = acc_hi

        def chunk_body(c, carry):
            t0 = base + c * SC_TOK_CHUNK
            pltpu.sync_copy(idx_hbm.at[pl.ds(t0, SC_TOK_CHUNK)], idx_v)
            pltpu.sync_copy(w_hbm.at[pl.ds(t0, SC_TOK_CHUNK)], w_v)

            def zero_body(z, carry2):
                tt, j = z // n_vec, z % n_vec
                out_v[tt, pl.ds(pl.multiple_of(j * SC_LANES, SC_LANES), SC_LANES)] = (
                    jnp.zeros((SC_LANES,), F32))
                return carry2

            lax.fori_loop(0, SC_TOK_CHUNK * n_vec, zero_body, 0)
            _sc_row_pipeline(idx_v, v_hbm, rows_v, sems, SC_TOK_CHUNK * groups, groups, compute)
            pltpu.sync_copy(out_v, out_hbm.at[pl.ds(t0, SC_TOK_CHUNK)])
            return carry

        lax.fori_loop(0, n_chunks, chunk_body, 0)

    return k(w, idx, vp)


def kernel(x, norm_mix_g, norm_ffn_g, final_norm_g, rel_bias, even_w_in, even_w_out,
           diff_lambda, diff_ln_g, odd_w_in, odd_b_f, odd_w_out, peer_wq, peer_subkeys,
           peer_u, peer_v):
    batch, seq, d = x.shape

    dil_tiles = [dilated_bias_tile(rel_bias[:, :A_HEADS], w, dl) for w, dl in DILATED_BRANCHES]
    diff_tiles = diff_bias_tiles(rel_bias[:, A_HEADS:], seq)
    lam_init = 0.8 - 0.6 * math.exp(-0.3 * 0)
    even_in, even_out = even_w_in[0].astype(BF16), even_w_out[0].astype(BF16)
    w_in = odd_w_in[0]
    odd_in, odd_out = w_in[:, :3 * C_WIDTH].astype(BF16), odd_w_out[0].astype(BF16)
    w_gate = jnp.pad(w_in[:, 3 * C_WIDTH:], ((0, 0), (0, LANES - C_HEADS)))
    b_f = jnp.pad(odd_b_f[0], (0, LANES - C_HEADS)).reshape(1, LANES)
    peer = [dict(wq=peer_wq[l].astype(BF16),
                 sk=peer_subkeys[l].reshape(2 * PEER_HEADS, N_KEYS, PEER_KEY_HALF).astype(BF16),
                 up=pack_bf16_pairs(peer_u[l]), vp=pack_bf16_pairs(peer_v[l]))
            for l in range(2)]

    bg = batch // BATCH_GROUPS
    m = bg * seq
    groups = [dict(x=x[g * bg:(g + 1) * bg].reshape(m, d)) for g in range(BATCH_GROUPS)]

    def peer_select(st, layer):
        q, st["hp"] = norm_matmul(st["x"], norm_ffn_g[layer], peer[layer]["wq"], want_h="packed")
        st["idx"], st["gate"] = peer_topk(q, peer[layer]["sk"])
        st["act"] = peer_expert_dots(st["hp"], st["idx"], peer[layer]["up"])

    def peer_combine(st, layer):
        w = gelu_gate(st["act"], st["gate"])
        st["peer"] = peer_expert_combine(w, st["idx"], peer[layer]["vp"])

    def even_layer(st):
        (p,) = norm_matmul(st["x"], norm_mix_g[0], even_in)
        branches = [dilated_branch(p, tile, dl, bg, seq)
                    for tile, (_, dl) in zip(dil_tiles, DILATED_BRANCHES)]
        ob = diff_attention(p.reshape(bg, seq, -1), diff_tiles, diff_lambda[0], diff_ln_g[0],
                            lam_init, bg, seq)
        st["x"] = even_out_proj(branches, ob.reshape(m, B_V_WIDTH), even_out, st["x"])
        peer_select(st, 0)

    def odd_layer(st):
        p, st["x"], h = norm_matmul(st["x"], norm_mix_g[1], odd_in, res=st["peer"], want_h="f32")
        fg = gate_matmul(h, w_gate)
        c = logsig_cumsum(fg.reshape(bg, seq, LANES), b_f, bg, seq)
        ct = c[:, :, :C_HEADS].transpose(0, 2, 1)
        o = fox_attention(p.reshape(bg, seq, -1), ct[:, :, :, None], ct[:, :, None, :], bg, seq)
        st["x"] = matmul_residual(o.reshape(m, C_WIDTH), odd_out, st["x"])
        peer_select(st, 1)

    stages = [(even_layer, st, 0) for st in groups] + [(odd_layer, st, 1) for st in groups]
    pending = None
    for fn, st, layer in stages:
        fn(st)
        if pending is not None:
            peer_combine(*pending)
        pending = (st, layer)
    peer_combine(*pending)
    outs =[add_norm(st["x"], st["peer"], final_norm_g).reshape(bg, seq, d) for st in groups]
    return jnp.concatenate(outs, axis=0)


def _gate_matmul_body(h_ref, w_ref, o_ref):
    o_ref[...] = jnp.dot(h_ref[...], w_ref[...], preferred_element_type=F32,
                         precision=lax.Precision.HIGHEST)


def gate_matmul(h, w, tm=512):
    m, d = h.shape
    n = w.shape[1]
    return pl.pallas_call(
        _gate_matmul_body,
        grid=(m // tm,),
        in_specs=[pl.BlockSpec((tm, d), lambda i: (i, 0)),
                  pl.BlockSpec((d, n), lambda i: (0, 0))],
        out_specs=pl.BlockSpec((tm, n), lambda i: (i, 0)),
        out_shape=jax.ShapeDtypeStruct((m, n), F32),
        compiler_params=pltpu.CompilerParams(dimension_semantics=("parallel",)),
        name="gate_matmul",
    )(h, w)
```

```python
import functools
import math

import numpy as np
import jax
import jax.numpy as jnp
from jax import lax
from jax.experimental import pallas as pl
from jax.experimental.pallas import tpu as pltpu
from jax.experimental.pallas import tpu_sc as plsc

F32 = jnp.float32
BF16 = jnp.bfloat16
I32 = jnp.int32

D_MODEL = 1024
HEAD_DIM = 64
A_HEADS = 8
DILATED_BRANCHES = ((128, 1), (512, 4), (2048, 16))
DIFF_HALF = 64
DIFF_VDIM = 128
B_HEADS = 4
C_HEADS = 16
N_BUCKETS = 32
MAX_DISTANCE = 2048
PEER_HEADS = 8
N_KEYS = 128
PEER_TOPK = 16
PEER_KEY_HALF = 128
RMS_EPS = 1e-6
NEG_INF = -1e30
A_WIDTH = A_HEADS * HEAD_DIM
B_QK_WIDTH = B_HEADS * 2 * DIFF_HALF
B_V_WIDTH = B_HEADS * DIFF_VDIM
C_WIDTH = C_HEADS * HEAD_DIM
QK_SCALE = 0.125
LANES = 128
DIL_BLOCK = 128
ATT_BLOCK = 512
BATCH_GROUPS = 4

SC_CORES = 2
SC_SUBCORES = 16
SC_LANES = 16
SC_WORKERS = SC_CORES * SC_SUBCORES


def _t5_bucket_table(n):
    max_exact = N_BUCKETS // 2
    d = np.arange(n)
    df = np.maximum(d, 1).astype(np.float32)
    large = max_exact + (
        np.log(df / np.float32(max_exact)) / np.float32(math.log(MAX_DISTANCE / max_exact))
        * np.float32(N_BUCKETS - max_exact)).astype(np.int32)
    large = np.minimum(large, N_BUCKETS - 1)
    return np.where(d < max_exact, d, large).astype(np.int32)


def _norm_matmul_body(*refs, has_res, want_h):
    it = iter(refs)
    x_ref = next(it)
    r_ref = next(it) if has_res else None
    g_ref = next(it)
    w_ref = next(it)
    o_ref = next(it)
    xs_ref = next(it) if has_res else None
    hout_ref = next(it) if want_h else None
    h_scr = next(it)

    @pl.when(pl.program_id(1) == 0)
    def _():
        x = x_ref[...]
        if has_res:
            x = x + r_ref[...]
            xs_ref[...] = x
        ms = jnp.mean(x * x, axis=-1, keepdims=True)
        h = x * lax.rsqrt(ms + RMS_EPS) * g_ref[...]
        if want_h == "f32":
            hout_ref[...] = h
        elif want_h == "packed":
            hout_ref[...] = pack_bf16_pairs(h)
        h_scr[...] = h.astype(BF16)

    o_ref[...] = jnp.dot(h_scr[...], w_ref[...],
                         preferred_element_type=F32).astype(o_ref.dtype)


def norm_matmul(x, g, w, *, res=None, want_h=None, out_dtype=BF16, tm=512, tn=512):
    m, d = x.shape
    n = w.shape[1]
    tn = min(tn, n)
    row = pl.BlockSpec((tm, d), lambda i, j: (i, 0))
    in_specs = [row] + ([row] if res is not None else []) + [
        pl.BlockSpec((1, d), lambda i, j: (0, 0)),
        pl.BlockSpec((d, tn), lambda i, j: (0, j))]
    out_specs = [pl.BlockSpec((tm, tn), lambda i, j: (i, j))]
    out_shape = [jax.ShapeDtypeStruct((m, n), out_dtype)]
    if res is not None:
        out_specs.append(row)
        out_shape.append(jax.ShapeDtypeStruct((m, d), F32))
    if want_h == "f32":
        out_specs.append(row)
        out_shape.append(jax.ShapeDtypeStruct((m, d), F32))
    elif want_h == "packed":
        out_specs.append(pl.BlockSpec((tm, d // 2), lambda i, j: (i, 0)))
        out_shape.append(jax.ShapeDtypeStruct((m, d // 2), I32))
    args =[x] + ([res] if res is not None else []) + [g.reshape(1, d), w]
    return pl.pallas_call(
        functools.partial(_norm_matmul_body, has_res=res is not None, want_h=want_h),
        grid=(m // tm, n // tn),
        in_specs=in_specs,
        out_specs=out_specs,
        out_shape=out_shape,
        scratch_shapes=[pltpu.VMEM((tm, d), BF16)],
        compiler_params=pltpu.CompilerParams(
            dimension_semantics=("parallel", "arbitrary")),
        name="norm_matmul",
    )(*args)


def _dilated_body(q_ref, kp_ref, kc_ref, vp_ref, vc_ref, b_ref, o_ref, lse_ref):
    i = pl.program_id(3)
    q = q_ref[...] * QK_SCALE
    k = jnp.concatenate([kp_ref[...], kc_ref[...]], axis=0)
    v = jnp.concatenate([vp_ref[...], vc_ref[...]], axis=0)
    col = lax.broadcasted_iota(I32, (DIL_BLOCK, 2 * DIL_BLOCK), 1)
    has_prev = jnp.logical_or(col >= DIL_BLOCK, i > 0)
    outs, lses = [], []
    for hh in range(2):
        sl = slice(hh * HEAD_DIM, (hh + 1) * HEAD_DIM)
        s = lax.dot_general(q[:, sl], k[:, sl], (((1,), (1,)), ((), ())),
                            preferred_element_type=F32)
        s = jnp.where(has_prev, s + b_ref[hh], NEG_INF)
        m = jnp.max(s, axis=-1, keepdims=True)
        p = jnp.exp(s - m)
        l = jnp.sum(p, axis=-1, keepdims=True)
        o = jnp.dot(p.astype(BF16), v[:, sl], preferred_element_type=F32) / l
        outs.append(o)
        lses.append(jnp.broadcast_to(m + jnp.log(l), (DIL_BLOCK, HEAD_DIM)))
    o_ref[...] = jnp.concatenate(outs, axis=1)
    lse_ref[...] = jnp.concatenate(lses, axis=1)


def dilated_branch(p, bias_tile, dil, batch, seq):
    n_cols = p.shape[1]
    cb = n_cols // LANES
    rows = seq // dil
    nblk = rows // DIL_BLOCK
    pv = p.reshape(batch, rows, dil * n_cols)
    kq, kk, kv = 0, A_WIDTH // LANES, 2 * A_WIDTH // LANES
    blk = (None, DIL_BLOCK, LANES)

    def spec(col0, prev):
        if prev:
            return pl.BlockSpec(blk, lambda b, h, r, i: (b, jnp.maximum(i - 1, 0), r * cb + col0 + h))
        return pl.BlockSpec(blk, lambda b, h, r, i: (b, i, r * cb + col0 + h))

    ocb = A_WIDTH // LANES
    ospec = pl.BlockSpec(blk, lambda b, h, r, i: (b, i, r * ocb + h))
    oshape = jax.ShapeDtypeStruct((batch, rows, dil * A_WIDTH), F32)
    o, lse = pl.pallas_call(
        _dilated_body,
        grid=(batch, A_HEADS // 2, dil, nblk),
        in_specs=[spec(kq, False), spec(kk, True), spec(kk, False),
                  spec(kv, True), spec(kv, False),
                  pl.BlockSpec((2, DIL_BLOCK, 2 * DIL_BLOCK), lambda b, h, r, i: (h, 0, 0))],
        out_specs=[ospec, ospec],
        out_shape=[oshape, oshape],
        compiler_params=pltpu.CompilerParams(
            dimension_semantics=("parallel", "parallel", "parallel", "arbitrary")),
        name=f"dilated_d{dil}",
    )(pv, pv, pv, pv, pv, bias_tile)
    return o.reshape(batch * seq, A_WIDTH), lse.reshape(batch * seq, A_WIDTH)


def dilated_bias_tile(rel_bias_a, window, dil):
    n = window // dil
    assert n == DIL_BLOCK
    bucket = _t5_bucket_table(window + 1)
    row = np.arange(DIL_BLOCK)[:, None]
    c = np.arange(2 * DIL_BLOCK)[None, :]
    j = DIL_BLOCK + row - c
    valid = (j >= 0) & (j <= n)
    bk = bucket[np.clip(j, 0, n) * dil]
    tile = rel_bias_a.T[:, bk]
    return jnp.where(jnp.asarray(valid)[None], tile, NEG_INF).astype(F32)


def _diff_body(q_ref, k_ref, v_ref, b_ref, lam_ref, g_ref, o_ref,
               m_scr, l_scr, acc_scr, *, n_tiles, lam_init):
    t = ATT_BLOCK
    qi = pl.program_id(2)
    q = q_ref[...] * QK_SCALE
    lane = lax.broadcasted_iota(I32, (1, LANES), 1)
    qa = [jnp.where((lane < DIFF_HALF) == (a == 0), q, jnp.zeros_like(q)) for a in range(2)]
    ones = jnp.ones((t, LANES), BF16)
    for a in range(2):
        m_scr[a] = jnp.full((t, LANES), NEG_INF, F32)
        l_scr[a] = jnp.zeros((t, LANES), F32)
        acc_scr[a] = jnp.zeros((t, DIFF_VDIM), F32)

    def step(j, masked):
        off = pl.multiple_of(j * t, t)
        ks = k_ref[pl.ds(off, t), :]
        vs = v_ref[pl.ds(off, t), :]
        bias = b_ref[jnp.minimum(qi - j, n_tiles - 1)]
        if masked:
            row = lax.broadcasted_iota(I32, (t, t), 0)
            col = lax.broadcasted_iota(I32, (t, t), 1)
            causal = row >= col
        for a in range(2):
            s = lax.dot_general(qa[a], ks, (((1,), (1,)), ((), ())),
                                preferred_element_type=F32) + bias
            if masked:
                s = jnp.where(causal, s, NEG_INF)
            m_prev = m_scr[a]
            m_new = jnp.maximum(m_prev, jnp.max(s, axis=-1, keepdims=True))
            alpha = jnp.exp(m_prev - m_new)
            pb = jnp.exp(s - jnp.concatenate([m_new] * (t // LANES), axis=1)).astype(BF16)
            l_scr[a] = alpha * l_scr[a] + jnp.dot(pb, ones, preferred_element_type=F32)
            acc_scr[a] = alpha * acc_scr[a] + jnp.dot(pb, vs, preferred_element_type=F32)
            m_scr[a] = m_new

    def loop_body(j, carry):
        step(j, False)
        return carry

    lax.fori_loop(0, qi, loop_body, 0)
    step(qi, True)

    lp = lam_ref[...]
    lam = (jnp.exp(jnp.sum(lp[0:1] * lp[1:2])) - jnp.exp(jnp.sum(lp[2:3] * lp[3:4]))
           + lam_init)
    o = acc_scr[0] / l_scr[0] - lam * (acc_scr[1] / l_scr[1])
    ms = jnp.mean(o * o, axis=-1, keepdims=True)
    y = o * lax.rsqrt(ms + RMS_EPS) * g_ref[...]
    o_ref[...] = (y * (1.0 - lam_init)).astype(o_ref.dtype)


def diff_attention(p, bias_tiles, lam_params, ln_g, lam_init, batch, seq):
    t = ATT_BLOCK
    n_tiles = bias_tiles.shape[1]
    cq = 3 * A_WIDTH // LANES
    ck = cq + B_QK_WIDTH // LANES
    cv = ck + B_QK_WIDTH // LANES
    return pl.pallas_call(
        functools.partial(_diff_body, n_tiles=n_tiles, lam_init=lam_init),
        grid=(batch, B_HEADS, seq // t),
        in_specs=[
            pl.BlockSpec((None, t, LANES), lambda b, h, i: (b, i, cq + h)),
            pl.BlockSpec((None, seq, LANES), lambda b, h, i: (b, 0, ck + h)),
            pl.BlockSpec((None, seq, LANES), lambda b, h, i: (b, 0, cv + h)),
            pl.BlockSpec((None, n_tiles, t, t), lambda b, h, i: (h, 0, 0, 0)),
            pl.BlockSpec((4, DIFF_HALF), lambda b, h, i: (0, 0)),
            pl.BlockSpec((1, DIFF_VDIM), lambda b, h, i: (0, 0)),
        ],
        out_specs=pl.BlockSpec((None, t, LANES), lambda b, h, i: (b, i, h)),
        out_shape=jax.ShapeDtypeStruct((batch, seq, B_V_WIDTH), BF16),
        scratch_shapes=[pltpu.VMEM((2, t, LANES), F32), pltpu.VMEM((2, t, LANES), F32),
                        pltpu.VMEM((2, t, DIFF_VDIM), F32)],
        compiler_params=pltpu.CompilerParams(
            dimension_semantics=("parallel", "parallel", "arbitrary")),
        name="diff_attention",
    )(p, p, p, bias_tiles, lam_params, ln_g.reshape(1, DIFF_VDIM))


def diff_bias_tiles(rel_bias_b, seq):
    t = ATT_BLOCK
    bucket = _t5_bucket_table(max(seq, 2 * MAX_DISTANCE) + 2 * t)
    sat = bucket[-1]
    d_sat = int(np.max(np.nonzero(bucket != sat)[0])) + 1
    n_full = (d_sat + t - 1 + t - 1) // t
    n_tiles = n_full + 1
    assert n_full * t - (t - 1) >= d_sat
    n = np.arange(2 * t)[None, :]
    base = np.arange(n_tiles)[:, None] * t
    dist = np.clip(np.where(n < t, base - n, base + 2 * t - n), 0, None)
    w = rel_bias_b.T[:, bucket[dist]].astype(F32)
    rep = jnp.broadcast_to(w[:, :, None, :], (B_HEADS, n_tiles, t, 2 * t))
    flat = rep.reshape(B_HEADS, n_tiles, 2 * t * t)[:, :, :t * (2 * t - 1)]
    return flat.reshape(B_HEADS, n_tiles, t, 2 * t - 1)[:, :, :, :t]


def _logsig_cumsum_body(f_ref, b_ref, c_ref, carry_scr):
    t = f_ref.shape[0]

    @pl.when(pl.program_id(1) == 0)
    def _():
        carry_scr[...] = jnp.zeros_like(carry_scr)

    x = f_ref[...] + b_ref[...]
    ls = jnp.minimum(x, 0.0) - jnp.log1p(jnp.exp(-jnp.abs(x)))
    row = lax.broadcasted_iota(I32, (t, t), 0)
    col = lax.broadcasted_iota(I32, (t, t), 1)
    tri = (row >= col).astype(F32)
    c = jnp.dot(tri, ls, preferred_element_type=F32,
                precision=lax.Precision.HIGHEST) + carry_scr[...]
    c_ref[...] = c
    carry_scr[...] = c[t - 1:t, :]


def logsig_cumsum(fg, b_f, batch, seq, t=512):
    return pl.pallas_call(
        _logsig_cumsum_body,
        grid=(batch, seq // t),
        in_specs=[pl.BlockSpec((None, t, LANES), lambda b, i: (b, i, 0)),
                  pl.BlockSpec((1, LANES), lambda b, i: (0, 0))],
        out_specs=pl.BlockSpec((None, t, LANES), lambda b, i: (b, i, 0)),
        out_shape=jax.ShapeDtypeStruct((batch, seq, LANES), F32),
        scratch_shapes=[pltpu.VMEM((1, LANES), F32)],
        compiler_params=pltpu.CompilerParams(
            dimension_semantics=("parallel", "arbitrary")),
        name="logsig_cumsum",
    )(fg, b_f)


def _fox_body(q_ref, k_ref, v_ref, cq_ref, ck_ref, o_ref, m_scr, acc_scr):
    t = ATT_BLOCK
    qi = pl.program_id(2)
    q = q_ref[...] * QK_SCALE
    lane = lax.broadcasted_iota(I32, (1, LANES), 1)
    own = [(lane < HEAD_DIM) == (a == 0) for a in range(2)]
    qa = [jnp.where(own[a], q, jnp.zeros_like(q)) for a in range(2)]
    cqb = [jnp.broadcast_to(cq_ref[a], (t, LANES)) for a in range(2)]
    for a in range(2):
        m_scr[a] = jnp.full((t, LANES), NEG_INF, F32)
        acc_scr[a] = jnp.zeros((t, LANES), F32)

    def step(j, masked):
        off = pl.multiple_of(j * t, t)
        ks = k_ref[pl.ds(off, t), :]
        vs = v_ref[pl.ds(off, t), :]
        if masked:
            row = lax.broadcasted_iota(I32, (t, t), 0)
            col = lax.broadcasted_iota(I32, (t, t), 1)
            causal = row >= col
        for a in range(2):
            s = lax.dot_general(qa[a], ks, (((1,), (1,)), ((), ())),
                                preferred_element_type=F32)
            s = s - ck_ref[a, :, pl.ds(off, t)]
            if masked:
                s = jnp.where(causal, s, NEG_INF)
            m_prev = m_scr[a]
            m_new = jnp.maximum(m_prev, jnp.max(s, axis=-1, keepdims=True) + cqb[a])
            alpha = jnp.exp(m_prev - m_new)
            shift = m_new - cqb[a]
            pb = jnp.exp(s - jnp.concatenate([shift] * (t // LANES), axis=1)).astype(BF16)
            v_aug = jnp.where(own[a], vs, jnp.ones_like(vs))
            acc_scr[a] = alpha * acc_scr[a] + jnp.dot(pb, v_aug, preferred_element_type=F32)
            m_scr[a] = m_new

    def loop_body(j, carry):
        step(j, False)
        return carry

    lax.fori_loop(0, qi, loop_body, 0)
    step(qi, True)
    r = [acc_scr[a] / pltpu.roll(acc_scr[a], HEAD_DIM, 1) for a in range(2)]
    o_ref[...] = jnp.where(own[0], r[0], r[1]).astype(o_ref.dtype)


def fox_attention(p, cq, ck, batch, seq):
    t = ATT_BLOCK
    nk = C_WIDTH // LANES
    return pl.pallas_call(
        _fox_body,
        grid=(batch, C_HEADS // 2, seq // t),
        in_specs=[
            pl.BlockSpec((None, t, LANES), lambda b, h, i: (b, i, h)),
            pl.BlockSpec((None, seq, LANES), lambda b, h, i: (b, 0, nk + h)),
            pl.BlockSpec((None, seq, LANES), lambda b, h, i: (b, 0, 2 * nk + h)),
            pl.BlockSpec((None, 2, t, 1), lambda b, h, i: (b, h, i, 0)),
            pl.BlockSpec((None, 2, 1, seq), lambda b, h, i: (b, h, 0, 0)),
        ],
        out_specs=pl.BlockSpec((None, t, LANES), lambda b, h, i: (b, i, h)),
        out_shape=jax.ShapeDtypeStruct((batch, seq, C_WIDTH), BF16),
        scratch_shapes=[pltpu.VMEM((2, t, LANES), F32), pltpu.VMEM((2, t, LANES), F32)],
        compiler_params=pltpu.CompilerParams(
            dimension_semantics=("parallel", "parallel", "arbitrary")),
        name="fox_attention",
    )(p, p, p, cq, ck)


def _even_out_body(o1, l1, o2, l2, o3, l3, ob_ref, w_ref, x_ref, out_ref, a_scr):
    @pl.when(pl.program_id(1) == 0)
    def _():
        a1, a2, a3 = l1[...], l2[...], l3[...]
        mx = jnp.maximum(jnp.maximum(a1, a2), a3)
        e1, e2, e3 = jnp.exp(a1 - mx), jnp.exp(a2 - mx), jnp.exp(a3 - mx)
        oa = (e1 * o1[...] + e2 * o2[...] + e3 * o3[...]) / (e1 + e2 + e3)
        a_scr[:, :A_WIDTH] = oa.astype(BF16)
        a_scr[:, A_WIDTH:] = ob_ref[...]

    out_ref[...] = x_ref[...] + jnp.dot(a_scr[...], w_ref[...], preferred_element_type=F32)


def even_out_proj(branches, ob, w, x, tm=512, tn=512):
    m, d = x.shape
    half = pl.BlockSpec((tm, A_WIDTH), lambda i, j: (i, 0))
    flat = [a for pair in branches for a in pair]
    return pl.pallas_call(
        _even_out_body,
        grid=(m // tm, d // tn),
        in_specs=[half] * 6 + [
            pl.BlockSpec((tm, B_V_WIDTH), lambda i, j: (i, 0)),
            pl.BlockSpec((A_WIDTH + B_V_WIDTH, tn), lambda i, j: (0, j)),
            pl.BlockSpec((tm, tn), lambda i, j: (i, j))],
        out_specs=pl.BlockSpec((tm, tn), lambda i, j: (i, j)),
        out_shape=jax.ShapeDtypeStruct((m, d), F32),
        scratch_shapes=[pltpu.VMEM((tm, A_WIDTH + B_V_WIDTH), BF16)],
        compiler_params=pltpu.CompilerParams(
            dimension_semantics=("parallel", "arbitrary")),
        name="even_out_proj",
    )(*flat, ob, w, x)


def _matmul_res_body(a_ref, w_ref, x_ref, o_ref):
    o_ref[...] = x_ref[...] + jnp.dot(a_ref[...], w_ref[...], preferred_element_type=F32)


def matmul_residual(a, w, x, tm=512, tn=512):
    m, k = a.shape
    n = w.shape[1]
    return pl.pallas_call(
        _matmul_res_body,
        grid=(m // tm, n // tn),
        in_specs=[pl.BlockSpec((tm, k), lambda i, j: (i, 0)),
                  pl.BlockSpec((k, tn), lambda i, j: (0, j)),
                  pl.BlockSpec((tm, tn), lambda i, j: (i, j))],
        out_specs=pl.BlockSpec((tm, tn), lambda i, j: (i, j)),
        out_shape=jax.ShapeDtypeStruct((m, n), F32),
        compiler_params=pltpu.CompilerParams(
            dimension_semantics=("parallel", "parallel")),
        name="matmul_residual",
    )(a, w, x)


def _add_norm_body(x_ref, r_ref, g_ref, o_ref):
    x = x_ref[...] + r_ref[...]
    ms = jnp.mean(x * x, axis=-1, keepdims=True)
    o_ref[...] = x * lax.rsqrt(ms + RMS_EPS) * g_ref[...]


def add_norm(x, r, g, tm=512):
    m, d = x.shape
    row = pl.BlockSpec((tm, d), lambda i: (i, 0))
    return pl.pallas_call(
        _add_norm_body,
        grid=(m // tm,),
        in_specs=[row, row, pl.BlockSpec((1, d), lambda i: (0, 0))],
        out_specs=row,
        out_shape=jax.ShapeDtypeStruct((m, d), F32),
        compiler_params=pltpu.CompilerParams(dimension_semantics=("parallel",)),
        name="add_norm",
    )(x, r, g.reshape(1, d))


PEER_CAND_ROWS = PEER_TOPK + 8 * (PEER_TOPK - 1)


def _peer_topk_body(q_ref, sk_ref, idx_ref, gate_ref, ts_scr, ti_scr, bs_scr, be_scr,
                    gt_scr, it_scr):
    tm = q_ref.shape[0]
    neg_inf = jnp.float32(-jnp.inf)
    key_id = lax.broadcasted_iota(I32, (N_KEYS, tm), 0)

    def pair_body(pr, carry):
        off = pl.multiple_of(pr * PEER_KEY_HALF, PEER_KEY_HALF)
        sc = lax.dot_general(sk_ref[pr], q_ref[:, pl.ds(off, PEER_KEY_HALF)],
                             (((1,), (1,)), ((), ())), preferred_element_type=F32)

        def k_body(k, vals):
            m = jnp.max(vals, axis=0, keepdims=True)
            sel = jnp.min(jnp.where(vals == m, key_id, N_KEYS), axis=0, keepdims=True)
            ts_scr[pr, pl.ds(k, 1), :] = m
            ti_scr[pr, pl.ds(k, 1), :] = sel
            return jnp.where(key_id == sel, neg_inf, vals)

        lax.fori_loop(0, PEER_TOPK, k_body, sc)
        return carry

    lax.fori_loop(0, 2 * PEER_HEADS, pair_body, 0)

    r = lax.broadcasted_iota(I32, (PEER_CAND_ROWS, 1), 0)
    cand_id = jnp.where(r < PEER_TOPK, r,
                        (1 + (r - PEER_TOPK) // 8) * PEER_TOPK + (r - PEER_TOPK) % 8)

    def head_body(h, carry):
        s1, s2 = ts_scr[2 * h], ts_scr[2 * h + 1]
        i1, i2 = ti_scr[2 * h] * N_KEYS, ti_scr[2 * h + 1]
        vals = jnp.concatenate(
            [s1[0:1] + s2] + [s1[a:a + 1] + s2[0:8] for a in range(1, PEER_TOPK)], axis=0)
        eidx = jnp.concatenate(
            [i1[0:1] + i2] + [i1[a:a + 1] + i2[0:8] for a in range(1, PEER_TOPK)], axis=0)

        def k_body(k, vals):
            m = jnp.max(vals, axis=0, keepdims=True)
            sel = jnp.min(jnp.where(vals == m, cand_id, PEER_TOPK * PEER_TOPK),
                          axis=0, keepdims=True)
            hit = cand_id == sel
            bs_scr[pl.ds(k, 1), :] = m
            be_scr[pl.ds(k, 1), :] = jnp.sum(jnp.where(hit, eidx, 0), axis=0, keepdims=True)
            return jnp.where(hit, neg_inf, vals)

        lax.fori_loop(0, PEER_TOPK, k_body, vals)
        bs = bs_scr[...]
        e = jnp.exp(bs - jnp.max(bs, axis=0, keepdims=True))
        row0 = pl.multiple_of(h * PEER_TOPK, PEER_TOPK)
        gt_scr[pl.ds(row0, PEER_TOPK), :] = e / jnp.sum(e, axis=0, keepdims=True)
        it_scr[pl.ds(row0, PEER_TOPK), :] = be_scr[...]
        return carry

    lax.fori_loop(0, PEER_HEADS, head_body, 0)
    gate_ref[...] = gt_scr[...].T
    idx_ref[...] = it_scr[...].T


def peer_topk(q, subkeys, tm=256):
    m = q.shape[0]
    n_sel = PEER_HEADS * PEER_TOPK
    out_spec = pl.BlockSpec((tm, n_sel), lambda i: (i, 0))
    return pl.pallas_call(
        _peer_topk_body,
        grid=(m // tm,),
        in_specs=[pl.BlockSpec((tm, q.shape[1]), lambda i: (i, 0)),
                  pl.BlockSpec(subkeys.shape, lambda i: (0, 0, 0))],
        out_specs=[out_spec, out_spec],
        out_shape=[jax.ShapeDtypeStruct((m, n_sel), I32),
                   jax.ShapeDtypeStruct((m, n_sel), F32)],
        scratch_shapes=[pltpu.VMEM((2 * PEER_HEADS, PEER_TOPK, tm), F32),
                        pltpu.VMEM((2 * PEER_HEADS, PEER_TOPK, tm), I32),
                        pltpu.VMEM((PEER_TOPK, tm), F32),
                        pltpu.VMEM((PEER_TOPK, tm), I32),
                        pltpu.VMEM((n_sel, tm), F32),
                        pltpu.VMEM((n_sel, tm), I32)],
        compiler_params=pltpu.CompilerParams(dimension_semantics=("parallel",)),
        name="peer_topk",
    )(q, subkeys)


def _gelu_gate_body(a_ref, g_ref, o_ref):
    a = a_ref[...]
    o_ref[...] = g_ref[...] * (0.5 * a * (1.0 + lax.erf(a * (2.0 ** -0.5))))


def gelu_gate(act, gate, tm=2048):
    m, n = act.shape
    spec = pl.BlockSpec((tm, n), lambda i: (i, 0))
    return pl.pallas_call(
        _gelu_gate_body,
        grid=(m // tm,),
        in_specs=[spec, spec],
        out_specs=spec,
        out_shape=jax.ShapeDtypeStruct((m, n), F32),
        compiler_params=pltpu.CompilerParams(dimension_semantics=("parallel",)),
        name="gelu_gate",
    )(act, gate)


SC_TOK_CHUNK = 32
SC_RING = 8
SC_BF16_GROUP = 4
SC_FMT = plsc.PackFormat.INTERLEAVED


def _sc_worker_id():
    return lax.axis_index("s") * SC_CORES + lax.axis_index("c")


def pack_bf16_pairs(t):
    half = t.shape[-1] // 2
    bits = lax.bitcast_convert_type(t.astype(BF16).astype(F32), I32)
    return (bits[..., half:] & jnp.int32(-65536)) | lax.shift_right_logical(
        bits[..., :half], jnp.int32(16))


def _sc_row_pipeline(idx_v, table_hbm, rows_v, sems, n_items, groups, compute):
    def gather(item):
        tt, g = item // groups, item % groups
        ids = idx_v[tt, pl.ds(g * SC_LANES, SC_LANES)]
        slot = item % SC_RING
        return pltpu.make_async_copy(table_hbm.at[ids], rows_v.at[slot], sems.at[slot])

    for s in range(SC_RING - 1):
        gather(s).start()

    def item_body(item, carry):
        nxt = item + SC_RING - 1

        @pl.when(nxt < n_items)
        def _():
            gather(nxt).start()

        gather(item).wait()
        compute(item // groups, item % groups, item % SC_RING)
        return carry

    lax.fori_loop(0, n_items, item_body, 0)


def peer_expert_dots(hp, idx, up):
    m, dw = hp.shape
    n_sel = idx.shape[1]
    per_w = m // SC_WORKERS
    n_chunks = per_w // SC_TOK_CHUNK
    groups = n_sel // SC_LANES
    step = SC_BF16_GROUP * SC_LANES
    mesh = plsc.VectorSubcoreMesh(core_axis_name="c", subcore_axis_name="s")

    @functools.partial(
        pl.kernel, mesh=mesh,
        out_type=jax.ShapeDtypeStruct((m, n_sel), F32),
        scratch_types=[
            pltpu.VMEM((SC_TOK_CHUNK, n_sel), I32),
            pltpu.VMEM((SC_TOK_CHUNK, dw), I32),
            pltpu.VMEM((SC_TOK_CHUNK, n_sel), F32),
            pltpu.VMEM((SC_RING, SC_LANES, dw), I32),
            pltpu.VMEM((SC_LANES * SC_LANES,), F32),
            pltpu.SemaphoreType.DMA((SC_RING,)),
        ],
        compiler_params=pltpu.CompilerParams(needs_layout_passes=False),
        name="peer_expert_dots",
    )
    def k(h_hbm, idx_hbm, u_hbm, act_hbm, idx_v, h_v, act_v, rows_v, part_v, sems):
        base = _sc_worker_id() * per_w
        lane = lax.broadcasted_iota(I32, (SC_LANES,), 0)

        def compute(tt, g, slot):
            def grp_body(q, accs):
                off = pl.multiple_of(q * step, step)
                xs = [plsc.bitcast(h_v[tt, pl.ds(off + c * SC_LANES, SC_LANES)], BF16)
                      for c in range(SC_BF16_GROUP)]
                new = []
                for e in range(SC_LANES):
                    s = None
                    for c in range(SC_BF16_GROUP):
                        p = plsc.bitcast(
                            rows_v[slot, e, pl.ds(off + c * SC_LANES, SC_LANES)], BF16) * xs[c]
                        s = p if s is None else s + p
                    lo, hi = plsc.unpack(s, format=SC_FMT)
                    new.append(accs[e] + (lo + hi))
                return tuple(new)

            accs = lax.fori_loop(
                0, dw // step, grp_body,
                tuple(jnp.zeros((SC_LANES,), F32) for _ in range(SC_LANES)))
            for e in range(SC_LANES):
                part_v[pl.ds(e * SC_LANES, SC_LANES)] = accs[e]
            tot = jnp.zeros((SC_LANES,), F32)
            for l in range(SC_LANES):
                tot = tot + plsc.load_gather(part_v, [lane * SC_LANES + l])
            act_v[tt, pl.ds(g * SC_LANES, SC_LANES)] = tot

        def chunk_body(c, carry):
            t0 = base + c * SC_TOK_CHUNK
            pltpu.sync_copy(idx_hbm.at[pl.ds(t0, SC_TOK_CHUNK)], idx_v)
            pltpu.sync_copy(h_hbm.at[pl.ds(t0, SC_TOK_CHUNK)], h_v)
            _sc_row_pipeline(idx_v, u_hbm, rows_v, sems, SC_TOK_CHUNK * groups, groups, compute)
            pltpu.sync_copy(act_v, act_hbm.at[pl.ds(t0, SC_TOK_CHUNK)])
            return carry

        lax.fori_loop(0, n_chunks, chunk_body, 0)

    return k(hp, idx, up)


def peer_expert_combine(w, idx, vp):
    m, n_sel = w.shape
    dw = vp.shape[1]
    d = 2 * dw
    per_w = m // SC_WORKERS
    n_chunks = per_w // SC_TOK_CHUNK
    n_vec = d // SC_LANES
    groups = n_sel // SC_LANES
    mesh = plsc.VectorSubcoreMesh(core_axis_name="c", subcore_axis_name="s")

    @functools.partial(
        pl.kernel, mesh=mesh,
        out_type=jax.ShapeDtypeStruct((m, d), F32),
        scratch_types=[
            pltpu.VMEM((SC_TOK_CHUNK, n_sel), I32),
            pltpu.VMEM((SC_TOK_CHUNK, n_sel), F32),
            pltpu.VMEM((SC_TOK_CHUNK, d), F32),
            pltpu.VMEM((SC_RING, SC_LANES, dw), I32),
            pltpu.SemaphoreType.DMA((SC_RING,)),
        ],
        compiler_params=pltpu.CompilerParams(needs_layout_passes=False),
        name="peer_expert_combine",
    )
    def k(w_hbm, idx_hbm, v_hbm, out_hbm, idx_v, w_v, out_v, rows_v, sems):
        base = _sc_worker_id() * per_w

        def compute(tt, g, slot):
            splat = []
            for e in range(SC_LANES):
                s = plsc.load_gather(w_v, [jnp.full((SC_LANES,), tt, I32),
                                           jnp.full((SC_LANES,), g * SC_LANES + e, I32)])
                splat.append(plsc.pack(s, s, format=SC_FMT))

            @plsc.parallel_loop(0, dw // SC_LANES)
            def _(j):
                off = pl.multiple_of(j * SC_LANES, SC_LANES)
                acc_lo = out_v[tt, pl.ds(off, SC_LANES)]
                acc_hi = out_v[tt, pl.ds(dw + off, SC_LANES)]
                for e0 in range(0, SC_LANES, SC_BF16_GROUP):
                    s = None
                    for e in range(e0, e0 + SC_BF16_GROUP):
                        p = plsc.bitcast(rows_v[slot, e, pl.ds(off, SC_LANES)], BF16) * splat[e]
                        s = p if s is None else s + p
                    lo, hi = plsc.unpack(s, format=SC_FMT)
                    acc_lo = acc_lo + lo
                    acc_hi = acc_hi + hi
                out_v[tt, pl.ds(off, SC_LANES)] = acc_lo
                out_v[tt, pl.ds(dw + off, SC_LANES)] = acc_hi

        def chunk_body(c, carry):
            t0 = base + c * SC_TOK_CHUNK
            pltpu.sync_copy(idx_hbm.at[pl.ds(t0, SC_TOK_CHUNK)], idx_v)
            pltpu.sync_copy(w_hbm.at[pl.ds(t0, SC_TOK_CHUNK)], w_v)

            def zero_body(z, carry2):
                tt, j = z // n_vec, z % n_vec
                out_v[tt, pl.ds(pl.multiple_of(j * SC_LANES, SC_LANES), SC_LANES)] = (
                    jnp.zeros((SC_LANES,), F32))
                return carry2

            lax.fori_loop(0, SC_TOK_CHUNK * n_vec, zero_body, 0)
            _sc_row_pipeline(idx_v, v_hbm, rows_v, sems, SC_TOK_CHUNK * groups, groups, compute)
            pltpu.sync_copy(out_v, out_hbm.at[pl.ds(t0, SC_TOK_CHUNK)])
            return carry

        lax.fori_loop(0, n_chunks, chunk_body, 0)

    return k(w, idx, vp)


def kernel(x, norm_mix_g, norm_ffn_g, final_norm_g, rel_bias, even_w_in, even_w_out,
           diff_lambda, diff_ln_g, odd_w_in, odd_b_f, odd_w_out, peer_wq, peer_subkeys,
           peer_u, peer_v):
    batch, seq, d = x.shape

    dil_tiles = [dilated_bias_tile(rel_bias[:, :A_HEADS], w, dl) for w, dl in DILATED_BRANCHES]
    diff_tiles = diff_bias_tiles(rel_bias[:, A_HEADS:], seq)
    lam_init = 0.8 - 0.6 * math.exp(-0.3 * 0)
    even_in, even_out = even_w_in[0].astype(BF16), even_w_out[0].astype(BF16)
    w_in = odd_w_in[0]
    odd_in, odd_out = w_in[:, :3 * C_WIDTH].astype(BF16), odd_w_out[0].astype(BF16)
    w_gate = jnp.pad(w_in[:, 3 * C_WIDTH:], ((0, 0), (0, LANES - C_HEADS)))
    b_f = jnp.pad(odd_b_f[0], (0, LANES - C_HEADS)).reshape(1, LANES)
    peer = [dict(wq=peer_wq[l].astype(BF16),
                 sk=peer_subkeys[l].reshape(2 * PEER_HEADS, N_KEYS, PEER_KEY_HALF).astype(BF16),
                 up=pack_bf16_pairs(peer_u[l]), vp=pack_bf16_pairs(peer_v[l]))
            for l in range(2)]

    bg = batch // BATCH_GROUPS
    m = bg * seq
    groups = [dict(x=x[g * bg:(g + 1) * bg].reshape(m, d)) for g in range(BATCH_GROUPS)]

    def peer_select(st, layer):
        q, hp = norm_matmul(st["x"], norm_ffn_g[layer], peer[layer]["wq"], want_h="packed")
        idx, gate = peer_topk(q, peer[layer]["sk"])
        return dict(st=st, layer=layer, idx=idx, gate=gate,
                    act=peer_expert_dots(hp, idx, peer[layer]["up"]))

    def peer_combine(rec):
        rec["w"] = gelu_gate(rec["act"], rec["gate"])
        rec["st"]["peer"] = peer_expert_combine(rec["w"], rec["idx"], peer[rec["layer"]]["vp"])

    def even_layer(st):
        (p,) = norm_matmul(st["x"], norm_mix_g[0], even_in)
        branches = [dilated_branch(p, tile, dl, bg, seq)
                    for tile, (_, dl) in zip(dil_tiles, DILATED_BRANCHES)]
        ob = diff_attention(p.reshape(bg, seq, -1), diff_tiles, diff_lambda[0], diff_ln_g[0],
                            lam_init, bg, seq)
        st["x"] = even_out_proj(branches, ob.reshape(m, B_V_WIDTH), even_out, st["x"])
        return peer_select(st, 0)

    def odd_layer(st):
        p, st["x"], h = norm_matmul(st["x"], norm_mix_g[1], odd_in, res=st["peer"], want_h="f32")
        fg = gate_matmul(h, w_gate)
        c = logsig_cumsum(fg.reshape(bg, seq, LANES), b_f, bg, seq)
        ct = c[:, :, :C_HEADS].transpose(0, 2, 1)
        o = fox_attention(p.reshape(bg, seq, -1), ct[:, :, :, None], ct[:, :, None, :], bg, seq)
        st["x"] = matmul_residual(o.reshape(m, C_WIDTH), odd_out, st["x"])
        return peer_select(st, 1)

    stages = [(even_layer, st) for st in groups] + [(odd_layer, st) for st in groups]
    recs = []
    for k, (fn, st) in enumerate(stages):
        if k >= 2:
            st["x"], recs[k - 2]["w"] = lax.optimization_barrier((st["x"], recs[k - 2]["w"]))
        rec = fn(st)
        if k >= 1:
            prev = recs[k - 1]
            prev["act"], rec["idx"] = lax.optimization_barrier((prev["act"], rec["idx"]))
            peer_combine(prev)
        recs.append(rec)
    peer_combine(recs[-1])
    outs = [add_norm(st["x"], st["peer"], final_norm_g).reshape(bg, seq, d) for st in groups]
    return jnp.concatenate(outs, axis=0)


def _gate_matmul_body(h_ref, w_ref, o_ref):
    o_ref[...] = jnp.dot(h_ref[...], w_ref[...], preferred_element_type=F32,
                         precision=lax.Precision.HIGHEST)


def gate_matmul(h, w, tm=512):
    m, d = h.shape
    n = w.shape[1]
    return pl.pallas_call(
        _gate_matmul_body,
        grid=(m // tm,),
        in_specs=[pl.BlockSpec((tm, d), lambda i: (i, 0)),
                  pl.BlockSpec((d, n), lambda i: (0, 0))],
        out_specs=pl.BlockSpec((tm, n), lambda i: (i, 0)),
        out_shape=jax.ShapeDtypeStruct((m, n), F32),
        compiler_params=pltpu.CompilerParams(dimension_semantics=("parallel",)),
        name="gate_matmul",
    )(h, w)
```

```python
import functools
import math

import numpy as np
import jax
import jax.numpy as jnp
from jax import lax
from jax.experimental import pallas as pl
from jax.experimental.pallas import tpu as pltpu
from jax.experimental.pallas import tpu_sc as plsc

F32 = jnp.float32
BF16 = jnp.bfloat16
I32 = jnp.int32

D_MODEL = 1024
HEAD_DIM = 64
A_HEADS = 8
DILATED_BRANCHES = ((128, 1), (512, 4), (2048, 16))
DIFF_HALF = 64
DIFF_VDIM = 128
B_HEADS = 4
C_HEADS = 16
N_BUCKETS = 32
MAX_DISTANCE = 2048
PEER_HEADS = 8
N_KEYS = 128
PEER_TOPK = 16
PEER_KEY_HALF = 128
RMS_EPS = 1e-6
NEG_INF = -1e30
A_WIDTH = A_HEADS * HEAD_DIM
B_QK_WIDTH = B_HEADS * 2 * DIFF_HALF
B_V_WIDTH = B_HEADS * DIFF_VDIM
C_WIDTH = C_HEADS * HEAD_DIM
QK_SCALE = 0.125
LANES = 128
DIL_BLOCK = 128
ATT_BLOCK = 512
BATCH_GROUPS = 4

SC_CORES = 2
SC_SUBCORES = 16
SC_LANES = 16
SC_WORKERS = SC_CORES * SC_SUBCORES


def _t5_bucket_table(n):
    max_exact = N_BUCKETS // 2
    d = np.arange(n)
    df = np.maximum(d, 1).astype(np.float32)
    large = max_exact + (
        np.log(df / np.float32(max_exact)) / np.float32(math.log(MAX_DISTANCE / max_exact))
        * np.float32(N_BUCKETS - max_exact)).astype(np.int32)
    large = np.minimum(large, N_BUCKETS - 1)
    return np.where(d < max_exact, d, large).astype(np.int32)


def _norm_matmul_body(*refs, has_res, want_h):
    it = iter(refs)
    x_ref = next(it)
    r_ref = next(it) if has_res else None
    g_ref = next(it)
    w_ref = next(it)
    o_ref = next(it)
    xs_ref = next(it) if has_res else None
    hout_ref = next(it) if want_h else None
    h_scr = next(it)

    @pl.when(pl.program_id(1) == 0)
    def _():
        x = x_ref[...]
        if has_res:
            x = x + r_ref[...]
            xs_ref[...] = x
        ms = jnp.mean(x * x, axis=-1, keepdims=True)
        h = x * lax.rsqrt(ms + RMS_EPS) * g_ref[...]
        if want_h == "f32":
            hout_ref[...] = h
        elif want_h == "packed":
            hout_ref[...] = pack_bf16_pairs(h)
        h_scr[...] = h.astype(BF16)

    o_ref[...] = jnp.dot(h_scr[...], w_ref[...],
                         preferred_element_type=F32).astype(o_ref.dtype)


def norm_matmul(x, g, w, *, res=None, want_h=None, out_dtype=BF16, tm=512, tn=512):
    m, d = x.shape
    n = w.shape[1]
    tn = min(tn, n)
    row = pl.BlockSpec((tm, d), lambda i, j: (i, 0))
    in_specs = [row] + ([row] if res is not None else []) + [
        pl.BlockSpec((1, d), lambda i, j: (0, 0)),
        pl.BlockSpec((d, tn), lambda i, j: (0, j))]
    out_specs = [pl.BlockSpec((tm, tn), lambda i, j: (i, j))]
    out_shape = [jax.ShapeDtypeStruct((m, n), out_dtype)]
    if res is not None:
        out_specs.append(row)
        out_shape.append(jax.ShapeDtypeStruct((m, d), F32))
    if want_h == "f32":
        out_specs.append(row)
        out_shape.append(jax.ShapeDtypeStruct((m, d), F32))
    elif want_h == "packed":
        out_specs.append(pl.BlockSpec((tm, d // 2), lambda i, j: (i, 0)))
        out_shape.append(jax.ShapeDtypeStruct((m, d // 2), I32))
    args =[x] + ([res] if res is not None else []) + [g.reshape(1, d), w]
    return pl.pallas_call(
        functools.partial(_norm_matmul_body, has_res=res is not None, want_h=want_h),
        grid=(m // tm, n // tn),
        in_specs=in_specs,
        out_specs=out_specs,
        out_shape=out_shape,
        scratch_shapes=[pltpu.VMEM((tm, d), BF16)],
        compiler_params=pltpu.CompilerParams(
            dimension_semantics=("parallel", "arbitrary")),
        name="norm_matmul",
    )(*args)


def _dilated_body(q_ref, kp_ref, kc_ref, vp_ref, vc_ref, b_ref, o_ref, lse_ref):
    i = pl.program_id(3)
    q = q_ref[...] * QK_SCALE
    k = jnp.concatenate([kp_ref[...], kc_ref[...]], axis=0)
    v = jnp.concatenate([vp_ref[...], vc_ref[...]], axis=0)
    col = lax.broadcasted_iota(I32, (DIL_BLOCK, 2 * DIL_BLOCK), 1)
    has_prev = jnp.logical_or(col >= DIL_BLOCK, i > 0)
    outs, lses = [], []
    for hh in range(2):
        sl = slice(hh * HEAD_DIM, (hh + 1) * HEAD_DIM)
        s = lax.dot_general(q[:, sl], k[:, sl], (((1,), (1,)), ((), ())),
                            preferred_element_type=F32)
        s = jnp.where(has_prev, s + b_ref[hh], NEG_INF)
        m = jnp.max(s, axis=-1, keepdims=True)
        p = jnp.exp(s - m)
        l = jnp.sum(p, axis=-1, keepdims=True)
        o = jnp.dot(p.astype(BF16), v[:, sl], preferred_element_type=F32) / l
        outs.append(o)
        lses.append(jnp.broadcast_to(m + jnp.log(l), (DIL_BLOCK, HEAD_DIM)))
    o_ref[...] = jnp.concatenate(outs, axis=1)
    lse_ref[...] = jnp.concatenate(lses, axis=1)


def dilated_branch(p, bias_tile, dil, batch, seq):
    n_cols = p.shape[1]
    cb = n_cols // LANES
    rows = seq // dil
    nblk = rows // DIL_BLOCK
    pv = p.reshape(batch, rows, dil * n_cols)
    kq, kk, kv = 0, A_WIDTH // LANES, 2 * A_WIDTH // LANES
    blk = (None, DIL_BLOCK, LANES)

    def spec(col0, prev):
        if prev:
            return pl.BlockSpec(blk, lambda b, h, r, i: (b, jnp.maximum(i - 1, 0), r * cb + col0 + h))
        return pl.BlockSpec(blk, lambda b, h, r, i: (b, i, r * cb + col0 + h))

    ocb = A_WIDTH // LANES
    ospec = pl.BlockSpec(blk, lambda b, h, r, i: (b, i, r * ocb + h))
    oshape = jax.ShapeDtypeStruct((batch, rows, dil * A_WIDTH), F32)
    o, lse = pl.pallas_call(
        _dilated_body,
        grid=(batch, A_HEADS // 2, dil, nblk),
        in_specs=[spec(kq, False), spec(kk, True), spec(kk, False),
                  spec(kv, True), spec(kv, False),
                  pl.BlockSpec((2, DIL_BLOCK, 2 * DIL_BLOCK), lambda b, h, r, i: (h, 0, 0))],
        out_specs=[ospec, ospec],
        out_shape=[oshape, oshape],
        compiler_params=pltpu.CompilerParams(
            dimension_semantics=("parallel", "parallel", "parallel", "arbitrary")),
        name=f"dilated_d{dil}",
    )(pv, pv, pv, pv, pv, bias_tile)
    return o.reshape(batch * seq, A_WIDTH), lse.reshape(batch * seq, A_WIDTH)


def dilated_bias_tile(rel_bias_a, window, dil):
    n = window // dil
    assert n == DIL_BLOCK
    bucket = _t5_bucket_table(window + 1)
    row = np.arange(DIL_BLOCK)[:, None]
    c = np.arange(2 * DIL_BLOCK)[None, :]
    j = DIL_BLOCK + row - c
    valid = (j >= 0) & (j <= n)
    bk = bucket[np.clip(j, 0, n) * dil]
    tile = rel_bias_a.T[:, bk]
    return jnp.where(jnp.asarray(valid)[None], tile, NEG_INF).astype(F32)


def _diff_body(q_ref, k_ref, v_ref, b_ref, lam_ref, g_ref, o_ref,
               m_scr, l_scr, acc_scr, *, n_tiles, lam_init):
    t = ATT_BLOCK
    qi = pl.program_id(2)
    q = q_ref[...] * QK_SCALE
    lane = lax.broadcasted_iota(I32, (1, LANES), 1)
    qa = [jnp.where((lane < DIFF_HALF) == (a == 0), q, jnp.zeros_like(q)) for a in range(2)]
    ones = jnp.ones((t, LANES), BF16)
    for a in range(2):
        m_scr[a] = jnp.full((t, LANES), NEG_INF, F32)
        l_scr[a] = jnp.zeros((t, LANES), F32)
        acc_scr[a] = jnp.zeros((t, DIFF_VDIM), F32)

    def step(j, masked):
        off = pl.multiple_of(j * t, t)
        ks = k_ref[pl.ds(off, t), :]
        vs = v_ref[pl.ds(off, t), :]
        bias = b_ref[jnp.minimum(qi - j, n_tiles - 1)]
        if masked:
            row = lax.broadcasted_iota(I32, (t, t), 0)
            col = lax.broadcasted_iota(I32, (t, t), 1)
            causal = row >= col
        for a in range(2):
            s = lax.dot_general(qa[a], ks, (((1,), (1,)), ((), ())),
                                preferred_element_type=F32) + bias
            if masked:
                s = jnp.where(causal, s, NEG_INF)
            m_prev = m_scr[a]
            m_new = jnp.maximum(m_prev, jnp.max(s, axis=-1, keepdims=True))
            alpha = jnp.exp(m_prev - m_new)
            pb = jnp.exp(s - jnp.concatenate([m_new] * (t // LANES), axis=1)).astype(BF16)
            l_scr[a] = alpha * l_scr[a] + jnp.dot(pb, ones, preferred_element_type=F32)
            acc_scr[a] = alpha * acc_scr[a] + jnp.dot(pb, vs, preferred_element_type=F32)
            m_scr[a] = m_new

    def loop_body(j, carry):
        step(j, False)
        return carry

    lax.fori_loop(0, qi, loop_body, 0)
    step(qi, True)

    lp = lam_ref[...]
    lam = (jnp.exp(jnp.sum(lp[0:1] * lp[1:2])) - jnp.exp(jnp.sum(lp[2:3] * lp[3:4]))
           + lam_init)
    o = acc_scr[0] / l_scr[0] - lam * (acc_scr[1] / l_scr[1])
    ms = jnp.mean(o * o, axis=-1, keepdims=True)
    y = o * lax.rsqrt(ms + RMS_EPS) * g_ref[...]
    o_ref[...] = (y * (1.0 - lam_init)).astype(o_ref.dtype)


def diff_attention(p, bias_tiles, lam_params, ln_g, lam_init, batch, seq):
    t = ATT_BLOCK
    n_tiles = bias_tiles.shape[1]
    cq = 3 * A_WIDTH // LANES
    ck = cq + B_QK_WIDTH // LANES
    cv = ck + B_QK_WIDTH // LANES
    return pl.pallas_call(
        functools.partial(_diff_body, n_tiles=n_tiles, lam_init=lam_init),
        grid=(batch, B_HEADS, seq // t),
        in_specs=[
            pl.BlockSpec((None, t, LANES), lambda b, h, i: (b, i, cq + h)),
            pl.BlockSpec((None, seq, LANES), lambda b, h, i: (b, 0, ck + h)),
            pl.BlockSpec((None, seq, LANES), lambda b, h, i: (b, 0, cv + h)),
            pl.BlockSpec((None, n_tiles, t, t), lambda b, h, i: (h, 0, 0, 0)),
            pl.BlockSpec((4, DIFF_HALF), lambda b, h, i: (0, 0)),
            pl.BlockSpec((1, DIFF_VDIM), lambda b, h, i: (0, 0)),
        ],
        out_specs=pl.BlockSpec((None, t, LANES), lambda b, h, i: (b, i, h)),
        out_shape=jax.ShapeDtypeStruct((batch, seq, B_V_WIDTH), BF16),
        scratch_shapes=[pltpu.VMEM((2, t, LANES), F32), pltpu.VMEM((2, t, LANES), F32),
                        pltpu.VMEM((2, t, DIFF_VDIM), F32)],
        compiler_params=pltpu.CompilerParams(
            dimension_semantics=("parallel", "parallel", "arbitrary")),
        name="diff_attention",
    )(p, p, p, bias_tiles, lam_params, ln_g.reshape(1, DIFF_VDIM))


def diff_bias_tiles(rel_bias_b, seq):
    t = ATT_BLOCK
    bucket = _t5_bucket_table(max(seq, 2 * MAX_DISTANCE) + 2 * t)
    sat = bucket[-1]
    d_sat = int(np.max(np.nonzero(bucket != sat)[0])) + 1
    n_full = (d_sat + t - 1 + t - 1) // t
    n_tiles = n_full + 1
    assert n_full * t - (t - 1) >= d_sat
    n = np.arange(2 * t)[None, :]
    base = np.arange(n_tiles)[:, None] * t
    dist = np.clip(np.where(n < t, base - n, base + 2 * t - n), 0, None)
    w = rel_bias_b.T[:, bucket[dist]].astype(F32)
    rep = jnp.broadcast_to(w[:, :, None, :], (B_HEADS, n_tiles, t, 2 * t))
    flat = rep.reshape(B_HEADS, n_tiles, 2 * t * t)[:, :, :t * (2 * t - 1)]
    return flat.reshape(B_HEADS, n_tiles, t, 2 * t - 1)[:, :, :, :t]


def _logsig_cumsum_body(f_ref, b_ref, c_ref, carry_scr):
    t = f_ref.shape[0]

    @pl.when(pl.program_id(1) == 0)
    def _():
        carry_scr[...] = jnp.zeros_like(carry_scr)

    x = f_ref[...] + b_ref[...]
    ls = jnp.minimum(x, 0.0) - jnp.log1p(jnp.exp(-jnp.abs(x)))
    row = lax.broadcasted_iota(I32, (t, t), 0)
    col = lax.broadcasted_iota(I32, (t, t), 1)
    tri = (row >= col).astype(F32)
    c = jnp.dot(tri, ls, preferred_element_type=F32,
                precision=lax.Precision.HIGHEST) + carry_scr[...]
    c_ref[...] = c
    carry_scr[...] = c[t - 1:t, :]


def logsig_cumsum(fg, b_f, batch, seq, t=512):
    return pl.pallas_call(
        _logsig_cumsum_body,
        grid=(batch, seq // t),
        in_specs=[pl.BlockSpec((None, t, LANES), lambda b, i: (b, i, 0)),
                  pl.BlockSpec((1, LANES), lambda b, i: (0, 0))],
        out_specs=pl.BlockSpec((None, t, LANES), lambda b, i: (b, i, 0)),
        out_shape=jax.ShapeDtypeStruct((batch, seq, LANES), F32),
        scratch_shapes=[pltpu.VMEM((1, LANES), F32)],
        compiler_params=pltpu.CompilerParams(
            dimension_semantics=("parallel", "arbitrary")),
        name="logsig_cumsum",
    )(fg, b_f)


def _fox_body(q_ref, k_ref, v_ref, cq_ref, ck_ref, o_ref, m_scr, acc_scr):
    t = ATT_BLOCK
    qi = pl.program_id(2)
    q = q_ref[...] * QK_SCALE
    lane = lax.broadcasted_iota(I32, (1, LANES), 1)
    own = [(lane < HEAD_DIM) == (a == 0) for a in range(2)]
    qa = [jnp.where(own[a], q, jnp.zeros_like(q)) for a in range(2)]
    cqb = [jnp.broadcast_to(cq_ref[a], (t, LANES)) for a in range(2)]
    for a in range(2):
        m_scr[a] = jnp.full((t, LANES), NEG_INF, F32)
        acc_scr[a] = jnp.zeros((t, LANES), F32)

    def step(j, masked):
        off = pl.multiple_of(j * t, t)
        ks = k_ref[pl.ds(off, t), :]
        vs = v_ref[pl.ds(off, t), :]
        if masked:
            row = lax.broadcasted_iota(I32, (t, t), 0)
            col = lax.broadcasted_iota(I32, (t, t), 1)
            causal = row >= col
        for a in range(2):
            s = lax.dot_general(qa[a], ks, (((1,), (1,)), ((), ())),
                                preferred_element_type=F32)
            s = s - ck_ref[a, :, pl.ds(off, t)]
            if masked:
                s = jnp.where(causal, s, NEG_INF)
            m_prev = m_scr[a]
            m_new = jnp.maximum(m_prev, jnp.max(s, axis=-1, keepdims=True) + cqb[a])
            alpha = jnp.exp(m_prev - m_new)
            shift = m_new - cqb[a]
            pb = jnp.exp(s - jnp.concatenate([shift] * (t // LANES), axis=1)).astype(BF16)
            v_aug = jnp.where(own[a], vs, jnp.ones_like(vs))
            acc_scr[a] = alpha * acc_scr[a] + jnp.dot(pb, v_aug, preferred_element_type=F32)
            m_scr[a] = m_new

    def loop_body(j, carry):
        step(j, False)
        return carry

    lax.fori_loop(0, qi, loop_body, 0)
    step(qi, True)
    r = [acc_scr[a] / pltpu.roll(acc_scr[a], HEAD_DIM, 1) for a in range(2)]
    o_ref[...] = jnp.where(own[0], r[0], r[1]).astype(o_ref.dtype)


def fox_attention(p, cq, ck, batch, seq):
    t = ATT_BLOCK
    nk = C_WIDTH // LANES
    return pl.pallas_call(
        _fox_body,
        grid=(batch, C_HEADS // 2, seq // t),
        in_specs=[
            pl.BlockSpec((None, t, LANES), lambda b, h, i: (b, i, h)),
            pl.BlockSpec((None, seq, LANES), lambda b, h, i: (b, 0, nk + h)),
            pl.BlockSpec((None, seq, LANES), lambda b, h, i: (b, 0, 2 * nk + h)),
            pl.BlockSpec((None, 2, t, 1), lambda b, h, i: (b, h, i, 0)),
            pl.BlockSpec((None, 2, 1, seq), lambda b, h, i: (b, h, 0, 0)),
        ],
        out_specs=pl.BlockSpec((None, t, LANES), lambda b, h, i: (b, i, h)),
        out_shape=jax.ShapeDtypeStruct((batch, seq, C_WIDTH), BF16),
        scratch_shapes=[pltpu.VMEM((2, t, LANES), F32), pltpu.VMEM((2, t, LANES), F32)],
        compiler_params=pltpu.CompilerParams(
            dimension_semantics=("parallel", "parallel", "arbitrary")),
        name="fox_attention",
    )(p, p, p, cq, ck)


def _even_out_body(o1, l1, o2, l2, o3, l3, ob_ref, w_ref, x_ref, out_ref, a_scr):
    @pl.when(pl.program_id(1) == 0)
    def _():
        a1, a2, a3 = l1[...], l2[...], l3[...]
        mx = jnp.maximum(jnp.maximum(a1, a2), a3)
        e1, e2, e3 = jnp.exp(a1 - mx), jnp.exp(a2 - mx), jnp.exp(a3 - mx)
        oa = (e1 * o1[...] + e2 * o2[...] + e3 * o3[...]) / (e1 + e2 + e3)
        a_scr[:, :A_WIDTH] = oa.astype(BF16)
        a_scr[:, A_WIDTH:] = ob_ref[...]

    out_ref[...] = x_ref[...] + jnp.dot(a_scr[...], w_ref[...], preferred_element_type=F32)


def even_out_proj(branches, ob, w, x, tm=512, tn=512):
    m, d = x.shape
    half = pl.BlockSpec((tm, A_WIDTH), lambda i, j: (i, 0))
    flat = [a for pair in branches for a in pair]
    return pl.pallas_call(
        _even_out_body,
        grid=(m // tm, d // tn),
        in_specs=[half] * 6 + [
            pl.BlockSpec((tm, B_V_WIDTH), lambda i, j: (i, 0)),
            pl.BlockSpec((A_WIDTH + B_V_WIDTH, tn), lambda i, j: (0, j)),
            pl.BlockSpec((tm, tn), lambda i, j: (i, j))],
        out_specs=pl.BlockSpec((tm, tn), lambda i, j: (i, j)),
        out_shape=jax.ShapeDtypeStruct((m, d), F32),
        scratch_shapes=[pltpu.VMEM((tm, A_WIDTH + B_V_WIDTH), BF16)],
        compiler_params=pltpu.CompilerParams(
            dimension_semantics=("parallel", "arbitrary")),
        name="even_out_proj",
    )(*flat, ob, w, x)


def _matmul_res_body(a_ref, w_ref, x_ref, o_ref):
    o_ref[...] = x_ref[...] + jnp.dot(a_ref[...], w_ref[...], preferred_element_type=F32)


def matmul_residual(a, w, x, tm=512, tn=512):
    m, k = a.shape
    n = w.shape[1]
    return pl.pallas_call(
        _matmul_res_body,
        grid=(m // tm, n // tn),
        in_specs=[pl.BlockSpec((tm, k), lambda i, j: (i, 0)),
                  pl.BlockSpec((k, tn), lambda i, j: (0, j)),
                  pl.BlockSpec((tm, tn), lambda i, j: (i, j))],
        out_specs=pl.BlockSpec((tm, tn), lambda i, j: (i, j)),
        out_shape=jax.ShapeDtypeStruct((m, n), F32),
        compiler_params=pltpu.CompilerParams(
            dimension_semantics=("parallel", "parallel")),
        name="matmul_residual",
    )(a, w, x)


def _add_norm_body(x_ref, r_ref, g_ref, o_ref):
    x = x_ref[...] + r_ref[...]
    ms = jnp.mean(x * x, axis=-1, keepdims=True)
    o_ref[...] = x * lax.rsqrt(ms + RMS_EPS) * g_ref[...]


def add_norm(x, r, g, tm=512):
    m, d = x.shape
    row = pl.BlockSpec((tm, d), lambda i: (i, 0))
    return pl.pallas_call(
        _add_norm_body,
        grid=(m // tm,),
        in_specs=[row, row, pl.BlockSpec((1, d), lambda i: (0, 0))],
        out_specs=row,
        out_shape=jax.ShapeDtypeStruct((m, d), F32),
        compiler_params=pltpu.CompilerParams(dimension_semantics=("parallel",)),
        name="add_norm",
    )(x, r, g.reshape(1, d))


PEER_CAND_ROWS = PEER_TOPK + 8 * (PEER_TOPK - 1)


def _peer_topk_body(q_ref, sk_ref, idx_ref, gate_ref, ts_scr, ti_scr, bs_scr, be_scr,
                    gt_scr, it_scr):
    tm = q_ref.shape[0]
    neg_inf = jnp.float32(-jnp.inf)
    key_id = lax.broadcasted_iota(I32, (N_KEYS, tm), 0)

    def pair_body(pr, carry):
        off = pl.multiple_of(pr * PEER_KEY_HALF, PEER_KEY_HALF)
        sc = lax.dot_general(sk_ref[pr], q_ref[:, pl.ds(off, PEER_KEY_HALF)],
                             (((1,), (1,)), ((), ())), preferred_element_type=F32)

        def k_body(k, vals):
            m = jnp.max(vals, axis=0, keepdims=True)
            sel = jnp.min(jnp.where(vals == m, key_id, N_KEYS), axis=0, keepdims=True)
            ts_scr[pr, pl.ds(k, 1), :] = m
            ti_scr[pr, pl.ds(k, 1), :] = sel
            return jnp.where(key_id == sel, neg_inf, vals)

        lax.fori_loop(0, PEER_TOPK, k_body, sc)
        return carry

    lax.fori_loop(0, 2 * PEER_HEADS, pair_body, 0)

    r = lax.broadcasted_iota(I32, (PEER_CAND_ROWS, 1), 0)
    cand_id = jnp.where(r < PEER_TOPK, r,
                        (1 + (r - PEER_TOPK) // 8) * PEER_TOPK + (r - PEER_TOPK) % 8)

    def head_body(h, carry):
        s1, s2 = ts_scr[2 * h], ts_scr[2 * h + 1]
        i1, i2 = ti_scr[2 * h] * N_KEYS, ti_scr[2 * h + 1]
        vals = jnp.concatenate(
            [s1[0:1] + s2] + [s1[a:a + 1] + s2[0:8] for a in range(1, PEER_TOPK)], axis=0)
        eidx = jnp.concatenate(
            [i1[0:1] + i2] + [i1[a:a + 1] + i2[0:8] for a in range(1, PEER_TOPK)], axis=0)

        def k_body(k, vals):
            m = jnp.max(vals, axis=0, keepdims=True)
            sel = jnp.min(jnp.where(vals == m, cand_id, PEER_TOPK * PEER_TOPK),
                          axis=0, keepdims=True)
            hit = cand_id == sel
            bs_scr[pl.ds(k, 1), :] = m
            be_scr[pl.ds(k, 1), :] = jnp.sum(jnp.where(hit, eidx, 0), axis=0, keepdims=True)
            return jnp.where(hit, neg_inf, vals)

        lax.fori_loop(0, PEER_TOPK, k_body, vals)
        bs = bs_scr[...]
        e = jnp.exp(bs - jnp.max(bs, axis=0, keepdims=True))
        row0 = pl.multiple_of(h * PEER_TOPK, PEER_TOPK)
        gt_scr[pl.ds(row0, PEER_TOPK), :] = e / jnp.sum(e, axis=0, keepdims=True)
        it_scr[pl.ds(row0, PEER_TOPK), :] = be_scr[...]
        return carry

    lax.fori_loop(0, PEER_HEADS, head_body, 0)
    gate_ref[...] = gt_scr[...].T
    idx_ref[...] = it_scr[...].T


def peer_topk(q, subkeys, tm=256):
    m = q.shape[0]
    n_sel = PEER_HEADS * PEER_TOPK
    out_spec = pl.BlockSpec((tm, n_sel), lambda i: (i, 0))
    return pl.pallas_call(
        _peer_topk_body,
        grid=(m // tm,),
        in_specs=[pl.BlockSpec((tm, q.shape[1]), lambda i: (i, 0)),
                  pl.BlockSpec(subkeys.shape, lambda i: (0, 0, 0))],
        out_specs=[out_spec, out_spec],
        out_shape=[jax.ShapeDtypeStruct((m, n_sel), I32),
                   jax.ShapeDtypeStruct((m, n_sel), F32)],
        scratch_shapes=[pltpu.VMEM((2 * PEER_HEADS, PEER_TOPK, tm), F32),
                        pltpu.VMEM((2 * PEER_HEADS, PEER_TOPK, tm), I32),
                        pltpu.VMEM((PEER_TOPK, tm), F32),
                        pltpu.VMEM((PEER_TOPK, tm), I32),
                        pltpu.VMEM((n_sel, tm), F32),
                        pltpu.VMEM((n_sel, tm), I32)],
        compiler_params=pltpu.CompilerParams(dimension_semantics=("parallel",)),
        name="peer_topk",
    )(q, subkeys)


def _gelu_gate_body(a_ref, g_ref, o_ref):
    a = a_ref[...]
    o_ref[...] = g_ref[...] * (0.5 * a * (1.0 + lax.erf(a * (2.0 ** -0.5))))


def gelu_gate(act, gate, tm=2048):
    m, n = act.shape
    spec = pl.BlockSpec((tm, n), lambda i: (i, 0))
    return pl.pallas_call(
        _gelu_gate_body,
        grid=(m // tm,),
        in_specs=[spec, spec],
        out_specs=spec,
        out_shape=jax.ShapeDtypeStruct((m, n), F32),
        compiler_params=pltpu.CompilerParams(dimension_semantics=("parallel",)),
        name="gelu_gate",
    )(act, gate)


SC_TOK_CHUNK = 32
SC_RING = 8
SC_BF16_GROUP = 4
SC_FMT = plsc.PackFormat.INTERLEAVED


def _sc_worker_id():
    return lax.axis_index("s") * SC_CORES + lax.axis_index("c")


def pack_bf16_pairs(t):
    half = t.shape[-1] // 2
    bits = lax.bitcast_convert_type(t.astype(BF16).astype(F32), I32)
    return (bits[..., half:] & jnp.int32(-65536)) | lax.shift_right_logical(
        bits[..., :half], jnp.int32(16))


def _sc_row_pipeline(idx_v, table_hbm, rows_v, sems, n_items, groups, compute):
    def gather(item):
        tt, g = item // groups, item % groups
        ids = idx_v[tt, pl.ds(g * SC_LANES, SC_LANES)]
        slot = item % SC_RING
        return pltpu.make_async_copy(table_hbm.at[ids], rows_v.at[slot], sems.at[slot])

    for s in range(SC_RING - 1):
        gather(s).start()

    def item_body(item, carry):
        nxt = item + SC_RING - 1

        @pl.when(nxt < n_items)
        def _():
            gather(nxt).start()

        gather(item).wait()
        compute(item // groups, item % groups, item % SC_RING)
        return carry

    lax.fori_loop(0, n_items, item_body, 0)


def peer_expert_dots(hp, idx, up):
    m, dw = hp.shape
    n_sel = idx.shape[1]
    per_w = m // SC_WORKERS
    n_chunks = per_w // SC_TOK_CHUNK
    groups = n_sel // SC_LANES
    step = SC_BF16_GROUP * SC_LANES
    mesh = plsc.VectorSubcoreMesh(core_axis_name="c", subcore_axis_name="s")

    @functools.partial(
        pl.kernel, mesh=mesh,
        out_type=jax.ShapeDtypeStruct((m, n_sel), F32),
        scratch_types=[
            pltpu.VMEM((SC_TOK_CHUNK, n_sel), I32),
            pltpu.VMEM((SC_TOK_CHUNK, dw), I32),
            pltpu.VMEM((SC_TOK_CHUNK, n_sel), F32),
            pltpu.VMEM((SC_RING, SC_LANES, dw), I32),
            pltpu.VMEM((SC_LANES * SC_LANES,), F32),
            pltpu.SemaphoreType.DMA((SC_RING,)),
        ],
        compiler_params=pltpu.CompilerParams(needs_layout_passes=False),
        name="peer_expert_dots",
    )
    def k(h_hbm, idx_hbm, u_hbm, act_hbm, idx_v, h_v, act_v, rows_v, part_v, sems):
        base = _sc_worker_id() * per_w
        lane = lax.broadcasted_iota(I32, (SC_LANES,), 0)

        def compute(tt, g, slot):
            def grp_body(q, accs):
                off = pl.multiple_of(q * step, step)
                xs = [plsc.bitcast(h_v[tt, pl.ds(off + c * SC_LANES, SC_LANES)], BF16)
                      for c in range(SC_BF16_GROUP)]
                new = []
                for e in range(SC_LANES):
                    s = None
                    for c in range(SC_BF16_GROUP):
                        p = plsc.bitcast(
                            rows_v[slot, e, pl.ds(off + c * SC_LANES, SC_LANES)], BF16) * xs[c]
                        s = p if s is None else s + p
                    lo, hi = plsc.unpack(s, format=SC_FMT)
                    new.append(accs[e] + (lo + hi))
                return tuple(new)

            accs = lax.fori_loop(
                0, dw // step, grp_body,
                tuple(jnp.zeros((SC_LANES,), F32) for _ in range(SC_LANES)))
            for e in range(SC_LANES):
                part_v[pl.ds(e * SC_LANES, SC_LANES)] = accs[e]
            tot = jnp.zeros((SC_LANES,), F32)
            for l in range(SC_LANES):
                tot = tot + plsc.load_gather(part_v, [lane * SC_LANES + l])
            act_v[tt, pl.ds(g * SC_LANES, SC_LANES)] = tot

        def chunk_body(c, carry):
            t0 = base + c * SC_TOK_CHUNK
            pltpu.sync_copy(idx_hbm.at[pl.ds(t0, SC_TOK_CHUNK)], idx_v)
            pltpu.sync_copy(h_hbm.at[pl.ds(t0, SC_TOK_CHUNK)], h_v)
            _sc_row_pipeline(idx_v, u_hbm, rows_v, sems, SC_TOK_CHUNK * groups, groups, compute)
            pltpu.sync_copy(act_v, act_hbm.at[pl.ds(t0, SC_TOK_CHUNK)])
            return carry

        lax.fori_loop(0, n_chunks, chunk_body, 0)

    return k(hp, idx, up)


def peer_expert_combine(w, idx, vp):
    m, n_sel = w.shape
    dw = vp.shape[1]
    d = 2 * dw
    per_w = m // SC_WORKERS
    n_chunks = per_w // SC_TOK_CHUNK
    n_vec = d // SC_LANES
    groups = n_sel // SC_LANES
    mesh = plsc.VectorSubcoreMesh(core_axis_name="c", subcore_axis_name="s")

    @functools.partial(
        pl.kernel, mesh=mesh,
        out_type=jax.ShapeDtypeStruct((m, d), F32),
        scratch_types=[
            pltpu.VMEM((SC_TOK_CHUNK, n_sel), I32),
            pltpu.VMEM((SC_TOK_CHUNK, n_sel), F32),
            pltpu.VMEM((SC_TOK_CHUNK, d), F32),
            pltpu.VMEM((SC_RING, SC_LANES, dw), I32),
            pltpu.SemaphoreType.DMA((SC_RING,)),
        ],
        compiler_params=pltpu.CompilerParams(needs_layout_passes=False),
        name="peer_expert_combine",
    )
    def k(w_hbm, idx_hbm, v_hbm, out_hbm, idx_v, w_v, out_v, rows_v, sems):
        base = _sc_worker_id() * per_w

        def compute(tt, g, slot):
            splat = []
            for e in range(SC_LANES):
                s = plsc.load_gather(w_v, [jnp.full((SC_LANES,), tt, I32),
                                           jnp.full((SC_LANES,), g * SC_LANES + e, I32)])
                splat.append(plsc.pack(s, s, format=SC_FMT))

            @plsc.parallel_loop(0, dw // SC_LANES)
            def _(j):
                off = pl.multiple_of(j * SC_LANES, SC_LANES)
                acc_lo = out_v[tt, pl.ds(off, SC_LANES)]
                acc_hi = out_v[tt, pl.ds(dw + off, SC_LANES)]
                for e0 in range(0, SC_LANES, SC_BF16_GROUP):
                    s = None
                    for e in range(e0, e0 + SC_BF16_GROUP):
                        p = plsc.bitcast(rows_v[slot, e, pl.ds(off, SC_LANES)], BF16) * splat[e]
                        s = p if s is None else s + p
                    lo, hi = plsc.unpack(s, format=SC_FMT)
                    acc_lo = acc_lo + lo
                    acc_hi = acc_hi + hi
                out_v[tt, pl.ds(off, SC_LANES)] = acc_lo
                out_v[tt, pl.ds(dw + off, SC_LANES)] = acc_hi

        def chunk_body(c, carry):
            t0 = base + c * SC_TOK_CHUNK
            pltpu.sync_copy(idx_hbm.at[pl.ds(t0, SC_TOK_CHUNK)], idx_v)
            pltpu.sync_copy(w_hbm.at[pl.ds(t0, SC_TOK_CHUNK)], w_v)

            def zero_body(z, carry2):
                tt, j = z // n_vec, z % n_vec
                out_v[tt, pl.ds(pl.multiple_of(j * SC_LANES, SC_LANES), SC_LANES)] = (
                    jnp.zeros((SC_LANES,), F32))
                return carry2

            lax.fori_loop(0, SC_TOK_CHUNK * n_vec, zero_body, 0)
            _sc_row_pipeline(idx_v, v_hbm, rows_v, sems, SC_TOK_CHUNK * groups, groups, compute)
            pltpu.sync_copy(out_v, out_hbm.at[pl.ds(t0, SC_TOK_CHUNK)])
            return carry

        lax.fori_loop(0, n_chunks, chunk_body, 0)

    return k(w, idx, vp)


def kernel(x, norm_mix_g, norm_ffn_g, final_norm_g, rel_bias, even_w_in, even_w_out,
           diff_lambda, diff_ln_g, odd_w_in, odd_b_f, odd_w_out, peer_wq, peer_subkeys,
           peer_u, peer_v):
    batch, seq, d = x.shape

    dil_tiles = [dilated_bias_tile(rel_bias[:, :A_HEADS], w, dl) for w, dl in DILATED_BRANCHES]
    diff_tiles = diff_bias_tiles(rel_bias[:, A_HEADS:], seq)
    lam_init = 0.8 - 0.6 * math.exp(-0.3 * 0)
    even_in, even_out = even_w_in[0].astype(BF16), even_w_out[0].astype(BF16)
    w_in = odd_w_in[0]
    odd_in, odd_out = w_in[:, :3 * C_WIDTH].astype(BF16), odd_w_out[0].astype(BF16)
    w_gate = jnp.pad(w_in[:, 3 * C_WIDTH:], ((0, 0), (0, LANES - C_HEADS)))
    b_f = jnp.pad(odd_b_f[0], (0, LANES - C_HEADS)).reshape(1, LANES)
    peer = [dict(wq=peer_wq[l].astype(BF16),
                 sk=peer_subkeys[l].reshape(2 * PEER_HEADS, N_KEYS, PEER_KEY_HALF).astype(BF16),
                 up=pack_bf16_pairs(peer_u[l]), vp=pack_bf16_pairs(peer_v[l]))
            for l in range(2)]

    bg = batch // BATCH_GROUPS
    m = bg * seq
    groups = [dict(x=x[g * bg:(g + 1) * bg].reshape(m, d)) for g in range(BATCH_GROUPS)]

    def peer_select(st, layer):
        q, hp = norm_matmul(st["x"], norm_ffn_g[layer], peer[layer]["wq"], want_h="packed")
        idx, gate = peer_topk(q, peer[layer]["sk"])
        return dict(st=st, layer=layer, idx=idx, gate=gate,
                    act=peer_expert_dots(hp, idx, peer[layer]["up"]))

    def peer_combine(rec):
        rec["w"] = gelu_gate(rec["act"], rec["gate"])
        rec["st"]["peer"] = peer_expert_combine(rec["w"], rec["idx"], peer[rec["layer"]]["vp"])

    def even_layer(st):
        (p,) = norm_matmul(st["x"], norm_mix_g[0], even_in)
        branches = [dilated_branch(p, tile, dl, bg, seq)
                    for tile, (_, dl) in zip(dil_tiles, DILATED_BRANCHES)]
        ob = diff_attention(p.reshape(bg, seq, -1), diff_tiles, diff_lambda[0], diff_ln_g[0],
                            lam_init, bg, seq)
        st["x"] = even_out_proj(branches, ob.reshape(m, B_V_WIDTH), even_out, st["x"])

    def odd_layer(st):
        p, st["x"], h = norm_matmul(st["x"], norm_mix_g[1], odd_in, res=st["peer"], want_h="f32")
        fg = gate_matmul(h, w_gate)
        c = logsig_cumsum(fg.reshape(bg, seq, LANES), b_f, bg, seq)
        ct = c[:, :, :C_HEADS].transpose(0, 2, 1)
        o = fox_attention(p.reshape(bg, seq, -1), ct[:, :, :, None], ct[:, :, None, :], bg, seq)
        st["x"] = matmul_residual(o.reshape(m, C_WIDTH), odd_out, st["x"])

    stages = [(even_layer, st, 0) for st in groups] + [(odd_layer, st, 1) for st in groups]
    recs = []
    for k, (mixer, st, layer) in enumerate(stages):
        mixer(st)
        if k >= 1:
            prev = recs[k - 1]
            prev["act"], st["x"] = lax.optimization_barrier((prev["act"], st["x"]))
            peer_combine(prev)
        if k >= 2:
            done = recs[k - 2]["st"]
            if done is not st:
                done["peer"], st["x"] = lax.optimization_barrier((done["peer"], st["x"]))
        recs.append(peer_select(st, layer))
    peer_combine(recs[-1])
    outs = [add_norm(st["x"], st["peer"], final_norm_g).reshape(bg, seq, d) for st in groups]
    return jnp.concatenate(outs, axis=0)


def _gate_matmul_body(h_ref, w_ref, o_ref):
    o_ref[...] = jnp.dot(h_ref[...], w_ref[...], preferred_element_type=F32,
                         precision=lax.Precision.HIGHEST)


def gate_matmul(h, w, tm=512):
    m, d = h.shape
    n = w.shape[1]
    return pl.pallas_call(
        _gate_matmul_body,
        grid=(m // tm,),
        in_specs=[pl.BlockSpec((tm, d), lambda i: (i, 0)),
                  pl.BlockSpec((d, n), lambda i: (0, 0))],
        out_specs=pl.BlockSpec((tm, n), lambda i: (i, 0)),
        out_shape=jax.ShapeDtypeStruct((m, n), F32),
        compiler_params=pltpu.CompilerParams(dimension_semantics=("parallel",)),
        name="gate_matmul",
    )(h, w)
```

```python
import functools
import math

import numpy as np
import jax
import jax.numpy as jnp
from jax import lax
from jax.experimental import pallas as pl
from jax.experimental.pallas import tpu as pltpu
from jax.experimental.pallas import tpu_sc as plsc

F32 = jnp.float32
BF16 = jnp.bfloat16
I32 = jnp.int32

D_MODEL = 1024
HEAD_DIM = 64
A_HEADS = 8
DILATED_BRANCHES = ((128, 1), (512, 4), (2048, 16))
DIFF_HALF = 64
DIFF_VDIM = 128
B_HEADS = 4
C_HEADS = 16
N_BUCKETS = 32
MAX_DISTANCE = 2048
PEER_HEADS = 8
N_KEYS = 128
PEER_TOPK = 16
PEER_KEY_HALF = 128
RMS_EPS = 1e-6
NEG_INF = -1e30
A_WIDTH = A_HEADS * HEAD_DIM
B_QK_WIDTH = B_HEADS * 2 * DIFF_HALF
B_V_WIDTH = B_HEADS * DIFF_VDIM
C_WIDTH = C_HEADS * HEAD_DIM
QK_SCALE = 0.125
LANES = 128
DIL_BLOCK = 128
ATT_BLOCK = 512
BATCH_GROUPS = 4

SC_CORES = 2
SC_SUBCORES = 16
SC_LANES = 16
SC_WORKERS = SC_CORES * SC_SUBCORES


def _t5_bucket_table(n):
    max_exact = N_BUCKETS // 2
    d = np.arange(n)
    df = np.maximum(d, 1).astype(np.float32)
    large = max_exact + (
        np.log(df / np.float32(max_exact)) / np.float32(math.log(MAX_DISTANCE / max_exact))
        * np.float32(N_BUCKETS - max_exact)).astype(np.int32)
    large = np.minimum(large, N_BUCKETS - 1)
    return np.where(d < max_exact, d, large).astype(np.int32)


def _norm_matmul_body(*refs, has_res, want_h):
    it = iter(refs)
    x_ref = next(it)
    r_ref = next(it) if has_res else None
    g_ref = next(it)
    w_ref = next(it)
    o_ref = next(it)
    xs_ref = next(it) if has_res else None
    hout_ref = next(it) if want_h else None
    h_scr = next(it)

    @pl.when(pl.program_id(1) == 0)
    def _():
        x = x_ref[...]
        if has_res:
            x = x + r_ref[...]
            xs_ref[...] = x
        ms = jnp.mean(x * x, axis=-1, keepdims=True)
        h = x * lax.rsqrt(ms + RMS_EPS) * g_ref[...]
        if want_h == "f32":
            hout_ref[...] = h
        elif want_h == "packed":
            hout_ref[...] = pack_bf16_pairs(h)
        h_scr[...] = h.astype(BF16)

    o_ref[...] = jnp.dot(h_scr[...], w_ref[...],
                         preferred_element_type=F32).astype(o_ref.dtype)


def norm_matmul(x, g, w, *, res=None, want_h=None, out_dtype=BF16, tm=512, tn=512):
    m, d = x.shape
    n = w.shape[1]
    tn = min(tn, n)
    row = pl.BlockSpec((tm, d), lambda i, j: (i, 0))
    in_specs = [row] + ([row] if res is not None else []) + [
        pl.BlockSpec((1, d), lambda i, j: (0, 0)),
        pl.BlockSpec((d, tn), lambda i, j: (0, j))]
    out_specs = [pl.BlockSpec((tm, tn), lambda i, j: (i, j))]
    out_shape = [jax.ShapeDtypeStruct((m, n), out_dtype)]
    if res is not None:
        out_specs.append(row)
        out_shape.append(jax.ShapeDtypeStruct((m, d), F32))
    if want_h == "f32":
        out_specs.append(row)
        out_shape.append(jax.ShapeDtypeStruct((m, d), F32))
    elif want_h == "packed":
        out_specs.append(pl.BlockSpec((tm, d // 2), lambda i, j: (i, 0)))
        out_shape.append(jax.ShapeDtypeStruct((m, d // 2), I32))
    args =[x] + ([res] if res is not None else []) + [g.reshape(1, d), w]
    return pl.pallas_call(
        functools.partial(_norm_matmul_body, has_res=res is not None, want_h=want_h),
        grid=(m // tm, n // tn),
        in_specs=in_specs,
        out_specs=out_specs,
        out_shape=out_shape,
        scratch_shapes=[pltpu.VMEM((tm, d), BF16)],
        compiler_params=pltpu.CompilerParams(
            dimension_semantics=("parallel", "arbitrary")),
        name="norm_matmul",
    )(*args)


def _dilated_body(q_ref, kp_ref, kc_ref, vp_ref, vc_ref, b_ref, o_ref, lse_ref):
    i = pl.program_id(3)
    q = q_ref[...] * QK_SCALE
    k = jnp.concatenate([kp_ref[...], kc_ref[...]], axis=0)
    v = jnp.concatenate([vp_ref[...], vc_ref[...]], axis=0)
    col = lax.broadcasted_iota(I32, (DIL_BLOCK, 2 * DIL_BLOCK), 1)
    has_prev = jnp.logical_or(col >= DIL_BLOCK, i > 0)
    outs, lses = [], []
    for hh in range(2):
        sl = slice(hh * HEAD_DIM, (hh + 1) * HEAD_DIM)
        s = lax.dot_general(q[:, sl], k[:, sl], (((1,), (1,)), ((), ())),
                            preferred_element_type=F32)
        s = jnp.where(has_prev, s + b_ref[hh], NEG_INF)
        m = jnp.max(s, axis=-1, keepdims=True)
        p = jnp.exp(s - m)
        l = jnp.sum(p, axis=-1, keepdims=True)
        o = jnp.dot(p.astype(BF16), v[:, sl], preferred_element_type=F32) / l
        outs.append(o)
        lses.append(jnp.broadcast_to(m + jnp.log(l), (DIL_BLOCK, HEAD_DIM)))
    o_ref[...] = jnp.concatenate(outs, axis=1)
    lse_ref[...] = jnp.concatenate(lses, axis=1)


def dilated_branch(p, bias_tile, dil, batch, seq):
    n_cols = p.shape[1]
    cb = n_cols // LANES
    rows = seq // dil
    nblk = rows // DIL_BLOCK
    pv = p.reshape(batch, rows, dil * n_cols)
    kq, kk, kv = 0, A_WIDTH // LANES, 2 * A_WIDTH // LANES
    blk = (None, DIL_BLOCK, LANES)

    def spec(col0, prev):
        if prev:
            return pl.BlockSpec(blk, lambda b, h, r, i: (b, jnp.maximum(i - 1, 0), r * cb + col0 + h))
        return pl.BlockSpec(blk, lambda b, h, r, i: (b, i, r * cb + col0 + h))

    ocb = A_WIDTH // LANES
    ospec = pl.BlockSpec(blk, lambda b, h, r, i: (b, i, r * ocb + h))
    oshape = jax.ShapeDtypeStruct((batch, rows, dil * A_WIDTH), F32)
    o, lse = pl.pallas_call(
        _dilated_body,
        grid=(batch, A_HEADS // 2, dil, nblk),
        in_specs=[spec(kq, False), spec(kk, True), spec(kk, False),
                  spec(kv, True), spec(kv, False),
                  pl.BlockSpec((2, DIL_BLOCK, 2 * DIL_BLOCK), lambda b, h, r, i: (h, 0, 0))],
        out_specs=[ospec, ospec],
        out_shape=[oshape, oshape],
        compiler_params=pltpu.CompilerParams(
            dimension_semantics=("parallel", "parallel", "parallel", "arbitrary")),
        name=f"dilated_d{dil}",
    )(pv, pv, pv, pv, pv, bias_tile)
    return o.reshape(batch * seq, A_WIDTH), lse.reshape(batch * seq, A_WIDTH)


def dilated_bias_tile(rel_bias_a, window, dil):
    n = window // dil
    assert n == DIL_BLOCK
    bucket = _t5_bucket_table(window + 1)
    row = np.arange(DIL_BLOCK)[:, None]
    c = np.arange(2 * DIL_BLOCK)[None, :]
    j = DIL_BLOCK + row - c
    valid = (j >= 0) & (j <= n)
    bk = bucket[np.clip(j, 0, n) * dil]
    tile = rel_bias_a.T[:, bk]
    return jnp.where(jnp.asarray(valid)[None], tile, NEG_INF).astype(F32)


def _diff_body(q_ref, k_ref, v_ref, b_ref, lam_ref, g_ref, o_ref,
               m_scr, l_scr, acc_scr, *, n_tiles, lam_init):
    t = ATT_BLOCK
    qi = pl.program_id(2)
    q = q_ref[...] * QK_SCALE
    lane = lax.broadcasted_iota(I32, (1, LANES), 1)
    qa = [jnp.where((lane < DIFF_HALF) == (a == 0), q, jnp.zeros_like(q)) for a in range(2)]
    ones = jnp.ones((t, LANES), BF16)
    for a in range(2):
        m_scr[a] = jnp.full((t, LANES), NEG_INF, F32)
        l_scr[a] = jnp.zeros((t, LANES), F32)
        acc_scr[a] = jnp.zeros((t, DIFF_VDIM), F32)

    def step(j, masked):
        off = pl.multiple_of(j * t, t)
        ks = k_ref[pl.ds(off, t), :]
        vs = v_ref[pl.ds(off, t), :]
        bias = b_ref[jnp.minimum(qi - j, n_tiles - 1)]
        if masked:
            row = lax.broadcasted_iota(I32, (t, t), 0)
            col = lax.broadcasted_iota(I32, (t, t), 1)
            causal = row >= col
        for a in range(2):
            s = lax.dot_general(qa[a], ks, (((1,), (1,)), ((), ())),
                                preferred_element_type=F32) + bias
            if masked:
                s = jnp.where(causal, s, NEG_INF)
            m_prev = m_scr[a]
            m_new = jnp.maximum(m_prev, jnp.max(s, axis=-1, keepdims=True))
            alpha = jnp.exp(m_prev - m_new)
            pb = jnp.exp(s - jnp.concatenate([m_new] * (t // LANES), axis=1)).astype(BF16)
            l_scr[a] = alpha * l_scr[a] + jnp.dot(pb, ones, preferred_element_type=F32)
            acc_scr[a] = alpha * acc_scr[a] + jnp.dot(pb, vs, preferred_element_type=F32)
            m_scr[a] = m_new

    def loop_body(j, carry):
        step(j, False)
        return carry

    lax.fori_loop(0, qi, loop_body, 0)
    step(qi, True)

    lp = lam_ref[...]
    lam = (jnp.exp(jnp.sum(lp[0:1] * lp[1:2])) - jnp.exp(jnp.sum(lp[2:3] * lp[3:4]))
           + lam_init)
    o = acc_scr[0] / l_scr[0] - lam * (acc_scr[1] / l_scr[1])
    ms = jnp.mean(o * o, axis=-1, keepdims=True)
    y = o * lax.rsqrt(ms + RMS_EPS) * g_ref[...]
    o_ref[...] = (y * (1.0 - lam_init)).astype(o_ref.dtype)


def diff_attention(p, bias_tiles, lam_params, ln_g, lam_init, batch, seq):
    t = ATT_BLOCK
    n_tiles = bias_tiles.shape[1]
    cq = 3 * A_WIDTH // LANES
    ck = cq + B_QK_WIDTH // LANES
    cv = ck + B_QK_WIDTH // LANES
    return pl.pallas_call(
        functools.partial(_diff_body, n_tiles=n_tiles, lam_init=lam_init),
        grid=(batch, B_HEADS, seq // t),
        in_specs=[
            pl.BlockSpec((None, t, LANES), lambda b, h, i: (b, i, cq + h)),
            pl.BlockSpec((None, seq, LANES), lambda b, h, i: (b, 0, ck + h)),
            pl.BlockSpec((None, seq, LANES), lambda b, h, i: (b, 0, cv + h)),
            pl.BlockSpec((None, n_tiles, t, t), lambda b, h, i: (h, 0, 0, 0)),
            pl.BlockSpec((4, DIFF_HALF), lambda b, h, i: (0, 0)),
            pl.BlockSpec((1, DIFF_VDIM), lambda b, h, i: (0, 0)),
        ],
        out_specs=pl.BlockSpec((None, t, LANES), lambda b, h, i: (b, i, h)),
        out_shape=jax.ShapeDtypeStruct((batch, seq, B_V_WIDTH), BF16),
        scratch_shapes=[pltpu.VMEM((2, t, LANES), F32), pltpu.VMEM((2, t, LANES), F32),
                        pltpu.VMEM((2, t, DIFF_VDIM), F32)],
        compiler_params=pltpu.CompilerParams(
            dimension_semantics=("parallel", "parallel", "arbitrary")),
        name="diff_attention",
    )(p, p, p, bias_tiles, lam_params, ln_g.reshape(1, DIFF_VDIM))


def diff_bias_tiles(rel_bias_b, seq):
    t = ATT_BLOCK
    bucket = _t5_bucket_table(max(seq, 2 * MAX_DISTANCE) + 2 * t)
    sat = bucket[-1]
    d_sat = int(np.max(np.nonzero(bucket != sat)[0])) + 1
    n_full = (d_sat + t - 1 + t - 1) // t
    n_tiles = n_full + 1
    assert n_full * t - (t - 1) >= d_sat
    n = np.arange(2 * t)[None, :]
    base = np.arange(n_tiles)[:, None] * t
    dist = np.clip(np.where(n < t, base - n, base + 2 * t - n), 0, None)
    w = rel_bias_b.T[:, bucket[dist]].astype(F32)
    rep = jnp.broadcast_to(w[:, :, None, :], (B_HEADS, n_tiles, t, 2 * t))
    flat = rep.reshape(B_HEADS, n_tiles, 2 * t * t)[:, :, :t * (2 * t - 1)]
    return flat.reshape(B_HEADS, n_tiles, t, 2 * t - 1)[:, :, :, :t]


def _logsig_cumsum_body(f_ref, b_ref, c_ref, carry_scr):
    t = f_ref.shape[0]

    @pl.when(pl.program_id(1) == 0)
    def _():
        carry_scr[...] = jnp.zeros_like(carry_scr)

    x = f_ref[...] + b_ref[...]
    ls = jnp.minimum(x, 0.0) - jnp.log1p(jnp.exp(-jnp.abs(x)))
    row = lax.broadcasted_iota(I32, (t, t), 0)
    col = lax.broadcasted_iota(I32, (t, t), 1)
    tri = (row >= col).astype(F32)
    c = jnp.dot(tri, ls, preferred_element_type=F32,
                precision=lax.Precision.HIGHEST) + carry_scr[...]
    c_ref[...] = c
    carry_scr[...] = c[t - 1:t, :]


def logsig_cumsum(fg, b_f, batch, seq, t=512):
    return pl.pallas_call(
        _logsig_cumsum_body,
        grid=(batch, seq // t),
        in_specs=[pl.BlockSpec((None, t, LANES), lambda b, i: (b, i, 0)),
                  pl.BlockSpec((1, LANES), lambda b, i: (0, 0))],
        out_specs=pl.BlockSpec((None, t, LANES), lambda b, i: (b, i, 0)),
        out_shape=jax.ShapeDtypeStruct((batch, seq, LANES), F32),
        scratch_shapes=[pltpu.VMEM((1, LANES), F32)],
        compiler_params=pltpu.CompilerParams(
            dimension_semantics=("parallel", "arbitrary")),
        name="logsig_cumsum",
    )(fg, b_f)


def _fox_body(q_ref, k_ref, v_ref, cq_ref, ck_ref, o_ref, m_scr, acc_scr):
    t = ATT_BLOCK
    qi = pl.program_id(2)
    q = q_ref[...] * QK_SCALE
    lane = lax.broadcasted_iota(I32, (1, LANES), 1)
    own = [(lane < HEAD_DIM) == (a == 0) for a in range(2)]
    qa = [jnp.where(own[a], q, jnp.zeros_like(q)) for a in range(2)]
    cqb = [jnp.broadcast_to(cq_ref[a], (t, LANES)) for a in range(2)]
    for a in range(2):
        m_scr[a] = jnp.full((t, LANES), NEG_INF, F32)
        acc_scr[a] = jnp.zeros((t, LANES), F32)

    def step(j, masked):
        off = pl.multiple_of(j * t, t)
        ks = k_ref[pl.ds(off, t), :]
        vs = v_ref[pl.ds(off, t), :]
        if masked:
            row = lax.broadcasted_iota(I32, (t, t), 0)
            col = lax.broadcasted_iota(I32, (t, t), 1)
            causal = row >= col
        for a in range(2):
            s = lax.dot_general(qa[a], ks, (((1,), (1,)), ((), ())),
                                preferred_element_type=F32)
            s = s - ck_ref[a, :, pl.ds(off, t)]
            if masked:
                s = jnp.where(causal, s, NEG_INF)
            m_prev = m_scr[a]
            m_new = jnp.maximum(m_prev, jnp.max(s, axis=-1, keepdims=True) + cqb[a])
            alpha = jnp.exp(m_prev - m_new)
            shift = m_new - cqb[a]
            pb = jnp.exp(s - jnp.concatenate([shift] * (t // LANES), axis=1)).astype(BF16)
            v_aug = jnp.where(own[a], vs, jnp.ones_like(vs))
            acc_scr[a] = alpha * acc_scr[a] + jnp.dot(pb, v_aug, preferred_element_type=F32)
            m_scr[a] = m_new

    def loop_body(j, carry):
        step(j, False)
        return carry

    lax.fori_loop(0, qi, loop_body, 0)
    step(qi, True)
    r = [acc_scr[a] / pltpu.roll(acc_scr[a], HEAD_DIM, 1) for a in range(2)]
    o_ref[...] = jnp.where(own[0], r[0], r[1]).astype(o_ref.dtype)


def fox_attention(p, cq, ck, batch, seq):
    t = ATT_BLOCK
    nk = C_WIDTH // LANES
    return pl.pallas_call(
        _fox_body,
        grid=(batch, C_HEADS // 2, seq // t),
        in_specs=[
            pl.BlockSpec((None, t, LANES), lambda b, h, i: (b, i, h)),
            pl.BlockSpec((None, seq, LANES), lambda b, h, i: (b, 0, nk + h)),
            pl.BlockSpec((None, seq, LANES), lambda b, h, i: (b, 0, 2 * nk + h)),
            pl.BlockSpec((None, 2, t, 1), lambda b, h, i: (b, h, i, 0)),
            pl.BlockSpec((None, 2, 1, seq), lambda b, h, i: (b, h, 0, 0)),
        ],
        out_specs=pl.BlockSpec((None, t, LANES), lambda b, h, i: (b, i, h)),
        out_shape=jax.ShapeDtypeStruct((batch, seq, C_WIDTH), BF16),
        scratch_shapes=[pltpu.VMEM((2, t, LANES), F32), pltpu.VMEM((2, t, LANES), F32)],
        compiler_params=pltpu.CompilerParams(
            dimension_semantics=("parallel", "parallel", "arbitrary")),
        name="fox_attention",
    )(p, p, p, cq, ck)


def _even_out_body(o1, l1, o2, l2, o3, l3, ob_ref, w_ref, x_ref, out_ref, a_scr):
    @pl.when(pl.program_id(1) == 0)
    def _():
        a1, a2, a3 = l1[...], l2[...], l3[...]
        mx = jnp.maximum(jnp.maximum(a1, a2), a3)
        e1, e2, e3 = jnp.exp(a1 - mx), jnp.exp(a2 - mx), jnp.exp(a3 - mx)
        oa = (e1 * o1[...] + e2 * o2[...] + e3 * o3[...]) / (e1 + e2 + e3)
        a_scr[:, :A_WIDTH] = oa.astype(BF16)
        a_scr[:, A_WIDTH:] = ob_ref[...]

    out_ref[...] = x_ref[...] + jnp.dot(a_scr[...], w_ref[...], preferred_element_type=F32)


def even_out_proj(branches, ob, w, x, tm=512, tn=512):
    m, d = x.shape
    half = pl.BlockSpec((tm, A_WIDTH), lambda i, j: (i, 0))
    flat = [a for pair in branches for a in pair]
    return pl.pallas_call(
        _even_out_body,
        grid=(m // tm, d // tn),
        in_specs=[half] * 6 + [
            pl.BlockSpec((tm, B_V_WIDTH), lambda i, j: (i, 0)),
            pl.BlockSpec((A_WIDTH + B_V_WIDTH, tn), lambda i, j: (0, j)),
            pl.BlockSpec((tm, tn), lambda i, j: (i, j))],
        out_specs=pl.BlockSpec((tm, tn), lambda i, j: (i, j)),
        out_shape=jax.ShapeDtypeStruct((m, d), F32),
        scratch_shapes=[pltpu.VMEM((tm, A_WIDTH + B_V_WIDTH), BF16)],
        compiler_params=pltpu.CompilerParams(
            dimension_semantics=("parallel", "arbitrary")),
        name="even_out_proj",
    )(*flat, ob, w, x)


def _matmul_res_body(a_ref, w_ref, x_ref, o_ref):
    o_ref[...] = x_ref[...] + jnp.dot(a_ref[...], w_ref[...], preferred_element_type=F32)


def matmul_residual(a, w, x, tm=512, tn=512):
    m, k = a.shape
    n = w.shape[1]
    return pl.pallas_call(
        _matmul_res_body,
        grid=(m // tm, n // tn),
        in_specs=[pl.BlockSpec((tm, k), lambda i, j: (i, 0)),
                  pl.BlockSpec((k, tn), lambda i, j: (0, j)),
                  pl.BlockSpec((tm, tn), lambda i, j: (i, j))],
        out_specs=pl.BlockSpec((tm, tn), lambda i, j: (i, j)),
        out_shape=jax.ShapeDtypeStruct((m, n), F32),
        compiler_params=pltpu.CompilerParams(
            dimension_semantics=("parallel", "parallel")),
        name="matmul_residual",
    )(a, w, x)


def _add_norm_body(x_ref, r_ref, g_ref, o_ref):
    x = x_ref[...] + r_ref[...]
    ms = jnp.mean(x * x, axis=-1, keepdims=True)
    o_ref[...] = x * lax.rsqrt(ms + RMS_EPS) * g_ref[...]


def add_norm(x, r, g, tm=512):
    m, d = x.shape
    row = pl.BlockSpec((tm, d), lambda i: (i, 0))
    return pl.pallas_call(
        _add_norm_body,
        grid=(m // tm,),
        in_specs=[row, row, pl.BlockSpec((1, d), lambda i: (0, 0))],
        out_specs=row,
        out_shape=jax.ShapeDtypeStruct((m, d), F32),
        compiler_params=pltpu.CompilerParams(dimension_semantics=("parallel",)),
        name="add_norm",
    )(x, r, g.reshape(1, d))


PEER_CAND_ROWS = PEER_TOPK + 8 * (PEER_TOPK - 1)


def _peer_topk_body(q_ref, sk_ref, idx_ref, gate_ref, ts_scr, ti_scr, bs_scr, be_scr,
                    gt_scr, it_scr):
    tm = q_ref.shape[0]
    neg_inf = jnp.float32(-jnp.inf)
    key_id = lax.broadcasted_iota(I32, (N_KEYS, tm), 0)

    def pair_body(pr, carry):
        off = pl.multiple_of(pr * PEER_KEY_HALF, PEER_KEY_HALF)
        sc = lax.dot_general(sk_ref[pr], q_ref[:, pl.ds(off, PEER_KEY_HALF)],
                             (((1,), (1,)), ((), ())), preferred_element_type=F32)

        def k_body(k, vals):
            m = jnp.max(vals, axis=0, keepdims=True)
            sel = jnp.min(jnp.where(vals == m, key_id, N_KEYS), axis=0, keepdims=True)
            ts_scr[pr, pl.ds(k, 1), :] = m
            ti_scr[pr, pl.ds(k, 1), :] = sel
            return jnp.where(key_id == sel, neg_inf, vals)

        lax.fori_loop(0, PEER_TOPK, k_body, sc)
        return carry

    lax.fori_loop(0, 2 * PEER_HEADS, pair_body, 0)

    r = lax.broadcasted_iota(I32, (PEER_CAND_ROWS, 1), 0)
    cand_id = jnp.where(r < PEER_TOPK, r,
                        (1 + (r - PEER_TOPK) // 8) * PEER_TOPK + (r - PEER_TOPK) % 8)

    def head_body(h, carry):
        s1, s2 = ts_scr[2 * h], ts_scr[2 * h + 1]
        i1, i2 = ti_scr[2 * h] * N_KEYS, ti_scr[2 * h + 1]
        vals = jnp.concatenate(
            [s1[0:1] + s2] + [s1[a:a + 1] + s2[0:8] for a in range(1, PEER_TOPK)], axis=0)
        eidx = jnp.concatenate(
            [i1[0:1] + i2] + [i1[a:a + 1] + i2[0:8] for a in range(1, PEER_TOPK)], axis=0)

        def k_body(k, vals):
            m = jnp.max(vals, axis=0, keepdims=True)
            sel = jnp.min(jnp.where(vals == m, cand_id, PEER_TOPK * PEER_TOPK),
                          axis=0, keepdims=True)
            hit = cand_id == sel
            bs_scr[pl.ds(k, 1), :] = m
            be_scr[pl.ds(k, 1), :] = jnp.sum(jnp.where(hit, eidx, 0), axis=0, keepdims=True)
            return jnp.where(hit, neg_inf, vals)

        lax.fori_loop(0, PEER_TOPK, k_body, vals)
        bs = bs_scr[...]
        e = jnp.exp(bs - jnp.max(bs, axis=0, keepdims=True))
        row0 = pl.multiple_of(h * PEER_TOPK, PEER_TOPK)
        gt_scr[pl.ds(row0, PEER_TOPK), :] = e / jnp.sum(e, axis=0, keepdims=True)
        it_scr[pl.ds(row0, PEER_TOPK), :] = be_scr[...]
        return carry

    lax.fori_loop(0, PEER_HEADS, head_body, 0)
    gate_ref[...] = gt_scr[...].T
    idx_ref[...] = it_scr[...].T


def peer_topk(q, subkeys, tm=256):
    m = q.shape[0]
    n_sel = PEER_HEADS * PEER_TOPK
    out_spec = pl.BlockSpec((tm, n_sel), lambda i: (i, 0))
    return pl.pallas_call(
        _peer_topk_body,
        grid=(m // tm,),
        in_specs=[pl.BlockSpec((tm, q.shape[1]), lambda i: (i, 0)),
                  pl.BlockSpec(subkeys.shape, lambda i: (0, 0, 0))],
        out_specs=[out_spec, out_spec],
        out_shape=[jax.ShapeDtypeStruct((m, n_sel), I32),
                   jax.ShapeDtypeStruct((m, n_sel), F32)],
        scratch_shapes=[pltpu.VMEM((2 * PEER_HEADS, PEER_TOPK, tm), F32),
                        pltpu.VMEM((2 * PEER_HEADS, PEER_TOPK, tm), I32),
                        pltpu.VMEM((PEER_TOPK, tm), F32),
                        pltpu.VMEM((PEER_TOPK, tm), I32),
                        pltpu.VMEM((n_sel, tm), F32),
                        pltpu.VMEM((n_sel, tm), I32)],
        compiler_params=pltpu.CompilerParams(dimension_semantics=("parallel",)),
        name="peer_topk",
    )(q, subkeys)


def _gelu_gate_body(a_ref, g_ref, o_ref):
    a = a_ref[...]
    o_ref[...] = g_ref[...] * (0.5 * a * (1.0 + lax.erf(a * (2.0 ** -0.5))))


def gelu_gate(act, gate, tm=2048):
    m, n = act.shape
    spec = pl.BlockSpec((tm, n), lambda i: (i, 0))
    return pl.pallas_call(
        _gelu_gate_body,
        grid=(m // tm,),
        in_specs=[spec, spec],
        out_specs=spec,
        out_shape=jax.ShapeDtypeStruct((m, n), F32),
        compiler_params=pltpu.CompilerParams(dimension_semantics=("parallel",)),
        name="gelu_gate",
    )(act, gate)


SC_TOK_CHUNK = 32
SC_RING = 8
SC_BF16_GROUP = 4
SC_FMT = plsc.PackFormat.INTERLEAVED


def _sc_worker_id():
    return lax.axis_index("s") * SC_CORES + lax.axis_index("c")


def pack_bf16_pairs(t):
    half = t.shape[-1] // 2
    bits = lax.bitcast_convert_type(t.astype(BF16).astype(F32), I32)
    return (bits[..., half:] & jnp.int32(-65536)) | lax.shift_right_logical(
        bits[..., :half], jnp.int32(16))


def _sc_row_pipeline(idx_v, table_hbm, rows_v, sems, n_items, groups, compute):
    def gather(item):
        tt, g = item // groups, item % groups
        ids = idx_v[tt, pl.ds(g * SC_LANES, SC_LANES)]
        slot = item % SC_RING
        return pltpu.make_async_copy(table_hbm.at[ids], rows_v.at[slot], sems.at[slot])

    for s in range(SC_RING - 1):
        gather(s).start()

    def item_body(item, carry):
        nxt = item + SC_RING - 1

        @pl.when(nxt < n_items)
        def _():
            gather(nxt).start()

        gather(item).wait()
        compute(item // groups, item % groups, item % SC_RING)
        return carry

    lax.fori_loop(0, n_items, item_body, 0)


def peer_expert_dots(hp, idx, up):
    m, dw = hp.shape
    n_sel = idx.shape[1]
    per_w = m // SC_WORKERS
    n_chunks = per_w // SC_TOK_CHUNK
    groups = n_sel // SC_LANES
    step = SC_BF16_GROUP * SC_LANES
    mesh = plsc.VectorSubcoreMesh(core_axis_name="c", subcore_axis_name="s")

    @functools.partial(
        pl.kernel, mesh=mesh,
        out_type=jax.ShapeDtypeStruct((m, n_sel), F32),
        scratch_types=[
            pltpu.VMEM((SC_TOK_CHUNK, n_sel), I32),
            pltpu.VMEM((SC_TOK_CHUNK, dw), I32),
            pltpu.VMEM((SC_TOK_CHUNK, n_sel), F32),
            pltpu.VMEM((SC_RING, SC_LANES, dw), I32),
            pltpu.VMEM((SC_LANES * SC_LANES,), F32),
            pltpu.SemaphoreType.DMA((SC_RING,)),
        ],
        compiler_params=pltpu.CompilerParams(needs_layout_passes=False),
        name="peer_expert_dots",
    )
    def k(h_hbm, idx_hbm, u_hbm, act_hbm, idx_v, h_v, act_v, rows_v, part_v, sems):
        base = _sc_worker_id() * per_w
        lane = lax.broadcasted_iota(I32, (SC_LANES,), 0)

        def compute(tt, g, slot):
            def grp_body(q, accs):
                off = pl.multiple_of(q * step, step)
                xs = [plsc.bitcast(h_v[tt, pl.ds(off + c * SC_LANES, SC_LANES)], BF16)
                      for c in range(SC_BF16_GROUP)]
                new = []
                for e in range(SC_LANES):
                    s = None
                    for c in range(SC_BF16_GROUP):
                        p = plsc.bitcast(
                            rows_v[slot, e, pl.ds(off + c * SC_LANES, SC_LANES)], BF16) * xs[c]
                        s = p if s is None else s + p
                    lo, hi = plsc.unpack(s, format=SC_FMT)
                    new.append(accs[e] + (lo + hi))
                return tuple(new)

            accs = lax.fori_loop(
                0, dw // step, grp_body,
                tuple(jnp.zeros((SC_LANES,), F32) for _ in range(SC_LANES)))
            for e in range(SC_LANES):
                part_v[pl.ds(e * SC_LANES, SC_LANES)] = accs[e]
            tot = jnp.zeros((SC_LANES,), F32)
            for l in range(SC_LANES):
                tot = tot + plsc.load_gather(part_v, [lane * SC_LANES + l])
            act_v[tt, pl.ds(g * SC_LANES, SC_LANES)] = tot

        def chunk_body(c, carry):
            t0 = base + c * SC_TOK_CHUNK
            pltpu.sync_copy(idx_hbm.at[pl.ds(t0, SC_TOK_CHUNK)], idx_v)
            pltpu.sync_copy(h_hbm.at[pl.ds(t0, SC_TOK_CHUNK)], h_v)
            _sc_row_pipeline(idx_v, u_hbm, rows_v, sems, SC_TOK_CHUNK * groups, groups, compute)
            pltpu.sync_copy(act_v, act_hbm.at[pl.ds(t0, SC_TOK_CHUNK)])
            return carry

        lax.fori_loop(0, n_chunks, chunk_body, 0)

    return k(hp, idx, up)


def peer_expert_combine(w, idx, vp):
    m, n_sel = w.shape
    dw = vp.shape[1]
    d = 2 * dw
    per_w = m // SC_WORKERS
    n_chunks = per_w // SC_TOK_CHUNK
    n_vec = d // SC_LANES
    groups = n_sel // SC_LANES
    mesh = plsc.VectorSubcoreMesh(core_axis_name="c", subcore_axis_name="s")

    @functools.partial(
        pl.kernel, mesh=mesh,
        out_type=jax.ShapeDtypeStruct((m, d), F32),
        scratch_types=[
            pltpu.VMEM((SC_TOK_CHUNK, n_sel), I32),
            pltpu.VMEM((SC_TOK_CHUNK, n_sel), F32),
            pltpu.VMEM((SC_TOK_CHUNK, d), F32),
            pltpu.VMEM((SC_RING, SC_LANES, dw), I32),
            pltpu.SemaphoreType.DMA((SC_RING,)),
        ],
        compiler_params=pltpu.CompilerParams(needs_layout_passes=False),
        name="peer_expert_combine",
    )
    def k(w_hbm, idx_hbm, v_hbm, out_hbm, idx_v, w_v, out_v, rows_v, sems):
        base = _sc_worker_id() * per_w

        def compute(tt, g, slot):
            splat = []
            for e in range(SC_LANES):
                s = plsc.load_gather(w_v, [jnp.full((SC_LANES,), tt, I32),
                                           jnp.full((SC_LANES,), g * SC_LANES + e, I32)])
                splat.append(plsc.pack(s, s, format=SC_FMT))

            @plsc.parallel_loop(0, dw // SC_LANES)
            def _(j):
                off = pl.multiple_of(j * SC_LANES, SC_LANES)
                acc_lo = out_v[tt, pl.ds(off, SC_LANES)]
                acc_hi = out_v[tt, pl.ds(dw + off, SC_LANES)]
                for e0 in range(0, SC_LANES, SC_BF16_GROUP):
                    s = None
                    for e in range(e0, e0 + SC_BF16_GROUP):
                        p = plsc.bitcast(rows_v[slot, e, pl.ds(off, SC_LANES)], BF16) * splat[e]
                        s = p if s is None else s + p
                    lo, hi = plsc.unpack(s, format=SC_FMT)
                    acc_lo = acc_lo + lo
                    acc_hi = acc_hi + hi
                out_v[tt, pl.ds(off, SC_LANES)] = acc_lo
                out_v[tt, pl.ds(dw + off, SC_LANES)] = acc_hi

        def chunk_body(c, carry):
            t0 = base + c * SC_TOK_CHUNK
            pltpu.sync_copy(idx_hbm.at[pl.ds(t0, SC_TOK_CHUNK)], idx_v)
            pltpu.sync_copy(w_hbm.at[pl.ds(t0, SC_TOK_CHUNK)], w_v)

            def zero_body(z, carry2):
                tt, j = z // n_vec, z % n_vec
                out_v[tt, pl.ds(pl.multiple_of(j * SC_LANES, SC_LANES), SC_LANES)] = (
                    jnp.zeros((SC_LANES,), F32))
                return carry2

            lax.fori_loop(0, SC_TOK_CHUNK * n_vec, zero_body, 0)
            _sc_row_pipeline(idx_v, v_hbm, rows_v, sems, SC_TOK_CHUNK * groups, groups, compute)
            pltpu.sync_copy(out_v, out_hbm.at[pl.ds(t0, SC_TOK_CHUNK)])
            return carry

        lax.fori_loop(0, n_chunks, chunk_body, 0)

    return k(w, idx, vp)


def kernel(x, norm_mix_g, norm_ffn_g, final_norm_g, rel_bias, even_w_in, even_w_out,
           diff_lambda, diff_ln_g, odd_w_in, odd_b_f, odd_w_out, peer_wq, peer_subkeys,
           peer_u, peer_v):
    batch, seq, d = x.shape

    dil_tiles = [dilated_bias_tile(rel_bias[:, :A_HEADS], w, dl) for w, dl in DILATED_BRANCHES]
    diff_tiles = diff_bias_tiles(rel_bias[:, A_HEADS:], seq)
    lam_init = 0.8 - 0.6 * math.exp(-0.3 * 0)
    even_in, even_out = even_w_in[0].astype(BF16), even_w_out[0].astype(BF16)
    w_in = odd_w_in[0]
    odd_in, odd_out = w_in[:, :3 * C_WIDTH].astype(BF16), odd_w_out[0].astype(BF16)
    w_gate = jnp.pad(w_in[:, 3 * C_WIDTH:], ((0, 0), (0, LANES - C_HEADS)))
    b_f = jnp.pad(odd_b_f[0], (0, LANES - C_HEADS)).reshape(1, LANES)
    peer = [dict(wq=peer_wq[l].astype(BF16),
                 sk=peer_subkeys[l].reshape(2 * PEER_HEADS, N_KEYS, PEER_KEY_HALF).astype(BF16),
                 up=pack_bf16_pairs(peer_u[l]), vp=pack_bf16_pairs(peer_v[l]))
            for l in range(2)]

    bg = batch // BATCH_GROUPS
    m = bg * seq
    groups = [dict(x=x[g * bg:(g + 1) * bg].reshape(m, d)) for g in range(BATCH_GROUPS)]

    def peer_select(st, layer):
        q, hp = norm_matmul(st["x"], norm_ffn_g[layer], peer[layer]["wq"], want_h="packed")
        idx, gate = peer_topk(q, peer[layer]["sk"])
        return dict(st=st, layer=layer, idx=idx, gate=gate,
                    act=peer_expert_dots(hp, idx, peer[layer]["up"]))

    def peer_combine(rec):
        rec["w"] = gelu_gate(rec["act"], rec["gate"])
        rec["st"]["peer"] = peer_expert_combine(rec["w"], rec["idx"], peer[rec["layer"]]["vp"])

    def even_layer(st):
        (p,) = norm_matmul(st["x"], norm_mix_g[0], even_in)
        branches = [dilated_branch(p, tile, dl, bg, seq)
                    for tile, (_, dl) in zip(dil_tiles, DILATED_BRANCHES)]
        ob = diff_attention(p.reshape(bg, seq, -1), diff_tiles, diff_lambda[0], diff_ln_g[0],
                            lam_init, bg, seq)
        st["x"] = even_out_proj(branches, ob.reshape(m, B_V_WIDTH), even_out, st["x"])

    def odd_layer(st):
        p, st["x"], h = norm_matmul(st["x"], norm_mix_g[1], odd_in, res=st["peer"], want_h="f32")
        fg = gate_matmul(h, w_gate)
        c = logsig_cumsum(fg.reshape(bg, seq, LANES), b_f, bg, seq)
        ct = c[:, :, :C_HEADS].transpose(0, 2, 1)
        o = fox_attention(p.reshape(bg, seq, -1), ct[:, :, :, None], ct[:, :, None, :], bg, seq)
        st["x"] = matmul_residual(o.reshape(m, C_WIDTH), odd_out, st["x"])

    stages = [(even_layer, st, 0) for st in groups] + [(odd_layer, st, 1) for st in groups]
    recs = []
    for k, (mixer, st, layer) in enumerate(stages):
        mixer(st)
        if k >= 1:
            prev = recs[k - 1]
            prev["act"], st["x"] = lax.optimization_barrier((prev["act"], st["x"]))
            peer_combine(prev)
            prev["w"], st["x"] = lax.optimization_barrier((prev["w"], st["x"]))
        if k >= 2:
            done = recs[k - 2]["st"]
            if done is not st:
                done["peer"], st["x"] = lax.optimization_barrier((done["peer"], st["x"]))
        recs.append(peer_select(st, layer))
    peer_combine(recs[-1])
    outs = [add_norm(st["x"], st["peer"], final_norm_g).reshape(bg, seq, d) for st in groups]
    return jnp.concatenate(outs, axis=0)


def _gate_matmul_body(h_ref, w_ref, o_ref):
    o_ref[...] = jnp.dot(h_ref[...], w_ref[...], preferred_element_type=F32,
                         precision=lax.Precision.HIGHEST)


def gate_matmul(h, w, tm=512):
    m, d = h.shape
    n = w.shape[1]
    return pl.pallas_call(
        _gate_matmul_body,
        grid=(m // tm,),
        in_specs=[pl.BlockSpec((tm, d), lambda i: (i, 0)),
                  pl.BlockSpec((d, n), lambda i: (0, 0))],
        out_specs=pl.BlockSpec((tm, n), lambda i: (i, 0)),
        out_shape=jax.ShapeDtypeStruct((m, n), F32),
        compiler_params=pltpu.CompilerParams(dimension_semantics=("parallel",)),
        name="gate_matmul",
    )(h, w)
```

```python
import functools
import math

import numpy as np
import jax
import jax.numpy as jnp
from jax import lax
from jax.experimental import pallas as pl
from jax.experimental.pallas import tpu as pltpu
from jax.experimental.pallas import tpu_sc as plsc

F32 = jnp.float32
BF16 = jnp.bfloat16
I32 = jnp.int32

D_MODEL = 1024
HEAD_DIM = 64
A_HEADS = 8
DILATED_BRANCHES = ((128, 1), (512, 4), (2048, 16))
DIFF_HALF = 64
DIFF_VDIM = 128
B_HEADS = 4
C_HEADS = 16
N_BUCKETS = 32
MAX_DISTANCE = 2048
PEER_HEADS = 8
N_KEYS = 128
PEER_TOPK = 16
PEER_KEY_HALF = 128
RMS_EPS = 1e-6
NEG_INF = -1e30
A_WIDTH = A_HEADS * HEAD_DIM
B_QK_WIDTH = B_HEADS * 2 * DIFF_HALF
B_V_WIDTH = B_HEADS * DIFF_VDIM
C_WIDTH = C_HEADS * HEAD_DIM
QK_SCALE = 0.125
LANES = 128
DIL_BLOCK = 128
ATT_BLOCK = 512
BATCH_GROUPS = 4

SC_CORES = 2
SC_SUBCORES = 16
SC_LANES = 16
SC_WORKERS = SC_CORES * SC_SUBCORES


def _t5_bucket_table(n):
    max_exact = N_BUCKETS // 2
    d = np.arange(n)
    df = np.maximum(d, 1).astype(np.float32)
    large = max_exact + (
        np.log(df / np.float32(max_exact)) / np.float32(math.log(MAX_DISTANCE / max_exact))
        * np.float32(N_BUCKETS - max_exact)).astype(np.int32)
    large = np.minimum(large, N_BUCKETS - 1)
    return np.where(d < max_exact, d, large).astype(np.int32)


def _norm_matmul_body(*refs, has_res, want_h):
    it = iter(refs)
    x_ref = next(it)
    r_ref = next(it) if has_res else None
    g_ref = next(it)
    w_ref = next(it)
    o_ref = next(it)
    xs_ref = next(it) if has_res else None
    hout_ref = next(it) if want_h else None
    h_scr = next(it)

    @pl.when(pl.program_id(1) == 0)
    def _():
        x = x_ref[...]
        if has_res:
            x = x + r_ref[...]
            xs_ref[...] = x
        ms = jnp.mean(x * x, axis=-1, keepdims=True)
        h = x * lax.rsqrt(ms + RMS_EPS) * g_ref[...]
        if want_h == "f32":
            hout_ref[...] = h
        elif want_h == "packed":
            hout_ref[...] = pack_bf16_pairs(h)
        h_scr[...] = h.astype(BF16)

    o_ref[...] = jnp.dot(h_scr[...], w_ref[...],
                         preferred_element_type=F32).astype(o_ref.dtype)


def norm_matmul(x, g, w, *, res=None, want_h=None, out_dtype=BF16, tm=1024, tn=1024):
    m, d = x.shape
    n = w.shape[1]
    tn = min(tn, n)
    row = pl.BlockSpec((tm, d), lambda i, j: (i, 0))
    in_specs = [row] + ([row] if res is not None else []) + [
        pl.BlockSpec((1, d), lambda i, j: (0, 0)),
        pl.BlockSpec((d, tn), lambda i, j: (0, j))]
    out_specs = [pl.BlockSpec((tm, tn), lambda i, j: (i, j))]
    out_shape = [jax.ShapeDtypeStruct((m, n), out_dtype)]
    if res is not None:
        out_specs.append(row)
        out_shape.append(jax.ShapeDtypeStruct((m, d), F32))
    if want_h == "f32":
        out_specs.append(row)
        out_shape.append(jax.ShapeDtypeStruct((m, d), F32))
    elif want_h == "packed":
        out_specs.append(pl.BlockSpec((tm, d // 2), lambda i, j: (i, 0)))
        out_shape.append(jax.ShapeDtypeStruct((m, d // 2), I32))
    args =[x] + ([res] if res is not None else []) + [g.reshape(1, d), w]
    return pl.pallas_call(
        functools.partial(_norm_matmul_body, has_res=res is not None, want_h=want_h),
        grid=(m // tm, n // tn),
        in_specs=in_specs,
        out_specs=out_specs,
        out_shape=out_shape,
        scratch_shapes=[pltpu.VMEM((tm, d), BF16)],
        compiler_params=pltpu.CompilerParams(
            dimension_semantics=("parallel", "arbitrary")),
        name="norm_matmul",
    )(*args)


def _dilated_body(q_ref, kp_ref, kc_ref, vp_ref, vc_ref, b_ref, o_ref, lse_ref):
    i = pl.program_id(3)
    q = q_ref[...] * QK_SCALE
    k = jnp.concatenate([kp_ref[...], kc_ref[...]], axis=0)
    v = jnp.concatenate([vp_ref[...], vc_ref[...]], axis=0)
    col = lax.broadcasted_iota(I32, (DIL_BLOCK, 2 * DIL_BLOCK), 1)
    has_prev = jnp.logical_or(col >= DIL_BLOCK, i > 0)
    outs, lses = [], []
    for hh in range(2):
        sl = slice(hh * HEAD_DIM, (hh + 1) * HEAD_DIM)
        s = lax.dot_general(q[:, sl], k[:, sl], (((1,), (1,)), ((), ())),
                            preferred_element_type=F32)
        s = jnp.where(has_prev, s + b_ref[hh], NEG_INF)
        m = jnp.max(s, axis=-1, keepdims=True)
        p = jnp.exp(s - m)
        l = jnp.sum(p, axis=-1, keepdims=True)
        o = jnp.dot(p.astype(BF16), v[:, sl], preferred_element_type=F32) / l
        outs.append(o)
        lses.append(jnp.broadcast_to(m + jnp.log(l), (DIL_BLOCK, HEAD_DIM)))
    o_ref[...] = jnp.concatenate(outs, axis=1)
    lse_ref[...] = jnp.concatenate(lses, axis=1)


def dilated_branch(p, bias_tile, dil, batch, seq):
    n_cols = p.shape[1]
    cb = n_cols // LANES
    rows = seq // dil
    nblk = rows // DIL_BLOCK
    pv = p.reshape(batch, rows, dil * n_cols)
    kq, kk, kv = 0, A_WIDTH // LANES, 2 * A_WIDTH // LANES
    blk = (None, DIL_BLOCK, LANES)

    def spec(col0, prev):
        if prev:
            return pl.BlockSpec(blk, lambda b, h, r, i: (b, jnp.maximum(i - 1, 0), r * cb + col0 + h))
        return pl.BlockSpec(blk, lambda b, h, r, i: (b, i, r * cb + col0 + h))

    ocb = A_WIDTH // LANES
    ospec = pl.BlockSpec(blk, lambda b, h, r, i: (b, i, r * ocb + h))
    oshape = jax.ShapeDtypeStruct((batch, rows, dil * A_WIDTH), F32)
    o, lse = pl.pallas_call(
        _dilated_body,
        grid=(batch, A_HEADS // 2, dil, nblk),
        in_specs=[spec(kq, False), spec(kk, True), spec(kk, False),
                  spec(kv, True), spec(kv, False),
                  pl.BlockSpec((2, DIL_BLOCK, 2 * DIL_BLOCK), lambda b, h, r, i: (h, 0, 0))],
        out_specs=[ospec, ospec],
        out_shape=[oshape, oshape],
        compiler_params=pltpu.CompilerParams(
            dimension_semantics=("parallel", "parallel", "parallel", "arbitrary")),
        name=f"dilated_d{dil}",
    )(pv, pv, pv, pv, pv, bias_tile)
    return o.reshape(batch * seq, A_WIDTH), lse.reshape(batch * seq, A_WIDTH)


def dilated_bias_tile(rel_bias_a, window, dil):
    n = window // dil
    assert n == DIL_BLOCK
    bucket = _t5_bucket_table(window + 1)
    row = np.arange(DIL_BLOCK)[:, None]
    c = np.arange(2 * DIL_BLOCK)[None, :]
    j = DIL_BLOCK + row - c
    valid = (j >= 0) & (j <= n)
    bk = bucket[np.clip(j, 0, n) * dil]
    tile = rel_bias_a.T[:, bk]
    return jnp.where(jnp.asarray(valid)[None], tile, NEG_INF).astype(F32)


def _diff_body(q_ref, k_ref, v_ref, b_ref, lam_ref, g_ref, o_ref,
               m_scr, l_scr, acc_scr, *, n_tiles, lam_init):
    t = ATT_BLOCK
    qi = pl.program_id(2)
    q = q_ref[...] * QK_SCALE
    lane = lax.broadcasted_iota(I32, (1, LANES), 1)
    qa = [jnp.where((lane < DIFF_HALF) == (a == 0), q, jnp.zeros_like(q)) for a in range(2)]
    ones = jnp.ones((t, LANES), BF16)
    for a in range(2):
        m_scr[a] = jnp.full((t, LANES), NEG_INF, F32)
        l_scr[a] = jnp.zeros((t, LANES), F32)
        acc_scr[a] = jnp.zeros((t, DIFF_VDIM), F32)

    def step(j, masked):
        off = pl.multiple_of(j * t, t)
        ks = k_ref[pl.ds(off, t), :]
        vs = v_ref[pl.ds(off, t), :]
        bias = b_ref[jnp.minimum(qi - j, n_tiles - 1)]
        if masked:
            row = lax.broadcasted_iota(I32, (t, t), 0)
            col = lax.broadcasted_iota(I32, (t, t), 1)
            causal = row >= col
        for a in range(2):
            s = lax.dot_general(qa[a], ks, (((1,), (1,)), ((), ())),
                                preferred_element_type=F32) + bias
            if masked:
                s = jnp.where(causal, s, NEG_INF)
            m_prev = m_scr[a]
            m_new = jnp.maximum(m_prev, jnp.max(s, axis=-1, keepdims=True))
            alpha = jnp.exp(m_prev - m_new)
            pb = jnp.exp(s - jnp.concatenate([m_new] * (t // LANES), axis=1)).astype(BF16)
            l_scr[a] = alpha * l_scr[a] + jnp.dot(pb, ones, preferred_element_type=F32)
            acc_scr[a] = alpha * acc_scr[a] + jnp.dot(pb, vs, preferred_element_type=F32)
            m_scr[a] = m_new

    def loop_body(j, carry):
        step(j, False)
        return carry

    lax.fori_loop(0, qi, loop_body, 0)
    step(qi, True)

    lp = lam_ref[...]
    lam = (jnp.exp(jnp.sum(lp[0:1] * lp[1:2])) - jnp.exp(jnp.sum(lp[2:3] * lp[3:4]))
           + lam_init)
    o = acc_scr[0] / l_scr[0] - lam * (acc_scr[1] / l_scr[1])
    ms = jnp.mean(o * o, axis=-1, keepdims=True)
    y = o * lax.rsqrt(ms + RMS_EPS) * g_ref[...]
    o_ref[...] = (y * (1.0 - lam_init)).astype(o_ref.dtype)


def diff_attention(p, bias_tiles, lam_params, ln_g, lam_init, batch, seq):
    t = ATT_BLOCK
    n_tiles = bias_tiles.shape[1]
    cq = 3 * A_WIDTH // LANES
    ck = cq + B_QK_WIDTH // LANES
    cv = ck + B_QK_WIDTH // LANES
    return pl.pallas_call(
        functools.partial(_diff_body, n_tiles=n_tiles, lam_init=lam_init),
        grid=(batch, B_HEADS, seq // t),
        in_specs=[
            pl.BlockSpec((None, t, LANES), lambda b, h, i: (b, i, cq + h)),
            pl.BlockSpec((None, seq, LANES), lambda b, h, i: (b, 0, ck + h)),
            pl.BlockSpec((None, seq, LANES), lambda b, h, i: (b, 0, cv + h)),
            pl.BlockSpec((None, n_tiles, t, t), lambda b, h, i: (h, 0, 0, 0)),
            pl.BlockSpec((4, DIFF_HALF), lambda b, h, i: (0, 0)),
            pl.BlockSpec((1, DIFF_VDIM), lambda b, h, i: (0, 0)),
        ],
        out_specs=pl.BlockSpec((None, t, LANES), lambda b, h, i: (b, i, h)),
        out_shape=jax.ShapeDtypeStruct((batch, seq, B_V_WIDTH), BF16),
        scratch_shapes=[pltpu.VMEM((2, t, LANES), F32), pltpu.VMEM((2, t, LANES), F32),
                        pltpu.VMEM((2, t, DIFF_VDIM), F32)],
        compiler_params=pltpu.CompilerParams(
            dimension_semantics=("parallel", "parallel", "arbitrary")),
        name="diff_attention",
    )(p, p, p, bias_tiles, lam_params, ln_g.reshape(1, DIFF_VDIM))


def diff_bias_tiles(rel_bias_b, seq):
    t = ATT_BLOCK
    bucket = _t5_bucket_table(max(seq, 2 * MAX_DISTANCE) + 2 * t)
    sat = bucket[-1]
    d_sat = int(np.max(np.nonzero(bucket != sat)[0])) + 1
    n_full = (d_sat + t - 1 + t - 1) // t
    n_tiles = n_full + 1
    assert n_full * t - (t - 1) >= d_sat
    n = np.arange(2 * t)[None, :]
    base = np.arange(n_tiles)[:, None] * t
    dist = np.clip(np.where(n < t, base - n, base + 2 * t - n), 0, None)
    w = rel_bias_b.T[:, bucket[dist]].astype(F32)
    rep = jnp.broadcast_to(w[:, :, None, :], (B_HEADS, n_tiles, t, 2 * t))
    flat = rep.reshape(B_HEADS, n_tiles, 2 * t * t)[:, :, :t * (2 * t - 1)]
    return flat.reshape(B_HEADS, n_tiles, t, 2 * t - 1)[:, :, :, :t]


def _logsig_cumsum_body(f_ref, b_ref, c_ref, carry_scr):
    t = f_ref.shape[0]

    @pl.when(pl.program_id(1) == 0)
    def _():
        carry_scr[...] = jnp.zeros_like(carry_scr)

    x = f_ref[...] + b_ref[...]
    ls = jnp.minimum(x, 0.0) - jnp.log1p(jnp.exp(-jnp.abs(x)))
    row = lax.broadcasted_iota(I32, (t, t), 0)
    col = lax.broadcasted_iota(I32, (t, t), 1)
    tri = (row >= col).astype(F32)
    c = jnp.dot(tri, ls, preferred_element_type=F32,
                precision=lax.Precision.HIGHEST) + carry_scr[...]
    c_ref[...] = c
    carry_scr[...] = c[t - 1:t, :]


def logsig_cumsum(fg, b_f, batch, seq, t=512):
    return pl.pallas_call(
        _logsig_cumsum_body,
        grid=(batch, seq // t),
        in_specs=[pl.BlockSpec((None, t, LANES), lambda b, i: (b, i, 0)),
                  pl.BlockSpec((1, LANES), lambda b, i: (0, 0))],
        out_specs=pl.BlockSpec((None, t, LANES), lambda b, i: (b, i, 0)),
        out_shape=jax.ShapeDtypeStruct((batch, seq, LANES), F32),
        scratch_shapes=[pltpu.VMEM((1, LANES), F32)],
        compiler_params=pltpu.CompilerParams(
            dimension_semantics=("parallel", "arbitrary")),
        name="logsig_cumsum",
    )(fg, b_f)


def _fox_body(q_ref, k_ref, v_ref, cq_ref, ck_ref, o_ref, m_scr, acc_scr):
    t = ATT_BLOCK
    qi = pl.program_id(2)
    q = q_ref[...] * QK_SCALE
    lane = lax.broadcasted_iota(I32, (1, LANES), 1)
    own = [(lane < HEAD_DIM) == (a == 0) for a in range(2)]
    qa = [jnp.where(own[a], q, jnp.zeros_like(q)) for a in range(2)]
    cqb = [jnp.broadcast_to(cq_ref[a], (t, LANES)) for a in range(2)]
    for a in range(2):
        m_scr[a] = jnp.full((t, LANES), NEG_INF, F32)
        acc_scr[a] = jnp.zeros((t, LANES), F32)

    def step(j, masked):
        off = pl.multiple_of(j * t, t)
        ks = k_ref[pl.ds(off, t), :]
        vs = v_ref[pl.ds(off, t), :]
        if masked:
            row = lax.broadcasted_iota(I32, (t, t), 0)
            col = lax.broadcasted_iota(I32, (t, t), 1)
            causal = row >= col
        for a in range(2):
            s = lax.dot_general(qa[a], ks, (((1,), (1,)), ((), ())),
                                preferred_element_type=F32)
            s = s - ck_ref[a, :, pl.ds(off, t)]
            if masked:
                s = jnp.where(causal, s, NEG_INF)
            m_prev = m_scr[a]
            m_new = jnp.maximum(m_prev, jnp.max(s, axis=-1, keepdims=True) + cqb[a])
            alpha = jnp.exp(m_prev - m_new)
            shift = m_new - cqb[a]
            pb = jnp.exp(s - jnp.concatenate([shift] * (t // LANES), axis=1)).astype(BF16)
            v_aug = jnp.where(own[a], vs, jnp.ones_like(vs))
            acc_scr[a] = alpha * acc_scr[a] + jnp.dot(pb, v_aug, preferred_element_type=F32)
            m_scr[a] = m_new

    def loop_body(j, carry):
        step(j, False)
        return carry

    lax.fori_loop(0, qi, loop_body, 0)
    step(qi, True)
    r = [acc_scr[a] / pltpu.roll(acc_scr[a], HEAD_DIM, 1) for a in range(2)]
    o_ref[...] = jnp.where(own[0], r[0], r[1]).astype(o_ref.dtype)


def fox_attention(p, cq, ck, batch, seq):
    t = ATT_BLOCK
    nk = C_WIDTH // LANES
    return pl.pallas_call(
        _fox_body,
        grid=(batch, C_HEADS // 2, seq // t),
        in_specs=[
            pl.BlockSpec((None, t, LANES), lambda b, h, i: (b, i, h)),
            pl.BlockSpec((None, seq, LANES), lambda b, h, i: (b, 0, nk + h)),
            pl.BlockSpec((None, seq, LANES), lambda b, h, i: (b, 0, 2 * nk + h)),
            pl.BlockSpec((None, 2, t, 1), lambda b, h, i: (b, h, i, 0)),
            pl.BlockSpec((None, 2, 1, seq), lambda b, h, i: (b, h, 0, 0)),
        ],
        out_specs=pl.BlockSpec((None, t, LANES), lambda b, h, i: (b, i, h)),
        out_shape=jax.ShapeDtypeStruct((batch, seq, C_WIDTH), BF16),
        scratch_shapes=[pltpu.VMEM((2, t, LANES), F32), pltpu.VMEM((2, t, LANES), F32)],
        compiler_params=pltpu.CompilerParams(
            dimension_semantics=("parallel", "parallel", "arbitrary")),
        name="fox_attention",
    )(p, p, p, cq, ck)


def _even_out_body(o1, l1, o2, l2, o3, l3, ob_ref, w_ref, x_ref, out_ref, a_scr):
    @pl.when(pl.program_id(1) == 0)
    def _():
        a1, a2, a3 = l1[...], l2[...], l3[...]
        mx = jnp.maximum(jnp.maximum(a1, a2), a3)
        e1, e2, e3 = jnp.exp(a1 - mx), jnp.exp(a2 - mx), jnp.exp(a3 - mx)
        oa = (e1 * o1[...] + e2 * o2[...] + e3 * o3[...]) / (e1 + e2 + e3)
        a_scr[:, :A_WIDTH] = oa.astype(BF16)
        a_scr[:, A_WIDTH:] = ob_ref[...]

    out_ref[...] = x_ref[...] + jnp.dot(a_scr[...], w_ref[...], preferred_element_type=F32)


def even_out_proj(branches, ob, w, x, tm=512, tn=1024):
    m, d = x.shape
    half = pl.BlockSpec((tm, A_WIDTH), lambda i, j: (i, 0))
    flat = [a for pair in branches for a in pair]
    return pl.pallas_call(
        _even_out_body,
        grid=(m // tm, d // tn),
        in_specs=[half] * 6 + [
            pl.BlockSpec((tm, B_V_WIDTH), lambda i, j: (i, 0)),
            pl.BlockSpec((A_WIDTH + B_V_WIDTH, tn), lambda i, j: (0, j)),
            pl.BlockSpec((tm, tn), lambda i, j: (i, j))],
        out_specs=pl.BlockSpec((tm, tn), lambda i, j: (i, j)),
        out_shape=jax.ShapeDtypeStruct((m, d), F32),
        scratch_shapes=[pltpu.VMEM((tm, A_WIDTH + B_V_WIDTH), BF16)],
        compiler_params=pltpu.CompilerParams(
            dimension_semantics=("parallel", "arbitrary")),
        name="even_out_proj",
    )(*flat, ob, w, x)


def _matmul_res_body(a_ref, w_ref, x_ref, o_ref):
    o_ref[...] = x_ref[...] + jnp.dot(a_ref[...], w_ref[...], preferred_element_type=F32)


def matmul_residual(a, w, x, tm=1024, tn=1024):
    m, k = a.shape
    n = w.shape[1]
    return pl.pallas_call(
        _matmul_res_body,
        grid=(m // tm, n // tn),
        in_specs=[pl.BlockSpec((tm, k), lambda i, j: (i, 0)),
                  pl.BlockSpec((k, tn), lambda i, j: (0, j)),
                  pl.BlockSpec((tm, tn), lambda i, j: (i, j))],
        out_specs=pl.BlockSpec((tm, tn), lambda i, j: (i, j)),
        out_shape=jax.ShapeDtypeStruct((m, n), F32),
        compiler_params=pltpu.CompilerParams(
            dimension_semantics=("parallel", "parallel")),
        name="matmul_residual",
    )(a, w, x)


def _add_norm_body(x_ref, r_ref, g_ref, o_ref):
    x = x_ref[...] + r_ref[...]
    ms = jnp.mean(x * x, axis=-1, keepdims=True)
    o_ref[...] = x * lax.rsqrt(ms + RMS_EPS) * g_ref[...]


def add_norm(x, r, g, tm=512):
    m, d = x.shape
    row = pl.BlockSpec((tm, d), lambda i: (i, 0))
    return pl.pallas_call(
        _add_norm_body,
        grid=(m // tm,),
        in_specs=[row, row, pl.BlockSpec((1, d), lambda i: (0, 0))],
        out_specs=row,
        out_shape=jax.ShapeDtypeStruct((m, d), F32),
        compiler_params=pltpu.CompilerParams(dimension_semantics=("parallel",)),
        name="add_norm",
    )(x, r, g.reshape(1, d))


PEER_CAND_ROWS = PEER_TOPK + 8 * (PEER_TOPK - 1)


def _peer_topk_body(q_ref, sk_ref, idx_ref, gate_ref, ts_scr, ti_scr, bs_scr, be_scr,
                    gt_scr, it_scr):
    tm = q_ref.shape[0]
    neg_inf = jnp.float32(-jnp.inf)
    key_id = lax.broadcasted_iota(I32, (N_KEYS, tm), 0)

    def pair_body(pr, carry):
        off = pl.multiple_of(pr * PEER_KEY_HALF, PEER_KEY_HALF)
        sc = lax.dot_general(sk_ref[pr], q_ref[:, pl.ds(off, PEER_KEY_HALF)],
                             (((1,), (1,)), ((), ())), preferred_element_type=F32)

        def k_body(k, vals):
            m = jnp.max(vals, axis=0, keepdims=True)
            sel = jnp.min(jnp.where(vals == m, key_id, N_KEYS), axis=0, keepdims=True)
            ts_scr[pr, pl.ds(k, 1), :] = m
            ti_scr[pr, pl.ds(k, 1), :] = sel
            return jnp.where(key_id == sel, neg_inf, vals)

        lax.fori_loop(0, PEER_TOPK, k_body, sc)
        return carry

    lax.fori_loop(0, 2 * PEER_HEADS, pair_body, 0)

    r = lax.broadcasted_iota(I32, (PEER_CAND_ROWS, 1), 0)
    cand_id = jnp.where(r < PEER_TOPK, r,
                        (1 + (r - PEER_TOPK) // 8) * PEER_TOPK + (r - PEER_TOPK) % 8)

    def head_body(h, carry):
        s1, s2 = ts_scr[2 * h], ts_scr[2 * h + 1]
        i1, i2 = ti_scr[2 * h] * N_KEYS, ti_scr[2 * h + 1]
        vals = jnp.concatenate(
            [s1[0:1] + s2] + [s1[a:a + 1] + s2[0:8] for a in range(1, PEER_TOPK)], axis=0)
        eidx = jnp.concatenate(
            [i1[0:1] + i2] + [i1[a:a + 1] + i2[0:8] for a in range(1, PEER_TOPK)], axis=0)

        def k_body(k, vals):
            m = jnp.max(vals, axis=0, keepdims=True)
            sel = jnp.min(jnp.where(vals == m, cand_id, PEER_TOPK * PEER_TOPK),
                          axis=0, keepdims=True)
            hit = cand_id == sel
            bs_scr[pl.ds(k, 1), :] = m
            be_scr[pl.ds(k, 1), :] = jnp.sum(jnp.where(hit, eidx, 0), axis=0, keepdims=True)
            return jnp.where(hit, neg_inf, vals)

        lax.fori_loop(0, PEER_TOPK, k_body, vals)
        bs = bs_scr[...]
        e = jnp.exp(bs - jnp.max(bs, axis=0, keepdims=True))
        row0 = pl.multiple_of(h * PEER_TOPK, PEER_TOPK)
        gt_scr[pl.ds(row0, PEER_TOPK), :] = e / jnp.sum(e, axis=0, keepdims=True)
        it_scr[pl.ds(row0, PEER_TOPK), :] = be_scr[...]
        return carry

    lax.fori_loop(0, PEER_HEADS, head_body, 0)
    gate_ref[...] = gt_scr[...].T
    idx_ref[...] = it_scr[...].T


def peer_topk(q, subkeys, tm=256):
    m = q.shape[0]
    n_sel = PEER_HEADS * PEER_TOPK
    out_spec = pl.BlockSpec((tm, n_sel), lambda i: (i, 0))
    return pl.pallas_call(
        _peer_topk_body,
        grid=(m // tm,),
        in_specs=[pl.BlockSpec((tm, q.shape[1]), lambda i: (i, 0)),
                  pl.BlockSpec(subkeys.shape, lambda i: (0, 0, 0))],
        out_specs=[out_spec, out_spec],
        out_shape=[jax.ShapeDtypeStruct((m, n_sel), I32),
                   jax.ShapeDtypeStruct((m, n_sel), F32)],
        scratch_shapes=[pltpu.VMEM((2 * PEER_HEADS, PEER_TOPK, tm), F32),
                        pltpu.VMEM((2 * PEER_HEADS, PEER_TOPK, tm), I32),
                        pltpu.VMEM((PEER_TOPK, tm), F32),
                        pltpu.VMEM((PEER_TOPK, tm), I32),
                        pltpu.VMEM((n_sel, tm), F32),
                        pltpu.VMEM((n_sel, tm), I32)],
        compiler_params=pltpu.CompilerParams(dimension_semantics=("parallel",)),
        name="peer_topk",
    )(q, subkeys)


def _gelu_gate_body(a_ref, g_ref, o_ref):
    a = a_ref[...]
    o_ref[...] = g_ref[...] * (0.5 * a * (1.0 + lax.erf(a * (2.0 ** -0.5))))


def gelu_gate(act, gate, tm=2048):
    m, n = act.shape
    spec = pl.BlockSpec((tm, n), lambda i: (i, 0))
    return pl.pallas_call(
        _gelu_gate_body,
        grid=(m // tm,),
        in_specs=[spec, spec],
        out_specs=spec,
        out_shape=jax.ShapeDtypeStruct((m, n), F32),
        compiler_params=pltpu.CompilerParams(dimension_semantics=("parallel",)),
        name="gelu_gate",
    )(act, gate)


SC_TOK_CHUNK = 32
SC_RING = 8
SC_BF16_GROUP = 4
SC_FMT = plsc.PackFormat.INTERLEAVED


def _sc_worker_id():
    return lax.axis_index("s") * SC_CORES + lax.axis_index("c")


def pack_bf16_pairs(t):
    half = t.shape[-1] // 2
    bits = lax.bitcast_convert_type(t.astype(BF16).astype(F32), I32)
    return (bits[..., half:] & jnp.int32(-65536)) | lax.shift_right_logical(
        bits[..., :half], jnp.int32(16))


def _sc_row_pipeline(idx_v, table_hbm, rows_v, sems, n_items, groups, compute):
    def gather(item):
        tt, g = item // groups, item % groups
        ids = idx_v[tt, pl.ds(g * SC_LANES, SC_LANES)]
        slot = item % SC_RING
        return pltpu.make_async_copy(table_hbm.at[ids], rows_v.at[slot], sems.at[slot])

    for s in range(SC_RING - 1):
        gather(s).start()

    def item_body(item, carry):
        nxt = item + SC_RING - 1

        @pl.when(nxt < n_items)
        def _():
            gather(nxt).start()

        gather(item).wait()
        compute(item // groups, item % groups, item % SC_RING)
        return carry

    lax.fori_loop(0, n_items, item_body, 0)


def peer_expert_dots(hp, idx, up):
    m, dw = hp.shape
    n_sel = idx.shape[1]
    per_w = m // SC_WORKERS
    n_chunks = per_w // SC_TOK_CHUNK
    groups = n_sel // SC_LANES
    step = SC_BF16_GROUP * SC_LANES
    mesh = plsc.VectorSubcoreMesh(core_axis_name="c", subcore_axis_name="s")

    @functools.partial(
        pl.kernel, mesh=mesh,
        out_type=jax.ShapeDtypeStruct((m, n_sel), F32),
        scratch_types=[
            pltpu.VMEM((SC_TOK_CHUNK, n_sel), I32),
            pltpu.VMEM((SC_TOK_CHUNK, dw), I32),
            pltpu.VMEM((SC_TOK_CHUNK, n_sel), F32),
            pltpu.VMEM((SC_RING, SC_LANES, dw), I32),
            pltpu.VMEM((SC_LANES * SC_LANES,), F32),
            pltpu.SemaphoreType.DMA((SC_RING,)),
        ],
        compiler_params=pltpu.CompilerParams(needs_layout_passes=False),
        name="peer_expert_dots",
    )
    def k(h_hbm, idx_hbm, u_hbm, act_hbm, idx_v, h_v, act_v, rows_v, part_v, sems):
        base = _sc_worker_id() * per_w
        lane = lax.broadcasted_iota(I32, (SC_LANES,), 0)

        def compute(tt, g, slot):
            def grp_body(q, accs):
                off = pl.multiple_of(q * step, step)
                xs = [plsc.bitcast(h_v[tt, pl.ds(off + c * SC_LANES, SC_LANES)], BF16)
                      for c in range(SC_BF16_GROUP)]
                new = []
                for e in range(SC_LANES):
                    s = None
                    for c in range(SC_BF16_GROUP):
                        p = plsc.bitcast(
                            rows_v[slot, e, pl.ds(off + c * SC_LANES, SC_LANES)], BF16) * xs[c]
                        s = p if s is None else s + p
                    lo, hi = plsc.unpack(s, format=SC_FMT)
                    new.append(accs[e] + (lo + hi))
                return tuple(new)

            accs = lax.fori_loop(
                0, dw // step, grp_body,
                tuple(jnp.zeros((SC_LANES,), F32) for _ in range(SC_LANES)))
            for e in range(SC_LANES):
                part_v[pl.ds(e * SC_LANES, SC_LANES)] = accs[e]
            tot = jnp.zeros((SC_LANES,), F32)
            for l in range(SC_LANES):
                tot = tot + plsc.load_gather(part_v, [lane * SC_LANES + l])
            act_v[tt, pl.ds(g * SC_LANES, SC_LANES)] = tot

        def chunk_body(c, carry):
            t0 = base + c * SC_TOK_CHUNK
            pltpu.sync_copy(idx_hbm.at[pl.ds(t0, SC_TOK_CHUNK)], idx_v)
            pltpu.sync_copy(h_hbm.at[pl.ds(t0, SC_TOK_CHUNK)], h_v)
            _sc_row_pipeline(idx_v, u_hbm, rows_v, sems, SC_TOK_CHUNK * groups, groups, compute)
            pltpu.sync_copy(act_v, act_hbm.at[pl.ds(t0, SC_TOK_CHUNK)])
            return carry

        lax.fori_loop(0, n_chunks, chunk_body, 0)

    return k(hp, idx, up)


def peer_expert_combine(w, idx, vp):
    m, n_sel = w.shape
    dw = vp.shape[1]
    d = 2 * dw
    per_w = m // SC_WORKERS
    n_chunks = per_w // SC_TOK_CHUNK
    n_vec = d // SC_LANES
    groups = n_sel // SC_LANES
    mesh = plsc.VectorSubcoreMesh(core_axis_name="c", subcore_axis_name="s")

    @functools.partial(
        pl.kernel, mesh=mesh,
        out_type=jax.ShapeDtypeStruct((m, d), F32),
        scratch_types=[
            pltpu.VMEM((SC_TOK_CHUNK, n_sel), I32),
            pltpu.VMEM((SC_TOK_CHUNK, n_sel), F32),
            pltpu.VMEM((SC_TOK_CHUNK, d), F32),
            pltpu.VMEM((SC_RING, SC_LANES, dw), I32),
            pltpu.SemaphoreType.DMA((SC_RING,)),
        ],
        compiler_params=pltpu.CompilerParams(needs_layout_passes=False),
        name="peer_expert_combine",
    )
    def k(w_hbm, idx_hbm, v_hbm, out_hbm, idx_v, w_v, out_v, rows_v, sems):
        base = _sc_worker_id() * per_w

        def compute(tt, g, slot):
            splat = []
            for e in range(SC_LANES):
                s = plsc.load_gather(w_v, [jnp.full((SC_LANES,), tt, I32),
                                           jnp.full((SC_LANES,), g * SC_LANES + e, I32)])
                splat.append(plsc.pack(s, s, format=SC_FMT))

            @plsc.parallel_loop(0, dw // SC_LANES)
            def _(j):
                off = pl.multiple_of(j * SC_LANES, SC_LANES)
                acc_lo = out_v[tt, pl.ds(off, SC_LANES)]
                acc_hi = out_v[tt, pl.ds(dw + off, SC_LANES)]
                for e0 in range(0, SC_LANES, SC_BF16_GROUP):
                    s = None
                    for e in range(e0, e0 + SC_BF16_GROUP):
                        p = plsc.bitcast(rows_v[slot, e, pl.ds(off, SC_LANES)], BF16) * splat[e]
                        s = p if s is None else s + p
                    lo, hi = plsc.unpack(s, format=SC_FMT)
                    acc_lo = acc_lo + lo
                    acc_hi = acc_hi + hi
                out_v[tt, pl.ds(off, SC_LANES)] = acc_lo
                out_v[tt, pl.ds(dw + off, SC_LANES)] = acc_hi

        def chunk_body(c, carry):
            t0 = base + c * SC_TOK_CHUNK
            pltpu.sync_copy(idx_hbm.at[pl.ds(t0, SC_TOK_CHUNK)], idx_v)
            pltpu.sync_copy(w_hbm.at[pl.ds(t0, SC_TOK_CHUNK)], w_v)

            def zero_body(z, carry2):
                tt, j = z // n_vec, z % n_vec
                out_v[tt, pl.ds(pl.multiple_of(j * SC_LANES, SC_LANES), SC_LANES)] = (
                    jnp.zeros((SC_LANES,), F32))
                return carry2

            lax.fori_loop(0, SC_TOK_CHUNK * n_vec, zero_body, 0)
            _sc_row_pipeline(idx_v, v_hbm, rows_v, sems, SC_TOK_CHUNK * groups, groups, compute)
            pltpu.sync_copy(out_v, out_hbm.at[pl.ds(t0, SC_TOK_CHUNK)])
            return carry

        lax.fori_loop(0, n_chunks, chunk_body, 0)

    return k(w, idx, vp)


def kernel(x, norm_mix_g, norm_ffn_g, final_norm_g, rel_bias, even_w_in, even_w_out,
           diff_lambda, diff_ln_g, odd_w_in, odd_b_f, odd_w_out, peer_wq, peer_subkeys,
           peer_u, peer_v):
    batch, seq, d = x.shape

    dil_tiles = [dilated_bias_tile(rel_bias[:, :A_HEADS], w, dl) for w, dl in DILATED_BRANCHES]
    diff_tiles = diff_bias_tiles(rel_bias[:, A_HEADS:], seq)
    lam_init = 0.8 - 0.6 * math.exp(-0.3 * 0)
    even_in, even_out = even_w_in[0].astype(BF16), even_w_out[0].astype(BF16)
    w_in = odd_w_in[0]
    odd_in, odd_out = w_in[:, :3 * C_WIDTH].astype(BF16), odd_w_out[0].astype(BF16)
    w_gate = jnp.pad(w_in[:, 3 * C_WIDTH:], ((0, 0), (0, LANES - C_HEADS)))
    b_f = jnp.pad(odd_b_f[0], (0, LANES - C_HEADS)).reshape(1, LANES)
    peer = [dict(wq=peer_wq[l].astype(BF16),
                 sk=peer_subkeys[l].reshape(2 * PEER_HEADS, N_KEYS, PEER_KEY_HALF).astype(BF16),
                 up=pack_bf16_pairs(peer_u[l]), vp=pack_bf16_pairs(peer_v[l]))
            for l in range(2)]

    bg = batch // BATCH_GROUPS
    m = bg * seq
    groups = [dict(x=x[g * bg:(g + 1) * bg].reshape(m, d)) for g in range(BATCH_GROUPS)]

    def peer_select(st, layer, after=None):
        q, hp = norm_matmul(st["x"], norm_ffn_g[layer], peer[layer]["wq"], want_h="packed")
        idx, gate = peer_topk(q, peer[layer]["sk"])
        if after is not None and after["st"] is not st:
            after["st"]["peer"], idx = lax.optimization_barrier((after["st"]["peer"], idx))
        return dict(st=st, layer=layer, idx=idx, gate=gate,
                    act=peer_expert_dots(hp, idx, peer[layer]["up"]))

    def peer_combine(rec):
        rec["w"] = gelu_gate(rec["act"], rec["gate"])
        rec["st"]["peer"] = peer_expert_combine(rec["w"], rec["idx"], peer[rec["layer"]]["vp"])

    def even_layer(st):
        (p,) = norm_matmul(st["x"], norm_mix_g[0], even_in)
        branches = [dilated_branch(p, tile, dl, bg, seq)
                    for tile, (_, dl) in zip(dil_tiles, DILATED_BRANCHES)]
        ob = diff_attention(p.reshape(bg, seq, -1), diff_tiles, diff_lambda[0], diff_ln_g[0],
                            lam_init, bg, seq)
        st["x"] = even_out_proj(branches, ob.reshape(m, B_V_WIDTH), even_out, st["x"])

    def odd_layer(st):
        p, st["x"], h = norm_matmul(st["x"], norm_mix_g[1], odd_in, res=st["peer"], want_h="f32")
        fg = gate_matmul(h, w_gate)
        c = logsig_cumsum(fg.reshape(bg, seq, LANES), b_f, bg, seq)
        ct = c[:, :, :C_HEADS].transpose(0, 2, 1)
        o = fox_attention(p.reshape(bg, seq, -1), ct[:, :, :, None], ct[:, :, None, :], bg, seq)
        st["x"] = matmul_residual(o.reshape(m, C_WIDTH), odd_out, st["x"])

    stages = [(even_layer, st, 0) for st in groups] + [(odd_layer, st, 1) for st in groups]
    recs = []
    for k, (mixer, st, layer) in enumerate(stages):
        mixer(st)
        if k >= 1:
            prev = recs[k - 1]
            prev["act"], st["x"] = lax.optimization_barrier((prev["act"], st["x"]))
            peer_combine(prev)
            prev["w"], st["x"] = lax.optimization_barrier((prev["w"], st["x"]))
        recs.append(peer_select(st, layer, after=recs[k - 1] if k >= 1 else None))
    peer_combine(recs[-1])
    outs = [add_norm(st["x"], st["peer"], final_norm_g).reshape(bg, seq, d) for st in groups]
    return jnp.concatenate(outs, axis=0)


def _gate_matmul_body(h_ref, w_ref, o_ref):
    o_ref[...] = jnp.dot(h_ref[...], w_ref[...], preferred_element_type=F32,
                         precision=lax.Precision.HIGHEST)


def gate_matmul(h, w, tm=512):
    m, d = h.shape
    n = w.shape[1]
    return pl.pallas_call(
        _gate_matmul_body,
        grid=(m // tm,),
        in_specs=[pl.BlockSpec((tm, d), lambda i: (i, 0)),
                  pl.BlockSpec((d, n), lambda i: (0, 0))],
        out_specs=pl.BlockSpec((tm, n), lambda i: (i, 0)),
        out_shape=jax.ShapeDtypeStruct((m, n), F32),
        compiler_params=pltpu.CompilerParams(dimension_semantics=("parallel",)),
        name="gate_matmul",
    )(h, w)
```

```python
import functools
import math

import numpy as np
import jax
import jax.numpy as jnp
from jax import lax
from jax.experimental import pallas as pl
from jax.experimental.pallas import tpu as pltpu
from jax.experimental.pallas import tpu_sc as plsc

F32 = jnp.float32
BF16 = jnp.bfloat16
I32 = jnp.int32

D_MODEL = 1024
HEAD_DIM = 64
A_HEADS = 8
DILATED_BRANCHES = ((128, 1), (512, 4), (2048, 16))
DIFF_HALF = 64
DIFF_VDIM = 128
B_HEADS = 4
C_HEADS = 16
N_BUCKETS = 32
MAX_DISTANCE = 2048
PEER_HEADS = 8
N_KEYS = 128
PEER_TOPK = 16
PEER_KEY_HALF = 128
RMS_EPS = 1e-6
NEG_INF = -1e30
A_WIDTH = A_HEADS * HEAD_DIM
B_QK_WIDTH = B_HEADS * 2 * DIFF_HALF
B_V_WIDTH = B_HEADS * DIFF_VDIM
C_WIDTH = C_HEADS * HEAD_DIM
QK_SCALE = 0.125
LANES = 128
DIL_BLOCK = 128
ATT_BLOCK = 512
BATCH_GROUPS = 4

SC_CORES = 2
SC_SUBCORES = 16
SC_LANES = 16
SC_WORKERS = SC_CORES * SC_SUBCORES


def _t5_bucket_table(n):
    max_exact = N_BUCKETS // 2
    d = np.arange(n)
    df = np.maximum(d, 1).astype(np.float32)
    large = max_exact + (
        np.log(df / np.float32(max_exact)) / np.float32(math.log(MAX_DISTANCE / max_exact))
        * np.float32(N_BUCKETS - max_exact)).astype(np.int32)
    large = np.minimum(large, N_BUCKETS - 1)
    return np.where(d < max_exact, d, large).astype(np.int32)


def _norm_matmul_body(*refs, has_res, want_h):
    it = iter(refs)
    x_ref = next(it)
    r_ref = next(it) if has_res else None
    g_ref = next(it)
    w_ref = next(it)
    o_ref = next(it)
    xs_ref = next(it) if has_res else None
    hout_ref = next(it) if want_h else None
    h_scr = next(it)

    @pl.when(pl.program_id(1) == 0)
    def _():
        x = x_ref[...]
        if has_res:
            x = x + r_ref[...]
            xs_ref[...] = x
        ms = jnp.mean(x * x, axis=-1, keepdims=True)
        h = x * lax.rsqrt(ms + RMS_EPS) * g_ref[...]
        if want_h == "f32":
            hout_ref[...] = h
        elif want_h == "packed":
            hout_ref[...] = pack_bf16_pairs(h)
        h_scr[...] = h.astype(BF16)

    o_ref[...] = jnp.dot(h_scr[...], w_ref[...],
                         preferred_element_type=F32).astype(o_ref.dtype)


def norm_matmul(x, g, w, *, res=None, want_h=None, out_dtype=BF16, tm=1024, tn=1024):
    m, d = x.shape
    n = w.shape[1]
    tn = min(tn, n)
    row = pl.BlockSpec((tm, d), lambda i, j: (i, 0))
    in_specs = [row] + ([row] if res is not None else []) + [
        pl.BlockSpec((1, d), lambda i, j: (0, 0)),
        pl.BlockSpec((d, tn), lambda i, j: (0, j))]
    out_specs = [pl.BlockSpec((tm, tn), lambda i, j: (i, j))]
    out_shape = [jax.ShapeDtypeStruct((m, n), out_dtype)]
    if res is not None:
        out_specs.append(row)
        out_shape.append(jax.ShapeDtypeStruct((m, d), F32))
    if want_h == "f32":
        out_specs.append(row)
        out_shape.append(jax.ShapeDtypeStruct((m, d), F32))
    elif want_h == "packed":
        out_specs.append(pl.BlockSpec((tm, d // 2), lambda i, j: (i, 0)))
        out_shape.append(jax.ShapeDtypeStruct((m, d // 2), I32))
    args =[x] + ([res] if res is not None else []) + [g.reshape(1, d), w]
    return pl.pallas_call(
        functools.partial(_norm_matmul_body, has_res=res is not None, want_h=want_h),
        grid=(m // tm, n // tn),
        in_specs=in_specs,
        out_specs=out_specs,
        out_shape=out_shape,
        scratch_shapes=[pltpu.VMEM((tm, d), BF16)],
        compiler_params=pltpu.CompilerParams(
            dimension_semantics=("parallel", "arbitrary")),
        name="norm_matmul",
    )(*args)


def _dilated_body(q_ref, kp_ref, kc_ref, vp_ref, vc_ref, b_ref, o_ref, lse_ref):
    i = pl.program_id(3)
    q = q_ref[...] * QK_SCALE
    k = jnp.concatenate([kp_ref[...], kc_ref[...]], axis=0)
    v = jnp.concatenate([vp_ref[...], vc_ref[...]], axis=0)
    col = lax.broadcasted_iota(I32, (DIL_BLOCK, 2 * DIL_BLOCK), 1)
    has_prev = jnp.logical_or(col >= DIL_BLOCK, i > 0)
    outs, lses = [], []
    for hh in range(2):
        sl = slice(hh * HEAD_DIM, (hh + 1) * HEAD_DIM)
        s = lax.dot_general(q[:, sl], k[:, sl], (((1,), (1,)), ((), ())),
                            preferred_element_type=F32)
        s = jnp.where(has_prev, s + b_ref[hh], NEG_INF)
        m = jnp.max(s, axis=-1, keepdims=True)
        p = jnp.exp(s - m)
        l = jnp.sum(p, axis=-1, keepdims=True)
        o = jnp.dot(p.astype(BF16), v[:, sl], preferred_element_type=F32) / l
        outs.append(o)
        lses.append(jnp.broadcast_to(m + jnp.log(l), (DIL_BLOCK, HEAD_DIM)))
    o_ref[...] = jnp.concatenate(outs, axis=1)
    lse_ref[...] = jnp.concatenate(lses, axis=1)


def dilated_branch(p, bias_tile, dil, batch, seq):
    if dil > 1:
        p = p[:, :3 * A_WIDTH]
    n_cols = p.shape[1]
    cb = n_cols // LANES
    rows = seq // dil
    nblk = rows // DIL_BLOCK
    pv = p.reshape(batch, rows, dil * n_cols)
    kq, kk, kv = 0, A_WIDTH // LANES, 2 * A_WIDTH // LANES
    blk = (None, DIL_BLOCK, LANES)

    def spec(col0, prev):
        if prev:
            return pl.BlockSpec(blk, lambda b, h, r, i: (b, jnp.maximum(i - 1, 0), r * cb + col0 + h))
        return pl.BlockSpec(blk, lambda b, h, r, i: (b, i, r * cb + col0 + h))

    ocb = A_WIDTH // LANES
    ospec = pl.BlockSpec(blk, lambda b, h, r, i: (b, i, r * ocb + h))
    oshape = jax.ShapeDtypeStruct((batch, rows, dil * A_WIDTH), F32)
    o, lse = pl.pallas_call(
        _dilated_body,
        grid=(batch, A_HEADS // 2, dil, nblk),
        in_specs=[spec(kq, False), spec(kk, True), spec(kk, False),
                  spec(kv, True), spec(kv, False),
                  pl.BlockSpec((2, DIL_BLOCK, 2 * DIL_BLOCK), lambda b, h, r, i: (h, 0, 0))],
        out_specs=[ospec, ospec],
        out_shape=[oshape, oshape],
        compiler_params=pltpu.CompilerParams(
            dimension_semantics=("parallel", "parallel", "parallel", "arbitrary")),
        name=f"dilated_d{dil}",
    )(pv, pv, pv, pv, pv, bias_tile)
    return o.reshape(batch * seq, A_WIDTH), lse.reshape(batch * seq, A_WIDTH)


def dilated_bias_tile(rel_bias_a, window, dil):
    n = window // dil
    assert n == DIL_BLOCK
    bucket = _t5_bucket_table(window + 1)
    period = 4 * n
    u = np.arange(period)
    valid = u <= n
    w = jnp.where(jnp.asarray(valid)[None],
                  rel_bias_a.T[:, bucket[np.where(valid, n - u, 0) * dil]], NEG_INF).astype(F32)
    rep = jnp.broadcast_to(w[:, None, :], (A_HEADS, n, period))
    flat = rep.reshape(A_HEADS, n * period)[:, :n * (period - 1)]
    return flat.reshape(A_HEADS, n, period - 1)[:, :, :2 * n]


def _diff_body(q_ref, k_ref, v_ref, b_ref, lam_ref, g_ref, o_ref,
               m_scr, l_scr, acc_scr, *, n_tiles, lam_init):
    t = ATT_BLOCK
    qi = pl.program_id(2)
    q = q_ref[...] * QK_SCALE
    lane = lax.broadcasted_iota(I32, (1, LANES), 1)
    qa = [jnp.where((lane < DIFF_HALF) == (a == 0), q, jnp.zeros_like(q)) for a in range(2)]
    ones = jnp.ones((t, LANES), BF16)
    for a in range(2):
        m_scr[a] = jnp.full((t, LANES), NEG_INF, F32)
        l_scr[a] = jnp.zeros((t, LANES), F32)
        acc_scr[a] = jnp.zeros((t, DIFF_VDIM), F32)

    def step(j, masked):
        off = pl.multiple_of(j * t, t)
        ks = k_ref[pl.ds(off, t), :]
        vs = v_ref[pl.ds(off, t), :]
        bias = b_ref[jnp.minimum(qi - j, n_tiles - 1)]
        if masked:
            row = lax.broadcasted_iota(I32, (t, t), 0)
            col = lax.broadcasted_iota(I32, (t, t), 1)
            causal = row >= col
        for a in range(2):
            s = lax.dot_general(qa[a], ks, (((1,), (1,)), ((), ())),
                                preferred_element_type=F32) + bias
            if masked:
                s = jnp.where(causal, s, NEG_INF)
            m_prev = m_scr[a]
            m_new = jnp.maximum(m_prev, jnp.max(s, axis=-1, keepdims=True))
            alpha = jnp.exp(m_prev - m_new)
            pb = jnp.exp(s - jnp.concatenate([m_new] * (t // LANES), axis=1)).astype(BF16)
            l_scr[a] = alpha * l_scr[a] + jnp.dot(pb, ones, preferred_element_type=F32)
            acc_scr[a] = alpha * acc_scr[a] + jnp.dot(pb, vs, preferred_element_type=F32)
            m_scr[a] = m_new

    def loop_body(j, carry):
        step(j, False)
        return carry

    lax.fori_loop(0, qi, loop_body, 0)
    step(qi, True)

    lp = lam_ref[...]
    lam = (jnp.exp(jnp.sum(lp[0:1] * lp[1:2])) - jnp.exp(jnp.sum(lp[2:3] * lp[3:4]))
           + lam_init)
    o = acc_scr[0] / l_scr[0] - lam * (acc_scr[1] / l_scr[1])
    ms = jnp.mean(o * o, axis=-1, keepdims=True)
    y = o * lax.rsqrt(ms + RMS_EPS) * g_ref[...]
    o_ref[...] = (y * (1.0 - lam_init)).astype(o_ref.dtype)


def diff_attention(p, bias_tiles, lam_params, ln_g, lam_init, batch, seq):
    t = ATT_BLOCK
    n_tiles = bias_tiles.shape[1]
    cq = 3 * A_WIDTH // LANES
    ck = cq + B_QK_WIDTH // LANES
    cv = ck + B_QK_WIDTH // LANES
    return pl.pallas_call(
        functools.partial(_diff_body, n_tiles=n_tiles, lam_init=lam_init),
        grid=(batch, B_HEADS, seq // t),
        in_specs=[
            pl.BlockSpec((None, t, LANES), lambda b, h, i: (b, i, cq + h)),
            pl.BlockSpec((None, seq, LANES), lambda b, h, i: (b, 0, ck + h)),
            pl.BlockSpec((None, seq, LANES), lambda b, h, i: (b, 0, cv + h)),
            pl.BlockSpec((None, n_tiles, t, t), lambda b, h, i: (h, 0, 0, 0)),
            pl.BlockSpec((4, DIFF_HALF), lambda b, h, i: (0, 0)),
            pl.BlockSpec((1, DIFF_VDIM), lambda b, h, i: (0, 0)),
        ],
        out_specs=pl.BlockSpec((None, t, LANES), lambda b, h, i: (b, i, h)),
        out_shape=jax.ShapeDtypeStruct((batch, seq, B_V_WIDTH), BF16),
        scratch_shapes=[pltpu.VMEM((2, t, LANES), F32), pltpu.VMEM((2, t, LANES), F32),
                        pltpu.VMEM((2, t, DIFF_VDIM), F32)],
        compiler_params=pltpu.CompilerParams(
            dimension_semantics=("parallel", "parallel", "arbitrary")),
        name="diff_attention",
    )(p, p, p, bias_tiles, lam_params, ln_g.reshape(1, DIFF_VDIM))


def diff_bias_tiles(rel_bias_b, seq):
    t = ATT_BLOCK
    bucket = _t5_bucket_table(max(seq, 2 * MAX_DISTANCE) + 2 * t)
    sat = bucket[-1]
    d_sat = int(np.max(np.nonzero(bucket != sat)[0])) + 1
    n_full = (d_sat + t - 1 + t - 1) // t
    n_tiles = n_full + 1
    assert n_full * t - (t - 1) >= d_sat
    n = np.arange(2 * t)[None, :]
    base = np.arange(n_tiles)[:, None] * t
    dist = np.clip(np.where(n < t, base - n, base + 2 * t - n), 0, None)
    w = rel_bias_b.T[:, bucket[dist]].astype(F32)
    rep = jnp.broadcast_to(w[:, :, None, :], (B_HEADS, n_tiles, t, 2 * t))
    flat = rep.reshape(B_HEADS, n_tiles, 2 * t * t)[:, :, :t * (2 * t - 1)]
    return flat.reshape(B_HEADS, n_tiles, t, 2 * t - 1)[:, :, :, :t]


def _logsig_cumsum_body(f_ref, b_ref, c_ref, carry_scr):
    t = f_ref.shape[0]

    @pl.when(pl.program_id(1) == 0)
    def _():
        carry_scr[...] = jnp.zeros_like(carry_scr)

    x = f_ref[...] + b_ref[...]
    ls = jnp.minimum(x, 0.0) - jnp.log1p(jnp.exp(-jnp.abs(x)))
    row = lax.broadcasted_iota(I32, (t, t), 0)
    col = lax.broadcasted_iota(I32, (t, t), 1)
    tri = (row >= col).astype(F32)
    c = jnp.dot(tri, ls, preferred_element_type=F32,
                precision=lax.Precision.HIGHEST) + carry_scr[...]
    c_ref[...] = c
    carry_scr[...] = c[t - 1:t, :]


def logsig_cumsum(fg, b_f, batch, seq, t=512):
    return pl.pallas_call(
        _logsig_cumsum_body,
        grid=(batch, seq // t),
        in_specs=[pl.BlockSpec((None, t, LANES), lambda b, i: (b, i, 0)),
                  pl.BlockSpec((1, LANES), lambda b, i: (0, 0))],
        out_specs=pl.BlockSpec((None, t, LANES), lambda b, i: (b, i, 0)),
        out_shape=jax.ShapeDtypeStruct((batch, seq, LANES), F32),
        scratch_shapes=[pltpu.VMEM((1, LANES), F32)],
        compiler_params=pltpu.CompilerParams(
            dimension_semantics=("parallel", "arbitrary")),
        name="logsig_cumsum",
    )(fg, b_f)


def _fox_body(q_ref, k_ref, v_ref, cq_ref, ck_ref, o_ref, m_scr, acc_scr):
    t = ATT_BLOCK
    qi = pl.program_id(2)
    q = q_ref[...] * QK_SCALE
    lane = lax.broadcasted_iota(I32, (1, LANES), 1)
    own = [(lane < HEAD_DIM) == (a == 0) for a in range(2)]
    qa = [jnp.where(own[a], q, jnp.zeros_like(q)) for a in range(2)]
    cqb = [jnp.broadcast_to(cq_ref[a], (t, LANES)) for a in range(2)]
    for a in range(2):
        m_scr[a] = jnp.full((t, LANES), NEG_INF, F32)
        acc_scr[a] = jnp.zeros((t, LANES), F32)

    def step(j, masked):
        off = pl.multiple_of(j * t, t)
        ks = k_ref[pl.ds(off, t), :]
        vs = v_ref[pl.ds(off, t), :]
        if masked:
            row = lax.broadcasted_iota(I32, (t, t), 0)
            col = lax.broadcasted_iota(I32, (t, t), 1)
            causal = row >= col
        for a in range(2):
            s = lax.dot_general(qa[a], ks, (((1,), (1,)), ((), ())),
                                preferred_element_type=F32)
            s = s - ck_ref[a, :, pl.ds(off, t)]
            if masked:
                s = jnp.where(causal, s, NEG_INF)
            m_prev = m_scr[a]
            m_new = jnp.maximum(m_prev, jnp.max(s, axis=-1, keepdims=True) + cqb[a])
            alpha = jnp.exp(m_prev - m_new)
            shift = m_new - cqb[a]
            pb = jnp.exp(s - jnp.concatenate([shift] * (t // LANES), axis=1)).astype(BF16)
            v_aug = jnp.where(own[a], vs, jnp.ones_like(vs))
            acc_scr[a] = alpha * acc_scr[a] + jnp.dot(pb, v_aug, preferred_element_type=F32)
            m_scr[a] = m_new

    def loop_body(j, carry):
        step(j, False)
        return carry

    lax.fori_loop(0, qi, loop_body, 0)
    step(qi, True)
    r = [acc_scr[a] / pltpu.roll(acc_scr[a], HEAD_DIM, 1) for a in range(2)]
    o_ref[...] = jnp.where(own[0], r[0], r[1]).astype(o_ref.dtype)


def fox_attention(p, cq, ck, batch, seq):
    t = ATT_BLOCK
    nk = C_WIDTH // LANES
    return pl.pallas_call(
        _fox_body,
        grid=(batch, C_HEADS // 2, seq // t),
        in_specs=[
            pl.BlockSpec((None, t, LANES), lambda b, h, i: (b, i, h)),
            pl.BlockSpec((None, seq, LANES), lambda b, h, i: (b, 0, nk + h)),
            pl.BlockSpec((None, seq, LANES), lambda b, h, i: (b, 0, 2 * nk + h)),
            pl.BlockSpec((None, 2, t, 1), lambda b, h, i: (b, h, i, 0)),
            pl.BlockSpec((None, 2, 1, seq), lambda b, h, i: (b, h, 0, 0)),
        ],
        out_specs=pl.BlockSpec((None, t, LANES), lambda b, h, i: (b, i, h)),
        out_shape=jax.ShapeDtypeStruct((batch, seq, C_WIDTH), BF16),
        scratch_shapes=[pltpu.VMEM((2, t, LANES), F32), pltpu.VMEM((2, t, LANES), F32)],
        compiler_params=pltpu.CompilerParams(
            dimension_semantics=("parallel", "parallel", "arbitrary")),
        name="fox_attention",
    )(p, p, p, cq, ck)


def _even_out_body(o1, l1, o2, l2, o3, l3, ob_ref, w_ref, x_ref, out_ref, a_scr):
    @pl.when(pl.program_id(1) == 0)
    def _():
        a1, a2, a3 = l1[...], l2[...], l3[...]
        mx = jnp.maximum(jnp.maximum(a1, a2), a3)
        e1, e2, e3 = jnp.exp(a1 - mx), jnp.exp(a2 - mx), jnp.exp(a3 - mx)
        oa = (e1 * o1[...] + e2 * o2[...] + e3 * o3[...]) / (e1 + e2 + e3)
        a_scr[:, :A_WIDTH] = oa.astype(BF16)
        a_scr[:, A_WIDTH:] = ob_ref[...]

    out_ref[...] = x_ref[...] + jnp.dot(a_scr[...], w_ref[...], preferred_element_type=F32)


def even_out_proj(branches, ob, w, x, tm=512, tn=1024):
    m, d = x.shape
    half = pl.BlockSpec((tm, A_WIDTH), lambda i, j: (i, 0))
    flat = [a for pair in branches for a in pair]
    return pl.pallas_call(
        _even_out_body,
        grid=(m // tm, d // tn),
        in_specs=[half] * 6 + [
            pl.BlockSpec((tm, B_V_WIDTH), lambda i, j: (i, 0)),
            pl.BlockSpec((A_WIDTH + B_V_WIDTH, tn), lambda i, j: (0, j)),
            pl.BlockSpec((tm, tn), lambda i, j: (i, j))],
        out_specs=pl.BlockSpec((tm, tn), lambda i, j: (i, j)),
        out_shape=jax.ShapeDtypeStruct((m, d), F32),
        scratch_shapes=[pltpu.VMEM((tm, A_WIDTH + B_V_WIDTH), BF16)],
        compiler_params=pltpu.CompilerParams(
            dimension_semantics=("parallel", "arbitrary")),
        name="even_out_proj",
    )(*flat, ob, w, x)


def _matmul_res_body(a_ref, w_ref, x_ref, o_ref):
    o_ref[...] = x_ref[...] + jnp.dot(a_ref[...], w_ref[...], preferred_element_type=F32)


def matmul_residual(a, w, x, tm=1024, tn=1024):
    m, k = a.shape
    n = w.shape[1]
    return pl.pallas_call(
        _matmul_res_body,
        grid=(m // tm, n // tn),
        in_specs=[pl.BlockSpec((tm, k), lambda i, j: (i, 0)),
                  pl.BlockSpec((k, tn), lambda i, j: (0, j)),
                  pl.BlockSpec((tm, tn), lambda i, j: (i, j))],
        out_specs=pl.BlockSpec((tm, tn), lambda i, j: (i, j)),
        out_shape=jax.ShapeDtypeStruct((m, n), F32),
        compiler_params=pltpu.CompilerParams(
            dimension_semantics=("parallel", "parallel")),
        name="matmul_residual",
    )(a, w, x)


def _add_norm_body(x_ref, r_ref, g_ref, o_ref):
    x = x_ref[...] + r_ref[...]
    ms = jnp.mean(x * x, axis=-1, keepdims=True)
    o_ref[...] = x * lax.rsqrt(ms + RMS_EPS) * g_ref[...]


def add_norm(x, r, g, tm=512):
    m, d = x.shape
    row = pl.BlockSpec((tm, d), lambda i: (i, 0))
    return pl.pallas_call(
        _add_norm_body,
        grid=(m // tm,),
        in_specs=[row, row, pl.BlockSpec((1, d), lambda i: (0, 0))],
        out_specs=row,
        out_shape=jax.ShapeDtypeStruct((m, d), F32),
        compiler_params=pltpu.CompilerParams(dimension_semantics=("parallel",)),
        name="add_norm",
    )(x, r, g.reshape(1, d))


def _peer_candidates():
    pairs = [(a, b) for a in range(PEER_TOPK) for b in range(PEER_TOPK)
             if (a + 1) * (b + 1) <= PEER_TOPK]
    rows = -(-len(pairs) // 8) * 8
    sel = np.zeros((2, rows, PEER_TOPK), np.float32)
    cid = np.full((rows, 1), PEER_TOPK * PEER_TOPK, np.int32) + np.arange(rows, dtype=np.int32)[:, None]
    for r, (a, b) in enumerate(pairs):
        sel[0, r, a] = 1.0
        sel[1, r, b] = 1.0
        cid[r, 0] = a * PEER_TOPK + b
    return sel, cid


def _peer_topk_body(q_ref, sk_ref, sel_ref, cid_ref, idx_ref, gate_ref, ts_scr, ti_scr,
                    bs_scr, be_scr, gt_scr, it_scr):
    tm = q_ref.shape[0]
    neg_inf = jnp.float32(-jnp.inf)
    key_id = lax.broadcasted_iota(I32, (N_KEYS, tm), 0)

    def pair_body(pr, carry):
        off = pl.multiple_of(pr * PEER_KEY_HALF, PEER_KEY_HALF)
        sc = lax.dot_general(sk_ref[pr], q_ref[:, pl.ds(off, PEER_KEY_HALF)],
                             (((1,), (1,)), ((), ())), preferred_element_type=F32)

        def k_body(k, vals):
            m = jnp.max(vals, axis=0, keepdims=True)
            sel = jnp.min(jnp.where(vals == m, key_id, N_KEYS), axis=0, keepdims=True)
            ts_scr[pr, pl.ds(k, 1), :] = m
            ti_scr[pr, pl.ds(k, 1), :] = sel
            return jnp.where(key_id == sel, neg_inf, vals)

        lax.fori_loop(0, PEER_TOPK, k_body, sc)
        return carry

    lax.fori_loop(0, 2 * PEER_HEADS, pair_body, 0)

    cand_id = cid_ref[...]
    pad = jnp.where(cand_id < PEER_TOPK * PEER_TOPK, 0.0, neg_inf)
    sel_a, sel_b = sel_ref[0], sel_ref[1]

    def pick(x1, x2):
        hi = lax.Precision.HIGHEST
        return (jnp.dot(sel_a, x1, preferred_element_type=F32, precision=hi)
                + jnp.dot(sel_b, x2, preferred_element_type=F32, precision=hi))

    def head_body(h, carry):
        vals = pick(ts_scr[2 * h], ts_scr[2 * h + 1]) + pad
        eidx = pick((ti_scr[2 * h] * N_KEYS).astype(F32),
                    ti_scr[2 * h + 1].astype(F32)).astype(I32)

        def k_body(k, vals):
            m = jnp.max(vals, axis=0, keepdims=True)
            sel = jnp.min(jnp.where(vals == m, cand_id, PEER_TOPK * PEER_TOPK),
                          axis=0, keepdims=True)
            hit = cand_id == sel
            bs_scr[pl.ds(k, 1), :] = m
            be_scr[pl.ds(k, 1), :] = jnp.sum(jnp.where(hit, eidx, 0), axis=0, keepdims=True)
            return jnp.where(hit, neg_inf, vals)

        lax.fori_loop(0, PEER_TOPK, k_body, vals)
        bs = bs_scr[...]
        e = jnp.exp(bs - jnp.max(bs, axis=0, keepdims=True))
        row0 = pl.multiple_of(h * PEER_TOPK, PEER_TOPK)
        gt_scr[pl.ds(row0, PEER_TOPK), :] = e / jnp.sum(e, axis=0, keepdims=True)
        it_scr[pl.ds(row0, PEER_TOPK), :] = be_scr[...]
        return carry

    lax.fori_loop(0, PEER_HEADS, head_body, 0)
    gate_ref[...] = gt_scr[...].T
    idx_ref[...] = it_scr[...].T


def peer_topk(q, subkeys, tm=256):
    m = q.shape[0]
    n_sel = PEER_HEADS * PEER_TOPK
    sel, cid = _peer_candidates()
    out_spec = pl.BlockSpec((tm, n_sel), lambda i: (i, 0))
    return pl.pallas_call(
        _peer_topk_body,
        grid=(m // tm,),
        in_specs=[pl.BlockSpec((tm, q.shape[1]), lambda i: (i, 0)),
                  pl.BlockSpec(subkeys.shape, lambda i: (0, 0, 0)),
                  pl.BlockSpec(sel.shape, lambda i: (0, 0, 0)),
                  pl.BlockSpec(cid.shape, lambda i: (0, 0))],
        out_specs=[out_spec, out_spec],
        out_shape=[jax.ShapeDtypeStruct((m, n_sel), I32),
                   jax.ShapeDtypeStruct((m, n_sel), F32)],
        scratch_shapes=[pltpu.VMEM((2 * PEER_HEADS, PEER_TOPK, tm), F32),
                        pltpu.VMEM((2 * PEER_HEADS, PEER_TOPK, tm), I32),
                        pltpu.VMEM((PEER_TOPK, tm), F32),
                        pltpu.VMEM((PEER_TOPK, tm), I32),
                        pltpu.VMEM((n_sel, tm), F32),
                        pltpu.VMEM((n_sel, tm), I32)],
        compiler_params=pltpu.CompilerParams(dimension_semantics=("parallel",)),
        name="peer_topk",
    )(q, subkeys, jnp.asarray(sel), jnp.asarray(cid))


def _gelu_gate_body(a_ref, g_ref, o_ref):
    a = a_ref[...]
    o_ref[...] = g_ref[...] * (0.5 * a * (1.0 + lax.erf(a * (2.0 ** -0.5))))


def gelu_gate(act, gate, tm=2048):
    m, n = act.shape
    spec = pl.BlockSpec((tm, n), lambda i: (i, 0))
    return pl.pallas_call(
        _gelu_gate_body,
        grid=(m // tm,),
        in_specs=[spec, spec],
        out_specs=spec,
        out_shape=jax.ShapeDtypeStruct((m, n), F32),
        compiler_params=pltpu.CompilerParams(dimension_semantics=("parallel",)),
        name="gelu_gate",
    )(act, gate)


SC_TOK_CHUNK = 32
SC_RING = 8
SC_BF16_GROUP = 4
SC_FMT = plsc.PackFormat.INTERLEAVED


def _sc_worker_id():
    return lax.axis_index("s") * SC_CORES + lax.axis_index("c")


def pack_bf16_pairs(t):
    half = t.shape[-1] // 2
    bits = lax.bitcast_convert_type(t.astype(BF16).astype(F32), I32)
    return (bits[..., half:] & jnp.int32(-65536)) | lax.shift_right_logical(
        bits[..., :half], jnp.int32(16))


def _sc_row_pipeline(idx_v, table_hbm, rows_v, sems, n_items, groups, compute):
    def gather(item):
        tt, g = item // groups, item % groups
        ids = idx_v[tt, pl.ds(g * SC_LANES, SC_LANES)]
        slot = item % SC_RING
        return pltpu.make_async_copy(table_hbm.at[ids], rows_v.at[slot], sems.at[slot])

    for s in range(SC_RING - 1):
        gather(s).start()

    def item_body(item, carry):
        nxt = item + SC_RING - 1

        @pl.when(nxt < n_items)
        def _():
            gather(nxt).start()

        gather(item).wait()
        compute(item // groups, item % groups, item % SC_RING)
        return carry

    lax.fori_loop(0, n_items, item_body, 0)


def peer_expert_dots(hp, idx, up):
    m, dw = hp.shape
    n_sel = idx.shape[1]
    per_w = m // SC_WORKERS
    n_chunks = per_w // SC_TOK_CHUNK
    groups = n_sel // SC_LANES
    step = SC_BF16_GROUP * SC_LANES
    mesh = plsc.VectorSubcoreMesh(core_axis_name="c", subcore_axis_name="s")

    @functools.partial(
        pl.kernel, mesh=mesh,
        out_type=jax.ShapeDtypeStruct((m, n_sel), F32),
        scratch_types=[
            pltpu.VMEM((SC_TOK_CHUNK, n_sel), I32),
            pltpu.VMEM((SC_TOK_CHUNK, dw), I32),
            pltpu.VMEM((SC_TOK_CHUNK, n_sel), F32),
            pltpu.VMEM((SC_RING, SC_LANES, dw), I32),
            pltpu.VMEM((SC_LANES * SC_LANES,), F32),
            pltpu.SemaphoreType.DMA((SC_RING,)),
        ],
        compiler_params=pltpu.CompilerParams(needs_layout_passes=False),
        name="peer_expert_dots",
    )
    def k(h_hbm, idx_hbm, u_hbm, act_hbm, idx_v, h_v, act_v, rows_v, part_v, sems):
        base = _sc_worker_id() * per_w
        lane = lax.broadcasted_iota(I32, (SC_LANES,), 0)

        def compute(tt, g, slot):
            def grp_body(q, accs):
                off = pl.multiple_of(q * step, step)
                xs = [plsc.bitcast(h_v[tt, pl.ds(off + c * SC_LANES, SC_LANES)], BF16)
                      for c in range(SC_BF16_GROUP)]
                new = []
                for e in range(SC_LANES):
                    s = None
                    for c in range(SC_BF16_GROUP):
                        p = plsc.bitcast(
                            rows_v[slot, e, pl.ds(off + c * SC_LANES, SC_LANES)], BF16) * xs[c]
                        s = p if s is None else s + p
                    lo, hi = plsc.unpack(s, format=SC_FMT)
                    new.append(accs[e] + (lo + hi))
                return tuple(new)

            accs = lax.fori_loop(
                0, dw // step, grp_body,
                tuple(jnp.zeros((SC_LANES,), F32) for _ in range(SC_LANES)))
            for e in range(SC_LANES):
                part_v[pl.ds(e * SC_LANES, SC_LANES)] = accs[e]
            tot = jnp.zeros((SC_LANES,), F32)
            for l in range(SC_LANES):
                tot = tot + plsc.load_gather(part_v, [lane * SC_LANES + l])
            act_v[tt, pl.ds(g * SC_LANES, SC_LANES)] = tot

        def chunk_body(c, carry):
            t0 = base + c * SC_TOK_CHUNK
            pltpu.sync_copy(idx_hbm.at[pl.ds(t0, SC_TOK_CHUNK)], idx_v)
            pltpu.sync_copy(h_hbm.at[pl.ds(t0, SC_TOK_CHUNK)], h_v)
            _sc_row_pipeline(idx_v, u_hbm, rows_v, sems, SC_TOK_CHUNK * groups, groups, compute)
            pltpu.sync_copy(act_v, act_hbm.at[pl.ds(t0, SC_TOK_CHUNK)])
            return carry

        lax.fori_loop(0, n_chunks, chunk_body, 0)

    return k(hp, idx, up)


def peer_expert_combine(w, idx, vp):
    m, n_sel = w.shape
    dw = vp.shape[1]
    d = 2 * dw
    per_w = m // SC_WORKERS
    n_chunks = per_w // SC_TOK_CHUNK
    n_vec = d // SC_LANES
    groups = n_sel // SC_LANES
    mesh = plsc.VectorSubcoreMesh(core_axis_name="c", subcore_axis_name="s")

    @functools.partial(
        pl.kernel, mesh=mesh,
        out_type=jax.ShapeDtypeStruct((m, d), F32),
        scratch_types=[
            pltpu.VMEM((SC_TOK_CHUNK, n_sel), I32),
            pltpu.VMEM((SC_TOK_CHUNK, n_sel), F32),
            pltpu.VMEM((SC_TOK_CHUNK, d), F32),
            pltpu.VMEM((SC_RING, SC_LANES, dw), I32),
            pltpu.SemaphoreType.DMA((SC_RING,)),
        ],
        compiler_params=pltpu.CompilerParams(needs_layout_passes=False),
        name="peer_expert_combine",
    )
    def k(w_hbm, idx_hbm, v_hbm, out_hbm, idx_v, w_v, out_v, rows_v, sems):
        base = _sc_worker_id() * per_w

        def compute(tt, g, slot):
            splat = []
            for e in range(SC_LANES):
                s = plsc.load_gather(w_v, [jnp.full((SC_LANES,), tt, I32),
                                           jnp.full((SC_LANES,), g * SC_LANES + e, I32)])
                splat.append(plsc.pack(s, s, format=SC_FMT))

            @plsc.parallel_loop(0, dw // SC_LANES)
            def _(j):
                off = pl.multiple_of(j * SC_LANES, SC_LANES)
                acc_lo = out_v[tt, pl.ds(off, SC_LANES)]
                acc_hi = out_v[tt, pl.ds(dw + off, SC_LANES)]
                for e0 in range(0, SC_LANES, SC_BF16_GROUP):
                    s = None
                    for e in range(e0, e0 + SC_BF16_GROUP):
                        p = plsc.bitcast(rows_v[slot, e, pl.ds(off, SC_LANES)], BF16) * splat[e]
                        s = p if s is None else s + p
                    lo, hi = plsc.unpack(s, format=SC_FMT)
                    acc_lo = acc_lo + lo
                    acc_hi = acc_hi + hi
                out_v[tt, pl.ds(off, SC_LANES)] = acc_lo
                out_v[tt, pl.ds(dw + off, SC_LANES)] = acc_hi

        def chunk_body(c, carry):
            t0 = base + c * SC_TOK_CHUNK
            pltpu.sync_copy(idx_hbm.at[pl.ds(t0, SC_TOK_CHUNK)], idx_v)
            pltpu.sync_copy(w_hbm.at[pl.ds(t0, SC_TOK_CHUNK)], w_v)

            def zero_body(z, carry2):
                tt, j = z // n_vec, z % n_vec
                out_v[tt, pl.ds(pl.multiple_of(j * SC_LANES, SC_LANES), SC_LANES)] = (
                    jnp.zeros((SC_LANES,), F32))
                return carry2

            lax.fori_loop(0, SC_TOK_CHUNK * n_vec, zero_body, 0)
            _sc_row_pipeline(idx_v, v_hbm, rows_v, sems, SC_TOK_CHUNK * groups, groups, compute)
            pltpu.sync_copy(out_v, out_hbm.at[pl.ds(t0, SC_TOK_CHUNK)])
            return carry

        lax.fori_loop(0, n_chunks, chunk_body, 0)

    return k(w, idx, vp)


def kernel(x, norm_mix_g, norm_ffn_g, final_norm_g, rel_bias, even_w_in, even_w_out,
           diff_lambda, diff_ln_g, odd_w_in, odd_b_f, odd_w_out, peer_wq, peer_subkeys,
           peer_u, peer_v):
    batch, seq, d = x.shape

    dil_tiles = [dilated_bias_tile(rel_bias[:, :A_HEADS], w, dl) for w, dl in DILATED_BRANCHES]
    diff_tiles = diff_bias_tiles(rel_bias[:, A_HEADS:], seq)
    lam_init = 0.8 - 0.6 * math.exp(-0.3 * 0)
    even_in, even_out = even_w_in[0].astype(BF16), even_w_out[0].astype(BF16)
    w_in = odd_w_in[0]
    odd_in, odd_out = w_in[:, :3 * C_WIDTH].astype(BF16), odd_w_out[0].astype(BF16)
    w_gate = jnp.pad(w_in[:, 3 * C_WIDTH:], ((0, 0), (0, LANES - C_HEADS)))
    b_f = jnp.pad(odd_b_f[0], (0, LANES - C_HEADS)).reshape(1, LANES)
    peer = [dict(wq=peer_wq[l].astype(BF16),
                 sk=peer_subkeys[l].reshape(2 * PEER_HEADS, N_KEYS, PEER_KEY_HALF).astype(BF16),
                 up=pack_bf16_pairs(peer_u[l]), vp=pack_bf16_pairs(peer_v[l]))
            for l in range(2)]

    bg = batch // BATCH_GROUPS
    m = bg * seq
    groups = [dict(x=x[g * bg:(g + 1) * bg].reshape(m, d)) for g in range(BATCH_GROUPS)]

    def peer_select(st, layer, after=None):
        q, hp = norm_matmul(st["x"], norm_ffn_g[layer], peer[layer]["wq"], want_h="packed")
        idx, gate = peer_topk(q, peer[layer]["sk"])
        if after is not None and after["st"] is not st:
            after["st"]["peer"], idx = lax.optimization_barrier((after["st"]["peer"], idx))
        return dict(st=st, layer=layer, idx=idx, gate=gate,
                    act=peer_expert_dots(hp, idx, peer[layer]["up"]))

    def peer_combine(rec):
        rec["w"] = gelu_gate(rec["act"], rec["gate"])
        rec["st"]["peer"] = peer_expert_combine(rec["w"], rec["idx"], peer[rec["layer"]]["vp"])

    def even_layer(st):
        (p,) = norm_matmul(st["x"], norm_mix_g[0], even_in)
        branches = [dilated_branch(p, tile, dl, bg, seq)
                    for tile, (_, dl) in zip(dil_tiles, DILATED_BRANCHES)]
        ob = diff_attention(p.reshape(bg, seq, -1), diff_tiles, diff_lambda[0], diff_ln_g[0],
                            lam_init, bg, seq)
        st["x"] = even_out_proj(branches, ob.reshape(m, B_V_WIDTH), even_out, st["x"])

    def odd_layer(st):
        p, st["x"], h = norm_matmul(st["x"], norm_mix_g[1], odd_in, res=st["peer"], want_h="f32")
        fg = gate_matmul(h, w_gate)
        c = logsig_cumsum(fg.reshape(bg, seq, LANES), b_f, bg, seq)
        ct = c[:, :, :C_HEADS].transpose(0, 2, 1)
        o = fox_attention(p.reshape(bg, seq, -1), ct[:, :, :, None], ct[:, :, None, :], bg, seq)
        st["x"] = matmul_residual(o.reshape(m, C_WIDTH), odd_out, st["x"])

    stages = [(even_layer, st, 0) for st in groups] + [(odd_layer, st, 1) for st in groups]
    recs = []
    for k, (mixer, st, layer) in enumerate(stages):
        mixer(st)
        if k >= 1:
            prev = recs[k - 1]
            prev["act"], st["x"] = lax.optimization_barrier((prev["act"], st["x"]))
            peer_combine(prev)
            prev["w"], st["x"] = lax.optimization_barrier((prev["w"], st["x"]))
        recs.append(peer_select(st, layer, after=recs[k - 1] if k >= 1 else None))
    peer_combine(recs[-1])
    outs = [add_norm(st["x"], st["peer"], final_norm_g).reshape(bg, seq, d) for st in groups]
    return jnp.concatenate(outs, axis=0)


def _gate_matmul_body(h_ref, w_ref, o_ref):
    o_ref[...] = jnp.dot(h_ref[...], w_ref[...], preferred_element_type=F32,
                         precision=lax.Precision.HIGHEST)


def gate_matmul(h, w, tm=512):
    m, d = h.shape
    n = w.shape[1]
    return pl.pallas_call(
        _gate_matmul_body,
        grid=(m // tm,),
        in_specs=[pl.BlockSpec((tm, d), lambda i: (i, 0)),
                  pl.BlockSpec((d, n), lambda i: (0, 0))],
        out_specs=pl.BlockSpec((tm, n), lambda i: (i, 0)),
        out_shape=jax.ShapeDtypeStruct((m, n), F32),
        compiler_params=pltpu.CompilerParams(dimension_semantics=("parallel",)),
        name="gate_matmul",
    )(h, w)
```

```python
import functools
import math

import numpy as np
import jax
import jax.numpy as jnp
from jax import lax
from jax.experimental import pallas as pl
from jax.experimental.pallas import tpu as pltpu
from jax.experimental.pallas import tpu_sc as plsc

F32 = jnp.float32
BF16 = jnp.bfloat16
I32 = jnp.int32

D_MODEL = 1024
HEAD_DIM = 64
A_HEADS = 8
DILATED_BRANCHES = ((128, 1), (512, 4), (2048, 16))
DIFF_HALF = 64
DIFF_VDIM = 128
B_HEADS = 4
C_HEADS = 16
N_BUCKETS = 32
MAX_DISTANCE = 2048
PEER_HEADS = 8
N_KEYS = 128
PEER_TOPK = 16
PEER_KEY_HALF = 128
RMS_EPS = 1e-6
NEG_INF = -1e30
A_WIDTH = A_HEADS * HEAD_DIM
B_QK_WIDTH = B_HEADS * 2 * DIFF_HALF
B_V_WIDTH = B_HEADS * DIFF_VDIM
C_WIDTH = C_HEADS * HEAD_DIM
QK_SCALE = 0.125
LANES = 128
DIL_BLOCK = 128
ATT_BLOCK = 512
BATCH_GROUPS = 4

SC_CORES = 2
SC_SUBCORES = 16
SC_LANES = 16
SC_WORKERS = SC_CORES * SC_SUBCORES


def _t5_bucket_table(n):
    max_exact = N_BUCKETS // 2
    d = np.arange(n)
    df = np.maximum(d, 1).astype(np.float32)
    large = max_exact + (
        np.log(df / np.float32(max_exact)) / np.float32(math.log(MAX_DISTANCE / max_exact))
        * np.float32(N_BUCKETS - max_exact)).astype(np.int32)
    large = np.minimum(large, N_BUCKETS - 1)
    return np.where(d < max_exact, d, large).astype(np.int32)


def _norm_matmul_body(*refs, has_res, want_h):
    it = iter(refs)
    x_ref = next(it)
    r_ref = next(it) if has_res else None
    g_ref = next(it)
    w_ref = next(it)
    o_ref = next(it)
    xs_ref = next(it) if has_res else None
    hout_ref = next(it) if want_h else None
    h_scr = next(it)

    @pl.when(pl.program_id(1) == 0)
    def _():
        x = x_ref[...]
        if has_res:
            x = x + r_ref[...]
            xs_ref[...] = x
        ms = jnp.mean(x * x, axis=-1, keepdims=True)
        h = x * lax.rsqrt(ms + RMS_EPS) * g_ref[...]
        if want_h == "f32":
            hout_ref[...] = h
        elif want_h == "packed":
            hout_ref[...] = pack_bf16_pairs(h)
        h_scr[...] = h.astype(BF16)

    o_ref[...] = jnp.dot(h_scr[...], w_ref[...],
                         preferred_element_type=F32).astype(o_ref.dtype)


def norm_matmul(x, g, w, *, res=None, want_h=None, out_dtype=BF16, tm=1024, tn=1024):
    m, d = x.shape
    n = w.shape[1]
    tn = min(tn, n)
    row = pl.BlockSpec((tm, d), lambda i, j: (i, 0))
    in_specs = [row] + ([row] if res is not None else []) + [
        pl.BlockSpec((1, d), lambda i, j: (0, 0)),
        pl.BlockSpec((d, tn), lambda i, j: (0, j))]
    out_specs = [pl.BlockSpec((tm, tn), lambda i, j: (i, j))]
    out_shape = [jax.ShapeDtypeStruct((m, n), out_dtype)]
    if res is not None:
        out_specs.append(row)
        out_shape.append(jax.ShapeDtypeStruct((m, d), F32))
    if want_h == "f32":
        out_specs.append(row)
        out_shape.append(jax.ShapeDtypeStruct((m, d), F32))
    elif want_h == "packed":
        out_specs.append(pl.BlockSpec((tm, d // 2), lambda i, j: (i, 0)))
        out_shape.append(jax.ShapeDtypeStruct((m, d // 2), I32))
    args =[x] + ([res] if res is not None else []) + [g.reshape(1, d), w]
    return pl.pallas_call(
        functools.partial(_norm_matmul_body, has_res=res is not None, want_h=want_h),
        grid=(m // tm, n // tn),
        in_specs=in_specs,
        out_specs=out_specs,
        out_shape=out_shape,
        scratch_shapes=[pltpu.VMEM((tm, d), BF16)],
        compiler_params=pltpu.CompilerParams(
            dimension_semantics=("parallel", "arbitrary")),
        name="norm_matmul",
    )(*args)


def _dilated_body(q_ref, kp_ref, kc_ref, vp_ref, vc_ref, b_ref, o_ref, lse_ref):
    i = pl.program_id(3)
    q = q_ref[...] * QK_SCALE
    k = jnp.concatenate([kp_ref[...], kc_ref[...]], axis=0)
    v = jnp.concatenate([vp_ref[...], vc_ref[...]], axis=0)
    col = lax.broadcasted_iota(I32, (DIL_BLOCK, 2 * DIL_BLOCK), 1)
    has_prev = jnp.logical_or(col >= DIL_BLOCK, i > 0)
    outs, lses = [], []
    for hh in range(2):
        sl = slice(hh * HEAD_DIM, (hh + 1) * HEAD_DIM)
        s = lax.dot_general(q[:, sl], k[:, sl], (((1,), (1,)), ((), ())),
                            preferred_element_type=F32)
        s = jnp.where(has_prev, s + b_ref[hh], NEG_INF)
        m = jnp.max(s, axis=-1, keepdims=True)
        p = jnp.exp(s - m)
        l = jnp.sum(p, axis=-1, keepdims=True)
        o = jnp.dot(p.astype(BF16), v[:, sl], preferred_element_type=F32) / l
        outs.append(o)
        lses.append(jnp.broadcast_to(m + jnp.log(l), (DIL_BLOCK, HEAD_DIM)))
    o_ref[...] = jnp.concatenate(outs, axis=1)
    lse_ref[...] = jnp.concatenate(lses, axis=1)


def dilated_branch(p, bias_tile, dil, batch, seq):
    if dil > 1:
        p = p[:, :3 * A_WIDTH]
    n_cols = p.shape[1]
    cb = n_cols // LANES
    rows = seq // dil
    nblk = rows // DIL_BLOCK
    pv = p.reshape(batch, rows, dil * n_cols)
    kq, kk, kv = 0, A_WIDTH // LANES, 2 * A_WIDTH // LANES
    blk = (None, DIL_BLOCK, LANES)

    def spec(col0, prev):
        if prev:
            return pl.BlockSpec(blk, lambda b, h, r, i: (b, jnp.maximum(i - 1, 0), r * cb + col0 + h))
        return pl.BlockSpec(blk, lambda b, h, r, i: (b, i, r * cb + col0 + h))

    ocb = A_WIDTH // LANES
    ospec = pl.BlockSpec(blk, lambda b, h, r, i: (b, i, r * ocb + h))
    oshape = jax.ShapeDtypeStruct((batch, rows, dil * A_WIDTH), F32)
    o, lse = pl.pallas_call(
        _dilated_body,
        grid=(batch, A_HEADS // 2, dil, nblk),
        in_specs=[spec(kq, False), spec(kk, True), spec(kk, False),
                  spec(kv, True), spec(kv, False),
                  pl.BlockSpec((2, DIL_BLOCK, 2 * DIL_BLOCK), lambda b, h, r, i: (h, 0, 0))],
        out_specs=[ospec, ospec],
        out_shape=[oshape, oshape],
        compiler_params=pltpu.CompilerParams(
            dimension_semantics=("parallel", "parallel", "parallel", "arbitrary")),
        name=f"dilated_d{dil}",
    )(pv, pv, pv, pv, pv, bias_tile)
    return o.reshape(batch * seq, A_WIDTH), lse.reshape(batch * seq, A_WIDTH)


def dilated_bias_tile(rel_bias_a, window, dil):
    n = window // dil
    assert n == DIL_BLOCK
    bucket = _t5_bucket_table(window + 1)
    period = 4 * n
    u = np.arange(period)
    valid = u <= n
    w = jnp.where(jnp.asarray(valid)[None],
                  rel_bias_a.T[:, bucket[np.where(valid, n - u, 0) * dil]], NEG_INF).astype(F32)
    rep = jnp.broadcast_to(w[:, None, :], (A_HEADS, n, period))
    flat = rep.reshape(A_HEADS, n * period)[:, :n * (period - 1)]
    return flat.reshape(A_HEADS, n, period - 1)[:, :, :2 * n]


def _diff_body(q_ref, k_ref, v_ref, b_ref, lam_ref, g_ref, o_ref,
               m_scr, l_scr, acc_scr, *, n_tiles, lam_init):
    t = ATT_BLOCK
    qi = pl.program_id(2)
    q = q_ref[...] * QK_SCALE
    lane = lax.broadcasted_iota(I32, (1, LANES), 1)
    qa = [jnp.where((lane < DIFF_HALF) == (a == 0), q, jnp.zeros_like(q)) for a in range(2)]
    ones = jnp.ones((t, LANES), BF16)
    for a in range(2):
        m_scr[a] = jnp.full((t, LANES), NEG_INF, F32)
        l_scr[a] = jnp.zeros((t, LANES), F32)
        acc_scr[a] = jnp.zeros((t, DIFF_VDIM), F32)

    def step(j, masked):
        off = pl.multiple_of(j * t, t)
        ks = k_ref[pl.ds(off, t), :]
        vs = v_ref[pl.ds(off, t), :]
        bias = b_ref[jnp.minimum(qi - j, n_tiles - 1)]
        if masked:
            row = lax.broadcasted_iota(I32, (t, t), 0)
            col = lax.broadcasted_iota(I32, (t, t), 1)
            causal = row >= col
        for a in range(2):
            s = lax.dot_general(qa[a], ks, (((1,), (1,)), ((), ())),
                                preferred_element_type=F32) + bias
            if masked:
                s = jnp.where(causal, s, NEG_INF)
            m_prev = m_scr[a]
            m_new = jnp.maximum(m_prev, jnp.max(s, axis=-1, keepdims=True))
            alpha = jnp.exp(m_prev - m_new)
            pb = jnp.exp(s - jnp.concatenate([m_new] * (t // LANES), axis=1)).astype(BF16)
            l_scr[a] = alpha * l_scr[a] + jnp.dot(pb, ones, preferred_element_type=F32)
            acc_scr[a] = alpha * acc_scr[a] + jnp.dot(pb, vs, preferred_element_type=F32)
            m_scr[a] = m_new

    def loop_body(j, carry):
        step(j, False)
        return carry

    lax.fori_loop(0, qi, loop_body, 0)
    step(qi, True)

    lp = lam_ref[...]
    lam = (jnp.exp(jnp.sum(lp[0:1] * lp[1:2])) - jnp.exp(jnp.sum(lp[2:3] * lp[3:4]))
           + lam_init)
    o = acc_scr[0] / l_scr[0] - lam * (acc_scr[1] / l_scr[1])
    ms = jnp.mean(o * o, axis=-1, keepdims=True)
    y = o * lax.rsqrt(ms + RMS_EPS) * g_ref[...]
    o_ref[...] = (y * (1.0 - lam_init)).astype(o_ref.dtype)


def diff_attention(p, bias_tiles, lam_params, ln_g, lam_init, batch, seq):
    t = ATT_BLOCK
    n_tiles = bias_tiles.shape[1]
    cq = 3 * A_WIDTH // LANES
    ck = cq + B_QK_WIDTH // LANES
    cv = ck + B_QK_WIDTH // LANES
    return pl.pallas_call(
        functools.partial(_diff_body, n_tiles=n_tiles, lam_init=lam_init),
        grid=(batch, B_HEADS, seq // t),
        in_specs=[
            pl.BlockSpec((None, t, LANES), lambda b, h, i: (b, i, cq + h)),
            pl.BlockSpec((None, seq, LANES), lambda b, h, i: (b, 0, ck + h)),
            pl.BlockSpec((None, seq, LANES), lambda b, h, i: (b, 0, cv + h)),
            pl.BlockSpec((None, n_tiles, t, t), lambda b, h, i: (h, 0, 0, 0)),
            pl.BlockSpec((4, DIFF_HALF), lambda b, h, i: (0, 0)),
            pl.BlockSpec((1, DIFF_VDIM), lambda b, h, i: (0, 0)),
        ],
        out_specs=pl.BlockSpec((None, t, LANES), lambda b, h, i: (b, i, h)),
        out_shape=jax.ShapeDtypeStruct((batch, seq, B_V_WIDTH), BF16),
        scratch_shapes=[pltpu.VMEM((2, t, LANES), F32), pltpu.VMEM((2, t, LANES), F32),
                        pltpu.VMEM((2, t, DIFF_VDIM), F32)],
        compiler_params=pltpu.CompilerParams(
            dimension_semantics=("parallel", "parallel", "arbitrary")),
        name="diff_attention",
    )(p, p, p, bias_tiles, lam_params, ln_g.reshape(1, DIFF_VDIM))


def diff_bias_tiles(rel_bias_b, seq):
    t = ATT_BLOCK
    bucket = _t5_bucket_table(max(seq, 2 * MAX_DISTANCE) + 2 * t)
    sat = bucket[-1]
    d_sat = int(np.max(np.nonzero(bucket != sat)[0])) + 1
    n_full = (d_sat + t - 1 + t - 1) // t
    n_tiles = n_full + 1
    assert n_full * t - (t - 1) >= d_sat
    n = np.arange(2 * t)[None, :]
    base = np.arange(n_tiles)[:, None] * t
    dist = np.clip(np.where(n < t, base - n, base + 2 * t - n), 0, None)
    w = rel_bias_b.T[:, bucket[dist]].astype(F32)
    rep = jnp.broadcast_to(w[:, :, None, :], (B_HEADS, n_tiles, t, 2 * t))
    flat = rep.reshape(B_HEADS, n_tiles, 2 * t * t)[:, :, :t * (2 * t - 1)]
    return flat.reshape(B_HEADS, n_tiles, t, 2 * t - 1)[:, :, :, :t]


def _logsig_cumsum_body(f_ref, b_ref, c_ref, carry_scr):
    t = f_ref.shape[0]

    @pl.when(pl.program_id(1) == 0)
    def _():
        carry_scr[...] = jnp.zeros_like(carry_scr)

    x = f_ref[...] + b_ref[...]
    ls = jnp.minimum(x, 0.0) - jnp.log1p(jnp.exp(-jnp.abs(x)))
    row = lax.broadcasted_iota(I32, (t, t), 0)
    col = lax.broadcasted_iota(I32, (t, t), 1)
    tri = (row >= col).astype(F32)
    c = jnp.dot(tri, ls, preferred_element_type=F32,
                precision=lax.Precision.HIGHEST) + carry_scr[...]
    c_ref[...] = c
    carry_scr[...] = c[t - 1:t, :]


def logsig_cumsum(fg, b_f, batch, seq, t=512):
    return pl.pallas_call(
        _logsig_cumsum_body,
        grid=(batch, seq // t),
        in_specs=[pl.BlockSpec((None, t, LANES), lambda b, i: (b, i, 0)),
                  pl.BlockSpec((1, LANES), lambda b, i: (0, 0))],
        out_specs=pl.BlockSpec((None, t, LANES), lambda b, i: (b, i, 0)),
        out_shape=jax.ShapeDtypeStruct((batch, seq, LANES), F32),
        scratch_shapes=[pltpu.VMEM((1, LANES), F32)],
        compiler_params=pltpu.CompilerParams(
            dimension_semantics=("parallel", "arbitrary")),
        name="logsig_cumsum",
    )(fg, b_f)


def _fox_body(q_ref, k_ref, v_ref, cq_ref, ck_ref, o_ref, m_scr, acc_scr):
    t = ATT_BLOCK
    qi = pl.program_id(2)
    q = q_ref[...] * QK_SCALE
    lane = lax.broadcasted_iota(I32, (1, LANES), 1)
    own = [(lane < HEAD_DIM) == (a == 0) for a in range(2)]
    qa = [jnp.where(own[a], q, jnp.zeros_like(q)) for a in range(2)]
    cqb = [jnp.broadcast_to(cq_ref[a], (t, LANES)) for a in range(2)]
    for a in range(2):
        m_scr[a] = jnp.full((t, LANES), NEG_INF, F32)
        acc_scr[a] = jnp.zeros((t, LANES), F32)

    def step(j, masked):
        off = pl.multiple_of(j * t, t)
        ks = k_ref[pl.ds(off, t), :]
        vs = v_ref[pl.ds(off, t), :]
        if masked:
            row = lax.broadcasted_iota(I32, (t, t), 0)
            col = lax.broadcasted_iota(I32, (t, t), 1)
            causal = row >= col
        for a in range(2):
            s = lax.dot_general(qa[a], ks, (((1,), (1,)), ((), ())),
                                preferred_element_type=F32)
            s = s - ck_ref[a, :, pl.ds(off, t)]
            if masked:
                s = jnp.where(causal, s, NEG_INF)
            m_prev = m_scr[a]
            m_new = jnp.maximum(m_prev, jnp.max(s, axis=-1, keepdims=True) + cqb[a])
            alpha = jnp.exp(m_prev - m_new)
            shift = m_new - cqb[a]
            pb = jnp.exp(s - jnp.concatenate([shift] * (t // LANES), axis=1)).astype(BF16)
            v_aug = jnp.where(own[a], vs, jnp.ones_like(vs))
            acc_scr[a] = alpha * acc_scr[a] + jnp.dot(pb, v_aug, preferred_element_type=F32)
            m_scr[a] = m_new

    def loop_body(j, carry):
        step(j, False)
        return carry

    lax.fori_loop(0, qi, loop_body, 0)
    step(qi, True)
    r = [acc_scr[a] / pltpu.roll(acc_scr[a], HEAD_DIM, 1) for a in range(2)]
    o_ref[...] = jnp.where(own[0], r[0], r[1]).astype(o_ref.dtype)


def fox_attention(p, cq, ck, batch, seq):
    t = ATT_BLOCK
    nk = C_WIDTH // LANES
    return pl.pallas_call(
        _fox_body,
        grid=(batch, C_HEADS // 2, seq // t),
        in_specs=[
            pl.BlockSpec((None, t, LANES), lambda b, h, i: (b, i, h)),
            pl.BlockSpec((None, seq, LANES), lambda b, h, i: (b, 0, nk + h)),
            pl.BlockSpec((None, seq, LANES), lambda b, h, i: (b, 0, 2 * nk + h)),
            pl.BlockSpec((None, 2, t, 1), lambda b, h, i: (b, h, i, 0)),
            pl.BlockSpec((None, 2, 1, seq), lambda b, h, i: (b, h, 0, 0)),
        ],
        out_specs=pl.BlockSpec((None, t, LANES), lambda b, h, i: (b, i, h)),
        out_shape=jax.ShapeDtypeStruct((batch, seq, C_WIDTH), BF16),
        scratch_shapes=[pltpu.VMEM((2, t, LANES), F32), pltpu.VMEM((2, t, LANES), F32)],
        compiler_params=pltpu.CompilerParams(
            dimension_semantics=("parallel", "parallel", "arbitrary")),
        name="fox_attention",
    )(p, p, p, cq, ck)


def _even_out_body(o1, l1, o2, l2, o3, l3, ob_ref, w_ref, x_ref, out_ref, a_scr):
    @pl.when(pl.program_id(1) == 0)
    def _():
        a1, a2, a3 = l1[...], l2[...], l3[...]
        mx = jnp.maximum(jnp.maximum(a1, a2), a3)
        e1, e2, e3 = jnp.exp(a1 - mx), jnp.exp(a2 - mx), jnp.exp(a3 - mx)
        oa = (e1 * o1[...] + e2 * o2[...] + e3 * o3[...]) / (e1 + e2 + e3)
        a_scr[:, :A_WIDTH] = oa.astype(BF16)
        a_scr[:, A_WIDTH:] = ob_ref[...]

    out_ref[...] = x_ref[...] + jnp.dot(a_scr[...], w_ref[...], preferred_element_type=F32)


def even_out_proj(branches, ob, w, x, tm=512, tn=1024):
    m, d = x.shape
    half = pl.BlockSpec((tm, A_WIDTH), lambda i, j: (i, 0))
    flat = [a for pair in branches for a in pair]
    return pl.pallas_call(
        _even_out_body,
        grid=(m // tm, d // tn),
        in_specs=[half] * 6 + [
            pl.BlockSpec((tm, B_V_WIDTH), lambda i, j: (i, 0)),
            pl.BlockSpec((A_WIDTH + B_V_WIDTH, tn), lambda i, j: (0, j)),
            pl.BlockSpec((tm, tn), lambda i, j: (i, j))],
        out_specs=pl.BlockSpec((tm, tn), lambda i, j: (i, j)),
        out_shape=jax.ShapeDtypeStruct((m, d), F32),
        scratch_shapes=[pltpu.VMEM((tm, A_WIDTH + B_V_WIDTH), BF16)],
        compiler_params=pltpu.CompilerParams(
            dimension_semantics=("parallel", "arbitrary")),
        name="even_out_proj",
    )(*flat, ob, w, x)


def _matmul_res_body(a_ref, w_ref, x_ref, o_ref):
    o_ref[...] = x_ref[...] + jnp.dot(a_ref[...], w_ref[...], preferred_element_type=F32)


def matmul_residual(a, w, x, tm=1024, tn=1024):
    m, k = a.shape
    n = w.shape[1]
    return pl.pallas_call(
        _matmul_res_body,
        grid=(m // tm, n // tn),
        in_specs=[pl.BlockSpec((tm, k), lambda i, j: (i, 0)),
                  pl.BlockSpec((k, tn), lambda i, j: (0, j)),
                  pl.BlockSpec((tm, tn), lambda i, j: (i, j))],
        out_specs=pl.BlockSpec((tm, tn), lambda i, j: (i, j)),
        out_shape=jax.ShapeDtypeStruct((m, n), F32),
        compiler_params=pltpu.CompilerParams(
            dimension_semantics=("parallel", "parallel")),
        name="matmul_residual",
    )(a, w, x)


def _add_norm_body(x_ref, r_ref, g_ref, o_ref):
    x = x_ref[...] + r_ref[...]
    ms = jnp.mean(x * x, axis=-1, keepdims=True)
    o_ref[...] = x * lax.rsqrt(ms + RMS_EPS) * g_ref[...]


def add_norm(x, r, g, tm=512):
    m, d = x.shape
    row = pl.BlockSpec((tm, d), lambda i: (i, 0))
    return pl.pallas_call(
        _add_norm_body,
        grid=(m // tm,),
        in_specs=[row, row, pl.BlockSpec((1, d), lambda i: (0, 0))],
        out_specs=row,
        out_shape=jax.ShapeDtypeStruct((m, d), F32),
        compiler_params=pltpu.CompilerParams(dimension_semantics=("parallel",)),
        name="add_norm",
    )(x, r, g.reshape(1, d))


def _peer_candidates():
    groups = [("a", 0, 0), ("a", 0, 8), ("a", 1, 0), ("b", 0, 8), ("a", 2, 0), ("a", 3, 0),
              ("b", 0, 0), ("b", 1, 0), ("b", 2, 0)]
    cid = np.zeros((8 * len(groups), 1), np.int32)
    seen = set()
    for g, (kind, fixed, start) in enumerate(groups):
        for r in range(8):
            a, b = (fixed, start + r) if kind == "a" else (start + r, fixed)
            row = 8 * g + r
            if (a + 1) * (b + 1) <= PEER_TOPK and (a, b) not in seen:
                seen.add((a, b))
                cid[row, 0] = a * PEER_TOPK + b
            else:
                cid[row, 0] = PEER_TOPK * PEER_TOPK + row
    assert len(seen) == sum((a + 1) * (b + 1) <= PEER_TOPK
                            for a in range(PEER_TOPK) for b in range(PEER_TOPK))
    return groups, cid


def _peer_topk_body(q_ref, sk_ref, cid_ref, idx_ref, gate_ref, ts_scr, ti_scr,
                    bs_scr, be_scr, cv_scr, ce_scr, sc_scr, *, groups):
    tm = q_ref.shape[0]
    neg_inf = jnp.float32(-jnp.inf)
    key_id = lax.broadcasted_iota(I32, (1, N_KEYS, tm), 1)
    batch = sc_scr.shape[0]

    def group_body(g, carry):
        for j in range(batch):
            pr = g * batch + j
            off = pl.multiple_of(pr * PEER_KEY_HALF, PEER_KEY_HALF)
            sc_scr[j] = lax.dot_general(sk_ref[pr], q_ref[:, pl.ds(off, PEER_KEY_HALF)],
                                        (((1,), (1,)), ((), ())), preferred_element_type=F32)
        rows = pl.ds(g * batch, batch)

        def k_body(k, c):
            vals = sc_scr[...]
            m = jnp.max(vals, axis=1, keepdims=True)
            sel = jnp.min(jnp.where(vals == m, key_id, N_KEYS), axis=1, keepdims=True)
            ts_scr[rows, pl.ds(k, 1), :] = m
            ti_scr[rows, pl.ds(k, 1), :] = sel
            sc_scr[...] = jnp.where(key_id == sel, neg_inf, vals)
            return c

        lax.fori_loop(0, PEER_TOPK, k_body, 0)
        return carry

    lax.fori_loop(0, 2 * PEER_HEADS // batch, group_body, 0)

    cand_id = cid_ref[...]
    pad = jnp.where(cand_id < PEER_TOPK * PEER_TOPK, 0.0, neg_inf)

    def pick(x1, x2):
        return jnp.concatenate(
            [x1[f:f + 1] + x2[s:s + 8] if kind == "a" else x1[s:s + 8] + x2[f:f + 1]
             for kind, f, s in groups], axis=0)

    def head_body(h, carry):
        cv_scr[h] = pick(ts_scr[2 * h], ts_scr[2 * h + 1]) + pad
        ce_scr[h] = pick(ti_scr[2 * h] * N_KEYS, ti_scr[2 * h + 1])
        return carry

    lax.fori_loop(0, PEER_HEADS, head_body, 0)

    cid3 = cand_id[None]

    def k_body(k, carry):
        vals = cv_scr[...]
        m = jnp.max(vals, axis=1, keepdims=True)
        sel = jnp.min(jnp.where(vals == m, cid3, PEER_TOPK * PEER_TOPK), axis=1, keepdims=True)
        hit = cid3 == sel
        bs_scr[:, pl.ds(k, 1), :] = m
        be_scr[:, pl.ds(k, 1), :] = jnp.sum(jnp.where(hit, ce_scr[...], 0), axis=1, keepdims=True)
        cv_scr[...] = jnp.where(hit, neg_inf, vals)
        return carry

    lax.fori_loop(0, PEER_TOPK, k_body, 0)
    bs = bs_scr[...]
    e = jnp.exp(bs - jnp.max(bs, axis=1, keepdims=True))
    gate = e / jnp.sum(e, axis=1, keepdims=True)
    n_sel = PEER_HEADS * PEER_TOPK
    gate_ref[...] = gate.reshape(n_sel, tm).T
    idx_ref[...] = be_scr[...].reshape(n_sel, tm).T


def peer_topk(q, subkeys, tm=256):
    m = q.shape[0]
    n_sel = PEER_HEADS * PEER_TOPK
    groups, cid = _peer_candidates()
    out_spec = pl.BlockSpec((tm, n_sel), lambda i: (i, 0))
    return pl.pallas_call(
        functools.partial(_peer_topk_body, groups=groups),
        grid=(m // tm,),
        in_specs=[pl.BlockSpec((tm, q.shape[1]), lambda i: (i, 0)),
                  pl.BlockSpec(subkeys.shape, lambda i: (0, 0, 0)),
                  pl.BlockSpec(cid.shape, lambda i: (0, 0))],
        out_specs=[out_spec, out_spec],
        out_shape=[jax.ShapeDtypeStruct((m, n_sel), I32),
                   jax.ShapeDtypeStruct((m, n_sel), F32)],
        scratch_shapes=[pltpu.VMEM((2 * PEER_HEADS, PEER_TOPK, tm), F32),
                        pltpu.VMEM((2 * PEER_HEADS, PEER_TOPK, tm), I32),
                        pltpu.VMEM((PEER_HEADS, PEER_TOPK, tm), F32),
                        pltpu.VMEM((PEER_HEADS, PEER_TOPK, tm), I32),
                        pltpu.VMEM((PEER_HEADS, cid.shape[0], tm), F32),
                        pltpu.VMEM((PEER_HEADS, cid.shape[0], tm), I32),
                        pltpu.VMEM((PEER_HEADS, N_KEYS, tm), F32)],
        compiler_params=pltpu.CompilerParams(dimension_semantics=("parallel",)),
        name="peer_topk",
    )(q, subkeys, jnp.asarray(cid))


def _gelu_gate_body(a_ref, g_ref, o_ref):
    a = a_ref[...]
    o_ref[...] = g_ref[...] * (0.5 * a * (1.0 + lax.erf(a * (2.0 ** -0.5))))


def gelu_gate(act, gate, tm=2048):
    m, n = act.shape
    spec = pl.BlockSpec((tm, n), lambda i: (i, 0))
    return pl.pallas_call(
        _gelu_gate_body,
        grid=(m // tm,),
        in_specs=[spec, spec],
        out_specs=spec,
        out_shape=jax.ShapeDtypeStruct((m, n), F32),
        compiler_params=pltpu.CompilerParams(dimension_semantics=("parallel",)),
        name="gelu_gate",
    )(act, gate)


SC_TOK_CHUNK = 32
SC_RING = 8
SC_BF16_GROUP = 4
SC_FMT = plsc.PackFormat.INTERLEAVED


def _sc_worker_id():
    return lax.axis_index("s") * SC_CORES + lax.axis_index("c")


def pack_bf16_pairs(t):
    half = t.shape[-1] // 2
    bits = lax.bitcast_convert_type(t.astype(BF16).astype(F32), I32)
    return (bits[..., half:] & jnp.int32(-65536)) | lax.shift_right_logical(
        bits[..., :half], jnp.int32(16))


def _sc_row_pipeline(idx_v, table_hbm, rows_v, sems, n_items, groups, compute):
    def gather(item):
        tt, g = item // groups, item % groups
        ids = idx_v[tt, pl.ds(g * SC_LANES, SC_LANES)]
        slot = item % SC_RING
        return pltpu.make_async_copy(table_hbm.at[ids], rows_v.at[slot], sems.at[slot])

    for s in range(SC_RING - 1):
        gather(s).start()

    def item_body(item, carry):
        nxt = item + SC_RING - 1

        @pl.when(nxt < n_items)
        def _():
            gather(nxt).start()

        gather(item).wait()
        compute(item // groups, item % groups, item % SC_RING)
        return carry

    lax.fori_loop(0, n_items, item_body, 0)


def peer_expert_dots(hp, idx, up):
    m, dw = hp.shape
    n_sel = idx.shape[1]
    per_w = m // SC_WORKERS
    n_chunks = per_w // SC_TOK_CHUNK
    groups = n_sel // SC_LANES
    step = SC_BF16_GROUP * SC_LANES
    mesh = plsc.VectorSubcoreMesh(core_axis_name="c", subcore_axis_name="s")

    @functools.partial(
        pl.kernel, mesh=mesh,
        out_type=jax.ShapeDtypeStruct((m, n_sel), F32),
        scratch_types=[
            pltpu.VMEM((SC_TOK_CHUNK, n_sel), I32),
            pltpu.VMEM((SC_TOK_CHUNK, dw), I32),
            pltpu.VMEM((SC_TOK_CHUNK, n_sel), F32),
            pltpu.VMEM((SC_RING, SC_LANES, dw), I32),
            pltpu.VMEM((SC_LANES * SC_LANES,), F32),
            pltpu.SemaphoreType.DMA((SC_RING,)),
        ],
        compiler_params=pltpu.CompilerParams(needs_layout_passes=False),
        name="peer_expert_dots",
    )
    def k(h_hbm, idx_hbm, u_hbm, act_hbm, idx_v, h_v, act_v, rows_v, part_v, sems):
        base = _sc_worker_id() * per_w
        lane = lax.broadcasted_iota(I32, (SC_LANES,), 0)

        def compute(tt, g, slot):
            def grp_body(q, accs):
                off = pl.multiple_of(q * step, step)
                xs = [plsc.bitcast(h_v[tt, pl.ds(off + c * SC_LANES, SC_LANES)], BF16)
                      for c in range(SC_BF16_GROUP)]
                new = []
                for e in range(SC_LANES):
                    s = None
                    for c in range(SC_BF16_GROUP):
                        p = plsc.bitcast(
                            rows_v[slot, e, pl.ds(off + c * SC_LANES, SC_LANES)], BF16) * xs[c]
                        s = p if s is None else s + p
                    lo, hi = plsc.unpack(s, format=SC_FMT)
                    new.append(accs[e] + (lo + hi))
                return tuple(new)

            accs = lax.fori_loop(
                0, dw // step, grp_body,
                tuple(jnp.zeros((SC_LANES,), F32) for _ in range(SC_LANES)))
            for e in range(SC_LANES):
                part_v[pl.ds(e * SC_LANES, SC_LANES)] = accs[e]
            tot = jnp.zeros((SC_LANES,), F32)
            for l in range(SC_LANES):
                tot = tot + plsc.load_gather(part_v, [lane * SC_LANES + l])
            act_v[tt, pl.ds(g * SC_LANES, SC_LANES)] = tot

        def chunk_body(c, carry):
            t0 = base + c * SC_TOK_CHUNK
            pltpu.sync_copy(idx_hbm.at[pl.ds(t0, SC_TOK_CHUNK)], idx_v)
            pltpu.sync_copy(h_hbm.at[pl.ds(t0, SC_TOK_CHUNK)], h_v)
            _sc_row_pipeline(idx_v, u_hbm, rows_v, sems, SC_TOK_CHUNK * groups, groups, compute)
            pltpu.sync_copy(act_v, act_hbm.at[pl.ds(t0, SC_TOK_CHUNK)])
            return carry

        lax.fori_loop(0, n_chunks, chunk_body, 0)

    return k(hp, idx, up)


def peer_expert_combine(w, idx, vp):
    m, n_sel = w.shape
    dw = vp.shape[1]
    d = 2 * dw
    per_w = m // SC_WORKERS
    n_chunks = per_w // SC_TOK_CHUNK
    n_vec = d // SC_LANES
    groups = n_sel // SC_LANES
    mesh = plsc.VectorSubcoreMesh(core_axis_name="c", subcore_axis_name="s")

    @functools.partial(
        pl.kernel, mesh=mesh,
        out_type=jax.ShapeDtypeStruct((m, d), F32),
        scratch_types=[
            pltpu.VMEM((SC_TOK_CHUNK, n_sel), I32),
            pltpu.VMEM((SC_TOK_CHUNK, n_sel), F32),
            pltpu.VMEM((SC_TOK_CHUNK, d), F32),
            pltpu.VMEM((SC_RING, SC_LANES, dw), I32),
            pltpu.SemaphoreType.DMA((SC_RING,)),
        ],
        compiler_params=pltpu.CompilerParams(needs_layout_passes=False),
        name="peer_expert_combine",
    )
    def k(w_hbm, idx_hbm, v_hbm, out_hbm, idx_v, w_v, out_v, rows_v, sems):
        base = _sc_worker_id() * per_w

        def compute(tt, g, slot):
            splat = []
            for e in range(SC_LANES):
                s = plsc.load_gather(w_v, [jnp.full((SC_LANES,), tt, I32),
                                           jnp.full((SC_LANES,), g * SC_LANES + e, I32)])
                splat.append(plsc.pack(s, s, format=SC_FMT))

            @plsc.parallel_loop(0, dw // SC_LANES)
            def _(j):
                off = pl.multiple_of(j * SC_LANES, SC_LANES)
                acc_lo = out_v[tt, pl.ds(off, SC_LANES)]
                acc_hi = out_v[tt, pl.ds(dw + off, SC_LANES)]
                for e0 in range(0, SC_LANES, SC_BF16_GROUP):
                    s = None
                    for e in range(e0, e0 + SC_BF16_GROUP):
                        p = plsc.bitcast(rows_v[slot, e, pl.ds(off, SC_LANES)], BF16) * splat[e]
                        s = p if s is None else s + p
                    lo, hi = plsc.unpack(s, format=SC_FMT)
                    acc_lo = acc_lo + lo
                    acc_hi = acc_hi + hi
                out_v[tt, pl.ds(off, SC_LANES)] = acc_lo
                out_v[tt, pl.ds(dw + off, SC_LANES)] = acc_hi

        def chunk_body(c, carry):
            t0 = base + c * SC_TOK_CHUNK
            pltpu.sync_copy(idx_hbm.at[pl.ds(t0, SC_TOK_CHUNK)], idx_v)
            pltpu.sync_copy(w_hbm.at[pl.ds(t0, SC_TOK_CHUNK)], w_v)

            def zero_body(z, carry2):
                tt, j = z // n_vec, z % n_vec
                out_v[tt, pl.ds(pl.multiple_of(j * SC_LANES, SC_LANES), SC_LANES)] = (
                    jnp.zeros((SC_LANES,), F32))
                return carry2

            lax.fori_loop(0, SC_TOK_CHUNK * n_vec, zero_body, 0)
            _sc_row_pipeline(idx_v, v_hbm, rows_v, sems, SC_TOK_CHUNK * groups, groups, compute)
            pltpu.sync_copy(out_v, out_hbm.at[pl.ds(t0, SC_TOK_CHUNK)])
            return carry

        lax.fori_loop(0, n_chunks, chunk_body, 0)

    return k(w, idx, vp)


def kernel(x, norm_mix_g, norm_ffn_g, final_norm_g, rel_bias, even_w_in, even_w_out,
           diff_lambda, diff_ln_g, odd_w_in, odd_b_f, odd_w_out, peer_wq, peer_subkeys,
           peer_u, peer_v):
    batch, seq, d = x.shape

    dil_tiles = [dilated_bias_tile(rel_bias[:, :A_HEADS], w, dl) for w, dl in DILATED_BRANCHES]
    diff_tiles = diff_bias_tiles(rel_bias[:, A_HEADS:], seq)
    lam_init = 0.8 - 0.6 * math.exp(-0.3 * 0)
    even_in, even_out = even_w_in[0].astype(BF16), even_w_out[0].astype(BF16)
    w_in = odd_w_in[0]
    odd_in, odd_out = w_in[:, :3 * C_WIDTH].astype(BF16), odd_w_out[0].astype(BF16)
    w_gate = jnp.pad(w_in[:, 3 * C_WIDTH:], ((0, 0), (0, LANES - C_HEADS)))
    b_f = jnp.pad(odd_b_f[0], (0, LANES - C_HEADS)).reshape(1, LANES)
    peer = [dict(wq=peer_wq[l].astype(BF16),
                 sk=peer_subkeys[l].reshape(2 * PEER_HEADS, N_KEYS, PEER_KEY_HALF).astype(BF16),
                 up=pack_bf16_pairs(peer_u[l]), vp=pack_bf16_pairs(peer_v[l]))
            for l in range(2)]

    bg = batch // BATCH_GROUPS
    m = bg * seq
    groups = [dict(x=x[g * bg:(g + 1) * bg].reshape(m, d)) for g in range(BATCH_GROUPS)]

    def peer_select(st, layer, after=None):
        q, hp = norm_matmul(st["x"], norm_ffn_g[layer], peer[layer]["wq"], want_h="packed")
        idx, gate = peer_topk(q, peer[layer]["sk"])
        if after is not None and after["st"] is not st:
            after["st"]["peer"], idx = lax.optimization_barrier((after["st"]["peer"], idx))
        return dict(st=st, layer=layer, idx=idx, gate=gate,
                    act=peer_expert_dots(hp, idx, peer[layer]["up"]))

    def peer_combine(rec):
        rec["w"] = gelu_gate(rec["act"], rec["gate"])
        rec["st"]["peer"] = peer_expert_combine(rec["w"], rec["idx"], peer[rec["layer"]]["vp"])

    def even_layer(st):
        (p,) = norm_matmul(st["x"], norm_mix_g[0], even_in)
        branches = [dilated_branch(p, tile, dl, bg, seq)
                    for tile, (_, dl) in zip(dil_tiles, DILATED_BRANCHES)]
        ob = diff_attention(p.reshape(bg, seq, -1), diff_tiles, diff_lambda[0], diff_ln_g[0],
                            lam_init, bg, seq)
        st["x"] = even_out_proj(branches, ob.reshape(m, B_V_WIDTH), even_out, st["x"])

    def odd_layer(st):
        p, st["x"], h = norm_matmul(st["x"], norm_mix_g[1], odd_in, res=st["peer"], want_h="f32")
        fg = gate_matmul(h, w_gate)
        c = logsig_cumsum(fg.reshape(bg, seq, LANES), b_f, bg, seq)
        ct = c[:, :, :C_HEADS].transpose(0, 2, 1)
        o = fox_attention(p.reshape(bg, seq, -1), ct[:, :, :, None], ct[:, :, None, :], bg, seq)
        st["x"] = matmul_residual(o.reshape(m, C_WIDTH), odd_out, st["x"])

    stages = [(even_layer, st, 0) for st in groups] + [(odd_layer, st, 1) for st in groups]
    recs = []
    for k, (mixer, st, layer) in enumerate(stages):
        mixer(st)
        if k >= 1:
            prev = recs[k - 1]
            prev["act"], st["x"] = lax.optimization_barrier((prev["act"], st["x"]))
            peer_combine(prev)
            prev["w"], st["x"] = lax.optimization_barrier((prev["w"], st["x"]))
        recs.append(peer_select(st, layer, after=recs[k - 1] if k >= 1 else None))
    peer_combine(recs[-1])
    outs = [add_norm(st["x"], st["peer"], final_norm_g).reshape(bg, seq, d) for st in groups]
    return jnp.concatenate(outs, axis=0)


def _gate_matmul_body(h_ref, w_ref, o_ref):
    o_ref[...] = jnp.dot(h_ref[...], w_ref[...], preferred_element_type=F32,
                         precision=lax.Precision.HIGHEST)


def gate_matmul(h, w, tm=512):
    m, d = h.shape
    n = w.shape[1]
    return pl.pallas_call(
        _gate_matmul_body,
        grid=(m // tm,),
        in_specs=[pl.BlockSpec((tm, d), lambda i: (i, 0)),
                  pl.BlockSpec((d, n), lambda i: (0, 0))],
        out_specs=pl.BlockSpec((tm, n), lambda i: (i, 0)),
        out_shape=jax.ShapeDtypeStruct((m, n), F32),
        compiler_params=pltpu.CompilerParams(dimension_semantics=("parallel",)),
        name="gate_matmul",
    )(h, w)
```

```python
import functools
import math

import numpy as np
import jax
import jax.numpy as jnp
from jax import lax
from jax.experimental import pallas as pl
from jax.experimental.pallas import tpu as pltpu
from jax.experimental.pallas import tpu_sc as plsc

F32 = jnp.float32
BF16 = jnp.bfloat16
I32 = jnp.int32

D_MODEL = 1024
HEAD_DIM = 64
A_HEADS = 8
DILATED_BRANCHES = ((128, 1), (512, 4), (2048, 16))
DIFF_HALF = 64
DIFF_VDIM = 128
B_HEADS = 4
C_HEADS = 16
N_BUCKETS = 32
MAX_DISTANCE = 2048
PEER_HEADS = 8
N_KEYS = 128
PEER_TOPK = 16
PEER_KEY_HALF = 128
RMS_EPS = 1e-6
NEG_INF = -1e30
A_WIDTH = A_HEADS * HEAD_DIM
B_QK_WIDTH = B_HEADS * 2 * DIFF_HALF
B_V_WIDTH = B_HEADS * DIFF_VDIM
C_WIDTH = C_HEADS * HEAD_DIM
QK_SCALE = 0.125
LANES = 128
DIL_BLOCK = 128
ATT_BLOCK = 512
BATCH_GROUPS = 4

SC_CORES = 2
SC_SUBCORES = 16
SC_LANES = 16
SC_WORKERS = SC_CORES * SC_SUBCORES


def _t5_bucket_table(n):
    max_exact = N_BUCKETS // 2
    d = np.arange(n)
    df = np.maximum(d, 1).astype(np.float32)
    large = max_exact + (
        np.log(df / np.float32(max_exact)) / np.float32(math.log(MAX_DISTANCE / max_exact))
        * np.float32(N_BUCKETS - max_exact)).astype(np.int32)
    large = np.minimum(large, N_BUCKETS - 1)
    return np.where(d < max_exact, d, large).astype(np.int32)


def _norm_matmul_body(*refs, has_res, want_h):
    it = iter(refs)
    x_ref = next(it)
    r_ref = next(it) if has_res else None
    g_ref = next(it)
    w_ref = next(it)
    o_ref = next(it)
    xs_ref = next(it) if has_res else None
    hout_ref = next(it) if want_h else None
    h_scr = next(it)

    @pl.when(pl.program_id(1) == 0)
    def _():
        x = x_ref[...]
        if has_res:
            x = x + r_ref[...]
            xs_ref[...] = x
        ms = jnp.mean(x * x, axis=-1, keepdims=True)
        h = x * lax.rsqrt(ms + RMS_EPS) * g_ref[...]
        if want_h == "f32":
            hout_ref[...] = h
        elif want_h == "packed":
            hout_ref[...] = pack_bf16_pairs(h)
        h_scr[...] = h.astype(BF16)

    o_ref[...] = jnp.dot(h_scr[...], w_ref[...],
                         preferred_element_type=F32).astype(o_ref.dtype)


def norm_matmul(x, g, w, *, res=None, want_h=None, out_dtype=BF16, tm=1024, tn=1024):
    m, d = x.shape
    n = w.shape[1]
    tn = min(tn, n)
    row = pl.BlockSpec((tm, d), lambda i, j: (i, 0))
    in_specs = [row] + ([row] if res is not None else []) + [
        pl.BlockSpec((1, d), lambda i, j: (0, 0)),
        pl.BlockSpec((d, tn), lambda i, j: (0, j))]
    out_specs = [pl.BlockSpec((tm, tn), lambda i, j: (i, j))]
    out_shape = [jax.ShapeDtypeStruct((m, n), out_dtype)]
    if res is not None:
        out_specs.append(row)
        out_shape.append(jax.ShapeDtypeStruct((m, d), F32))
    if want_h == "f32":
        out_specs.append(row)
        out_shape.append(jax.ShapeDtypeStruct((m, d), F32))
    elif want_h == "packed":
        out_specs.append(pl.BlockSpec((tm, d // 2), lambda i, j: (i, 0)))
        out_shape.append(jax.ShapeDtypeStruct((m, d // 2), I32))
    args =[x] + ([res] if res is not None else []) + [g.reshape(1, d), w]
    return pl.pallas_call(
        functools.partial(_norm_matmul_body, has_res=res is not None, want_h=want_h),
        grid=(m // tm, n // tn),
        in_specs=in_specs,
        out_specs=out_specs,
        out_shape=out_shape,
        scratch_shapes=[pltpu.VMEM((tm, d), BF16)],
        compiler_params=pltpu.CompilerParams(
            dimension_semantics=("parallel", "arbitrary")),
        name="norm_matmul",
    )(*args)


def _dilated_body(q_ref, kp_ref, kc_ref, vp_ref, vc_ref, b_ref, o_ref, lse_ref):
    i = pl.program_id(3)
    q = q_ref[...] * QK_SCALE
    k = jnp.concatenate([kp_ref[...], kc_ref[...]], axis=0)
    v = jnp.concatenate([vp_ref[...], vc_ref[...]], axis=0)
    col = lax.broadcasted_iota(I32, (DIL_BLOCK, 2 * DIL_BLOCK), 1)
    has_prev = jnp.logical_or(col >= DIL_BLOCK, i > 0)
    outs, lses = [], []
    for hh in range(2):
        sl = slice(hh * HEAD_DIM, (hh + 1) * HEAD_DIM)
        s = lax.dot_general(q[:, sl], k[:, sl], (((1,), (1,)), ((), ())),
                            preferred_element_type=F32)
        s = jnp.where(has_prev, s + b_ref[hh], NEG_INF)
        m = jnp.max(s, axis=-1, keepdims=True)
        p = jnp.exp(s - m)
        l = jnp.sum(p, axis=-1, keepdims=True)
        o = jnp.dot(p.astype(BF16), v[:, sl], preferred_element_type=F32) / l
        outs.append(o)
        lses.append(jnp.broadcast_to(m + jnp.log(l), (DIL_BLOCK, HEAD_DIM)))
    o_ref[...] = jnp.concatenate(outs, axis=1)
    lse_ref[...] = jnp.concatenate(lses, axis=1)


def dilated_branch(p, bias_tile, dil, batch, seq):
    if dil > 1:
        p = p[:, :3 * A_WIDTH]
    n_cols = p.shape[1]
    cb = n_cols // LANES
    rows = seq // dil
    nblk = rows // DIL_BLOCK
    pv = p.reshape(batch, rows, dil * n_cols)
    kq, kk, kv = 0, A_WIDTH // LANES, 2 * A_WIDTH // LANES
    blk = (None, DIL_BLOCK, LANES)

    def spec(col0, prev):
        if prev:
            return pl.BlockSpec(blk, lambda b, h, r, i: (b, jnp.maximum(i - 1, 0), r * cb + col0 + h))
        return pl.BlockSpec(blk, lambda b, h, r, i: (b, i, r * cb + col0 + h))

    ocb = A_WIDTH // LANES
    ospec = pl.BlockSpec(blk, lambda b, h, r, i: (b, i, r * ocb + h))
    oshape = jax.ShapeDtypeStruct((batch, rows, dil * A_WIDTH), F32)
    o, lse = pl.pallas_call(
        _dilated_body,
        grid=(batch, A_HEADS // 2, dil, nblk),
        in_specs=[spec(kq, False), spec(kk, True), spec(kk, False),
                  spec(kv, True), spec(kv, False),
                  pl.BlockSpec((2, DIL_BLOCK, 2 * DIL_BLOCK), lambda b, h, r, i: (h, 0, 0))],
        out_specs=[ospec, ospec],
        out_shape=[oshape, oshape],
        compiler_params=pltpu.CompilerParams(
            dimension_semantics=("parallel", "parallel", "parallel", "arbitrary")),
        name=f"dilated_d{dil}",
    )(pv, pv, pv, pv, pv, bias_tile)
    return o.reshape(batch * seq, A_WIDTH), lse.reshape(batch * seq, A_WIDTH)


def dilated_bias_tile(rel_bias_a, window, dil):
    n = window // dil
    assert n == DIL_BLOCK
    bucket = _t5_bucket_table(window + 1)
    period = 4 * n
    u = np.arange(period)
    valid = u <= n
    w = jnp.where(jnp.asarray(valid)[None],
                  rel_bias_a.T[:, bucket[np.where(valid, n - u, 0) * dil]], NEG_INF).astype(F32)
    rep = jnp.broadcast_to(w[:, None, :], (A_HEADS, n, period))
    flat = rep.reshape(A_HEADS, n * period)[:, :n * (period - 1)]
    return flat.reshape(A_HEADS, n, period - 1)[:, :, :2 * n]


def _diff_body(q_ref, k_ref, v_ref, b_ref, lam_ref, g_ref, o_ref,
               m_scr, l_scr, acc_scr, *, n_tiles, lam_init):
    t = ATT_BLOCK
    qi = pl.program_id(2)
    q = q_ref[...] * QK_SCALE
    lane = lax.broadcasted_iota(I32, (1, LANES), 1)
    qa = [jnp.where((lane < DIFF_HALF) == (a == 0), q, jnp.zeros_like(q)) for a in range(2)]
    ones = jnp.ones((t, LANES), BF16)
    for a in range(2):
        m_scr[a] = jnp.full((t, LANES), NEG_INF, F32)
        l_scr[a] = jnp.zeros((t, LANES), F32)
        acc_scr[a] = jnp.zeros((t, DIFF_VDIM), F32)

    def step(j, masked):
        off = pl.multiple_of(j * t, t)
        ks = k_ref[pl.ds(off, t), :]
        vs = v_ref[pl.ds(off, t), :]
        bias = b_ref[jnp.minimum(qi - j, n_tiles - 1)]
        if masked:
            row = lax.broadcasted_iota(I32, (t, t), 0)
            col = lax.broadcasted_iota(I32, (t, t), 1)
            causal = row >= col
        for a in range(2):
            s = lax.dot_general(qa[a], ks, (((1,), (1,)), ((), ())),
                                preferred_element_type=F32) + bias
            if masked:
                s = jnp.where(causal, s, NEG_INF)
            m_prev = m_scr[a]
            m_new = jnp.maximum(m_prev, jnp.max(s, axis=-1, keepdims=True))
            alpha = jnp.exp(m_prev - m_new)
            pb = jnp.exp(s - jnp.concatenate([m_new] * (t // LANES), axis=1)).astype(BF16)
            l_scr[a] = alpha * l_scr[a] + jnp.dot(pb, ones, preferred_element_type=F32)
            acc_scr[a] = alpha * acc_scr[a] + jnp.dot(pb, vs, preferred_element_type=F32)
            m_scr[a] = m_new

    def loop_body(j, carry):
        step(j, False)
        return carry

    lax.fori_loop(0, qi, loop_body, 0)
    step(qi, True)

    lp = lam_ref[...]
    lam = (jnp.exp(jnp.sum(lp[0:1] * lp[1:2])) - jnp.exp(jnp.sum(lp[2:3] * lp[3:4]))
           + lam_init)
    o = acc_scr[0] / l_scr[0] - lam * (acc_scr[1] / l_scr[1])
    ms = jnp.mean(o * o, axis=-1, keepdims=True)
    y = o * lax.rsqrt(ms + RMS_EPS) * g_ref[...]
    o_ref[...] = (y * (1.0 - lam_init)).astype(o_ref.dtype)


def diff_attention(p, bias_tiles, lam_params, ln_g, lam_init, batch, seq):
    t = ATT_BLOCK
    n_tiles = bias_tiles.shape[1]
    cq = 3 * A_WIDTH // LANES
    ck = cq + B_QK_WIDTH // LANES
    cv = ck + B_QK_WIDTH // LANES
    return pl.pallas_call(
        functools.partial(_diff_body, n_tiles=n_tiles, lam_init=lam_init),
        grid=(batch, B_HEADS, seq // t),
        in_specs=[
            pl.BlockSpec((None, t, LANES), lambda b, h, i: (b, i, cq + h)),
            pl.BlockSpec((None, seq, LANES), lambda b, h, i: (b, 0, ck + h)),
            pl.BlockSpec((None, seq, LANES), lambda b, h, i: (b, 0, cv + h)),
            pl.BlockSpec((None, n_tiles, t, t), lambda b, h, i: (h, 0, 0, 0)),
            pl.BlockSpec((4, DIFF_HALF), lambda b, h, i: (0, 0)),
            pl.BlockSpec((1, DIFF_VDIM), lambda b, h, i: (0, 0)),
        ],
        out_specs=pl.BlockSpec((None, t, LANES), lambda b, h, i: (b, i, h)),
        out_shape=jax.ShapeDtypeStruct((batch, seq, B_V_WIDTH), BF16),
        scratch_shapes=[pltpu.VMEM((2, t, LANES), F32), pltpu.VMEM((2, t, LANES), F32),
                        pltpu.VMEM((2, t, DIFF_VDIM), F32)],
        compiler_params=pltpu.CompilerParams(
            dimension_semantics=("parallel", "parallel", "arbitrary")),
        name="diff_attention",
    )(p, p, p, bias_tiles, lam_params, ln_g.reshape(1, DIFF_VDIM))


def diff_bias_tiles(rel_bias_b, seq):
    t = ATT_BLOCK
    bucket = _t5_bucket_table(max(seq, 2 * MAX_DISTANCE) + 2 * t)
    sat = bucket[-1]
    d_sat = int(np.max(np.nonzero(bucket != sat)[0])) + 1
    n_full = (d_sat + t - 1 + t - 1) // t
    n_tiles = n_full + 1
    assert n_full * t - (t - 1) >= d_sat
    n = np.arange(2 * t)[None, :]
    base = np.arange(n_tiles)[:, None] * t
    dist = np.clip(np.where(n < t, base - n, base + 2 * t - n), 0, None)
    w = rel_bias_b.T[:, bucket[dist]].astype(F32)
    rep = jnp.broadcast_to(w[:, :, None, :], (B_HEADS, n_tiles, t, 2 * t))
    flat = rep.reshape(B_HEADS, n_tiles, 2 * t * t)[:, :, :t * (2 * t - 1)]
    return flat.reshape(B_HEADS, n_tiles, t, 2 * t - 1)[:, :, :, :t]


def _logsig_cumsum_body(f_ref, b_ref, c_ref, carry_scr):
    t = f_ref.shape[0]

    @pl.when(pl.program_id(1) == 0)
    def _():
        carry_scr[...] = jnp.zeros_like(carry_scr)

    x = f_ref[...] + b_ref[...]
    ls = jnp.minimum(x, 0.0) - jnp.log1p(jnp.exp(-jnp.abs(x)))
    row = lax.broadcasted_iota(I32, (t, t), 0)
    col = lax.broadcasted_iota(I32, (t, t), 1)
    tri = (row >= col).astype(F32)
    c = jnp.dot(tri, ls, preferred_element_type=F32,
                precision=lax.Precision.HIGHEST) + carry_scr[...]
    c_ref[...] = c
    carry_scr[...] = c[t - 1:t, :]


def logsig_cumsum(fg, b_f, batch, seq, t=512):
    return pl.pallas_call(
        _logsig_cumsum_body,
        grid=(batch, seq // t),
        in_specs=[pl.BlockSpec((None, t, LANES), lambda b, i: (b, i, 0)),
                  pl.BlockSpec((1, LANES), lambda b, i: (0, 0))],
        out_specs=pl.BlockSpec((None, t, LANES), lambda b, i: (b, i, 0)),
        out_shape=jax.ShapeDtypeStruct((batch, seq, LANES), F32),
        scratch_shapes=[pltpu.VMEM((1, LANES), F32)],
        compiler_params=pltpu.CompilerParams(
            dimension_semantics=("parallel", "arbitrary")),
        name="logsig_cumsum",
    )(fg, b_f)


def _fox_body(q_ref, k_ref, v_ref, cq_ref, ck_ref, o_ref, m_scr, acc_scr):
    t = ATT_BLOCK
    qi = pl.program_id(2)
    q = q_ref[...] * QK_SCALE
    lane = lax.broadcasted_iota(I32, (1, LANES), 1)
    own = [(lane < HEAD_DIM) == (a == 0) for a in range(2)]
    qa = [jnp.where(own[a], q, jnp.zeros_like(q)) for a in range(2)]
    cqb = [jnp.broadcast_to(cq_ref[a], (t, LANES)) for a in range(2)]
    for a in range(2):
        m_scr[a] = jnp.full((t, LANES), NEG_INF, F32)
        acc_scr[a] = jnp.zeros((t, LANES), F32)

    def step(j, masked):
        off = pl.multiple_of(j * t, t)
        ks = k_ref[pl.ds(off, t), :]
        vs = v_ref[pl.ds(off, t), :]
        if masked:
            row = lax.broadcasted_iota(I32, (t, t), 0)
            col = lax.broadcasted_iota(I32, (t, t), 1)
            causal = row >= col
        for a in range(2):
            s = lax.dot_general(qa[a], ks, (((1,), (1,)), ((), ())),
                                preferred_element_type=F32)
            s = s - ck_ref[a, :, pl.ds(off, t)]
            if masked:
                s = jnp.where(causal, s, NEG_INF)
            m_prev = m_scr[a]
            m_new = jnp.maximum(m_prev, jnp.max(s, axis=-1, keepdims=True) + cqb[a])
            alpha = jnp.exp(m_prev - m_new)
            shift = m_new - cqb[a]
            pb = jnp.exp(s - jnp.concatenate([shift] * (t // LANES), axis=1)).astype(BF16)
            v_aug = jnp.where(own[a], vs, jnp.ones_like(vs))
            acc_scr[a] = alpha * acc_scr[a] + jnp.dot(pb, v_aug, preferred_element_type=F32)
            m_scr[a] = m_new

    def loop_body(j, carry):
        step(j, False)
        return carry

    lax.fori_loop(0, qi, loop_body, 0)
    step(qi, True)
    r = [acc_scr[a] / pltpu.roll(acc_scr[a], HEAD_DIM, 1) for a in range(2)]
    o_ref[...] = jnp.where(own[0], r[0], r[1]).astype(o_ref.dtype)


def fox_attention(p, cq, ck, batch, seq):
    t = ATT_BLOCK
    nk = C_WIDTH // LANES
    return pl.pallas_call(
        _fox_body,
        grid=(batch, C_HEADS // 2, seq // t),
        in_specs=[
            pl.BlockSpec((None, t, LANES), lambda b, h, i: (b, i, h)),
            pl.BlockSpec((None, seq, LANES), lambda b, h, i: (b, 0, nk + h)),
            pl.BlockSpec((None, seq, LANES), lambda b, h, i: (b, 0, 2 * nk + h)),
            pl.BlockSpec((None, 2, t, 1), lambda b, h, i: (b, h, i, 0)),
            pl.BlockSpec((None, 2, 1, seq), lambda b, h, i: (b, h, 0, 0)),
        ],
        out_specs=pl.BlockSpec((None, t, LANES), lambda b, h, i: (b, i, h)),
        out_shape=jax.ShapeDtypeStruct((batch, seq, C_WIDTH), BF16),
        scratch_shapes=[pltpu.VMEM((2, t, LANES), F32), pltpu.VMEM((2, t, LANES), F32)],
        compiler_params=pltpu.CompilerParams(
            dimension_semantics=("parallel", "parallel", "arbitrary")),
        name="fox_attention",
    )(p, p, p, cq, ck)


def _even_out_body(o1, l1, o2, l2, o3, l3, ob_ref, w_ref, x_ref, out_ref, a_scr):
    @pl.when(pl.program_id(1) == 0)
    def _():
        a1, a2, a3 = l1[...], l2[...], l3[...]
        mx = jnp.maximum(jnp.maximum(a1, a2), a3)
        e1, e2, e3 = jnp.exp(a1 - mx), jnp.exp(a2 - mx), jnp.exp(a3 - mx)
        oa = (e1 * o1[...] + e2 * o2[...] + e3 * o3[...]) / (e1 + e2 + e3)
        a_scr[:, :A_WIDTH] = oa.astype(BF16)
        a_scr[:, A_WIDTH:] = ob_ref[...]

    out_ref[...] = x_ref[...] + jnp.dot(a_scr[...], w_ref[...], preferred_element_type=F32)


def even_out_proj(branches, ob, w, x, tm=512, tn=1024):
    m, d = x.shape
    half = pl.BlockSpec((tm, A_WIDTH), lambda i, j: (i, 0))
    flat = [a for pair in branches for a in pair]
    return pl.pallas_call(
        _even_out_body,
        grid=(m // tm, d // tn),
        in_specs=[half] * 6 + [
            pl.BlockSpec((tm, B_V_WIDTH), lambda i, j: (i, 0)),
            pl.BlockSpec((A_WIDTH + B_V_WIDTH, tn), lambda i, j: (0, j)),
            pl.BlockSpec((tm, tn), lambda i, j: (i, j))],
        out_specs=pl.BlockSpec((tm, tn), lambda i, j: (i, j)),
        out_shape=jax.ShapeDtypeStruct((m, d), F32),
        scratch_shapes=[pltpu.VMEM((tm, A_WIDTH + B_V_WIDTH), BF16)],
        compiler_params=pltpu.CompilerParams(
            dimension_semantics=("parallel", "arbitrary")),
        name="even_out_proj",
    )(*flat, ob, w, x)


def _matmul_res_body(a_ref, w_ref, x_ref, o_ref):
    o_ref[...] = x_ref[...] + jnp.dot(a_ref[...], w_ref[...], preferred_element_type=F32)


def matmul_residual(a, w, x, tm=1024, tn=1024):
    m, k = a.shape
    n = w.shape[1]
    return pl.pallas_call(
        _matmul_res_body,
        grid=(m // tm, n // tn),
        in_specs=[pl.BlockSpec((tm, k), lambda i, j: (i, 0)),
                  pl.BlockSpec((k, tn), lambda i, j: (0, j)),
                  pl.BlockSpec((tm, tn), lambda i, j: (i, j))],
        out_specs=pl.BlockSpec((tm, tn), lambda i, j: (i, j)),
        out_shape=jax.ShapeDtypeStruct((m, n), F32),
        compiler_params=pltpu.CompilerParams(
            dimension_semantics=("parallel", "parallel")),
        name="matmul_residual",
    )(a, w, x)


def _add_norm_body(x_ref, r_ref, g_ref, o_ref):
    x = x_ref[...] + r_ref[...]
    ms = jnp.mean(x * x, axis=-1, keepdims=True)
    o_ref[...] = x * lax.rsqrt(ms + RMS_EPS) * g_ref[...]


def add_norm(x, r, g, tm=512):
    m, d = x.shape
    row = pl.BlockSpec((tm, d), lambda i: (i, 0))
    return pl.pallas_call(
        _add_norm_body,
        grid=(m // tm,),
        in_specs=[row, row, pl.BlockSpec((1, d), lambda i: (0, 0))],
        out_specs=row,
        out_shape=jax.ShapeDtypeStruct((m, d), F32),
        compiler_params=pltpu.CompilerParams(dimension_semantics=("parallel",)),
        name="add_norm",
    )(x, r, g.reshape(1, d))


def _peer_candidates():
    groups = [("a", 0, 0), ("a", 0, 8), ("a", 1, 0), ("b", 0, 8), ("a", 2, 0), ("a", 3, 0),
              ("b", 0, 0), ("b", 1, 0), ("b", 2, 0)]
    cid = np.zeros((8 * len(groups), 1), np.int32)
    seen = set()
    for g, (kind, fixed, start) in enumerate(groups):
        for r in range(8):
            a, b = (fixed, start + r) if kind == "a" else (start + r, fixed)
            row = 8 * g + r
            if (a + 1) * (b + 1) <= PEER_TOPK and (a, b) not in seen:
                seen.add((a, b))
                cid[row, 0] = a * PEER_TOPK + b
            else:
                cid[row, 0] = PEER_TOPK * PEER_TOPK + row
    assert len(seen) == sum((a + 1) * (b + 1) <= PEER_TOPK
                            for a in range(PEER_TOPK) for b in range(PEER_TOPK))
    return groups, cid


def _peer_topk_body(q_ref, sk_ref, cid_ref, idx_ref, gate_ref, ts_scr, ti_scr,
                    bs_scr, be_scr, cv_scr, ce_scr, sc_scr, *, groups):
    tm = q_ref.shape[0]
    neg_inf = jnp.float32(-jnp.inf)
    key_id = lax.broadcasted_iota(I32, (1, N_KEYS, tm), 1)
    batch = sc_scr.shape[0]

    def group_body(g, carry):
        for j in range(batch):
            pr = g * batch + j
            off = pl.multiple_of(pr * PEER_KEY_HALF, PEER_KEY_HALF)
            sc_scr[j] = lax.dot_general(sk_ref[pr], q_ref[:, pl.ds(off, PEER_KEY_HALF)],
                                        (((1,), (1,)), ((), ())), preferred_element_type=F32)
        rows = pl.ds(g * batch, batch)

        def k_body(k, c):
            vals = sc_scr[...]
            m = jnp.max(vals, axis=1, keepdims=True)
            sel = jnp.min(jnp.where(vals == m, key_id, N_KEYS), axis=1, keepdims=True)
            ts_scr[rows, pl.ds(k, 1), :] = m
            ti_scr[rows, pl.ds(k, 1), :] = sel
            sc_scr[...] = jnp.where(key_id == sel, neg_inf, vals)
            return c

        lax.fori_loop(0, PEER_TOPK, k_body, 0)
        return carry

    lax.fori_loop(0, 2 * PEER_HEADS // batch, group_body, 0)

    cand_id = cid_ref[...]
    pad = jnp.where(cand_id < PEER_TOPK * PEER_TOPK, 0.0, neg_inf)

    def pick(x1, x2):
        return jnp.concatenate(
            [x1[f:f + 1] + x2[s:s + 8] if kind == "a" else x1[s:s + 8] + x2[f:f + 1]
             for kind, f, s in groups], axis=0)

    def head_body(h, carry):
        cv_scr[h] = pick(ts_scr[2 * h], ts_scr[2 * h + 1]) + pad
        ce_scr[h] = pick(ti_scr[2 * h] * N_KEYS, ti_scr[2 * h + 1])
        return carry

    lax.fori_loop(0, PEER_HEADS, head_body, 0)

    cid3 = cand_id[None]

    def k_body(k, carry):
        vals = cv_scr[...]
        m = jnp.max(vals, axis=1, keepdims=True)
        sel = jnp.min(jnp.where(vals == m, cid3, PEER_TOPK * PEER_TOPK), axis=1, keepdims=True)
        hit = cid3 == sel
        bs_scr[:, pl.ds(k, 1), :] = m
        be_scr[:, pl.ds(k, 1), :] = jnp.sum(jnp.where(hit, ce_scr[...], 0), axis=1, keepdims=True)
        cv_scr[...] = jnp.where(hit, neg_inf, vals)
        return carry

    lax.fori_loop(0, PEER_TOPK, k_body, 0)
    bs = bs_scr[...]
    e = jnp.exp(bs - jnp.max(bs, axis=1, keepdims=True))
    gate = e / jnp.sum(e, axis=1, keepdims=True)
    n_sel = PEER_HEADS * PEER_TOPK
    gate_ref[...] = gate.reshape(n_sel, tm).T
    idx_ref[...] = be_scr[...].reshape(n_sel, tm).T


def peer_topk(q, subkeys, tm=256):
    m = q.shape[0]
    n_sel = PEER_HEADS * PEER_TOPK
    groups, cid = _peer_candidates()
    out_spec = pl.BlockSpec((tm, n_sel), lambda i: (i, 0))
    return pl.pallas_call(
        functools.partial(_peer_topk_body, groups=groups),
        grid=(m // tm,),
        in_specs=[pl.BlockSpec((tm, q.shape[1]), lambda i: (i, 0)),
                  pl.BlockSpec(subkeys.shape, lambda i: (0, 0, 0)),
                  pl.BlockSpec(cid.shape, lambda i: (0, 0))],
        out_specs=[out_spec, out_spec],
        out_shape=[jax.ShapeDtypeStruct((m, n_sel), I32),
                   jax.ShapeDtypeStruct((m, n_sel), F32)],
        scratch_shapes=[pltpu.VMEM((2 * PEER_HEADS, PEER_TOPK, tm), F32),
                        pltpu.VMEM((2 * PEER_HEADS, PEER_TOPK, tm), I32),
                        pltpu.VMEM((PEER_HEADS, PEER_TOPK, tm), F32),
                        pltpu.VMEM((PEER_HEADS, PEER_TOPK, tm), I32),
                        pltpu.VMEM((PEER_HEADS, cid.shape[0], tm), F32),
                        pltpu.VMEM((PEER_HEADS, cid.shape[0], tm), I32),
                        pltpu.VMEM((PEER_HEADS, N_KEYS, tm), F32)],
        compiler_params=pltpu.CompilerParams(dimension_semantics=("parallel",)),
        name="peer_topk",
    )(q, subkeys, jnp.asarray(cid))


def _gelu_gate_body(a_ref, g_ref, o_ref):
    a = a_ref[...]
    o_ref[...] = g_ref[...] * (0.5 * a * (1.0 + lax.erf(a * (2.0 ** -0.5))))


def gelu_gate(act, gate, tm=2048):
    m, n = act.shape
    spec = pl.BlockSpec((tm, n), lambda i: (i, 0))
    return pl.pallas_call(
        _gelu_gate_body,
        grid=(m // tm,),
        in_specs=[spec, spec],
        out_specs=spec,
        out_shape=jax.ShapeDtypeStruct((m, n), F32),
        compiler_params=pltpu.CompilerParams(dimension_semantics=("parallel",)),
        name="gelu_gate",
    )(act, gate)


SC_TOK_CHUNK = 32
SC_RING = 8
SC_BF16_GROUP = 4
SC_FMT = plsc.PackFormat.INTERLEAVED


def _sc_worker_id():
    return lax.axis_index("s") * SC_CORES + lax.axis_index("c")


def pack_bf16_pairs(t):
    half = t.shape[-1] // 2
    bits = lax.bitcast_convert_type(t.astype(BF16).astype(F32), I32)
    return (bits[..., half:] & jnp.int32(-65536)) | lax.shift_right_logical(
        bits[..., :half], jnp.int32(16))


def _sc_row_pipeline(idx_v, table_hbm, rows_v, sems, n_items, groups, compute):
    def gather(item):
        tt, g = item // groups, item % groups
        ids = idx_v[tt, pl.ds(g * SC_LANES, SC_LANES)]
        slot = item % SC_RING
        return pltpu.make_async_copy(table_hbm.at[ids], rows_v.at[slot], sems.at[slot])

    for s in range(SC_RING - 1):
        gather(s).start()

    def item_body(item, carry):
        nxt = item + SC_RING - 1

        @pl.when(nxt < n_items)
        def _():
            gather(nxt).start()

        gather(item).wait()
        compute(item // groups, item % groups, item % SC_RING)
        return carry

    lax.fori_loop(0, n_items, item_body, 0)


def peer_expert_dots(hp, idx, up):
    m, dw = hp.shape
    n_sel = idx.shape[1]
    per_w = m // SC_WORKERS
    n_chunks = per_w // SC_TOK_CHUNK
    groups = n_sel // SC_LANES
    step = SC_BF16_GROUP * SC_LANES
    mesh = plsc.VectorSubcoreMesh(core_axis_name="c", subcore_axis_name="s")

    @functools.partial(
        pl.kernel, mesh=mesh,
        out_type=jax.ShapeDtypeStruct((m, n_sel), F32),
        scratch_types=[
            pltpu.VMEM((SC_TOK_CHUNK, n_sel), I32),
            pltpu.VMEM((SC_TOK_CHUNK, dw), I32),
            pltpu.VMEM((SC_TOK_CHUNK, n_sel), F32),
            pltpu.VMEM((SC_RING, SC_LANES, dw), I32),
            pltpu.VMEM((SC_LANES * SC_LANES,), F32),
            pltpu.SemaphoreType.DMA((SC_RING,)),
        ],
        compiler_params=pltpu.CompilerParams(needs_layout_passes=False),
        name="peer_expert_dots",
    )
    def k(h_hbm, idx_hbm, u_hbm, act_hbm, idx_v, h_v, act_v, rows_v, part_v, sems):
        base = _sc_worker_id() * per_w
        lane = lax.broadcasted_iota(I32, (SC_LANES,), 0)

        def compute(tt, g, slot):
            def grp_body(q, accs):
                off = pl.multiple_of(q * step, step)
                xs = [plsc.bitcast(h_v[tt, pl.ds(off + c * SC_LANES, SC_LANES)], BF16)
                      for c in range(SC_BF16_GROUP)]
                new = []
                for e in range(SC_LANES):
                    s = None
                    for c in range(SC_BF16_GROUP):
                        p = plsc.bitcast(
                            rows_v[slot, e, pl.ds(off + c * SC_LANES, SC_LANES)], BF16) * xs[c]
                        s = p if s is None else s + p
                    lo, hi = plsc.unpack(s, format=SC_FMT)
                    new.append(accs[e] + (lo + hi))
                return tuple(new)

            accs = lax.fori_loop(
                0, dw // step, grp_body,
                tuple(jnp.zeros((SC_LANES,), F32) for _ in range(SC_LANES)))
            for e in range(SC_LANES):
                part_v[pl.ds(e * SC_LANES, SC_LANES)] = accs[e]
            tot = jnp.zeros((SC_LANES,), F32)
            for l in range(SC_LANES):
                tot = tot + plsc.load_gather(part_v, [lane * SC_LANES + l])
            act_v[tt, pl.ds(g * SC_LANES, SC_LANES)] = tot

        def chunk_body(c, carry):
            t0 = base + c * SC_TOK_CHUNK
            pltpu.sync_copy(idx_hbm.at[pl.ds(t0, SC_TOK_CHUNK)], idx_v)
            pltpu.sync_copy(h_hbm.at[pl.ds(t0, SC_TOK_CHUNK)], h_v)
            _sc_row_pipeline(idx_v, u_hbm, rows_v, sems, SC_TOK_CHUNK * groups, groups, compute)
            pltpu.sync_copy(act_v, act_hbm.at[pl.ds(t0, SC_TOK_CHUNK)])
            return carry

        lax.fori_loop(0, n_chunks, chunk_body, 0)

    return k(hp, idx, up)


def peer_expert_combine(w, idx, vp):
    m, n_sel = w.shape
    dw = vp.shape[1]
    d = 2 * dw
    per_w = m // SC_WORKERS
    n_chunks = per_w // SC_TOK_CHUNK
    n_vec = d // SC_LANES
    groups = n_sel // SC_LANES
    mesh = plsc.VectorSubcoreMesh(core_axis_name="c", subcore_axis_name="s")

    @functools.partial(
        pl.kernel, mesh=mesh,
        out_type=jax.ShapeDtypeStruct((m, d), F32),
        scratch_types=[
            pltpu.VMEM((SC_TOK_CHUNK, n_sel), I32),
            pltpu.VMEM((SC_TOK_CHUNK, n_sel), F32),
            pltpu.VMEM((SC_TOK_CHUNK, d), F32),
            pltpu.VMEM((SC_RING, SC_LANES, dw), I32),
            pltpu.SemaphoreType.DMA((SC_RING,)),
        ],
        compiler_params=pltpu.CompilerParams(needs_layout_passes=False),
        name="peer_expert_combine",
    )
    def k(w_hbm, idx_hbm, v_hbm, out_hbm, idx_v, w_v, out_v, rows_v, sems):
        base = _sc_worker_id() * per_w

        def compute(tt, g, slot):
            splat = []
            for e in range(SC_LANES):
                s = plsc.load_gather(w_v, [jnp.full((SC_LANES,), tt, I32),
                                           jnp.full((SC_LANES,), g * SC_LANES + e, I32)])
                splat.append(plsc.pack(s, s, format=SC_FMT))

            @plsc.parallel_loop(0, dw // SC_LANES)
            def _(j):
                off = pl.multiple_of(j * SC_LANES, SC_LANES)
                acc_lo = out_v[tt, pl.ds(off, SC_LANES)]
                acc_hi = out_v[tt, pl.ds(dw + off, SC_LANES)]
                for e0 in range(0, SC_LANES, SC_BF16_GROUP):
                    s = None
                    for e in range(e0, e0 + SC_BF16_GROUP):
                        p = plsc.bitcast(rows_v[slot, e, pl.ds(off, SC_LANES)], BF16) * splat[e]
                        s = p if s is None else s + p
                    lo, hi = plsc.unpack(s, format=SC_FMT)
                    acc_lo = acc_lo + lo
                    acc_hi = acc_hi + hi
                out_v[tt, pl.ds(off, SC_LANES)] = acc_lo
                out_v[tt, pl.ds(dw + off, SC_LANES)] = acc_hi

        def chunk_body(c, carry):
            t0 = base + c * SC_TOK_CHUNK
            pltpu.sync_copy(idx_hbm.at[pl.ds(t0, SC_TOK_CHUNK)], idx_v)
            pltpu.sync_copy(w_hbm.at[pl.ds(t0, SC_TOK_CHUNK)], w_v)

            def zero_body(z, carry2):
                tt, j = z // n_vec, z % n_vec
                out_v[tt, pl.ds(pl.multiple_of(j * SC_LANES, SC_LANES), SC_LANES)] = (
                    jnp.zeros((SC_LANES,), F32))
                return carry2

            lax.fori_loop(0, SC_TOK_CHUNK * n_vec, zero_body, 0)
            _sc_row_pipeline(idx_v, v_hbm, rows_v, sems, SC_TOK_CHUNK * groups, groups, compute)
            pltpu.sync_copy(out_v, out_hbm.at[pl.ds(t0, SC_TOK_CHUNK)])
            return carry

        lax.fori_loop(0, n_chunks, chunk_body, 0)

    return k(w, idx, vp)


def kernel(x, norm_mix_g, norm_ffn_g, final_norm_g, rel_bias, even_w_in, even_w_out,
           diff_lambda, diff_ln_g, odd_w_in, odd_b_f, odd_w_out, peer_wq, peer_subkeys,
           peer_u, peer_v):
    batch, seq, d = x.shape

    dil_tiles = [dilated_bias_tile(rel_bias[:, :A_HEADS], w, dl) for w, dl in DILATED_BRANCHES]
    diff_tiles = diff_bias_tiles(rel_bias[:, A_HEADS:], seq)
    lam_init = 0.8 - 0.6 * math.exp(-0.3 * 0)
    even_in, even_out = even_w_in[0].astype(BF16), even_w_out[0].astype(BF16)
    w_in = odd_w_in[0]
    odd_in, odd_out = w_in[:, :3 * C_WIDTH].astype(BF16), odd_w_out[0].astype(BF16)
    w_gate = jnp.pad(w_in[:, 3 * C_WIDTH:], ((0, 0), (0, LANES - C_HEADS)))
    b_f = jnp.pad(odd_b_f[0], (0, LANES - C_HEADS)).reshape(1, LANES)
    peer = [dict(wq=peer_wq[l].astype(BF16),
                 sk=peer_subkeys[l].reshape(2 * PEER_HEADS, N_KEYS, PEER_KEY_HALF).astype(BF16),
                 up=pack_bf16_pairs(peer_u[l]), vp=pack_bf16_pairs(peer_v[l]))
            for l in range(2)]

    bg = batch // BATCH_GROUPS
    m = bg * seq
    groups = [dict(x=x[g * bg:(g + 1) * bg].reshape(m, d)) for g in range(BATCH_GROUPS)]

    def peer_select(st, layer, after=None):
        q, hp = norm_matmul(st["x"], norm_ffn_g[layer], peer[layer]["wq"], want_h="packed")
        idx, gate = peer_topk(q, peer[layer]["sk"])
        if after is not None and after["st"] is not st:
            after["st"]["peer"], idx = lax.optimization_barrier((after["st"]["peer"], idx))
        return dict(st=st, layer=layer, idx=idx, gate=gate,
                    act=peer_expert_dots(hp, idx, peer[layer]["up"]))

    def peer_combine(rec):
        rec["w"] = gelu_gate(rec["act"], rec["gate"])
        rec["st"]["peer"] = peer_expert_combine(rec["w"], rec["idx"], peer[rec["layer"]]["vp"])

    def even_layer(st, tie):
        (p,) = norm_matmul(st["x"], norm_mix_g[0], even_in)
        branches = [dilated_branch(p, tile, dl, bg, seq)
                    for tile, (_, dl) in zip(dil_tiles, DILATED_BRANCHES)]
        ob = tie(diff_attention(p.reshape(bg, seq, -1), diff_tiles, diff_lambda[0], diff_ln_g[0],
                                lam_init, bg, seq))
        st["x"] = even_out_proj(branches, ob.reshape(m, B_V_WIDTH), even_out, st["x"])

    def odd_layer(st, tie):
        p, st["x"], h = norm_matmul(st["x"], norm_mix_g[1], odd_in, res=st["peer"], want_h="f32")
        fg = gate_matmul(h, w_gate)
        c = logsig_cumsum(fg.reshape(bg, seq, LANES), b_f, bg, seq)
        ct = c[:, :, :C_HEADS].transpose(0, 2, 1)
        o = tie(fox_attention(p.reshape(bg, seq, -1), ct[:, :, :, None], ct[:, :, None, :],
                              bg, seq))
        st["x"] = matmul_residual(o.reshape(m, C_WIDTH), odd_out, st["x"])

    stages = [(even_layer, st, 0) for st in groups] + [(odd_layer, st, 1) for st in groups]
    recs = []
    for k, (mixer, st, layer) in enumerate(stages):
        def tie(t, prev=recs[k - 1] if k >= 1 else None):
            if prev is None:
                return t
            prev["act"], t = lax.optimization_barrier((prev["act"], t))
            peer_combine(prev)
            prev["w"], t = lax.optimization_barrier((prev["w"], t))
            return t

        mixer(st, tie)
        recs.append(peer_select(st, layer, after=recs[k - 2] if k >= 2 else None))
    peer_combine(recs[-1])
    outs = [add_norm(st["x"], st["peer"], final_norm_g).reshape(bg, seq, d) for st in groups]
    return jnp.concatenate(outs, axis=0)


def _gate_matmul_body(h_ref, w_ref, o_ref):
    o_ref[...] = jnp.dot(h_ref[...], w_ref[...], preferred_element_type=F32,
                         precision=lax.Precision.HIGHEST)


def gate_matmul(h, w, tm=512):
    m, d = h.shape
    n = w.shape[1]
    return pl.pallas_call(
        _gate_matmul_body,
        grid=(m // tm,),
        in_specs=[pl.BlockSpec((tm, d), lambda i: (i, 0)),
                  pl.BlockSpec((d, n), lambda i: (0, 0))],
        out_specs=pl.BlockSpec((tm, n), lambda i: (i, 0)),
        out_shape=jax.ShapeDtypeStruct((m, n), F32),
        compiler_params=pltpu.CompilerParams(dimension_semantics=("parallel",)),
        name="gate_matmul",
    )(h, w)
```

```python
import functools
import math

import numpy as np
import jax
import jax.numpy as jnp
from jax import lax
from jax.experimental import pallas as pl
from jax.experimental.pallas import tpu as pltpu
from jax.experimental.pallas import tpu_sc as plsc

F32 = jnp.float32
BF16 = jnp.bfloat16
I32 = jnp.int32

D_MODEL = 1024
HEAD_DIM = 64
A_HEADS = 8
DILATED_BRANCHES = ((128, 1), (512, 4), (2048, 16))
DIFF_HALF = 64
DIFF_VDIM = 128
B_HEADS = 4
C_HEADS = 16
N_BUCKETS = 32
MAX_DISTANCE = 2048
PEER_HEADS = 8
N_KEYS = 128
PEER_TOPK = 16
PEER_KEY_HALF = 128
RMS_EPS = 1e-6
NEG_INF = -1e30
A_WIDTH = A_HEADS * HEAD_DIM
B_QK_WIDTH = B_HEADS * 2 * DIFF_HALF
B_V_WIDTH = B_HEADS * DIFF_VDIM
C_WIDTH = C_HEADS * HEAD_DIM
QK_SCALE = 0.125
LANES = 128
DIL_BLOCK = 128
ATT_BLOCK = 512
BATCH_GROUPS = 8

SC_CORES = 2
SC_SUBCORES = 16
SC_LANES = 16
SC_WORKERS = SC_CORES * SC_SUBCORES


def _t5_bucket_table(n):
    max_exact = N_BUCKETS // 2
    d = np.arange(n)
    df = np.maximum(d, 1).astype(np.float32)
    large = max_exact + (
        np.log(df / np.float32(max_exact)) / np.float32(math.log(MAX_DISTANCE / max_exact))
        * np.float32(N_BUCKETS - max_exact)).astype(np.int32)
    large = np.minimum(large, N_BUCKETS - 1)
    return np.where(d < max_exact, d, large).astype(np.int32)


def _norm_matmul_body(*refs, has_res, want_h):
    it = iter(refs)
    x_ref = next(it)
    r_ref = next(it) if has_res else None
    g_ref = next(it)
    w_ref = next(it)
    o_ref = next(it)
    xs_ref = next(it) if has_res else None
    hout_ref = next(it) if want_h else None
    h_scr = next(it)

    @pl.when(pl.program_id(1) == 0)
    def _():
        x = x_ref[...]
        if has_res:
            x = x + r_ref[...]
            xs_ref[...] = x
        ms = jnp.mean(x * x, axis=-1, keepdims=True)
        h = x * lax.rsqrt(ms + RMS_EPS) * g_ref[...]
        if want_h == "f32":
            hout_ref[...] = h
        elif want_h == "packed":
            hout_ref[...] = pack_bf16_pairs(h)
        h_scr[...] = h.astype(BF16)

    o_ref[...] = jnp.dot(h_scr[...], w_ref[...],
                         preferred_element_type=F32).astype(o_ref.dtype)


def norm_matmul(x, g, w, *, res=None, want_h=None, out_dtype=BF16, tm=1024, tn=1024):
    m, d = x.shape
    n = w.shape[1]
    tn = min(tn, n)
    row = pl.BlockSpec((tm, d), lambda i, j: (i, 0))
    in_specs = [row] + ([row] if res is not None else []) + [
        pl.BlockSpec((1, d), lambda i, j: (0, 0)),
        pl.BlockSpec((d, tn), lambda i, j: (0, j))]
    out_specs = [pl.BlockSpec((tm, tn), lambda i, j: (i, j))]
    out_shape = [jax.ShapeDtypeStruct((m, n), out_dtype)]
    if res is not None:
        out_specs.append(row)
        out_shape.append(jax.ShapeDtypeStruct((m, d), F32))
    if want_h == "f32":
        out_specs.append(row)
        out_shape.append(jax.ShapeDtypeStruct((m, d), F32))
    elif want_h == "packed":
        out_specs.append(pl.BlockSpec((tm, d // 2), lambda i, j: (i, 0)))
        out_shape.append(jax.ShapeDtypeStruct((m, d // 2), I32))
    args =[x] + ([res] if res is not None else []) + [g.reshape(1, d), w]
    return pl.pallas_call(
        functools.partial(_norm_matmul_body, has_res=res is not None, want_h=want_h),
        grid=(m // tm, n // tn),
        in_specs=in_specs,
        out_specs=out_specs,
        out_shape=out_shape,
        scratch_shapes=[pltpu.VMEM((tm, d), BF16)],
        compiler_params=pltpu.CompilerParams(
            dimension_semantics=("parallel", "arbitrary")),
        name="norm_matmul",
    )(*args)


def _dilated_body(q_ref, kp_ref, kc_ref, vp_ref, vc_ref, b_ref, o_ref, lse_ref):
    i = pl.program_id(3)
    q = q_ref[...] * QK_SCALE
    k = jnp.concatenate([kp_ref[...], kc_ref[...]], axis=0)
    v = jnp.concatenate([vp_ref[...], vc_ref[...]], axis=0)
    col = lax.broadcasted_iota(I32, (DIL_BLOCK, 2 * DIL_BLOCK), 1)
    has_prev = jnp.logical_or(col >= DIL_BLOCK, i > 0)
    outs, lses = [], []
    for hh in range(2):
        sl = slice(hh * HEAD_DIM, (hh + 1) * HEAD_DIM)
        s = lax.dot_general(q[:, sl], k[:, sl], (((1,), (1,)), ((), ())),
                            preferred_element_type=F32)
        s = jnp.where(has_prev, s + b_ref[hh], NEG_INF)
        m = jnp.max(s, axis=-1, keepdims=True)
        p = jnp.exp(s - m)
        l = jnp.sum(p, axis=-1, keepdims=True)
        o = jnp.dot(p.astype(BF16), v[:, sl], preferred_element_type=F32) / l
        outs.append(o)
        lses.append(jnp.broadcast_to(m + jnp.log(l), (DIL_BLOCK, HEAD_DIM)))
    o_ref[...] = jnp.concatenate(outs, axis=1)
    lse_ref[...] = jnp.concatenate(lses, axis=1)


def dilated_branch(p, bias_tile, dil, batch, seq):
    if dil > 1:
        p = p[:, :3 * A_WIDTH]
    n_cols = p.shape[1]
    cb = n_cols // LANES
    rows = seq // dil
    nblk = rows // DIL_BLOCK
    pv = p.reshape(batch, rows, dil * n_cols)
    kq, kk, kv = 0, A_WIDTH // LANES, 2 * A_WIDTH // LANES
    blk = (None, DIL_BLOCK, LANES)

    def spec(col0, prev):
        if prev:
            return pl.BlockSpec(blk, lambda b, h, r, i: (b, jnp.maximum(i - 1, 0), r * cb + col0 + h))
        return pl.BlockSpec(blk, lambda b, h, r, i: (b, i, r * cb + col0 + h))

    ocb = A_WIDTH // LANES
    ospec = pl.BlockSpec(blk, lambda b, h, r, i: (b, i, r * ocb + h))
    oshape = jax.ShapeDtypeStruct((batch, rows, dil * A_WIDTH), F32)
    o, lse = pl.pallas_call(
        _dilated_body,
        grid=(batch, A_HEADS // 2, dil, nblk),
        in_specs=[spec(kq, False), spec(kk, True), spec(kk, False),
                  spec(kv, True), spec(kv, False),
                  pl.BlockSpec((2, DIL_BLOCK, 2 * DIL_BLOCK), lambda b, h, r, i: (h, 0, 0))],
        out_specs=[ospec, ospec],
        out_shape=[oshape, oshape],
        compiler_params=pltpu.CompilerParams(
            dimension_semantics=("parallel", "parallel", "parallel", "arbitrary")),
        name=f"dilated_d{dil}",
    )(pv, pv, pv, pv, pv, bias_tile)
    return o.reshape(batch * seq, A_WIDTH), lse.reshape(batch * seq, A_WIDTH)


def dilated_bias_tile(rel_bias_a, window, dil):
    n = window // dil
    assert n == DIL_BLOCK
    bucket = _t5_bucket_table(window + 1)
    period = 4 * n
    u = np.arange(period)
    valid = u <= n
    w = jnp.where(jnp.asarray(valid)[None],
                  rel_bias_a.T[:, bucket[np.where(valid, n - u, 0) * dil]], NEG_INF).astype(F32)
    rep = jnp.broadcast_to(w[:, None, :], (A_HEADS, n, period))
    flat = rep.reshape(A_HEADS, n * period)[:, :n * (period - 1)]
    return flat.reshape(A_HEADS, n, period - 1)[:, :, :2 * n]


def _diff_body(q_ref, k_ref, v_ref, b_ref, lam_ref, g_ref, o_ref,
               m_scr, l_scr, acc_scr, *, n_tiles, lam_init):
    t = ATT_BLOCK
    qi = pl.program_id(2)
    q = q_ref[...] * QK_SCALE
    lane = lax.broadcasted_iota(I32, (1, LANES), 1)
    qa = [jnp.where((lane < DIFF_HALF) == (a == 0), q, jnp.zeros_like(q)) for a in range(2)]
    ones = jnp.ones((t, LANES), BF16)
    for a in range(2):
        m_scr[a] = jnp.full((t, LANES), NEG_INF, F32)
        l_scr[a] = jnp.zeros((t, LANES), F32)
        acc_scr[a] = jnp.zeros((t, DIFF_VDIM), F32)

    def step(j, masked):
        off = pl.multiple_of(j * t, t)
        ks = k_ref[pl.ds(off, t), :]
        vs = v_ref[pl.ds(off, t), :]
        bias = b_ref[jnp.minimum(qi - j, n_tiles - 1)]
        if masked:
            row = lax.broadcasted_iota(I32, (t, t), 0)
            col = lax.broadcasted_iota(I32, (t, t), 1)
            causal = row >= col
        for a in range(2):
            s = lax.dot_general(qa[a], ks, (((1,), (1,)), ((), ())),
                                preferred_element_type=F32) + bias
            if masked:
                s = jnp.where(causal, s, NEG_INF)
            m_prev = m_scr[a]
            m_new = jnp.maximum(m_prev, jnp.max(s, axis=-1, keepdims=True))
            alpha = jnp.exp(m_prev - m_new)
            pb = jnp.exp(s - jnp.concatenate([m_new] * (t // LANES), axis=1)).astype(BF16)
            l_scr[a] = alpha * l_scr[a] + jnp.dot(pb, ones, preferred_element_type=F32)
            acc_scr[a] = alpha * acc_scr[a] + jnp.dot(pb, vs, preferred_element_type=F32)
            m_scr[a] = m_new

    def loop_body(j, carry):
        step(j, False)
        return carry

    lax.fori_loop(0, qi, loop_body, 0)
    step(qi, True)

    lp = lam_ref[...]
    lam = (jnp.exp(jnp.sum(lp[0:1] * lp[1:2])) - jnp.exp(jnp.sum(lp[2:3] * lp[3:4]))
           + lam_init)
    o = acc_scr[0] / l_scr[0] - lam * (acc_scr[1] / l_scr[1])
    ms = jnp.mean(o * o, axis=-1, keepdims=True)
    y = o * lax.rsqrt(ms + RMS_EPS) * g_ref[...]
    o_ref[...] = (y * (1.0 - lam_init)).astype(o_ref.dtype)


def diff_attention(p, bias_tiles, lam_params, ln_g, lam_init, batch, seq):
    t = ATT_BLOCK
    n_tiles = bias_tiles.shape[1]
    cq = 3 * A_WIDTH // LANES
    ck = cq + B_QK_WIDTH // LANES
    cv = ck + B_QK_WIDTH // LANES
    return pl.pallas_call(
        functools.partial(_diff_body, n_tiles=n_tiles, lam_init=lam_init),
        grid=(batch, B_HEADS, seq // t),
        in_specs=[
            pl.BlockSpec((None, t, LANES), lambda b, h, i: (b, i, cq + h)),
            pl.BlockSpec((None, seq, LANES), lambda b, h, i: (b, 0, ck + h)),
            pl.BlockSpec((None, seq, LANES), lambda b, h, i: (b, 0, cv + h)),
            pl.BlockSpec((None, n_tiles, t, t), lambda b, h, i: (h, 0, 0, 0)),
            pl.BlockSpec((4, DIFF_HALF), lambda b, h, i: (0, 0)),
            pl.BlockSpec((1, DIFF_VDIM), lambda b, h, i: (0, 0)),
        ],
        out_specs=pl.BlockSpec((None, t, LANES), lambda b, h, i: (b, i, h)),
        out_shape=jax.ShapeDtypeStruct((batch, seq, B_V_WIDTH), BF16),
        scratch_shapes=[pltpu.VMEM((2, t, LANES), F32), pltpu.VMEM((2, t, LANES), F32),
                        pltpu.VMEM((2, t, DIFF_VDIM), F32)],
        compiler_params=pltpu.CompilerParams(
            dimension_semantics=("parallel", "parallel", "arbitrary")),
        name="diff_attention",
    )(p, p, p, bias_tiles, lam_params, ln_g.reshape(1, DIFF_VDIM))


def diff_bias_tiles(rel_bias_b, seq):
    t = ATT_BLOCK
    bucket = _t5_bucket_table(max(seq, 2 * MAX_DISTANCE) + 2 * t)
    sat = bucket[-1]
    d_sat = int(np.max(np.nonzero(bucket != sat)[0])) + 1
    n_full = (d_sat + t - 1 + t - 1) // t
    n_tiles = n_full + 1
    assert n_full * t - (t - 1) >= d_sat
    n = np.arange(2 * t)[None, :]
    base = np.arange(n_tiles)[:, None] * t
    dist = np.clip(np.where(n < t, base - n, base + 2 * t - n), 0, None)
    w = rel_bias_b.T[:, bucket[dist]].astype(F32)
    rep = jnp.broadcast_to(w[:, :, None, :], (B_HEADS, n_tiles, t, 2 * t))
    flat = rep.reshape(B_HEADS, n_tiles, 2 * t * t)[:, :, :t * (2 * t - 1)]
    return flat.reshape(B_HEADS, n_tiles, t, 2 * t - 1)[:, :, :, :t]


def _logsig_cumsum_body(f_ref, b_ref, c_ref, carry_scr):
    t = f_ref.shape[0]

    @pl.when(pl.program_id(1) == 0)
    def _():
        carry_scr[...] = jnp.zeros_like(carry_scr)

    x = f_ref[...] + b_ref[...]
    ls = jnp.minimum(x, 0.0) - jnp.log1p(jnp.exp(-jnp.abs(x)))
    row = lax.broadcasted_iota(I32, (t, t), 0)
    col = lax.broadcasted_iota(I32, (t, t), 1)
    tri = (row >= col).astype(F32)
    c = jnp.dot(tri, ls, preferred_element_type=F32,
                precision=lax.Precision.HIGHEST) + carry_scr[...]
    c_ref[...] = c
    carry_scr[...] = c[t - 1:t, :]


def logsig_cumsum(fg, b_f, batch, seq, t=512):
    return pl.pallas_call(
        _logsig_cumsum_body,
        grid=(batch, seq // t),
        in_specs=[pl.BlockSpec((None, t, LANES), lambda b, i: (b, i, 0)),
                  pl.BlockSpec((1, LANES), lambda b, i: (0, 0))],
        out_specs=pl.BlockSpec((None, t, LANES), lambda b, i: (b, i, 0)),
        out_shape=jax.ShapeDtypeStruct((batch, seq, LANES), F32),
        scratch_shapes=[pltpu.VMEM((1, LANES), F32)],
        compiler_params=pltpu.CompilerParams(
            dimension_semantics=("parallel", "arbitrary")),
        name="logsig_cumsum",
    )(fg, b_f)


def _fox_body(q_ref, k_ref, v_ref, cq_ref, ck_ref, o_ref, m_scr, acc_scr):
    t = ATT_BLOCK
    qi = pl.program_id(2)
    q = q_ref[...] * QK_SCALE
    lane = lax.broadcasted_iota(I32, (1, LANES), 1)
    own = [(lane < HEAD_DIM) == (a == 0) for a in range(2)]
    qa = [jnp.where(own[a], q, jnp.zeros_like(q)) for a in range(2)]
    cqb = [jnp.broadcast_to(cq_ref[a], (t, LANES)) for a in range(2)]
    for a in range(2):
        m_scr[a] = jnp.full((t, LANES), NEG_INF, F32)
        acc_scr[a] = jnp.zeros((t, LANES), F32)

    def step(j, masked):
        off = pl.multiple_of(j * t, t)
        ks = k_ref[pl.ds(off, t), :]
        vs = v_ref[pl.ds(off, t), :]
        if masked:
            row = lax.broadcasted_iota(I32, (t, t), 0)
            col = lax.broadcasted_iota(I32, (t, t), 1)
            causal = row >= col
        for a in range(2):
            s = lax.dot_general(qa[a], ks, (((1,), (1,)), ((), ())),
                                preferred_element_type=F32)
            s = s - ck_ref[a, :, pl.ds(off, t)]
            if masked:
                s = jnp.where(causal, s, NEG_INF)
            m_prev = m_scr[a]
            m_new = jnp.maximum(m_prev, jnp.max(s, axis=-1, keepdims=True) + cqb[a])
            alpha = jnp.exp(m_prev - m_new)
            shift = m_new - cqb[a]
            pb = jnp.exp(s - jnp.concatenate([shift] * (t // LANES), axis=1)).astype(BF16)
            v_aug = jnp.where(own[a], vs, jnp.ones_like(vs))
            acc_scr[a] = alpha * acc_scr[a] + jnp.dot(pb, v_aug, preferred_element_type=F32)
            m_scr[a] = m_new

    def loop_body(j, carry):
        step(j, False)
        return carry

    lax.fori_loop(0, qi, loop_body, 0)
    step(qi, True)
    r = [acc_scr[a] / pltpu.roll(acc_scr[a], HEAD_DIM, 1) for a in range(2)]
    o_ref[...] = jnp.where(own[0], r[0], r[1]).astype(o_ref.dtype)


def fox_attention(p, cq, ck, batch, seq):
    t = ATT_BLOCK
    nk = C_WIDTH // LANES
    return pl.pallas_call(
        _fox_body,
        grid=(batch, C_HEADS // 2, seq // t),
        in_specs=[
            pl.BlockSpec((None, t, LANES), lambda b, h, i: (b, i, h)),
            pl.BlockSpec((None, seq, LANES), lambda b, h, i: (b, 0, nk + h)),
            pl.BlockSpec((None, seq, LANES), lambda b, h, i: (b, 0, 2 * nk + h)),
            pl.BlockSpec((None, 2, t, 1), lambda b, h, i: (b, h, i, 0)),
            pl.BlockSpec((None, 2, 1, seq), lambda b, h, i: (b, h, 0, 0)),
        ],
        out_specs=pl.BlockSpec((None, t, LANES), lambda b, h, i: (b, i, h)),
        out_shape=jax.ShapeDtypeStruct((batch, seq, C_WIDTH), BF16),
        scratch_shapes=[pltpu.VMEM((2, t, LANES), F32), pltpu.VMEM((2, t, LANES), F32)],
        compiler_params=pltpu.CompilerParams(
            dimension_semantics=("parallel", "parallel", "arbitrary")),
        name="fox_attention",
    )(p, p, p, cq, ck)


def _even_out_body(o1, l1, o2, l2, o3, l3, ob_ref, w_ref, x_ref, out_ref, a_scr):
    @pl.when(pl.program_id(1) == 0)
    def _():
        a1, a2, a3 = l1[...], l2[...], l3[...]
        mx = jnp.maximum(jnp.maximum(a1, a2), a3)
        e1, e2, e3 = jnp.exp(a1 - mx), jnp.exp(a2 - mx), jnp.exp(a3 - mx)
        oa = (e1 * o1[...] + e2 * o2[...] + e3 * o3[...]) / (e1 + e2 + e3)
        a_scr[:, :A_WIDTH] = oa.astype(BF16)
        a_scr[:, A_WIDTH:] = ob_ref[...]

    out_ref[...] = x_ref[...] + jnp.dot(a_scr[...], w_ref[...], preferred_element_type=F32)


def even_out_proj(branches, ob, w, x, tm=512, tn=1024):
    m, d = x.shape
    half = pl.BlockSpec((tm, A_WIDTH), lambda i, j: (i, 0))
    flat = [a for pair in branches for a in pair]
    return pl.pallas_call(
        _even_out_body,
        grid=(m // tm, d // tn),
        in_specs=[half] * 6 + [
            pl.BlockSpec((tm, B_V_WIDTH), lambda i, j: (i, 0)),
            pl.BlockSpec((A_WIDTH + B_V_WIDTH, tn), lambda i, j: (0, j)),
            pl.BlockSpec((tm, tn), lambda i, j: (i, j))],
        out_specs=pl.BlockSpec((tm, tn), lambda i, j: (i, j)),
        out_shape=jax.ShapeDtypeStruct((m, d), F32),
        scratch_shapes=[pltpu.VMEM((tm, A_WIDTH + B_V_WIDTH), BF16)],
        compiler_params=pltpu.CompilerParams(
            dimension_semantics=("parallel", "arbitrary")),
        name="even_out_proj",
    )(*flat, ob, w, x)


def _matmul_res_body(a_ref, w_ref, x_ref, o_ref):
    o_ref[...] = x_ref[...] + jnp.dot(a_ref[...], w_ref[...], preferred_element_type=F32)


def matmul_residual(a, w, x, tm=1024, tn=1024):
    m, k = a.shape
    n = w.shape[1]
    return pl.pallas_call(
        _matmul_res_body,
        grid=(m // tm, n // tn),
        in_specs=[pl.BlockSpec((tm, k), lambda i, j: (i, 0)),
                  pl.BlockSpec((k, tn), lambda i, j: (0, j)),
                  pl.BlockSpec((tm, tn), lambda i, j: (i, j))],
        out_specs=pl.BlockSpec((tm, tn), lambda i, j: (i, j)),
        out_shape=jax.ShapeDtypeStruct((m, n), F32),
        compiler_params=pltpu.CompilerParams(
            dimension_semantics=("parallel", "parallel")),
        name="matmul_residual",
    )(a, w, x)


def _add_norm_body(x_ref, r_ref, g_ref, o_ref):
    x = x_ref[...] + r_ref[...]
    ms = jnp.mean(x * x, axis=-1, keepdims=True)
    o_ref[...] = x * lax.rsqrt(ms + RMS_EPS) * g_ref[...]


def add_norm(x, r, g, tm=512):
    m, d = x.shape
    row = pl.BlockSpec((tm, d), lambda i: (i, 0))
    return pl.pallas_call(
        _add_norm_body,
        grid=(m // tm,),
        in_specs=[row, row, pl.BlockSpec((1, d), lambda i: (0, 0))],
        out_specs=row,
        out_shape=jax.ShapeDtypeStruct((m, d), F32),
        compiler_params=pltpu.CompilerParams(dimension_semantics=("parallel",)),
        name="add_norm",
    )(x, r, g.reshape(1, d))


def _peer_candidates():
    groups = [("a", 0, 0), ("a", 0, 8), ("a", 1, 0), ("b", 0, 8), ("a", 2, 0), ("a", 3, 0),
              ("b", 0, 0), ("b", 1, 0), ("b", 2, 0)]
    cid = np.zeros((8 * len(groups), 1), np.int32)
    seen = set()
    for g, (kind, fixed, start) in enumerate(groups):
        for r in range(8):
            a, b = (fixed, start + r) if kind == "a" else (start + r, fixed)
            row = 8 * g + r
            if (a + 1) * (b + 1) <= PEER_TOPK and (a, b) not in seen:
                seen.add((a, b))
                cid[row, 0] = a * PEER_TOPK + b
            else:
                cid[row, 0] = PEER_TOPK * PEER_TOPK + row
    assert len(seen) == sum((a + 1) * (b + 1) <= PEER_TOPK
                            for a in range(PEER_TOPK) for b in range(PEER_TOPK))
    return groups, cid


def _peer_topk_body(q_ref, sk_ref, cid_ref, idx_ref, gate_ref, ts_scr, ti_scr,
                    bs_scr, be_scr, cv_scr, ce_scr, sc_scr, *, groups):
    tm = q_ref.shape[0]
    neg_inf = jnp.float32(-jnp.inf)
    key_id = lax.broadcasted_iota(I32, (1, N_KEYS, tm), 1)
    batch = sc_scr.shape[0]

    def group_body(g, carry):
        for j in range(batch):
            pr = g * batch + j
            off = pl.multiple_of(pr * PEER_KEY_HALF, PEER_KEY_HALF)
            sc_scr[j] = lax.dot_general(sk_ref[pr], q_ref[:, pl.ds(off, PEER_KEY_HALF)],
                                        (((1,), (1,)), ((), ())), preferred_element_type=F32)
        rows = pl.ds(g * batch, batch)

        def k_body(k, c):
            vals = sc_scr[...]
            m = jnp.max(vals, axis=1, keepdims=True)
            sel = jnp.min(jnp.where(vals == m, key_id, N_KEYS), axis=1, keepdims=True)
            ts_scr[rows, pl.ds(k, 1), :] = m
            ti_scr[rows, pl.ds(k, 1), :] = sel
            sc_scr[...] = jnp.where(key_id == sel, neg_inf, vals)
            return c

        lax.fori_loop(0, PEER_TOPK, k_body, 0)
        return carry

    lax.fori_loop(0, 2 * PEER_HEADS // batch, group_body, 0)

    cand_id = cid_ref[...]
    pad = jnp.where(cand_id < PEER_TOPK * PEER_TOPK, 0.0, neg_inf)

    def pick(x1, x2):
        return jnp.concatenate(
            [x1[f:f + 1] + x2[s:s + 8] if kind == "a" else x1[s:s + 8] + x2[f:f + 1]
             for kind, f, s in groups], axis=0)

    def head_body(h, carry):
        cv_scr[h] = pick(ts_scr[2 * h], ts_scr[2 * h + 1]) + pad
        ce_scr[h] = pick(ti_scr[2 * h] * N_KEYS, ti_scr[2 * h + 1])
        return carry

    lax.fori_loop(0, PEER_HEADS, head_body, 0)

    cid3 = cand_id[None]

    def k_body(k, carry):
        vals = cv_scr[...]
        m = jnp.max(vals, axis=1, keepdims=True)
        sel = jnp.min(jnp.where(vals == m, cid3, PEER_TOPK * PEER_TOPK), axis=1, keepdims=True)
        hit = cid3 == sel
        bs_scr[:, pl.ds(k, 1), :] = m
        be_scr[:, pl.ds(k, 1), :] = jnp.sum(jnp.where(hit, ce_scr[...], 0), axis=1, keepdims=True)
        cv_scr[...] = jnp.where(hit, neg_inf, vals)
        return carry

    lax.fori_loop(0, PEER_TOPK, k_body, 0)
    bs = bs_scr[...]
    e = jnp.exp(bs - jnp.max(bs, axis=1, keepdims=True))
    gate = e / jnp.sum(e, axis=1, keepdims=True)
    n_sel = PEER_HEADS * PEER_TOPK
    gate_ref[...] = gate.reshape(n_sel, tm).T
    idx_ref[...] = be_scr[...].reshape(n_sel, tm).T


def peer_topk(q, subkeys, tm=256):
    m = q.shape[0]
    n_sel = PEER_HEADS * PEER_TOPK
    groups, cid = _peer_candidates()
    out_spec = pl.BlockSpec((tm, n_sel), lambda i: (i, 0))
    return pl.pallas_call(
        functools.partial(_peer_topk_body, groups=groups),
        grid=(m // tm,),
        in_specs=[pl.BlockSpec((tm, q.shape[1]), lambda i: (i, 0)),
                  pl.BlockSpec(subkeys.shape, lambda i: (0, 0, 0)),
                  pl.BlockSpec(cid.shape, lambda i: (0, 0))],
        out_specs=[out_spec, out_spec],
        out_shape=[jax.ShapeDtypeStruct((m, n_sel), I32),
                   jax.ShapeDtypeStruct((m, n_sel), F32)],
        scratch_shapes=[pltpu.VMEM((2 * PEER_HEADS, PEER_TOPK, tm), F32),
                        pltpu.VMEM((2 * PEER_HEADS, PEER_TOPK, tm), I32),
                        pltpu.VMEM((PEER_HEADS, PEER_TOPK, tm), F32),
                        pltpu.VMEM((PEER_HEADS, PEER_TOPK, tm), I32),
                        pltpu.VMEM((PEER_HEADS, cid.shape[0], tm), F32),
                        pltpu.VMEM((PEER_HEADS, cid.shape[0], tm), I32),
                        pltpu.VMEM((PEER_HEADS, N_KEYS, tm), F32)],
        compiler_params=pltpu.CompilerParams(dimension_semantics=("parallel",)),
        name="peer_topk",
    )(q, subkeys, jnp.asarray(cid))


def _gelu_gate_body(a_ref, g_ref, o_ref):
    a = a_ref[...]
    o_ref[...] = g_ref[...] * (0.5 * a * (1.0 + lax.erf(a * (2.0 ** -0.5))))


def gelu_gate(act, gate, tm=2048):
    m, n = act.shape
    spec = pl.BlockSpec((tm, n), lambda i: (i, 0))
    return pl.pallas_call(
        _gelu_gate_body,
        grid=(m // tm,),
        in_specs=[spec, spec],
        out_specs=spec,
        out_shape=jax.ShapeDtypeStruct((m, n), F32),
        compiler_params=pltpu.CompilerParams(dimension_semantics=("parallel",)),
        name="gelu_gate",
    )(act, gate)


SC_TOK_CHUNK = 32
SC_RING = 8
SC_BF16_GROUP = 4
SC_FMT = plsc.PackFormat.INTERLEAVED


def _sc_worker_id():
    return lax.axis_index("s") * SC_CORES + lax.axis_index("c")


def pack_bf16_pairs(t):
    half = t.shape[-1] // 2
    bits = lax.bitcast_convert_type(t.astype(BF16).astype(F32), I32)
    return (bits[..., half:] & jnp.int32(-65536)) | lax.shift_right_logical(
        bits[..., :half], jnp.int32(16))


def _sc_row_pipeline(idx_v, table_hbm, rows_v, sems, n_items, groups, compute):
    def gather(item):
        tt, g = item // groups, item % groups
        ids = idx_v[tt, pl.ds(g * SC_LANES, SC_LANES)]
        slot = item % SC_RING
        return pltpu.make_async_copy(table_hbm.at[ids], rows_v.at[slot], sems.at[slot])

    for s in range(SC_RING - 1):
        gather(s).start()

    def item_body(item, carry):
        nxt = item + SC_RING - 1

        @pl.when(nxt < n_items)
        def _():
            gather(nxt).start()

        gather(item).wait()
        compute(item // groups, item % groups, item % SC_RING)
        return carry

    lax.fori_loop(0, n_items, item_body, 0)


def peer_expert_dots(hp, idx, up):
    m, dw = hp.shape
    n_sel = idx.shape[1]
    per_w = m // SC_WORKERS
    n_chunks = per_w // SC_TOK_CHUNK
    groups = n_sel // SC_LANES
    step = SC_BF16_GROUP * SC_LANES
    mesh = plsc.VectorSubcoreMesh(core_axis_name="c", subcore_axis_name="s")

    @functools.partial(
        pl.kernel, mesh=mesh,
        out_type=jax.ShapeDtypeStruct((m, n_sel), F32),
        scratch_types=[
            pltpu.VMEM((SC_TOK_CHUNK, n_sel), I32),
            pltpu.VMEM((SC_TOK_CHUNK, dw), I32),
            pltpu.VMEM((SC_TOK_CHUNK, n_sel), F32),
            pltpu.VMEM((SC_RING, SC_LANES, dw), I32),
            pltpu.VMEM((SC_LANES * SC_LANES,), F32),
            pltpu.SemaphoreType.DMA((SC_RING,)),
        ],
        compiler_params=pltpu.CompilerParams(needs_layout_passes=False),
        name="peer_expert_dots",
    )
    def k(h_hbm, idx_hbm, u_hbm, act_hbm, idx_v, h_v, act_v, rows_v, part_v, sems):
        base = _sc_worker_id() * per_w
        lane = lax.broadcasted_iota(I32, (SC_LANES,), 0)

        def compute(tt, g, slot):
            def grp_body(q, accs):
                off = pl.multiple_of(q * step, step)
                xs = [plsc.bitcast(h_v[tt, pl.ds(off + c * SC_LANES, SC_LANES)], BF16)
                      for c in range(SC_BF16_GROUP)]
                new = []
                for e in range(SC_LANES):
                    s = None
                    for c in range(SC_BF16_GROUP):
                        p = plsc.bitcast(
                            rows_v[slot, e, pl.ds(off + c * SC_LANES, SC_LANES)], BF16) * xs[c]
                        s = p if s is None else s + p
                    lo, hi = plsc.unpack(s, format=SC_FMT)
                    new.append(accs[e] + (lo + hi))
                return tuple(new)

            accs = lax.fori_loop(
                0, dw // step, grp_body,
                tuple(jnp.zeros((SC_LANES,), F32) for _ in range(SC_LANES)))
            for e in range(SC_LANES):
                part_v[pl.ds(e * SC_LANES, SC_LANES)] = accs[e]
            tot = jnp.zeros((SC_LANES,), F32)
            for l in range(SC_LANES):
                tot = tot + plsc.load_gather(part_v, [lane * SC_LANES + l])
            act_v[tt, pl.ds(g * SC_LANES, SC_LANES)] = tot

        def chunk_body(c, carry):
            t0 = base + c * SC_TOK_CHUNK
            pltpu.sync_copy(idx_hbm.at[pl.ds(t0, SC_TOK_CHUNK)], idx_v)
            pltpu.sync_copy(h_hbm.at[pl.ds(t0, SC_TOK_CHUNK)], h_v)
            _sc_row_pipeline(idx_v, u_hbm, rows_v, sems, SC_TOK_CHUNK * groups, groups, compute)
            pltpu.sync_copy(act_v, act_hbm.at[pl.ds(t0, SC_TOK_CHUNK)])
            return carry

        lax.fori_loop(0, n_chunks, chunk_body, 0)

    return k(hp, idx, up)


def peer_expert_combine(w, idx, vp):
    m, n_sel = w.shape
    dw = vp.shape[1]
    d = 2 * dw
    per_w = m // SC_WORKERS
    n_chunks = per_w // SC_TOK_CHUNK
    n_vec = d // SC_LANES
    groups = n_sel // SC_LANES
    mesh = plsc.VectorSubcoreMesh(core_axis_name="c", subcore_axis_name="s")

    @functools.partial(
        pl.kernel, mesh=mesh,
        out_type=jax.ShapeDtypeStruct((m, d), F32),
        scratch_types=[
            pltpu.VMEM((SC_TOK_CHUNK, n_sel), I32),
            pltpu.VMEM((SC_TOK_CHUNK, n_sel), F32),
            pltpu.VMEM((SC_TOK_CHUNK, d), F32),
            pltpu.VMEM((SC_RING, SC_LANES, dw), I32),
            pltpu.SemaphoreType.DMA((SC_RING,)),
        ],
        compiler_params=pltpu.CompilerParams(needs_layout_passes=False),
        name="peer_expert_combine",
    )
    def k(w_hbm, idx_hbm, v_hbm, out_hbm, idx_v, w_v, out_v, rows_v, sems):
        base = _sc_worker_id() * per_w

        def compute(tt, g, slot):
            splat = []
            for e in range(SC_LANES):
                s = plsc.load_gather(w_v, [jnp.full((SC_LANES,), tt, I32),
                                           jnp.full((SC_LANES,), g * SC_LANES + e, I32)])
                splat.append(plsc.pack(s, s, format=SC_FMT))

            @plsc.parallel_loop(0, dw // SC_LANES)
            def _(j):
                off = pl.multiple_of(j * SC_LANES, SC_LANES)
                acc_lo = out_v[tt, pl.ds(off, SC_LANES)]
                acc_hi = out_v[tt, pl.ds(dw + off, SC_LANES)]
                for e0 in range(0, SC_LANES, SC_BF16_GROUP):
                    s = None
                    for e in range(e0, e0 + SC_BF16_GROUP):
                        p = plsc.bitcast(rows_v[slot, e, pl.ds(off, SC_LANES)], BF16) * splat[e]
                        s = p if s is None else s + p
                    lo, hi = plsc.unpack(s, format=SC_FMT)
                    acc_lo = acc_lo + lo
                    acc_hi = acc_hi + hi
                out_v[tt, pl.ds(off, SC_LANES)] = acc_lo
                out_v[tt, pl.ds(dw + off, SC_LANES)] = acc_hi

        def chunk_body(c, carry):
            t0 = base + c * SC_TOK_CHUNK
            pltpu.sync_copy(idx_hbm.at[pl.ds(t0, SC_TOK_CHUNK)], idx_v)
            pltpu.sync_copy(w_hbm.at[pl.ds(t0, SC_TOK_CHUNK)], w_v)

            def zero_body(z, carry2):
                tt, j = z // n_vec, z % n_vec
                out_v[tt, pl.ds(pl.multiple_of(j * SC_LANES, SC_LANES), SC_LANES)] = (
                    jnp.zeros((SC_LANES,), F32))
                return carry2

            lax.fori_loop(0, SC_TOK_CHUNK * n_vec, zero_body, 0)
            _sc_row_pipeline(idx_v, v_hbm, rows_v, sems, SC_TOK_CHUNK * groups, groups, compute)
            pltpu.sync_copy(out_v, out_hbm.at[pl.ds(t0, SC_TOK_CHUNK)])
            return carry

        lax.fori_loop(0, n_chunks, chunk_body, 0)

    return k(w, idx, vp)


def kernel(x, norm_mix_g, norm_ffn_g, final_norm_g, rel_bias, even_w_in, even_w_out,
           diff_lambda, diff_ln_g, odd_w_in, odd_b_f, odd_w_out, peer_wq, peer_subkeys,
           peer_u, peer_v):
    batch, seq, d = x.shape

    dil_tiles = [dilated_bias_tile(rel_bias[:, :A_HEADS], w, dl) for w, dl in DILATED_BRANCHES]
    diff_tiles = diff_bias_tiles(rel_bias[:, A_HEADS:], seq)
    lam_init = 0.8 - 0.6 * math.exp(-0.3 * 0)
    even_in, even_out = even_w_in[0].astype(BF16), even_w_out[0].astype(BF16)
    w_in = odd_w_in[0]
    odd_in, odd_out = w_in[:, :3 * C_WIDTH].astype(BF16), odd_w_out[0].astype(BF16)
    w_gate = jnp.pad(w_in[:, 3 * C_WIDTH:], ((0, 0), (0, LANES - C_HEADS)))
    b_f = jnp.pad(odd_b_f[0], (0, LANES - C_HEADS)).reshape(1, LANES)
    peer = [dict(wq=peer_wq[l].astype(BF16),
                 sk=peer_subkeys[l].reshape(2 * PEER_HEADS, N_KEYS, PEER_KEY_HALF).astype(BF16),
                 up=pack_bf16_pairs(peer_u[l]), vp=pack_bf16_pairs(peer_v[l]))
            for l in range(2)]

    bg = batch // BATCH_GROUPS
    m = bg * seq
    groups = [dict(x=x[g * bg:(g + 1) * bg].reshape(m, d)) for g in range(BATCH_GROUPS)]

    def peer_select(st, layer, after=None):
        q, hp = norm_matmul(st["x"], norm_ffn_g[layer], peer[layer]["wq"], want_h="packed")
        idx, gate = peer_topk(q, peer[layer]["sk"])
        if after is not None and after["st"] is not st:
            after["st"]["peer"], idx = lax.optimization_barrier((after["st"]["peer"], idx))
        return dict(st=st, layer=layer, idx=idx, gate=gate,
                    act=peer_expert_dots(hp, idx, peer[layer]["up"]))

    def peer_combine(rec):
        rec["w"] = gelu_gate(rec["act"], rec["gate"])
        rec["st"]["peer"] = peer_expert_combine(rec["w"], rec["idx"], peer[rec["layer"]]["vp"])

    def even_layer(st, tie):
        (p,) = norm_matmul(st["x"], norm_mix_g[0], even_in)
        branches = [dilated_branch(p, tile, dl, bg, seq)
                    for tile, (_, dl) in zip(dil_tiles, DILATED_BRANCHES)]
        ob = tie(diff_attention(p.reshape(bg, seq, -1), diff_tiles, diff_lambda[0], diff_ln_g[0],
                                lam_init, bg, seq))
        st["x"] = even_out_proj(branches, ob.reshape(m, B_V_WIDTH), even_out, st["x"])

    def odd_layer(st, tie):
        p, st["x"], h = norm_matmul(st["x"], norm_mix_g[1], odd_in, res=st["peer"], want_h="f32")
        fg = gate_matmul(h, w_gate)
        c = logsig_cumsum(fg.reshape(bg, seq, LANES), b_f, bg, seq)
        ct = c[:, :, :C_HEADS].transpose(0, 2, 1)
        o = tie(fox_attention(p.reshape(bg, seq, -1), ct[:, :, :, None], ct[:, :, None, :],
                              bg, seq))
        st["x"] = matmul_residual(o.reshape(m, C_WIDTH), odd_out, st["x"])

    stages = [(even_layer, st, 0) for st in groups] + [(odd_layer, st, 1) for st in groups]
    recs = []
    for k, (mixer, st, layer) in enumerate(stages):
        def tie(t, prev=recs[k - 1] if k >= 1 else None):
            if prev is None:
                return t
            prev["act"], t = lax.optimization_barrier((prev["act"], t))
            peer_combine(prev)
            prev["w"], t = lax.optimization_barrier((prev["w"], t))
            return t

        mixer(st, tie)
        recs.append(peer_select(st, layer, after=recs[k - 2] if k >= 2 else None))
    peer_combine(recs[-1])
    outs = [add_norm(st["x"], st["peer"], final_norm_g).reshape(bg, seq, d) for st in groups]
    return jnp.concatenate(outs, axis=0)


def _gate_matmul_body(h_ref, w_ref, o_ref):
    o_ref[...] = jnp.dot(h_ref[...], w_ref[...], preferred_element_type=F32,
                         precision=lax.Precision.HIGHEST)


def gate_matmul(h, w, tm=512):
    m, d = h.shape
    n = w.shape[1]
    return pl.pallas_call(
        _gate_matmul_body,
        grid=(m // tm,),
        in_specs=[pl.BlockSpec((tm, d), lambda i: (i, 0)),
                  pl.BlockSpec((d, n), lambda i: (0, 0))],
        out_specs=pl.BlockSpec((tm, n), lambda i: (i, 0)),
        out_shape=jax.ShapeDtypeStruct((m, n), F32),
        compiler_params=pltpu.CompilerParams(dimension_semantics=("parallel",)),
        name="gate_matmul",
    )(h, w)
```

```python
import functools
import math

import numpy as np
import jax
import jax.numpy as jnp
from jax import lax
from jax.experimental import pallas as pl
from jax.experimental.pallas import tpu as pltpu
from jax.experimental.pallas import tpu_sc as plsc

F32 = jnp.float32
BF16 = jnp.bfloat16
I32 = jnp.int32

D_MODEL = 1024
HEAD_DIM = 64
A_HEADS = 8
DILATED_BRANCHES = ((128, 1), (512, 4), (2048, 16))
DIFF_HALF = 64
DIFF_VDIM = 128
B_HEADS = 4
C_HEADS = 16
N_BUCKETS = 32
MAX_DISTANCE = 2048
PEER_HEADS = 8
N_KEYS = 128
PEER_TOPK = 16
PEER_KEY_HALF = 128
RMS_EPS = 1e-6
NEG_INF = -1e30
A_WIDTH = A_HEADS * HEAD_DIM
B_QK_WIDTH = B_HEADS * 2 * DIFF_HALF
B_V_WIDTH = B_HEADS * DIFF_VDIM
C_WIDTH = C_HEADS * HEAD_DIM
QK_SCALE = 0.125
LANES = 128
DIL_BLOCK = 128
ATT_BLOCK = 512
BATCH_GROUPS = 8

SC_CORES = 2
SC_SUBCORES = 16
SC_LANES = 16
SC_WORKERS = SC_CORES * SC_SUBCORES


def _t5_bucket_table(n):
    max_exact = N_BUCKETS // 2
    d = np.arange(n)
    df = np.maximum(d, 1).astype(np.float32)
    large = max_exact + (
        np.log(df / np.float32(max_exact)) / np.float32(math.log(MAX_DISTANCE / max_exact))
        * np.float32(N_BUCKETS - max_exact)).astype(np.int32)
    large = np.minimum(large, N_BUCKETS - 1)
    return np.where(d < max_exact, d, large).astype(np.int32)


def _norm_matmul_body(*refs, has_res, want_h):
    it = iter(refs)
    x_ref = next(it)
    r_ref = next(it) if has_res else None
    g_ref = next(it)
    w_ref = next(it)
    o_ref = next(it)
    xs_ref = next(it) if has_res else None
    hout_ref = next(it) if want_h else None
    h_scr = next(it)

    @pl.when(pl.program_id(1) == 0)
    def _():
        x = x_ref[...]
        if has_res:
            x = x + r_ref[...]
            xs_ref[...] = x
        ms = jnp.mean(x * x, axis=-1, keepdims=True)
        h = x * lax.rsqrt(ms + RMS_EPS) * g_ref[...]
        if want_h == "f32":
            hout_ref[...] = h
        elif want_h == "packed":
            hout_ref[...] = pack_bf16_pairs(h)
        h_scr[...] = h.astype(BF16)

    o_ref[...] = jnp.dot(h_scr[...], w_ref[...],
                         preferred_element_type=F32).astype(o_ref.dtype)


def norm_matmul(x, g, w, *, res=None, want_h=None, out_dtype=BF16, tm=1024, tn=1024):
    m, d = x.shape
    n = w.shape[1]
    tn = min(tn, n)
    row = pl.BlockSpec((tm, d), lambda i, j: (i, 0))
    in_specs = [row] + ([row] if res is not None else []) + [
        pl.BlockSpec((1, d), lambda i, j: (0, 0)),
        pl.BlockSpec((d, tn), lambda i, j: (0, j))]
    out_specs = [pl.BlockSpec((tm, tn), lambda i, j: (i, j))]
    out_shape = [jax.ShapeDtypeStruct((m, n), out_dtype)]
    if res is not None:
        out_specs.append(row)
        out_shape.append(jax.ShapeDtypeStruct((m, d), F32))
    if want_h == "f32":
        out_specs.append(row)
        out_shape.append(jax.ShapeDtypeStruct((m, d), F32))
    elif want_h == "packed":
        out_specs.append(pl.BlockSpec((tm, d // 2), lambda i, j: (i, 0)))
        out_shape.append(jax.ShapeDtypeStruct((m, d // 2), I32))
    args =[x] + ([res] if res is not None else []) + [g.reshape(1, d), w]
    return pl.pallas_call(
        functools.partial(_norm_matmul_body, has_res=res is not None, want_h=want_h),
        grid=(m // tm, n // tn),
        in_specs=in_specs,
        out_specs=out_specs,
        out_shape=out_shape,
        scratch_shapes=[pltpu.VMEM((tm, d), BF16)],
        compiler_params=pltpu.CompilerParams(
            dimension_semantics=("parallel", "arbitrary")),
        name="norm_matmul",
    )(*args)


def _dilated_body(q_ref, kp_ref, kc_ref, vp_ref, vc_ref, b_ref, o_ref, lse_ref):
    i = pl.program_id(2)
    col = lax.broadcasted_iota(I32, (DIL_BLOCK, 2 * DIL_BLOCK), 1)
    has_prev = jnp.logical_or(col >= DIL_BLOCK, i > 0)
    lane = lax.broadcasted_iota(I32, (1, LANES), 1)
    own = [(lane < HEAD_DIM) == (a == 0) for a in range(2)]
    for pr in range(A_HEADS // 2):
        cs = slice(pr * LANES, (pr + 1) * LANES)
        q = q_ref[:, cs] * QK_SCALE
        k = jnp.concatenate([kp_ref[:, cs], kc_ref[:, cs]], axis=0)
        v = jnp.concatenate([vp_ref[:, cs], vc_ref[:, cs]], axis=0)
        outs, lses = [], []
        for a in range(2):
            s = lax.dot_general(jnp.where(own[a], q, jnp.zeros_like(q)), k,
                                (((1,), (1,)), ((), ())), preferred_element_type=F32)
            s = jnp.where(has_prev, s + b_ref[2 * pr + a], NEG_INF)
            m = jnp.max(s, axis=-1, keepdims=True)
            pb = jnp.exp(s - m).astype(BF16)
            acc = jnp.dot(pb, jnp.where(own[a], v, jnp.ones_like(v)), preferred_element_type=F32)
            l = pltpu.roll(acc, HEAD_DIM, 1)
            outs.append(acc / l)
            lses.append(m + jnp.log(l))
        o_ref[:, cs] = jnp.where(own[0], outs[0], outs[1])
        lse_ref[:, cs] = jnp.where(own[0], lses[0], lses[1])


def dilated_branch(p, bias_tile, dil, batch, seq):
    if dil > 1:
        p = p[:, :3 * A_WIDTH]
    n_cols = p.shape[1]
    cb = n_cols // A_WIDTH
    rows = seq // dil
    nblk = rows // DIL_BLOCK
    pv = p.reshape(batch, rows, dil * n_cols)
    blk = (None, DIL_BLOCK, A_WIDTH)

    def spec(which, prev):
        if prev:
            return pl.BlockSpec(blk, lambda b, r, i: (b, jnp.maximum(i - 1, 0), r * cb + which))
        return pl.BlockSpec(blk, lambda b, r, i: (b, i, r * cb + which))

    ospec = pl.BlockSpec(blk, lambda b, r, i: (b, i, r))
    oshape = jax.ShapeDtypeStruct((batch, rows, dil * A_WIDTH), F32)
    o, lse = pl.pallas_call(
        _dilated_body,
        grid=(batch, dil, nblk),
        in_specs=[spec(0, False), spec(1, True), spec(1, False), spec(2, True), spec(2, False),
                  pl.BlockSpec((A_HEADS, DIL_BLOCK, 2 * DIL_BLOCK), lambda b, r, i: (0, 0, 0))],
        out_specs=[ospec, ospec],
        out_shape=[oshape, oshape],
        compiler_params=pltpu.CompilerParams(
            dimension_semantics=("parallel", "parallel", "arbitrary")),
        name=f"dilated_d{dil}",
    )(pv, pv, pv, pv, pv, bias_tile)
    return o.reshape(batch * seq, A_WIDTH), lse.reshape(batch * seq, A_WIDTH)


def dilated_bias_tile(rel_bias_a, window, dil):
    n = window // dil
    assert n == DIL_BLOCK
    bucket = _t5_bucket_table(window + 1)
    period = 4 * n
    u = np.arange(period)
    valid = u <= n
    w = jnp.where(jnp.asarray(valid)[None],
                  rel_bias_a.T[:, bucket[np.where(valid, n - u, 0) * dil]], NEG_INF).astype(F32)
    rep = jnp.broadcast_to(w[:, None, :], (A_HEADS, n, period))
    flat = rep.reshape(A_HEADS, n * period)[:, :n * (period - 1)]
    return flat.reshape(A_HEADS, n, period - 1)[:, :, :2 * n]


def _diff_body(q_ref, k_ref, v_ref, b_ref, lam_ref, g_ref, o_ref,
               m_scr, l_scr, acc_scr, *, n_tiles, lam_init):
    t = ATT_BLOCK
    qi = pl.program_id(2)
    q = q_ref[...] * QK_SCALE
    lane = lax.broadcasted_iota(I32, (1, LANES), 1)
    qa = [jnp.where((lane < DIFF_HALF) == (a == 0), q, jnp.zeros_like(q)) for a in range(2)]
    ones = jnp.ones((t, LANES), BF16)
    for a in range(2):
        m_scr[a] = jnp.full((t, LANES), NEG_INF, F32)
        l_scr[a] = jnp.zeros((t, LANES), F32)
        acc_scr[a] = jnp.zeros((t, DIFF_VDIM), F32)

    def step(j, masked):
        off = pl.multiple_of(j * t, t)
        ks = k_ref[pl.ds(off, t), :]
        vs = v_ref[pl.ds(off, t), :]
        bias = b_ref[jnp.minimum(qi - j, n_tiles - 1)]
        if masked:
            row = lax.broadcasted_iota(I32, (t, t), 0)
            col = lax.broadcasted_iota(I32, (t, t), 1)
            causal = row >= col
        for a in range(2):
            s = lax.dot_general(qa[a], ks, (((1,), (1,)), ((), ())),
                                preferred_element_type=F32) + bias
            if masked:
                s = jnp.where(causal, s, NEG_INF)
            m_prev = m_scr[a]
            m_new = jnp.maximum(m_prev, jnp.max(s, axis=-1, keepdims=True))
            alpha = jnp.exp(m_prev - m_new)
            pb = jnp.exp(s - jnp.concatenate([m_new] * (t // LANES), axis=1)).astype(BF16)
            l_scr[a] = alpha * l_scr[a] + jnp.dot(pb, ones, preferred_element_type=F32)
            acc_scr[a] = alpha * acc_scr[a] + jnp.dot(pb, vs, preferred_element_type=F32)
            m_scr[a] = m_new

    def loop_body(j, carry):
        step(j, False)
        return carry

    lax.fori_loop(0, qi, loop_body, 0)
    step(qi, True)

    lp = lam_ref[...]
    lam = (jnp.exp(jnp.sum(lp[0:1] * lp[1:2])) - jnp.exp(jnp.sum(lp[2:3] * lp[3:4]))
           + lam_init)
    o = acc_scr[0] / l_scr[0] - lam * (acc_scr[1] / l_scr[1])
    ms = jnp.mean(o * o, axis=-1, keepdims=True)
    y = o * lax.rsqrt(ms + RMS_EPS) * g_ref[...]
    o_ref[...] = (y * (1.0 - lam_init)).astype(o_ref.dtype)


def diff_attention(p, bias_tiles, lam_params, ln_g, lam_init, batch, seq):
    t = ATT_BLOCK
    n_tiles = bias_tiles.shape[1]
    cq = 3 * A_WIDTH // LANES
    ck = cq + B_QK_WIDTH // LANES
    cv = ck + B_QK_WIDTH // LANES
    return pl.pallas_call(
        functools.partial(_diff_body, n_tiles=n_tiles, lam_init=lam_init),
        grid=(batch, B_HEADS, seq // t),
        in_specs=[
            pl.BlockSpec((None, t, LANES), lambda b, h, i: (b, i, cq + h)),
            pl.BlockSpec((None, seq, LANES), lambda b, h, i: (b, 0, ck + h)),
            pl.BlockSpec((None, seq, LANES), lambda b, h, i: (b, 0, cv + h)),
            pl.BlockSpec((None, n_tiles, t, t), lambda b, h, i: (h, 0, 0, 0)),
            pl.BlockSpec((4, DIFF_HALF), lambda b, h, i: (0, 0)),
            pl.BlockSpec((1, DIFF_VDIM), lambda b, h, i: (0, 0)),
        ],
        out_specs=pl.BlockSpec((None, t, LANES), lambda b, h, i: (b, i, h)),
        out_shape=jax.ShapeDtypeStruct((batch, seq, B_V_WIDTH), BF16),
        scratch_shapes=[pltpu.VMEM((2, t, LANES), F32), pltpu.VMEM((2, t, LANES), F32),
                        pltpu.VMEM((2, t, DIFF_VDIM), F32)],
        compiler_params=pltpu.CompilerParams(
            dimension_semantics=("parallel", "parallel", "arbitrary")),
        name="diff_attention",
    )(p, p, p, bias_tiles, lam_params, ln_g.reshape(1, DIFF_VDIM))


def diff_bias_tiles(rel_bias_b, seq):
    t = ATT_BLOCK
    bucket = _t5_bucket_table(max(seq, 2 * MAX_DISTANCE) + 2 * t)
    sat = bucket[-1]
    d_sat = int(np.max(np.nonzero(bucket != sat)[0])) + 1
    n_full = (d_sat + t - 1 + t - 1) // t
    n_tiles = n_full + 1
    assert n_full * t - (t - 1) >= d_sat
    n = np.arange(2 * t)[None, :]
    base = np.arange(n_tiles)[:, None] * t
    dist = np.clip(np.where(n < t, base - n, base + 2 * t - n), 0, None)
    w = rel_bias_b.T[:, bucket[dist]].astype(F32)
    rep = jnp.broadcast_to(w[:, :, None, :], (B_HEADS, n_tiles, t, 2 * t))
    flat = rep.reshape(B_HEADS, n_tiles, 2 * t * t)[:, :, :t * (2 * t - 1)]
    return flat.reshape(B_HEADS, n_tiles, t, 2 * t - 1)[:, :, :, :t]


def _logsig_cumsum_body(f_ref, b_ref, c_ref, carry_scr):
    t = f_ref.shape[0]

    @pl.when(pl.program_id(1) == 0)
    def _():
        carry_scr[...] = jnp.zeros_like(carry_scr)

    x = f_ref[...] + b_ref[...]
    ls = jnp.minimum(x, 0.0) - jnp.log1p(jnp.exp(-jnp.abs(x)))
    row = lax.broadcasted_iota(I32, (t, t), 0)
    col = lax.broadcasted_iota(I32, (t, t), 1)
    tri = (row >= col).astype(F32)
    c = jnp.dot(tri, ls, preferred_element_type=F32,
                precision=lax.Precision.HIGHEST) + carry_scr[...]
    c_ref[...] = c
    carry_scr[...] = c[t - 1:t, :]


def logsig_cumsum(fg, b_f, batch, seq, t=512):
    return pl.pallas_call(
        _logsig_cumsum_body,
        grid=(batch, seq // t),
        in_specs=[pl.BlockSpec((None, t, LANES), lambda b, i: (b, i, 0)),
                  pl.BlockSpec((1, LANES), lambda b, i: (0, 0))],
        out_specs=pl.BlockSpec((None, t, LANES), lambda b, i: (b, i, 0)),
        out_shape=jax.ShapeDtypeStruct((batch, seq, LANES), F32),
        scratch_shapes=[pltpu.VMEM((1, LANES), F32)],
        compiler_params=pltpu.CompilerParams(
            dimension_semantics=("parallel", "arbitrary")),
        name="logsig_cumsum",
    )(fg, b_f)


def _fox_body(q_ref, k_ref, v_ref, cq_ref, ck_ref, o_ref, m_scr, acc_scr):
    t = ATT_BLOCK
    qi = pl.program_id(2)
    q = q_ref[...] * QK_SCALE
    lane = lax.broadcasted_iota(I32, (1, LANES), 1)
    own = [(lane < HEAD_DIM) == (a == 0) for a in range(2)]
    qa = [jnp.where(own[a], q, jnp.zeros_like(q)) for a in range(2)]
    cqb = [jnp.broadcast_to(cq_ref[a], (t, LANES)) for a in range(2)]
    for a in range(2):
        m_scr[a] = jnp.full((t, LANES), NEG_INF, F32)
        acc_scr[a] = jnp.zeros((t, LANES), F32)

    def step(j, masked):
        off = pl.multiple_of(j * t, t)
        ks = k_ref[pl.ds(off, t), :]
        vs = v_ref[pl.ds(off, t), :]
        if masked:
            row = lax.broadcasted_iota(I32, (t, t), 0)
            col = lax.broadcasted_iota(I32, (t, t), 1)
            causal = row >= col
        for a in range(2):
            s = lax.dot_general(qa[a], ks, (((1,), (1,)), ((), ())),
                                preferred_element_type=F32)
            s = s - ck_ref[a, :, pl.ds(off, t)]
            if masked:
                s = jnp.where(causal, s, NEG_INF)
            m_prev = m_scr[a]
            m_new = jnp.maximum(m_prev, jnp.max(s, axis=-1, keepdims=True) + cqb[a])
            alpha = jnp.exp(m_prev - m_new)
            shift = m_new - cqb[a]
            pb = jnp.exp(s - jnp.concatenate([shift] * (t // LANES), axis=1)).astype(BF16)
            v_aug = jnp.where(own[a], vs, jnp.ones_like(vs))
            acc_scr[a] = alpha * acc_scr[a] + jnp.dot(pb, v_aug, preferred_element_type=F32)
            m_scr[a] = m_new

    def loop_body(j, carry):
        step(j, False)
        return carry

    lax.fori_loop(0, qi, loop_body, 0)
    step(qi, True)
    r = [acc_scr[a] / pltpu.roll(acc_scr[a], HEAD_DIM, 1) for a in range(2)]
    o_ref[...] = jnp.where(own[0], r[0], r[1]).astype(o_ref.dtype)


def fox_attention(p, cq, ck, batch, seq):
    t = ATT_BLOCK
    nk = C_WIDTH // LANES
    return pl.pallas_call(
        _fox_body,
        grid=(batch, C_HEADS // 2, seq // t),
        in_specs=[
            pl.BlockSpec((None, t, LANES), lambda b, h, i: (b, i, h)),
            pl.BlockSpec((None, seq, LANES), lambda b, h, i: (b, 0, nk + h)),
            pl.BlockSpec((None, seq, LANES), lambda b, h, i: (b, 0, 2 * nk + h)),
            pl.BlockSpec((None, 2, t, 1), lambda b, h, i: (b, h, i, 0)),
            pl.BlockSpec((None, 2, 1, seq), lambda b, h, i: (b, h, 0, 0)),
        ],
        out_specs=pl.BlockSpec((None, t, LANES), lambda b, h, i: (b, i, h)),
        out_shape=jax.ShapeDtypeStruct((batch, seq, C_WIDTH), BF16),
        scratch_shapes=[pltpu.VMEM((2, t, LANES), F32), pltpu.VMEM((2, t, LANES), F32)],
        compiler_params=pltpu.CompilerParams(
            dimension_semantics=("parallel", "parallel", "arbitrary")),
        name="fox_attention",
    )(p, p, p, cq, ck)


def _even_out_body(o1, l1, o2, l2, o3, l3, ob_ref, w_ref, x_ref, out_ref, a_scr):
    @pl.when(pl.program_id(1) == 0)
    def _():
        a1, a2, a3 = l1[...], l2[...], l3[...]
        mx = jnp.maximum(jnp.maximum(a1, a2), a3)
        e1, e2, e3 = jnp.exp(a1 - mx), jnp.exp(a2 - mx), jnp.exp(a3 - mx)
        oa = (e1 * o1[...] + e2 * o2[...] + e3 * o3[...]) / (e1 + e2 + e3)
        a_scr[:, :A_WIDTH] = oa.astype(BF16)
        a_scr[:, A_WIDTH:] = ob_ref[...]

    out_ref[...] = x_ref[...] + jnp.dot(a_scr[...], w_ref[...], preferred_element_type=F32)


def even_out_proj(branches, ob, w, x, tm=512, tn=1024):
    m, d = x.shape
    half = pl.BlockSpec((tm, A_WIDTH), lambda i, j: (i, 0))
    flat = [a for pair in branches for a in pair]
    return pl.pallas_call(
        _even_out_body,
        grid=(m // tm, d // tn),
        in_specs=[half] * 6 + [
            pl.BlockSpec((tm, B_V_WIDTH), lambda i, j: (i, 0)),
            pl.BlockSpec((A_WIDTH + B_V_WIDTH, tn), lambda i, j: (0, j)),
            pl.BlockSpec((tm, tn), lambda i, j: (i, j))],
        out_specs=pl.BlockSpec((tm, tn), lambda i, j: (i, j)),
        out_shape=jax.ShapeDtypeStruct((m, d), F32),
        scratch_shapes=[pltpu.VMEM((tm, A_WIDTH + B_V_WIDTH), BF16)],
        compiler_params=pltpu.CompilerParams(
            dimension_semantics=("parallel", "arbitrary")),
        name="even_out_proj",
    )(*flat, ob, w, x)


def _matmul_res_body(a_ref, w_ref, x_ref, o_ref):
    o_ref[...] = x_ref[...] + jnp.dot(a_ref[...], w_ref[...], preferred_element_type=F32)


def matmul_residual(a, w, x, tm=1024, tn=1024):
    m, k = a.shape
    n = w.shape[1]
    return pl.pallas_call(
        _matmul_res_body,
        grid=(m // tm, n // tn),
        in_specs=[pl.BlockSpec((tm, k), lambda i, j: (i, 0)),
                  pl.BlockSpec((k, tn), lambda i, j: (0, j)),
                  pl.BlockSpec((tm, tn), lambda i, j: (i, j))],
        out_specs=pl.BlockSpec((tm, tn), lambda i, j: (i, j)),
        out_shape=jax.ShapeDtypeStruct((m, n), F32),
        compiler_params=pltpu.CompilerParams(
            dimension_semantics=("parallel", "parallel")),
        name="matmul_residual",
    )(a, w, x)


def _add_norm_body(x_ref, r_ref, g_ref, o_ref):
    x = x_ref[...] + r_ref[...]
    ms = jnp.mean(x * x, axis=-1, keepdims=True)
    o_ref[...] = x * lax.rsqrt(ms + RMS_EPS) * g_ref[...]


def add_norm(x, r, g, tm=512):
    m, d = x.shape
    row = pl.BlockSpec((tm, d), lambda i: (i, 0))
    return pl.pallas_call(
        _add_norm_body,
        grid=(m // tm,),
        in_specs=[row, row, pl.BlockSpec((1, d), lambda i: (0, 0))],
        out_specs=row,
        out_shape=jax.ShapeDtypeStruct((m, d), F32),
        compiler_params=pltpu.CompilerParams(dimension_semantics=("parallel",)),
        name="add_norm",
    )(x, r, g.reshape(1, d))


def _peer_candidates():
    groups = [("a", 0, 0), ("a", 0, 8), ("a", 1, 0), ("b", 0, 8), ("a", 2, 0), ("a", 3, 0),
              ("b", 0, 0), ("b", 1, 0), ("b", 2, 0)]
    cid = np.zeros((8 * len(groups), 1), np.int32)
    seen = set()
    for g, (kind, fixed, start) in enumerate(groups):
        for r in range(8):
            a, b = (fixed, start + r) if kind == "a" else (start + r, fixed)
            row = 8 * g + r
            if (a + 1) * (b + 1) <= PEER_TOPK and (a, b) not in seen:
                seen.add((a, b))
                cid[row, 0] = a * PEER_TOPK + b
            else:
                cid[row, 0] = PEER_TOPK * PEER_TOPK + row
    assert len(seen) == sum((a + 1) * (b + 1) <= PEER_TOPK
                            for a in range(PEER_TOPK) for b in range(PEER_TOPK))
    return groups, cid


def _peer_topk_body(q_ref, sk_ref, cid_ref, idx_ref, gate_ref, ts_scr, ti_scr,
                    bs_scr, be_scr, cv_scr, ce_scr, sc_scr, *, groups):
    tm = q_ref.shape[0]
    neg_inf = jnp.float32(-jnp.inf)
    key_id = lax.broadcasted_iota(I32, (1, N_KEYS, tm), 1)
    batch = sc_scr.shape[0]

    def group_body(g, carry):
        for j in range(batch):
            pr = g * batch + j
            off = pl.multiple_of(pr * PEER_KEY_HALF, PEER_KEY_HALF)
            sc_scr[j] = lax.dot_general(sk_ref[pr], q_ref[:, pl.ds(off, PEER_KEY_HALF)],
                                        (((1,), (1,)), ((), ())), preferred_element_type=F32)
        rows = pl.ds(g * batch, batch)

        def k_body(k, c):
            vals = sc_scr[...]
            m = jnp.max(vals, axis=1, keepdims=True)
            sel = jnp.min(jnp.where(vals == m, key_id, N_KEYS), axis=1, keepdims=True)
            ts_scr[rows, pl.ds(k, 1), :] = m
            ti_scr[rows, pl.ds(k, 1), :] = sel
            sc_scr[...] = jnp.where(key_id == sel, neg_inf, vals)
            return c

        lax.fori_loop(0, PEER_TOPK, k_body, 0)
        return carry

    lax.fori_loop(0, 2 * PEER_HEADS // batch, group_body, 0)

    cand_id = cid_ref[...]
    pad = jnp.where(cand_id < PEER_TOPK * PEER_TOPK, 0.0, neg_inf)

    def pick(x1, x2):
        return jnp.concatenate(
            [x1[f:f + 1] + x2[s:s + 8] if kind == "a" else x1[s:s + 8] + x2[f:f + 1]
             for kind, f, s in groups], axis=0)

    def head_body(h, carry):
        cv_scr[h] = pick(ts_scr[2 * h], ts_scr[2 * h + 1]) + pad
        ce_scr[h] = pick(ti_scr[2 * h] * N_KEYS, ti_scr[2 * h + 1])
        return carry

    lax.fori_loop(0, PEER_HEADS, head_body, 0)

    cid3 = cand_id[None]

    def k_body(k, carry):
        vals = cv_scr[...]
        m = jnp.max(vals, axis=1, keepdims=True)
        sel = jnp.min(jnp.where(vals == m, cid3, PEER_TOPK * PEER_TOPK), axis=1, keepdims=True)
        hit = cid3 == sel
        bs_scr[:, pl.ds(k, 1), :] = m
        be_scr[:, pl.ds(k, 1), :] = jnp.sum(jnp.where(hit, ce_scr[...], 0), axis=1, keepdims=True)
        cv_scr[...] = jnp.where(hit, neg_inf, vals)
        return carry

    lax.fori_loop(0, PEER_TOPK, k_body, 0)
    bs = bs_scr[...]
    e = jnp.exp(bs - jnp.max(bs, axis=1, keepdims=True))
    gate = e / jnp.sum(e, axis=1, keepdims=True)
    n_sel = PEER_HEADS * PEER_TOPK
    gate_ref[...] = gate.reshape(n_sel, tm).T
    idx_ref[...] = be_scr[...].reshape(n_sel, tm).T


def peer_topk(q, subkeys, tm=256):
    m = q.shape[0]
    n_sel = PEER_HEADS * PEER_TOPK
    groups, cid = _peer_candidates()
    out_spec = pl.BlockSpec((tm, n_sel), lambda i: (i, 0))
    return pl.pallas_call(
        functools.partial(_peer_topk_body, groups=groups),
        grid=(m // tm,),
        in_specs=[pl.BlockSpec((tm, q.shape[1]), lambda i: (i, 0)),
                  pl.BlockSpec(subkeys.shape, lambda i: (0, 0, 0)),
                  pl.BlockSpec(cid.shape, lambda i: (0, 0))],
        out_specs=[out_spec, out_spec],
        out_shape=[jax.ShapeDtypeStruct((m, n_sel), I32),
                   jax.ShapeDtypeStruct((m, n_sel), F32)],
        scratch_shapes=[pltpu.VMEM((2 * PEER_HEADS, PEER_TOPK, tm), F32),
                        pltpu.VMEM((2 * PEER_HEADS, PEER_TOPK, tm), I32),
                        pltpu.VMEM((PEER_HEADS, PEER_TOPK, tm), F32),
                        pltpu.VMEM((PEER_HEADS, PEER_TOPK, tm), I32),
                        pltpu.VMEM((PEER_HEADS, cid.shape[0], tm), F32),
                        pltpu.VMEM((PEER_HEADS, cid.shape[0], tm), I32),
                        pltpu.VMEM((PEER_HEADS, N_KEYS, tm), F32)],
        compiler_params=pltpu.CompilerParams(dimension_semantics=("parallel",)),
        name="peer_topk",
    )(q, subkeys, jnp.asarray(cid))


def _gelu_gate_body(a_ref, g_ref, o_ref):
    a = a_ref[...]
    o_ref[...] = g_ref[...] * (0.5 * a * (1.0 + lax.erf(a * (2.0 ** -0.5))))


def gelu_gate(act, gate, tm=2048):
    m, n = act.shape
    spec = pl.BlockSpec((tm, n), lambda i: (i, 0))
    return pl.pallas_call(
        _gelu_gate_body,
        grid=(m // tm,),
        in_specs=[spec, spec],
        out_specs=spec,
        out_shape=jax.ShapeDtypeStruct((m, n), F32),
        compiler_params=pltpu.CompilerParams(dimension_semantics=("parallel",)),
        name="gelu_gate",
    )(act, gate)


SC_TOK_CHUNK = 32
SC_RING = 8
SC_BF16_GROUP = 4
SC_FMT = plsc.PackFormat.INTERLEAVED


def _sc_worker_id():
    return lax.axis_index("s") * SC_CORES + lax.axis_index("c")


def pack_bf16_pairs(t):
    half = t.shape[-1] // 2
    bits = lax.bitcast_convert_type(t.astype(BF16).astype(F32), I32)
    return (bits[..., half:] & jnp.int32(-65536)) | lax.shift_right_logical(
        bits[..., :half], jnp.int32(16))


def _sc_row_pipeline(idx_v, table_hbm, rows_v, sems, n_items, groups, compute):
    def gather(item):
        tt, g = item // groups, item % groups
        ids = idx_v[tt, pl.ds(g * SC_LANES, SC_LANES)]
        slot = item % SC_RING
        return pltpu.make_async_copy(table_hbm.at[ids], rows_v.at[slot], sems.at[slot])

    for s in range(SC_RING - 1):
        gather(s).start()

    def item_body(item, carry):
        nxt = item + SC_RING - 1

        @pl.when(nxt < n_items)
        def _():
            gather(nxt).start()

        gather(item).wait()
        compute(item // groups, item % groups, item % SC_RING)
        return carry

    lax.fori_loop(0, n_items, item_body, 0)


def peer_expert_dots(hp, idx, up):
    m, dw = hp.shape
    n_sel = idx.shape[1]
    per_w = m // SC_WORKERS
    n_chunks = per_w // SC_TOK_CHUNK
    groups = n_sel // SC_LANES
    step = SC_BF16_GROUP * SC_LANES
    mesh = plsc.VectorSubcoreMesh(core_axis_name="c", subcore_axis_name="s")

    @functools.partial(
        pl.kernel, mesh=mesh,
        out_type=jax.ShapeDtypeStruct((m, n_sel), F32),
        scratch_types=[
            pltpu.VMEM((SC_TOK_CHUNK, n_sel), I32),
            pltpu.VMEM((SC_TOK_CHUNK, dw), I32),
            pltpu.VMEM((SC_TOK_CHUNK, n_sel), F32),
            pltpu.VMEM((SC_RING, SC_LANES, dw), I32),
            pltpu.VMEM((SC_LANES * SC_LANES,), F32),
            pltpu.SemaphoreType.DMA((SC_RING,)),
        ],
        compiler_params=pltpu.CompilerParams(needs_layout_passes=False),
        name="peer_expert_dots",
    )
    def k(h_hbm, idx_hbm, u_hbm, act_hbm, idx_v, h_v, act_v, rows_v, part_v, sems):
        base = _sc_worker_id() * per_w
        lane = lax.broadcasted_iota(I32, (SC_LANES,), 0)

        def compute(tt, g, slot):
            def grp_body(q, accs):
                off = pl.multiple_of(q * step, step)
                xs = [plsc.bitcast(h_v[tt, pl.ds(off + c * SC_LANES, SC_LANES)], BF16)
                      for c in range(SC_BF16_GROUP)]
                new = []
                for e in range(SC_LANES):
                    s = None
                    for c in range(SC_BF16_GROUP):
                        p = plsc.bitcast(
                            rows_v[slot, e, pl.ds(off + c * SC_LANES, SC_LANES)], BF16) * xs[c]
                        s = p if s is None else s + p
                    lo, hi = plsc.unpack(s, format=SC_FMT)
                    new.append(accs[e] + (lo + hi))
                return tuple(new)

            accs = lax.fori_loop(
                0, dw // step, grp_body,
                tuple(jnp.zeros((SC_LANES,), F32) for _ in range(SC_LANES)))
            for e in range(SC_LANES):
                part_v[pl.ds(e * SC_LANES, SC_LANES)] = accs[e]
            tot = jnp.zeros((SC_LANES,), F32)
            for l in range(SC_LANES):
                tot = tot + plsc.load_gather(part_v, [lane * SC_LANES + l])
            act_v[tt, pl.ds(g * SC_LANES, SC_LANES)] = tot

        def chunk_body(c, carry):
            t0 = base + c * SC_TOK_CHUNK
            pltpu.sync_copy(idx_hbm.at[pl.ds(t0, SC_TOK_CHUNK)], idx_v)
            pltpu.sync_copy(h_hbm.at[pl.ds(t0, SC_TOK_CHUNK)], h_v)
            _sc_row_pipeline(idx_v, u_hbm, rows_v, sems, SC_TOK_CHUNK * groups, groups, compute)
            pltpu.sync_copy(act_v, act_hbm.at[pl.ds(t0, SC_TOK_CHUNK)])
            return carry

        lax.fori_loop(0, n_chunks, chunk_body, 0)

    return k(hp, idx, up)


def peer_expert_combine(w, idx, vp):
    m, n_sel = w.shape
    dw = vp.shape[1]
    d = 2 * dw
    per_w = m // SC_WORKERS
    n_chunks = per_w // SC_TOK_CHUNK
    n_vec = d // SC_LANES
    groups = n_sel // SC_LANES
    mesh = plsc.VectorSubcoreMesh(core_axis_name="c", subcore_axis_name="s")

    @functools.partial(
        pl.kernel, mesh=mesh,
        out_type=jax.ShapeDtypeStruct((m, d), F32),
        scratch_types=[
            pltpu.VMEM((SC_TOK_CHUNK, n_sel), I32),
            pltpu.VMEM((SC_TOK_CHUNK, n_sel), F32),
            pltpu.VMEM((SC_TOK_CHUNK, d), F32),
            pltpu.VMEM((SC_RING, SC_LANES, dw), I32),
            pltpu.SemaphoreType.DMA((SC_RING,)),
        ],
        compiler_params=pltpu.CompilerParams(needs_layout_passes=False),
        name="peer_expert_combine",
    )
    def k(w_hbm, idx_hbm, v_hbm, out_hbm, idx_v, w_v, out_v, rows_v, sems):
        base = _sc_worker_id() * per_w

        def compute(tt, g, slot):
            splat = []
            for e in range(SC_LANES):
                s = plsc.load_gather(w_v, [jnp.full((SC_LANES,), tt, I32),
                                           jnp.full((SC_LANES,), g * SC_LANES + e, I32)])
                splat.append(plsc.pack(s, s, format=SC_FMT))

            @plsc.parallel_loop(0, dw // SC_LANES)
            def _(j):
                off = pl.multiple_of(j * SC_LANES, SC_LANES)
                acc_lo = out_v[tt, pl.ds(off, SC_LANES)]
                acc_hi = out_v[tt, pl.ds(dw + off, SC_LANES)]
                for e0 in range(0, SC_LANES, SC_BF16_GROUP):
                    s = None
                    for e in range(e0, e0 + SC_BF16_GROUP):
                        p = plsc.bitcast(rows_v[slot, e, pl.ds(off, SC_LANES)], BF16) * splat[e]
                        s = p if s is None else s + p
                    lo, hi = plsc.unpack(s, format=SC_FMT)
                    acc_lo = acc_lo + lo
                    acc_hi = acc_hi + hi
                out_v[tt, pl.ds(off, SC_LANES)] = acc_lo
                out_v[tt, pl.ds(dw + off, SC_LANES)] = acc_hi

        def chunk_body(c, carry):
            t0 = base + c * SC_TOK_CHUNK
            pltpu.sync_copy(idx_hbm.at[pl.ds(t0, SC_TOK_CHUNK)], idx_v)
            pltpu.sync_copy(w_hbm.at[pl.ds(t0, SC_TOK_CHUNK)], w_v)

            def zero_body(z, carry2):
                tt, j = z // n_vec, z % n_vec
                out_v[tt, pl.ds(pl.multiple_of(j * SC_LANES, SC_LANES), SC_LANES)] = (
                    jnp.zeros((SC_LANES,), F32))
                return carry2

            lax.fori_loop(0, SC_TOK_CHUNK * n_vec, zero_body, 0)
            _sc_row_pipeline(idx_v, v_hbm, rows_v, sems, SC_TOK_CHUNK * groups, groups, compute)
            pltpu.sync_copy(out_v, out_hbm.at[pl.ds(t0, SC_TOK_CHUNK)])
            return carry

        lax.fori_loop(0, n_chunks, chunk_body, 0)

    return k(w, idx, vp)


def kernel(x, norm_mix_g, norm_ffn_g, final_norm_g, rel_bias, even_w_in, even_w_out,
           diff_lambda, diff_ln_g, odd_w_in, odd_b_f, odd_w_out, peer_wq, peer_subkeys,
           peer_u, peer_v):
    batch, seq, d = x.shape

    dil_tiles = [dilated_bias_tile(rel_bias[:, :A_HEADS], w, dl) for w, dl in DILATED_BRANCHES]
    diff_tiles = diff_bias_tiles(rel_bias[:, A_HEADS:], seq)
    lam_init = 0.8 - 0.6 * math.exp(-0.3 * 0)
    even_in, even_out = even_w_in[0].astype(BF16), even_w_out[0].astype(BF16)
    w_in = odd_w_in[0]
    odd_in, odd_out = w_in[:, :3 * C_WIDTH].astype(BF16), odd_w_out[0].astype(BF16)
    w_gate = jnp.pad(w_in[:, 3 * C_WIDTH:], ((0, 0), (0, LANES - C_HEADS)))
    b_f = jnp.pad(odd_b_f[0], (0, LANES - C_HEADS)).reshape(1, LANES)
    peer = [dict(wq=peer_wq[l].astype(BF16),
                 sk=peer_subkeys[l].reshape(2 * PEER_HEADS, N_KEYS, PEER_KEY_HALF).astype(BF16),
                 up=pack_bf16_pairs(peer_u[l]), vp=pack_bf16_pairs(peer_v[l]))
            for l in range(2)]

    bg = batch // BATCH_GROUPS
    m = bg * seq
    groups = [dict(x=x[g * bg:(g + 1) * bg].reshape(m, d)) for g in range(BATCH_GROUPS)]

    def peer_select(st, layer, after=None):
        q, hp = norm_matmul(st["x"], norm_ffn_g[layer], peer[layer]["wq"], want_h="packed")
        idx, gate = peer_topk(q, peer[layer]["sk"])
        if after is not None and after["st"] is not st:
            after["st"]["peer"], idx = lax.optimization_barrier((after["st"]["peer"], idx))
        return dict(st=st, layer=layer, idx=idx, gate=gate,
                    act=peer_expert_dots(hp, idx, peer[layer]["up"]))

    def peer_combine(rec):
        rec["w"] = gelu_gate(rec["act"], rec["gate"])
        rec["st"]["peer"] = peer_expert_combine(rec["w"], rec["idx"], peer[rec["layer"]]["vp"])

    def even_layer(st, tie):
        (p,) = norm_matmul(st["x"], norm_mix_g[0], even_in)
        branches = [dilated_branch(p, tile, dl, bg, seq)
                    for tile, (_, dl) in zip(dil_tiles, DILATED_BRANCHES)]
        ob = tie(diff_attention(p.reshape(bg, seq, -1), diff_tiles, diff_lambda[0], diff_ln_g[0],
                                lam_init, bg, seq))
        st["x"] = even_out_proj(branches, ob.reshape(m, B_V_WIDTH), even_out, st["x"])

    def odd_layer(st, tie):
        p, st["x"], h = norm_matmul(st["x"], norm_mix_g[1], odd_in, res=st["peer"], want_h="f32")
        fg = gate_matmul(h, w_gate)
        c = logsig_cumsum(fg.reshape(bg, seq, LANES), b_f, bg, seq)
        ct = c[:, :, :C_HEADS].transpose(0, 2, 1)
        o = tie(fox_attention(p.reshape(bg, seq, -1), ct[:, :, :, None], ct[:, :, None, :],
                              bg, seq))
        st["x"] = matmul_residual(o.reshape(m, C_WIDTH), odd_out, st["x"])

    stages = [(even_layer, st, 0) for st in groups] + [(odd_layer, st, 1) for st in groups]
    recs = []
    for k, (mixer, st, layer) in enumerate(stages):
        def tie(t, prev=recs[k - 1] if k >= 1 else None):
            if prev is None:
                return t
            prev["act"], t = lax.optimization_barrier((prev["act"], t))
            peer_combine(prev)
            prev["w"], t = lax.optimization_barrier((prev["w"], t))
            return t

        mixer(st, tie)
        recs.append(peer_select(st, layer, after=recs[k - 2] if k >= 2 else None))
    peer_combine(recs[-1])
    outs = [add_norm(st["x"], st["peer"], final_norm_g).reshape(bg, seq, d) for st in groups]
    return jnp.concatenate(outs, axis=0)


def _gate_matmul_body(h_ref, w_ref, o_ref):
    o_ref[...] = jnp.dot(h_ref[...], w_ref[...], preferred_element_type=F32,
                         precision=lax.Precision.HIGHEST)


def gate_matmul(h, w, tm=512):
    m, d = h.shape
    n = w.shape[1]
    return pl.pallas_call(
        _gate_matmul_body,
        grid=(m // tm,),
        in_specs=[pl.BlockSpec((tm, d), lambda i: (i, 0)),
                  pl.BlockSpec((d, n), lambda i: (0, 0))],
        out_specs=pl.BlockSpec((tm, n), lambda i: (i, 0)),
        out_shape=jax.ShapeDtypeStruct((m, n), F32),
        compiler_params=pltpu.CompilerParams(dimension_semantics=("parallel",)),
        name="gate_matmul",
    )(h, w)
```

```python
import functools
import math

import numpy as np
import jax
import jax.numpy as jnp
from jax import lax
from jax.experimental import pallas as pl
from jax.experimental.pallas import tpu as pltpu
from jax.experimental.pallas import tpu_sc as plsc

F32 = jnp.float32
BF16 = jnp.bfloat16
I32 = jnp.int32

D_MODEL = 1024
HEAD_DIM = 64
A_HEADS = 8
DILATED_BRANCHES = ((128, 1), (512, 4), (2048, 16))
DIFF_HALF = 64
DIFF_VDIM = 128
B_HEADS = 4
C_HEADS = 16
N_BUCKETS = 32
MAX_DISTANCE = 2048
PEER_HEADS = 8
N_KEYS = 128
PEER_TOPK = 16
PEER_KEY_HALF = 128
RMS_EPS = 1e-6
NEG_INF = -1e30
A_WIDTH = A_HEADS * HEAD_DIM
B_QK_WIDTH = B_HEADS * 2 * DIFF_HALF
B_V_WIDTH = B_HEADS * DIFF_VDIM
C_WIDTH = C_HEADS * HEAD_DIM
QK_SCALE = 0.125
LANES = 128
DIL_BLOCK = 128
ATT_BLOCK = 512
BATCH_GROUPS = 8

SC_CORES = 2
SC_SUBCORES = 16
SC_LANES = 16
SC_WORKERS = SC_CORES * SC_SUBCORES


def _t5_bucket_table(n):
    max_exact = N_BUCKETS // 2
    d = np.arange(n)
    df = np.maximum(d, 1).astype(np.float32)
    large = max_exact + (
        np.log(df / np.float32(max_exact)) / np.float32(math.log(MAX_DISTANCE / max_exact))
        * np.float32(N_BUCKETS - max_exact)).astype(np.int32)
    large = np.minimum(large, N_BUCKETS - 1)
    return np.where(d < max_exact, d, large).astype(np.int32)


def _norm_matmul_body(*refs, has_res, want_h):
    it = iter(refs)
    x_ref = next(it)
    r_ref = next(it) if has_res else None
    g_ref = next(it)
    w_ref = next(it)
    o_ref = next(it)
    xs_ref = next(it) if has_res else None
    hout_ref = next(it) if want_h else None
    h_scr = next(it)

    @pl.when(pl.program_id(1) == 0)
    def _():
        x = x_ref[...]
        if has_res:
            x = x + r_ref[...]
            xs_ref[...] = x
        ms = jnp.mean(x * x, axis=-1, keepdims=True)
        h = x * lax.rsqrt(ms + RMS_EPS) * g_ref[...]
        if want_h == "f32":
            hout_ref[...] = h
        elif want_h == "packed":
            hout_ref[...] = pack_bf16_pairs(h)
        h_scr[...] = h.astype(BF16)

    o_ref[...] = jnp.dot(h_scr[...], w_ref[...],
                         preferred_element_type=F32).astype(o_ref.dtype)


def norm_matmul(x, g, w, *, res=None, want_h=None, out_dtype=BF16, tm=1024, tn=1024):
    m, d = x.shape
    n = w.shape[1]
    tn = min(tn, n)
    row = pl.BlockSpec((tm, d), lambda i, j: (i, 0))
    in_specs = [row] + ([row] if res is not None else []) + [
        pl.BlockSpec((1, d), lambda i, j: (0, 0)),
        pl.BlockSpec((d, tn), lambda i, j: (0, j))]
    out_specs = [pl.BlockSpec((tm, tn), lambda i, j: (i, j))]
    out_shape = [jax.ShapeDtypeStruct((m, n), out_dtype)]
    if res is not None:
        out_specs.append(row)
        out_shape.append(jax.ShapeDtypeStruct((m, d), F32))
    if want_h == "f32":
        out_specs.append(row)
        out_shape.append(jax.ShapeDtypeStruct((m, d), F32))
    elif want_h == "packed":
        out_specs.append(pl.BlockSpec((tm, d // 2), lambda i, j: (i, 0)))
        out_shape.append(jax.ShapeDtypeStruct((m, d // 2), I32))
    args =[x] + ([res] if res is not None else []) + [g.reshape(1, d), w]
    return pl.pallas_call(
        functools.partial(_norm_matmul_body, has_res=res is not None, want_h=want_h),
        grid=(m // tm, n // tn),
        in_specs=in_specs,
        out_specs=out_specs,
        out_shape=out_shape,
        scratch_shapes=[pltpu.VMEM((tm, d), BF16)],
        compiler_params=pltpu.CompilerParams(
            dimension_semantics=("parallel", "arbitrary")),
        name="norm_matmul",
    )(*args)


def _dilated_body(q_ref, kp_ref, kc_ref, vp_ref, vc_ref, b_ref, o_ref, lse_ref):
    i = pl.program_id(2)
    col = lax.broadcasted_iota(I32, (DIL_BLOCK, 2 * DIL_BLOCK), 1)
    has_prev = jnp.logical_or(col >= DIL_BLOCK, i > 0)
    lane = lax.broadcasted_iota(I32, (1, LANES), 1)
    own = [(lane < HEAD_DIM) == (a == 0) for a in range(2)]
    for pr in range(A_HEADS // 2):
        cs = slice(pr * LANES, (pr + 1) * LANES)
        q = q_ref[:, cs] * QK_SCALE
        k = jnp.concatenate([kp_ref[:, cs], kc_ref[:, cs]], axis=0)
        v = jnp.concatenate([vp_ref[:, cs], vc_ref[:, cs]], axis=0)
        outs, lses = [], []
        for a in range(2):
            s = lax.dot_general(jnp.where(own[a], q, jnp.zeros_like(q)), k,
                                (((1,), (1,)), ((), ())), preferred_element_type=F32)
            s = jnp.where(has_prev, s + b_ref[2 * pr + a], NEG_INF)
            m = jnp.max(s, axis=-1, keepdims=True)
            pb = jnp.exp(s - m).astype(BF16)
            acc = jnp.dot(pb, jnp.where(own[a], v, jnp.ones_like(v)), preferred_element_type=F32)
            l = pltpu.roll(acc, HEAD_DIM, 1)
            outs.append(acc / l)
            lses.append(m + jnp.log(l))
        o_ref[:, cs] = jnp.where(own[0], outs[0], outs[1])
        lse_ref[:, cs] = jnp.where(own[0], lses[0], lses[1])


def dilated_branch(p, bias_tile, dil, batch, seq):
    if dil > 1:
        p = p[:, :3 * A_WIDTH]
    n_cols = p.shape[1]
    cb = n_cols // A_WIDTH
    rows = seq // dil
    nblk = rows // DIL_BLOCK
    pv = p.reshape(batch, rows, dil * n_cols)
    blk = (None, DIL_BLOCK, A_WIDTH)

    def spec(which, prev):
        if prev:
            return pl.BlockSpec(blk, lambda b, r, i: (b, jnp.maximum(i - 1, 0), r * cb + which))
        return pl.BlockSpec(blk, lambda b, r, i: (b, i, r * cb + which))

    ospec = pl.BlockSpec(blk, lambda b, r, i: (b, i, r))
    oshape = jax.ShapeDtypeStruct((batch, rows, dil * A_WIDTH), F32)
    o, lse = pl.pallas_call(
        _dilated_body,
        grid=(batch, dil, nblk),
        in_specs=[spec(0, False), spec(1, True), spec(1, False), spec(2, True), spec(2, False),
                  pl.BlockSpec((A_HEADS, DIL_BLOCK, 2 * DIL_BLOCK), lambda b, r, i: (0, 0, 0))],
        out_specs=[ospec, ospec],
        out_shape=[oshape, oshape],
        compiler_params=pltpu.CompilerParams(
            dimension_semantics=("parallel", "parallel", "arbitrary")),
        name=f"dilated_d{dil}",
    )(pv, pv, pv, pv, pv, bias_tile)
    return o.reshape(batch * seq, A_WIDTH), lse.reshape(batch * seq, A_WIDTH)


def dilated_bias_tile(rel_bias_a, window, dil):
    n = window // dil
    assert n == DIL_BLOCK
    bucket = _t5_bucket_table(window + 1)
    period = 4 * n
    u = np.arange(period)
    valid = u <= n
    w = jnp.where(jnp.asarray(valid)[None],
                  rel_bias_a.T[:, bucket[np.where(valid, n - u, 0) * dil]], NEG_INF).astype(F32)
    rep = jnp.broadcast_to(w[:, None, :], (A_HEADS, n, period))
    flat = rep.reshape(A_HEADS, n * period)[:, :n * (period - 1)]
    return flat.reshape(A_HEADS, n, period - 1)[:, :, :2 * n]


def _diff_body(q_ref, k_ref, v_ref, b_ref, lam_ref, g_ref, o_ref,
               m_scr, l_scr, acc_scr, *, n_tiles, lam_init):
    t = ATT_BLOCK
    qi = pl.program_id(2)
    q = q_ref[...] * QK_SCALE
    lane = lax.broadcasted_iota(I32, (1, LANES), 1)
    qa = [jnp.where((lane < DIFF_HALF) == (a == 0), q, jnp.zeros_like(q)) for a in range(2)]
    ones = jnp.ones((t, LANES), BF16)
    for a in range(2):
        m_scr[a] = jnp.full((t, LANES), NEG_INF, F32)
        l_scr[a] = jnp.zeros((t, LANES), F32)
        acc_scr[a] = jnp.zeros((t, DIFF_VDIM), F32)

    def step(j, masked):
        off = pl.multiple_of(j * t, t)
        ks = k_ref[pl.ds(off, t), :]
        vs = v_ref[pl.ds(off, t), :]
        bias = b_ref[jnp.minimum(qi - j, n_tiles - 1)]
        if masked:
            row = lax.broadcasted_iota(I32, (t, t), 0)
            col = lax.broadcasted_iota(I32, (t, t), 1)
            causal = row >= col
        for a in range(2):
            s = lax.dot_general(qa[a], ks, (((1,), (1,)), ((), ())),
                                preferred_element_type=F32) + bias
            if masked:
                s = jnp.where(causal, s, NEG_INF)
            m_prev = m_scr[a]
            m_new = jnp.maximum(m_prev, jnp.max(s, axis=-1, keepdims=True))
            alpha = jnp.exp(m_prev - m_new)
            pb = jnp.exp(s - jnp.concatenate([m_new] * (t // LANES), axis=1)).astype(BF16)
            l_scr[a] = alpha * l_scr[a] + jnp.dot(pb, ones, preferred_element_type=F32)
            acc_scr[a] = alpha * acc_scr[a] + jnp.dot(pb, vs, preferred_element_type=F32)
            m_scr[a] = m_new

    def loop_body(j, carry):
        step(j, False)
        return carry

    lax.fori_loop(0, qi, loop_body, 0)
    step(qi, True)

    lp = lam_ref[...]
    lam = (jnp.exp(jnp.sum(lp[0:1] * lp[1:2])) - jnp.exp(jnp.sum(lp[2:3] * lp[3:4]))
           + lam_init)
    o = acc_scr[0] / l_scr[0] - lam * (acc_scr[1] / l_scr[1])
    ms = jnp.mean(o * o, axis=-1, keepdims=True)
    y = o * lax.rsqrt(ms + RMS_EPS) * g_ref[...]
    o_ref[...] = (y * (1.0 - lam_init)).astype(o_ref.dtype)


def diff_attention(p, bias_tiles, lam_params, ln_g, lam_init, batch, seq):
    t = ATT_BLOCK
    n_tiles = bias_tiles.shape[1]
    cq = 3 * A_WIDTH // LANES
    ck = cq + B_QK_WIDTH // LANES
    cv = ck + B_QK_WIDTH // LANES
    return pl.pallas_call(
        functools.partial(_diff_body, n_tiles=n_tiles, lam_init=lam_init),
        grid=(batch, B_HEADS, seq // t),
        in_specs=[
            pl.BlockSpec((None, t, LANES), lambda b, h, i: (b, i, cq + h)),
            pl.BlockSpec((None, seq, LANES), lambda b, h, i: (b, 0, ck + h)),
            pl.BlockSpec((None, seq, LANES), lambda b, h, i: (b, 0, cv + h)),
            pl.BlockSpec((None, n_tiles, t, t), lambda b, h, i: (h, 0, 0, 0)),
            pl.BlockSpec((4, DIFF_HALF), lambda b, h, i: (0, 0)),
            pl.BlockSpec((1, DIFF_VDIM), lambda b, h, i: (0, 0)),
        ],
        out_specs=pl.BlockSpec((None, t, LANES), lambda b, h, i: (b, i, h)),
        out_shape=jax.ShapeDtypeStruct((batch, seq, B_V_WIDTH), BF16),
        scratch_shapes=[pltpu.VMEM((2, t, LANES), F32), pltpu.VMEM((2, t, LANES), F32),
                        pltpu.VMEM((2, t, DIFF_VDIM), F32)],
        compiler_params=pltpu.CompilerParams(
            dimension_semantics=("parallel", "parallel", "arbitrary")),
        name="diff_attention",
    )(p, p, p, bias_tiles, lam_params, ln_g.reshape(1, DIFF_VDIM))


def diff_bias_tiles(rel_bias_b, seq):
    t = ATT_BLOCK
    bucket = _t5_bucket_table(max(seq, 2 * MAX_DISTANCE) + 2 * t)
    sat = bucket[-1]
    d_sat = int(np.max(np.nonzero(bucket != sat)[0])) + 1
    n_full = (d_sat + t - 1 + t - 1) // t
    n_tiles = n_full + 1
    assert n_full * t - (t - 1) >= d_sat
    n = np.arange(2 * t)[None, :]
    base = np.arange(n_tiles)[:, None] * t
    dist = np.clip(np.where(n < t, base - n, base + 2 * t - n), 0, None)
    w = rel_bias_b.T[:, bucket[dist]].astype(F32)
    rep = jnp.broadcast_to(w[:, :, None, :], (B_HEADS, n_tiles, t, 2 * t))
    flat = rep.reshape(B_HEADS, n_tiles, 2 * t * t)[:, :, :t * (2 * t - 1)]
    return flat.reshape(B_HEADS, n_tiles, t, 2 * t - 1)[:, :, :, :t]


def _logsig_cumsum_body(f_ref, b_ref, c_ref, carry_scr):
    t = f_ref.shape[0]

    @pl.when(pl.program_id(1) == 0)
    def _():
        carry_scr[...] = jnp.zeros_like(carry_scr)

    x = f_ref[...] + b_ref[...]
    ls = jnp.minimum(x, 0.0) - jnp.log1p(jnp.exp(-jnp.abs(x)))
    row = lax.broadcasted_iota(I32, (t, t), 0)
    col = lax.broadcasted_iota(I32, (t, t), 1)
    tri = (row >= col).astype(F32)
    c = jnp.dot(tri, ls, preferred_element_type=F32,
                precision=lax.Precision.HIGHEST) + carry_scr[...]
    c_ref[...] = c
    carry_scr[...] = c[t - 1:t, :]


def logsig_cumsum(fg, b_f, batch, seq, t=512):
    return pl.pallas_call(
        _logsig_cumsum_body,
        grid=(batch, seq // t),
        in_specs=[pl.BlockSpec((None, t, LANES), lambda b, i: (b, i, 0)),
                  pl.BlockSpec((1, LANES), lambda b, i: (0, 0))],
        out_specs=pl.BlockSpec((None, t, LANES), lambda b, i: (b, i, 0)),
        out_shape=jax.ShapeDtypeStruct((batch, seq, LANES), F32),
        scratch_shapes=[pltpu.VMEM((1, LANES), F32)],
        compiler_params=pltpu.CompilerParams(
            dimension_semantics=("parallel", "arbitrary")),
        name="logsig_cumsum",
    )(fg, b_f)


def _fox_body(q_ref, k_ref, v_ref, cq_ref, ck_ref, o_ref, m_scr, acc_scr):
    t = ATT_BLOCK
    qi = pl.program_id(2)
    q = q_ref[...] * QK_SCALE
    lane = lax.broadcasted_iota(I32, (1, LANES), 1)
    own = [(lane < HEAD_DIM) == (a == 0) for a in range(2)]
    qa = [jnp.where(own[a], q, jnp.zeros_like(q)) for a in range(2)]
    cqb = [jnp.broadcast_to(cq_ref[a], (t, LANES)) for a in range(2)]
    for a in range(2):
        m_scr[a] = jnp.full((t, LANES), NEG_INF, F32)
        acc_scr[a] = jnp.zeros((t, LANES), F32)

    def step(j, masked):
        off = pl.multiple_of(j * t, t)
        ks = k_ref[pl.ds(off, t), :]
        vs = v_ref[pl.ds(off, t), :]
        if masked:
            row = lax.broadcasted_iota(I32, (t, t), 0)
            col = lax.broadcasted_iota(I32, (t, t), 1)
            causal = row >= col
        for a in range(2):
            s = lax.dot_general(qa[a], ks, (((1,), (1,)), ((), ())),
                                preferred_element_type=F32)
            s = s - ck_ref[a, :, pl.ds(off, t)]
            if masked:
                s = jnp.where(causal, s, NEG_INF)
            m_prev = m_scr[a]
            m_new = jnp.maximum(m_prev, jnp.max(s, axis=-1, keepdims=True) + cqb[a])
            alpha = jnp.exp(m_prev - m_new)
            shift = m_new - cqb[a]
            pb = jnp.exp(s - jnp.concatenate([shift] * (t // LANES), axis=1)).astype(BF16)
            v_aug = jnp.where(own[a], vs, jnp.ones_like(vs))
            acc_scr[a] = alpha * acc_scr[a] + jnp.dot(pb, v_aug, preferred_element_type=F32)
            m_scr[a] = m_new

    def loop_body(j, carry):
        step(j, False)
        return carry

    lax.fori_loop(0, qi, loop_body, 0)
    step(qi, True)
    r = [acc_scr[a] / pltpu.roll(acc_scr[a], HEAD_DIM, 1) for a in range(2)]
    o_ref[...] = jnp.where(own[0], r[0], r[1]).astype(o_ref.dtype)


def fox_attention(p, cq, ck, batch, seq):
    t = ATT_BLOCK
    nk = C_WIDTH // LANES
    return pl.pallas_call(
        _fox_body,
        grid=(batch, C_HEADS // 2, seq // t),
        in_specs=[
            pl.BlockSpec((None, t, LANES), lambda b, h, i: (b, i, h)),
            pl.BlockSpec((None, seq, LANES), lambda b, h, i: (b, 0, nk + h)),
            pl.BlockSpec((None, seq, LANES), lambda b, h, i: (b, 0, 2 * nk + h)),
            pl.BlockSpec((None, 2, t, 1), lambda b, h, i: (b, h, i, 0)),
            pl.BlockSpec((None, 2, 1, seq), lambda b, h, i: (b, h, 0, 0)),
        ],
        out_specs=pl.BlockSpec((None, t, LANES), lambda b, h, i: (b, i, h)),
        out_shape=jax.ShapeDtypeStruct((batch, seq, C_WIDTH), BF16),
        scratch_shapes=[pltpu.VMEM((2, t, LANES), F32), pltpu.VMEM((2, t, LANES), F32)],
        compiler_params=pltpu.CompilerParams(
            dimension_semantics=("parallel", "parallel", "arbitrary")),
        name="fox_attention",
    )(p, p, p, cq, ck)


def _even_out_body(o1, l1, o2, l2, o3, l3, ob_ref, w_ref, x_ref, out_ref, a_scr):
    @pl.when(pl.program_id(1) == 0)
    def _():
        a1, a2, a3 = l1[...], l2[...], l3[...]
        mx = jnp.maximum(jnp.maximum(a1, a2), a3)
        e1, e2, e3 = jnp.exp(a1 - mx), jnp.exp(a2 - mx), jnp.exp(a3 - mx)
        oa = (e1 * o1[...] + e2 * o2[...] + e3 * o3[...]) / (e1 + e2 + e3)
        a_scr[:, :A_WIDTH] = oa.astype(BF16)
        a_scr[:, A_WIDTH:] = ob_ref[...]

    out_ref[...] = x_ref[...] + jnp.dot(a_scr[...], w_ref[...], preferred_element_type=F32)


def even_out_proj(branches, ob, w, x, tm=512, tn=1024):
    m, d = x.shape
    half = pl.BlockSpec((tm, A_WIDTH), lambda i, j: (i, 0))
    flat = [a for pair in branches for a in pair]
    return pl.pallas_call(
        _even_out_body,
        grid=(m // tm, d // tn),
        in_specs=[half] * 6 + [
            pl.BlockSpec((tm, B_V_WIDTH), lambda i, j: (i, 0)),
            pl.BlockSpec((A_WIDTH + B_V_WIDTH, tn), lambda i, j: (0, j)),
            pl.BlockSpec((tm, tn), lambda i, j: (i, j))],
        out_specs=pl.BlockSpec((tm, tn), lambda i, j: (i, j)),
        out_shape=jax.ShapeDtypeStruct((m, d), F32),
        scratch_shapes=[pltpu.VMEM((tm, A_WIDTH + B_V_WIDTH), BF16)],
        compiler_params=pltpu.CompilerParams(
            dimension_semantics=("parallel", "arbitrary")),
        name="even_out_proj",
    )(*flat, ob, w, x)


def _matmul_res_body(a_ref, w_ref, x_ref, o_ref):
    o_ref[...] = x_ref[...] + jnp.dot(a_ref[...], w_ref[...], preferred_element_type=F32)


def matmul_residual(a, w, x, tm=1024, tn=1024):
    m, k = a.shape
    n = w.shape[1]
    return pl.pallas_call(
        _matmul_res_body,
        grid=(m // tm, n // tn),
        in_specs=[pl.BlockSpec((tm, k), lambda i, j: (i, 0)),
                  pl.BlockSpec((k, tn), lambda i, j: (0, j)),
                  pl.BlockSpec((tm, tn), lambda i, j: (i, j))],
        out_specs=pl.BlockSpec((tm, tn), lambda i, j: (i, j)),
        out_shape=jax.ShapeDtypeStruct((m, n), F32),
        compiler_params=pltpu.CompilerParams(
            dimension_semantics=("parallel", "parallel")),
        name="matmul_residual",
    )(a, w, x)


def _add_norm_body(x_ref, r_ref, g_ref, o_ref):
    x = x_ref[...] + r_ref[...]
    ms = jnp.mean(x * x, axis=-1, keepdims=True)
    o_ref[...] = x * lax.rsqrt(ms + RMS_EPS) * g_ref[...]


def add_norm(x, r, g, tm=512):
    m, d = x.shape
    row = pl.BlockSpec((tm, d), lambda i: (i, 0))
    return pl.pallas_call(
        _add_norm_body,
        grid=(m // tm,),
        in_specs=[row, row, pl.BlockSpec((1, d), lambda i: (0, 0))],
        out_specs=row,
        out_shape=jax.ShapeDtypeStruct((m, d), F32),
        compiler_params=pltpu.CompilerParams(dimension_semantics=("parallel",)),
        name="add_norm",
    )(x, r, g.reshape(1, d))


def _peer_candidates():
    groups = [("a", 0, 0), ("a", 0, 8), ("a", 1, 0), ("b", 0, 8), ("a", 2, 0), ("a", 3, 0),
              ("b", 0, 0), ("b", 1, 0), ("b", 2, 0)]
    cid = np.zeros((8 * len(groups), 1), np.int32)
    seen = set()
    for g, (kind, fixed, start) in enumerate(groups):
        for r in range(8):
            a, b = (fixed, start + r) if kind == "a" else (start + r, fixed)
            row = 8 * g + r
            if (a + 1) * (b + 1) <= PEER_TOPK and (a, b) not in seen:
                seen.add((a, b))
                cid[row, 0] = a * PEER_TOPK + b
            else:
                cid[row, 0] = PEER_TOPK * PEER_TOPK + row
    assert len(seen) == sum((a + 1) * (b + 1) <= PEER_TOPK
                            for a in range(PEER_TOPK) for b in range(PEER_TOPK))
    return groups, cid


def _peer_topk_body(q_ref, sk_ref, cid_ref, idx_ref, gate_ref, ts_scr, ti_scr,
                    bs_scr, be_scr, cv_scr, ce_scr, sc_scr, *, groups):
    tm = q_ref.shape[0]
    neg_inf = jnp.float32(-jnp.inf)
    key_id = lax.broadcasted_iota(I32, (1, N_KEYS, tm), 1)
    batch = sc_scr.shape[0]

    def group_body(g, carry):
        for j in range(batch):
            pr = g * batch + j
            off = pl.multiple_of(pr * PEER_KEY_HALF, PEER_KEY_HALF)
            sc_scr[j] = lax.dot_general(sk_ref[pr], q_ref[:, pl.ds(off, PEER_KEY_HALF)],
                                        (((1,), (1,)), ((), ())), preferred_element_type=F32)
        rows = pl.ds(g * batch, batch)

        def k_body(k, c):
            vals = sc_scr[...]
            m = jnp.max(vals, axis=1, keepdims=True)
            sel = jnp.min(jnp.where(vals == m, key_id, N_KEYS), axis=1, keepdims=True)
            ts_scr[rows, pl.ds(k, 1), :] = m
            ti_scr[rows, pl.ds(k, 1), :] = sel
            sc_scr[...] = jnp.where(key_id == sel, neg_inf, vals)
            return c

        lax.fori_loop(0, PEER_TOPK, k_body, 0)
        return carry

    lax.fori_loop(0, 2 * PEER_HEADS // batch, group_body, 0)

    cand_id = cid_ref[...]
    pad = jnp.where(cand_id < PEER_TOPK * PEER_TOPK, 0.0, neg_inf)

    def pick(x1, x2):
        return jnp.concatenate(
            [x1[f:f + 1] + x2[s:s + 8] if kind == "a" else x1[s:s + 8] + x2[f:f + 1]
             for kind, f, s in groups], axis=0)

    def head_body(h, carry):
        cv_scr[h] = pick(ts_scr[2 * h], ts_scr[2 * h + 1]) + pad
        ce_scr[h] = pick(ti_scr[2 * h] * N_KEYS, ti_scr[2 * h + 1])
        return carry

    lax.fori_loop(0, PEER_HEADS, head_body, 0)

    cid3 = cand_id[None]

    def k_body(k, carry):
        vals = cv_scr[...]
        m = jnp.max(vals, axis=1, keepdims=True)
        sel = jnp.min(jnp.where(vals == m, cid3, PEER_TOPK * PEER_TOPK), axis=1, keepdims=True)
        hit = cid3 == sel
        bs_scr[:, pl.ds(k, 1), :] = m
        be_scr[:, pl.ds(k, 1), :] = jnp.sum(jnp.where(hit, ce_scr[...], 0), axis=1, keepdims=True)
        cv_scr[...] = jnp.where(hit, neg_inf, vals)
        return carry

    lax.fori_loop(0, PEER_TOPK, k_body, 0)
    bs = bs_scr[...]
    e = jnp.exp(bs - jnp.max(bs, axis=1, keepdims=True))
    gate = e / jnp.sum(e, axis=1, keepdims=True)
    n_sel = PEER_HEADS * PEER_TOPK
    gate_ref[...] = gate.reshape(n_sel, tm).T
    idx_ref[...] = be_scr[...].reshape(n_sel, tm).T


def peer_topk(q, subkeys, tm=256):
    m = q.shape[0]
    n_sel = PEER_HEADS * PEER_TOPK
    groups, cid = _peer_candidates()
    out_spec = pl.BlockSpec((tm, n_sel), lambda i: (i, 0))
    return pl.pallas_call(
        functools.partial(_peer_topk_body, groups=groups),
        grid=(m // tm,),
        in_specs=[pl.BlockSpec((tm, q.shape[1]), lambda i: (i, 0)),
                  pl.BlockSpec(subkeys.shape, lambda i: (0, 0, 0)),
                  pl.BlockSpec(cid.shape, lambda i: (0, 0))],
        out_specs=[out_spec, out_spec],
        out_shape=[jax.ShapeDtypeStruct((m, n_sel), I32),
                   jax.ShapeDtypeStruct((m, n_sel), F32)],
        scratch_shapes=[pltpu.VMEM((2 * PEER_HEADS, PEER_TOPK, tm), F32),
                        pltpu.VMEM((2 * PEER_HEADS, PEER_TOPK, tm), I32),
                        pltpu.VMEM((PEER_HEADS, PEER_TOPK, tm), F32),
                        pltpu.VMEM((PEER_HEADS, PEER_TOPK, tm), I32),
                        pltpu.VMEM((PEER_HEADS, cid.shape[0], tm), F32),
                        pltpu.VMEM((PEER_HEADS, cid.shape[0], tm), I32),
                        pltpu.VMEM((PEER_HEADS, N_KEYS, tm), F32)],
        compiler_params=pltpu.CompilerParams(dimension_semantics=("parallel",)),
        name="peer_topk",
    )(q, subkeys, jnp.asarray(cid))


def _gelu_gate_body(a_ref, g_ref, o_ref):
    a = a_ref[...]
    o_ref[...] = g_ref[...] * (0.5 * a * (1.0 + lax.erf(a * (2.0 ** -0.5))))


def gelu_gate(act, gate, tm=2048):
    m, n = act.shape
    spec = pl.BlockSpec((tm, n), lambda i: (i, 0))
    return pl.pallas_call(
        _gelu_gate_body,
        grid=(m // tm,),
        in_specs=[spec, spec],
        out_specs=spec,
        out_shape=jax.ShapeDtypeStruct((m, n), F32),
        compiler_params=pltpu.CompilerParams(dimension_semantics=("parallel",)),
        name="gelu_gate",
    )(act, gate)


SC_TOK_CHUNK = 32
SC_DOTS_CHUNK = 64
SC_RING = 8
SC_BF16_GROUP = 4
SC_FMT = plsc.PackFormat.INTERLEAVED


def _sc_worker_id():
    return lax.axis_index("s") * SC_CORES + lax.axis_index("c")


def pack_bf16_pairs(t):
    half = t.shape[-1] // 2
    bits = lax.bitcast_convert_type(t.astype(BF16).astype(F32), I32)
    return (bits[..., half:] & jnp.int32(-65536)) | lax.shift_right_logical(
        bits[..., :half], jnp.int32(16))


def _sc_row_pipeline(idx_v, table_hbm, rows_v, sems, n_items, groups, compute):
    def gather(item):
        tt, g = item // groups, item % groups
        ids = idx_v[tt, pl.ds(g * SC_LANES, SC_LANES)]
        slot = item % SC_RING
        return pltpu.make_async_copy(table_hbm.at[ids], rows_v.at[slot], sems.at[slot])

    for s in range(SC_RING - 1):
        gather(s).start()

    def item_body(item, carry):
        nxt = item + SC_RING - 1

        @pl.when(nxt < n_items)
        def _():
            gather(nxt).start()

        gather(item).wait()
        compute(item // groups, item % groups, item % SC_RING)
        return carry

    lax.fori_loop(0, n_items, item_body, 0)


def peer_expert_dots(hp, idx, up):
    m, dw = hp.shape
    n_sel = idx.shape[1]
    per_w = m // SC_WORKERS
    chunk = SC_DOTS_CHUNK
    n_chunks = per_w // chunk
    groups = n_sel // SC_LANES
    step = SC_BF16_GROUP * SC_LANES
    mesh = plsc.VectorSubcoreMesh(core_axis_name="c", subcore_axis_name="s")

    @functools.partial(
        pl.kernel, mesh=mesh,
        out_type=jax.ShapeDtypeStruct((m, n_sel), F32),
        scratch_types=[
            pltpu.VMEM((chunk, n_sel), I32),
            pltpu.VMEM((chunk, dw), I32),
            pltpu.VMEM((chunk, n_sel), F32),
            pltpu.VMEM((SC_RING, SC_LANES, dw), I32),
            pltpu.VMEM((SC_LANES * SC_LANES,), F32),
            pltpu.SemaphoreType.DMA((SC_RING,)),
        ],
        compiler_params=pltpu.CompilerParams(needs_layout_passes=False),
        name="peer_expert_dots",
    )
    def k(h_hbm, idx_hbm, u_hbm, act_hbm, idx_v, h_v, act_v, rows_v, part_v, sems):
        base = _sc_worker_id() * per_w
        lane = lax.broadcasted_iota(I32, (SC_LANES,), 0)

        def compute(tt, g, slot):
            def grp_body(q, accs):
                off = pl.multiple_of(q * step, step)
                xs = [plsc.bitcast(h_v[tt, pl.ds(off + c * SC_LANES, SC_LANES)], BF16)
                      for c in range(SC_BF16_GROUP)]
                new = []
                for e in range(SC_LANES):
                    s = None
                    for c in range(SC_BF16_GROUP):
                        p = plsc.bitcast(
                            rows_v[slot, e, pl.ds(off + c * SC_LANES, SC_LANES)], BF16) * xs[c]
                        s = p if s is None else s + p
                    lo, hi = plsc.unpack(s, format=SC_FMT)
                    new.append(accs[e] + (lo + hi))
                return tuple(new)

            accs = lax.fori_loop(
                0, dw // step, grp_body,
                tuple(jnp.zeros((SC_LANES,), F32) for _ in range(SC_LANES)))
            for e in range(SC_LANES):
                part_v[pl.ds(e * SC_LANES, SC_LANES)] = accs[e]
            tot = jnp.zeros((SC_LANES,), F32)
            for l in range(SC_LANES):
                tot = tot + plsc.load_gather(part_v, [lane * SC_LANES + l])
            act_v[tt, pl.ds(g * SC_LANES, SC_LANES)] = tot

        def chunk_body(c, carry):
            t0 = base + c * chunk
            pltpu.sync_copy(idx_hbm.at[pl.ds(t0, chunk)], idx_v)
            pltpu.sync_copy(h_hbm.at[pl.ds(t0, chunk)], h_v)
            _sc_row_pipeline(idx_v, u_hbm, rows_v, sems, chunk * groups, groups, compute)
            pltpu.sync_copy(act_v, act_hbm.at[pl.ds(t0, chunk)])
            return carry

        lax.fori_loop(0, n_chunks, chunk_body, 0)

    return k(hp, idx, up)


def peer_expert_combine(w, idx, vp):
    m, n_sel = w.shape
    dw = vp.shape[1]
    d = 2 * dw
    per_w = m // SC_WORKERS
    n_chunks = per_w // SC_TOK_CHUNK
    n_vec = d // SC_LANES
    groups = n_sel // SC_LANES
    mesh = plsc.VectorSubcoreMesh(core_axis_name="c", subcore_axis_name="s")

    @functools.partial(
        pl.kernel, mesh=mesh,
        out_type=jax.ShapeDtypeStruct((m, d), F32),
        scratch_types=[
            pltpu.VMEM((SC_TOK_CHUNK, n_sel), I32),
            pltpu.VMEM((SC_TOK_CHUNK, n_sel), F32),
            pltpu.VMEM((SC_TOK_CHUNK, d), F32),
            pltpu.VMEM((SC_RING, SC_LANES, dw), I32),
            pltpu.SemaphoreType.DMA((SC_RING,)),
        ],
        compiler_params=pltpu.CompilerParams(needs_layout_passes=False),
        name="peer_expert_combine",
    )
    def k(w_hbm, idx_hbm, v_hbm, out_hbm, idx_v, w_v, out_v, rows_v, sems):
        base = _sc_worker_id() * per_w

        def compute(tt, g, slot):
            splat = []
            for e in range(SC_LANES):
                s = plsc.load_gather(w_v, [jnp.full((SC_LANES,), tt, I32),
                                           jnp.full((SC_LANES,), g * SC_LANES + e, I32)])
                splat.append(plsc.pack(s, s, format=SC_FMT))

            @plsc.parallel_loop(0, dw // SC_LANES)
            def _(j):
                off = pl.multiple_of(j * SC_LANES, SC_LANES)
                acc_lo = out_v[tt, pl.ds(off, SC_LANES)]
                acc_hi = out_v[tt, pl.ds(dw + off, SC_LANES)]
                for e0 in range(0, SC_LANES, SC_BF16_GROUP):
                    s = None
                    for e in range(e0, e0 + SC_BF16_GROUP):
                        p = plsc.bitcast(rows_v[slot, e, pl.ds(off, SC_LANES)], BF16) * splat[e]
                        s = p if s is None else s + p
                    lo, hi = plsc.unpack(s, format=SC_FMT)
                    acc_lo = acc_lo + lo
                    acc_hi = acc_hi + hi
                out_v[tt, pl.ds(off, SC_LANES)] = acc_lo
                out_v[tt, pl.ds(dw + off, SC_LANES)] = acc_hi

        def chunk_body(c, carry):
            t0 = base + c * SC_TOK_CHUNK
            pltpu.sync_copy(idx_hbm.at[pl.ds(t0, SC_TOK_CHUNK)], idx_v)
            pltpu.sync_copy(w_hbm.at[pl.ds(t0, SC_TOK_CHUNK)], w_v)

            def zero_body(z, carry2):
                tt, j = z // n_vec, z % n_vec
                out_v[tt, pl.ds(pl.multiple_of(j * SC_LANES, SC_LANES), SC_LANES)] = (
                    jnp.zeros((SC_LANES,), F32))
                return carry2

            lax.fori_loop(0, SC_TOK_CHUNK * n_vec, zero_body, 0)
            _sc_row_pipeline(idx_v, v_hbm, rows_v, sems, SC_TOK_CHUNK * groups, groups, compute)
            pltpu.sync_copy(out_v, out_hbm.at[pl.ds(t0, SC_TOK_CHUNK)])
            return carry

        lax.fori_loop(0, n_chunks, chunk_body, 0)

    return k(w, idx, vp)


def kernel(x, norm_mix_g, norm_ffn_g, final_norm_g, rel_bias, even_w_in, even_w_out,
           diff_lambda, diff_ln_g, odd_w_in, odd_b_f, odd_w_out, peer_wq, peer_subkeys,
           peer_u, peer_v):
    batch, seq, d = x.shape

    dil_tiles = [dilated_bias_tile(rel_bias[:, :A_HEADS], w, dl) for w, dl in DILATED_BRANCHES]
    diff_tiles = diff_bias_tiles(rel_bias[:, A_HEADS:], seq)
    lam_init = 0.8 - 0.6 * math.exp(-0.3 * 0)
    even_in, even_out = even_w_in[0].astype(BF16), even_w_out[0].astype(BF16)
    w_in = odd_w_in[0]
    odd_in, odd_out = w_in[:, :3 * C_WIDTH].astype(BF16), odd_w_out[0].astype(BF16)
    w_gate = jnp.pad(w_in[:, 3 * C_WIDTH:], ((0, 0), (0, LANES - C_HEADS)))
    b_f = jnp.pad(odd_b_f[0], (0, LANES - C_HEADS)).reshape(1, LANES)
    peer = [dict(wq=peer_wq[l].astype(BF16),
                 sk=peer_subkeys[l].reshape(2 * PEER_HEADS, N_KEYS, PEER_KEY_HALF).astype(BF16))
            for l in range(2)]
    packed_tables = {}

    def expert_table(table, layer, anchor):
        if (id(table), layer) not in packed_tables:
            rows, _ = lax.optimization_barrier((table[layer], anchor))
            packed_tables[(id(table), layer)] = pack_bf16_pairs(rows)
        return packed_tables[(id(table), layer)]

    bg = batch // BATCH_GROUPS
    m = bg * seq
    groups = [dict(x=x[g * bg:(g + 1) * bg].reshape(m, d)) for g in range(BATCH_GROUPS)]

    def peer_select(st, layer, after=None):
        q, hp = norm_matmul(st["x"], norm_ffn_g[layer], peer[layer]["wq"], want_h="packed")
        idx, gate = peer_topk(q, peer[layer]["sk"])
        if after is not None and after["st"] is not st:
            after["st"]["peer"], idx = lax.optimization_barrier((after["st"]["peer"], idx))
        return dict(st=st, layer=layer, idx=idx, gate=gate,
                    act=peer_expert_dots(hp, idx, expert_table(peer_u, layer, st["x"])))

    def peer_combine(rec):
        rec["w"] = gelu_gate(rec["act"], rec["gate"])
        rec["st"]["peer"] = peer_expert_combine(
            rec["w"], rec["idx"], expert_table(peer_v, rec["layer"], rec["act"]))

    def even_layer(st, tie):
        (p,) = norm_matmul(st["x"], norm_mix_g[0], even_in)
        branches = [dilated_branch(p, tile, dl, bg, seq)
                    for tile, (_, dl) in zip(dil_tiles, DILATED_BRANCHES)]
        ob = tie(diff_attention(p.reshape(bg, seq, -1), diff_tiles, diff_lambda[0], diff_ln_g[0],
                                lam_init, bg, seq))
        st["x"] = even_out_proj(branches, ob.reshape(m, B_V_WIDTH), even_out, st["x"])

    def odd_layer(st, tie):
        p, st["x"], h = norm_matmul(st["x"], norm_mix_g[1], odd_in, res=st["peer"], want_h="f32")
        fg = gate_matmul(h, w_gate)
        c = logsig_cumsum(fg.reshape(bg, seq, LANES), b_f, bg, seq)
        ct = c[:, :, :C_HEADS].transpose(0, 2, 1)
        o = tie(fox_attention(p.reshape(bg, seq, -1), ct[:, :, :, None], ct[:, :, None, :],
                              bg, seq))
        st["x"] = matmul_residual(o.reshape(m, C_WIDTH), odd_out, st["x"])

    stages = [(even_layer, st, 0) for st in groups] + [(odd_layer, st, 1) for st in groups]
    recs = []
    for k, (mixer, st, layer) in enumerate(stages):
        def tie(t, prev=recs[k - 1] if k >= 1 else None):
            if prev is None:
                return t
            prev["act"], t = lax.optimization_barrier((prev["act"], t))
            peer_combine(prev)
            prev["w"], t = lax.optimization_barrier((prev["w"], t))
            return t

        mixer(st, tie)
        recs.append(peer_select(st, layer, after=recs[k - 2] if k >= 2 else None))
    peer_combine(recs[-1])
    outs = [add_norm(st["x"], st["peer"], final_norm_g).reshape(bg, seq, d) for st in groups]
    return jnp.concatenate(outs, axis=0)


def _gate_matmul_body(h_ref, w_ref, o_ref):
    o_ref[...] = jnp.dot(h_ref[...], w_ref[...], preferred_element_type=F32,
                         precision=lax.Precision.HIGHEST)


def gate_matmul(h, w, tm=512):
    m, d = h.shape
    n = w.shape[1]
    return pl.pallas_call(
        _gate_matmul_body,
        grid=(m // tm,),
        in_specs=[pl.BlockSpec((tm, d), lambda i: (i, 0)),
                  pl.BlockSpec((d, n), lambda i: (0, 0))],
        out_specs=pl.BlockSpec((tm, n), lambda i: (i, 0)),
        out_shape=jax.ShapeDtypeStruct((m, n), F32),
        compiler_params=pltpu.CompilerParams(dimension_semantics=("parallel",)),
        name="gate_matmul",
    )(h, w)
```

```python
import functools
import math

import numpy as np
import jax
import jax.numpy as jnp
from jax import lax
from jax.experimental import pallas as pl
from jax.experimental.pallas import tpu as pltpu
from jax.experimental.pallas import tpu_sc as plsc

F32 = jnp.float32
BF16 = jnp.bfloat16
I32 = jnp.int32

HEAD_DIM = 64
A_HEADS = 8
DILATED_BRANCHES = ((128, 1), (512, 4), (2048, 16))
DIFF_HALF = 64
DIFF_VDIM = 128
B_HEADS = 4
C_HEADS = 16
N_BUCKETS = 32
MAX_DISTANCE = 2048
PEER_HEADS = 8
N_KEYS = 128
PEER_TOPK = 16
PEER_KEY_HALF = 128
RMS_EPS = 1e-6
NEG_INF = -1e30
A_WIDTH = A_HEADS * HEAD_DIM
B_QK_WIDTH = B_HEADS * 2 * DIFF_HALF
B_V_WIDTH = B_HEADS * DIFF_VDIM
C_WIDTH = C_HEADS * HEAD_DIM
QK_SCALE = 0.125
LANES = 128
DIL_BLOCK = 128
ATT_BLOCK = 512
BATCH_GROUPS = 8

SC_CORES = 2
SC_SUBCORES = 16
SC_LANES = 16
SC_WORKERS = SC_CORES * SC_SUBCORES


def _t5_bucket_table(n):
    max_exact = N_BUCKETS // 2
    d = np.arange(n)
    df = np.maximum(d, 1).astype(np.float32)
    large = max_exact + (
        np.log(df / np.float32(max_exact)) / np.float32(math.log(MAX_DISTANCE / max_exact))
        * np.float32(N_BUCKETS - max_exact)).astype(np.int32)
    large = np.minimum(large, N_BUCKETS - 1)
    return np.where(d < max_exact, d, large).astype(np.int32)


def _norm_matmul_body(*refs, has_res, want_h):
    it = iter(refs)
    x_ref = next(it)
    r_ref = next(it) if has_res else None
    g_ref = next(it)
    w_ref = next(it)
    o_ref = next(it)
    xs_ref = next(it) if has_res else None
    hout_ref = next(it) if want_h else None
    h_scr = next(it)

    @pl.when(pl.program_id(1) == 0)
    def _():
        x = x_ref[...]
        if has_res:
            x = x + r_ref[...]
            xs_ref[...] = x
        ms = jnp.mean(x * x, axis=-1, keepdims=True)
        h = x * lax.rsqrt(ms + RMS_EPS) * g_ref[...]
        if want_h == "f32":
            hout_ref[...] = h
        elif want_h == "packed":
            hout_ref[...] = pack_bf16_pairs(h)
        h_scr[...] = h.astype(BF16)

    o_ref[...] = jnp.dot(h_scr[...], w_ref[...],
                         preferred_element_type=F32).astype(o_ref.dtype)


def norm_matmul(x, g, w, *, res=None, want_h=None, out_dtype=BF16, tm=1024, tn=1024):
    m, d = x.shape
    n = w.shape[1]
    tn = min(tn, n)
    row = pl.BlockSpec((tm, d), lambda i, j: (i, 0))
    in_specs = [row] + ([row] if res is not None else []) + [
        pl.BlockSpec((1, d), lambda i, j: (0, 0)),
        pl.BlockSpec((d, tn), lambda i, j: (0, j))]
    out_specs = [pl.BlockSpec((tm, tn), lambda i, j: (i, j))]
    out_shape = [jax.ShapeDtypeStruct((m, n), out_dtype)]
    if res is not None:
        out_specs.append(row)
        out_shape.append(jax.ShapeDtypeStruct((m, d), F32))
    if want_h == "f32":
        out_specs.append(row)
        out_shape.append(jax.ShapeDtypeStruct((m, d), F32))
    elif want_h == "packed":
        out_specs.append(pl.BlockSpec((tm, d // 2), lambda i, j: (i, 0)))
        out_shape.append(jax.ShapeDtypeStruct((m, d // 2), I32))
    args =[x] + ([res] if res is not None else []) + [g.reshape(1, d), w]
    return pl.pallas_call(
        functools.partial(_norm_matmul_body, has_res=res is not None, want_h=want_h),
        grid=(m // tm, n // tn),
        in_specs=in_specs,
        out_specs=out_specs,
        out_shape=out_shape,
        scratch_shapes=[pltpu.VMEM((tm, d), BF16)],
        compiler_params=pltpu.CompilerParams(
            dimension_semantics=("parallel", "arbitrary")),
        name="norm_matmul",
    )(*args)


def _dilated_body(q_ref, kp_ref, kc_ref, vp_ref, vc_ref, b_ref, o_ref, lse_ref):
    i = pl.program_id(2)
    col = lax.broadcasted_iota(I32, (DIL_BLOCK, 2 * DIL_BLOCK), 1)
    has_prev = jnp.logical_or(col >= DIL_BLOCK, i > 0)
    lane = lax.broadcasted_iota(I32, (1, LANES), 1)
    own = [(lane < HEAD_DIM) == (a == 0) for a in range(2)]
    for pr in range(A_HEADS // 2):
        cs = slice(pr * LANES, (pr + 1) * LANES)
        q = q_ref[:, cs] * QK_SCALE
        k = jnp.concatenate([kp_ref[:, cs], kc_ref[:, cs]], axis=0)
        v = jnp.concatenate([vp_ref[:, cs], vc_ref[:, cs]], axis=0)
        outs, lses = [], []
        for a in range(2):
            s = lax.dot_general(jnp.where(own[a], q, jnp.zeros_like(q)), k,
                                (((1,), (1,)), ((), ())), preferred_element_type=F32)
            s = jnp.where(has_prev, s + b_ref[2 * pr + a], NEG_INF)
            m = jnp.max(s, axis=-1, keepdims=True)
            pb = jnp.exp(s - m).astype(BF16)
            acc = jnp.dot(pb, jnp.where(own[a], v, jnp.ones_like(v)), preferred_element_type=F32)
            l = pltpu.roll(acc, HEAD_DIM, 1)
            outs.append(acc / l)
            lses.append(m + jnp.log(l))
        o_ref[:, cs] = jnp.where(own[0], outs[0], outs[1])
        lse_ref[:, cs] = jnp.where(own[0], lses[0], lses[1])


def dilated_branch(p, bias_tile, dil, batch, seq):
    if dil > 1:
        p = p[:, :3 * A_WIDTH]
    n_cols = p.shape[1]
    cb = n_cols // A_WIDTH
    rows = seq // dil
    nblk = rows // DIL_BLOCK
    pv = p.reshape(batch, rows, dil * n_cols)
    blk = (None, DIL_BLOCK, A_WIDTH)

    def spec(which, prev):
        if prev:
            return pl.BlockSpec(blk, lambda b, r, i: (b, jnp.maximum(i - 1, 0), r * cb + which))
        return pl.BlockSpec(blk, lambda b, r, i: (b, i, r * cb + which))

    ospec = pl.BlockSpec(blk, lambda b, r, i: (b, i, r))
    oshape = jax.ShapeDtypeStruct((batch, rows, dil * A_WIDTH), F32)
    o, lse = pl.pallas_call(
        _dilated_body,
        grid=(batch, dil, nblk),
        in_specs=[spec(0, False), spec(1, True), spec(1, False), spec(2, True), spec(2, False),
                  pl.BlockSpec((A_HEADS, DIL_BLOCK, 2 * DIL_BLOCK), lambda b, r, i: (0, 0, 0))],
        out_specs=[ospec, ospec],
        out_shape=[oshape, oshape],
        compiler_params=pltpu.CompilerParams(
            dimension_semantics=("parallel", "parallel", "arbitrary")),
        name=f"dilated_d{dil}",
    )(pv, pv, pv, pv, pv, bias_tile)
    return o.reshape(batch * seq, A_WIDTH), lse.reshape(batch * seq, A_WIDTH)


def dilated_bias_tile(rel_bias_a, window, dil):
    n = window // dil
    assert n == DIL_BLOCK
    bucket = _t5_bucket_table(window + 1)
    period = 4 * n
    u = np.arange(period)
    valid = u <= n
    w = jnp.where(jnp.asarray(valid)[None],
                  rel_bias_a.T[:, bucket[np.where(valid, n - u, 0) * dil]], NEG_INF).astype(F32)
    rep = jnp.broadcast_to(w[:, None, :], (A_HEADS, n, period))
    flat = rep.reshape(A_HEADS, n * period)[:, :n * (period - 1)]
    return flat.reshape(A_HEADS, n, period - 1)[:, :, :2 * n]


def _dilated_fused_body(q_ref, k_ref, v_ref, b_ref, o_ref, qf, kf, vf, m_scr, acc_scr, *, seq):
    qf[...] = q_ref[...].astype(F32) * QK_SCALE
    kf[...] = k_ref[...].astype(F32)
    vf[...] = v_ref[...].astype(F32)
    m_scr[...] = jnp.full(m_scr.shape, NEG_INF, F32)
    acc_scr[...] = jnp.zeros(acc_scr.shape, F32)
    n = DIL_BLOCK
    col = lax.broadcasted_iota(I32, (n, 2 * n), 1)
    lane = lax.broadcasted_iota(I32, (1, LANES), 1)
    own = [(lane < HEAD_DIM) == (a == 0) for a in range(2)]
    for bi, (_, dil) in enumerate(DILATED_BRANCHES):
        span = dil * n

        def class_body(r, carry, bi=bi, dil=dil, span=span):
            def block_body(i, carry2):
                start = r + span * i
                rows = pl.ds(start, n, stride=dil)
                prev = pl.ds(jnp.maximum(start - span, r), n, stride=dil)
                q = qf[rows, :].astype(BF16)
                k = jnp.concatenate([kf[prev, :], kf[rows, :]], axis=0).astype(BF16)
                v = jnp.concatenate([vf[prev, :], vf[rows, :]], axis=0).astype(BF16)
                has_prev = jnp.logical_or(col >= n, i > 0)
                for a in range(2):
                    s = lax.dot_general(jnp.where(own[a], q, jnp.zeros_like(q)), k,
                                        (((1,), (1,)), ((), ())), preferred_element_type=F32)
                    s = jnp.where(has_prev, s + b_ref[bi, a], NEG_INF)
                    m_prev = m_scr[a, rows, :]
                    m_new = jnp.maximum(m_prev, jnp.max(s, axis=-1, keepdims=True))
                    alpha = jnp.exp(m_prev - m_new)
                    pb = jnp.exp(s - jnp.concatenate([m_new, m_new], axis=1)).astype(BF16)
                    acc_scr[a, rows, :] = alpha * acc_scr[a, rows, :] + jnp.dot(
                        pb, jnp.where(own[a], v, jnp.ones_like(v)), preferred_element_type=F32)
                    m_scr[a, rows, :] = m_new
                return carry2

            lax.fori_loop(0, seq // span, block_body, 0)
            return carry

        lax.fori_loop(0, dil, class_body, 0)
    r = [acc_scr[a] / pltpu.roll(acc_scr[a], HEAD_DIM, 1) for a in range(2)]
    o_ref[...] = jnp.where(own[0], r[0], r[1]).astype(o_ref.dtype)


def dilated_attention(p, bias_tiles, batch, seq):
    nk = A_WIDTH // LANES
    blk = (None, seq, LANES)
    return pl.pallas_call(
        functools.partial(_dilated_fused_body, seq=seq),
        grid=(batch, A_HEADS // 2),
        in_specs=[pl.BlockSpec(blk, lambda b, h: (b, 0, h)),
                  pl.BlockSpec(blk, lambda b, h: (b, 0, nk + h)),
                  pl.BlockSpec(blk, lambda b, h: (b, 0, 2 * nk + h)),
                  pl.BlockSpec((len(DILATED_BRANCHES), 2, DIL_BLOCK, 2 * DIL_BLOCK),
                               lambda b, h: (0, h, 0, 0))],
        out_specs=pl.BlockSpec(blk, lambda b, h: (b, 0, h)),
        out_shape=jax.ShapeDtypeStruct((batch, seq, A_WIDTH), BF16),
        scratch_shapes=[pltpu.VMEM((seq, LANES), F32)] * 3 + [
            pltpu.VMEM((2, seq, LANES), F32), pltpu.VMEM((2, seq, LANES), F32)],
        compiler_params=pltpu.CompilerParams(dimension_semantics=("parallel", "parallel")),
        name="dilated_attention",
    )(p, p, p, bias_tiles)


def _diff_body(q_ref, k_ref, v_ref, b_ref, lam_ref, g_ref, o_ref,
               m_scr, l_scr, acc_scr, *, n_tiles, lam_init):
    t = ATT_BLOCK
    qi = pl.program_id(2)
    q = q_ref[...] * QK_SCALE
    lane = lax.broadcasted_iota(I32, (1, LANES), 1)
    qa = [jnp.where((lane < DIFF_HALF) == (a == 0), q, jnp.zeros_like(q)) for a in range(2)]
    ones = jnp.ones((t, LANES), BF16)
    for a in range(2):
        m_scr[a] = jnp.full((t, LANES), NEG_INF, F32)
        l_scr[a] = jnp.zeros((t, LANES), F32)
        acc_scr[a] = jnp.zeros((t, DIFF_VDIM), F32)

    def step(j, masked):
        off = pl.multiple_of(j * t, t)
        ks = k_ref[pl.ds(off, t), :]
        vs = v_ref[pl.ds(off, t), :]
        bias = b_ref[jnp.minimum(qi - j, n_tiles - 1)]
        if masked:
            row = lax.broadcasted_iota(I32, (t, t), 0)
            col = lax.broadcasted_iota(I32, (t, t), 1)
            causal = row >= col
        for a in range(2):
            s = lax.dot_general(qa[a], ks, (((1,), (1,)), ((), ())),
                                preferred_element_type=F32) + bias
            if masked:
                s = jnp.where(causal, s, NEG_INF)
            m_prev = m_scr[a]
            m_new = jnp.maximum(m_prev, jnp.max(s, axis=-1, keepdims=True))
            alpha = jnp.exp(m_prev - m_new)
            pb = jnp.exp(s - jnp.concatenate([m_new] * (t // LANES), axis=1)).astype(BF16)
            l_scr[a] = alpha * l_scr[a] + jnp.dot(pb, ones, preferred_element_type=F32)
            acc_scr[a] = alpha * acc_scr[a] + jnp.dot(pb, vs, preferred_element_type=F32)
            m_scr[a] = m_new

    def loop_body(j, carry):
        step(j, False)
        return carry

    lax.fori_loop(0, qi, loop_body, 0)
    step(qi, True)

    lp = lam_ref[...]
    lam = (jnp.exp(jnp.sum(lp[0:1] * lp[1:2])) - jnp.exp(jnp.sum(lp[2:3] * lp[3:4]))
           + lam_init)
    o = acc_scr[0] / l_scr[0] - lam * (acc_scr[1] / l_scr[1])
    ms = jnp.mean(o * o, axis=-1, keepdims=True)
    y = o * lax.rsqrt(ms + RMS_EPS) * g_ref[...]
    o_ref[...] = (y * (1.0 - lam_init)).astype(o_ref.dtype)


def diff_attention(p, bias_tiles, lam_params, ln_g, lam_init, batch, seq):
    t = ATT_BLOCK
    n_tiles = bias_tiles.shape[1]
    cq = 3 * A_WIDTH // LANES
    ck = cq + B_QK_WIDTH // LANES
    cv = ck + B_QK_WIDTH // LANES
    return pl.pallas_call(
        functools.partial(_diff_body, n_tiles=n_tiles, lam_init=lam_init),
        grid=(batch, B_HEADS, seq // t),
        in_specs=[
            pl.BlockSpec((None, t, LANES), lambda b, h, i: (b, i, cq + h)),
            pl.BlockSpec((None, seq, LANES), lambda b, h, i: (b, 0, ck + h)),
            pl.BlockSpec((None, seq, LANES), lambda b, h, i: (b, 0, cv + h)),
            pl.BlockSpec((None, n_tiles, t, t), lambda b, h, i: (h, 0, 0, 0)),
            pl.BlockSpec((4, DIFF_HALF), lambda b, h, i: (0, 0)),
            pl.BlockSpec((1, DIFF_VDIM), lambda b, h, i: (0, 0)),
        ],
        out_specs=pl.BlockSpec((None, t, LANES), lambda b, h, i: (b, i, h)),
        out_shape=jax.ShapeDtypeStruct((batch, seq, B_V_WIDTH), BF16),
        scratch_shapes=[pltpu.VMEM((2, t, LANES), F32), pltpu.VMEM((2, t, LANES), F32),
                        pltpu.VMEM((2, t, DIFF_VDIM), F32)],
        compiler_params=pltpu.CompilerParams(
            dimension_semantics=("parallel", "parallel", "arbitrary")),
        name="diff_attention",
    )(p, p, p, bias_tiles, lam_params, ln_g.reshape(1, DIFF_VDIM))


def diff_bias_tiles(rel_bias_b, seq):
    t = ATT_BLOCK
    bucket = _t5_bucket_table(max(seq, 2 * MAX_DISTANCE) + 2 * t)
    sat = bucket[-1]
    d_sat = int(np.max(np.nonzero(bucket != sat)[0])) + 1
    n_full = (d_sat + t - 1 + t - 1) // t
    n_tiles = n_full + 1
    assert n_full * t - (t - 1) >= d_sat
    n = np.arange(2 * t)[None, :]
    base = np.arange(n_tiles)[:, None] * t
    dist = np.clip(np.where(n < t, base - n, base + 2 * t - n), 0, None)
    w = rel_bias_b.T[:, bucket[dist]].astype(F32)
    rep = jnp.broadcast_to(w[:, :, None, :], (B_HEADS, n_tiles, t, 2 * t))
    flat = rep.reshape(B_HEADS, n_tiles, 2 * t * t)[:, :, :t * (2 * t - 1)]
    return flat.reshape(B_HEADS, n_tiles, t, 2 * t - 1)[:, :, :, :t]


def _logsig_cumsum_body(f_ref, b_ref, c_ref, carry_scr):
    t = f_ref.shape[0]

    @pl.when(pl.program_id(1) == 0)
    def _():
        carry_scr[...] = jnp.zeros_like(carry_scr)

    x = f_ref[...] + b_ref[...]
    ls = jnp.minimum(x, 0.0) - jnp.log1p(jnp.exp(-jnp.abs(x)))
    row = lax.broadcasted_iota(I32, (t, t), 0)
    col = lax.broadcasted_iota(I32, (t, t), 1)
    tri = (row >= col).astype(F32)
    c = jnp.dot(tri, ls, preferred_element_type=F32,
                precision=lax.Precision.HIGHEST) + carry_scr[...]
    c_ref[...] = c
    carry_scr[...] = c[t - 1:t, :]


def logsig_cumsum(fg, b_f, batch, seq, t=512):
    return pl.pallas_call(
        _logsig_cumsum_body,
        grid=(batch, seq // t),
        in_specs=[pl.BlockSpec((None, t, LANES), lambda b, i: (b, i, 0)),
                  pl.BlockSpec((1, LANES), lambda b, i: (0, 0))],
        out_specs=pl.BlockSpec((None, t, LANES), lambda b, i: (b, i, 0)),
        out_shape=jax.ShapeDtypeStruct((batch, seq, LANES), F32),
        scratch_shapes=[pltpu.VMEM((1, LANES), F32)],
        compiler_params=pltpu.CompilerParams(
            dimension_semantics=("parallel", "arbitrary")),
        name="logsig_cumsum",
    )(fg, b_f)


def _fox_body(q_ref, k_ref, v_ref, cq_ref, ck_ref, o_ref, m_scr, acc_scr):
    t = ATT_BLOCK
    qi = pl.program_id(2)
    q = q_ref[...] * QK_SCALE
    lane = lax.broadcasted_iota(I32, (1, LANES), 1)
    own = [(lane < HEAD_DIM) == (a == 0) for a in range(2)]
    qa = [jnp.where(own[a], q, jnp.zeros_like(q)) for a in range(2)]
    cqb = [jnp.broadcast_to(cq_ref[a], (t, LANES)) for a in range(2)]
    for a in range(2):
        m_scr[a] = jnp.full((t, LANES), NEG_INF, F32)
        acc_scr[a] = jnp.zeros((t, LANES), F32)

    def step(j, masked):
        off = pl.multiple_of(j * t, t)
        ks = k_ref[pl.ds(off, t), :]
        vs = v_ref[pl.ds(off, t), :]
        if masked:
            row = lax.broadcasted_iota(I32, (t, t), 0)
            col = lax.broadcasted_iota(I32, (t, t), 1)
            causal = row >= col
        for a in range(2):
            s = lax.dot_general(qa[a], ks, (((1,), (1,)), ((), ())),
                                preferred_element_type=F32)
            s = s - ck_ref[a, :, pl.ds(off, t)]
            if masked:
                s = jnp.where(causal, s, NEG_INF)
            m_prev = m_scr[a]
            m_new = jnp.maximum(m_prev, jnp.max(s, axis=-1, keepdims=True) + cqb[a])
            alpha = jnp.exp(m_prev - m_new)
            shift = m_new - cqb[a]
            pb = jnp.exp(s - jnp.concatenate([shift] * (t // LANES), axis=1)).astype(BF16)
            v_aug = jnp.where(own[a], vs, jnp.ones_like(vs))
            acc_scr[a] = alpha * acc_scr[a] + jnp.dot(pb, v_aug, preferred_element_type=F32)
            m_scr[a] = m_new

    def loop_body(j, carry):
        step(j, False)
        return carry

    lax.fori_loop(0, qi, loop_body, 0)
    step(qi, True)
    r = [acc_scr[a] / pltpu.roll(acc_scr[a], HEAD_DIM, 1) for a in range(2)]
    o_ref[...] = jnp.where(own[0], r[0], r[1]).astype(o_ref.dtype)


def fox_attention(p, cq, ck, batch, seq):
    t = ATT_BLOCK
    nk = C_WIDTH // LANES
    return pl.pallas_call(
        _fox_body,
        grid=(batch, C_HEADS // 2, seq // t),
        in_specs=[
            pl.BlockSpec((None, t, LANES), lambda b, h, i: (b, i, h)),
            pl.BlockSpec((None, seq, LANES), lambda b, h, i: (b, 0, nk + h)),
            pl.BlockSpec((None, seq, LANES), lambda b, h, i: (b, 0, 2 * nk + h)),
            pl.BlockSpec((None, 2, t, 1), lambda b, h, i: (b, h, i, 0)),
            pl.BlockSpec((None, 2, 1, seq), lambda b, h, i: (b, h, 0, 0)),
        ],
        out_specs=pl.BlockSpec((None, t, LANES), lambda b, h, i: (b, i, h)),
        out_shape=jax.ShapeDtypeStruct((batch, seq, C_WIDTH), BF16),
        scratch_shapes=[pltpu.VMEM((2, t, LANES), F32), pltpu.VMEM((2, t, LANES), F32)],
        compiler_params=pltpu.CompilerParams(
            dimension_semantics=("parallel", "parallel", "arbitrary")),
        name="fox_attention",
    )(p, p, p, cq, ck)


def _even_out_body(o1, l1, o2, l2, o3, l3, ob_ref, w_ref, x_ref, out_ref, a_scr):
    @pl.when(pl.program_id(1) == 0)
    def _():
        a1, a2, a3 = l1[...], l2[...], l3[...]
        mx = jnp.maximum(jnp.maximum(a1, a2), a3)
        e1, e2, e3 = jnp.exp(a1 - mx), jnp.exp(a2 - mx), jnp.exp(a3 - mx)
        oa = (e1 * o1[...] + e2 * o2[...] + e3 * o3[...]) / (e1 + e2 + e3)
        a_scr[:, :A_WIDTH] = oa.astype(BF16)
        a_scr[:, A_WIDTH:] = ob_ref[...]

    out_ref[...] = x_ref[...] + jnp.dot(a_scr[...], w_ref[...], preferred_element_type=F32)


def even_out_proj(branches, ob, w, x, tm=512, tn=1024):
    m, d = x.shape
    half = pl.BlockSpec((tm, A_WIDTH), lambda i, j: (i, 0))
    flat = [a for pair in branches for a in pair]
    return pl.pallas_call(
        _even_out_body,
        grid=(m // tm, d // tn),
        in_specs=[half] * 6 + [
            pl.BlockSpec((tm, B_V_WIDTH), lambda i, j: (i, 0)),
            pl.BlockSpec((A_WIDTH + B_V_WIDTH, tn), lambda i, j: (0, j)),
            pl.BlockSpec((tm, tn), lambda i, j: (i, j))],
        out_specs=pl.BlockSpec((tm, tn), lambda i, j: (i, j)),
        out_shape=jax.ShapeDtypeStruct((m, d), F32),
        scratch_shapes=[pltpu.VMEM((tm, A_WIDTH + B_V_WIDTH), BF16)],
        compiler_params=pltpu.CompilerParams(
            dimension_semantics=("parallel", "arbitrary")),
        name="even_out_proj",
    )(*flat, ob, w, x)


def _concat_out_body(oa_ref, ob_ref, w_ref, x_ref, out_ref):
    out_ref[...] = (x_ref[...]
                    + jnp.dot(oa_ref[...], w_ref[:A_WIDTH, :], preferred_element_type=F32)
                    + jnp.dot(ob_ref[...], w_ref[A_WIDTH:, :], preferred_element_type=F32))


def concat_out_proj(oa, ob, w, x, tm=1024, tn=1024):
    m, d = x.shape
    return pl.pallas_call(
        _concat_out_body,
        grid=(m // tm, d // tn),
        in_specs=[pl.BlockSpec((tm, A_WIDTH), lambda i, j: (i, 0)),
                  pl.BlockSpec((tm, B_V_WIDTH), lambda i, j: (i, 0)),
                  pl.BlockSpec((A_WIDTH + B_V_WIDTH, tn), lambda i, j: (0, j)),
                  pl.BlockSpec((tm, tn), lambda i, j: (i, j))],
        out_specs=pl.BlockSpec((tm, tn), lambda i, j: (i, j)),
        out_shape=jax.ShapeDtypeStruct((m, d), F32),
        compiler_params=pltpu.CompilerParams(dimension_semantics=("parallel", "parallel")),
        name="concat_out_proj",
    )(oa, ob, w, x)


def _matmul_res_body(a_ref, w_ref, x_ref, o_ref):
    o_ref[...] = x_ref[...] + jnp.dot(a_ref[...], w_ref[...], preferred_element_type=F32)


def matmul_residual(a, w, x, tm=1024, tn=1024):
    m, k = a.shape
    n = w.shape[1]
    return pl.pallas_call(
        _matmul_res_body,
        grid=(m // tm, n // tn),
        in_specs=[pl.BlockSpec((tm, k), lambda i, j: (i, 0)),
                  pl.BlockSpec((k, tn), lambda i, j: (0, j)),
                  pl.BlockSpec((tm, tn), lambda i, j: (i, j))],
        out_specs=pl.BlockSpec((tm, tn), lambda i, j: (i, j)),
        out_shape=jax.ShapeDtypeStruct((m, n), F32),
        compiler_params=pltpu.CompilerParams(
            dimension_semantics=("parallel", "parallel")),
        name="matmul_residual",
    )(a, w, x)


def _add_norm_body(x_ref, r_ref, g_ref, o_ref):
    x = x_ref[...] + r_ref[...]
    ms = jnp.mean(x * x, axis=-1, keepdims=True)
    o_ref[...] = x * lax.rsqrt(ms + RMS_EPS) * g_ref[...]


def add_norm(x, r, g, tm=512):
    m, d = x.shape
    row = pl.BlockSpec((tm, d), lambda i: (i, 0))
    return pl.pallas_call(
        _add_norm_body,
        grid=(m // tm,),
        in_specs=[row, row, pl.BlockSpec((1, d), lambda i: (0, 0))],
        out_specs=row,
        out_shape=jax.ShapeDtypeStruct((m, d), F32),
        compiler_params=pltpu.CompilerParams(dimension_semantics=("parallel",)),
        name="add_norm",
    )(x, r, g.reshape(1, d))


def _peer_candidates():
    groups = [("a", 0, 0), ("a", 0, 8), ("a", 1, 0), ("b", 0, 8), ("a", 2, 0), ("a", 3, 0),
              ("b", 0, 0), ("b", 1, 0), ("b", 2, 0)]
    cid = np.zeros((8 * len(groups), 1), np.int32)
    seen = set()
    for g, (kind, fixed, start) in enumerate(groups):
        for r in range(8):
            a, b = (fixed, start + r) if kind == "a" else (start + r, fixed)
            row = 8 * g + r
            if (a + 1) * (b + 1) <= PEER_TOPK and (a, b) not in seen:
                seen.add((a, b))
                cid[row, 0] = a * PEER_TOPK + b
            else:
                cid[row, 0] = PEER_TOPK * PEER_TOPK + row
    assert len(seen) == sum((a + 1) * (b + 1) <= PEER_TOPK
                            for a in range(PEER_TOPK) for b in range(PEER_TOPK))
    return groups, cid


def _peer_topk_body(q_ref, sk_ref, cid_ref, idx_ref, gate_ref, ts_scr, ti_scr,
                    bs_scr, be_scr, cv_scr, ce_scr, sc_scr, *, groups):
    tm = q_ref.shape[0]
    neg_inf = jnp.float32(-jnp.inf)
    key_id = lax.broadcasted_iota(I32, (1, N_KEYS, tm), 1)
    batch = sc_scr.shape[0]

    def group_body(g, carry):
        for j in range(batch):
            pr = g * batch + j
            off = pl.multiple_of(pr * PEER_KEY_HALF, PEER_KEY_HALF)
            sc_scr[j] = lax.dot_general(sk_ref[pr], q_ref[:, pl.ds(off, PEER_KEY_HALF)],
                                        (((1,), (1,)), ((), ())), preferred_element_type=F32)
        rows = pl.ds(g * batch, batch)

        def k_body(k, c):
            vals = sc_scr[...]
            m = jnp.max(vals, axis=1, keepdims=True)
            sel = jnp.min(jnp.where(vals == m, key_id, N_KEYS), axis=1, keepdims=True)
            ts_scr[rows, pl.ds(k, 1), :] = m
            ti_scr[rows, pl.ds(k, 1), :] = sel
            sc_scr[...] = jnp.where(key_id == sel, neg_inf, vals)
            return c

        lax.fori_loop(0, PEER_TOPK, k_body, 0)
        return carry

    lax.fori_loop(0, 2 * PEER_HEADS // batch, group_body, 0)

    cand_id = cid_ref[...]
    pad = jnp.where(cand_id < PEER_TOPK * PEER_TOPK, 0.0, neg_inf)

    def pick(x1, x2):
        return jnp.concatenate(
            [x1[f:f + 1] + x2[s:s + 8] if kind == "a" else x1[s:s + 8] + x2[f:f + 1]
             for kind, f, s in groups], axis=0)

    def head_body(h, carry):
        cv_scr[h] = pick(ts_scr[2 * h], ts_scr[2 * h + 1]) + pad
        ce_scr[h] = pick(ti_scr[2 * h] * N_KEYS, ti_scr[2 * h + 1])
        return carry

    lax.fori_loop(0, PEER_HEADS, head_body, 0)

    cid3 = cand_id[None]

    def k_body(k, carry):
        vals = cv_scr[...]
        m = jnp.max(vals, axis=1, keepdims=True)
        sel = jnp.min(jnp.where(vals == m, cid3, PEER_TOPK * PEER_TOPK), axis=1, keepdims=True)
        hit = cid3 == sel
        bs_scr[:, pl.ds(k, 1), :] = m
        be_scr[:, pl.ds(k, 1), :] = jnp.sum(jnp.where(hit, ce_scr[...], 0), axis=1, keepdims=True)
        cv_scr[...] = jnp.where(hit, neg_inf, vals)
        return carry

    lax.fori_loop(0, PEER_TOPK, k_body, 0)
    bs = bs_scr[...]
    e = jnp.exp(bs - jnp.max(bs, axis=1, keepdims=True))
    gate = e / jnp.sum(e, axis=1, keepdims=True)
    n_sel = PEER_HEADS * PEER_TOPK
    gate_ref[...] = gate.reshape(n_sel, tm).T
    idx_ref[...] = be_scr[...].reshape(n_sel, tm).T


def peer_topk(q, subkeys, tm=256):
    m = q.shape[0]
    n_sel = PEER_HEADS * PEER_TOPK
    groups, cid = _peer_candidates()
    out_spec = pl.BlockSpec((tm, n_sel), lambda i: (i, 0))
    return pl.pallas_call(
        functools.partial(_peer_topk_body, groups=groups),
        grid=(m // tm,),
        in_specs=[pl.BlockSpec((tm, q.shape[1]), lambda i: (i, 0)),
                  pl.BlockSpec(subkeys.shape, lambda i: (0, 0, 0)),
                  pl.BlockSpec(cid.shape, lambda i: (0, 0))],
        out_specs=[out_spec, out_spec],
        out_shape=[jax.ShapeDtypeStruct((m, n_sel), I32),
                   jax.ShapeDtypeStruct((m, n_sel), F32)],
        scratch_shapes=[pltpu.VMEM((2 * PEER_HEADS, PEER_TOPK, tm), F32),
                        pltpu.VMEM((2 * PEER_HEADS, PEER_TOPK, tm), I32),
                        pltpu.VMEM((PEER_HEADS, PEER_TOPK, tm), F32),
                        pltpu.VMEM((PEER_HEADS, PEER_TOPK, tm), I32),
                        pltpu.VMEM((PEER_HEADS, cid.shape[0], tm), F32),
                        pltpu.VMEM((PEER_HEADS, cid.shape[0], tm), I32),
                        pltpu.VMEM((PEER_HEADS, N_KEYS, tm), F32)],
        compiler_params=pltpu.CompilerParams(dimension_semantics=("parallel",)),
        name="peer_topk",
    )(q, subkeys, jnp.asarray(cid))


def _gelu_gate_body(a_ref, g_ref, o_ref):
    a = a_ref[...]
    o_ref[...] = g_ref[...] * (0.5 * a * (1.0 + lax.erf(a * (2.0 ** -0.5))))


def gelu_gate(act, gate, tm=2048):
    m, n = act.shape
    spec = pl.BlockSpec((tm, n), lambda i: (i, 0))
    return pl.pallas_call(
        _gelu_gate_body,
        grid=(m // tm,),
        in_specs=[spec, spec],
        out_specs=spec,
        out_shape=jax.ShapeDtypeStruct((m, n), F32),
        compiler_params=pltpu.CompilerParams(dimension_semantics=("parallel",)),
        name="gelu_gate",
    )(act, gate)


SC_TOK_CHUNK = 32
SC_DOTS_CHUNK = 64
SC_RING = 8
SC_BF16_GROUP = 4
SC_FMT = plsc.PackFormat.INTERLEAVED


def _sc_worker_id():
    return lax.axis_index("s") * SC_CORES + lax.axis_index("c")


def pack_bf16_pairs(t):
    half = t.shape[-1] // 2
    bits = lax.bitcast_convert_type(t.astype(BF16).astype(F32), I32)
    return (bits[..., half:] & jnp.int32(-65536)) | lax.shift_right_logical(
        bits[..., :half], jnp.int32(16))


def _sc_row_pipeline(idx_v, table_hbm, rows_v, sems, n_items, groups, compute):
    def gather(item):
        tt, g = item // groups, item % groups
        ids = idx_v[tt, pl.ds(g * SC_LANES, SC_LANES)]
        slot = item % SC_RING
        return pltpu.make_async_copy(table_hbm.at[ids], rows_v.at[slot], sems.at[slot])

    for s in range(SC_RING - 1):
        gather(s).start()

    def item_body(item, carry):
        nxt = item + SC_RING - 1

        @pl.when(nxt < n_items)
        def _():
            gather(nxt).start()

        gather(item).wait()
        compute(item // groups, item % groups, item % SC_RING)
        return carry

    lax.fori_loop(0, n_items, item_body, 0)


def peer_expert_dots(hp, idx, up):
    m, dw = hp.shape
    n_sel = idx.shape[1]
    per_w = m // SC_WORKERS
    chunk = SC_DOTS_CHUNK
    n_chunks = per_w // chunk
    groups = n_sel // SC_LANES
    step = SC_BF16_GROUP * SC_LANES
    mesh = plsc.VectorSubcoreMesh(core_axis_name="c", subcore_axis_name="s")

    @functools.partial(
        pl.kernel, mesh=mesh,
        out_type=jax.ShapeDtypeStruct((m, n_sel), F32),
        scratch_types=[
            pltpu.VMEM((chunk, n_sel), I32),
            pltpu.VMEM((chunk, dw), I32),
            pltpu.VMEM((chunk, n_sel), F32),
            pltpu.VMEM((SC_RING, SC_LANES, dw), I32),
            pltpu.VMEM((SC_LANES * SC_LANES,), F32),
            pltpu.SemaphoreType.DMA((SC_RING,)),
        ],
        compiler_params=pltpu.CompilerParams(needs_layout_passes=False),
        name="peer_expert_dots",
    )
    def k(h_hbm, idx_hbm, u_hbm, act_hbm, idx_v, h_v, act_v, rows_v, part_v, sems):
        base = _sc_worker_id() * per_w
        lane = lax.broadcasted_iota(I32, (SC_LANES,), 0)

        def compute(tt, g, slot):
            def grp_body(q, accs):
                off = pl.multiple_of(q * step, step)
                xs = [plsc.bitcast(h_v[tt, pl.ds(off + c * SC_LANES, SC_LANES)], BF16)
                      for c in range(SC_BF16_GROUP)]
                new = []
                for e in range(SC_LANES):
                    ps = [plsc.bitcast(
                        rows_v[slot, e, pl.ds(off + c * SC_LANES, SC_LANES)], BF16) * xs[c]
                        for c in range(SC_BF16_GROUP)]
                    while len(ps) > 1:
                        ps = [ps[i] + ps[i + 1] for i in range(0, len(ps), 2)]
                    lo, hi = plsc.unpack(ps[0], format=SC_FMT)
                    new.append(accs[e] + (lo + hi))
                return tuple(new)

            accs = lax.fori_loop(
                0, dw // step, grp_body,
                tuple(jnp.zeros((SC_LANES,), F32) for _ in range(SC_LANES)))
            for e in range(SC_LANES):
                part_v[pl.ds(e * SC_LANES, SC_LANES)] = accs[e]
            tot = jnp.zeros((SC_LANES,), F32)
            for l in range(SC_LANES):
                tot = tot + plsc.load_gather(part_v, [lane * SC_LANES + l])
            act_v[tt, pl.ds(g * SC_LANES, SC_LANES)] = tot

        def chunk_body(c, carry):
            t0 = base + c * chunk
            pltpu.sync_copy(idx_hbm.at[pl.ds(t0, chunk)], idx_v)
            pltpu.sync_copy(h_hbm.at[pl.ds(t0, chunk)], h_v)
            _sc_row_pipeline(idx_v, u_hbm, rows_v, sems, chunk * groups, groups, compute)
            pltpu.sync_copy(act_v, act_hbm.at[pl.ds(t0, chunk)])
            return carry

        lax.fori_loop(0, n_chunks, chunk_body, 0)

    return k(hp, idx, up)


def peer_expert_combine(w, idx, vp):
    m, n_sel = w.shape
    dw = vp.shape[1]
    d = 2 * dw
    per_w = m // SC_WORKERS
    n_chunks = per_w // SC_TOK_CHUNK
    n_vec = d // SC_LANES
    groups = n_sel // SC_LANES
    mesh = plsc.VectorSubcoreMesh(core_axis_name="c", subcore_axis_name="s")

    @functools.partial(
        pl.kernel, mesh=mesh,
        out_type=jax.ShapeDtypeStruct((m, d), F32),
        scratch_types=[
            pltpu.VMEM((SC_TOK_CHUNK, n_sel), I32),
            pltpu.VMEM((SC_TOK_CHUNK, n_sel), F32),
            pltpu.VMEM((SC_TOK_CHUNK, d), F32),
            pltpu.VMEM((SC_RING, SC_LANES, dw), I32),
            pltpu.SemaphoreType.DMA((SC_RING,)),
        ],
        compiler_params=pltpu.CompilerParams(needs_layout_passes=False),
        name="peer_expert_combine",
    )
    def k(w_hbm, idx_hbm, v_hbm, out_hbm, idx_v, w_v, out_v, rows_v, sems):
        base = _sc_worker_id() * per_w

        def compute(tt, g, slot):
            splat = []
            for e in range(SC_LANES):
                s = plsc.load_gather(w_v, [jnp.full((SC_LANES,), tt, I32),
                                           jnp.full((SC_LANES,), g * SC_LANES + e, I32)])
                splat.append(plsc.pack(s, s, format=SC_FMT))

            @plsc.parallel_loop(0, dw // SC_LANES)
            def _(j):
                off = pl.multiple_of(j * SC_LANES, SC_LANES)
                acc_lo = out_v[tt, pl.ds(off, SC_LANES)]
                acc_hi = out_v[tt, pl.ds(dw + off, SC_LANES)]
                for e0 in range(0, SC_LANES, SC_BF16_GROUP):
                    s = None
                    for e in range(e0, e0 + SC_BF16_GROUP):
                        p = plsc.bitcast(rows_v[slot, e, pl.ds(off, SC_LANES)], BF16) * splat[e]
                        s = p if s is None else s + p
                    lo, hi = plsc.unpack(s, format=SC_FMT)
                    acc_lo = acc_lo + lo
                    acc_hi = acc_hi + hi
                out_v[tt, pl.ds(off, SC_LANES)] = acc_lo
                out_v[tt, pl.ds(dw + off, SC_LANES)] = acc_hi

        def chunk_body(c, carry):
            t0 = base + c * SC_TOK_CHUNK
            pltpu.sync_copy(idx_hbm.at[pl.ds(t0, SC_TOK_CHUNK)], idx_v)
            pltpu.sync_copy(w_hbm.at[pl.ds(t0, SC_TOK_CHUNK)], w_v)

            def zero_body(z, carry2):
                tt, j = z // n_vec, z % n_vec
                out_v[tt, pl.ds(pl.multiple_of(j * SC_LANES, SC_LANES), SC_LANES)] = (
                    jnp.zeros((SC_LANES,), F32))
                return carry2

            lax.fori_loop(0, SC_TOK_CHUNK * n_vec, zero_body, 0)
            _sc_row_pipeline(idx_v, v_hbm, rows_v, sems, SC_TOK_CHUNK * groups, groups, compute)
            pltpu.sync_copy(out_v, out_hbm.at[pl.ds(t0, SC_TOK_CHUNK)])
            return carry

        lax.fori_loop(0, n_chunks, chunk_body, 0)

    return k(w, idx, vp)


def kernel(x, norm_mix_g, norm_ffn_g, final_norm_g, rel_bias, even_w_in, even_w_out,
           diff_lambda, diff_ln_g, odd_w_in, odd_b_f, odd_w_out, peer_wq, peer_subkeys,
           peer_u, peer_v):
    batch, seq, d = x.shape

    dil_tiles = jnp.stack(
        [dilated_bias_tile(rel_bias[:, :A_HEADS], w, dl) for w, dl in DILATED_BRANCHES])
    diff_tiles = diff_bias_tiles(rel_bias[:, A_HEADS:], seq)
    lam_init = 0.8 - 0.6 * math.exp(-0.3 * 0)
    even_in, even_out = even_w_in[0].astype(BF16), even_w_out[0].astype(BF16)
    w_in = odd_w_in[0]
    odd_in, odd_out = w_in[:, :3 * C_WIDTH].astype(BF16), odd_w_out[0].astype(BF16)
    w_gate = jnp.pad(w_in[:, 3 * C_WIDTH:], ((0, 0), (0, LANES - C_HEADS)))
    b_f = jnp.pad(odd_b_f[0], (0, LANES - C_HEADS)).reshape(1, LANES)
    peer = [dict(wq=peer_wq[l].astype(BF16),
                 sk=peer_subkeys[l].reshape(2 * PEER_HEADS, N_KEYS, PEER_KEY_HALF).astype(BF16))
            for l in range(2)]
    packed_tables = {}

    def expert_table(table, layer, anchor):
        if (id(table), layer) not in packed_tables:
            rows, _ = lax.optimization_barrier((table[layer], anchor))
            packed_tables[(id(table), layer)] = pack_bf16_pairs(rows)
        return packed_tables[(id(table), layer)]

    assert batch % BATCH_GROUPS == 0
    bg = batch // BATCH_GROUPS
    m = bg * seq
    groups = [dict(x=x[g * bg:(g + 1) * bg].reshape(m, d)) for g in range(BATCH_GROUPS)]

    def peer_select(st, layer, after=None):
        q, hp = norm_matmul(st["x"], norm_ffn_g[layer], peer[layer]["wq"], want_h="packed")
        idx, gate = peer_topk(q, peer[layer]["sk"])
        if after is not None and after["st"] is not st:
            after["st"]["peer"], idx =lax.optimization_barrier((after["st"]["peer"], idx))
        return dict(st=st, layer=layer, idx=idx, gate=gate,
                    act=peer_expert_dots(hp, idx, expert_table(peer_u, layer, st["x"])))

    def peer_combine(rec):
        rec["w"] = gelu_gate(rec["act"], rec["gate"])
        rec["st"]["peer"] = peer_expert_combine(
            rec["w"], rec["idx"], expert_table(peer_v, rec["layer"], rec["act"]))

    def even_layer(st, tie):
        (p,) = norm_matmul(st["x"], norm_mix_g[0], even_in)
        p3 = p.reshape(bg, seq, -1)
        oa = dilated_attention(p3, dil_tiles, bg, seq)
        ob = tie(diff_attention(p3, diff_tiles, diff_lambda[0], diff_ln_g[0], lam_init, bg, seq))
        st["x"] = concat_out_proj(oa.reshape(m, A_WIDTH), ob.reshape(m, B_V_WIDTH), even_out,
                                  st["x"])

    def odd_layer(st, tie):
        p, st["x"], h = norm_matmul(st["x"], norm_mix_g[1], odd_in, res=st["peer"], want_h="f32")
        fg = gate_matmul(h, w_gate)
        c = logsig_cumsum(fg.reshape(bg, seq, LANES), b_f, bg, seq)
        ct = c[:, :, :C_HEADS].transpose(0, 2, 1)
        o = tie(fox_attention(p.reshape(bg, seq, -1), ct[:, :, :, None], ct[:, :, None, :],
                              bg, seq))
        st["x"] = matmul_residual(o.reshape(m, C_WIDTH), odd_out, st["x"])

    stages = [(even_layer, st, 0) for st in groups] + [(odd_layer, st, 1) for st in groups]
    recs = []
    for k, (mixer, st, layer) in enumerate(stages):
        def tie(t, prev=recs[k - 1] if k >= 1 else None):
            if prev is None:
                return t
            prev["act"], t = lax.optimization_barrier((prev["act"], t))
            peer_combine(prev)
            prev["w"], t = lax.optimization_barrier((prev["w"], t))
            return t

        mixer(st, tie)
        recs.append(peer_select(st, layer, after=recs[max(k - 2, 0)] if k >= 1 else None))
    peer_combine(recs[-1])
    outs = [add_norm(st["x"], st["peer"], final_norm_g).reshape(bg, seq, d) for st in groups]
    return jnp.concatenate(outs, axis=0)


def _gate_matmul_body(h_ref, w_ref, o_ref):
    o_ref[...] = jnp.dot(h_ref[...], w_ref[...], preferred_element_type=F32,
                         precision=lax.Precision.HIGHEST)


def gate_matmul(h, w, tm=512):
    m, d = h.shape
    n = w.shape[1]
    return pl.pallas_call(
        _gate_matmul_body,
        grid=(m // tm,),
        in_specs=[pl.BlockSpec((tm, d), lambda i: (i, 0)),
                  pl.BlockSpec((d, n), lambda i: (0, 0))],
        out_specs=pl.BlockSpec((tm, n), lambda i: (i, 0)),
        out_shape=jax.ShapeDtypeStruct((m, n), F32),
        compiler_params=pltpu.CompilerParams(dimension_semantics=("parallel",)),
        name="gate_matmul",
    )(h, w)
```

```python
import functools
import math

import numpy as np
import jax
import jax.numpy as jnp
from jax import lax
from jax.experimental import pallas as pl
from jax.experimental.pallas import tpu as pltpu
from jax.experimental.pallas import tpu_sc as plsc

F32 = jnp.float32
BF16 = jnp.bfloat16
I32 = jnp.int32

HEAD_DIM = 64
A_HEADS = 8
DILATED_BRANCHES = ((128, 1), (512, 4), (2048, 16))
DIFF_HALF = 64
DIFF_VDIM = 128
B_HEADS = 4
C_HEADS = 16
N_BUCKETS = 32
MAX_DISTANCE = 2048
PEER_HEADS = 8
N_KEYS = 128
PEER_TOPK = 16
PEER_KEY_HALF = 128
RMS_EPS = 1e-6
NEG_INF = -1e30
A_WIDTH = A_HEADS * HEAD_DIM
B_QK_WIDTH = B_HEADS * 2 * DIFF_HALF
B_V_WIDTH = B_HEADS * DIFF_VDIM
C_WIDTH = C_HEADS * HEAD_DIM
QK_SCALE = 0.125
LANES = 128
DIL_BLOCK = 128
ATT_BLOCK = 512
BATCH_GROUPS = 8

SC_CORES = 2
SC_SUBCORES = 16
SC_LANES = 16
SC_WORKERS = SC_CORES * SC_SUBCORES


def _t5_bucket_table(n):
    max_exact = N_BUCKETS // 2
    d = np.arange(n)
    df = np.maximum(d, 1).astype(np.float32)
    large = max_exact + (
        np.log(df / np.float32(max_exact)) / np.float32(math.log(MAX_DISTANCE / max_exact))
        * np.float32(N_BUCKETS - max_exact)).astype(np.int32)
    large = np.minimum(large, N_BUCKETS - 1)
    return np.where(d < max_exact, d, large).astype(np.int32)


def _norm_matmul_body(*refs, has_res, want_h):
    it = iter(refs)
    x_ref = next(it)
    r_ref = next(it) if has_res else None
    g_ref = next(it)
    w_ref = next(it)
    gw_ref = next(it) if want_h == "gate" else None
    o_ref = next(it)
    xs_ref = next(it) if has_res else None
    hout_ref = next(it) if want_h else None
    h_scr = next(it)

    @pl.when(pl.program_id(1) == 0)
    def _():
        x = x_ref[...]
        if has_res:
            x = x + r_ref[...]
            xs_ref[...] = x
        ms = jnp.mean(x * x, axis=-1, keepdims=True)
        h = x * lax.rsqrt(ms + RMS_EPS) * g_ref[...]
        if want_h == "gate":
            hout_ref[...] = jnp.dot(h, gw_ref[...], preferred_element_type=F32,
                                    precision=lax.Precision.HIGHEST)
        elif want_h == "packed":
            hout_ref[...] = pack_bf16_pairs(h)
        h_scr[...] = h.astype(BF16)

    o_ref[...] = jnp.dot(h_scr[...], w_ref[...],
                         preferred_element_type=F32).astype(o_ref.dtype)


def norm_matmul(x, g, w, *, res=None, want_h=None, gate_w=None, out_dtype=BF16,
                tm=1024, tn=1024):
    m, d = x.shape
    n = w.shape[1]
    tn = min(tn, n)
    row = pl.BlockSpec((tm, d), lambda i, j: (i, 0))
    in_specs = [row] + ([row] if res is not None else []) + [
        pl.BlockSpec((1, d), lambda i, j: (0, 0)),
        pl.BlockSpec((d, tn), lambda i, j: (0, j))]
    out_specs = [pl.BlockSpec((tm, tn), lambda i, j: (i, j))]
    out_shape = [jax.ShapeDtypeStruct((m, n), out_dtype)]
    if res is not None:
        out_specs.append(row)
        out_shape.append(jax.ShapeDtypeStruct((m, d), F32))
    if want_h == "gate":
        in_specs.append(pl.BlockSpec(gate_w.shape, lambda i, j: (0, 0)))
        out_specs.append(pl.BlockSpec((tm, gate_w.shape[1]), lambda i, j: (i, 0)))
        out_shape.append(jax.ShapeDtypeStruct((m, gate_w.shape[1]), F32))
    elif want_h == "packed":
        out_specs.append(pl.BlockSpec((tm, d // 2), lambda i, j: (i, 0)))
        out_shape.append(jax.ShapeDtypeStruct((m, d // 2), I32))
    args = [x] + ([res] if res is not None else []) + [g.reshape(1, d), w] + (
        [gate_w] if want_h == "gate" else [])
    return pl.pallas_call(
        functools.partial(_norm_matmul_body, has_res=res is not None, want_h=want_h),
        grid=(m // tm, n // tn),
        in_specs=in_specs,
        out_specs=out_specs,
        out_shape=out_shape,
        scratch_shapes=[pltpu.VMEM((tm, d), BF16)],
        compiler_params=pltpu.CompilerParams(
            dimension_semantics=("parallel", "arbitrary")),
        name="norm_matmul",
    )(*args)


def dilated_bias_tile(rel_bias_a, window, dil):
    n = window // dil
    assert n == DIL_BLOCK
    bucket = _t5_bucket_table(window + 1)
    period = 4 * n
    u = np.arange(period)
    valid = u <= n
    w = jnp.where(jnp.asarray(valid)[None],
                  rel_bias_a.T[:, bucket[np.where(valid, n - u, 0) * dil]], NEG_INF).astype(F32)
    rep = jnp.broadcast_to(w[:, None, :], (A_HEADS, n, period))
    flat = rep.reshape(A_HEADS, n * period)[:, :n * (period - 1)]
    return flat.reshape(A_HEADS, n, period - 1)[:, :, :2 * n]


def _dilated_fused_body(q_ref, k_ref, v_ref, b_ref, o_ref, qf, kf, vf, m_scr, acc_scr, *, seq):
    qf[...] = q_ref[...].astype(F32) * QK_SCALE
    kf[...] = k_ref[...].astype(F32)
    vf[...] = v_ref[...].astype(F32)
    m_scr[...] = jnp.full(m_scr.shape, NEG_INF, F32)
    acc_scr[...] = jnp.zeros(acc_scr.shape, F32)
    n = DIL_BLOCK
    col = lax.broadcasted_iota(I32, (n, 2 * n), 1)
    lane = lax.broadcasted_iota(I32, (1, LANES), 1)
    own = [(lane < HEAD_DIM) == (a == 0) for a in range(2)]
    for bi, (_, dil) in enumerate(DILATED_BRANCHES):
        span = dil * n

        def class_body(r, carry, bi=bi, dil=dil, span=span):
            def block_body(i, carry2):
                start = r + span * i
                rows = pl.ds(start, n, stride=dil)
                prev = pl.ds(jnp.maximum(start - span, r), n, stride=dil)
                q = qf[rows, :].astype(BF16)
                k = jnp.concatenate([kf[prev, :], kf[rows, :]], axis=0).astype(BF16)
                v = jnp.concatenate([vf[prev, :], vf[rows, :]], axis=0).astype(BF16)
                has_prev = jnp.logical_or(col >= n, i > 0)
                for a in range(2):
                    s = lax.dot_general(jnp.where(own[a], q, jnp.zeros_like(q)), k,
                                        (((1,), (1,)), ((), ())), preferred_element_type=F32)
                    s = jnp.where(has_prev, s + b_ref[bi, a], NEG_INF)
                    m_prev = m_scr[a, rows, :]
                    m_new = jnp.maximum(m_prev, jnp.max(s, axis=-1, keepdims=True))
                    alpha = jnp.exp(m_prev - m_new)
                    pb = jnp.exp(s - jnp.concatenate([m_new, m_new], axis=1)).astype(BF16)
                    acc_scr[a, rows, :] = alpha * acc_scr[a, rows, :] + jnp.dot(
                        pb, jnp.where(own[a], v, jnp.ones_like(v)), preferred_element_type=F32)
                    m_scr[a, rows, :] = m_new
                return carry2

            lax.fori_loop(0, seq // span, block_body, 0)
            return carry

        lax.fori_loop(0, dil, class_body, 0)
    r = [acc_scr[a] / pltpu.roll(acc_scr[a], HEAD_DIM, 1) for a in range(2)]
    o_ref[...] = jnp.where(own[0], r[0], r[1]).astype(o_ref.dtype)


def dilated_attention(p, bias_tiles, batch, seq):
    nk = A_WIDTH // LANES
    blk = (None, seq, LANES)
    return pl.pallas_call(
        functools.partial(_dilated_fused_body, seq=seq),
        grid=(batch, A_HEADS // 2),
        in_specs=[pl.BlockSpec(blk, lambda b, h: (b, 0, h)),
                  pl.BlockSpec(blk, lambda b, h: (b, 0, nk + h)),
                  pl.BlockSpec(blk, lambda b, h: (b, 0, 2 * nk + h)),
                  pl.BlockSpec((len(DILATED_BRANCHES), 2, DIL_BLOCK, 2 * DIL_BLOCK),
                               lambda b, h: (0, h, 0, 0))],
        out_specs=pl.BlockSpec(blk, lambda b, h: (b, 0, h)),
        out_shape=jax.ShapeDtypeStruct((batch, seq, A_WIDTH), BF16),
        scratch_shapes=[pltpu.VMEM((seq, LANES), F32)] * 3 + [
            pltpu.VMEM((2, seq, LANES), F32), pltpu.VMEM((2, seq, LANES), F32)],
        compiler_params=pltpu.CompilerParams(dimension_semantics=("parallel", "parallel")),
        name="dilated_attention",
    )(p, p, p, bias_tiles)


def _diff_body(q_ref, k_ref, v_ref, b_ref, lam_ref, g_ref, o_ref,
               m_scr, l_scr, acc_scr, *, n_tiles, lam_init):
    t = ATT_BLOCK
    qi = pl.program_id(2)
    q = q_ref[...] * QK_SCALE
    lane = lax.broadcasted_iota(I32, (1, LANES), 1)
    qa = [jnp.where((lane < DIFF_HALF) == (a == 0), q, jnp.zeros_like(q)) for a in range(2)]
    ones = jnp.ones((t, LANES), BF16)
    for a in range(2):
        m_scr[a] = jnp.full((t, LANES), NEG_INF, F32)
        l_scr[a] = jnp.zeros((t, LANES), F32)
        acc_scr[a] = jnp.zeros((t, DIFF_VDIM), F32)

    def step(j, masked):
        off = pl.multiple_of(j * t, t)
        ks = k_ref[pl.ds(off, t), :]
        vs = v_ref[pl.ds(off, t), :]
        bias = b_ref[jnp.minimum(qi - j, n_tiles - 1)]
        if masked:
            row = lax.broadcasted_iota(I32, (t, t), 0)
            col = lax.broadcasted_iota(I32, (t, t), 1)
            causal = row >= col
        for a in range(2):
            s = lax.dot_general(qa[a], ks, (((1,), (1,)), ((), ())),
                                preferred_element_type=F32) + bias
            if masked:
                s = jnp.where(causal, s, NEG_INF)
            m_prev = m_scr[a]
            m_new = jnp.maximum(m_prev, jnp.max(s, axis=-1, keepdims=True))
            alpha = jnp.exp(m_prev - m_new)
            pb = jnp.exp(s - jnp.concatenate([m_new] * (t // LANES), axis=1)).astype(BF16)
            l_scr[a] = alpha * l_scr[a] + jnp.dot(pb, ones, preferred_element_type=F32)
            acc_scr[a] = alpha * acc_scr[a] + jnp.dot(pb, vs, preferred_element_type=F32)
            m_scr[a] = m_new

    def loop_body(j, carry):
        step(j, False)
        return carry

    lax.fori_loop(0, qi, loop_body, 0)
    step(qi, True)

    lp = lam_ref[...]
    lam = (jnp.exp(jnp.sum(lp[0:1] * lp[1:2])) - jnp.exp(jnp.sum(lp[2:3] * lp[3:4]))
           + lam_init)
    o = acc_scr[0] / l_scr[0] - lam * (acc_scr[1] / l_scr[1])
    ms = jnp.mean(o * o, axis=-1, keepdims=True)
    y = o * lax.rsqrt(ms + RMS_EPS) * g_ref[...]
    o_ref[...] = (y * (1.0 - lam_init)).astype(o_ref.dtype)


def diff_attention(p, bias_tiles, lam_params, ln_g, lam_init, batch, seq):
    t = ATT_BLOCK
    n_tiles = bias_tiles.shape[1]
    cq = 3 * A_WIDTH // LANES
    ck = cq + B_QK_WIDTH // LANES
    cv = ck + B_QK_WIDTH // LANES
    return pl.pallas_call(
        functools.partial(_diff_body, n_tiles=n_tiles, lam_init=lam_init),
        grid=(batch, B_HEADS, seq // t),
        in_specs=[
            pl.BlockSpec((None, t, LANES), lambda b, h, i: (b, i, cq + h)),
            pl.BlockSpec((None, seq, LANES), lambda b, h, i: (b, 0, ck + h)),
            pl.BlockSpec((None, seq, LANES), lambda b, h, i: (b, 0, cv + h)),
            pl.BlockSpec((None, n_tiles, t, t), lambda b, h, i: (h, 0, 0, 0)),
            pl.BlockSpec((4, DIFF_HALF), lambda b, h, i: (0, 0)),
            pl.BlockSpec((1, DIFF_VDIM), lambda b, h, i: (0, 0)),
        ],
        out_specs=pl.BlockSpec((None, t, LANES), lambda b, h, i: (b, i, h)),
        out_shape=jax.ShapeDtypeStruct((batch, seq, B_V_WIDTH), BF16),
        scratch_shapes=[pltpu.VMEM((2, t, LANES), F32), pltpu.VMEM((2, t, LANES), F32),
                        pltpu.VMEM((2, t, DIFF_VDIM), F32)],
        compiler_params=pltpu.CompilerParams(
            dimension_semantics=("parallel", "parallel", "arbitrary")),
        name="diff_attention",
    )(p, p, p, bias_tiles, lam_params, ln_g.reshape(1, DIFF_VDIM))


def diff_bias_tiles(rel_bias_b, seq):
    t = ATT_BLOCK
    bucket = _t5_bucket_table(max(seq, 2 * MAX_DISTANCE) + 2 * t)
    sat = bucket[-1]
    d_sat = int(np.max(np.nonzero(bucket != sat)[0])) + 1
    n_full = (d_sat + t - 1 + t - 1) // t
    n_tiles = n_full + 1
    assert n_full * t - (t - 1) >= d_sat
    n = np.arange(2 * t)[None, :]
    base = np.arange(n_tiles)[:, None] * t
    dist = np.clip(np.where(n < t, base - n, base + 2 * t - n), 0, None)
    w = rel_bias_b.T[:, bucket[dist]].astype(F32)
    rep = jnp.broadcast_to(w[:, :, None, :], (B_HEADS, n_tiles, t, 2 * t))
    flat = rep.reshape(B_HEADS, n_tiles, 2 * t * t)[:, :, :t * (2 * t - 1)]
    return flat.reshape(B_HEADS, n_tiles, t, 2 * t - 1)[:, :, :, :t]


def _logsig_cumsum_body(f_ref, b_ref, c_ref, carry_scr):
    t = f_ref.shape[0]

    @pl.when(pl.program_id(1) == 0)
    def _():
        carry_scr[...] = jnp.zeros_like(carry_scr)

    x = f_ref[...] + b_ref[...]
    ls = jnp.minimum(x, 0.0) - jnp.log1p(jnp.exp(-jnp.abs(x)))
    row = lax.broadcasted_iota(I32, (t, t), 0)
    col = lax.broadcasted_iota(I32, (t, t), 1)
    tri = (row >= col).astype(F32)
    c = jnp.dot(tri, ls, preferred_element_type=F32,
                precision=lax.Precision.HIGHEST) + carry_scr[...]
    c_ref[...] = c
    carry_scr[...] = c[t - 1:t, :]


def logsig_cumsum(fg, b_f, batch, seq, t=512):
    return pl.pallas_call(
        _logsig_cumsum_body,
        grid=(batch, seq // t),
        in_specs=[pl.BlockSpec((None, t, LANES), lambda b, i: (b, i, 0)),
                  pl.BlockSpec((1, LANES), lambda b, i: (0, 0))],
        out_specs=pl.BlockSpec((None, t, LANES), lambda b, i: (b, i, 0)),
        out_shape=jax.ShapeDtypeStruct((batch, seq, LANES), F32),
        scratch_shapes=[pltpu.VMEM((1, LANES), F32)],
        compiler_params=pltpu.CompilerParams(
            dimension_semantics=("parallel", "arbitrary")),
        name="logsig_cumsum",
    )(fg, b_f)


def _fox_body(q_ref, k_ref, v_ref, cq_ref, ck_ref, o_ref, m_scr, acc_scr):
    t = ATT_BLOCK
    qi = pl.program_id(2)
    q = q_ref[...] * QK_SCALE
    lane = lax.broadcasted_iota(I32, (1, LANES), 1)
    own = [(lane < HEAD_DIM) == (a == 0) for a in range(2)]
    qa = [jnp.where(own[a], q, jnp.zeros_like(q)) for a in range(2)]
    cqb = [jnp.broadcast_to(cq_ref[a], (t, LANES)) for a in range(2)]
    for a in range(2):
        m_scr[a] = jnp.full((t, LANES), NEG_INF, F32)
        acc_scr[a] = jnp.zeros((t, LANES), F32)

    def step(j, masked):
        off = pl.multiple_of(j * t, t)
        ks = k_ref[pl.ds(off, t), :]
        vs = v_ref[pl.ds(off, t), :]
        if masked:
            row = lax.broadcasted_iota(I32, (t, t), 0)
            col = lax.broadcasted_iota(I32, (t, t), 1)
            causal = row >= col
        for a in range(2):
            s = lax.dot_general(qa[a], ks, (((1,), (1,)), ((), ())),
                                preferred_element_type=F32)
            s = s - ck_ref[a, :, pl.ds(off, t)]
            if masked:
                s = jnp.where(causal, s, NEG_INF)
            m_prev = m_scr[a]
            m_new = jnp.maximum(m_prev, jnp.max(s, axis=-1, keepdims=True) + cqb[a])
            alpha = jnp.exp(m_prev - m_new)
            shift = m_new - cqb[a]
            pb = jnp.exp(s - jnp.concatenate([shift] * (t // LANES), axis=1)).astype(BF16)
            v_aug = jnp.where(own[a], vs, jnp.ones_like(vs))
            acc_scr[a] = alpha * acc_scr[a] + jnp.dot(pb, v_aug, preferred_element_type=F32)
            m_scr[a] = m_new

    def loop_body(j, carry):
        step(j, False)
        return carry

    lax.fori_loop(0, qi, loop_body, 0)
    step(qi, True)
    r = [acc_scr[a] / pltpu.roll(acc_scr[a], HEAD_DIM, 1) for a in range(2)]
    o_ref[...] = jnp.where(own[0], r[0], r[1]).astype(o_ref.dtype)


def fox_attention(p, cq, ck, batch, seq):
    t = ATT_BLOCK
    nk = C_WIDTH // LANES
    return pl.pallas_call(
        _fox_body,
        grid=(batch, C_HEADS // 2, seq // t),
        in_specs=[
            pl.BlockSpec((None, t, LANES), lambda b, h, i: (b, i, h)),
            pl.BlockSpec((None, seq, LANES), lambda b, h, i: (b, 0, nk + h)),
            pl.BlockSpec((None, seq, LANES), lambda b, h, i: (b, 0, 2 * nk + h)),
            pl.BlockSpec((None, 2, t, 1), lambda b, h, i: (b, h, i, 0)),
            pl.BlockSpec((None, 2, 1, seq), lambda b, h, i: (b, h, 0, 0)),
        ],
        out_specs=pl.BlockSpec((None, t, LANES), lambda b, h, i: (b, i, h)),
        out_shape=jax.ShapeDtypeStruct((batch, seq, C_WIDTH), BF16),
        scratch_shapes=[pltpu.VMEM((2, t, LANES), F32), pltpu.VMEM((2, t, LANES), F32)],
        compiler_params=pltpu.CompilerParams(
            dimension_semantics=("parallel", "parallel", "arbitrary")),
        name="fox_attention",
    )(p, p, p, cq, ck)


def _concat_out_body(oa_ref, ob_ref, w_ref, x_ref, out_ref):
    out_ref[...] = (x_ref[...]
                    + jnp.dot(oa_ref[...], w_ref[:A_WIDTH, :], preferred_element_type=F32)
                    + jnp.dot(ob_ref[...], w_ref[A_WIDTH:, :], preferred_element_type=F32))


def concat_out_proj(oa, ob, w, x, tm=1024, tn=1024):
    m, d = x.shape
    return pl.pallas_call(
        _concat_out_body,
        grid=(m // tm, d // tn),
        in_specs=[pl.BlockSpec((tm, A_WIDTH), lambda i, j: (i, 0)),
                  pl.BlockSpec((tm, B_V_WIDTH), lambda i, j: (i, 0)),
                  pl.BlockSpec((A_WIDTH + B_V_WIDTH, tn), lambda i, j: (0, j)),
                  pl.BlockSpec((tm, tn), lambda i, j: (i, j))],
        out_specs=pl.BlockSpec((tm, tn), lambda i, j: (i, j)),
        out_shape=jax.ShapeDtypeStruct((m, d), F32),
        compiler_params=pltpu.CompilerParams(dimension_semantics=("parallel", "parallel")),
        name="concat_out_proj",
    )(oa, ob, w, x)


def _matmul_res_body(a_ref, w_ref, x_ref, o_ref):
    o_ref[...] = x_ref[...] + jnp.dot(a_ref[...], w_ref[...], preferred_element_type=F32)


def matmul_residual(a, w, x, tm=1024, tn=1024):
    m, k = a.shape
    n = w.shape[1]
    return pl.pallas_call(
        _matmul_res_body,
        grid=(m // tm, n // tn),
        in_specs=[pl.BlockSpec((tm, k), lambda i, j: (i, 0)),
                  pl.BlockSpec((k, tn), lambda i, j: (0, j)),
                  pl.BlockSpec((tm, tn), lambda i, j: (i, j))],
        out_specs=pl.BlockSpec((tm, tn), lambda i, j: (i, j)),
        out_shape=jax.ShapeDtypeStruct((m, n), F32),
        compiler_params=pltpu.CompilerParams(
            dimension_semantics=("parallel", "parallel")),
        name="matmul_residual",
    )(a, w, x)


def _add_norm_body(x_ref, r_ref, g_ref, o_ref):
    x = x_ref[...] + r_ref[...]
    ms = jnp.mean(x * x, axis=-1, keepdims=True)
    o_ref[...] = x * lax.rsqrt(ms + RMS_EPS) * g_ref[...]


def add_norm(x, r, g, tm=512):
    m, d = x.shape
    row = pl.BlockSpec((tm, d), lambda i: (i, 0))
    return pl.pallas_call(
        _add_norm_body,
        grid=(m // tm,),
        in_specs=[row, row, pl.BlockSpec((1, d), lambda i: (0, 0))],
        out_specs=row,
        out_shape=jax.ShapeDtypeStruct((m, d), F32),
        compiler_params=pltpu.CompilerParams(dimension_semantics=("parallel",)),
        name="add_norm",
    )(x, r, g.reshape(1, d))


def _peer_candidates():
    groups = [("a", 0, 0), ("a", 0, 8), ("a", 1, 0), ("b", 0, 8), ("a", 2, 0), ("a", 3, 0),
              ("b", 0, 0), ("b", 1, 0), ("b", 2, 0)]
    cid = np.zeros((8 * len(groups), 1), np.int32)
    seen = set()
    for g, (kind, fixed, start) in enumerate(groups):
        for r in range(8):
            a, b = (fixed, start + r) if kind == "a" else (start + r, fixed)
            row = 8 * g + r
            if (a + 1) * (b + 1) <= PEER_TOPK and (a, b) not in seen:
                seen.add((a, b))
                cid[row, 0] = a * PEER_TOPK + b
            else:
                cid[row, 0] = PEER_TOPK * PEER_TOPK + row
    assert len(seen) == sum((a + 1) * (b + 1) <= PEER_TOPK
                            for a in range(PEER_TOPK) for b in range(PEER_TOPK))
    return groups, cid


def _peer_topk_body(q_ref, sk_ref, cid_ref, idx_ref, gate_ref, ts_scr, ti_scr,
                    bs_scr, be_scr, cv_scr, ce_scr, sc_scr, *, groups):
    tm = q_ref.shape[0]
    neg_inf = jnp.float32(-jnp.inf)
    key_id = lax.broadcasted_iota(I32, (1, N_KEYS, tm), 1)
    batch = sc_scr.shape[0]

    def group_body(g, carry):
        for j in range(batch):
            pr = g * batch + j
            off = pl.multiple_of(pr * PEER_KEY_HALF, PEER_KEY_HALF)
            sc_scr[j] = lax.dot_general(sk_ref[pr], q_ref[:, pl.ds(off, PEER_KEY_HALF)],
                                        (((1,), (1,)), ((), ())), preferred_element_type=F32)
        rows = pl.ds(g * batch, batch)

        def k_body(k, c):
            vals = sc_scr[...]
            m = jnp.max(vals, axis=1, keepdims=True)
            sel = jnp.min(jnp.where(vals == m, key_id, N_KEYS), axis=1, keepdims=True)
            ts_scr[rows, pl.ds(k, 1), :] = m
            ti_scr[rows, pl.ds(k, 1), :] = sel
            sc_scr[...] = jnp.where(key_id == sel, neg_inf, vals)
            return c

        lax.fori_loop(0, PEER_TOPK, k_body, 0)
        return carry

    lax.fori_loop(0, 2 * PEER_HEADS // batch, group_body, 0)

    cand_id = cid_ref[...]
    pad = jnp.where(cand_id < PEER_TOPK * PEER_TOPK, 0.0, neg_inf)

    def pick(x1, x2):
        return jnp.concatenate(
            [x1[f:f + 1] + x2[s:s + 8] if kind == "a" else x1[s:s + 8] + x2[f:f + 1]
             for kind, f, s in groups], axis=0)

    def head_body(h, carry):
        cv_scr[h] = pick(ts_scr[2 * h], ts_scr[2 * h + 1]) + pad
        ce_scr[h] = pick(ti_scr[2 * h] * N_KEYS, ti_scr[2 * h + 1])
        return carry

    lax.fori_loop(0, PEER_HEADS, head_body, 0)

    cid3 = cand_id[None]

    def k_body(k, carry):
        vals = cv_scr[...]
        m = jnp.max(vals, axis=1, keepdims=True)
        sel = jnp.min(jnp.where(vals == m, cid3, PEER_TOPK * PEER_TOPK), axis=1, keepdims=True)
        hit = cid3 == sel
        bs_scr[:, pl.ds(k, 1), :] = m
        be_scr[:, pl.ds(k, 1), :] = jnp.sum(jnp.where(hit, ce_scr[...], 0), axis=1, keepdims=True)
        cv_scr[...] = jnp.where(hit, neg_inf, vals)
        return carry

    lax.fori_loop(0, PEER_TOPK, k_body, 0)
    bs = bs_scr[...]
    e = jnp.exp(bs - jnp.max(bs, axis=1, keepdims=True))
    gate = e / jnp.sum(e, axis=1, keepdims=True)
    n_sel = PEER_HEADS * PEER_TOPK
    gate_ref[...] = gate.reshape(n_sel, tm).T
    idx_ref[...] = be_scr[...].reshape(n_sel, tm).T


def peer_topk(q, subkeys, tm=256):
    m = q.shape[0]
    n_sel = PEER_HEADS * PEER_TOPK
    groups, cid = _peer_candidates()
    out_spec = pl.BlockSpec((tm, n_sel), lambda i: (i, 0))
    return pl.pallas_call(
        functools.partial(_peer_topk_body, groups=groups),
        grid=(m // tm,),
        in_specs=[pl.BlockSpec((tm, q.shape[1]), lambda i: (i, 0)),
                  pl.BlockSpec(subkeys.shape, lambda i: (0, 0, 0)),
                  pl.BlockSpec(cid.shape, lambda i: (0, 0))],
        out_specs=[out_spec, out_spec],
        out_shape=[jax.ShapeDtypeStruct((m, n_sel), I32),
                   jax.ShapeDtypeStruct((m, n_sel), F32)],
        scratch_shapes=[pltpu.VMEM((2 * PEER_HEADS, PEER_TOPK, tm), F32),
                        pltpu.VMEM((2 * PEER_HEADS, PEER_TOPK, tm), I32),
                        pltpu.VMEM((PEER_HEADS, PEER_TOPK, tm), F32),
                        pltpu.VMEM((PEER_HEADS, PEER_TOPK, tm), I32),
                        pltpu.VMEM((PEER_HEADS, cid.shape[0], tm), F32),
                        pltpu.VMEM((PEER_HEADS, cid.shape[0], tm), I32),
                        pltpu.VMEM((PEER_HEADS, N_KEYS, tm), F32)],
        compiler_params=pltpu.CompilerParams(dimension_semantics=("parallel",)),
        name="peer_topk",
    )(q, subkeys, jnp.asarray(cid))


def _gelu_gate_body(a_ref, g_ref, o_ref):
    a = a_ref[...]
    o_ref[...] = g_ref[...] * (0.5 * a * (1.0 + lax.erf(a * (2.0 ** -0.5))))


def gelu_gate(act, gate, tm=2048):
    m, n = act.shape
    spec = pl.BlockSpec((tm, n), lambda i: (i, 0))
    return pl.pallas_call(
        _gelu_gate_body,
        grid=(m // tm,),
        in_specs=[spec, spec],
        out_specs=spec,
        out_shape=jax.ShapeDtypeStruct((m, n), F32),
        compiler_params=pltpu.CompilerParams(dimension_semantics=("parallel",)),
        name="gelu_gate",
    )(act, gate)


SC_TOK_CHUNK = 32
SC_DOTS_CHUNK = 64
SC_RING = 8
SC_BF16_GROUP = 4
SC_FMT = plsc.PackFormat.INTERLEAVED


def _sc_worker_id():
    return lax.axis_index("s") * SC_CORES + lax.axis_index("c")


def pack_bf16_pairs(t):
    half = t.shape[-1] // 2
    bits = lax.bitcast_convert_type(t.astype(BF16).astype(F32), I32)
    return (bits[..., half:] & jnp.int32(-65536)) | lax.shift_right_logical(
        bits[..., :half], jnp.int32(16))


def _sc_row_pipeline(idx_v, table_hbm, rows_v, sems, n_items, groups, compute):
    def gather(item):
        tt, g = item // groups, item % groups
        ids = idx_v[tt, pl.ds(g * SC_LANES, SC_LANES)]
        slot = item % SC_RING
        return pltpu.make_async_copy(table_hbm.at[ids], rows_v.at[slot], sems.at[slot])

    for s in range(SC_RING - 1):
        gather(s).start()

    def item_body(item, carry):
        nxt = item + SC_RING - 1

        @pl.when(nxt < n_items)
        def _():
            gather(nxt).start()

        gather(item).wait()
        compute(item // groups, item % groups, item % SC_RING)
        return carry

    lax.fori_loop(0, n_items, item_body, 0)


def peer_expert_dots(hp, idx, up):
    m, dw = hp.shape
    n_sel = idx.shape[1]
    per_w = m // SC_WORKERS
    chunk = SC_DOTS_CHUNK
    n_chunks = per_w // chunk
    groups = n_sel // SC_LANES
    step = SC_BF16_GROUP * SC_LANES
    mesh = plsc.VectorSubcoreMesh(core_axis_name="c", subcore_axis_name="s")

    @functools.partial(
        pl.kernel, mesh=mesh,
        out_type=jax.ShapeDtypeStruct((m, n_sel), F32),
        scratch_types=[
            pltpu.VMEM((chunk, n_sel), I32),
            pltpu.VMEM((chunk, dw), I32),
            pltpu.VMEM((chunk, n_sel), F32),
            pltpu.VMEM((SC_RING, SC_LANES, dw), I32),
            pltpu.VMEM((SC_LANES * SC_LANES,), F32),
            pltpu.SemaphoreType.DMA((SC_RING,)),
        ],
        compiler_params=pltpu.CompilerParams(needs_layout_passes=False),
        name="peer_expert_dots",
    )
    def k(h_hbm, idx_hbm, u_hbm, act_hbm, idx_v, h_v, act_v, rows_v, part_v, sems):
        base = _sc_worker_id() * per_w
        lane = lax.broadcasted_iota(I32, (SC_LANES,), 0)

        def compute(tt, g, slot):
            def grp_body(q, accs):
                off = pl.multiple_of(q * step, step)
                xs = [plsc.bitcast(h_v[tt, pl.ds(off + c * SC_LANES, SC_LANES)], BF16)
                      for c in range(SC_BF16_GROUP)]
                new = []
                for e in range(SC_LANES):
                    ps = [plsc.bitcast(
                        rows_v[slot, e, pl.ds(off + c * SC_LANES, SC_LANES)], BF16) * xs[c]
                        for c in range(SC_BF16_GROUP)]
                    while len(ps) > 1:
                        ps = [ps[i] + ps[i + 1] for i in range(0, len(ps), 2)]
                    lo, hi = plsc.unpack(ps[0], format=SC_FMT)
                    new.append(accs[e] + (lo + hi))
                return tuple(new)

            accs = lax.fori_loop(
                0, dw // step, grp_body,
                tuple(jnp.zeros((SC_LANES,), F32) for _ in range(SC_LANES)))
            for e in range(SC_LANES):
                part_v[pl.ds(e * SC_LANES, SC_LANES)] = accs[e]
            tot = jnp.zeros((SC_LANES,), F32)
            for l in range(SC_LANES):
                tot = tot + plsc.load_gather(part_v, [lane * SC_LANES + l])
            act_v[tt, pl.ds(g * SC_LANES, SC_LANES)] = tot

        def chunk_body(c, carry):
            t0 = base + c * chunk
            pltpu.sync_copy(idx_hbm.at[pl.ds(t0, chunk)], idx_v)
            pltpu.sync_copy(h_hbm.at[pl.ds(t0, chunk)], h_v)
            _sc_row_pipeline(idx_v, u_hbm, rows_v, sems, chunk * groups, groups, compute)
            pltpu.sync_copy(act_v, act_hbm.at[pl.ds(t0, chunk)])
            return carry

        lax.fori_loop(0, n_chunks, chunk_body, 0)

    return k(hp, idx, up)


def peer_expert_combine(w, idx, vp):
    m, n_sel = w.shape
    dw = vp.shape[1]
    d = 2 * dw
    per_w = m // SC_WORKERS
    n_chunks = per_w // SC_TOK_CHUNK
    n_vec = d // SC_LANES
    groups = n_sel // SC_LANES
    mesh = plsc.VectorSubcoreMesh(core_axis_name="c", subcore_axis_name="s")

    @functools.partial(
        pl.kernel, mesh=mesh,
        out_type=jax.ShapeDtypeStruct((m, d), F32),
        scratch_types=[
            pltpu.VMEM((SC_TOK_CHUNK, n_sel), I32),
            pltpu.VMEM((SC_TOK_CHUNK, n_sel), F32),
            pltpu.VMEM((SC_TOK_CHUNK, d), F32),
            pltpu.VMEM((SC_RING, SC_LANES, dw), I32),
            pltpu.SemaphoreType.DMA((SC_RING,)),
        ],
        compiler_params=pltpu.CompilerParams(needs_layout_passes=False),
        name="peer_expert_combine",
    )
    def k(w_hbm, idx_hbm, v_hbm, out_hbm, idx_v, w_v, out_v, rows_v, sems):
        base = _sc_worker_id() * per_w

        def compute(tt, g, slot):
            splat = []
            for e in range(SC_LANES):
                s = plsc.load_gather(w_v, [jnp.full((SC_LANES,), tt, I32),
                                           jnp.full((SC_LANES,), g * SC_LANES + e, I32)])
                splat.append(plsc.pack(s, s, format=SC_FMT))

            @plsc.parallel_loop(0, dw // SC_LANES)
            def _(j):
                off = pl.multiple_of(j * SC_LANES, SC_LANES)
                acc_lo = out_v[tt, pl.ds(off, SC_LANES)]
                acc_hi = out_v[tt, pl.ds(dw + off, SC_LANES)]
                for e0 in range(0, SC_LANES, SC_BF16_GROUP):
                    s = None
                    for e in range(e0, e0 + SC_BF16_GROUP):
                        p = plsc.bitcast(rows_v[slot, e, pl.ds(off, SC_LANES)], BF16) * splat[e]
                        s = p if s is None else s + p
                    lo, hi = plsc.unpack(s, format=SC_FMT)
                    acc_lo = acc_lo + lo
                    acc_hi = acc_hi + hi
                out_v[tt, pl.ds(off, SC_LANES)] = acc_lo
                out_v[tt, pl.ds(dw + off, SC_LANES)] = acc_hi

        def chunk_body(c, carry):
            t0 = base + c * SC_TOK_CHUNK
            pltpu.sync_copy(idx_hbm.at[pl.ds(t0, SC_TOK_CHUNK)], idx_v)
            pltpu.sync_copy(w_hbm.at[pl.ds(t0, SC_TOK_CHUNK)], w_v)

            def zero_body(z, carry2):
                tt, j = z // n_vec, z % n_vec
                out_v[tt, pl.ds(pl.multiple_of(j * SC_LANES, SC_LANES), SC_LANES)] = (
                    jnp.zeros((SC_LANES,), F32))
                return carry2

            lax.fori_loop(0, SC_TOK_CHUNK * n_vec, zero_body, 0)
            _sc_row_pipeline(idx_v, v_hbm, rows_v, sems, SC_TOK_CHUNK * groups, groups, compute)
            pltpu.sync_copy(out_v, out_hbm.at[pl.ds(t0, SC_TOK_CHUNK)])
            return carry

        lax.fori_loop(0, n_chunks, chunk_body, 0)

    return k(w, idx, vp)


def kernel(x, norm_mix_g, norm_ffn_g, final_norm_g, rel_bias, even_w_in, even_w_out,
           diff_lambda, diff_ln_g, odd_w_in, odd_b_f, odd_w_out, peer_wq, peer_subkeys,
           peer_u, peer_v):
    batch, seq, d = x.shape

    dil_tiles = jnp.stack(
        [dilated_bias_tile(rel_bias[:, :A_HEADS], w, dl) for w, dl in DILATED_BRANCHES])
    diff_tiles = diff_bias_tiles(rel_bias[:, A_HEADS:], seq)
    lam_init = 0.8 - 0.6 * math.exp(-0.3 * 0)
    even_in, even_out = even_w_in[0].astype(BF16), even_w_out[0].astype(BF16)
    w_in = odd_w_in[0]
    odd_in, odd_out = w_in[:, :3 * C_WIDTH].astype(BF16), odd_w_out[0].astype(BF16)
    w_gate = jnp.pad(w_in[:, 3 * C_WIDTH:], ((0, 0), (0, LANES - C_HEADS)))
    b_f = jnp.pad(odd_b_f[0], (0, LANES - C_HEADS)).reshape(1, LANES)
    peer = [dict(wq=peer_wq[l].astype(BF16),
                 sk=peer_subkeys[l].reshape(2 * PEER_HEADS, N_KEYS, PEER_KEY_HALF).astype(BF16))
            for l in range(2)]
    packed_tables = {}

    def expert_table(table, layer, anchor):
        if (id(table), layer) not in packed_tables:
            rows, _ = lax.optimization_barrier((table[layer], anchor))
            packed_tables[(id(table), layer)] = pack_bf16_pairs(rows)
        return packed_tables[(id(table), layer)]

    assert batch % BATCH_GROUPS == 0
    bg = batch // BATCH_GROUPS
    m = bg * seq
    groups = [dict(x=x[g * bg:(g + 1) * bg].reshape(m, d)) for g in range(BATCH_GROUPS)]

    def peer_select(st, layer, after=None):
        q, hp = norm_matmul(st["x"], norm_ffn_g[layer], peer[layer]["wq"], want_h="packed")
        idx, gate = peer_topk(q, peer[layer]["sk"])
        if after is not None and after["st"] is not st:
            after["st"]["peer"], idx =lax.optimization_barrier((after["st"]["peer"], idx))
        return dict(st=st, layer=layer, idx=idx, gate=gate,
                    act=peer_expert_dots(hp, idx, expert_table(peer_u, layer, st["x"])))

    def peer_combine(rec):
        rec["w"] = gelu_gate(rec["act"], rec["gate"])
        rec["st"]["peer"] = peer_expert_combine(
            rec["w"], rec["idx"], expert_table(peer_v, rec["layer"], rec["act"]))

    def even_layer(st, tie):
        (p,) = norm_matmul(st["x"], norm_mix_g[0], even_in)
        p3 = p.reshape(bg, seq, -1)
        oa = dilated_attention(p3, dil_tiles, bg, seq)
        ob = tie(diff_attention(p3, diff_tiles, diff_lambda[0], diff_ln_g[0], lam_init, bg, seq))
        st["x"] = concat_out_proj(oa.reshape(m, A_WIDTH), ob.reshape(m, B_V_WIDTH), even_out,
                                  st["x"])

    def odd_layer(st, tie):
        p, st["x"], fg = norm_matmul(st["x"], norm_mix_g[1], odd_in, res=st["peer"],
                                     want_h="gate", gate_w=w_gate)
        c = logsig_cumsum(fg.reshape(bg, seq, LANES), b_f, bg, seq)
        ct = c[:, :, :C_HEADS].transpose(0, 2, 1)
        o = tie(fox_attention(p.reshape(bg, seq, -1), ct[:, :, :, None], ct[:, :, None, :],
                              bg, seq))
        st["x"] = matmul_residual(o.reshape(m, C_WIDTH), odd_out, st["x"])

    stages = [(even_layer, st, 0) for st in groups] + [(odd_layer, st, 1) for st in groups]
    recs = []
    for k, (mixer, st, layer) in enumerate(stages):
        def tie(t, prev=recs[k - 1] if k >= 1 else None):
            if prev is None:
                return t
            prev["act"], t = lax.optimization_barrier((prev["act"], t))
            peer_combine(prev)
            prev["w"], t = lax.optimization_barrier((prev["w"], t))
            return t

        mixer(st, tie)
        recs.append(peer_select(st, layer, after=recs[max(k - 2, 0)] if k >= 1 else None))
    peer_combine(recs[-1])
    outs = [add_norm(st["x"], st["peer"], final_norm_g).reshape(bg, seq, d) for st in groups]
    return jnp.concatenate(outs, axis=0)
```

```python
import functools
import math

import numpy as np
import jax
import jax.numpy as jnp
from jax import lax
from jax.experimental import pallas as pl
from jax.experimental.pallas import tpu as pltpu
from jax.experimental.pallas import tpu_sc as plsc

F32 = jnp.float32
BF16 = jnp.bfloat16
I32 = jnp.int32

HEAD_DIM = 64
A_HEADS = 8
DILATED_BRANCHES = ((128, 1), (512, 4), (2048, 16))
DIFF_HALF = 64
DIFF_VDIM = 128
B_HEADS = 4
C_HEADS = 16
N_BUCKETS = 32
MAX_DISTANCE = 2048
PEER_HEADS = 8
N_KEYS = 128
PEER_TOPK = 16
PEER_KEY_HALF = 128
RMS_EPS = 1e-6
NEG_INF = -1e30
A_WIDTH = A_HEADS * HEAD_DIM
B_QK_WIDTH = B_HEADS * 2 * DIFF_HALF
B_V_WIDTH = B_HEADS * DIFF_VDIM
C_WIDTH = C_HEADS * HEAD_DIM
QK_SCALE = 0.125
LANES = 128
DIL_BLOCK = 128
ATT_BLOCK = 512
BATCH_GROUPS = 8

SC_CORES = 2
SC_SUBCORES = 16
SC_LANES = 16
SC_WORKERS = SC_CORES * SC_SUBCORES


def _t5_bucket_table(n):
    max_exact = N_BUCKETS // 2
    d = np.arange(n)
    df = np.maximum(d, 1).astype(np.float32)
    large = max_exact + (
        np.log(df / np.float32(max_exact)) / np.float32(math.log(MAX_DISTANCE / max_exact))
        * np.float32(N_BUCKETS - max_exact)).astype(np.int32)
    large = np.minimum(large, N_BUCKETS - 1)
    return np.where(d < max_exact, d, large).astype(np.int32)


def _norm_matmul_body(*refs, has_res, want_h):
    it = iter(refs)
    x_ref = next(it)
    r_ref = next(it) if has_res else None
    g_ref = next(it)
    w_ref = next(it)
    gw_ref = next(it) if want_h == "gate" else None
    o_ref = next(it)
    xs_ref = next(it) if has_res else None
    hout_ref = next(it) if want_h else None
    h_scr = next(it)

    @pl.when(pl.program_id(1) == 0)
    def _():
        x = x_ref[...]
        if has_res:
            x = x + r_ref[...]
            xs_ref[...] = x
        ms = jnp.mean(x * x, axis=-1, keepdims=True)
        h = x * lax.rsqrt(ms + RMS_EPS) * g_ref[...]
        if want_h == "gate":
            hout_ref[...] = jnp.dot(h, gw_ref[...], preferred_element_type=F32,
                                    precision=lax.Precision.HIGHEST)
        elif want_h == "packed":
            hout_ref[...] = pack_bf16_pairs(h)
        h_scr[...] = h.astype(BF16)

    o_ref[...] = jnp.dot(h_scr[...], w_ref[...],
                         preferred_element_type=F32).astype(o_ref.dtype)


def norm_matmul(x, g, w, *, res=None, want_h=None, gate_w=None, out_dtype=BF16,
                tm=1024, tn=1024):
    m, d = x.shape
    n = w.shape[1]
    tn = min(tn, n)
    row = pl.BlockSpec((tm, d), lambda i, j: (i, 0))
    in_specs = [row] + ([row] if res is not None else []) + [
        pl.BlockSpec((1, d), lambda i, j: (0, 0)),
        pl.BlockSpec((d, tn), lambda i, j: (0, j))]
    out_specs = [pl.BlockSpec((tm, tn), lambda i, j: (i, j))]
    out_shape = [jax.ShapeDtypeStruct((m, n), out_dtype)]
    if res is not None:
        out_specs.append(row)
        out_shape.append(jax.ShapeDtypeStruct((m, d), F32))
    if want_h == "gate":
        in_specs.append(pl.BlockSpec(gate_w.shape, lambda i, j: (0, 0)))
        out_specs.append(pl.BlockSpec((tm, gate_w.shape[1]), lambda i, j: (i, 0)))
        out_shape.append(jax.ShapeDtypeStruct((m, gate_w.shape[1]), F32))
    elif want_h == "packed":
        out_specs.append(pl.BlockSpec((tm, d // 2), lambda i, j: (i, 0)))
        out_shape.append(jax.ShapeDtypeStruct((m, d // 2), I32))
    args = [x] + ([res] if res is not None else []) + [g.reshape(1, d), w] + (
        [gate_w] if want_h == "gate" else [])
    return pl.pallas_call(
        functools.partial(_norm_matmul_body, has_res=res is not None, want_h=want_h),
        grid=(m // tm, n // tn),
        in_specs=in_specs,
        out_specs=out_specs,
        out_shape=out_shape,
        scratch_shapes=[pltpu.VMEM((tm, d), BF16)],
        compiler_params=pltpu.CompilerParams(
            dimension_semantics=("parallel", "arbitrary")),
        name="norm_matmul",
    )(*args)


def dilated_bias_tile(rel_bias_a, window, dil):
    n = window // dil
    assert n == DIL_BLOCK
    bucket = _t5_bucket_table(window + 1)
    period = 4 * n
    u = np.arange(period)
    valid = u <= n
    w = jnp.where(jnp.asarray(valid)[None],
                  rel_bias_a.T[:, bucket[np.where(valid, n - u, 0) * dil]], NEG_INF).astype(F32)
    rep = jnp.broadcast_to(w[:, None, :], (A_HEADS, n, period))
    flat = rep.reshape(A_HEADS, n * period)[:, :n * (period - 1)]
    return flat.reshape(A_HEADS, n, period - 1)[:, :, :2 * n]


def _dilated_fused_body(q_ref, k_ref, v_ref, b_ref, o_ref, qf, kf, vf, m_scr, acc_scr, *, seq):
    qf[...] = q_ref[...].astype(F32) * QK_SCALE
    kf[...] = k_ref[...].astype(F32)
    vf[...] = v_ref[...].astype(F32)
    m_scr[...] = jnp.full(m_scr.shape, NEG_INF, F32)
    acc_scr[...] = jnp.zeros(acc_scr.shape, F32)
    n = DIL_BLOCK
    col = lax.broadcasted_iota(I32, (n, 2 * n), 1)
    lane = lax.broadcasted_iota(I32, (1, LANES), 1)
    own = [(lane < HEAD_DIM) == (a == 0) for a in range(2)]
    for bi, (_, dil) in enumerate(DILATED_BRANCHES):
        span = dil * n

        def class_body(r, carry, bi=bi, dil=dil, span=span):
            def block_body(i2, carry2):
                loaded = []
                for u in range(2):
                    i = 2 * i2 + u
                    start = r + span * i
                    rows = pl.ds(start, n, stride=dil)
                    prev = pl.ds(jnp.maximum(start - span, r), n, stride=dil)
                    q = qf[rows, :].astype(BF16)
                    k = jnp.concatenate([kf[prev, :], kf[rows, :]], axis=0).astype(BF16)
                    v = jnp.concatenate([vf[prev, :], vf[rows, :]], axis=0).astype(BF16)
                    has_prev = jnp.logical_or(col >= n, i > 0)
                    state = [(m_scr[a, rows, :], acc_scr[a, rows, :]) for a in range(2)]
                    loaded.append((rows, q, k, v, has_prev, state))
                results = []
                for rows, q, k, v, has_prev, state in loaded:
                    for a in range(2):
                        m_prev, acc_prev = state[a]
                        s = lax.dot_general(jnp.where(own[a], q, jnp.zeros_like(q)), k,
                                            (((1,), (1,)), ((), ())), preferred_element_type=F32)
                        s = jnp.where(has_prev, s + b_ref[bi, a], NEG_INF)
                        m_new = jnp.maximum(m_prev, jnp.max(s, axis=-1, keepdims=True))
                        alpha = jnp.exp(m_prev - m_new)
                        pb = jnp.exp(s - jnp.concatenate([m_new, m_new], axis=1)).astype(BF16)
                        acc_new = alpha * acc_prev + jnp.dot(
                            pb, jnp.where(own[a], v, jnp.ones_like(v)), preferred_element_type=F32)
                        results.append((a, rows, m_new, acc_new))
                for a, rows, m_new, acc_new in results:
                    m_scr[a, rows, :] = m_new
                    acc_scr[a, rows, :] = acc_new
                return carry2

            lax.fori_loop(0, seq // span // 2, block_body, 0)
            return carry

        lax.fori_loop(0, dil, class_body, 0)
    r = [acc_scr[a] / pltpu.roll(acc_scr[a], HEAD_DIM, 1) for a in range(2)]
    o_ref[...] = jnp.where(own[0], r[0], r[1]).astype(o_ref.dtype)


def dilated_attention(p, bias_tiles, batch, seq):
    nk = A_WIDTH // LANES
    blk = (None, seq, LANES)
    assert all(seq % (2 * dil * DIL_BLOCK) == 0 for _, dil in DILATED_BRANCHES)
    return pl.pallas_call(
        functools.partial(_dilated_fused_body, seq=seq),
        grid=(batch, A_HEADS // 2),
        in_specs=[pl.BlockSpec(blk, lambda b, h: (b, 0, h)),
                  pl.BlockSpec(blk, lambda b, h: (b, 0, nk + h)),
                  pl.BlockSpec(blk, lambda b, h: (b, 0, 2 * nk + h)),
                  pl.BlockSpec((len(DILATED_BRANCHES), 2, DIL_BLOCK, 2 * DIL_BLOCK),
                               lambda b, h: (0, h, 0, 0))],
        out_specs=pl.BlockSpec(blk, lambda b, h: (b, 0, h)),
        out_shape=jax.ShapeDtypeStruct((batch, seq, A_WIDTH), BF16),
        scratch_shapes=[pltpu.VMEM((seq, LANES), F32)] * 3 + [
            pltpu.VMEM((2, seq, LANES), F32), pltpu.VMEM((2, seq, LANES), F32)],
        compiler_params=pltpu.CompilerParams(dimension_semantics=("parallel", "parallel")),
        name="dilated_attention",
    )(p, p, p, bias_tiles)


def _diff_body(q_ref, k_ref, v_ref, b_ref, lam_ref, g_ref, o_ref,
               m_scr, l_scr, acc_scr, *, n_tiles, lam_init):
    t = ATT_BLOCK
    qi = pl.program_id(2)
    q = q_ref[...] * QK_SCALE
    lane = lax.broadcasted_iota(I32, (1, LANES), 1)
    qa = [jnp.where((lane < DIFF_HALF) == (a == 0), q, jnp.zeros_like(q)) for a in range(2)]
    ones = jnp.ones((t, LANES), BF16)
    for a in range(2):
        m_scr[a] = jnp.full((t, LANES), NEG_INF, F32)
        l_scr[a] = jnp.zeros((t, LANES), F32)
        acc_scr[a] = jnp.zeros((t, DIFF_VDIM), F32)

    def step(j, masked):
        off = pl.multiple_of(j * t, t)
        ks = k_ref[pl.ds(off, t), :]
        vs = v_ref[pl.ds(off, t), :]
        bias = b_ref[jnp.minimum(qi - j, n_tiles - 1)]
        if masked:
            row = lax.broadcasted_iota(I32, (t, t), 0)
            col = lax.broadcasted_iota(I32, (t, t), 1)
            causal = row >= col
        for a in range(2):
            s = lax.dot_general(qa[a], ks, (((1,), (1,)), ((), ())),
                                preferred_element_type=F32) + bias
            if masked:
                s = jnp.where(causal, s, NEG_INF)
            m_prev = m_scr[a]
            m_new = jnp.maximum(m_prev, jnp.max(s, axis=-1, keepdims=True))
            alpha = jnp.exp(m_prev - m_new)
            pb = jnp.exp(s - jnp.concatenate([m_new] * (t // LANES), axis=1)).astype(BF16)
            l_scr[a] = alpha * l_scr[a] + jnp.dot(pb, ones, preferred_element_type=F32)
            acc_scr[a] = alpha * acc_scr[a] + jnp.dot(pb, vs, preferred_element_type=F32)
            m_scr[a] = m_new

    def loop_body(j, carry):
        step(j, False)
        return carry

    lax.fori_loop(0, qi, loop_body, 0)
    step(qi, True)

    lp = lam_ref[...]
    lam = (jnp.exp(jnp.sum(lp[0:1] * lp[1:2])) - jnp.exp(jnp.sum(lp[2:3] * lp[3:4]))
           + lam_init)
    o = acc_scr[0] / l_scr[0] - lam * (acc_scr[1] / l_scr[1])
    ms = jnp.mean(o * o, axis=-1, keepdims=True)
    y = o * lax.rsqrt(ms + RMS_EPS) * g_ref[...]
    o_ref[...] = (y * (1.0 - lam_init)).astype(o_ref.dtype)


def diff_attention(p, bias_tiles, lam_params, ln_g, lam_init, batch, seq):
    t = ATT_BLOCK
    n_tiles = bias_tiles.shape[1]
    cq = 3 * A_WIDTH // LANES
    ck = cq + B_QK_WIDTH // LANES
    cv = ck + B_QK_WIDTH // LANES
    return pl.pallas_call(
        functools.partial(_diff_body, n_tiles=n_tiles, lam_init=lam_init),
        grid=(batch, B_HEADS, seq // t),
        in_specs=[
            pl.BlockSpec((None, t, LANES), lambda b, h, i: (b, i, cq + h)),
            pl.BlockSpec((None, seq, LANES), lambda b, h, i: (b, 0, ck + h)),
            pl.BlockSpec((None, seq, LANES), lambda b, h, i: (b, 0, cv + h)),
            pl.BlockSpec((None, n_tiles, t, t), lambda b, h, i: (h, 0, 0, 0)),
            pl.BlockSpec((4, DIFF_HALF), lambda b, h, i: (0, 0)),
            pl.BlockSpec((1, DIFF_VDIM), lambda b, h, i: (0, 0)),
        ],
        out_specs=pl.BlockSpec((None, t, LANES), lambda b, h, i: (b, i, h)),
        out_shape=jax.ShapeDtypeStruct((batch, seq, B_V_WIDTH), BF16),
        scratch_shapes=[pltpu.VMEM((2, t, LANES), F32), pltpu.VMEM((2, t, LANES), F32),
                        pltpu.VMEM((2, t, DIFF_VDIM), F32)],
        compiler_params=pltpu.CompilerParams(
            dimension_semantics=("parallel", "parallel", "arbitrary")),
        name="diff_attention",
    )(p, p, p, bias_tiles, lam_params, ln_g.reshape(1, DIFF_VDIM))


def diff_bias_tiles(rel_bias_b, seq):
    t = ATT_BLOCK
    bucket = _t5_bucket_table(max(seq, 2 * MAX_DISTANCE) + 2 * t)
    sat = bucket[-1]
    d_sat = int(np.max(np.nonzero(bucket != sat)[0])) + 1
    n_full = (d_sat + t - 1 + t - 1) // t
    n_tiles = n_full + 1
    assert n_full * t - (t - 1) >= d_sat
    n = np.arange(2 * t)[None, :]
    base = np.arange(n_tiles)[:, None] * t
    dist = np.clip(np.where(n < t, base - n, base + 2 * t - n), 0, None)
    w = rel_bias_b.T[:, bucket[dist]].astype(F32)
    rep = jnp.broadcast_to(w[:, :, None, :], (B_HEADS, n_tiles, t, 2 * t))
    flat = rep.reshape(B_HEADS, n_tiles, 2 * t * t)[:, :, :t * (2 * t - 1)]
    return flat.reshape(B_HEADS, n_tiles, t, 2 * t - 1)[:, :, :, :t]


def _logsig_cumsum_body(f_ref, b_ref, c_ref, carry_scr):
    t = f_ref.shape[0]

    @pl.when(pl.program_id(1) == 0)
    def _():
        carry_scr[...] = jnp.zeros_like(carry_scr)

    x = f_ref[...] + b_ref[...]
    ls = jnp.minimum(x, 0.0) - jnp.log1p(jnp.exp(-jnp.abs(x)))
    row = lax.broadcasted_iota(I32, (t, t), 0)
    col = lax.broadcasted_iota(I32, (t, t), 1)
    tri = (row >= col).astype(F32)
    c = jnp.dot(tri, ls, preferred_element_type=F32,
                precision=lax.Precision.HIGHEST) + carry_scr[...]
    c_ref[...] = c
    carry_scr[...] = c[t - 1:t, :]


def logsig_cumsum(fg, b_f, batch, seq, t=512):
    return pl.pallas_call(
        _logsig_cumsum_body,
        grid=(batch, seq // t),
        in_specs=[pl.BlockSpec((None, t, LANES), lambda b, i: (b, i, 0)),
                  pl.BlockSpec((1, LANES), lambda b, i: (0, 0))],
        out_specs=pl.BlockSpec((None, t, LANES), lambda b, i: (b, i, 0)),
        out_shape=jax.ShapeDtypeStruct((batch, seq, LANES), F32),
        scratch_shapes=[pltpu.VMEM((1, LANES), F32)],
        compiler_params=pltpu.CompilerParams(
            dimension_semantics=("parallel", "arbitrary")),
        name="logsig_cumsum",
    )(fg, b_f)


def _fox_body(q_ref, k_ref, v_ref, cq_ref, ck_ref, o_ref, m_scr, acc_scr):
    t = ATT_BLOCK
    qi = pl.program_id(2)
    q = q_ref[...] * QK_SCALE
    lane = lax.broadcasted_iota(I32, (1, LANES), 1)
    own = [(lane < HEAD_DIM) == (a == 0) for a in range(2)]
    qa = [jnp.where(own[a], q, jnp.zeros_like(q)) for a in range(2)]
    cqb = [jnp.broadcast_to(cq_ref[a], (t, LANES)) for a in range(2)]
    for a in range(2):
        m_scr[a] = jnp.full((t, LANES), NEG_INF, F32)
        acc_scr[a] = jnp.zeros((t, LANES), F32)

    def step(j, masked):
        off = pl.multiple_of(j * t, t)
        ks = k_ref[pl.ds(off, t), :]
        vs = v_ref[pl.ds(off, t), :]
        if masked:
            row = lax.broadcasted_iota(I32, (t, t), 0)
            col = lax.broadcasted_iota(I32, (t, t), 1)
            causal = row >= col
        for a in range(2):
            s = lax.dot_general(qa[a], ks, (((1,), (1,)), ((), ())),
                                preferred_element_type=F32)
            s = s - ck_ref[a, :, pl.ds(off, t)]
            if masked:
                s = jnp.where(causal, s, NEG_INF)
            m_prev = m_scr[a]
            m_new = jnp.maximum(m_prev, jnp.max(s, axis=-1, keepdims=True) + cqb[a])
            alpha = jnp.exp(m_prev - m_new)
            shift = m_new - cqb[a]
            pb = jnp.exp(s - jnp.concatenate([shift] * (t // LANES), axis=1)).astype(BF16)
            v_aug = jnp.where(own[a], vs, jnp.ones_like(vs))
            acc_scr[a] = alpha * acc_scr[a] + jnp.dot(pb, v_aug, preferred_element_type=F32)
            m_scr[a] = m_new

    def loop_body(j, carry):
        step(j, False)
        return carry

    lax.fori_loop(0, qi, loop_body, 0)
    step(qi, True)
    r = [acc_scr[a] / pltpu.roll(acc_scr[a], HEAD_DIM, 1) for a in range(2)]
    o_ref[...] = jnp.where(own[0], r[0], r[1]).astype(o_ref.dtype)


def fox_attention(p, cq, ck, batch, seq):
    t = ATT_BLOCK
    nk = C_WIDTH // LANES
    return pl.pallas_call(
        _fox_body,
        grid=(batch, C_HEADS // 2, seq // t),
        in_specs=[
            pl.BlockSpec((None, t, LANES), lambda b, h, i: (b, i, h)),
            pl.BlockSpec((None, seq, LANES), lambda b, h, i: (b, 0, nk + h)),
            pl.BlockSpec((None, seq, LANES), lambda b, h, i: (b, 0, 2 * nk + h)),
            pl.BlockSpec((None, 2, t, 1), lambda b, h, i: (b, h, i, 0)),
            pl.BlockSpec((None, 2, 1, seq), lambda b, h, i: (b, h, 0, 0)),
        ],
        out_specs=pl.BlockSpec((None, t, LANES), lambda b, h, i: (b, i, h)),
        out_shape=jax.ShapeDtypeStruct((batch, seq, C_WIDTH), BF16),
        scratch_shapes=[pltpu.VMEM((2, t, LANES), F32), pltpu.VMEM((2, t, LANES), F32)],
        compiler_params=pltpu.CompilerParams(
            dimension_semantics=("parallel", "parallel", "arbitrary")),
        name="fox_attention",
    )(p, p, p, cq, ck)


def _concat_out_body(oa_ref, ob_ref, w_ref, x_ref, out_ref):
    out_ref[...] = (x_ref[...]
                    + jnp.dot(oa_ref[...], w_ref[:A_WIDTH, :], preferred_element_type=F32)
                    + jnp.dot(ob_ref[...], w_ref[A_WIDTH:, :], preferred_element_type=F32))


def concat_out_proj(oa, ob, w, x, tm=1024, tn=1024):
    m, d = x.shape
    return pl.pallas_call(
        _concat_out_body,
        grid=(m // tm, d // tn),
        in_specs=[pl.BlockSpec((tm, A_WIDTH), lambda i, j: (i, 0)),
                  pl.BlockSpec((tm, B_V_WIDTH), lambda i, j: (i, 0)),
                  pl.BlockSpec((A_WIDTH + B_V_WIDTH, tn), lambda i, j: (0, j)),
                  pl.BlockSpec((tm, tn), lambda i, j: (i, j))],
        out_specs=pl.BlockSpec((tm, tn), lambda i, j: (i, j)),
        out_shape=jax.ShapeDtypeStruct((m, d), F32),
        compiler_params=pltpu.CompilerParams(dimension_semantics=("parallel", "parallel")),
        name="concat_out_proj",
    )(oa, ob, w, x)


def _matmul_res_body(a_ref, w_ref, x_ref, o_ref):
    o_ref[...] = x_ref[...] + jnp.dot(a_ref[...], w_ref[...], preferred_element_type=F32)


def matmul_residual(a, w, x, tm=1024, tn=1024):
    m, k = a.shape
    n = w.shape[1]
    return pl.pallas_call(
        _matmul_res_body,
        grid=(m // tm, n // tn),
        in_specs=[pl.BlockSpec((tm, k), lambda i, j: (i, 0)),
                  pl.BlockSpec((k, tn), lambda i, j: (0, j)),
                  pl.BlockSpec((tm, tn), lambda i, j: (i, j))],
        out_specs=pl.BlockSpec((tm, tn), lambda i, j: (i, j)),
        out_shape=jax.ShapeDtypeStruct((m, n), F32),
        compiler_params=pltpu.CompilerParams(
            dimension_semantics=("parallel", "parallel")),
        name="matmul_residual",
    )(a, w, x)


def _add_norm_body(x_ref, r_ref, g_ref, o_ref):
    x = x_ref[...] + r_ref[...]
    ms = jnp.mean(x * x, axis=-1, keepdims=True)
    o_ref[...] = x * lax.rsqrt(ms + RMS_EPS) * g_ref[...]


def add_norm(x, r, g, tm=512):
    m, d = x.shape
    row = pl.BlockSpec((tm, d), lambda i: (i, 0))
    return pl.pallas_call(
        _add_norm_body,
        grid=(m // tm,),
        in_specs=[row, row, pl.BlockSpec((1, d), lambda i: (0, 0))],
        out_specs=row,
        out_shape=jax.ShapeDtypeStruct((m, d), F32),
        compiler_params=pltpu.CompilerParams(dimension_semantics=("parallel",)),
        name="add_norm",
    )(x, r, g.reshape(1, d))


def _peer_candidates():
    groups = [("a", 0, 0), ("a", 0, 8), ("a", 1, 0), ("b", 0, 8), ("a", 2, 0), ("a", 3, 0),
              ("b", 0, 0), ("b", 1, 0), ("b", 2, 0)]
    cid = np.zeros((8 * len(groups), 1), np.int32)
    seen = set()
    for g, (kind, fixed, start) in enumerate(groups):
        for r in range(8):
            a, b = (fixed, start + r) if kind == "a" else (start + r, fixed)
            row = 8 * g + r
            if (a + 1) * (b + 1) <= PEER_TOPK and (a, b) not in seen:
                seen.add((a, b))
                cid[row, 0] = a * PEER_TOPK + b
            else:
                cid[row, 0] = PEER_TOPK * PEER_TOPK + row
    assert len(seen) == sum((a + 1) * (b + 1) <= PEER_TOPK
                            for a in range(PEER_TOPK) for b in range(PEER_TOPK))
    return groups, cid


def _peer_topk_body(q_ref, sk_ref, cid_ref, idx_ref, gate_ref, ts_scr, ti_scr,
                    bs_scr, be_scr, cv_scr, ce_scr, sc_scr, *, groups):
    tm = q_ref.shape[0]
    neg_inf = jnp.float32(-jnp.inf)
    key_id = lax.broadcasted_iota(I32, (1, N_KEYS, tm), 1)
    batch = sc_scr.shape[0]

    def group_body(g, carry):
        for j in range(batch):
            pr = g * batch + j
            off = pl.multiple_of(pr * PEER_KEY_HALF, PEER_KEY_HALF)
            sc_scr[j] = lax.dot_general(sk_ref[pr], q_ref[:, pl.ds(off, PEER_KEY_HALF)],
                                        (((1,), (1,)), ((), ())), preferred_element_type=F32)
        rows = pl.ds(g * batch, batch)

        def k_body(k, c):
            vals = sc_scr[...]
            m = jnp.max(vals, axis=1, keepdims=True)
            sel = jnp.min(jnp.where(vals == m, key_id, N_KEYS), axis=1, keepdims=True)
            ts_scr[rows, pl.ds(k, 1), :] = m
            ti_scr[rows, pl.ds(k, 1), :] = sel
            sc_scr[...] = jnp.where(key_id == sel, neg_inf, vals)
            return c

        lax.fori_loop(0, PEER_TOPK, k_body, 0)
        return carry

    lax.fori_loop(0, 2 * PEER_HEADS // batch, group_body, 0)

    cand_id = cid_ref[...]
    pad = jnp.where(cand_id < PEER_TOPK * PEER_TOPK, 0.0, neg_inf)

    def pick(x1, x2):
        return jnp.concatenate(
            [x1[f:f + 1] + x2[s:s + 8] if kind == "a" else x1[s:s + 8] + x2[f:f + 1]
             for kind, f, s in groups], axis=0)

    def head_body(h, carry):
        cv_scr[h] = pick(ts_scr[2 * h], ts_scr[2 * h + 1]) + pad
        ce_scr[h] = pick(ti_scr[2 * h] * N_KEYS, ti_scr[2 * h + 1])
        return carry

    lax.fori_loop(0, PEER_HEADS, head_body, 0)

    cid3 = cand_id[None]

    def k_body(k, carry):
        vals = cv_scr[...]
        m = jnp.max(vals, axis=1, keepdims=True)
        sel = jnp.min(jnp.where(vals == m, cid3, PEER_TOPK * PEER_TOPK), axis=1, keepdims=True)
        hit = cid3 == sel
        bs_scr[:, pl.ds(k, 1), :] = m
        be_scr[:, pl.ds(k, 1), :] = jnp.sum(jnp.where(hit, ce_scr[...], 0), axis=1, keepdims=True)
        cv_scr[...] = jnp.where(hit, neg_inf, vals)
        return carry

    lax.fori_loop(0, PEER_TOPK, k_body, 0)
    bs = bs_scr[...]
    e = jnp.exp(bs - jnp.max(bs, axis=1, keepdims=True))
    gate = e / jnp.sum(e, axis=1, keepdims=True)
    n_sel = PEER_HEADS * PEER_TOPK
    gate_ref[...] = gate.reshape(n_sel, tm).T
    idx_ref[...] = be_scr[...].reshape(n_sel, tm).T


def peer_topk(q, subkeys, tm=256):
    m = q.shape[0]
    n_sel = PEER_HEADS * PEER_TOPK
    groups, cid = _peer_candidates()
    out_spec = pl.BlockSpec((tm, n_sel), lambda i: (i, 0))
    return pl.pallas_call(
        functools.partial(_peer_topk_body, groups=groups),
        grid=(m // tm,),
        in_specs=[pl.BlockSpec((tm, q.shape[1]), lambda i: (i, 0)),
                  pl.BlockSpec(subkeys.shape, lambda i: (0, 0, 0)),
                  pl.BlockSpec(cid.shape, lambda i: (0, 0))],
        out_specs=[out_spec, out_spec],
        out_shape=[jax.ShapeDtypeStruct((m, n_sel), I32),
                   jax.ShapeDtypeStruct((m, n_sel), F32)],
        scratch_shapes=[pltpu.VMEM((2 * PEER_HEADS, PEER_TOPK, tm), F32),
                        pltpu.VMEM((2 * PEER_HEADS, PEER_TOPK, tm), I32),
                        pltpu.VMEM((PEER_HEADS, PEER_TOPK, tm), F32),
                        pltpu.VMEM((PEER_HEADS, PEER_TOPK, tm), I32),
                        pltpu.VMEM((PEER_HEADS, cid.shape[0], tm), F32),
                        pltpu.VMEM((PEER_HEADS, cid.shape[0], tm), I32),
                        pltpu.VMEM((PEER_HEADS, N_KEYS, tm), F32)],
        compiler_params=pltpu.CompilerParams(dimension_semantics=("parallel",)),
        name="peer_topk",
    )(q, subkeys, jnp.asarray(cid))


def _gelu_gate_body(a_ref, g_ref, o_ref):
    a = a_ref[...]
    o_ref[...] = g_ref[...] * (0.5 * a * (1.0 + lax.erf(a * (2.0 ** -0.5))))


def gelu_gate(act, gate, tm=2048):
    m, n = act.shape
    spec = pl.BlockSpec((tm, n), lambda i: (i, 0))
    return pl.pallas_call(
        _gelu_gate_body,
        grid=(m // tm,),
        in_specs=[spec, spec],
        out_specs=spec,
        out_shape=jax.ShapeDtypeStruct((m, n), F32),
        compiler_params=pltpu.CompilerParams(dimension_semantics=("parallel",)),
        name="gelu_gate",
    )(act, gate)


SC_TOK_CHUNK = 32
SC_DOTS_CHUNK = 64
SC_RING = 8
SC_BF16_GROUP = 4
SC_FMT = plsc.PackFormat.INTERLEAVED


def _sc_worker_id():
    return lax.axis_index("s") * SC_CORES + lax.axis_index("c")


def pack_bf16_pairs(t):
    half = t.shape[-1] // 2
    bits = lax.bitcast_convert_type(t.astype(BF16).astype(F32), I32)
    return (bits[..., half:] & jnp.int32(-65536)) | lax.shift_right_logical(
        bits[..., :half], jnp.int32(16))


def _sc_row_pipeline(idx_v, table_hbm, rows_v, sems, n_items, groups, compute):
    def gather(item):
        tt, g = item // groups, item % groups
        ids = idx_v[tt, pl.ds(g * SC_LANES, SC_LANES)]
        slot = item % SC_RING
        return pltpu.make_async_copy(table_hbm.at[ids], rows_v.at[slot], sems.at[slot])

    for s in range(SC_RING - 1):
        gather(s).start()

    def item_body(item, carry):
        nxt = item + SC_RING - 1

        @pl.when(nxt < n_items)
        def _():
            gather(nxt).start()

        gather(item).wait()
        compute(item // groups, item % groups, item % SC_RING)
        return carry

    lax.fori_loop(0, n_items, item_body, 0)


def peer_expert_dots(hp, idx, up):
    m, dw = hp.shape
    n_sel = idx.shape[1]
    per_w = m // SC_WORKERS
    chunk = SC_DOTS_CHUNK
    n_chunks = per_w // chunk
    groups = n_sel // SC_LANES
    step = SC_BF16_GROUP * SC_LANES
    mesh = plsc.VectorSubcoreMesh(core_axis_name="c", subcore_axis_name="s")

    @functools.partial(
        pl.kernel, mesh=mesh,
        out_type=jax.ShapeDtypeStruct((m, n_sel), F32),
        scratch_types=[
            pltpu.VMEM((chunk, n_sel), I32),
            pltpu.VMEM((chunk, dw), I32),
            pltpu.VMEM((chunk, n_sel), F32),
            pltpu.VMEM((SC_RING, SC_LANES, dw), I32),
            pltpu.VMEM((SC_LANES * SC_LANES,), F32),
            pltpu.SemaphoreType.DMA((SC_RING,)),
        ],
        compiler_params=pltpu.CompilerParams(needs_layout_passes=False),
        name="peer_expert_dots",
    )
    def k(h_hbm, idx_hbm, u_hbm, act_hbm, idx_v, h_v, act_v, rows_v, part_v, sems):
        base = _sc_worker_id() * per_w
        lane = lax.broadcasted_iota(I32, (SC_LANES,), 0)

        def compute(tt, g, slot):
            def grp_body(q, accs):
                off = pl.multiple_of(q * step, step)
                xs = [plsc.bitcast(h_v[tt, pl.ds(off + c * SC_LANES, SC_LANES)], BF16)
                      for c in range(SC_BF16_GROUP)]
                new = []
                for e in range(SC_LANES):
                    ps = [plsc.bitcast(
                        rows_v[slot, e, pl.ds(off + c * SC_LANES, SC_LANES)], BF16) * xs[c]
                        for c in range(SC_BF16_GROUP)]
                    while len(ps) > 1:
                        ps = [ps[i] + ps[i + 1] for i in range(0, len(ps), 2)]
                    lo, hi = plsc.unpack(ps[0], format=SC_FMT)
                    new.append(accs[e] + (lo + hi))
                return tuple(new)

            accs = lax.fori_loop(
                0, dw // step, grp_body,
                tuple(jnp.zeros((SC_LANES,), F32) for _ in range(SC_LANES)))
            for e in range(SC_LANES):
                part_v[pl.ds(e * SC_LANES, SC_LANES)] = accs[e]
            tot = jnp.zeros((SC_LANES,), F32)
            for l in range(SC_LANES):
                tot = tot + plsc.load_gather(part_v, [lane * SC_LANES + l])
            act_v[tt, pl.ds(g * SC_LANES, SC_LANES)] = tot

        def chunk_body(c, carry):
            t0 = base + c * chunk
            pltpu.sync_copy(idx_hbm.at[pl.ds(t0, chunk)], idx_v)
            pltpu.sync_copy(h_hbm.at[pl.ds(t0, chunk)], h_v)
            _sc_row_pipeline(idx_v, u_hbm, rows_v, sems, chunk * groups, groups, compute)
            pltpu.sync_copy(act_v, act_hbm.at[pl.ds(t0, chunk)])
            return carry

        lax.fori_loop(0, n_chunks, chunk_body, 0)

    return k(hp, idx, up)


def peer_expert_combine(w, idx, vp):
    m, n_sel = w.shape
    dw = vp.shape[1]
    d = 2 * dw
    per_w = m // SC_WORKERS
    n_chunks = per_w // SC_TOK_CHUNK
    n_vec = d // SC_LANES
    groups = n_sel // SC_LANES
    mesh = plsc.VectorSubcoreMesh(core_axis_name="c", subcore_axis_name="s")

    @functools.partial(
        pl.kernel, mesh=mesh,
        out_type=jax.ShapeDtypeStruct((m, d), F32),
        scratch_types=[
            pltpu.VMEM((SC_TOK_CHUNK, n_sel), I32),
            pltpu.VMEM((SC_TOK_CHUNK, n_sel), F32),
            pltpu.VMEM((SC_TOK_CHUNK, d), F32),
            pltpu.VMEM((SC_RING, SC_LANES, dw), I32),
            pltpu.SemaphoreType.DMA((SC_RING,)),
        ],
        compiler_params=pltpu.CompilerParams(needs_layout_passes=False),
        name="peer_expert_combine",
    )
    def k(w_hbm, idx_hbm, v_hbm, out_hbm, idx_v, w_v, out_v, rows_v, sems):
        base = _sc_worker_id() * per_w

        def compute(tt, g, slot):
            splat = []
            for e in range(SC_LANES):
                s = plsc.load_gather(w_v, [jnp.full((SC_LANES,), tt, I32),
                                           jnp.full((SC_LANES,), g * SC_LANES + e, I32)])
                splat.append(plsc.pack(s, s, format=SC_FMT))

            @plsc.parallel_loop(0, dw // SC_LANES)
            def _(j):
                off = pl.multiple_of(j * SC_LANES, SC_LANES)
                acc_lo = out_v[tt, pl.ds(off, SC_LANES)]
                acc_hi = out_v[tt, pl.ds(dw + off, SC_LANES)]
                for e0 in range(0, SC_LANES, SC_BF16_GROUP):
                    s = None
                    for e in range(e0, e0 + SC_BF16_GROUP):
                        p = plsc.bitcast(rows_v[slot, e, pl.ds(off, SC_LANES)], BF16) * splat[e]
                        s = p if s is None else s + p
                    lo, hi = plsc.unpack(s, format=SC_FMT)
                    acc_lo = acc_lo + lo
                    acc_hi = acc_hi + hi
                out_v[tt, pl.ds(off, SC_LANES)] = acc_lo
                out_v[tt, pl.ds(dw + off, SC_LANES)] = acc_hi

        def chunk_body(c, carry):
            t0 = base + c * SC_TOK_CHUNK
            pltpu.sync_copy(idx_hbm.at[pl.ds(t0, SC_TOK_CHUNK)], idx_v)
            pltpu.sync_copy(w_hbm.at[pl.ds(t0, SC_TOK_CHUNK)], w_v)

            def zero_body(z, carry2):
                tt, j = z // n_vec, z % n_vec
                out_v[tt, pl.ds(pl.multiple_of(j * SC_LANES, SC_LANES), SC_LANES)] = (
                    jnp.zeros((SC_LANES,), F32))
                return carry2

            lax.fori_loop(0, SC_TOK_CHUNK * n_vec, zero_body, 0)
            _sc_row_pipeline(idx_v, v_hbm, rows_v, sems, SC_TOK_CHUNK * groups, groups, compute)
            pltpu.sync_copy(out_v, out_hbm.at[pl.ds(t0, SC_TOK_CHUNK)])
            return carry

        lax.fori_loop(0, n_chunks, chunk_body, 0)

    return k(w, idx, vp)


def kernel(x, norm_mix_g, norm_ffn_g, final_norm_g, rel_bias, even_w_in, even_w_out,
           diff_lambda, diff_ln_g, odd_w_in, odd_b_f, odd_w_out, peer_wq, peer_subkeys,
           peer_u, peer_v):
    batch, seq, d = x.shape

    dil_tiles = jnp.stack(
        [dilated_bias_tile(rel_bias[:, :A_HEADS], w, dl) for w, dl in DILATED_BRANCHES])
    diff_tiles = diff_bias_tiles(rel_bias[:, A_HEADS:], seq)
    lam_init = 0.8 - 0.6 * math.exp(-0.3 * 0)
    even_in, even_out = even_w_in[0].astype(BF16), even_w_out[0].astype(BF16)
    w_in = odd_w_in[0]
    odd_in, odd_out = w_in[:, :3 * C_WIDTH].astype(BF16), odd_w_out[0].astype(BF16)
    w_gate = jnp.pad(w_in[:, 3 * C_WIDTH:], ((0, 0), (0, LANES - C_HEADS)))
    b_f = jnp.pad(odd_b_f[0], (0, LANES - C_HEADS)).reshape(1, LANES)
    peer = [dict(wq=peer_wq[l].astype(BF16),
                 sk=peer_subkeys[l].reshape(2 * PEER_HEADS, N_KEYS, PEER_KEY_HALF).astype(BF16))
            for l in range(2)]
    packed_tables = {}

    def expert_table(table, layer, anchor):
        if (id(table), layer) not in packed_tables:
            rows, _ = lax.optimization_barrier((table[layer], anchor))
            packed_tables[(id(table), layer)] = pack_bf16_pairs(rows)
        return packed_tables[(id(table), layer)]

    assert batch % BATCH_GROUPS == 0
    bg = batch // BATCH_GROUPS
    m = bg * seq
    groups = [dict(x=x[g * bg:(g + 1) * bg].reshape(m, d)) for g in range(BATCH_GROUPS)]

    def peer_select(st, layer, after=None):
        q, hp = norm_matmul(st["x"], norm_ffn_g[layer], peer[layer]["wq"], want_h="packed")
        idx, gate = peer_topk(q, peer[layer]["sk"])
        if after is not None and after["st"] is not st:
            after["st"]["peer"], idx =lax.optimization_barrier((after["st"]["peer"], idx))
        return dict(st=st, layer=layer, idx=idx, gate=gate,
                    act=peer_expert_dots(hp, idx, expert_table(peer_u, layer, st["x"])))

    def peer_combine(rec):
        rec["w"] = gelu_gate(rec["act"], rec["gate"])
        rec["st"]["peer"] = peer_expert_combine(
            rec["w"], rec["idx"], expert_table(peer_v, rec["layer"], rec["act"]))

    def even_layer(st, tie):
        (p,) = norm_matmul(st["x"], norm_mix_g[0], even_in)
        p3 = p.reshape(bg, seq, -1)
        oa = dilated_attention(p3, dil_tiles, bg, seq)
        ob = tie(diff_attention(p3, diff_tiles, diff_lambda[0], diff_ln_g[0], lam_init, bg, seq))
        st["x"] = concat_out_proj(oa.reshape(m, A_WIDTH), ob.reshape(m, B_V_WIDTH), even_out,
                                  st["x"])

    def odd_layer(st, tie):
        p, st["x"], fg = norm_matmul(st["x"], norm_mix_g[1], odd_in, res=st["peer"],
                                     want_h="gate", gate_w=w_gate)
        c = logsig_cumsum(fg.reshape(bg, seq, LANES), b_f, bg, seq)
        ct = c[:, :, :C_HEADS].transpose(0, 2, 1)
        o = tie(fox_attention(p.reshape(bg, seq, -1), ct[:, :, :, None], ct[:, :, None, :],
                              bg, seq))
        st["x"] = matmul_residual(o.reshape(m, C_WIDTH), odd_out, st["x"])

    stages = [(even_layer, st, 0) for st in groups] + [(odd_layer, st, 1) for st in groups]
    recs = []
    for k, (mixer, st, layer) in enumerate(stages):
        def tie(t, prev=recs[k - 1] if k >= 1 else None):
            if prev is None:
                return t
            prev["act"], t = lax.optimization_barrier((prev["act"], t))
            peer_combine(prev)
            prev["w"], t = lax.optimization_barrier((prev["w"], t))
            return t

        mixer(st, tie)
        recs.append(peer_select(st, layer, after=recs[max(k - 2, 0)] if k >= 1 else None))
    peer_combine(recs[-1])
    outs = [add_norm(st["x"], st["peer"], final_norm_g).reshape(bg, seq, d) for st in groups]
    return jnp.concatenate(outs, axis=0)
```

```python
import functools
import math

import numpy as np
import jax
import jax.numpy as jnp
from jax import lax
from jax.experimental import pallas as pl
from jax.experimental.pallas import tpu as pltpu
from jax.experimental.pallas import tpu_sc as plsc

F32 = jnp.float32
BF16 = jnp.bfloat16
I32 = jnp.int32

HEAD_DIM = 64
A_HEADS = 8
DILATED_BRANCHES = ((128, 1), (512, 4), (2048, 16))
DIFF_HALF = 64
DIFF_VDIM = 128
B_HEADS = 4
C_HEADS = 16
N_BUCKETS = 32
MAX_DISTANCE = 2048
PEER_HEADS = 8
N_KEYS = 128
PEER_TOPK = 16
PEER_KEY_HALF = 128
RMS_EPS = 1e-6
NEG_INF = -1e30
A_WIDTH = A_HEADS * HEAD_DIM
B_QK_WIDTH = B_HEADS * 2 * DIFF_HALF
B_V_WIDTH = B_HEADS * DIFF_VDIM
C_WIDTH = C_HEADS * HEAD_DIM
QK_SCALE = 0.125
LANES = 128
DIL_BLOCK = 128
ATT_BLOCK = 512
BATCH_GROUPS = 8

SC_CORES = 2
SC_SUBCORES = 16
SC_LANES = 16
SC_WORKERS = SC_CORES * SC_SUBCORES


def _t5_bucket_table(n):
    max_exact = N_BUCKETS // 2
    d = np.arange(n)
    df = np.maximum(d, 1).astype(np.float32)
    large = max_exact + (
        np.log(df / np.float32(max_exact)) / np.float32(math.log(MAX_DISTANCE / max_exact))
        * np.float32(N_BUCKETS - max_exact)).astype(np.int32)
    large = np.minimum(large, N_BUCKETS - 1)
    return np.where(d < max_exact, d, large).astype(np.int32)


def _norm_matmul_body(*refs, has_res, want_h):
    it = iter(refs)
    x_ref = next(it)
    r_ref = next(it) if has_res else None
    g_ref = next(it)
    w_ref = next(it)
    gw_ref = next(it) if want_h == "gate" else None
    o_ref = next(it)
    xs_ref = next(it) if has_res else None
    hout_ref = next(it) if want_h else None
    h_scr = next(it)

    @pl.when(pl.program_id(1) == 0)
    def _():
        x = x_ref[...]
        if has_res:
            x = x + r_ref[...]
            xs_ref[...] = x
        ms = jnp.mean(x * x, axis=-1, keepdims=True)
        h = x * lax.rsqrt(ms + RMS_EPS) * g_ref[...]
        if want_h == "gate":
            hout_ref[...] = jnp.dot(h, gw_ref[...], preferred_element_type=F32,
                                    precision=lax.Precision.HIGHEST)
        elif want_h == "packed":
            hout_ref[...] = pack_bf16_pairs(h)
        h_scr[...] = h.astype(BF16)

    o_ref[...] = jnp.dot(h_scr[...], w_ref[...],
                         preferred_element_type=F32).astype(o_ref.dtype)


def norm_matmul(x, g, w, *, res=None, want_h=None, gate_w=None, out_dtype=BF16,
                tm=1024, tn=1024):
    m, d = x.shape
    n = w.shape[1]
    tn = min(tn, n)
    row = pl.BlockSpec((tm, d), lambda i, j: (i, 0))
    in_specs = [row] + ([row] if res is not None else []) + [
        pl.BlockSpec((1, d), lambda i, j: (0, 0)),
        pl.BlockSpec((d, tn), lambda i, j: (0, j))]
    out_specs = [pl.BlockSpec((tm, tn), lambda i, j: (i, j))]
    out_shape = [jax.ShapeDtypeStruct((m, n), out_dtype)]
    if res is not None:
        out_specs.append(row)
        out_shape.append(jax.ShapeDtypeStruct((m, d), F32))
    if want_h == "gate":
        in_specs.append(pl.BlockSpec(gate_w.shape, lambda i, j: (0, 0)))
        out_specs.append(pl.BlockSpec((tm, gate_w.shape[1]), lambda i, j: (i, 0)))
        out_shape.append(jax.ShapeDtypeStruct((m, gate_w.shape[1]), F32))
    elif want_h == "packed":
        out_specs.append(pl.BlockSpec((tm, d // 2), lambda i, j: (i, 0)))
        out_shape.append(jax.ShapeDtypeStruct((m, d // 2), I32))
    args = [x] + ([res] if res is not None else []) + [g.reshape(1, d), w] + (
        [gate_w] if want_h == "gate" else [])
    return pl.pallas_call(
        functools.partial(_norm_matmul_body, has_res=res is not None, want_h=want_h),
        grid=(m // tm, n // tn),
        in_specs=in_specs,
        out_specs=out_specs,
        out_shape=out_shape,
        scratch_shapes=[pltpu.VMEM((tm, d), BF16)],
        compiler_params=pltpu.CompilerParams(
            dimension_semantics=("parallel", "arbitrary")),
        name="norm_matmul",
    )(*args)


def dilated_bias_tile(rel_bias_a, window, dil):
    n = window // dil
    assert n == DIL_BLOCK
    bucket = _t5_bucket_table(window + 1)
    period = 4 * n
    u = np.arange(period)
    valid = u <= n
    w = jnp.where(jnp.asarray(valid)[None],
                  rel_bias_a.T[:, bucket[np.where(valid, n - u, 0) * dil]], NEG_INF).astype(F32)
    rep = jnp.broadcast_to(w[:, None, :], (A_HEADS, n, period))
    flat = rep.reshape(A_HEADS, n * period)[:, :n * (period - 1)]
    return flat.reshape(A_HEADS, n, period - 1)[:, :, :2 * n]


def _dilated_fused_body(q_ref, k_ref, v_ref, b_ref, o_ref, qf, kf, vf, m_scr, acc_scr, *, seq):
    qf[...] = q_ref[...].astype(F32) * QK_SCALE
    kf[...] = k_ref[...].astype(F32)
    vf[...] = v_ref[...].astype(F32)
    m_scr[...] = jnp.full(m_scr.shape, NEG_INF, F32)
    acc_scr[...] = jnp.zeros(acc_scr.shape, F32)
    n = DIL_BLOCK
    col = lax.broadcasted_iota(I32, (n, 2 * n), 1)
    lane = lax.broadcasted_iota(I32, (1, LANES), 1)
    own = [(lane < HEAD_DIM) == (a == 0) for a in range(2)]
    for bi, (_, dil) in enumerate(DILATED_BRANCHES):
        span = dil * n

        def class_body(r, carry, bi=bi, dil=dil, span=span):
            def block_body(i2, carry2):
                loaded = []
                for u in range(2):
                    i = 2 * i2 + u
                    start = r + span * i
                    rows = pl.ds(start, n, stride=dil)
                    prev = pl.ds(jnp.maximum(start - span, r), n, stride=dil)
                    q = qf[rows, :].astype(BF16)
                    k = jnp.concatenate([kf[prev, :], kf[rows, :]], axis=0).astype(BF16)
                    v = jnp.concatenate([vf[prev, :], vf[rows, :]], axis=0).astype(BF16)
                    has_prev = jnp.logical_or(col >= n, i > 0)
                    state = [(m_scr[a, rows, :], acc_scr[a, rows, :]) for a in range(2)]
                    loaded.append((rows, q, k, v, has_prev, state))
                results = []
                for rows, q, k, v, has_prev, state in loaded:
                    for a in range(2):
                        m_prev, acc_prev = state[a]
                        s = lax.dot_general(jnp.where(own[a], q, jnp.zeros_like(q)), k,
                                            (((1,), (1,)), ((), ())), preferred_element_type=F32)
                        s = jnp.where(has_prev, s + b_ref[bi, a], NEG_INF)
                        m_new = jnp.maximum(m_prev, jnp.max(s, axis=-1, keepdims=True))
                        alpha = jnp.exp(m_prev - m_new)
                        pb = jnp.exp(s - jnp.concatenate([m_new, m_new], axis=1)).astype(BF16)
                        acc_new = alpha * acc_prev + jnp.dot(
                            pb, jnp.where(own[a], v, jnp.ones_like(v)), preferred_element_type=F32)
                        results.append((a, rows, m_new, acc_new))
                for a, rows, m_new, acc_new in results:
                    m_scr[a, rows, :] = m_new
                    acc_scr[a, rows, :] = acc_new
                return carry2

            lax.fori_loop(0, seq // span // 2, block_body, 0)
            return carry

        lax.fori_loop(0, dil, class_body, 0)
    r = [acc_scr[a] / pltpu.roll(acc_scr[a], HEAD_DIM, 1) for a in range(2)]
    o_ref[...] = jnp.where(own[0], r[0], r[1]).astype(o_ref.dtype)


def dilated_attention(p, bias_tiles, batch, seq):
    nk = A_WIDTH // LANES
    blk = (None, seq, LANES)
    assert all(seq % (2 * dil * DIL_BLOCK) == 0 for _, dil in DILATED_BRANCHES)
    return pl.pallas_call(
        functools.partial(_dilated_fused_body, seq=seq),
        grid=(batch, A_HEADS // 2),
        in_specs=[pl.BlockSpec(blk, lambda b, h: (b, 0, h)),
                  pl.BlockSpec(blk, lambda b, h: (b, 0, nk + h)),
                  pl.BlockSpec(blk, lambda b, h: (b, 0, 2 * nk + h)),
                  pl.BlockSpec((len(DILATED_BRANCHES), 2, DIL_BLOCK, 2 * DIL_BLOCK),
                               lambda b, h: (0, h, 0, 0))],
        out_specs=pl.BlockSpec(blk, lambda b, h: (b, 0, h)),
        out_shape=jax.ShapeDtypeStruct((batch, seq, A_WIDTH), BF16),
        scratch_shapes=[pltpu.VMEM((seq, LANES), F32)] * 3 + [
            pltpu.VMEM((2, seq, LANES), F32), pltpu.VMEM((2, seq, LANES), F32)],
        compiler_params=pltpu.CompilerParams(dimension_semantics=("parallel", "parallel")),
        name="dilated_attention",
    )(p, p, p, bias_tiles)


def _diff_body(q_ref, k_ref, v_ref, b_ref, lam_ref, g_ref, o_ref,
               m_scr, l_scr, acc_scr, *, n_tiles, lam_init):
    t = ATT_BLOCK
    qi = pl.program_id(2)
    q = q_ref[...] * QK_SCALE
    lane = lax.broadcasted_iota(I32, (1, LANES), 1)
    qa = [jnp.where((lane < DIFF_HALF) == (a == 0), q, jnp.zeros_like(q)) for a in range(2)]
    ones = jnp.ones((t, LANES), BF16)
    for a in range(2):
        m_scr[a] = jnp.full((t, LANES), NEG_INF, F32)
        l_scr[a] = jnp.zeros((t, LANES), F32)
        acc_scr[a] = jnp.zeros((t, DIFF_VDIM), F32)

    def step(j, masked):
        off = pl.multiple_of(j * t, t)
        ks = k_ref[pl.ds(off, t), :]
        vs = v_ref[pl.ds(off, t), :]
        bias = b_ref[jnp.minimum(qi - j, n_tiles - 1)]
        if masked:
            row = lax.broadcasted_iota(I32, (t, t), 0)
            col = lax.broadcasted_iota(I32, (t, t), 1)
            causal = row >= col
        for a in range(2):
            s = lax.dot_general(qa[a], ks, (((1,), (1,)), ((), ())),
                                preferred_element_type=F32) + bias
            if masked:
                s = jnp.where(causal, s, NEG_INF)
            m_prev = m_scr[a]
            m_new = jnp.maximum(m_prev, jnp.max(s, axis=-1, keepdims=True))
            alpha = jnp.exp(m_prev - m_new)
            pb = jnp.exp(s - jnp.concatenate([m_new] * (t // LANES), axis=1)).astype(BF16)
            l_scr[a] = alpha * l_scr[a] + jnp.dot(pb, ones, preferred_element_type=F32)
            acc_scr[a] = alpha * acc_scr[a] + jnp.dot(pb, vs, preferred_element_type=F32)
            m_scr[a] = m_new

    def loop_body(j, carry):
        step(j, False)
        return carry

    lax.fori_loop(0, qi, loop_body, 0)
    step(qi, True)

    lp = lam_ref[...]
    lam = (jnp.exp(jnp.sum(lp[0:1] * lp[1:2])) - jnp.exp(jnp.sum(lp[2:3] * lp[3:4]))
           + lam_init)
    o = acc_scr[0] / l_scr[0] - lam * (acc_scr[1] / l_scr[1])
    ms = jnp.mean(o * o, axis=-1, keepdims=True)
    y = o * lax.rsqrt(ms + RMS_EPS) * g_ref[...]
    o_ref[...] = (y * (1.0 - lam_init)).astype(o_ref.dtype)


def diff_attention(p, bias_tiles, lam_params, ln_g, lam_init, batch, seq):
    t = ATT_BLOCK
    n_tiles = bias_tiles.shape[1]
    cq = 3 * A_WIDTH // LANES
    ck = cq + B_QK_WIDTH // LANES
    cv = ck + B_QK_WIDTH // LANES
    return pl.pallas_call(
        functools.partial(_diff_body, n_tiles=n_tiles, lam_init=lam_init),
        grid=(batch, B_HEADS, seq // t),
        in_specs=[
            pl.BlockSpec((None, t, LANES), lambda b, h, i: (b, i, cq + h)),
            pl.BlockSpec((None, seq, LANES), lambda b, h, i: (b, 0, ck + h)),
            pl.BlockSpec((None, seq, LANES), lambda b, h, i: (b, 0, cv + h)),
            pl.BlockSpec((None, n_tiles, t, t), lambda b, h, i: (h, 0, 0, 0)),
            pl.BlockSpec((4, DIFF_HALF), lambda b, h, i: (0, 0)),
            pl.BlockSpec((1, DIFF_VDIM), lambda b, h, i: (0, 0)),
        ],
        out_specs=pl.BlockSpec((None, t, LANES), lambda b, h, i: (b, i, h)),
        out_shape=jax.ShapeDtypeStruct((batch, seq, B_V_WIDTH), BF16),
        scratch_shapes=[pltpu.VMEM((2, t, LANES), F32), pltpu.VMEM((2, t, LANES), F32),
                        pltpu.VMEM((2, t, DIFF_VDIM), F32)],
        compiler_params=pltpu.CompilerParams(
            dimension_semantics=("parallel", "parallel", "arbitrary")),
        name="diff_attention",
    )(p, p, p, bias_tiles, lam_params, ln_g.reshape(1, DIFF_VDIM))


def diff_bias_tiles(rel_bias_b, seq):
    t = ATT_BLOCK
    bucket = _t5_bucket_table(max(seq, 2 * MAX_DISTANCE) + 2 * t)
    sat = bucket[-1]
    d_sat = int(np.max(np.nonzero(bucket != sat)[0])) + 1
    n_full = (d_sat + t - 1 + t - 1) // t
    n_tiles = n_full + 1
    assert n_full * t - (t - 1) >= d_sat
    s = LANES
    nb = t // s
    deltas = np.arange(-(nb - 1), nb * n_tiles)
    n = np.arange(2 * s)[None, :]
    base = deltas[:, None] * s
    dist = np.clip(np.where(n < s, base - n, base + 2 * s - n), 0, None)
    w = rel_bias_b.T[:, bucket[dist]].astype(F32)
    rep = jnp.broadcast_to(w[:, :, None, :], (B_HEADS, len(deltas), s, 2 * s))
    flat = rep.reshape(B_HEADS, len(deltas), 2 * s * s)[:, :, :s * (2 * s - 1)]
    sub = flat.reshape(B_HEADS, len(deltas), s, 2 * s - 1)[:, :, :, :s]
    tiles = [jnp.concatenate(
        [jnp.concatenate([sub[:, d * nb + bi - bj + nb - 1] for bj in range(nb)], axis=-1)
         for bi in range(nb)], axis=-2) for d in range(n_tiles)]
    return jnp.stack(tiles, axis=1)


def _logsig_cumsum_body(f_ref, b_ref, c_ref, carry_scr):
    t = f_ref.shape[0]

    @pl.when(pl.program_id(1) == 0)
    def _():
        carry_scr[...] = jnp.zeros_like(carry_scr)

    x = f_ref[...] + b_ref[...]
    ls = jnp.minimum(x, 0.0) - jnp.log1p(jnp.exp(-jnp.abs(x)))
    row = lax.broadcasted_iota(I32, (t, t), 0)
    col = lax.broadcasted_iota(I32, (t, t), 1)
    tri = (row >= col).astype(F32)
    c = jnp.dot(tri, ls, preferred_element_type=F32,
                precision=lax.Precision.HIGHEST) + carry_scr[...]
    c_ref[...] = c
    carry_scr[...] = c[t - 1:t, :]


def logsig_cumsum(fg, b_f, batch, seq, t=512):
    return pl.pallas_call(
        _logsig_cumsum_body,
        grid=(batch, seq // t),
        in_specs=[pl.BlockSpec((None, t, LANES), lambda b, i: (b, i, 0)),
                  pl.BlockSpec((1, LANES), lambda b, i: (0, 0))],
        out_specs=pl.BlockSpec((None, t, LANES), lambda b, i: (b, i, 0)),
        out_shape=jax.ShapeDtypeStruct((batch, seq, LANES), F32),
        scratch_shapes=[pltpu.VMEM((1, LANES), F32)],
        compiler_params=pltpu.CompilerParams(
            dimension_semantics=("parallel", "arbitrary")),
        name="logsig_cumsum",
    )(fg, b_f)


def _fox_body(q_ref, k_ref, v_ref, cq_ref, ck_ref, o_ref, m_scr, acc_scr):
    t = ATT_BLOCK
    qi = pl.program_id(2)
    q = q_ref[...] * QK_SCALE
    lane = lax.broadcasted_iota(I32, (1, LANES), 1)
    own = [(lane < HEAD_DIM) == (a == 0) for a in range(2)]
    qa = [jnp.where(own[a], q, jnp.zeros_like(q)) for a in range(2)]
    cqb = [jnp.broadcast_to(cq_ref[a], (t, LANES)) for a in range(2)]
    for a in range(2):
        m_scr[a] = jnp.full((t, LANES), NEG_INF, F32)
        acc_scr[a] = jnp.zeros((t, LANES), F32)

    def step(j, masked):
        off = pl.multiple_of(j * t, t)
        ks = k_ref[pl.ds(off, t), :]
        vs = v_ref[pl.ds(off, t), :]
        if masked:
            row = lax.broadcasted_iota(I32, (t, t), 0)
            col = lax.broadcasted_iota(I32, (t, t), 1)
            causal = row >= col
        for a in range(2):
            s = lax.dot_general(qa[a], ks, (((1,), (1,)), ((), ())),
                                preferred_element_type=F32)
            s = s - ck_ref[a, :, pl.ds(off, t)]
            if masked:
                s = jnp.where(causal, s, NEG_INF)
            m_prev = m_scr[a]
            m_new = jnp.maximum(m_prev, jnp.max(s, axis=-1, keepdims=True) + cqb[a])
            alpha = jnp.exp(m_prev - m_new)
            shift = m_new - cqb[a]
            pb = jnp.exp(s - jnp.concatenate([shift] * (t // LANES), axis=1)).astype(BF16)
            v_aug = jnp.where(own[a], vs, jnp.ones_like(vs))
            acc_scr[a] = alpha * acc_scr[a] + jnp.dot(pb, v_aug, preferred_element_type=F32)
            m_scr[a] = m_new

    def loop_body(j, carry):
        step(j, False)
        return carry

    lax.fori_loop(0, qi, loop_body, 0)
    step(qi, True)
    r = [acc_scr[a] / pltpu.roll(acc_scr[a], HEAD_DIM, 1) for a in range(2)]
    o_ref[...] = jnp.where(own[0], r[0], r[1]).astype(o_ref.dtype)


def fox_attention(p, cq, ck, batch, seq):
    t = ATT_BLOCK
    nk = C_WIDTH // LANES
    return pl.pallas_call(
        _fox_body,
        grid=(batch, C_HEADS // 2, seq // t),
        in_specs=[
            pl.BlockSpec((None, t, LANES), lambda b, h, i: (b, i, h)),
            pl.BlockSpec((None, seq, LANES), lambda b, h, i: (b, 0, nk + h)),
            pl.BlockSpec((None, seq, LANES), lambda b, h, i: (b, 0, 2 * nk + h)),
            pl.BlockSpec((None, 2, t, 1), lambda b, h, i: (b, h, i, 0)),
            pl.BlockSpec((None, 2, 1, seq), lambda b, h, i: (b, h, 0, 0)),
        ],
        out_specs=pl.BlockSpec((None, t, LANES), lambda b, h, i: (b, i, h)),
        out_shape=jax.ShapeDtypeStruct((batch, seq, C_WIDTH), BF16),
        scratch_shapes=[pltpu.VMEM((2, t, LANES), F32), pltpu.VMEM((2, t, LANES), F32)],
        compiler_params=pltpu.CompilerParams(
            dimension_semantics=("parallel", "parallel", "arbitrary")),
        name="fox_attention",
    )(p, p, p, cq, ck)


def _concat_out_body(oa_ref, ob_ref, w_ref, x_ref, out_ref):
    out_ref[...] = (x_ref[...]
                    + jnp.dot(oa_ref[...], w_ref[:A_WIDTH, :], preferred_element_type=F32)
                    + jnp.dot(ob_ref[...], w_ref[A_WIDTH:, :], preferred_element_type=F32))


def concat_out_proj(oa, ob, w, x, tm=1024, tn=1024):
    m, d = x.shape
    return pl.pallas_call(
        _concat_out_body,
        grid=(m // tm, d // tn),
        in_specs=[pl.BlockSpec((tm, A_WIDTH), lambda i, j: (i, 0)),
                  pl.BlockSpec((tm, B_V_WIDTH), lambda i, j: (i, 0)),
                  pl.BlockSpec((A_WIDTH + B_V_WIDTH, tn), lambda i, j: (0, j)),
                  pl.BlockSpec((tm, tn), lambda i, j: (i, j))],
        out_specs=pl.BlockSpec((tm, tn), lambda i, j: (i, j)),
        out_shape=jax.ShapeDtypeStruct((m, d), F32),
        compiler_params=pltpu.CompilerParams(dimension_semantics=("parallel", "parallel")),
        name="concat_out_proj",
    )(oa, ob, w, x)


def _matmul_res_body(a_ref, w_ref, x_ref, o_ref):
    o_ref[...] = x_ref[...] + jnp.dot(a_ref[...], w_ref[...], preferred_element_type=F32)


def matmul_residual(a, w, x, tm=1024, tn=1024):
    m, k = a.shape
    n = w.shape[1]
    return pl.pallas_call(
        _matmul_res_body,
        grid=(m // tm, n // tn),
        in_specs=[pl.BlockSpec((tm, k), lambda i, j: (i, 0)),
                  pl.BlockSpec((k, tn), lambda i, j: (0, j)),
                  pl.BlockSpec((tm, tn), lambda i, j: (i, j))],
        out_specs=pl.BlockSpec((tm, tn), lambda i, j: (i, j)),
        out_shape=jax.ShapeDtypeStruct((m, n), F32),
        compiler_params=pltpu.CompilerParams(
            dimension_semantics=("parallel", "parallel")),
        name="matmul_residual",
    )(a, w, x)


def _add_norm_body(x_ref, r_ref, g_ref, o_ref):
    x = x_ref[...] + r_ref[...]
    ms = jnp.mean(x * x, axis=-1, keepdims=True)
    o_ref[...] = x * lax.rsqrt(ms + RMS_EPS) * g_ref[...]


def add_norm(x, r, g, tm=512):
    m, d = x.shape
    row = pl.BlockSpec((tm, d), lambda i: (i, 0))
    return pl.pallas_call(
        _add_norm_body,
        grid=(m // tm,),
        in_specs=[row, row, pl.BlockSpec((1, d), lambda i: (0, 0))],
        out_specs=row,
        out_shape=jax.ShapeDtypeStruct((m, d), F32),
        compiler_params=pltpu.CompilerParams(dimension_semantics=("parallel",)),
        name="add_norm",
    )(x, r, g.reshape(1, d))


def _peer_candidates():
    groups = [("a", 0, 0), ("a", 0, 8), ("a", 1, 0), ("b", 0, 8), ("a", 2, 0), ("a", 3, 0),
              ("b", 0, 0), ("b", 1, 0), ("b", 2, 0)]
    cid = np.zeros((8 * len(groups), 1), np.int32)
    seen = set()
    for g, (kind, fixed, start) in enumerate(groups):
        for r in range(8):
            a, b = (fixed, start + r) if kind == "a" else (start + r, fixed)
            row = 8 * g + r
            if (a + 1) * (b + 1) <= PEER_TOPK and (a, b) not in seen:
                seen.add((a, b))
                cid[row, 0] = a * PEER_TOPK + b
            else:
                cid[row, 0] = PEER_TOPK * PEER_TOPK + row
    assert len(seen) == sum((a + 1) * (b + 1) <= PEER_TOPK
                            for a in range(PEER_TOPK) for b in range(PEER_TOPK))
    return groups, cid


def _peer_topk_body(q_ref, sk_ref, cid_ref, idx_ref, gate_ref, ts_scr, ti_scr,
                    bs_scr, be_scr, cv_scr, ce_scr, sc_scr, *, groups):
    tm = q_ref.shape[0]
    neg_inf = jnp.float32(-jnp.inf)
    key_id = lax.broadcasted_iota(I32, (1, N_KEYS, tm), 1)
    batch = sc_scr.shape[0]

    def group_body(g, carry):
        for j in range(batch):
            pr = g * batch + j
            off = pl.multiple_of(pr * PEER_KEY_HALF, PEER_KEY_HALF)
            sc_scr[j] = lax.dot_general(sk_ref[pr], q_ref[:, pl.ds(off, PEER_KEY_HALF)],
                                        (((1,), (1,)), ((), ())), preferred_element_type=F32)
        rows = pl.ds(g * batch, batch)

        def k_body(k, c):
            vals = sc_scr[...]
            m = jnp.max(vals, axis=1, keepdims=True)
            sel = jnp.min(jnp.where(vals == m, key_id, N_KEYS), axis=1, keepdims=True)
            ts_scr[rows, pl.ds(k, 1), :] = m
            ti_scr[rows, pl.ds(k, 1), :] = sel
            sc_scr[...] = jnp.where(key_id == sel, neg_inf, vals)
            return c

        lax.fori_loop(0, PEER_TOPK, k_body, 0)
        return carry

    lax.fori_loop(0, 2 * PEER_HEADS // batch, group_body, 0)

    cand_id = cid_ref[...]
    pad = jnp.where(cand_id < PEER_TOPK * PEER_TOPK, 0.0, neg_inf)

    def pick(x1, x2):
        return jnp.concatenate(
            [x1[f:f + 1] + x2[s:s + 8] if kind == "a" else x1[s:s + 8] + x2[f:f + 1]
             for kind, f, s in groups], axis=0)

    def head_body(h, carry):
        cv_scr[h] = pick(ts_scr[2 * h], ts_scr[2 * h + 1]) + pad
        ce_scr[h] = pick(ti_scr[2 * h] * N_KEYS, ti_scr[2 * h + 1])
        return carry

    lax.fori_loop(0, PEER_HEADS, head_body, 0)

    cid3 = cand_id[None]

    def k_body(k, carry):
        vals = cv_scr[...]
        m = jnp.max(vals, axis=1, keepdims=True)
        sel = jnp.min(jnp.where(vals == m, cid3, PEER_TOPK * PEER_TOPK), axis=1, keepdims=True)
        hit = cid3 == sel
        bs_scr[:, pl.ds(k, 1), :] = m
        be_scr[:, pl.ds(k, 1), :] = jnp.sum(jnp.where(hit, ce_scr[...], 0), axis=1, keepdims=True)
        cv_scr[...] = jnp.where(hit, neg_inf, vals)
        return carry

    lax.fori_loop(0, PEER_TOPK, k_body, 0)
    bs = bs_scr[...]
    e = jnp.exp(bs - jnp.max(bs, axis=1, keepdims=True))
    gate = e / jnp.sum(e, axis=1, keepdims=True)
    n_sel = PEER_HEADS * PEER_TOPK
    gate_ref[...] = gate.reshape(n_sel, tm).T
    idx_ref[...] = be_scr[...].reshape(n_sel, tm).T


def peer_topk(q, subkeys, tm=256):
    m = q.shape[0]
    n_sel = PEER_HEADS * PEER_TOPK
    groups, cid = _peer_candidates()
    out_spec = pl.BlockSpec((tm, n_sel), lambda i: (i, 0))
    return pl.pallas_call(
        functools.partial(_peer_topk_body, groups=groups),
        grid=(m // tm,),
        in_specs=[pl.BlockSpec((tm, q.shape[1]), lambda i: (i, 0)),
                  pl.BlockSpec(subkeys.shape, lambda i: (0, 0, 0)),
                  pl.BlockSpec(cid.shape, lambda i: (0, 0))],
        out_specs=[out_spec, out_spec],
        out_shape=[jax.ShapeDtypeStruct((m, n_sel), I32),
                   jax.ShapeDtypeStruct((m, n_sel), F32)],
        scratch_shapes=[pltpu.VMEM((2 * PEER_HEADS, PEER_TOPK, tm), F32),
                        pltpu.VMEM((2 * PEER_HEADS, PEER_TOPK, tm), I32),
                        pltpu.VMEM((PEER_HEADS, PEER_TOPK, tm), F32),
                        pltpu.VMEM((PEER_HEADS, PEER_TOPK, tm), I32),
                        pltpu.VMEM((PEER_HEADS, cid.shape[0], tm), F32),
                        pltpu.VMEM((PEER_HEADS, cid.shape[0], tm), I32),
                        pltpu.VMEM((PEER_HEADS, N_KEYS, tm), F32)],
        compiler_params=pltpu.CompilerParams(dimension_semantics=("parallel",)),
        name="peer_topk",
    )(q, subkeys, jnp.asarray(cid))


def _gelu_gate_body(a_ref, g_ref, o_ref):
    a = a_ref[...]
    o_ref[...] = g_ref[...] * (0.5 * a * (1.0 + lax.erf(a * (2.0 ** -0.5))))


def gelu_gate(act, gate, tm=2048):
    m, n = act.shape
    spec = pl.BlockSpec((tm, n), lambda i: (i, 0))
    return pl.pallas_call(
        _gelu_gate_body,
        grid=(m // tm,),
        in_specs=[spec, spec],
        out_specs=spec,
        out_shape=jax.ShapeDtypeStruct((m, n), F32),
        compiler_params=pltpu.CompilerParams(dimension_semantics=("parallel",)),
        name="gelu_gate",
    )(act, gate)


SC_TOK_CHUNK = 32
SC_DOTS_CHUNK = 64
SC_RING = 8
SC_BF16_GROUP = 4
SC_FMT = plsc.PackFormat.INTERLEAVED


def _sc_worker_id():
    return lax.axis_index("s") * SC_CORES + lax.axis_index("c")


def pack_bf16_pairs(t):
    half = t.shape[-1] // 2
    bits = lax.bitcast_convert_type(t.astype(BF16).astype(F32), I32)
    return (bits[..., half:] & jnp.int32(-65536)) | lax.shift_right_logical(
        bits[..., :half], jnp.int32(16))


def _sc_row_pipeline(idx_v, table_hbm, rows_v, sems, n_items, groups, compute):
    def gather(item):
        tt, g = item // groups, item % groups
        ids = idx_v[tt, pl.ds(g * SC_LANES, SC_LANES)]
        slot = item % SC_RING
        return pltpu.make_async_copy(table_hbm.at[ids], rows_v.at[slot], sems.at[slot])

    for s in range(SC_RING - 1):
        gather(s).start()

    def item_body(item, carry):
        nxt = item + SC_RING - 1

        @pl.when(nxt < n_items)
        def _():
            gather(nxt).start()

        gather(item).wait()
        compute(item // groups, item % groups, item % SC_RING)
        return carry

    lax.fori_loop(0, n_items, item_body, 0)


def peer_expert_dots(hp, idx, up):
    m, dw = hp.shape
    n_sel = idx.shape[1]
    per_w = m // SC_WORKERS
    chunk = SC_DOTS_CHUNK
    n_chunks = per_w // chunk
    groups = n_sel // SC_LANES
    step = SC_BF16_GROUP * SC_LANES
    mesh = plsc.VectorSubcoreMesh(core_axis_name="c", subcore_axis_name="s")

    @functools.partial(
        pl.kernel, mesh=mesh,
        out_type=jax.ShapeDtypeStruct((m, n_sel), F32),
        scratch_types=[
            pltpu.VMEM((chunk, n_sel), I32),
            pltpu.VMEM((chunk, dw), I32),
            pltpu.VMEM((chunk, n_sel), F32),
            pltpu.VMEM((SC_RING, SC_LANES, dw), I32),
            pltpu.VMEM((SC_LANES * SC_LANES,), F32),
            pltpu.SemaphoreType.DMA((SC_RING,)),
        ],
        compiler_params=pltpu.CompilerParams(needs_layout_passes=False),
        name="peer_expert_dots",
    )
    def k(h_hbm, idx_hbm, u_hbm, act_hbm, idx_v, h_v, act_v, rows_v, part_v, sems):
        base = _sc_worker_id() * per_w
        lane = lax.broadcasted_iota(I32, (SC_LANES,), 0)

        def compute(tt, g, slot):
            def grp_body(q, accs):
                off = pl.multiple_of(q * step, step)
                xs = [plsc.bitcast(h_v[tt, pl.ds(off + c * SC_LANES, SC_LANES)], BF16)
                      for c in range(SC_BF16_GROUP)]
                new = []
                for e in range(SC_LANES):
                    ps = [plsc.bitcast(
                        rows_v[slot, e, pl.ds(off + c * SC_LANES, SC_LANES)], BF16) * xs[c]
                        for c in range(SC_BF16_GROUP)]
                    while len(ps) > 1:
                        ps = [ps[i] + ps[i + 1] for i in range(0, len(ps), 2)]
                    lo, hi = plsc.unpack(ps[0], format=SC_FMT)
                    new.append(accs[e] + (lo + hi))
                return tuple(new)

            accs = lax.fori_loop(
                0, dw // step, grp_body,
                tuple(jnp.zeros((SC_LANES,), F32) for _ in range(SC_LANES)))
            for e in range(SC_LANES):
                part_v[pl.ds(e * SC_LANES, SC_LANES)] = accs[e]
            tot = jnp.zeros((SC_LANES,), F32)
            for l in range(SC_LANES):
                tot = tot + plsc.load_gather(part_v, [lane * SC_LANES + l])
            act_v[tt, pl.ds(g * SC_LANES, SC_LANES)] = tot

        def chunk_body(c, carry):
            t0 = base + c * chunk
            pltpu.sync_copy(idx_hbm.at[pl.ds(t0, chunk)], idx_v)
            pltpu.sync_copy(h_hbm.at[pl.ds(t0, chunk)], h_v)
            _sc_row_pipeline(idx_v, u_hbm, rows_v, sems, chunk * groups, groups, compute)
            pltpu.sync_copy(act_v, act_hbm.at[pl.ds(t0, chunk)])
            return carry

        lax.fori_loop(0, n_chunks, chunk_body, 0)

    return k(hp, idx, up)


def peer_expert_combine(w, idx, vp):
    m, n_sel = w.shape
    dw = vp.shape[1]
    d = 2 * dw
    per_w = m // SC_WORKERS
    n_chunks = per_w // SC_TOK_CHUNK
    n_vec = d // SC_LANES
    groups = n_sel // SC_LANES
    mesh = plsc.VectorSubcoreMesh(core_axis_name="c", subcore_axis_name="s")

    @functools.partial(
        pl.kernel, mesh=mesh,
        out_type=jax.ShapeDtypeStruct((m, d), F32),
        scratch_types=[
            pltpu.VMEM((SC_TOK_CHUNK, n_sel), I32),
            pltpu.VMEM((SC_TOK_CHUNK, n_sel), F32),
            pltpu.VMEM((SC_TOK_CHUNK, d), F32),
            pltpu.VMEM((SC_RING, SC_LANES, dw), I32),
            pltpu.SemaphoreType.DMA((SC_RING,)),
        ],
        compiler_params=pltpu.CompilerParams(needs_layout_passes=False),
        name="peer_expert_combine",
    )
    def k(w_hbm, idx_hbm, v_hbm, out_hbm, idx_v, w_v, out_v, rows_v, sems):
        base = _sc_worker_id() * per_w

        def compute(tt, g, slot):
            splat = []
            for e in range(SC_LANES):
                s = plsc.load_gather(w_v, [jnp.full((SC_LANES,), tt, I32),
                                           jnp.full((SC_LANES,), g * SC_LANES + e, I32)])
                splat.append(plsc.pack(s, s, format=SC_FMT))

            @plsc.parallel_loop(0, dw // SC_LANES)
            def _(j):
                off = pl.multiple_of(j * SC_LANES, SC_LANES)
                acc_lo = out_v[tt, pl.ds(off, SC_LANES)]
                acc_hi = out_v[tt, pl.ds(dw + off, SC_LANES)]
                for e0 in range(0, SC_LANES, SC_BF16_GROUP):
                    s = None
                    for e in range(e0, e0 + SC_BF16_GROUP):
                        p = plsc.bitcast(rows_v[slot, e, pl.ds(off, SC_LANES)], BF16) * splat[e]
                        s = p if s is None else s + p
                    lo, hi = plsc.unpack(s, format=SC_FMT)
                    acc_lo = acc_lo + lo
                    acc_hi = acc_hi + hi
                out_v[tt, pl.ds(off, SC_LANES)] = acc_lo
                out_v[tt, pl.ds(dw + off, SC_LANES)] = acc_hi

        def chunk_body(c, carry):
            t0 = base + c * SC_TOK_CHUNK
            pltpu.sync_copy(idx_hbm.at[pl.ds(t0, SC_TOK_CHUNK)], idx_v)
            pltpu.sync_copy(w_hbm.at[pl.ds(t0, SC_TOK_CHUNK)], w_v)

            def zero_body(z, carry2):
                tt, j = z // n_vec, z % n_vec
                out_v[tt, pl.ds(pl.multiple_of(j * SC_LANES, SC_LANES), SC_LANES)] = (
                    jnp.zeros((SC_LANES,), F32))
                return carry2

            lax.fori_loop(0, SC_TOK_CHUNK * n_vec, zero_body, 0)
            _sc_row_pipeline(idx_v, v_hbm, rows_v, sems, SC_TOK_CHUNK * groups, groups, compute)
            pltpu.sync_copy(out_v, out_hbm.at[pl.ds(t0, SC_TOK_CHUNK)])
            return carry

        lax.fori_loop(0, n_chunks, chunk_body, 0)

    return k(w, idx, vp)


def kernel(x, norm_mix_g, norm_ffn_g, final_norm_g, rel_bias, even_w_in, even_w_out,
           diff_lambda, diff_ln_g, odd_w_in, odd_b_f, odd_w_out, peer_wq, peer_subkeys,
           peer_u, peer_v):
    batch, seq, d = x.shape

    dil_tiles = jnp.stack(
        [dilated_bias_tile(rel_bias[:, :A_HEADS], w, dl) for w, dl in DILATED_BRANCHES])
    diff_tiles = diff_bias_tiles(rel_bias[:, A_HEADS:], seq)
    lam_init = 0.8 - 0.6 * math.exp(-0.3 * 0)
    even_in, even_out = even_w_in[0].astype(BF16), even_w_out[0].astype(BF16)
    w_in = odd_w_in[0]
    odd_in, odd_out = w_in[:, :3 * C_WIDTH].astype(BF16), odd_w_out[0].astype(BF16)
    w_gate = jnp.pad(w_in[:, 3 * C_WIDTH:], ((0, 0), (0, LANES - C_HEADS)))
    b_f = jnp.pad(odd_b_f[0], (0, LANES - C_HEADS)).reshape(1, LANES)
    peer = [dict(wq=peer_wq[l].astype(BF16),
                 sk=peer_subkeys[l].reshape(2 * PEER_HEADS, N_KEYS, PEER_KEY_HALF).astype(BF16))
            for l in range(2)]
    packed_tables = {}

    def expert_table(table, layer, anchor):
        if (id(table), layer) not in packed_tables:
            rows, _ = lax.optimization_barrier((table[layer], anchor))
            packed_tables[(id(table), layer)] = pack_bf16_pairs(rows)
        return packed_tables[(id(table), layer)]

    assert batch % BATCH_GROUPS == 0
    bg = batch // BATCH_GROUPS
    m = bg * seq
    groups = [dict(g=g) for g in range(BATCH_GROUPS)]

    def peer_select(st, layer, after=None):
        q, hp = norm_matmul(st["x"], norm_ffn_g[layer], peer[layer]["wq"], want_h="packed")
        idx, gate = peer_topk(q, peer[layer]["sk"])
        if after is not None and after["st"] is not st:
            after["st"]["peer"], idx =lax.optimization_barrier((after["st"]["peer"], idx))
        return dict(st=st, layer=layer, idx=idx, gate=gate,
                    act=peer_expert_dots(hp, idx, expert_table(peer_u, layer, st["x"])))

    def peer_combine(rec):
        rec["w"] = gelu_gate(rec["act"], rec["gate"])
        rec["st"]["peer"] = peer_expert_combine(
            rec["w"], rec["idx"], expert_table(peer_v, rec["layer"], rec["act"]))

    def even_layer(st, tie):
        (p,) = norm_matmul(st["x"], norm_mix_g[0], even_in)
        p3 = p.reshape(bg, seq, -1)
        oa = dilated_attention(p3, dil_tiles, bg, seq)
        ob = tie(diff_attention(p3, diff_tiles, diff_lambda[0], diff_ln_g[0], lam_init, bg, seq))
        st["x"] = concat_out_proj(oa.reshape(m, A_WIDTH), ob.reshape(m, B_V_WIDTH), even_out,
                                  st["x"])

    def odd_layer(st, tie):
        p, st["x"], fg = norm_matmul(st["x"], norm_mix_g[1], odd_in, res=st["peer"],
                                     want_h="gate", gate_w=w_gate)
        c = logsig_cumsum(fg.reshape(bg, seq, LANES), b_f, bg, seq)
        ct = c[:, :, :C_HEADS].transpose(0, 2, 1)
        o = tie(fox_attention(p.reshape(bg, seq, -1), ct[:, :, :, None], ct[:, :, None, :],
                              bg, seq))
        st["x"] = matmul_residual(o.reshape(m, C_WIDTH), odd_out, st["x"])

    stages = [(even_layer, st, 0) for st in groups] + [(odd_layer, st, 1) for st in groups]
    recs = []
    for k, (mixer, st, layer) in enumerate(stages):
        def tie(t, prev=recs[k - 1] if k >= 1 else None):
            if prev is None:
                return t
            prev["act"], t = lax.optimization_barrier((prev["act"], t))
            peer_combine(prev)
            prev["w"], t = lax.optimization_barrier((prev["w"], t))
            return t

        if "x" not in st:
            xin = x if k == 0 else lax.optimization_barrier((x, recs[k - 1]["idx"]))[0]
            st["x"] = xin[st["g"] * bg:(st["g"] + 1) * bg].reshape(m, d)
        mixer(st, tie)
        recs.append(peer_select(st, layer, after=recs[max(k - 2, 0)] if k >= 1 else None))
    peer_combine(recs[-1])
    outs = [add_norm(st["x"], st["peer"], final_norm_g).reshape(bg, seq, d) for st in groups]
    return jnp.concatenate(outs, axis=0)
```

```python
import functools
import math

import numpy as np
import jax
import jax.numpy as jnp
from jax import lax
from jax.experimental import pallas as pl
from jax.experimental.pallas import tpu as pltpu
from jax.experimental.pallas import tpu_sc as plsc

F32 = jnp.float32
BF16 = jnp.bfloat16
I32 = jnp.int32

HEAD_DIM = 64
A_HEADS = 8
DILATED_BRANCHES = ((128, 1), (512, 4), (2048, 16))
DIFF_HALF = 64
DIFF_VDIM = 128
B_HEADS = 4
C_HEADS = 16
N_BUCKETS = 32
MAX_DISTANCE = 2048
PEER_HEADS = 8
N_KEYS = 128
PEER_TOPK = 16
PEER_KEY_HALF = 128
RMS_EPS = 1e-6
NEG_INF = -1e30
A_WIDTH = A_HEADS * HEAD_DIM
B_QK_WIDTH = B_HEADS * 2 * DIFF_HALF
B_V_WIDTH = B_HEADS * DIFF_VDIM
C_WIDTH = C_HEADS * HEAD_DIM
QK_SCALE = 0.125
LANES = 128
DIL_BLOCK = 128
ATT_BLOCK = 512
BATCH_GROUPS = 8

SC_CORES = 2
SC_SUBCORES = 16
SC_LANES = 16
SC_WORKERS = SC_CORES * SC_SUBCORES


def _t5_bucket_table(n):
    max_exact = N_BUCKETS // 2
    d = np.arange(n)
    df = np.maximum(d, 1).astype(np.float32)
    large = max_exact + (
        np.log(df / np.float32(max_exact)) / np.float32(math.log(MAX_DISTANCE / max_exact))
        * np.float32(N_BUCKETS - max_exact)).astype(np.int32)
    large = np.minimum(large, N_BUCKETS - 1)
    return np.where(d < max_exact, d, large).astype(np.int32)


def _norm_matmul_body(*refs, has_res, want_h):
    it = iter(refs)
    x_ref = next(it)
    r_ref = next(it) if has_res else None
    g_ref = next(it)
    w_ref = next(it)
    gw_ref = next(it) if want_h == "gate" else None
    o_ref = next(it)
    xs_ref = next(it) if has_res else None
    hout_ref = next(it) if want_h else None
    h_scr = next(it)

    @pl.when(pl.program_id(1) == 0)
    def _():
        x = x_ref[...]
        if has_res:
            x = x + r_ref[...]
            xs_ref[...] = x
        ms = jnp.mean(x * x, axis=-1, keepdims=True)
        h = x * lax.rsqrt(ms + RMS_EPS) * g_ref[...]
        if want_h == "gate":
            hout_ref[...] = jnp.dot(h, gw_ref[...], preferred_element_type=F32,
                                    precision=lax.Precision.HIGHEST)
        elif want_h == "packed":
            hout_ref[...] = pack_bf16_pairs(h)
        h_scr[...] = h.astype(BF16)

    o_ref[...] = jnp.dot(h_scr[...], w_ref[...],
                         preferred_element_type=F32).astype(o_ref.dtype)


def norm_matmul(x, g, w, *, res=None, want_h=None, gate_w=None, out_dtype=BF16,
                tm=1024, tn=1024):
    m, d = x.shape
    n = w.shape[1]
    tn = min(tn, n)
    row = pl.BlockSpec((tm, d), lambda i, j: (i, 0))
    in_specs = [row] + ([row] if res is not None else []) + [
        pl.BlockSpec((1, d), lambda i, j: (0, 0)),
        pl.BlockSpec((d, tn), lambda i, j: (0, j))]
    out_specs = [pl.BlockSpec((tm, tn), lambda i, j: (i, j))]
    out_shape = [jax.ShapeDtypeStruct((m, n), out_dtype)]
    if res is not None:
        out_specs.append(row)
        out_shape.append(jax.ShapeDtypeStruct((m, d), F32))
    if want_h == "gate":
        in_specs.append(pl.BlockSpec(gate_w.shape, lambda i, j: (0, 0)))
        out_specs.append(pl.BlockSpec((tm, gate_w.shape[1]), lambda i, j: (i, 0)))
        out_shape.append(jax.ShapeDtypeStruct((m, gate_w.shape[1]), F32))
    elif want_h == "packed":
        out_specs.append(pl.BlockSpec((tm, d // 2), lambda i, j: (i, 0)))
        out_shape.append(jax.ShapeDtypeStruct((m, d // 2), I32))
    args = [x] + ([res] if res is not None else []) + [g.reshape(1, d), w] + (
        [gate_w] if want_h == "gate" else [])
    return pl.pallas_call(
        functools.partial(_norm_matmul_body, has_res=res is not None, want_h=want_h),
        grid=(m // tm, n // tn),
        in_specs=in_specs,
        out_specs=out_specs,
        out_shape=out_shape,
        scratch_shapes=[pltpu.VMEM((tm, d), BF16)],
        compiler_params=pltpu.CompilerParams(
            dimension_semantics=("parallel", "arbitrary")),
        name="norm_matmul",
    )(*args)


def dilated_bias_tile(rel_bias_a, window, dil):
    n = window // dil
    assert n == DIL_BLOCK
    bucket = _t5_bucket_table(window + 1)
    period = 4 * n
    u = np.arange(period)
    valid = u <= n
    w = jnp.where(jnp.asarray(valid)[None],
                  rel_bias_a.T[:, bucket[np.where(valid, n - u, 0) * dil]], NEG_INF).astype(F32)
    rep = jnp.broadcast_to(w[:, None, :], (A_HEADS, n, period))
    flat = rep.reshape(A_HEADS, n * period)[:, :n * (period - 1)]
    return flat.reshape(A_HEADS, n, period - 1)[:, :, :2 * n]


def _dilated_fused_body(q_ref, k_ref, v_ref, b_ref, o_ref, qf, kf, vf, m_scr, acc_scr, *, seq):
    qf[...] = q_ref[...].astype(F32) * QK_SCALE
    kf[...] = k_ref[...].astype(F32)
    vf[...] = v_ref[...].astype(F32)
    m_scr[...] = jnp.full(m_scr.shape, NEG_INF, F32)
    acc_scr[...] = jnp.zeros(acc_scr.shape, F32)
    n = DIL_BLOCK
    col = lax.broadcasted_iota(I32, (n, 2 * n), 1)
    lane = lax.broadcasted_iota(I32, (1, LANES), 1)
    own = [(lane < HEAD_DIM) == (a == 0) for a in range(2)]
    for bi, (_, dil) in enumerate(DILATED_BRANCHES):
        span = dil * n

        def class_body(r, carry, bi=bi, dil=dil, span=span):
            def block_body(i2, carry2):
                loaded = []
                for u in range(2):
                    i = 2 * i2 + u
                    start = r + span * i
                    rows = pl.ds(start, n, stride=dil)
                    prev = pl.ds(jnp.maximum(start - span, r), n, stride=dil)
                    q = qf[rows, :].astype(BF16)
                    k = jnp.concatenate([kf[prev, :], kf[rows, :]], axis=0).astype(BF16)
                    v = jnp.concatenate([vf[prev, :], vf[rows, :]], axis=0).astype(BF16)
                    has_prev = jnp.logical_or(col >= n, i > 0)
                    state = [(m_scr[a, rows, :], acc_scr[a, rows, :]) for a in range(2)]
                    loaded.append((rows, q, k, v, has_prev, state))
                results = []
                for rows, q, k, v, has_prev, state in loaded:
                    for a in range(2):
                        m_prev, acc_prev = state[a]
                        s = lax.dot_general(jnp.where(own[a], q, jnp.zeros_like(q)), k,
                                            (((1,), (1,)), ((), ())), preferred_element_type=F32)
                        s = jnp.where(has_prev, s + b_ref[bi, a], NEG_INF)
                        m_new = jnp.maximum(m_prev, jnp.max(s, axis=-1, keepdims=True))
                        alpha = jnp.exp(m_prev - m_new)
                        pb = jnp.exp(s - jnp.concatenate([m_new, m_new], axis=1)).astype(BF16)
                        acc_new = alpha * acc_prev + jnp.dot(
                            pb, jnp.where(own[a], v, jnp.ones_like(v)), preferred_element_type=F32)
                        results.append((a, rows, m_new, acc_new))
                for a, rows, m_new, acc_new in results:
                    m_scr[a, rows, :] = m_new
                    acc_scr[a, rows, :] = acc_new
                return carry2

            lax.fori_loop(0, seq // span // 2, block_body, 0)
            return carry

        lax.fori_loop(0, dil, class_body, 0)
    r = [acc_scr[a] / pltpu.roll(acc_scr[a], HEAD_DIM, 1) for a in range(2)]
    o_ref[...] = jnp.where(own[0], r[0], r[1]).astype(o_ref.dtype)


def dilated_attention(p, bias_tiles, batch, seq):
    nk = A_WIDTH // LANES
    blk = (None, seq, LANES)
    assert all(seq % (2 * dil * DIL_BLOCK) == 0 for _, dil in DILATED_BRANCHES)
    return pl.pallas_call(
        functools.partial(_dilated_fused_body, seq=seq),
        grid=(batch, A_HEADS // 2),
        in_specs=[pl.BlockSpec(blk, lambda b, h: (b, 0, h)),
                  pl.BlockSpec(blk, lambda b, h: (b, 0, nk + h)),
                  pl.BlockSpec(blk, lambda b, h: (b, 0, 2 * nk + h)),
                  pl.BlockSpec((len(DILATED_BRANCHES), 2, DIL_BLOCK, 2 * DIL_BLOCK),
                               lambda b, h: (0, h, 0, 0))],
        out_specs=pl.BlockSpec(blk, lambda b, h: (b, 0, h)),
        out_shape=jax.ShapeDtypeStruct((batch, seq, A_WIDTH), BF16),
        scratch_shapes=[pltpu.VMEM((seq, LANES), F32)] * 3 + [
            pltpu.VMEM((2, seq, LANES), F32), pltpu.VMEM((2, seq, LANES), F32)],
        compiler_params=pltpu.CompilerParams(dimension_semantics=("parallel", "parallel")),
        name="dilated_attention",
    )(p, p, p, bias_tiles)


def _diff_body(q_ref, k_ref, v_ref, b_ref, lam_ref, g_ref, o_ref,
               m_scr, l_scr, acc_scr, *, n_tiles, lam_init):
    t = ATT_BLOCK
    qi = pl.program_id(2)
    q = q_ref[...] * QK_SCALE
    lane = lax.broadcasted_iota(I32, (1, LANES), 1)
    qa = [jnp.where((lane < DIFF_HALF) == (a == 0), q, jnp.zeros_like(q)) for a in range(2)]
    ones = jnp.ones((t, LANES), BF16)
    for a in range(2):
        m_scr[a] = jnp.full((t, LANES), NEG_INF, F32)
        l_scr[a] = jnp.zeros((t, LANES), F32)
        acc_scr[a] = jnp.zeros((t, DIFF_VDIM), F32)

    def step(j, masked):
        off = pl.multiple_of(j * t, t)
        ks = k_ref[pl.ds(off, t), :]
        vs = v_ref[pl.ds(off, t), :]
        bias = b_ref[jnp.minimum(qi - j, n_tiles - 1)]
        if masked:
            row = lax.broadcasted_iota(I32, (t, t), 0)
            col = lax.broadcasted_iota(I32, (t, t), 1)
            causal = row >= col
        for a in range(2):
            s = lax.dot_general(qa[a], ks, (((1,), (1,)), ((), ())),
                                preferred_element_type=F32) + bias
            if masked:
                s = jnp.where(causal, s, NEG_INF)
            m_prev = m_scr[a]
            m_new = jnp.maximum(m_prev, jnp.max(s, axis=-1, keepdims=True))
            alpha = jnp.exp(m_prev - m_new)
            pb = jnp.exp(s - jnp.concatenate([m_new] * (t // LANES), axis=1)).astype(BF16)
            l_scr[a] = alpha * l_scr[a] + jnp.dot(pb, ones, preferred_element_type=F32)
            acc_scr[a] = alpha * acc_scr[a] + jnp.dot(pb, vs, preferred_element_type=F32)
            m_scr[a] = m_new

    def loop_body(j, carry):
        step(j, False)
        return carry

    lax.fori_loop(0, qi, loop_body, 0)
    step(qi, True)

    lp = lam_ref[...]
    lam = (jnp.exp(jnp.sum(lp[0:1] * lp[1:2])) - jnp.exp(jnp.sum(lp[2:3] * lp[3:4]))
           + lam_init)
    o = acc_scr[0] / l_scr[0] - lam * (acc_scr[1] / l_scr[1])
    ms = jnp.mean(o * o, axis=-1, keepdims=True)
    y = o * lax.rsqrt(ms + RMS_EPS) * g_ref[...]
    o_ref[...] = (y * (1.0 - lam_init)).astype(o_ref.dtype)


def diff_attention(p, bias_tiles, lam_params, ln_g, lam_init, batch, seq):
    t = ATT_BLOCK
    n_tiles = bias_tiles.shape[1]
    cq = 3 * A_WIDTH // LANES
    ck = cq + B_QK_WIDTH // LANES
    cv = ck + B_QK_WIDTH // LANES
    return pl.pallas_call(
        functools.partial(_diff_body, n_tiles=n_tiles, lam_init=lam_init),
        grid=(batch, B_HEADS, seq // t),
        in_specs=[
            pl.BlockSpec((None, t, LANES), lambda b, h, i: (b, i, cq + h)),
            pl.BlockSpec((None, seq, LANES), lambda b, h, i: (b, 0, ck + h)),
            pl.BlockSpec((None, seq, LANES), lambda b, h, i: (b, 0, cv + h)),
            pl.BlockSpec((None, n_tiles, t, t), lambda b, h, i: (h, 0, 0, 0)),
            pl.BlockSpec((4, DIFF_HALF), lambda b, h, i: (0, 0)),
            pl.BlockSpec((1, DIFF_VDIM), lambda b, h, i: (0, 0)),
        ],
        out_specs=pl.BlockSpec((None, t, LANES), lambda b, h, i: (b, i, h)),
        out_shape=jax.ShapeDtypeStruct((batch, seq, B_V_WIDTH), BF16),
        scratch_shapes=[pltpu.VMEM((2, t, LANES), F32), pltpu.VMEM((2, t, LANES), F32),
                        pltpu.VMEM((2, t, DIFF_VDIM), F32)],
        compiler_params=pltpu.CompilerParams(
            dimension_semantics=("parallel", "parallel", "arbitrary")),
        name="diff_attention",
    )(p, p, p, bias_tiles, lam_params, ln_g.reshape(1, DIFF_VDIM))


def diff_bias_tiles(rel_bias_b, seq):
    t = ATT_BLOCK
    bucket = _t5_bucket_table(max(seq, 2 * MAX_DISTANCE) + 2 * t)
    sat = bucket[-1]
    d_sat = int(np.max(np.nonzero(bucket != sat)[0])) + 1
    n_full = (d_sat + t - 1 + t - 1) // t
    n_tiles = n_full + 1
    assert n_full * t - (t - 1) >= d_sat
    s = LANES
    nb = t // s
    deltas = np.arange(-(nb - 1), nb * n_tiles)
    n = np.arange(2 * s)[None, :]
    base = deltas[:, None] * s
    dist = np.clip(np.where(n < s, base - n, base + 2 * s - n), 0, None)
    w = rel_bias_b.T[:, bucket[dist]].astype(F32)
    rep = jnp.broadcast_to(w[:, :, None, :], (B_HEADS, len(deltas), s, 2 * s))
    flat = rep.reshape(B_HEADS, len(deltas), 2 * s * s)[:, :, :s * (2 * s - 1)]
    sub = flat.reshape(B_HEADS, len(deltas), s, 2 * s - 1)[:, :, :, :s]
    tiles = [jnp.concatenate(
        [jnp.concatenate([sub[:, d * nb + bi - bj + nb - 1] for bj in range(nb)], axis=-1)
         for bi in range(nb)], axis=-2) for d in range(n_tiles)]
    return jnp.stack(tiles, axis=1)


def _logsig_cumsum_body(f_ref, b_ref, c_ref, carry_scr):
    t = f_ref.shape[0]

    @pl.when(pl.program_id(1) == 0)
    def _():
        carry_scr[...] = jnp.zeros_like(carry_scr)

    x = f_ref[...] + b_ref[...]
    ls = jnp.minimum(x, 0.0) - jnp.log1p(jnp.exp(-jnp.abs(x)))
    row = lax.broadcasted_iota(I32, (t, t), 0)
    col = lax.broadcasted_iota(I32, (t, t), 1)
    tri = (row >= col).astype(F32)
    c = jnp.dot(tri, ls, preferred_element_type=F32,
                precision=lax.Precision.HIGHEST) + carry_scr[...]
    c_ref[...] = c
    carry_scr[...] = c[t - 1:t, :]


def logsig_cumsum(fg, b_f, batch, seq, t=512):
    return pl.pallas_call(
        _logsig_cumsum_body,
        grid=(batch, seq // t),
        in_specs=[pl.BlockSpec((None, t, LANES), lambda b, i: (b, i, 0)),
                  pl.BlockSpec((1, LANES), lambda b, i: (0, 0))],
        out_specs=pl.BlockSpec((None, t, LANES), lambda b, i: (b, i, 0)),
        out_shape=jax.ShapeDtypeStruct((batch, seq, LANES), F32),
        scratch_shapes=[pltpu.VMEM((1, LANES), F32)],
        compiler_params=pltpu.CompilerParams(
            dimension_semantics=("parallel", "arbitrary")),
        name="logsig_cumsum",
    )(fg, b_f)


def _fox_body(q_ref, k_ref, v_ref, cq_ref, ck_ref, o_ref, m_scr, acc_scr):
    t = ATT_BLOCK
    qi = pl.program_id(2)
    q = q_ref[...] * QK_SCALE
    lane = lax.broadcasted_iota(I32, (1, LANES), 1)
    own = [(lane < HEAD_DIM) == (a == 0) for a in range(2)]
    qa = [jnp.where(own[a], q, jnp.zeros_like(q)) for a in range(2)]
    cqb = [jnp.broadcast_to(cq_ref[a], (t, LANES)) for a in range(2)]
    for a in range(2):
        m_scr[a] = jnp.full((t, LANES), NEG_INF, F32)
        acc_scr[a] = jnp.zeros((t, LANES), F32)

    def step(j, masked):
        off = pl.multiple_of(j * t, t)
        ks = k_ref[pl.ds(off, t), :]
        vs = v_ref[pl.ds(off, t), :]
        if masked:
            row = lax.broadcasted_iota(I32, (t, t), 0)
            col = lax.broadcasted_iota(I32, (t, t), 1)
            causal = row >= col
        for a in range(2):
            s = lax.dot_general(qa[a], ks, (((1,), (1,)), ((), ())),
                                preferred_element_type=F32)
            s = s - ck_ref[a, :, pl.ds(off, t)]
            if masked:
                s = jnp.where(causal, s, NEG_INF)
            m_prev = m_scr[a]
            m_new = jnp.maximum(m_prev, jnp.max(s, axis=-1, keepdims=True) + cqb[a])
            alpha = jnp.exp(m_prev - m_new)
            shift = m_new - cqb[a]
            pb = jnp.exp(s - jnp.concatenate([shift] * (t // LANES), axis=1)).astype(BF16)
            v_aug = jnp.where(own[a], vs, jnp.ones_like(vs))
            acc_scr[a] = alpha * acc_scr[a] + jnp.dot(pb, v_aug, preferred_element_type=F32)
            m_scr[a] = m_new

    def loop_body(j, carry):
        step(j, False)
        return carry

    lax.fori_loop(0, qi, loop_body, 0)
    step(qi, True)
    r = [acc_scr[a] / pltpu.roll(acc_scr[a], HEAD_DIM, 1) for a in range(2)]
    o_ref[...] = jnp.where(own[0], r[0], r[1]).astype(o_ref.dtype)


def fox_attention(p, cq, ck, batch, seq):
    t = ATT_BLOCK
    nk = C_WIDTH // LANES
    return pl.pallas_call(
        _fox_body,
        grid=(batch, C_HEADS // 2, seq // t),
        in_specs=[
            pl.BlockSpec((None, t, LANES), lambda b, h, i: (b, i, h)),
            pl.BlockSpec((None, seq, LANES), lambda b, h, i: (b, 0, nk + h)),
            pl.BlockSpec((None, seq, LANES), lambda b, h, i: (b, 0, 2 * nk + h)),
            pl.BlockSpec((None, 2, t, 1), lambda b, h, i: (b, h, i, 0)),
            pl.BlockSpec((None, 2, 1, seq), lambda b, h, i: (b, h, 0, 0)),
        ],
        out_specs=pl.BlockSpec((None, t, LANES), lambda b, h, i: (b, i, h)),
        out_shape=jax.ShapeDtypeStruct((batch, seq, C_WIDTH), BF16),
        scratch_shapes=[pltpu.VMEM((2, t, LANES), F32), pltpu.VMEM((2, t, LANES), F32)],
        compiler_params=pltpu.CompilerParams(
            dimension_semantics=("parallel", "parallel", "arbitrary")),
        name="fox_attention",
    )(p, p, p, cq, ck)


def _concat_out_body(oa_ref, ob_ref, w_ref, x_ref, out_ref):
    out_ref[...] = (x_ref[...]
                    + jnp.dot(oa_ref[...], w_ref[:A_WIDTH, :], preferred_element_type=F32)
                    + jnp.dot(ob_ref[...], w_ref[A_WIDTH:, :], preferred_element_type=F32))


def concat_out_proj(oa, ob, w, x, tm=1024, tn=1024):
    m, d = x.shape
    return pl.pallas_call(
        _concat_out_body,
        grid=(m // tm, d // tn),
        in_specs=[pl.BlockSpec((tm, A_WIDTH), lambda i, j: (i, 0)),
                  pl.BlockSpec((tm, B_V_WIDTH), lambda i, j: (i, 0)),
                  pl.BlockSpec((A_WIDTH + B_V_WIDTH, tn), lambda i, j: (0, j)),
                  pl.BlockSpec((tm, tn), lambda i, j: (i, j))],
        out_specs=pl.BlockSpec((tm, tn), lambda i, j: (i, j)),
        out_shape=jax.ShapeDtypeStruct((m, d), F32),
        compiler_params=pltpu.CompilerParams(dimension_semantics=("parallel", "parallel")),
        name="concat_out_proj",
    )(oa, ob, w, x)


def _matmul_res_body(a_ref, w_ref, x_ref, o_ref):
    o_ref[...] = x_ref[...] + jnp.dot(a_ref[...], w_ref[...], preferred_element_type=F32)


def matmul_residual(a, w, x, tm=1024, tn=1024):
    m, k = a.shape
    n = w.shape[1]
    return pl.pallas_call(
        _matmul_res_body,
        grid=(m // tm, n // tn),
        in_specs=[pl.BlockSpec((tm, k), lambda i, j: (i, 0)),
                  pl.BlockSpec((k, tn), lambda i, j: (0, j)),
                  pl.BlockSpec((tm, tn), lambda i, j: (i, j))],
        out_specs=pl.BlockSpec((tm, tn), lambda i, j: (i, j)),
        out_shape=jax.ShapeDtypeStruct((m, n), F32),
        compiler_params=pltpu.CompilerParams(
            dimension_semantics=("parallel", "parallel")),
        name="matmul_residual",
    )(a, w, x)


def _add_norm_body(x_ref, r_ref, g_ref, o_ref):
    x = x_ref[...] + r_ref[...]
    ms = jnp.mean(x * x, axis=-1, keepdims=True)
    o_ref[...] = x * lax.rsqrt(ms + RMS_EPS) * g_ref[...]


def add_norm(x, r, g, tm=512):
    m, d = x.shape
    row = pl.BlockSpec((tm, d), lambda i: (i, 0))
    return pl.pallas_call(
        _add_norm_body,
        grid=(m // tm,),
        in_specs=[row, row, pl.BlockSpec((1, d), lambda i: (0, 0))],
        out_specs=row,
        out_shape=jax.ShapeDtypeStruct((m, d), F32),
        compiler_params=pltpu.CompilerParams(dimension_semantics=("parallel",)),
        name="add_norm",
    )(x, r, g.reshape(1, d))


def _peer_candidates():
    groups = [("a", 0, 0), ("a", 0, 8), ("a", 1, 0), ("b", 0, 8), ("a", 2, 0), ("a", 3, 0),
              ("b", 0, 0), ("b", 1, 0), ("b", 2, 0)]
    cid = np.zeros((8 * len(groups), 1), np.int32)
    seen = set()
    for g, (kind, fixed, start) in enumerate(groups):
        for r in range(8):
            a, b = (fixed, start + r) if kind == "a" else (start + r, fixed)
            row = 8 * g + r
            if (a + 1) * (b + 1) <= PEER_TOPK and (a, b) not in seen:
                seen.add((a, b))
                cid[row, 0] = a * PEER_TOPK + b
            else:
                cid[row, 0] = PEER_TOPK * PEER_TOPK + row
    assert len(seen) == sum((a + 1) * (b + 1) <= PEER_TOPK
                            for a in range(PEER_TOPK) for b in range(PEER_TOPK))
    return groups, cid


def _peer_topk_body(q_ref, sk_ref, cid_ref, idx_ref, gate_ref, ts_scr, ti_scr,
                    bs_scr, be_scr, cv_scr, ce_scr, sc_scr, *, groups):
    tm = q_ref.shape[0]
    neg_inf = jnp.float32(-jnp.inf)
    key_id = lax.broadcasted_iota(I32, (1, N_KEYS, tm), 1)
    batch = sc_scr.shape[0]

    def group_body(g, carry):
        for j in range(batch):
            pr = g * batch + j
            off = pl.multiple_of(pr * PEER_KEY_HALF, PEER_KEY_HALF)
            sc_scr[j] = lax.dot_general(sk_ref[pr], q_ref[:, pl.ds(off, PEER_KEY_HALF)],
                                        (((1,), (1,)), ((), ())), preferred_element_type=F32)
        rows = pl.ds(g * batch, batch)

        def k_body(k, c):
            vals = sc_scr[...]
            m = jnp.max(vals, axis=1, keepdims=True)
            sel = jnp.min(jnp.where(vals == m, key_id, N_KEYS), axis=1, keepdims=True)
            ts_scr[rows, pl.ds(k, 1), :] = m
            ti_scr[rows, pl.ds(k, 1), :] = sel
            sc_scr[...] = jnp.where(key_id == sel, neg_inf, vals)
            return c

        lax.fori_loop(0, PEER_TOPK, k_body, 0)
        return carry

    lax.fori_loop(0, 2 * PEER_HEADS // batch, group_body, 0)

    cand_id = cid_ref[...]
    pad = jnp.where(cand_id < PEER_TOPK * PEER_TOPK, 0.0, neg_inf)

    def pick(x1, x2):
        return jnp.concatenate(
            [x1[f:f + 1] + x2[s:s + 8] if kind == "a" else x1[s:s + 8] + x2[f:f + 1]
             for kind, f, s in groups], axis=0)

    def head_body(h, carry):
        cv_scr[h] = pick(ts_scr[2 * h], ts_scr[2 * h + 1]) + pad
        ce_scr[h] = pick(ti_scr[2 * h] * N_KEYS, ti_scr[2 * h + 1])
        return carry

    lax.fori_loop(0, PEER_HEADS, head_body, 0)

    cid3 = cand_id[None]

    def k_body(k, carry):
        vals = cv_scr[...]
        m = jnp.max(vals, axis=1, keepdims=True)
        sel = jnp.min(jnp.where(vals == m, cid3, PEER_TOPK * PEER_TOPK), axis=1, keepdims=True)
        hit = cid3 == sel
        bs_scr[:, pl.ds(k, 1), :] = m
        be_scr[:, pl.ds(k, 1), :] = jnp.sum(jnp.where(hit, ce_scr[...], 0), axis=1, keepdims=True)
        cv_scr[...] = jnp.where(hit, neg_inf, vals)
        return carry

    lax.fori_loop(0, PEER_TOPK, k_body, 0)
    bs = bs_scr[...]
    e = jnp.exp(bs - jnp.max(bs, axis=1, keepdims=True))
    gate = e / jnp.sum(e, axis=1, keepdims=True)
    n_sel = PEER_HEADS * PEER_TOPK
    gate_ref[...] = gate.reshape(n_sel, tm).T
    idx_ref[...] = be_scr[...].reshape(n_sel, tm).T


def peer_topk(q, subkeys, tm=256):
    m = q.shape[0]
    n_sel = PEER_HEADS * PEER_TOPK
    groups, cid = _peer_candidates()
    out_spec = pl.BlockSpec((tm, n_sel), lambda i: (i, 0))
    return pl.pallas_call(
        functools.partial(_peer_topk_body, groups=groups),
        grid=(m // tm,),
        in_specs=[pl.BlockSpec((tm, q.shape[1]), lambda i: (i, 0)),
                  pl.BlockSpec(subkeys.shape, lambda i: (0, 0, 0)),
                  pl.BlockSpec(cid.shape, lambda i: (0, 0))],
        out_specs=[out_spec, out_spec],
        out_shape=[jax.ShapeDtypeStruct((m, n_sel), I32),
                   jax.ShapeDtypeStruct((m, n_sel), F32)],
        scratch_shapes=[pltpu.VMEM((2 * PEER_HEADS, PEER_TOPK, tm), F32),
                        pltpu.VMEM((2 * PEER_HEADS, PEER_TOPK, tm), I32),
                        pltpu.VMEM((PEER_HEADS, PEER_TOPK, tm), F32),
                        pltpu.VMEM((PEER_HEADS, PEER_TOPK, tm), I32),
                        pltpu.VMEM((PEER_HEADS, cid.shape[0], tm), F32),
                        pltpu.VMEM((PEER_HEADS, cid.shape[0], tm), I32),
                        pltpu.VMEM((PEER_HEADS, N_KEYS, tm), F32)],
        compiler_params=pltpu.CompilerParams(dimension_semantics=("parallel",)),
        name="peer_topk",
    )(q, subkeys, jnp.asarray(cid))


def _gelu_gate_body(a_ref, g_ref, o_ref):
    a = a_ref[...]
    o_ref[...] = g_ref[...] * (0.5 * a * (1.0 + lax.erf(a * (2.0 ** -0.5))))


def gelu_gate(act, gate, tm=2048):
    m, n = act.shape
    spec = pl.BlockSpec((tm, n), lambda i: (i, 0))
    return pl.pallas_call(
        _gelu_gate_body,
        grid=(m // tm,),
        in_specs=[spec, spec],
        out_specs=spec,
        out_shape=jax.ShapeDtypeStruct((m, n), F32),
        compiler_params=pltpu.CompilerParams(dimension_semantics=("parallel",)),
        name="gelu_gate",
    )(act, gate)


SC_TOK_CHUNK = 32
SC_DOTS_CHUNK = 64
SC_RING = 8
SC_BF16_GROUP = 4
SC_FMT = plsc.PackFormat.INTERLEAVED


def _sc_worker_id():
    return lax.axis_index("s") * SC_CORES + lax.axis_index("c")


def pack_bf16_pairs(t):
    half = t.shape[-1] // 2
    bits = lax.bitcast_convert_type(t.astype(BF16).astype(F32), I32)
    return (bits[..., half:] & jnp.int32(-65536)) | lax.shift_right_logical(
        bits[..., :half], jnp.int32(16))


def _sc_row_pipeline(idx_v, table_hbm, rows_v, sems, n_items, groups, compute):
    def gather(item):
        tt, g = item // groups, item % groups
        ids = idx_v[tt, pl.ds(g * SC_LANES, SC_LANES)]
        slot = item % SC_RING
        return pltpu.make_async_copy(table_hbm.at[ids], rows_v.at[slot], sems.at[slot])

    for s in range(SC_RING - 1):
        gather(s).start()

    def item_body(item, carry):
        nxt = item + SC_RING - 1

        @pl.when(nxt < n_items)
        def _():
            gather(nxt).start()

        gather(item).wait()
        compute(item // groups, item % groups, item % SC_RING)
        return carry

    lax.fori_loop(0, n_items, item_body, 0)


def peer_expert_dots(hp, idx, up):
    m, dw = hp.shape
    n_sel = idx.shape[1]
    per_w = m // SC_WORKERS
    chunk = SC_DOTS_CHUNK
    n_chunks = per_w // chunk
    groups = n_sel // SC_LANES
    step = SC_BF16_GROUP * SC_LANES
    mesh = plsc.VectorSubcoreMesh(core_axis_name="c", subcore_axis_name="s")

    @functools.partial(
        pl.kernel, mesh=mesh,
        out_type=jax.ShapeDtypeStruct((m, n_sel), F32),
        scratch_types=[
            pltpu.VMEM((chunk, n_sel), I32),
            pltpu.VMEM((chunk, dw), I32),
            pltpu.VMEM((chunk, n_sel), F32),
            pltpu.VMEM((SC_RING, SC_LANES, dw), I32),
            pltpu.VMEM((SC_LANES * SC_LANES,), F32),
            pltpu.SemaphoreType.DMA((SC_RING,)),
        ],
        compiler_params=pltpu.CompilerParams(needs_layout_passes=False),
        name="peer_expert_dots",
    )
    def k(h_hbm, idx_hbm, u_hbm, act_hbm, idx_v, h_v, act_v, rows_v, part_v, sems):
        base = _sc_worker_id() * per_w
        lane = lax.broadcasted_iota(I32, (SC_LANES,), 0)

        def compute(tt, g, slot):
            def grp_body(q, accs):
                off = pl.multiple_of(q * step, step)
                xs = [plsc.bitcast(h_v[tt, pl.ds(off + c * SC_LANES, SC_LANES)], BF16)
                      for c in range(SC_BF16_GROUP)]
                new = []
                for e in range(SC_LANES):
                    ps = [plsc.bitcast(
                        rows_v[slot, e, pl.ds(off + c * SC_LANES, SC_LANES)], BF16) * xs[c]
                        for c in range(SC_BF16_GROUP)]
                    while len(ps) > 1:
                        ps = [ps[i] + ps[i + 1] for i in range(0, len(ps), 2)]
                    lo, hi = plsc.unpack(ps[0], format=SC_FMT)
                    new.append(accs[e] + (lo + hi))
                return tuple(new)

            accs = lax.fori_loop(
                0, dw // step, grp_body,
                tuple(jnp.zeros((SC_LANES,), F32) for _ in range(SC_LANES)))
            for e in range(SC_LANES):
                part_v[pl.ds(e * SC_LANES, SC_LANES)] = accs[e]
            tot = jnp.zeros((SC_LANES,), F32)
            for l in range(SC_LANES):
                tot = tot + plsc.load_gather(part_v, [lane * SC_LANES + l])
            act_v[tt, pl.ds(g * SC_LANES, SC_LANES)] = tot

        def chunk_body(c, carry):
            t0 = base + c * chunk
            pltpu.sync_copy(idx_hbm.at[pl.ds(t0, chunk)], idx_v)
            pltpu.sync_copy(h_hbm.at[pl.ds(t0, chunk)], h_v)
            _sc_row_pipeline(idx_v, u_hbm, rows_v, sems, chunk * groups, groups, compute)
            pltpu.sync_copy(act_v, act_hbm.at[pl.ds(t0, chunk)])
            return carry

        lax.fori_loop(0, n_chunks, chunk_body, 0)

    return k(hp, idx, up)


def peer_expert_combine(w, idx, vp):
    m, n_sel = w.shape
    dw = vp.shape[1]
    d = 2 * dw
    per_w = m // SC_WORKERS
    n_chunks = per_w // SC_TOK_CHUNK
    n_vec = d // SC_LANES
    groups = n_sel // SC_LANES
    mesh = plsc.VectorSubcoreMesh(core_axis_name="c", subcore_axis_name="s")

    @functools.partial(
        pl.kernel, mesh=mesh,
        out_type=jax.ShapeDtypeStruct((m, d), F32),
        scratch_types=[
            pltpu.VMEM((SC_TOK_CHUNK, n_sel), I32),
            pltpu.VMEM((SC_TOK_CHUNK, n_sel), F32),
            pltpu.VMEM((SC_TOK_CHUNK, d), F32),
            pltpu.VMEM((SC_RING, SC_LANES, dw), I32),
            pltpu.SemaphoreType.DMA((SC_RING,)),
        ],
        compiler_params=pltpu.CompilerParams(needs_layout_passes=False),
        name="peer_expert_combine",
    )
    def k(w_hbm, idx_hbm, v_hbm, out_hbm, idx_v, w_v, out_v, rows_v, sems):
        base = _sc_worker_id() * per_w

        def compute(tt, g, slot):
            splat = []
            for e in range(SC_LANES):
                s = plsc.load_gather(w_v, [jnp.full((SC_LANES,), tt, I32),
                                           jnp.full((SC_LANES,), g * SC_LANES + e, I32)])
                splat.append(plsc.pack(s, s, format=SC_FMT))

            @plsc.parallel_loop(0, dw // SC_LANES)
            def _(j):
                off = pl.multiple_of(j * SC_LANES, SC_LANES)
                acc_lo = out_v[tt, pl.ds(off, SC_LANES)]
                acc_hi = out_v[tt, pl.ds(dw + off, SC_LANES)]
                for e0 in range(0, SC_LANES, SC_BF16_GROUP):
                    s = None
                    for e in range(e0, e0 + SC_BF16_GROUP):
                        p = plsc.bitcast(rows_v[slot, e, pl.ds(off, SC_LANES)], BF16) * splat[e]
                        s = p if s is None else s + p
                    lo, hi = plsc.unpack(s, format=SC_FMT)
                    acc_lo = acc_lo + lo
                    acc_hi = acc_hi + hi
                out_v[tt, pl.ds(off, SC_LANES)] = acc_lo
                out_v[tt, pl.ds(dw + off, SC_LANES)] = acc_hi

        def chunk_body(c, carry):
            t0 = base + c * SC_TOK_CHUNK
            pltpu.sync_copy(idx_hbm.at[pl.ds(t0, SC_TOK_CHUNK)], idx_v)
            pltpu.sync_copy(w_hbm.at[pl.ds(t0, SC_TOK_CHUNK)], w_v)

            def zero_body(z, carry2):
                tt, j = z // n_vec, z % n_vec
                out_v[tt, pl.ds(pl.multiple_of(j * SC_LANES, SC_LANES), SC_LANES)] = (
                    jnp.zeros((SC_LANES,), F32))
                return carry2

            lax.fori_loop(0, SC_TOK_CHUNK * n_vec, zero_body, 0)
            _sc_row_pipeline(idx_v, v_hbm, rows_v, sems, SC_TOK_CHUNK * groups, groups, compute)
            pltpu.sync_copy(out_v, out_hbm.at[pl.ds(t0, SC_TOK_CHUNK)])
            return carry

        lax.fori_loop(0, n_chunks, chunk_body, 0)

    return k(w, idx, vp)


def kernel(x, norm_mix_g, norm_ffn_g, final_norm_g, rel_bias, even_w_in, even_w_out,
           diff_lambda, diff_ln_g, odd_w_in, odd_b_f, odd_w_out, peer_wq, peer_subkeys,
           peer_u, peer_v):
    batch, seq, d = x.shape

    dil_tiles = jnp.stack(
        [dilated_bias_tile(rel_bias[:, :A_HEADS], w, dl) for w, dl in DILATED_BRANCHES])
    diff_tiles = diff_bias_tiles(rel_bias[:, A_HEADS:], seq)
    lam_init = 0.8 - 0.6 * math.exp(-0.3 * 0)
    even_in, even_out = even_w_in[0].astype(BF16), even_w_out[0].astype(BF16)
    w_in = odd_w_in[0]
    odd_in, odd_out = w_in[:, :3 * C_WIDTH].astype(BF16), odd_w_out[0].astype(BF16)
    w_gate = jnp.pad(w_in[:, 3 * C_WIDTH:], ((0, 0), (0, LANES - C_HEADS)))
    b_f = jnp.pad(odd_b_f[0], (0, LANES - C_HEADS)).reshape(1, LANES)
    peer = [dict(wq=peer_wq[l].astype(BF16),
                 sk=peer_subkeys[l].reshape(2 * PEER_HEADS, N_KEYS, PEER_KEY_HALF).astype(BF16))
            for l in range(2)]
    packed_tables = {}

    def expert_table(table, layer, anchor):
        if (id(table), layer) not in packed_tables:
            rows, _ = lax.optimization_barrier((table[layer], anchor))
            packed_tables[(id(table), layer)] = pack_bf16_pairs(rows)
        return packed_tables[(id(table), layer)]

    assert batch % BATCH_GROUPS == 0
    bg = batch // BATCH_GROUPS
    m = bg * seq
    groups = [dict(g=g) for g in range(BATCH_GROUPS)]

    def peer_select(st, layer, after=None, parts=1):
        rows = m // parts
        pieces = []
        for part in range(parts):
            xs = st["x"] if parts == 1 else st["x"][part * rows:(part + 1) * rows]
            q, hp = norm_matmul(xs, norm_ffn_g[layer], peer[layer]["wq"], want_h="packed")
            idx, gate = peer_topk(q, peer[layer]["sk"])
            if after is not None and after["st"] is not st:
                after["st"]["peer"], idx = lax.optimization_barrier((after["st"]["peer"], idx))
            pieces.append((idx, gate,
                           peer_expert_dots(hp, idx, expert_table(peer_u, layer, st["x"]))))
        idx, gate, act = (jnp.concatenate(c, axis=0) if parts > 1 else c[0] for c in zip(*pieces))
        return dict(st=st, layer=layer, idx=idx, gate=gate, act=act)

    def peer_combine(rec):
        rec["w"] = gelu_gate(rec["act"], rec["gate"])
        rec["st"]["peer"] = peer_expert_combine(
            rec["w"], rec["idx"], expert_table(peer_v, rec["layer"], rec["act"]))

    def even_layer(st, tie):
        (p,) = norm_matmul(st["x"], norm_mix_g[0], even_in)
        p3 = p.reshape(bg, seq, -1)
        oa = dilated_attention(p3, dil_tiles, bg, seq)
        ob = tie(diff_attention(p3, diff_tiles, diff_lambda[0], diff_ln_g[0], lam_init, bg, seq))
        st["x"] = concat_out_proj(oa.reshape(m, A_WIDTH), ob.reshape(m, B_V_WIDTH), even_out,
                                  st["x"])

    def odd_layer(st, tie):
        p, st["x"], fg = norm_matmul(st["x"], norm_mix_g[1], odd_in, res=st["peer"],
                                     want_h="gate", gate_w=w_gate)
        c = logsig_cumsum(fg.reshape(bg, seq, LANES), b_f, bg, seq)
        ct = c[:, :, :C_HEADS].transpose(0, 2, 1)
        o = tie(fox_attention(p.reshape(bg, seq, -1), ct[:, :, :, None], ct[:, :, None, :],
                              bg, seq))
        st["x"] = matmul_residual(o.reshape(m, C_WIDTH), odd_out, st["x"])

    stages = [(even_layer, st, 0) for st in groups] + [(odd_layer, st, 1) for st in groups]
    recs = []
    for k, (mixer, st, layer) in enumerate(stages):
        def tie(t, prev=recs[k - 1] if k >= 1 else None):
            if prev is None:
                return t
            prev["act"], t = lax.optimization_barrier((prev["act"], t))
            peer_combine(prev)
            prev["w"], t = lax.optimization_barrier((prev["w"], t))
            return t

        if "x" not in st:
            xin = x if k == 0 else lax.optimization_barrier((x, recs[k - 1]["idx"]))[0]
            st["x"] = xin[st["g"] * bg:(st["g"] + 1) * bg].reshape(m, d)
        mixer(st, tie)
        recs.append(peer_select(st, layer, after=recs[max(k - 2, 0)] if k >= 1 else None,
                                parts=2 if k == 0 else 1))
    peer_combine(recs[-1])
    outs = [add_norm(st["x"], st["peer"], final_norm_g).reshape(bg, seq, d) for st in groups]
    return jnp.concatenate(outs, axis=0)
```

```python
import functools
import math

import numpy as np
import jax
import jax.numpy as jnp
from jax import lax
from jax.experimental import pallas as pl
from jax.experimental.pallas import tpu as pltpu
from jax.experimental.pallas import tpu_sc as plsc

F32 = jnp.float32
BF16 = jnp.bfloat16
I32 = jnp.int32

HEAD_DIM = 64
A_HEADS = 8
DILATED_BRANCHES = ((128, 1), (512, 4), (2048, 16))
DIFF_HALF = 64
DIFF_VDIM = 128
B_HEADS = 4
C_HEADS = 16
N_BUCKETS = 32
MAX_DISTANCE = 2048
PEER_HEADS = 8
N_KEYS = 128
PEER_TOPK = 16
PEER_KEY_HALF = 128
RMS_EPS = 1e-6
NEG_INF = -1e30
A_WIDTH = A_HEADS * HEAD_DIM
B_QK_WIDTH = B_HEADS * 2 * DIFF_HALF
B_V_WIDTH = B_HEADS * DIFF_VDIM
C_WIDTH = C_HEADS * HEAD_DIM
QK_SCALE = 0.125
LANES = 128
DIL_BLOCK = 128
ATT_BLOCK = 512
BATCH_GROUPS = 8

SC_CORES = 2
SC_SUBCORES = 16
SC_LANES = 16
SC_WORKERS = SC_CORES * SC_SUBCORES


def _t5_bucket_table(n):
    max_exact = N_BUCKETS // 2
    d = np.arange(n)
    df = np.maximum(d, 1).astype(np.float32)
    large = max_exact + (
        np.log(df / np.float32(max_exact)) / np.float32(math.log(MAX_DISTANCE / max_exact))
        * np.float32(N_BUCKETS - max_exact)).astype(np.int32)
    large = np.minimum(large, N_BUCKETS - 1)
    return np.where(d < max_exact, d, large).astype(np.int32)


def _norm_matmul_body(*refs, has_res, want_h):
    it = iter(refs)
    x_ref = next(it)
    r_ref = next(it) if has_res else None
    g_ref = next(it)
    w_ref = next(it)
    gw_ref = next(it) if want_h == "gate" else None
    o_ref = next(it)
    xs_ref = next(it) if has_res else None
    hout_ref = next(it) if want_h else None
    h_scr = next(it)

    @pl.when(pl.program_id(1) == 0)
    def _():
        x = x_ref[...]
        if has_res:
            x = x + r_ref[...]
            xs_ref[...] = x
        ms = jnp.mean(x * x, axis=-1, keepdims=True)
        h = x * lax.rsqrt(ms + RMS_EPS) * g_ref[...]
        if want_h == "gate":
            hout_ref[...] = jnp.dot(h, gw_ref[...], preferred_element_type=F32,
                                    precision=lax.Precision.HIGHEST)
        elif want_h == "packed":
            hout_ref[...] = pack_bf16_pairs(h)
        h_scr[...] = h.astype(BF16)

    o_ref[...] = jnp.dot(h_scr[...], w_ref[...],
                         preferred_element_type=F32).astype(o_ref.dtype)


def norm_matmul(x, g, w, *, res=None, want_h=None, gate_w=None, out_dtype=BF16,
                tm=1024, tn=1024):
    m, d = x.shape
    n = w.shape[1]
    tn = min(tn, n)
    row = pl.BlockSpec((tm, d), lambda i, j: (i, 0))
    in_specs = [row] + ([row] if res is not None else []) + [
        pl.BlockSpec((1, d), lambda i, j: (0, 0)),
        pl.BlockSpec((d, tn), lambda i, j: (0, j))]
    out_specs = [pl.BlockSpec((tm, tn), lambda i, j: (i, j))]
    out_shape = [jax.ShapeDtypeStruct((m, n), out_dtype)]
    if res is not None:
        out_specs.append(row)
        out_shape.append(jax.ShapeDtypeStruct((m, d), F32))
    if want_h == "gate":
        in_specs.append(pl.BlockSpec(gate_w.shape, lambda i, j: (0, 0)))
        out_specs.append(pl.BlockSpec((tm, gate_w.shape[1]), lambda i, j: (i, 0)))
        out_shape.append(jax.ShapeDtypeStruct((m, gate_w.shape[1]), F32))
    elif want_h == "packed":
        out_specs.append(pl.BlockSpec((tm, d // 2), lambda i, j: (i, 0)))
        out_shape.append(jax.ShapeDtypeStruct((m, d // 2), I32))
    args = [x] + ([res] if res is not None else []) + [g.reshape(1, d), w] + (
        [gate_w] if want_h == "gate" else [])
    return pl.pallas_call(
        functools.partial(_norm_matmul_body, has_res=res is not None, want_h=want_h),
        grid=(m // tm, n // tn),
        in_specs=in_specs,
        out_specs=out_specs,
        out_shape=out_shape,
        scratch_shapes=[pltpu.VMEM((tm, d), BF16)],
        compiler_params=pltpu.CompilerParams(
            dimension_semantics=("parallel", "arbitrary")),
        name="norm_matmul",
    )(*args)


def dilated_bias_tile(rel_bias_a, window, dil):
    n = window // dil
    assert n == DIL_BLOCK
    bucket = _t5_bucket_table(window + 1)
    period = 4 * n
    u = np.arange(period)
    valid = u <= n
    w = jnp.where(jnp.asarray(valid)[None],
                  rel_bias_a.T[:, bucket[np.where(valid, n - u, 0) * dil]], NEG_INF).astype(F32)
    rep = jnp.broadcast_to(w[:, None, :], (A_HEADS, n, period))
    flat = rep.reshape(A_HEADS, n * period)[:, :n * (period - 1)]
    return flat.reshape(A_HEADS, n, period - 1)[:, :, :2 * n]


def _dilated_fused_body(q_ref, k_ref, v_ref, b_ref, o_ref, qf, kf, vf, m_scr, acc_scr, *, seq):
    qf[...] = q_ref[...].astype(F32) * QK_SCALE
    kf[...] = k_ref[...].astype(F32)
    vf[...] = v_ref[...].astype(F32)
    m_scr[...] = jnp.full(m_scr.shape, NEG_INF, F32)
    acc_scr[...] = jnp.zeros(acc_scr.shape, F32)
    n = DIL_BLOCK
    col = lax.broadcasted_iota(I32, (n, 2 * n), 1)
    lane = lax.broadcasted_iota(I32, (1, LANES), 1)
    own = [(lane < HEAD_DIM) == (a == 0) for a in range(2)]
    for bi, (_, dil) in enumerate(DILATED_BRANCHES):
        span = dil * n

        def class_body(r, carry, bi=bi, dil=dil, span=span):
            def block_body(i2, carry2):
                loaded = []
                for u in range(2):
                    i = 2 * i2 + u
                    start = r + span * i
                    rows = pl.ds(start, n, stride=dil)
                    prev = pl.ds(jnp.maximum(start - span, r), n, stride=dil)
                    q = qf[rows, :].astype(BF16)
                    k = jnp.concatenate([kf[prev, :], kf[rows, :]], axis=0).astype(BF16)
                    v = jnp.concatenate([vf[prev, :], vf[rows, :]], axis=0).astype(BF16)
                    has_prev = jnp.logical_or(col >= n, i > 0)
                    state = [(m_scr[a, rows, :], acc_scr[a, rows, :]) for a in range(2)]
                    loaded.append((rows, q, k, v, has_prev, state))
                results = []
                for rows, q, k, v, has_prev, state in loaded:
                    for a in range(2):
                        m_prev, acc_prev = state[a]
                        s = lax.dot_general(jnp.where(own[a], q, jnp.zeros_like(q)), k,
                                            (((1,), (1,)), ((), ())), preferred_element_type=F32)
                        s = jnp.where(has_prev, s + b_ref[bi, a], NEG_INF)
                        m_new = jnp.maximum(m_prev, jnp.max(s, axis=-1, keepdims=True))
                        alpha = jnp.exp(m_prev - m_new)
                        pb = jnp.exp(s - jnp.concatenate([m_new, m_new], axis=1)).astype(BF16)
                        acc_new = alpha * acc_prev + jnp.dot(
                            pb, jnp.where(own[a], v, jnp.ones_like(v)), preferred_element_type=F32)
                        results.append((a, rows, m_new, acc_new))
                for a, rows, m_new, acc_new in results:
                    m_scr[a, rows, :] = m_new
                    acc_scr[a, rows, :] = acc_new
                return carry2

            lax.fori_loop(0, seq // span // 2, block_body, 0)
            return carry

        lax.fori_loop(0, dil, class_body, 0)
    r = [acc_scr[a] / pltpu.roll(acc_scr[a], HEAD_DIM, 1) for a in range(2)]
    o_ref[...] = jnp.where(own[0], r[0], r[1]).astype(o_ref.dtype)


def dilated_attention(p, bias_tiles, batch, seq):
    nk = A_WIDTH // LANES
    blk = (None, seq, LANES)
    assert all(seq % (2 * dil * DIL_BLOCK) == 0 for _, dil in DILATED_BRANCHES)
    return pl.pallas_call(
        functools.partial(_dilated_fused_body, seq=seq),
        grid=(batch, A_HEADS // 2),
        in_specs=[pl.BlockSpec(blk, lambda b, h: (b, 0, h)),
                  pl.BlockSpec(blk, lambda b, h: (b, 0, nk + h)),
                  pl.BlockSpec(blk, lambda b, h: (b, 0, 2 * nk + h)),
                  pl.BlockSpec((len(DILATED_BRANCHES), 2, DIL_BLOCK, 2 * DIL_BLOCK),
                               lambda b, h: (0, h, 0, 0))],
        out_specs=pl.BlockSpec(blk, lambda b, h: (b, 0, h)),
        out_shape=jax.ShapeDtypeStruct((batch, seq, A_WIDTH), BF16),
        scratch_shapes=[pltpu.VMEM((seq, LANES), F32)] * 3 + [
            pltpu.VMEM((2, seq, LANES), F32), pltpu.VMEM((2, seq, LANES), F32)],
        compiler_params=pltpu.CompilerParams(dimension_semantics=("parallel", "parallel")),
        name="dilated_attention",
    )(p, p, p, bias_tiles)


def _diff_body(q_ref, k_ref, v_ref, b_ref, lam_ref, g_ref, o_ref,
               m_scr, l_scr, acc_scr, *, n_tiles, lam_init):
    t = ATT_BLOCK
    qi = pl.program_id(2)
    q = q_ref[...] * QK_SCALE
    lane = lax.broadcasted_iota(I32, (1, LANES), 1)
    qa = [jnp.where((lane < DIFF_HALF) == (a == 0), q, jnp.zeros_like(q)) for a in range(2)]
    ones = jnp.ones((t, LANES), BF16)
    for a in range(2):
        m_scr[a] = jnp.full((t, LANES), NEG_INF, F32)
        l_scr[a] = jnp.zeros((t, LANES), F32)
        acc_scr[a] = jnp.zeros((t, DIFF_VDIM), F32)

    def step(j, masked):
        off = pl.multiple_of(j * t, t)
        ks = k_ref[pl.ds(off, t), :]
        vs = v_ref[pl.ds(off, t), :]
        bias = b_ref[jnp.minimum(qi - j, n_tiles - 1)]
        if masked:
            row = lax.broadcasted_iota(I32, (t, t), 0)
            col = lax.broadcasted_iota(I32, (t, t), 1)
            causal = row >= col
        for a in range(2):
            s = lax.dot_general(qa[a], ks, (((1,), (1,)), ((), ())),
                                preferred_element_type=F32) + bias
            if masked:
                s = jnp.where(causal, s, NEG_INF)
            m_prev = m_scr[a]
            m_new = jnp.maximum(m_prev, jnp.max(s, axis=-1, keepdims=True))
            alpha = jnp.exp(m_prev - m_new)
            pb = jnp.exp(s - jnp.concatenate([m_new] * (t // LANES), axis=1)).astype(BF16)
            l_scr[a] = alpha * l_scr[a] + jnp.dot(pb, ones, preferred_element_type=F32)
            acc_scr[a] = alpha * acc_scr[a] + jnp.dot(pb, vs, preferred_element_type=F32)
            m_scr[a] = m_new

    def loop_body(j, carry):
        step(j, False)
        return carry

    lax.fori_loop(0, qi, loop_body, 0)
    step(qi, True)

    lp = lam_ref[...]
    lam = (jnp.exp(jnp.sum(lp[0:1] * lp[1:2])) - jnp.exp(jnp.sum(lp[2:3] * lp[3:4]))
           + lam_init)
    o = acc_scr[0] / l_scr[0] - lam * (acc_scr[1] / l_scr[1])
    ms = jnp.mean(o * o, axis=-1, keepdims=True)
    y = o * lax.rsqrt(ms + RMS_EPS) * g_ref[...]
    o_ref[...] = (y * (1.0 - lam_init)).astype(o_ref.dtype)


def diff_attention(p, bias_tiles, lam_params, ln_g, lam_init, batch, seq):
    t = ATT_BLOCK
    n_tiles = bias_tiles.shape[1]
    cq = 3 * A_WIDTH // LANES
    ck = cq + B_QK_WIDTH // LANES
    cv = ck + B_QK_WIDTH // LANES
    return pl.pallas_call(
        functools.partial(_diff_body, n_tiles=n_tiles, lam_init=lam_init),
        grid=(batch, B_HEADS, seq // t),
        in_specs=[
            pl.BlockSpec((None, t, LANES), lambda b, h, i: (b, i, cq + h)),
            pl.BlockSpec((None, seq, LANES), lambda b, h, i: (b, 0, ck + h)),
            pl.BlockSpec((None, seq, LANES), lambda b, h, i: (b, 0, cv + h)),
            pl.BlockSpec((None, n_tiles, t, t), lambda b, h, i: (h, 0, 0, 0)),
            pl.BlockSpec((4, DIFF_HALF), lambda b, h, i: (0, 0)),
            pl.BlockSpec((1, DIFF_VDIM), lambda b, h, i: (0, 0)),
        ],
        out_specs=pl.BlockSpec((None, t, LANES), lambda b, h, i: (b, i, h)),
        out_shape=jax.ShapeDtypeStruct((batch, seq, B_V_WIDTH), BF16),
        scratch_shapes=[pltpu.VMEM((2, t, LANES), F32), pltpu.VMEM((2, t, LANES), F32),
                        pltpu.VMEM((2, t, DIFF_VDIM), F32)],
        compiler_params=pltpu.CompilerParams(
            dimension_semantics=("parallel", "parallel", "arbitrary")),
        name="diff_attention",
    )(p, p, p, bias_tiles, lam_params, ln_g.reshape(1, DIFF_VDIM))


def diff_bias_tiles(rel_bias_b, seq):
    t = ATT_BLOCK
    bucket = _t5_bucket_table(max(seq, 2 * MAX_DISTANCE) + 2 * t)
    sat = bucket[-1]
    d_sat = int(np.max(np.nonzero(bucket != sat)[0])) + 1
    n_full = (d_sat + t - 1 + t - 1) // t
    n_tiles = n_full + 1
    assert n_full * t - (t - 1) >= d_sat
    s = LANES
    nb = t // s
    deltas = np.arange(-(nb - 1), nb * n_tiles)
    n = np.arange(2 * s)[None, :]
    base = deltas[:, None] * s
    dist = np.clip(np.where(n < s, base - n, base + 2 * s - n), 0, None)
    w = rel_bias_b.T[:, bucket[dist]].astype(F32)
    rep = jnp.broadcast_to(w[:, :, None, :], (B_HEADS, len(deltas), s, 2 * s))
    flat = rep.reshape(B_HEADS, len(deltas), 2 * s * s)[:, :, :s * (2 * s - 1)]
    sub = flat.reshape(B_HEADS, len(deltas), s, 2 * s - 1)[:, :, :, :s]
    tiles = [jnp.concatenate(
        [jnp.concatenate([sub[:, d * nb + bi - bj + nb - 1] for bj in range(nb)], axis=-1)
         for bi in range(nb)], axis=-2) for d in range(n_tiles)]
    return jnp.stack(tiles, axis=1)


def _logsig_cumsum_body(f_ref, b_ref, c_ref, carry_scr):
    t = f_ref.shape[0]

    @pl.when(pl.program_id(1) == 0)
    def _():
        carry_scr[...] = jnp.zeros_like(carry_scr)

    x = f_ref[...] + b_ref[...]
    ls = jnp.minimum(x, 0.0) - jnp.log1p(jnp.exp(-jnp.abs(x)))
    row = lax.broadcasted_iota(I32, (t, t), 0)
    col = lax.broadcasted_iota(I32, (t, t), 1)
    tri = (row >= col).astype(F32)
    c = jnp.dot(tri, ls, preferred_element_type=F32,
                precision=lax.Precision.HIGHEST) + carry_scr[...]
    c_ref[...] = c
    carry_scr[...] = c[t - 1:t, :]


def logsig_cumsum(fg, b_f, batch, seq, t=512):
    return pl.pallas_call(
        _logsig_cumsum_body,
        grid=(batch, seq // t),
        in_specs=[pl.BlockSpec((None, t, LANES), lambda b, i: (b, i, 0)),
                  pl.BlockSpec((1, LANES), lambda b, i: (0, 0))],
        out_specs=pl.BlockSpec((None, t, LANES), lambda b, i: (b, i, 0)),
        out_shape=jax.ShapeDtypeStruct((batch, seq, LANES), F32),
        scratch_shapes=[pltpu.VMEM((1, LANES), F32)],
        compiler_params=pltpu.CompilerParams(
            dimension_semantics=("parallel", "arbitrary")),
        name="logsig_cumsum",
    )(fg, b_f)


def _fox_body(q_ref, k_ref, v_ref, cq_ref, ck_ref, o_ref, m_scr, acc_scr):
    t = ATT_BLOCK
    qi = pl.program_id(2)
    q = q_ref[...] * QK_SCALE
    lane = lax.broadcasted_iota(I32, (1, LANES), 1)
    own = [(lane < HEAD_DIM) == (a == 0) for a in range(2)]
    qa = [jnp.where(own[a], q, jnp.zeros_like(q)) for a in range(2)]
    cqb = [jnp.broadcast_to(cq_ref[a], (t, LANES)) for a in range(2)]
    for a in range(2):
        m_scr[a] = jnp.full((t, LANES), NEG_INF, F32)
        acc_scr[a] = jnp.zeros((t, LANES), F32)

    def step(j, masked):
        off = pl.multiple_of(j * t, t)
        ks = k_ref[pl.ds(off, t), :]
        vs = v_ref[pl.ds(off, t), :]
        if masked:
            row = lax.broadcasted_iota(I32, (t, t), 0)
            col = lax.broadcasted_iota(I32, (t, t), 1)
            causal = row >= col
        for a in range(2):
            s = lax.dot_general(qa[a], ks, (((1,), (1,)), ((), ())),
                                preferred_element_type=F32)
            s = s - ck_ref[a, :, pl.ds(off, t)]
            if masked:
                s = jnp.where(causal, s, NEG_INF)
            m_prev = m_scr[a]
            m_new = jnp.maximum(m_prev, jnp.max(s, axis=-1, keepdims=True) + cqb[a])
            alpha = jnp.exp(m_prev - m_new)
            shift = m_new - cqb[a]
            pb = jnp.exp(s - jnp.concatenate([shift] * (t // LANES), axis=1)).astype(BF16)
            v_aug = jnp.where(own[a], vs, jnp.ones_like(vs))
            acc_scr[a] = alpha * acc_scr[a] + jnp.dot(pb, v_aug, preferred_element_type=F32)
            m_scr[a] = m_new

    def loop_body(j, carry):
        step(j, False)
        return carry

    lax.fori_loop(0, qi, loop_body, 0)
    step(qi, True)
    r = [acc_scr[a] / pltpu.roll(acc_scr[a], HEAD_DIM, 1) for a in range(2)]
    o_ref[...] = jnp.where(own[0], r[0], r[1]).astype(o_ref.dtype)


def fox_attention(p, cq, ck, batch, seq):
    t = ATT_BLOCK
    nk = C_WIDTH // LANES
    return pl.pallas_call(
        _fox_body,
        grid=(batch, C_HEADS // 2, seq // t),
        in_specs=[
            pl.BlockSpec((None, t, LANES), lambda b, h, i: (b, i, h)),
            pl.BlockSpec((None, seq, LANES), lambda b, h, i: (b, 0, nk + h)),
            pl.BlockSpec((None, seq, LANES), lambda b, h, i: (b, 0, 2 * nk + h)),
            pl.BlockSpec((None, 2, t, 1), lambda b, h, i: (b, h, i, 0)),
            pl.BlockSpec((None, 2, 1, seq), lambda b, h, i: (b, h, 0, 0)),
        ],
        out_specs=pl.BlockSpec((None, t, LANES), lambda b, h, i: (b, i, h)),
        out_shape=jax.ShapeDtypeStruct((batch, seq, C_WIDTH), BF16),
        scratch_shapes=[pltpu.VMEM((2, t, LANES), F32), pltpu.VMEM((2, t, LANES), F32)],
        compiler_params=pltpu.CompilerParams(
            dimension_semantics=("parallel", "parallel", "arbitrary")),
        name="fox_attention",
    )(p, p, p, cq, ck)


def _concat_out_body(oa_ref, ob_ref, w_ref, x_ref, out_ref):
    out_ref[...] = (x_ref[...]
                    + jnp.dot(oa_ref[...], w_ref[:A_WIDTH, :], preferred_element_type=F32)
                    + jnp.dot(ob_ref[...], w_ref[A_WIDTH:, :], preferred_element_type=F32))


def concat_out_proj(oa, ob, w, x, tm=1024, tn=1024):
    m, d = x.shape
    return pl.pallas_call(
        _concat_out_body,
        grid=(m // tm, d // tn),
        in_specs=[pl.BlockSpec((tm, A_WIDTH), lambda i, j: (i, 0)),
                  pl.BlockSpec((tm, B_V_WIDTH), lambda i, j: (i, 0)),
                  pl.BlockSpec((A_WIDTH + B_V_WIDTH, tn), lambda i, j: (0, j)),
                  pl.BlockSpec((tm, tn), lambda i, j: (i, j))],
        out_specs=pl.BlockSpec((tm, tn), lambda i, j: (i, j)),
        out_shape=jax.ShapeDtypeStruct((m, d), F32),
        compiler_params=pltpu.CompilerParams(dimension_semantics=("parallel", "parallel")),
        name="concat_out_proj",
    )(oa, ob, w, x)


def _matmul_res_body(a_ref, w_ref, x_ref, o_ref):
    o_ref[...] = x_ref[...] + jnp.dot(a_ref[...], w_ref[...], preferred_element_type=F32)


def matmul_residual(a, w, x, tm=1024, tn=1024):
    m, k = a.shape
    n = w.shape[1]
    return pl.pallas_call(
        _matmul_res_body,
        grid=(m // tm, n // tn),
        in_specs=[pl.BlockSpec((tm, k), lambda i, j: (i, 0)),
                  pl.BlockSpec((k, tn), lambda i, j: (0, j)),
                  pl.BlockSpec((tm, tn), lambda i, j: (i, j))],
        out_specs=pl.BlockSpec((tm, tn), lambda i, j: (i, j)),
        out_shape=jax.ShapeDtypeStruct((m, n), F32),
        compiler_params=pltpu.CompilerParams(
            dimension_semantics=("parallel", "parallel")),
        name="matmul_residual",
    )(a, w, x)


def _add_norm_body(x_ref, r_ref, g_ref, o_ref):
    x = x_ref[...] + r_ref[...]
    ms = jnp.mean(x * x, axis=-1, keepdims=True)
    o_ref[...] = x * lax.rsqrt(ms + RMS_EPS) * g_ref[...]


def add_norm(x, r, g, tm=512):
    m, d = x.shape
    row = pl.BlockSpec((tm, d), lambda i: (i, 0))
    return pl.pallas_call(
        _add_norm_body,
        grid=(m // tm,),
        in_specs=[row, row, pl.BlockSpec((1, d), lambda i: (0, 0))],
        out_specs=row,
        out_shape=jax.ShapeDtypeStruct((m, d), F32),
        compiler_params=pltpu.CompilerParams(dimension_semantics=("parallel",)),
        name="add_norm",
    )(x, r, g.reshape(1, d))


def _peer_candidates():
    groups = [("a", 0, 0), ("a", 0, 8), ("a", 1, 0), ("b", 0, 8), ("a", 2, 0), ("a", 3, 0),
              ("b", 0, 0), ("b", 1, 0), ("b", 2, 0)]
    cid = np.zeros((8 * len(groups), 1), np.int32)
    seen = set()
    for g, (kind, fixed, start) in enumerate(groups):
        for r in range(8):
            a, b = (fixed, start + r) if kind == "a" else (start + r, fixed)
            row = 8 * g + r
            if (a + 1) * (b + 1) <= PEER_TOPK and (a, b) not in seen:
                seen.add((a, b))
                cid[row, 0] = a * PEER_TOPK + b
            else:
                cid[row, 0] = PEER_TOPK * PEER_TOPK + row
    assert len(seen) == sum((a + 1) * (b + 1) <= PEER_TOPK
                            for a in range(PEER_TOPK) for b in range(PEER_TOPK))
    return groups, cid


def _peer_topk_body(q_ref, sk_ref, cid_ref, idx_ref, gate_ref, ts_scr, ti_scr,
                    bs_scr, be_scr, cv_scr, ce_scr, sc_scr, *, groups):
    tm = q_ref.shape[0]
    neg_inf = jnp.float32(-jnp.inf)
    key_id = lax.broadcasted_iota(I32, (1, N_KEYS, tm), 1)
    batch = sc_scr.shape[0]

    def group_body(g, carry):
        for j in range(batch):
            pr = g * batch + j
            off = pl.multiple_of(pr * PEER_KEY_HALF, PEER_KEY_HALF)
            sc_scr[j] = lax.dot_general(sk_ref[pr], q_ref[:, pl.ds(off, PEER_KEY_HALF)],
                                        (((1,), (1,)), ((), ())), preferred_element_type=F32)
        rows = pl.ds(g * batch, batch)

        def k_body(k, c):
            vals = sc_scr[...]
            m = jnp.max(vals, axis=1, keepdims=True)
            sel = jnp.min(jnp.where(vals == m, key_id, N_KEYS), axis=1, keepdims=True)
            ts_scr[rows, pl.ds(k, 1), :] = m
            ti_scr[rows, pl.ds(k, 1), :] = sel
            sc_scr[...] = jnp.where(key_id == sel, neg_inf, vals)
            return c

        lax.fori_loop(0, PEER_TOPK, k_body, 0)
        return carry

    lax.fori_loop(0, 2 * PEER_HEADS // batch, group_body, 0)

    cand_id = cid_ref[...]
    pad = jnp.where(cand_id < PEER_TOPK * PEER_TOPK, 0.0, neg_inf)

    def pick(x1, x2):
        return jnp.concatenate(
            [x1[f:f + 1] + x2[s:s + 8] if kind == "a" else x1[s:s + 8] + x2[f:f + 1]
             for kind, f, s in groups], axis=0)

    def head_body(h, carry):
        cv_scr[h] = pick(ts_scr[2 * h], ts_scr[2 * h + 1]) + pad
        ce_scr[h] = pick(ti_scr[2 * h] * N_KEYS, ti_scr[2 * h + 1])
        return carry

    lax.fori_loop(0, PEER_HEADS, head_body, 0)

    cid3 = cand_id[None]

    def k_body(k, carry):
        vals = cv_scr[...]
        m = jnp.max(vals, axis=1, keepdims=True)
        sel = jnp.min(jnp.where(vals == m, cid3, PEER_TOPK * PEER_TOPK), axis=1, keepdims=True)
        hit = cid3 == sel
        bs_scr[:, pl.ds(k, 1), :] = m
        be_scr[:, pl.ds(k, 1), :] = jnp.sum(jnp.where(hit, ce_scr[...], 0), axis=1, keepdims=True)
        cv_scr[...] = jnp.where(hit, neg_inf, vals)
        return carry

    lax.fori_loop(0, PEER_TOPK, k_body, 0)
    bs = bs_scr[...]
    e = jnp.exp(bs - jnp.max(bs, axis=1, keepdims=True))
    gate = e / jnp.sum(e, axis=1, keepdims=True)
    n_sel = PEER_HEADS * PEER_TOPK
    gate_ref[...] = gate.reshape(n_sel, tm).T
    idx_ref[...] = be_scr[...].reshape(n_sel, tm).T


def peer_topk(q, subkeys, tm=256):
    m = q.shape[0]
    n_sel = PEER_HEADS * PEER_TOPK
    groups, cid = _peer_candidates()
    out_spec = pl.BlockSpec((tm, n_sel), lambda i: (i, 0))
    return pl.pallas_call(
        functools.partial(_peer_topk_body, groups=groups),
        grid=(m // tm,),
        in_specs=[pl.BlockSpec((tm, q.shape[1]), lambda i: (i, 0)),
                  pl.BlockSpec(subkeys.shape, lambda i: (0, 0, 0)),
                  pl.BlockSpec(cid.shape, lambda i: (0, 0))],
        out_specs=[out_spec, out_spec],
        out_shape=[jax.ShapeDtypeStruct((m, n_sel), I32),
                   jax.ShapeDtypeStruct((m, n_sel), F32)],
        scratch_shapes=[pltpu.VMEM((2 * PEER_HEADS, PEER_TOPK, tm), F32),
                        pltpu.VMEM((2 * PEER_HEADS, PEER_TOPK, tm), I32),
                        pltpu.VMEM((PEER_HEADS, PEER_TOPK, tm), F32),
                        pltpu.VMEM((PEER_HEADS, PEER_TOPK, tm), I32),
                        pltpu.VMEM((PEER_HEADS, cid.shape[0], tm), F32),
                        pltpu.VMEM((PEER_HEADS, cid.shape[0], tm), I32),
                        pltpu.VMEM((PEER_HEADS, N_KEYS, tm), F32)],
        compiler_params=pltpu.CompilerParams(dimension_semantics=("parallel",)),
        name="peer_topk",
    )(q, subkeys, jnp.asarray(cid))


def _gelu_gate_body(a_ref, g_ref, o_ref):
    a = a_ref[...]
    o_ref[...] = g_ref[...] * (0.5 * a * (1.0 + lax.erf(a * (2.0 ** -0.5))))


def gelu_gate(act, gate, tm=2048):
    m, n = act.shape
    spec = pl.BlockSpec((tm, n), lambda i: (i, 0))
    return pl.pallas_call(
        _gelu_gate_body,
        grid=(m // tm,),
        in_specs=[spec, spec],
        out_specs=spec,
        out_shape=jax.ShapeDtypeStruct((m, n), F32),
        compiler_params=pltpu.CompilerParams(dimension_semantics=("parallel",)),
        name="gelu_gate",
    )(act, gate)


SC_TOK_CHUNK = 32
SC_DOTS_CHUNK = 64
SC_RING = 8
SC_BF16_GROUP = 4
SC_FMT = plsc.PackFormat.INTERLEAVED


def _sc_worker_id():
    return lax.axis_index("s") * SC_CORES + lax.axis_index("c")


def pack_bf16_pairs(t):
    half = t.shape[-1] // 2
    bits = lax.bitcast_convert_type(t.astype(BF16).astype(F32), I32)
    return (bits[..., half:] & jnp.int32(-65536)) | lax.shift_right_logical(
        bits[..., :half], jnp.int32(16))


def _sc_row_pipeline(idx_v, table_hbm, rows_v, sems, n_items, groups, compute):
    def gather(item):
        tt, g = item // groups, item % groups
        ids = idx_v[tt, pl.ds(g * SC_LANES, SC_LANES)]
        slot = item % SC_RING
        return pltpu.make_async_copy(table_hbm.at[ids], rows_v.at[slot], sems.at[slot])

    for s in range(SC_RING - 1):
        gather(s).start()

    def item_body(item, carry):
        nxt = item + SC_RING - 1

        @pl.when(nxt < n_items)
        def _():
            gather(nxt).start()

        gather(item).wait()
        compute(item // groups, item % groups, item % SC_RING)
        return carry

    lax.fori_loop(0, n_items, item_body, 0)


def peer_expert_dots(hp, idx, up):
    m, dw = hp.shape
    n_sel = idx.shape[1]
    per_w = m // SC_WORKERS
    chunk = SC_DOTS_CHUNK
    n_chunks = per_w // chunk
    groups = n_sel // SC_LANES
    step = SC_BF16_GROUP * SC_LANES
    mesh = plsc.VectorSubcoreMesh(core_axis_name="c", subcore_axis_name="s")

    @functools.partial(
        pl.kernel, mesh=mesh,
        out_type=jax.ShapeDtypeStruct((m, n_sel), F32),
        scratch_types=[
            pltpu.VMEM((chunk, n_sel), I32),
            pltpu.VMEM((chunk, dw), I32),
            pltpu.VMEM((chunk, n_sel), F32),
            pltpu.VMEM((SC_RING, SC_LANES, dw), I32),
            pltpu.VMEM((SC_LANES * SC_LANES,), F32),
            pltpu.SemaphoreType.DMA((SC_RING,)),
        ],
        compiler_params=pltpu.CompilerParams(needs_layout_passes=False),
        name="peer_expert_dots",
    )
    def k(h_hbm, idx_hbm, u_hbm, act_hbm, idx_v, h_v, act_v, rows_v, part_v, sems):
        base = _sc_worker_id() * per_w
        lane = lax.broadcasted_iota(I32, (SC_LANES,), 0)

        def compute(tt, g, slot):
            def grp_body(q, accs):
                off = pl.multiple_of(q * step, step)
                xs = [plsc.bitcast(h_v[tt, pl.ds(off + c * SC_LANES, SC_LANES)], BF16)
                      for c in range(SC_BF16_GROUP)]
                new = []
                for e in range(SC_LANES):
                    ps = [plsc.bitcast(
                        rows_v[slot, e, pl.ds(off + c * SC_LANES, SC_LANES)], BF16) * xs[c]
                        for c in range(SC_BF16_GROUP)]
                    while len(ps) > 1:
                        ps = [ps[i] + ps[i + 1] for i in range(0, len(ps), 2)]
                    lo, hi = plsc.unpack(ps[0], format=SC_FMT)
                    new.append(accs[e] + (lo + hi))
                return tuple(new)

            accs = lax.fori_loop(
                0, dw // step, grp_body,
                tuple(jnp.zeros((SC_LANES,), F32) for _ in range(SC_LANES)))
            for e in range(SC_LANES):
                part_v[pl.ds(e * SC_LANES, SC_LANES)] = accs[e]
            tot = jnp.zeros((SC_LANES,), F32)
            for l in range(SC_LANES):
                tot = tot + plsc.load_gather(part_v, [lane * SC_LANES + l])
            act_v[tt, pl.ds(g * SC_LANES, SC_LANES)] = tot

        def chunk_body(c, carry):
            t0 = base + c * chunk
            pltpu.sync_copy(idx_hbm.at[pl.ds(t0, chunk)], idx_v)
            pltpu.sync_copy(h_hbm.at[pl.ds(t0, chunk)], h_v)
            _sc_row_pipeline(idx_v, u_hbm, rows_v, sems, chunk * groups, groups, compute)
            pltpu.sync_copy(act_v, act_hbm.at[pl.ds(t0, chunk)])
            return carry

        lax.fori_loop(0, n_chunks, chunk_body, 0)

    return k(hp, idx, up)


def peer_expert_combine(w, idx, vp):
    m, n_sel = w.shape
    dw = vp.shape[1]
    d = 2 * dw
    per_w = m // SC_WORKERS
    n_chunks = per_w // SC_TOK_CHUNK
    n_vec = d // SC_LANES
    groups = n_sel // SC_LANES
    mesh = plsc.VectorSubcoreMesh(core_axis_name="c", subcore_axis_name="s")

    @functools.partial(
        pl.kernel, mesh=mesh,
        out_type=jax.ShapeDtypeStruct((m, d), F32),
        scratch_types=[
            pltpu.VMEM((SC_TOK_CHUNK, n_sel), I32),
            pltpu.VMEM((SC_TOK_CHUNK, n_sel), F32),
            pltpu.VMEM((SC_TOK_CHUNK, d), F32),
            pltpu.VMEM((SC_RING, SC_LANES, dw), I32),
            pltpu.SemaphoreType.DMA((SC_RING,)),
        ],
        compiler_params=pltpu.CompilerParams(needs_layout_passes=False),
        name="peer_expert_combine",
    )
    def k(w_hbm, idx_hbm, v_hbm, out_hbm, idx_v, w_v, out_v, rows_v, sems):
        base = _sc_worker_id() * per_w

        def compute(tt, g, slot):
            splat = []
            for e in range(SC_LANES):
                s = plsc.load_gather(w_v, [jnp.full((SC_LANES,), tt, I32),
                                           jnp.full((SC_LANES,), g * SC_LANES + e, I32)])
                splat.append(plsc.pack(s, s, format=SC_FMT))

            @plsc.parallel_loop(0, dw // SC_LANES)
            def _(j):
                off = pl.multiple_of(j * SC_LANES, SC_LANES)
                acc_lo = out_v[tt, pl.ds(off, SC_LANES)]
                acc_hi = out_v[tt, pl.ds(dw + off, SC_LANES)]
                for e0 in range(0, SC_LANES, SC_BF16_GROUP):
                    s = None
                    for e in range(e0, e0 + SC_BF16_GROUP):
                        p = plsc.bitcast(rows_v[slot, e, pl.ds(off, SC_LANES)], BF16) * splat[e]
                        s = p if s is None else s + p
                    lo, hi = plsc.unpack(s, format=SC_FMT)
                    acc_lo = acc_lo + lo
                    acc_hi = acc_hi + hi
                out_v[tt, pl.ds(off, SC_LANES)] = acc_lo
                out_v[tt, pl.ds(dw + off, SC_LANES)] = acc_hi

        def chunk_body(c, carry):
            t0 = base + c * SC_TOK_CHUNK
            pltpu.sync_copy(idx_hbm.at[pl.ds(t0, SC_TOK_CHUNK)], idx_v)
            pltpu.sync_copy(w_hbm.at[pl.ds(t0, SC_TOK_CHUNK)], w_v)

            def zero_body(z, carry2):
                tt, j = z // n_vec, z % n_vec
                out_v[tt, pl.ds(pl.multiple_of(j * SC_LANES, SC_LANES), SC_LANES)] = (
                    jnp.zeros((SC_LANES,), F32))
                return carry2

            lax.fori_loop(0, SC_TOK_CHUNK * n_vec, zero_body, 0)
            _sc_row_pipeline(idx_v, v_hbm, rows_v, sems, SC_TOK_CHUNK * groups, groups, compute)
            pltpu.sync_copy(out_v, out_hbm.at[pl.ds(t0, SC_TOK_CHUNK)])
            return carry

        lax.fori_loop(0, n_chunks, chunk_body, 0)

    return k(w, idx, vp)


def kernel(x, norm_mix_g, norm_ffn_g, final_norm_g, rel_bias, even_w_in, even_w_out,
           diff_lambda, diff_ln_g, odd_w_in, odd_b_f, odd_w_out, peer_wq, peer_subkeys,
           peer_u, peer_v):
    batch, seq, d = x.shape

    dil_tiles = jnp.stack(
        [dilated_bias_tile(rel_bias[:, :A_HEADS], w, dl) for w, dl in DILATED_BRANCHES])
    diff_tiles = diff_bias_tiles(rel_bias[:, A_HEADS:], seq)
    lam_init = 0.8 - 0.6 * math.exp(-0.3 * 0)
    even_in, even_out = even_w_in[0].astype(BF16), even_w_out[0].astype(BF16)
    w_in = odd_w_in[0]
    odd_in, odd_out = w_in[:, :3 * C_WIDTH].astype(BF16), odd_w_out[0].astype(BF16)
    w_gate = jnp.pad(w_in[:, 3 * C_WIDTH:], ((0, 0), (0, LANES - C_HEADS)))
    b_f = jnp.pad(odd_b_f[0], (0, LANES - C_HEADS)).reshape(1, LANES)
    peer = [dict(wq=peer_wq[l].astype(BF16),
                 sk=peer_subkeys[l].reshape(2 * PEER_HEADS, N_KEYS, PEER_KEY_HALF).astype(BF16))
            for l in range(2)]
    packed_tables = {}

    def expert_table(table, layer, anchor):
        if (id(table), layer) not in packed_tables:
            rows, _ = lax.optimization_barrier((table[layer], anchor))
            packed_tables[(id(table), layer)] = pack_bf16_pairs(rows)
        return packed_tables[(id(table), layer)]

    assert batch % BATCH_GROUPS == 0
    bg = batch // BATCH_GROUPS
    m = bg * seq
    groups = [dict(g=g) for g in range(BATCH_GROUPS)]

    def peer_select(st, layer, after=None, parts=1):
        rows = m // parts
        pieces = []
        for part in range(parts):
            xs = st["x"] if parts == 1 else st["x"][part * rows:(part + 1) * rows]
            q, hp = norm_matmul(xs, norm_ffn_g[layer], peer[layer]["wq"], want_h="packed")
            idx, gate = peer_topk(q, peer[layer]["sk"])
            if after is not None and after["st"] is not st:
                after["st"]["peer"], idx = lax.optimization_barrier((after["st"]["peer"], idx))
            pieces.append((idx, gate,
                           peer_expert_dots(hp, idx, expert_table(peer_u, layer, st["x"]))))
        idx, gate, act = (jnp.concatenate(c, axis=0) if parts > 1 else c[0] for c in zip(*pieces))
        return dict(st=st, layer=layer, idx=idx, gate=gate, act=act)

    def peer_combine(rec):
        rec["w"] = gelu_gate(rec["act"], rec["gate"])
        rec["st"]["peer"] = peer_expert_combine(
            rec["w"], rec["idx"], expert_table(peer_v, rec["layer"], rec["act"]))

    def even_layer(st, tie):
        (p,) = norm_matmul(st["x"], norm_mix_g[0], even_in)
        p3 = p.reshape(bg, seq, -1)
        oa = dilated_attention(p3, dil_tiles, bg, seq)
        ob = tie(diff_attention(p3, diff_tiles, diff_lambda[0], diff_ln_g[0], lam_init, bg, seq))
        st["x"] = concat_out_proj(oa.reshape(m, A_WIDTH), ob.reshape(m, B_V_WIDTH), even_out,
                                  st["x"])

    def odd_layer(st, tie):
        p, st["x"], fg = norm_matmul(st["x"], norm_mix_g[1], odd_in, res=st["peer"],
                                     want_h="gate", gate_w=w_gate)
        c = logsig_cumsum(fg.reshape(bg, seq, LANES), b_f, bg, seq)
        ct = c[:, :, :C_HEADS].transpose(0, 2, 1)
        o = tie(fox_attention(p.reshape(bg, seq, -1), ct[:, :, :, None], ct[:, :, None, :],
                              bg, seq))
        st["x"] = matmul_residual(o.reshape(m, C_WIDTH), odd_out, st["x"])

    stages = [(even_layer, st, 0) for st in groups] + [(odd_layer, st, 1) for st in groups]
    recs = []
    for k, (mixer, st, layer) in enumerate(stages):
        def tie(t, prev=recs[k - 1] if k >= 1 else None):
            if prev is None:
                return t
            prev["act"], t = lax.optimization_barrier((prev["act"], t))
            peer_combine(prev)
            prev["w"], t = lax.optimization_barrier((prev["w"], t))
            return t

        if "x" not in st:
            xin = x if k == 0 else lax.optimization_barrier((x, recs[k - 1]["idx"]))[0]
            st["x"] = xin[st["g"] * bg:(st["g"] + 1) * bg].reshape(m, d)
        mixer(st, tie)
        recs.append(peer_select(st, layer, after=recs[max(k - 2, 0)] if k >= 1 else None,
                                parts=2 if k <= 1 else 1))
    peer_combine(recs[-1])
    outs = [add_norm(st["x"], st["peer"], final_norm_g).reshape(bg, seq, d) for st in groups]
    return jnp.concatenate(outs, axis=0)
```

```python
import functools
import math

import numpy as np
import jax
import jax.numpy as jnp
from jax import lax
from jax.experimental import pallas as pl
from jax.experimental.pallas import tpu as pltpu
from jax.experimental.pallas import tpu_sc as plsc

F32 = jnp.float32
BF16 = jnp.bfloat16
I32 = jnp.int32

HEAD_DIM = 64
A_HEADS = 8
DILATED_BRANCHES = ((128, 1), (512, 4), (2048, 16))
DIFF_HALF = 64
DIFF_VDIM = 128
B_HEADS = 4
C_HEADS = 16
N_BUCKETS = 32
MAX_DISTANCE = 2048
PEER_HEADS = 8
N_KEYS = 128
PEER_TOPK = 16
PEER_KEY_HALF = 128
RMS_EPS = 1e-6
NEG_INF = -1e30
A_WIDTH = A_HEADS * HEAD_DIM
B_QK_WIDTH = B_HEADS * 2 * DIFF_HALF
B_V_WIDTH = B_HEADS * DIFF_VDIM
C_WIDTH = C_HEADS * HEAD_DIM
QK_SCALE = 0.125
LANES = 128
DIL_BLOCK = 128
ATT_BLOCK = 512
BATCH_GROUPS = 8

SC_CORES = 2
SC_SUBCORES = 16
SC_LANES = 16
SC_WORKERS = SC_CORES * SC_SUBCORES


def _t5_bucket_table(n):
    max_exact = N_BUCKETS // 2
    d = np.arange(n)
    df = np.maximum(d, 1).astype(np.float32)
    large = max_exact + (
        np.log(df / np.float32(max_exact)) / np.float32(math.log(MAX_DISTANCE / max_exact))
        * np.float32(N_BUCKETS - max_exact)).astype(np.int32)
    large = np.minimum(large, N_BUCKETS - 1)
    return np.where(d < max_exact, d, large).astype(np.int32)


def _norm_matmul_body(*refs, has_res, want_h):
    it = iter(refs)
    x_ref = next(it)
    r_ref = next(it) if has_res else None
    g_ref = next(it)
    w_ref = next(it)
    gw_ref = next(it) if want_h == "gate" else None
    o_ref = next(it)
    xs_ref = next(it) if has_res else None
    hout_ref = next(it) if want_h else None
    h_scr = next(it)

    @pl.when(pl.program_id(1) == 0)
    def _():
        x = x_ref[...]
        if has_res:
            x = x + r_ref[...]
            xs_ref[...] = x
        ms = jnp.mean(x * x, axis=-1, keepdims=True)
        h = x * lax.rsqrt(ms + RMS_EPS) * g_ref[...]
        if want_h == "gate":
            hout_ref[...] = jnp.dot(h, gw_ref[...], preferred_element_type=F32,
                                    precision=lax.Precision.HIGHEST)
        elif want_h == "packed":
            hout_ref[...] = pack_bf16_pairs(h)
        h_scr[...] = h.astype(BF16)

    o_ref[...] = jnp.dot(h_scr[...], w_ref[...],
                         preferred_element_type=F32).astype(o_ref.dtype)


def norm_matmul(x, g, w, *, res=None, want_h=None, gate_w=None, out_dtype=BF16,
                tm=1024, tn=1024):
    m, d = x.shape
    n = w.shape[1]
    tn = min(tn, n)
    row = pl.BlockSpec((tm, d), lambda i, j: (i, 0))
    in_specs = [row] + ([row] if res is not None else []) + [
        pl.BlockSpec((1, d), lambda i, j: (0, 0)),
        pl.BlockSpec((d, tn), lambda i, j: (0, j))]
    out_specs = [pl.BlockSpec((tm, tn), lambda i, j: (i, j))]
    out_shape = [jax.ShapeDtypeStruct((m, n), out_dtype)]
    if res is not None:
        out_specs.append(row)
        out_shape.append(jax.ShapeDtypeStruct((m, d), F32))
    if want_h == "gate":
        in_specs.append(pl.BlockSpec(gate_w.shape, lambda i, j: (0, 0)))
        out_specs.append(pl.BlockSpec((tm, gate_w.shape[1]), lambda i, j: (i, 0)))
        out_shape.append(jax.ShapeDtypeStruct((m, gate_w.shape[1]), F32))
    elif want_h == "packed":
        out_specs.append(pl.BlockSpec((tm, d // 2), lambda i, j: (i, 0)))
        out_shape.append(jax.ShapeDtypeStruct((m, d // 2), I32))
    args = [x] + ([res] if res is not None else []) + [g.reshape(1, d), w] + (
        [gate_w] if want_h == "gate" else [])
    return pl.pallas_call(
        functools.partial(_norm_matmul_body, has_res=res is not None, want_h=want_h),
        grid=(m // tm, n // tn),
        in_specs=in_specs,
        out_specs=out_specs,
        out_shape=out_shape,
        scratch_shapes=[pltpu.VMEM((tm, d), BF16)],
        compiler_params=pltpu.CompilerParams(
            dimension_semantics=("parallel", "arbitrary")),
        name="norm_matmul",
    )(*args)


def dilated_bias_tile(rel_bias_a, window, dil):
    n = window // dil
    assert n == DIL_BLOCK
    bucket = _t5_bucket_table(window + 1)
    period = 4 * n
    u = np.arange(period)
    valid = u <= n
    w = jnp.where(jnp.asarray(valid)[None],
                  rel_bias_a.T[:, bucket[np.where(valid, n - u, 0) * dil]], NEG_INF).astype(F32)
    rep = jnp.broadcast_to(w[:, None, :], (A_HEADS, n, period))
    flat = rep.reshape(A_HEADS, n * period)[:, :n * (period - 1)]
    return flat.reshape(A_HEADS, n, period - 1)[:, :, :2 * n]


def _dilated_fused_body(q_ref, k_ref, v_ref, b_ref, o_ref, qf, kf, vf, m_scr, acc_scr, *, seq):
    qf[...] = q_ref[...].astype(F32) * QK_SCALE
    kf[...] = k_ref[...].astype(F32)
    vf[...] = v_ref[...].astype(F32)
    m_scr[...] = jnp.full(m_scr.shape, NEG_INF, F32)
    acc_scr[...] = jnp.zeros(acc_scr.shape, F32)
    n = DIL_BLOCK
    col = lax.broadcasted_iota(I32, (n, 2 * n), 1)
    lane = lax.broadcasted_iota(I32, (1, LANES), 1)
    own = [(lane < HEAD_DIM) == (a == 0) for a in range(2)]
    for bi, (_, dil) in enumerate(DILATED_BRANCHES):
        span = dil * n

        def class_body(r, carry, bi=bi, dil=dil, span=span):
            def block_body(i2, carry2):
                loaded = []
                for u in range(2):
                    i = 2 * i2 + u
                    start = r + span * i
                    rows = pl.ds(start, n, stride=dil)
                    prev = pl.ds(jnp.maximum(start - span, r), n, stride=dil)
                    q = qf[rows, :].astype(BF16)
                    k = jnp.concatenate([kf[prev, :], kf[rows, :]], axis=0).astype(BF16)
                    v = jnp.concatenate([vf[prev, :], vf[rows, :]], axis=0).astype(BF16)
                    has_prev = jnp.logical_or(col >= n, i > 0)
                    state = [(m_scr[a, rows, :], acc_scr[a, rows, :]) for a in range(2)]
                    loaded.append((rows, q, k, v, has_prev, state))
                results = []
                for rows, q, k, v, has_prev, state in loaded:
                    for a in range(2):
                        m_prev, acc_prev = state[a]
                        s = lax.dot_general(jnp.where(own[a], q, jnp.zeros_like(q)), k,
                                            (((1,), (1,)), ((), ())), preferred_element_type=F32)
                        s = jnp.where(has_prev, s + b_ref[bi, a], NEG_INF)
                        m_new = jnp.maximum(m_prev, jnp.max(s, axis=-1, keepdims=True))
                        alpha = jnp.exp(m_prev - m_new)
                        pb = jnp.exp(s - jnp.concatenate([m_new, m_new], axis=1)).astype(BF16)
                        acc_new = alpha * acc_prev + jnp.dot(
                            pb, jnp.where(own[a], v, jnp.ones_like(v)), preferred_element_type=F32)
                        results.append((a, rows, m_new, acc_new))
                for a, rows, m_new, acc_new in results:
                    m_scr[a, rows, :] = m_new
                    acc_scr[a, rows, :] = acc_new
                return carry2

            lax.fori_loop(0, seq // span // 2, block_body, 0)
            return carry

        lax.fori_loop(0, dil, class_body, 0)
    r = [acc_scr[a] / pltpu.roll(acc_scr[a], HEAD_DIM, 1) for a in range(2)]
    o_ref[...] = jnp.where(own[0], r[0], r[1]).astype(o_ref.dtype)


def dilated_attention(p, bias_tiles, batch, seq):
    nk = A_WIDTH // LANES
    blk = (None, seq, LANES)
    assert all(seq % (2 * dil * DIL_BLOCK) == 0 for _, dil in DILATED_BRANCHES)
    return pl.pallas_call(
        functools.partial(_dilated_fused_body, seq=seq),
        grid=(batch, A_HEADS // 2),
        in_specs=[pl.BlockSpec(blk, lambda b, h: (b, 0, h)),
                  pl.BlockSpec(blk, lambda b, h: (b, 0, nk + h)),
                  pl.BlockSpec(blk, lambda b, h: (b, 0, 2 * nk + h)),
                  pl.BlockSpec((len(DILATED_BRANCHES), 2, DIL_BLOCK, 2 * DIL_BLOCK),
                               lambda b, h: (0, h, 0, 0))],
        out_specs=pl.BlockSpec(blk, lambda b, h: (b, 0, h)),
        out_shape=jax.ShapeDtypeStruct((batch, seq, A_WIDTH), BF16),
        scratch_shapes=[pltpu.VMEM((seq, LANES), F32)] * 3 + [
            pltpu.VMEM((2, seq, LANES), F32), pltpu.VMEM((2, seq, LANES), F32)],
        compiler_params=pltpu.CompilerParams(dimension_semantics=("parallel", "parallel")),
        name="dilated_attention",
    )(p, p, p, bias_tiles)


def _diff_body(q_ref, k_ref, v_ref, b_ref, lam_ref, g_ref, o_ref,
               m_scr, l_scr, acc_scr, *, n_tiles, lam_init):
    t = ATT_BLOCK
    qi = pl.program_id(2)
    q = q_ref[...] * QK_SCALE
    lane = lax.broadcasted_iota(I32, (1, LANES), 1)
    qa = [jnp.where((lane < DIFF_HALF) == (a == 0), q, jnp.zeros_like(q)) for a in range(2)]
    ones = jnp.ones((t, LANES), BF16)
    for a in range(2):
        m_scr[a] = jnp.full((t, LANES), NEG_INF, F32)
        l_scr[a] = jnp.zeros((t, LANES), F32)
        acc_scr[a] = jnp.zeros((t, DIFF_VDIM), F32)

    def step(j, masked):
        off = pl.multiple_of(j * t, t)
        ks = k_ref[pl.ds(off, t), :]
        vs = v_ref[pl.ds(off, t), :]
        bias = b_ref[jnp.minimum(qi - j, n_tiles - 1)]
        if masked:
            row = lax.broadcasted_iota(I32, (t, t), 0)
            col = lax.broadcasted_iota(I32, (t, t), 1)
            causal = row >= col
        for a in range(2):
            s = lax.dot_general(qa[a], ks, (((1,), (1,)), ((), ())),
                                preferred_element_type=F32) + bias
            if masked:
                s = jnp.where(causal, s, NEG_INF)
            m_prev = m_scr[a]
            m_new = jnp.maximum(m_prev, jnp.max(s, axis=-1, keepdims=True))
            alpha = jnp.exp(m_prev - m_new)
            pb = jnp.exp(s - jnp.concatenate([m_new] * (t // LANES), axis=1)).astype(BF16)
            l_scr[a] = alpha * l_scr[a] + jnp.dot(pb, ones, preferred_element_type=F32)
            acc_scr[a] = alpha * acc_scr[a] + jnp.dot(pb, vs, preferred_element_type=F32)
            m_scr[a] = m_new

    def loop_body(j, carry):
        step(j, False)
        return carry

    lax.fori_loop(0, qi, loop_body, 0)
    step(qi, True)

    lp = lam_ref[...]
    lam = (jnp.exp(jnp.sum(lp[0:1] * lp[1:2])) - jnp.exp(jnp.sum(lp[2:3] * lp[3:4]))
           + lam_init)
    o = acc_scr[0] / l_scr[0] - lam * (acc_scr[1] / l_scr[1])
    ms = jnp.mean(o * o, axis=-1, keepdims=True)
    y = o * lax.rsqrt(ms + RMS_EPS) * g_ref[...]
    o_ref[...] = (y * (1.0 - lam_init)).astype(o_ref.dtype)


def diff_attention(p, bias_tiles, lam_params, ln_g, lam_init, batch, seq):
    t = ATT_BLOCK
    n_tiles = bias_tiles.shape[1]
    cq = 3 * A_WIDTH // LANES
    ck = cq + B_QK_WIDTH // LANES
    cv = ck + B_QK_WIDTH // LANES
    return pl.pallas_call(
        functools.partial(_diff_body, n_tiles=n_tiles, lam_init=lam_init),
        grid=(batch, B_HEADS, seq // t),
        in_specs=[
            pl.BlockSpec((None, t, LANES), lambda b, h, i: (b, i, cq + h)),
            pl.BlockSpec((None, seq, LANES), lambda b, h, i: (b, 0, ck + h)),
            pl.BlockSpec((None, seq, LANES), lambda b, h, i: (b, 0, cv + h)),
            pl.BlockSpec((None, n_tiles, t, t), lambda b, h, i: (h, 0, 0, 0)),
            pl.BlockSpec((4, DIFF_HALF), lambda b, h, i: (0, 0)),
            pl.BlockSpec((1, DIFF_VDIM), lambda b, h, i: (0, 0)),
        ],
        out_specs=pl.BlockSpec((None, t, LANES), lambda b, h, i: (b, i, h)),
        out_shape=jax.ShapeDtypeStruct((batch, seq, B_V_WIDTH), BF16),
        scratch_shapes=[pltpu.VMEM((2, t, LANES), F32), pltpu.VMEM((2, t, LANES), F32),
                        pltpu.VMEM((2, t, DIFF_VDIM), F32)],
        compiler_params=pltpu.CompilerParams(
            dimension_semantics=("parallel", "parallel", "arbitrary")),
        name="diff_attention",
    )(p, p, p, bias_tiles, lam_params, ln_g.reshape(1, DIFF_VDIM))


def diff_bias_tiles(rel_bias_b, seq):
    t = ATT_BLOCK
    bucket = _t5_bucket_table(max(seq, 2 * MAX_DISTANCE) + 2 * t)
    sat = bucket[-1]
    d_sat = int(np.max(np.nonzero(bucket != sat)[0])) + 1
    n_full = (d_sat + t - 1 + t - 1) // t
    n_tiles = n_full + 1
    assert n_full * t - (t - 1) >= d_sat
    s = LANES
    nb = t // s
    deltas = np.arange(-(nb - 1), nb * n_tiles)
    n = np.arange(2 * s)[None, :]
    base = deltas[:, None] * s
    dist = np.clip(np.where(n < s, base - n, base + 2 * s - n), 0, None)
    w = rel_bias_b.T[:, bucket[dist]].astype(F32)
    rep = jnp.broadcast_to(w[:, :, None, :], (B_HEADS, len(deltas), s, 2 * s))
    flat = rep.reshape(B_HEADS, len(deltas), 2 * s * s)[:, :, :s * (2 * s - 1)]
    sub = flat.reshape(B_HEADS, len(deltas), s, 2 * s - 1)[:, :, :, :s]
    tiles = [jnp.concatenate(
        [jnp.concatenate([sub[:, d * nb + bi - bj + nb - 1] for bj in range(nb)], axis=-1)
         for bi in range(nb)], axis=-2) for d in range(n_tiles)]
    return jnp.stack(tiles, axis=1)


def _logsig_cumsum_body(f_ref, b_ref, c_ref, carry_scr):
    t = f_ref.shape[0]

    @pl.when(pl.program_id(1) == 0)
    def _():
        carry_scr[...] = jnp.zeros_like(carry_scr)

    x = f_ref[...] + b_ref[...]
    ls = jnp.minimum(x, 0.0) - jnp.log1p(jnp.exp(-jnp.abs(x)))
    row = lax.broadcasted_iota(I32, (t, t), 0)
    col = lax.broadcasted_iota(I32, (t, t), 1)
    tri = (row >= col).astype(F32)
    c = jnp.dot(tri, ls, preferred_element_type=F32,
                precision=lax.Precision.HIGHEST) + carry_scr[...]
    c_ref[...] = c
    carry_scr[...] = c[t - 1:t, :]


def logsig_cumsum(fg, b_f, batch, seq, t=512):
    return pl.pallas_call(
        _logsig_cumsum_body,
        grid=(batch, seq // t),
        in_specs=[pl.BlockSpec((None, t, LANES), lambda b, i: (b, i, 0)),
                  pl.BlockSpec((1, LANES), lambda b, i: (0, 0))],
        out_specs=pl.BlockSpec((None, t, LANES), lambda b, i: (b, i, 0)),
        out_shape=jax.ShapeDtypeStruct((batch, seq, LANES), F32),
        scratch_shapes=[pltpu.VMEM((1, LANES), F32)],
        compiler_params=pltpu.CompilerParams(
            dimension_semantics=("parallel", "arbitrary")),
        name="logsig_cumsum",
    )(fg, b_f)


def _fox_body(q_ref, k_ref, v_ref, cq_ref, ck_ref, o_ref, m_scr, acc_scr):
    t = ATT_BLOCK
    qi = pl.program_id(2)
    q = q_ref[...] * QK_SCALE
    lane = lax.broadcasted_iota(I32, (1, LANES), 1)
    own = [(lane < HEAD_DIM) == (a == 0) for a in range(2)]
    qa = [jnp.where(own[a], q, jnp.zeros_like(q)) for a in range(2)]
    cqb = [jnp.broadcast_to(cq_ref[a], (t, LANES)) for a in range(2)]
    for a in range(2):
        m_scr[a] = jnp.full((t, LANES), NEG_INF, F32)
        acc_scr[a] = jnp.zeros((t, LANES), F32)

    def step(j, masked):
        off = pl.multiple_of(j * t, t)
        ks = k_ref[pl.ds(off, t), :]
        vs = v_ref[pl.ds(off, t), :]
        if masked:
            row = lax.broadcasted_iota(I32, (t, t), 0)
            col = lax.broadcasted_iota(I32, (t, t), 1)
            causal = row >= col
        for a in range(2):
            s = lax.dot_general(qa[a], ks, (((1,), (1,)), ((), ())),
                                preferred_element_type=F32)
            s = s - ck_ref[a, :, pl.ds(off, t)]
            if masked:
                s = jnp.where(causal, s, NEG_INF)
            m_prev = m_scr[a]
            m_new = jnp.maximum(m_prev, jnp.max(s, axis=-1, keepdims=True) + cqb[a])
            alpha = jnp.exp(m_prev - m_new)
            shift = m_new - cqb[a]
            pb = jnp.exp(s - jnp.concatenate([shift] * (t // LANES), axis=1)).astype(BF16)
            v_aug = jnp.where(own[a], vs, jnp.ones_like(vs))
            acc_scr[a] = alpha * acc_scr[a] + jnp.dot(pb, v_aug, preferred_element_type=F32)
            m_scr[a] = m_new

    def loop_body(j, carry):
        step(j, False)
        return carry

    lax.fori_loop(0, qi, loop_body, 0)
    step(qi, True)
    r = [acc_scr[a] / pltpu.roll(acc_scr[a], HEAD_DIM, 1) for a in range(2)]
    o_ref[...] = jnp.where(own[0], r[0], r[1]).astype(o_ref.dtype)


def fox_attention(p, cq, ck, batch, seq):
    t = ATT_BLOCK
    nk = C_WIDTH // LANES
    return pl.pallas_call(
        _fox_body,
        grid=(batch, C_HEADS // 2, seq // t),
        in_specs=[
            pl.BlockSpec((None, t, LANES), lambda b, h, i: (b, i, h)),
            pl.BlockSpec((None, seq, LANES), lambda b, h, i: (b, 0, nk + h)),
            pl.BlockSpec((None, seq, LANES), lambda b, h, i: (b, 0, 2 * nk + h)),
            pl.BlockSpec((None, 2, t, 1), lambda b, h, i: (b, h, i, 0)),
            pl.BlockSpec((None, 2, 1, seq), lambda b, h, i: (b, h, 0, 0)),
        ],
        out_specs=pl.BlockSpec((None, t, LANES), lambda b, h, i: (b, i, h)),
        out_shape=jax.ShapeDtypeStruct((batch, seq, C_WIDTH), BF16),
        scratch_shapes=[pltpu.VMEM((2, t, LANES), F32), pltpu.VMEM((2, t, LANES), F32)],
        compiler_params=pltpu.CompilerParams(
            dimension_semantics=("parallel", "parallel", "arbitrary")),
        name="fox_attention",
    )(p, p, p, cq, ck)


def _concat_out_body(oa_ref, ob_ref, w_ref, x_ref, out_ref):
    out_ref[...] = (x_ref[...]
                    + jnp.dot(oa_ref[...], w_ref[:A_WIDTH, :], preferred_element_type=F32)
                    + jnp.dot(ob_ref[...], w_ref[A_WIDTH:, :], preferred_element_type=F32))


def concat_out_proj(oa, ob, w, x, tm=1024, tn=1024):
    m, d = x.shape
    return pl.pallas_call(
        _concat_out_body,
        grid=(m // tm, d // tn),
        in_specs=[pl.BlockSpec((tm, A_WIDTH), lambda i, j: (i, 0)),
                  pl.BlockSpec((tm, B_V_WIDTH), lambda i, j: (i, 0)),
                  pl.BlockSpec((A_WIDTH + B_V_WIDTH, tn), lambda i, j: (0, j)),
                  pl.BlockSpec((tm, tn), lambda i, j: (i, j))],
        out_specs=pl.BlockSpec((tm, tn), lambda i, j: (i, j)),
        out_shape=jax.ShapeDtypeStruct((m, d), F32),
        compiler_params=pltpu.CompilerParams(dimension_semantics=("parallel", "parallel")),
        name="concat_out_proj",
    )(oa, ob, w, x)


def _matmul_res_body(a_ref, w_ref, x_ref, o_ref):
    o_ref[...] = x_ref[...] + jnp.dot(a_ref[...], w_ref[...], preferred_element_type=F32)


def matmul_residual(a, w, x, tm=1024, tn=1024):
    m, k = a.shape
    n = w.shape[1]
    return pl.pallas_call(
        _matmul_res_body,
        grid=(m // tm, n // tn),
        in_specs=[pl.BlockSpec((tm, k), lambda i, j: (i, 0)),
                  pl.BlockSpec((k, tn), lambda i, j: (0, j)),
                  pl.BlockSpec((tm, tn), lambda i, j: (i, j))],
        out_specs=pl.BlockSpec((tm, tn), lambda i, j: (i, j)),
        out_shape=jax.ShapeDtypeStruct((m, n), F32),
        compiler_params=pltpu.CompilerParams(
            dimension_semantics=("parallel", "parallel")),
        name="matmul_residual",
    )(a, w, x)


def _add_norm_body(x_ref, r_ref, g_ref, o_ref):
    x = x_ref[...] + r_ref[...]
    ms = jnp.mean(x * x, axis=-1, keepdims=True)
    o_ref[...] = x * lax.rsqrt(ms + RMS_EPS) * g_ref[...]


def add_norm(x, r, g, tm=512):
    m, d = x.shape
    row = pl.BlockSpec((tm, d), lambda i: (i, 0))
    return pl.pallas_call(
        _add_norm_body,
        grid=(m // tm,),
        in_specs=[row, row, pl.BlockSpec((1, d), lambda i: (0, 0))],
        out_specs=row,
        out_shape=jax.ShapeDtypeStruct((m, d), F32),
        compiler_params=pltpu.CompilerParams(dimension_semantics=("parallel",)),
        name="add_norm",
    )(x, r, g.reshape(1, d))


def _peer_candidates():
    groups = [("a", 0, 0), ("a", 0, 8), ("a", 1, 0), ("b", 0, 8), ("a", 2, 0), ("a", 3, 0),
              ("b", 0, 0), ("b", 1, 0), ("b", 2, 0)]
    cid = np.zeros((8 * len(groups), 1), np.int32)
    seen = set()
    for g, (kind, fixed, start) in enumerate(groups):
        for r in range(8):
            a, b = (fixed, start + r) if kind == "a" else (start + r, fixed)
            row = 8 * g + r
            if (a + 1) * (b + 1) <= PEER_TOPK and (a, b) not in seen:
                seen.add((a, b))
                cid[row, 0] = a * PEER_TOPK + b
            else:
                cid[row, 0] = PEER_TOPK * PEER_TOPK + row
    assert len(seen) == sum((a + 1) * (b + 1) <= PEER_TOPK
                            for a in range(PEER_TOPK) for b in range(PEER_TOPK))
    return groups, cid


def _peer_topk_body(q_ref, sk_ref, cid_ref, idx_ref, gate_ref, ts_scr, ti_scr,
                    bs_scr, be_scr, cv_scr, ce_scr, sc_scr, *, groups):
    tm = q_ref.shape[0]
    neg_inf = jnp.float32(-jnp.inf)
    key_id = lax.broadcasted_iota(I32, (1, N_KEYS, tm), 1)
    batch = sc_scr.shape[0]

    def group_body(g, carry):
        for j in range(batch):
            pr = g * batch + j
            off = pl.multiple_of(pr * PEER_KEY_HALF, PEER_KEY_HALF)
            sc_scr[j] = lax.dot_general(sk_ref[pr], q_ref[:, pl.ds(off, PEER_KEY_HALF)],
                                        (((1,), (1,)), ((), ())), preferred_element_type=F32)
        rows = pl.ds(g * batch, batch)

        def k_body(k, c):
            vals = sc_scr[...]
            m = jnp.max(vals, axis=1, keepdims=True)
            sel = jnp.min(jnp.where(vals == m, key_id, N_KEYS), axis=1, keepdims=True)
            ts_scr[rows, pl.ds(k, 1), :] = m
            ti_scr[rows, pl.ds(k, 1), :] = sel
            sc_scr[...] = jnp.where(key_id == sel, neg_inf, vals)
            return c

        lax.fori_loop(0, PEER_TOPK, k_body, 0)
        return carry

    lax.fori_loop(0, 2 * PEER_HEADS // batch, group_body, 0)

    cand_id = cid_ref[...]
    pad = jnp.where(cand_id < PEER_TOPK * PEER_TOPK, 0.0, neg_inf)

    def pick(x1, x2):
        return jnp.concatenate(
            [x1[f:f + 1] + x2[s:s + 8] if kind == "a" else x1[s:s + 8] + x2[f:f + 1]
             for kind, f, s in groups], axis=0)

    def head_body(h, carry):
        cv_scr[h] = pick(ts_scr[2 * h], ts_scr[2 * h + 1]) + pad
        ce_scr[h] = pick(ti_scr[2 * h] * N_KEYS, ti_scr[2 * h + 1])
        return carry

    lax.fori_loop(0, PEER_HEADS, head_body, 0)

    cid3 = cand_id[None]

    def k_body(k, carry):
        vals = cv_scr[...]
        m = jnp.max(vals, axis=1, keepdims=True)
        sel = jnp.min(jnp.where(vals == m, cid3, PEER_TOPK * PEER_TOPK), axis=1, keepdims=True)
        hit = cid3 == sel
        bs_scr[:, pl.ds(k, 1), :] = m
        be_scr[:, pl.ds(k, 1), :] = jnp.sum(jnp.where(hit, ce_scr[...], 0), axis=1, keepdims=True)
        cv_scr[...] = jnp.where(hit, neg_inf, vals)
        return carry

    lax.fori_loop(0, PEER_TOPK, k_body, 0)
    bs = bs_scr[...]
    e = jnp.exp(bs - jnp.max(bs, axis=1, keepdims=True))
    gate = e / jnp.sum(e, axis=1, keepdims=True)
    n_sel = PEER_HEADS * PEER_TOPK
    gate_ref[...] = gate.reshape(n_sel, tm).T
    idx_ref[...] = be_scr[...].reshape(n_sel, tm).T


def peer_topk(q, subkeys, tm=256):
    m = q.shape[0]
    n_sel = PEER_HEADS * PEER_TOPK
    groups, cid = _peer_candidates()
    out_spec = pl.BlockSpec((tm, n_sel), lambda i: (i, 0))
    return pl.pallas_call(
        functools.partial(_peer_topk_body, groups=groups),
        grid=(m // tm,),
        in_specs=[pl.BlockSpec((tm, q.shape[1]), lambda i: (i, 0)),
                  pl.BlockSpec(subkeys.shape, lambda i: (0, 0, 0)),
                  pl.BlockSpec(cid.shape, lambda i: (0, 0))],
        out_specs=[out_spec, out_spec],
        out_shape=[jax.ShapeDtypeStruct((m, n_sel), I32),
                   jax.ShapeDtypeStruct((m, n_sel), F32)],
        scratch_shapes=[pltpu.VMEM((2 * PEER_HEADS, PEER_TOPK, tm), F32),
                        pltpu.VMEM((2 * PEER_HEADS, PEER_TOPK, tm), I32),
                        pltpu.VMEM((PEER_HEADS, PEER_TOPK, tm), F32),
                        pltpu.VMEM((PEER_HEADS, PEER_TOPK, tm), I32),
                        pltpu.VMEM((PEER_HEADS, cid.shape[0], tm), F32),
                        pltpu.VMEM((PEER_HEADS, cid.shape[0], tm), I32),
                        pltpu.VMEM((PEER_HEADS, N_KEYS, tm), F32)],
        compiler_params=pltpu.CompilerParams(dimension_semantics=("parallel",)),
        name="peer_topk",
    )(q, subkeys, jnp.asarray(cid))


def _gelu_gate_body(p_ref, g_ref, o_ref):
    n = g_ref.shape[1]
    seg = (lax.broadcasted_iota(I32, (n * SC_LANES, n), 0) // SC_LANES
           == lax.broadcasted_iota(I32, (n * SC_LANES, n), 1)).astype(F32)
    a = jnp.dot(p_ref[...], seg, preferred_element_type=F32, precision=lax.Precision.HIGHEST)
    o_ref[...] = g_ref[...] * (0.5 * a * (1.0 + lax.erf(a * (2.0 ** -0.5))))


def gelu_gate(partial, gate, tm=512):
    m, n = gate.shape
    spec = pl.BlockSpec((tm, n), lambda i: (i, 0))
    return pl.pallas_call(
        _gelu_gate_body,
        grid=(m // tm,),
        in_specs=[pl.BlockSpec((tm, n * SC_LANES), lambda i: (i, 0)), spec],
        out_specs=spec,
        out_shape=jax.ShapeDtypeStruct((m, n), F32),
        compiler_params=pltpu.CompilerParams(dimension_semantics=("parallel",)),
        name="gelu_gate",
    )(partial, gate)


SC_TOK_CHUNK = 32
SC_DOTS_CHUNK = 16
SC_RING = 8
SC_BF16_GROUP = 4
SC_FMT = plsc.PackFormat.INTERLEAVED


def _sc_worker_id():
    return lax.axis_index("s") * SC_CORES + lax.axis_index("c")


def pack_bf16_pairs(t):
    half = t.shape[-1] // 2
    bits = lax.bitcast_convert_type(t.astype(BF16).astype(F32), I32)
    return (bits[..., half:] & jnp.int32(-65536)) | lax.shift_right_logical(
        bits[..., :half], jnp.int32(16))


def _sc_row_pipeline(idx_v, table_hbm, rows_v, sems, n_items, groups, compute):
    def gather(item):
        tt, g = item // groups, item % groups
        ids = idx_v[tt, pl.ds(g * SC_LANES, SC_LANES)]
        slot = item % SC_RING
        return pltpu.make_async_copy(table_hbm.at[ids], rows_v.at[slot], sems.at[slot])

    for s in range(SC_RING - 1):
        gather(s).start()

    def item_body(item, carry):
        nxt = item + SC_RING - 1

        @pl.when(nxt < n_items)
        def _():
            gather(nxt).start()

        gather(item).wait()
        compute(item // groups, item % groups, item % SC_RING)
        return carry

    lax.fori_loop(0, n_items, item_body, 0)


def peer_expert_dots(hp, idx, up):
    m, dw = hp.shape
    n_sel = idx.shape[1]
    per_w = m // SC_WORKERS
    chunk = SC_DOTS_CHUNK
    n_chunks = per_w // chunk
    groups = n_sel // SC_LANES
    step = SC_BF16_GROUP * SC_LANES
    mesh = plsc.VectorSubcoreMesh(core_axis_name="c", subcore_axis_name="s")

    @functools.partial(
        pl.kernel, mesh=mesh,
        out_type=jax.ShapeDtypeStruct((m, n_sel * SC_LANES), F32),
        scratch_types=[
            pltpu.VMEM((chunk, n_sel), I32),
            pltpu.VMEM((chunk, dw), I32),
            pltpu.VMEM((chunk, n_sel * SC_LANES), F32),
            pltpu.VMEM((SC_RING, SC_LANES, dw), I32),
            pltpu.SemaphoreType.DMA((SC_RING,)),
        ],
        compiler_params=pltpu.CompilerParams(needs_layout_passes=False),
        name="peer_expert_dots",
    )
    def k(h_hbm, idx_hbm, u_hbm, act_hbm, idx_v, h_v, act_v, rows_v, sems):
        base = _sc_worker_id() * per_w

        def compute(tt, g, slot):
            def grp_body(q, accs):
                off = pl.multiple_of(q * step, step)
                xs = [plsc.bitcast(h_v[tt, pl.ds(off + c * SC_LANES, SC_LANES)], BF16)
                      for c in range(SC_BF16_GROUP)]
                new = []
                for e in range(SC_LANES):
                    ps = [plsc.bitcast(
                        rows_v[slot, e, pl.ds(off + c * SC_LANES, SC_LANES)], BF16) * xs[c]
                        for c in range(SC_BF16_GROUP)]
                    while len(ps) > 1:
                        ps = [ps[i] + ps[i + 1] for i in range(0, len(ps), 2)]
                    lo, hi = plsc.unpack(ps[0], format=SC_FMT)
                    new.append(accs[e] + (lo + hi))
                return tuple(new)

            accs = lax.fori_loop(
                0, dw // step, grp_body,
                tuple(jnp.zeros((SC_LANES,), F32) for _ in range(SC_LANES)))
            for e in range(SC_LANES):
                act_v[tt, pl.ds((g * SC_LANES + e) * SC_LANES, SC_LANES)] = accs[e]

        def chunk_body(c, carry):
            t0 = base + c * chunk
            pltpu.sync_copy(idx_hbm.at[pl.ds(t0, chunk)], idx_v)
            pltpu.sync_copy(h_hbm.at[pl.ds(t0, chunk)], h_v)
            _sc_row_pipeline(idx_v, u_hbm, rows_v, sems, chunk * groups, groups, compute)
            pltpu.sync_copy(act_v, act_hbm.at[pl.ds(t0, chunk)])
            return carry

        lax.fori_loop(0, n_chunks, chunk_body, 0)

    return k(hp, idx, up)


def peer_expert_combine(w, idx, vp):
    m, n_sel = w.shape
    dw = vp.shape[1]
    d = 2 * dw
    per_w = m // SC_WORKERS
    n_chunks = per_w // SC_TOK_CHUNK
    n_vec = d // SC_LANES
    groups = n_sel // SC_LANES
    mesh = plsc.VectorSubcoreMesh(core_axis_name="c", subcore_axis_name="s")

    @functools.partial(
        pl.kernel, mesh=mesh,
        out_type=jax.ShapeDtypeStruct((m, d), F32),
        scratch_types=[
            pltpu.VMEM((SC_TOK_CHUNK, n_sel), I32),
            pltpu.VMEM((SC_TOK_CHUNK, n_sel), F32),
            pltpu.VMEM((SC_TOK_CHUNK, d), F32),
            pltpu.VMEM((SC_RING, SC_LANES, dw), I32),
            pltpu.SemaphoreType.DMA((SC_RING,)),
        ],
        compiler_params=pltpu.CompilerParams(needs_layout_passes=False),
        name="peer_expert_combine",
    )
    def k(w_hbm, idx_hbm, v_hbm, out_hbm, idx_v, w_v, out_v, rows_v, sems):
        base = _sc_worker_id() * per_w

        def compute(tt, g, slot):
            splat = []
            for e in range(SC_LANES):
                s = plsc.load_gather(w_v, [jnp.full((SC_LANES,), tt, I32),
                                           jnp.full((SC_LANES,), g * SC_LANES + e, I32)])
                splat.append(plsc.pack(s, s, format=SC_FMT))

            @plsc.parallel_loop(0, dw // SC_LANES)
            def _(j):
                off = pl.multiple_of(j * SC_LANES, SC_LANES)
                acc_lo = out_v[tt, pl.ds(off, SC_LANES)]
                acc_hi = out_v[tt, pl.ds(dw + off, SC_LANES)]
                for e0 in range(0, SC_LANES, SC_BF16_GROUP):
                    s = None
                    for e in range(e0, e0 + SC_BF16_GROUP):
                        p = plsc.bitcast(rows_v[slot, e, pl.ds(off, SC_LANES)], BF16) * splat[e]
                        s = p if s is None else s + p
                    lo, hi = plsc.unpack(s, format=SC_FMT)
                    acc_lo = acc_lo + lo
                    acc_hi = acc_hi + hi
                out_v[tt, pl.ds(off, SC_LANES)] = acc_lo
                out_v[tt, pl.ds(dw + off, SC_LANES)] = acc_hi

        def chunk_body(c, carry):
            t0 = base + c * SC_TOK_CHUNK
            pltpu.sync_copy(idx_hbm.at[pl.ds(t0, SC_TOK_CHUNK)], idx_v)
            pltpu.sync_copy(w_hbm.at[pl.ds(t0, SC_TOK_CHUNK)], w_v)

            def zero_body(z, carry2):
                tt, j = z // n_vec, z % n_vec
                out_v[tt, pl.ds(pl.multiple_of(j * SC_LANES, SC_LANES), SC_LANES)] = (
                    jnp.zeros((SC_LANES,), F32))
                return carry2

            lax.fori_loop(0, SC_TOK_CHUNK * n_vec, zero_body, 0)
            _sc_row_pipeline(idx_v, v_hbm, rows_v, sems, SC_TOK_CHUNK * groups, groups, compute)
            pltpu.sync_copy(out_v, out_hbm.at[pl.ds(t0, SC_TOK_CHUNK)])
            return carry

        lax.fori_loop(0, n_chunks, chunk_body, 0)

    return k(w, idx, vp)


def kernel(x, norm_mix_g, norm_ffn_g, final_norm_g, rel_bias, even_w_in, even_w_out,
           diff_lambda, diff_ln_g, odd_w_in, odd_b_f, odd_w_out, peer_wq, peer_subkeys,
           peer_u, peer_v):
    batch, seq, d = x.shape

    dil_tiles = jnp.stack(
        [dilated_bias_tile(rel_bias[:, :A_HEADS], w, dl) for w, dl in DILATED_BRANCHES])
    diff_tiles = diff_bias_tiles(rel_bias[:, A_HEADS:], seq)
    lam_init = 0.8 - 0.6 * math.exp(-0.3 * 0)
    even_in, even_out = even_w_in[0].astype(BF16), even_w_out[0].astype(BF16)
    w_in = odd_w_in[0]
    odd_in, odd_out = w_in[:, :3 * C_WIDTH].astype(BF16), odd_w_out[0].astype(BF16)
    w_gate = jnp.pad(w_in[:, 3 * C_WIDTH:], ((0, 0), (0, LANES - C_HEADS)))
    b_f = jnp.pad(odd_b_f[0], (0, LANES - C_HEADS)).reshape(1, LANES)
    peer = [dict(wq=peer_wq[l].astype(BF16),
                 sk=peer_subkeys[l].reshape(2 * PEER_HEADS, N_KEYS, PEER_KEY_HALF).astype(BF16))
            for l in range(2)]
    packed_tables = {}

    def expert_table(table, layer, anchor):
        if (id(table), layer) not in packed_tables:
            rows, _ = lax.optimization_barrier((table[layer], anchor))
            packed_tables[(id(table), layer)] = pack_bf16_pairs(rows)
        return packed_tables[(id(table), layer)]

    assert batch % BATCH_GROUPS == 0
    bg = batch // BATCH_GROUPS
    m = bg * seq
    groups = [dict(g=g) for g in range(BATCH_GROUPS)]

    def peer_select(st, layer, after=None, parts=1):
        rows = m // parts
        pieces = []
        for part in range(parts):
            xs = st["x"] if parts == 1 else st["x"][part * rows:(part + 1) * rows]
            q, hp = norm_matmul(xs, norm_ffn_g[layer], peer[layer]["wq"], want_h="packed")
            idx, gate = peer_topk(q, peer[layer]["sk"])
            if after is not None and after["st"] is not st:
                after["st"]["peer"], idx = lax.optimization_barrier((after["st"]["peer"], idx))
            pieces.append((idx, gate,
                           peer_expert_dots(hp, idx, expert_table(peer_u, layer, st["x"]))))
        idx, gate, act = (jnp.concatenate(c, axis=0) if parts > 1 else c[0] for c in zip(*pieces))
        return dict(st=st, layer=layer, idx=idx, gate=gate, act=act)

    def peer_combine(rec):
        rec["w"] = gelu_gate(rec["act"], rec["gate"])
        rec["st"]["peer"] = peer_expert_combine(
            rec["w"], rec["idx"], expert_table(peer_v, rec["layer"], rec["act"]))

    def even_layer(st, tie):
        (p,) = norm_matmul(st["x"], norm_mix_g[0], even_in)
        p3 = p.reshape(bg, seq, -1)
        oa = dilated_attention(p3, dil_tiles, bg, seq)
        ob = tie(diff_attention(p3, diff_tiles, diff_lambda[0], diff_ln_g[0], lam_init, bg, seq))
        st["x"] = concat_out_proj(oa.reshape(m, A_WIDTH), ob.reshape(m, B_V_WIDTH), even_out,
                                  st["x"])

    def odd_layer(st, tie):
        p, st["x"], fg = norm_matmul(st["x"], norm_mix_g[1], odd_in, res=st["peer"],
                                     want_h="gate", gate_w=w_gate)
        c = logsig_cumsum(fg.reshape(bg, seq, LANES), b_f, bg, seq)
        ct = c[:, :, :C_HEADS].transpose(0, 2, 1)
        o = tie(fox_attention(p.reshape(bg, seq, -1), ct[:, :, :, None], ct[:, :, None, :],
                              bg, seq))
        st["x"] = matmul_residual(o.reshape(m, C_WIDTH), odd_out, st["x"])

    stages = [(even_layer, st, 0) for st in groups] + [(odd_layer, st, 1) for st in groups]
    recs = []
    for k, (mixer, st, layer) in enumerate(stages):
        def tie(t, prev=recs[k - 1] if k >= 1 else None):
            if prev is None:
                return t
            prev["act"], t = lax.optimization_barrier((prev["act"], t))
            peer_combine(prev)
            prev["w"], t = lax.optimization_barrier((prev["w"], t))
            return t

        if "x" not in st:
            xin = x if k == 0 else lax.optimization_barrier((x, recs[k - 1]["idx"]))[0]
            st["x"] = xin[st["g"] * bg:(st["g"] + 1) * bg].reshape(m, d)
        mixer(st, tie)
        recs.append(peer_select(st, layer, after=recs[max(k - 2, 0)] if k >= 1 else None,
                                parts=2 if k == 0 else 1))
    peer_combine(recs[-1])
    outs = [add_norm(st["x"], st["peer"], final_norm_g).reshape(bg, seq, d) for st in groups]
    return jnp.concatenate(outs, axis=0)
```
